```python
import jax, jax.numpy as jnp
from jax import lax
import numpy as np

D_MODEL = 1024
BATCH = 1
SEQ = 16384
DEPTH = 1
DEC_BATCH = 16
DEC_SEQ = 16
PAST_LEN = 2048

CHUNK = 64
Q_BLOCK = 128
FOX_HEADS = 8
FOX_HEAD_DIM = 64
RET_HEADS = 4
RET_KEY_DIM = 128
RET_VAL_DIM = 256
FFN_DIM = 2816
CONV_WIDTH = 3
EPS = 1e-6
ROPE_BASE = 10000.0

FOX_W = FOX_HEADS * FOX_HEAD_DIM
RET_KW = RET_HEADS * RET_KEY_DIM
RET_VW = RET_HEADS * RET_VAL_DIM
IN_COLS = 3 * FOX_W + FOX_HEADS + 2 * RET_KW + 2 * RET_VW + 2 * D_MODEL

kernel_name = "hybrid_fox_retention_convffn_stream_step"


def rmsnorm(x, g):
    xf = x.astype(jnp.float32)
    y = xf * lax.rsqrt(jnp.mean(xf * xf, axis=-1, keepdims=True) + EPS)
    return (y * g.astype(jnp.float32)).astype(x.dtype)


def rotary(x, pos):
    half = x.shape[-1] // 2
    inv = 1.0 / (ROPE_BASE ** jnp.linspace(0.0, 1.0, half, dtype=jnp.float32))
    ang = pos.astype(jnp.float32)[:, None] * inv[None, :]
    cos = jnp.cos(ang)[:, None, :]
    sin = jnp.sin(ang)[:, None, :]
    xf = x.astype(jnp.float32)
    x1, x2 = xf[..., :half], xf[..., half:]
    return jnp.concatenate([x1 * cos - x2 * sin, x1 * sin + x2 * cos], axis=-1).astype(x.dtype)


def ret_log_gamma():
    return jnp.log(1.0 - jnp.exp2(-5.0 - jnp.arange(RET_HEADS, dtype=jnp.float32)))


def mixer_inputs(h, w_in, b_fox_f, pos):
    B, L, _ = h.shape
    z = h @ w_in
    sizes = (FOX_W, FOX_W, FOX_W, FOX_HEADS, RET_KW, RET_KW, RET_VW, RET_VW, D_MODEL, D_MODEL)
    points = np.cumsum(sizes)[:-1].tolist()
    qa, ka, va, fa, qb, kb, vb, gb, gma, gmb = jnp.split(z, points, axis=-1)
    qa = qa.reshape(B, L, FOX_HEADS, FOX_HEAD_DIM)
    ka = ka.reshape(B, L, FOX_HEADS, FOX_HEAD_DIM)
    va = va.reshape(B, L, FOX_HEADS, FOX_HEAD_DIM)
    logf = jax.nn.log_sigmoid(fa.astype(jnp.float32) + b_fox_f.astype(jnp.float32))
    qb = rotary(qb.reshape(B, L, RET_HEADS, RET_KEY_DIM), pos)
    kb = rotary(kb.reshape(B, L, RET_HEADS, RET_KEY_DIM), pos) * (RET_KEY_DIM ** -0.5)
    vb = vb.reshape(B, L, RET_HEADS, RET_VAL_DIM)
    return qa, ka, va, logf, qb, kb, vb, gb, gma, gmb


def fox_attend(q, c_q, q_pos, k, v, ck_t):
    s = jnp.einsum('bqhd,bkhd->bhqk', q, k).astype(jnp.float32) * (FOX_HEAD_DIM ** -0.5)
    s = s + jnp.transpose(c_q, (0, 2, 1))[..., None] - ck_t[:, :, None, :]
    mask = jnp.arange(k.shape[1])[None, :] <= q_pos[:, None]
    s = jnp.where(mask, s, -jnp.inf)
    p = jax.nn.softmax(s, axis=-1)
    return jnp.einsum('bhqk,bkhd->bqhd', p.astype(v.dtype), v)


def fox_prompt(q, k, v, logf):
    B, S = q.shape[:2]
    nb = S // Q_BLOCK
    c = jnp.cumsum(logf, axis=1)
    ck_t = jnp.transpose(c, (0, 2, 1))
    qb = q.reshape(B, nb, Q_BLOCK, FOX_HEADS, FOX_HEAD_DIM).swapaxes(0, 1)
    cb = c.reshape(B, nb, Q_BLOCK, FOX_HEADS).swapaxes(0, 1)

    def block(args):
        qi, ci, bi = args
        return fox_attend(qi, ci, bi * Q_BLOCK + jnp.arange(Q_BLOCK), k, v, ck_t)

    o = lax.map(block, (qb, cb, jnp.arange(nb)))
    return o.swapaxes(0, 1).reshape(B, S, FOX_HEADS, FOX_HEAD_DIM)


def fox_sample(q, k, v, logf, cache_k, cache_v, cache_logf):
    P = cache_k.shape[1]
    T = q.shape[1]
    k_all = jnp.concatenate([cache_k.astype(k.dtype), k], axis=1)
    v_all = jnp.concatenate([cache_v.astype(v.dtype), v], axis=1)
    c = jnp.cumsum(jnp.concatenate([cache_logf.astype(jnp.float32), logf], axis=1), axis=1)
    return fox_attend(q, c[:, P:], P + jnp.arange(T), k_all, v_all, jnp.transpose(c, (0, 2, 1)))


def retention_chunk(q, k, v, state, lg):
    q = q.astype(jnp.float32); k = k.astype(jnp.float32); v = v.astype(jnp.float32)
    state = state.astype(jnp.float32)
    L = q.shape[1]
    idx = jnp.arange(L, dtype=jnp.float32)
    diff = idx[:, None] - idx[None, :]
    dmat = jnp.where(diff[None] >= 0, jnp.exp(jnp.maximum(diff, 0.0)[None] * lg[:, None, None]), 0.0)
    scores = jnp.einsum('bnhd,bmhd->bhnm', q, k) * dmat[None]
    inner = jnp.einsum('bhnm,bmhv->bnhv', scores, v)
    xi = jnp.exp((idx[:, None] + 1.0) * lg[None, :])
    cross = jnp.einsum('bnhd,bhdv->bnhv', q, state) * xi[None, :, :, None]
    zeta = jnp.exp((L - 1.0 - idx[:, None]) * lg[None, :])
    new_state = jnp.exp(L * lg)[None, :, None, None] * state + jnp.einsum('bmhd,bmhv->bhdv', k * zeta[None, :, :, None], v)
    return inner + cross, new_state


def retention_prompt(q, k, v):
    B, S = q.shape[:2]
    nc = S // CHUNK
    lg = ret_log_gamma()
    qc = q.reshape(B, nc, CHUNK, RET_HEADS, RET_KEY_DIM).swapaxes(0, 1)
    kc = k.reshape(B, nc, CHUNK, RET_HEADS, RET_KEY_DIM).swapaxes(0, 1)
    vc = v.reshape(B, nc, CHUNK, RET_HEADS, RET_VAL_DIM).swapaxes(0, 1)

    def step(state, xs):
        o, s = retention_chunk(xs[0], xs[1], xs[2], state, lg)
        return s, o

    init = jnp.zeros((B, RET_HEADS, RET_KEY_DIM, RET_VAL_DIM), jnp.float32)
    final, outs = lax.scan(step, init, (qc, kc, vc))
    return outs.swapaxes(0, 1).reshape(B, S, RET_HEADS, RET_VAL_DIM), final


def mixer_output(o_a, o_b, g_b, g_ma, g_mb, gn_g, w_pa, w_pb, w_o):
    B, L = o_a.shape[:2]
    ya = o_a.reshape(B, L, FOX_W) @ w_pa
    of = o_b.astype(jnp.float32)
    mu = jnp.mean(of, axis=-1, keepdims=True)
    var = jnp.mean(jnp.square(of - mu), axis=-1, keepdims=True)
    n = ((of - mu) * lax.rsqrt(var + EPS)).reshape(B, L, RET_VW) * gn_g.astype(jnp.float32)
    n = n * jax.nn.silu(g_b.astype(jnp.float32))
    yb = n.astype(ya.dtype) @ w_pb
    y = jax.nn.sigmoid(g_ma) * ya + jax.nn.sigmoid(g_mb) * yb
    return y @ w_o


def conv_ffn(h, prev, w_up, conv_w, conv_b, w_down):
    u = h @ w_up
    L = u.shape[1]
    ext = jnp.concatenate([prev.astype(u.dtype), u], axis=1)
    y = conv_b + sum(conv_w[j] * ext[:, j:j + L] for j in range(CONV_WIDTH))
    a, b = jnp.split(y, 2, axis=-1)
    return (jax.nn.gelu(a) * b) @ w_down, ext[:, L:]


def setup_inputs(seed: int = 0) -> dict:
    key = jax.random.key(seed)
    ks = jax.random.split(key, 24)
    f32 = jnp.float32
    nrm = lambda k, shape, s=1.0: jax.random.normal(k, shape, f32) * s
    return {
        "x_prompt": nrm(ks[0], (BATCH, SEQ, D_MODEL)),
        "x_sample": nrm(ks[1], (DEC_BATCH, DEC_SEQ, D_MODEL)),
        "cache_fox_k": nrm(ks[2], (DEPTH, DEC_BATCH, PAST_LEN, FOX_HEADS, FOX_HEAD_DIM)),
        "cache_fox_v": nrm(ks[3], (DEPTH, DEC_BATCH, PAST_LEN, FOX_HEADS, FOX_HEAD_DIM)),
        "cache_fox_logf": jax.nn.log_sigmoid(3.0 + nrm(ks[4], (DEPTH, DEC_BATCH, PAST_LEN, FOX_HEADS))),
        "state_ret": nrm(ks[5], (DEPTH, DEC_BATCH, RET_HEADS, RET_KEY_DIM, RET_VAL_DIM)),
        "state_ffn_conv": nrm(ks[6], (DEPTH, DEC_BATCH, CONV_WIDTH - 1, 2 * FFN_DIM)),
        "norm_mix_g": 1.0 + nrm(ks[7], (DEPTH, D_MODEL), 0.02),
        "w_in": nrm(ks[8], (DEPTH, D_MODEL, IN_COLS), D_MODEL ** -0.5),
        "b_fox_f": jnp.linspace(1.0, 5.0, FOX_HEADS, dtype=f32)[None, :] + nrm(ks[9], (DEPTH, FOX_HEADS), 0.1),
        "gn_ret_g": 1.0 + nrm(ks[10], (DEPTH, RET_VW), 0.02),
        "w_pa": nrm(ks[11], (DEPTH, FOX_W, D_MODEL), FOX_W ** -0.5),
        "w_pb": nrm(ks[12], (DEPTH, RET_VW, D_MODEL), RET_VW ** -0.5),
        "w_o": nrm(ks[13], (DEPTH, D_MODEL, D_MODEL), D_MODEL ** -0.5),
        "norm_ffn_g": 1.0 + nrm(ks[14], (DEPTH, D_MODEL), 0.02),
        "w_up": nrm(ks[15], (DEPTH, D_MODEL, 2 * FFN_DIM), D_MODEL ** -0.5),
        "conv_w": nrm(ks[16], (DEPTH, CONV_WIDTH, 2 * FFN_DIM), CONV_WIDTH ** -0.5),
        "conv_b": nrm(ks[17], (DEPTH, 2 * FFN_DIM), 0.02),
        "w_down": nrm(ks[18], (DEPTH, FFN_DIM, D_MODEL), FFN_DIM ** -0.5),
        "norm_final_g": 1.0 + nrm(ks[19], (D_MODEL,), 0.02),
    }


def reference(x_prompt, x_sample, cache_fox_k, cache_fox_v, cache_fox_logf, state_ret, state_ffn_conv,
              norm_mix_g, w_in, b_fox_f, gn_ret_g, w_pa, w_pb, w_o, norm_ffn_g, w_up, conv_w, conv_b,
              w_down, norm_final_g):
    pos_p = jnp.arange(x_prompt.shape[1])
    pos_s = cache_fox_k.shape[2] + jnp.arange(x_sample.shape[1])
    lg = ret_log_gamma()
    xp, xs = x_prompt, x_sample
    pk, pv, pf, pr, pc = [], [], [], [], []
    sk, sv, sf, sr, sc = [], [], [], [], []
    for l in range(DEPTH):
        h = rmsnorm(xp, norm_mix_g[l])
        qa, ka, va, logf, qb, kb, vb, gb, gma, gmb = mixer_inputs(h, w_in[l], b_fox_f[l], pos_p)
        oa = fox_prompt(qa, ka, va, logf)
        ob, s_ret = retention_prompt(qb, kb, vb)
        xp = xp + mixer_output(oa, ob, gb, gma, gmb, gn_ret_g[l], w_pa[l], w_pb[l], w_o[l])
        h = rmsnorm(xp, norm_ffn_g[l])
        zeros_prev = jnp.zeros((xp.shape[0], CONV_WIDTH - 1, 2 * FFN_DIM), h.dtype)
        f, s_conv = conv_ffn(h, zeros_prev, w_up[l], conv_w[l], conv_b[l], w_down[l])
        xp = xp + f
        pk.append(ka); pv.append(va); pf.append(logf); pr.append(s_ret); pc.append(s_conv)
        h = rmsnorm(xs, norm_mix_g[l])
        qa, ka, va, logf, qb, kb, vb, gb, gma, gmb = mixer_inputs(h, w_in[l], b_fox_f[l], pos_s)
        oa = fox_sample(qa, ka, va, logf, cache_fox_k[l], cache_fox_v[l], cache_fox_logf[l])
        ob, s_ret = retention_chunk(qb, kb, vb, state_ret[l], lg)
        xs = xs + mixer_output(oa, ob, gb, gma, gmb, gn_ret_g[l], w_pa[l], w_pb[l], w_o[l])
        h = rmsnorm(xs, norm_ffn_g[l])
        f, s_conv = conv_ffn(h, state_ffn_conv[l], w_up[l], conv_w[l], conv_b[l], w_down[l])
        xs = xs + f
        sk.append(ka); sv.append(va); sf.append(logf); sr.append(s_ret); sc.append(s_conv)
    y_prompt = rmsnorm(xp, norm_final_g)
    y_sample = rmsnorm(xs, norm_final_g)
    return (y_prompt, y_sample,
            jnp.stack(pk), jnp.stack(pv), jnp.stack(pf), jnp.stack(pr), jnp.stack(pc),
            jnp.stack(sk), jnp.stack(sv), jnp.stack(sf), jnp.stack(sr), jnp.stack(sc))
```

```python
import functools
import math

import numpy as np
import jax
import jax.numpy as jnp
from jax import lax
from jax.experimental import pallas as pl
from jax.experimental.pallas import tpu as pltpu

F32 = jnp.float32
BF16 = jnp.bfloat16

D_MODEL = 1024
FOX_HEADS = 8
FOX_HEAD_DIM = 64
RET_HEADS = 4
RET_KEY_DIM = 128
RET_VAL_DIM = 256
FFN_DIM = 2816
CONV_WIDTH = 3
EPS = 1e-6
ROPE_BASE = 10000.0

FOX_W = FOX_HEADS * FOX_HEAD_DIM
RET_KW = RET_HEADS * RET_KEY_DIM
RET_VW = RET_HEADS * RET_VAL_DIM

LOG2E = 1.4426950408889634
LANES = 128
HEAD_PAD = LANES
V_PAD = 80
BIAS_COL = FOX_HEAD_DIM
NEG = -1e30
FFN_CHUNK = 256
N_FFN_CHUNKS = FFN_DIM // FFN_CHUNK
VMEM_LIMIT = 56 * 1024 * 1024


def _rmsnorm(x, g):
    ms = jnp.mean(x * x, axis=-1, keepdims=True)
    return x * lax.rsqrt(ms + EPS) * g


def _split3(x):
    hi = x.astype(BF16)
    r1 = x - hi.astype(F32)
    mid = r1.astype(BF16)
    lo = (r1 - mid.astype(F32)).astype(BF16)
    return hi, mid, lo


def _log_sigmoid(x):
    return jnp.minimum(x, 0.0) - jnp.log1p(jnp.exp(-jnp.abs(x)))


def _dot(a, b):
    return jnp.dot(a, b, preferred_element_type=F32)


def _rotary(x, cos2, sin2):
    return x * cos2 + pltpu.roll(x, RET_KEY_DIM // 2, 1) * sin2


def _const_spec(shape):
    n = len(shape)
    return pl.BlockSpec(shape, lambda *_: (0,) * n)


def _inproj_prompt_kernel(x_ref, g_ref, wq_ref, wk_ref, wv_ref, wkv_ref, wf_ref, bf_ref, wb_ref,
                          cos_ref, sin_ref, tri_ref, eq_ref, ek_ref, oneq_ref, onek_ref, onev_ref,
                          qT_ref, ka_ref, vT_ref, k32_ref, v32_ref, logf_ref, qb_ref, kb_ref, vb_ref,
                          carry_ref):
    tm = x_ref.shape[0]

    @pl.when(pl.program_id(0) == 0)
    def _():
        carry_ref[...] = jnp.zeros_like(carry_ref)

    h = _rmsnorm(x_ref[...], g_ref[...]).astype(BF16)

    logf = _log_sigmoid(_dot(h, wf_ref[...]) + bf_ref[...])
    logf_ref[...] = logf[:, :FOX_HEADS]
    lane = lax.broadcasted_iota(jnp.int32, logf.shape, 1)
    logf = jnp.where(lane < FOX_HEADS, logf, 0.0)
    r = _dot(tri_ref[...], jnp.concatenate(_split3(logf), axis=1))
    c = r[:, :LANES] + r[:, LANES:2 * LANES] + r[:, 2 * LANES:] + carry_ref[...]
    carry_ref[...] = c[tm - 1:tm, :]
    c3 = jnp.concatenate(_split3(c * LOG2E), axis=1)

    q_aug = (_dot(h, wq_ref[...]) * (FOX_HEAD_DIM ** -0.5 * LOG2E)
             + _dot(c3, eq_ref[...]) + oneq_ref[...])
    qT = q_aug.T.astype(BF16)
    k_aug = (_dot(h, wk_ref[...]) + _dot(c3, ek_ref[...]) + onek_ref[...]).astype(BF16)
    vT = (_dot(h, wv_ref[...]) + onev_ref[...]).T.astype(BF16)
    for hh in range(FOX_HEADS):
        qT_ref[hh] = qT[hh * HEAD_PAD:(hh + 1) * HEAD_PAD, :]
        ka_ref[hh] = k_aug[:, hh * HEAD_PAD:(hh + 1) * HEAD_PAD]
        vT_ref[hh] = vT[hh * V_PAD:(hh + 1) * V_PAD, :]

    zkv = _dot(h, wkv_ref[...])
    k32_ref[...] = zkv[:, :FOX_W]
    v32_ref[...] = zkv[:, FOX_W:]

    zb = _dot(h, wb_ref[...])
    cos2 = cos_ref[...]
    sin2 = sin_ref[...]
    for hh in range(RET_HEADS):
        sl = slice(hh * RET_KEY_DIM, (hh + 1) * RET_KEY_DIM)
        qb_ref[:, sl] = _rotary(zb[:, sl], cos2, sin2).astype(BF16)
        xk = zb[:, RET_KW + hh * RET_KEY_DIM:RET_KW + (hh + 1) * RET_KEY_DIM]
        kb_ref[:, sl] = (_rotary(xk, cos2, sin2) * (RET_KEY_DIM ** -0.5)).astype(BF16)
    vb_ref[...] = zb[:, 2 * RET_KW:].astype(BF16)


def _inproj_prompt(x, g, wq, wk, wv, wkv, wf, bf, wb, cos2, sin2, consts, tm):
    S = x.shape[0]
    tri, eq, ek, oneq, onek, onev = consts
    row = lambda w: pl.BlockSpec((tm, w), lambda i: (i, 0))
    in_specs = [row(D_MODEL), _const_spec(g.shape), _const_spec(wq.shape), _const_spec(wk.shape),
                _const_spec(wv.shape), _const_spec(wkv.shape), _const_spec(wf.shape),
                _const_spec(bf.shape), _const_spec(wb.shape), row(LANES), row(LANES),
                _const_spec(tri.shape), _const_spec(eq.shape), _const_spec(ek.shape),
                _const_spec(oneq.shape), _const_spec(onek.shape), _const_spec(onev.shape)]
    out_shape = (
        jax.ShapeDtypeStruct((FOX_HEADS, HEAD_PAD, S), BF16),
        jax.ShapeDtypeStruct((FOX_HEADS, S, HEAD_PAD), BF16),
        jax.ShapeDtypeStruct((FOX_HEADS, V_PAD, S), BF16),
        jax.ShapeDtypeStruct((S, FOX_W), F32),
        jax.ShapeDtypeStruct((S, FOX_W), F32),
        jax.ShapeDtypeStruct((S, FOX_HEADS), F32),
        jax.ShapeDtypeStruct((S, RET_KW), BF16),
        jax.ShapeDtypeStruct((S, RET_KW), BF16),
        jax.ShapeDtypeStruct((S, RET_VW), BF16),
    )
    out_specs = (
        pl.BlockSpec((FOX_HEADS, HEAD_PAD, tm), lambda i: (0, 0, i)),
        pl.BlockSpec((FOX_HEADS, tm, HEAD_PAD), lambda i: (0, i, 0)),
        pl.BlockSpec((FOX_HEADS, V_PAD, tm), lambda i: (0, 0, i)),
        row(FOX_W), row(FOX_W), row(FOX_HEADS), row(RET_KW), row(RET_KW), row(RET_VW),
    )
    return pl.pallas_call(
        _inproj_prompt_kernel,
        grid=(S // tm,),
        in_specs=in_specs,
        out_specs=out_specs,
        out_shape=out_shape,
        scratch_shapes=[pltpu.VMEM((1, LANES), F32)],
        compiler_params=pltpu.CompilerParams(dimension_semantics=("arbitrary",),
                                             vmem_limit_bytes=VMEM_LIMIT),
        name="inproj_prompt",
    )(x, g, wq, wk, wv, wkv, wf, bf, wb, cos2, sin2, tri, eq, ek, oneq, onek, onev)


def _inproj_sample_kernel(x_ref, g_ref, wqkv_ref, wf_ref, bf_ref, wb_ref, cos_ref, sin_ref,
                          q_ref, k32_ref, v32_ref, logf_ref, qb_ref, kb_ref, vb_ref):
    h = _rmsnorm(x_ref[...], g_ref[...]).astype(BF16)
    logf = _log_sigmoid(_dot(h, wf_ref[...]) + bf_ref[...])
    logf_ref[...] = logf[:, :FOX_HEADS]
    z = _dot(h, wqkv_ref[...])
    q_ref[...] = (z[:, :FOX_W] * (FOX_HEAD_DIM ** -0.5 * LOG2E)).astype(BF16)
    k32_ref[...] = z[:, FOX_W:2 * FOX_W]
    v32_ref[...] = z[:, 2 * FOX_W:]
    zb = _dot(h, wb_ref[...])
    cos2 = cos_ref[...]
    sin2 = sin_ref[...]
    for hh in range(RET_HEADS):
        sl = slice(hh * RET_KEY_DIM, (hh + 1) * RET_KEY_DIM)
        qb_ref[:, sl] = _rotary(zb[:, sl], cos2, sin2).astype(BF16)
        xk = zb[:, RET_KW + hh * RET_KEY_DIM:RET_KW + (hh + 1) * RET_KEY_DIM]
        kb_ref[:, sl] = (_rotary(xk, cos2, sin2) * (RET_KEY_DIM ** -0.5)).astype(BF16)
    vb_ref[...] = zb[:, 2 * RET_KW:].astype(BF16)


def _inproj_sample(x, g, wqkv, wf, bf, wb, cos2, sin2):
    M = x.shape[0]
    args = (x, g, wqkv, wf, bf, wb, cos2, sin2)
    out_shape = (
        jax.ShapeDtypeStruct((M, FOX_W), BF16),
        jax.ShapeDtypeStruct((M, FOX_W), F32),
        jax.ShapeDtypeStruct((M, FOX_W), F32),
        jax.ShapeDtypeStruct((M, FOX_HEADS), F32),
        jax.ShapeDtypeStruct((M, RET_KW), BF16),
        jax.ShapeDtypeStruct((M, RET_KW), BF16),
        jax.ShapeDtypeStruct((M, RET_VW), BF16),
    )
    return pl.pallas_call(
        _inproj_sample_kernel,
        grid=(1,),
        in_specs=[_const_spec(a.shape) for a in args],
        out_specs=tuple(_const_spec(o.shape) for o in out_shape),
        out_shape=out_shape,
        compiler_params=pltpu.CompilerParams(dimension_semantics=("arbitrary",),
                                             vmem_limit_bytes=VMEM_LIMIT),
        name="inproj_sample",
    )(*args)


def _fox_prompt_kernel(it_ref, jt_ref, qT_ref, ka_ref, vT_ref, o_ref, m_ref, acc_ref):
    t = pl.program_id(0)
    i = it_ref[t]
    j = jt_ref[t]

    @pl.when(j == 0)
    def _():
        m_ref[...] = jnp.full_like(m_ref, NEG)
        acc_ref[...] = jnp.zeros_like(acc_ref)

    def head_step(hh, masked):
        s = _dot(ka_ref[hh], qT_ref[hh])
        if masked:
            kk = lax.broadcasted_iota(jnp.int32, s.shape, 0)
            qq = lax.broadcasted_iota(jnp.int32, s.shape, 1)
            s = jnp.where(kk > qq, NEG, s)
        m_old = m_ref[hh]
        m_new = jnp.maximum(m_old, jnp.max(s, axis=0, keepdims=True))
        p = jnp.exp2(s - m_new).astype(BF16)
        alpha = jnp.exp2(m_old - m_new)
        acc_ref[hh] = alpha * acc_ref[hh] + _dot(vT_ref[hh], p)
        m_ref[hh] = m_new

    @pl.when(j < i)
    def _():
        def body(hh, carry):
            head_step(hh, False)
            return carry
        lax.fori_loop(0, FOX_HEADS, body, 0)

    @pl.when(j == i)
    def _():
        def body(hh, carry):
            head_step(hh, True)
            return carry
        lax.fori_loop(0, FOX_HEADS, body, 0)
        for pr in range(FOX_HEADS // 2):
            halves = []
            for hh in (2 * pr, 2 * pr + 1):
                a = acc_ref[hh]
                halves.append(a[:FOX_HEAD_DIM] / a[FOX_HEAD_DIM:FOX_HEAD_DIM + 1])
            o_ref[pr] = jnp.concatenate(halves, axis=0).T.astype(BF16)


def _fox_prompt(qT, ka, vT, T):
    S = ka.shape[1]
    nb = S // T
    it = np.array([i for i in range(nb) for _ in range(i + 1)], np.int32)
    jt = np.array([j for i in range(nb) for j in range(i + 1)], np.int32)
    grid_spec = pltpu.PrefetchScalarGridSpec(
        num_scalar_prefetch=2,
        grid=(len(it),),
        in_specs=[
            pl.BlockSpec((FOX_HEADS, HEAD_PAD, T), lambda t, it, jt: (0, 0, it[t])),
            pl.BlockSpec((FOX_HEADS, T, HEAD_PAD), lambda t, it, jt: (0, jt[t], 0)),
            pl.BlockSpec((FOX_HEADS, V_PAD, T), lambda t, it, jt: (0, 0, jt[t])),
        ],
        out_specs=pl.BlockSpec((FOX_HEADS // 2, T, LANES), lambda t, it, jt: (0, it[t], 0)),
        scratch_shapes=[pltpu.VMEM((FOX_HEADS, 1, T), F32),
                        pltpu.VMEM((FOX_HEADS, V_PAD, T), F32)],
    )
    return pl.pallas_call(
        _fox_prompt_kernel,
        grid_spec=grid_spec,
        out_shape=jax.ShapeDtypeStruct((FOX_HEADS // 2, S, LANES), BF16),
        compiler_params=pltpu.CompilerParams(dimension_semantics=("arbitrary",),
                                             vmem_limit_bytes=VMEM_LIMIT),
        name="fox_prompt",
    )(jnp.asarray(it), jnp.asarray(jt), qT, ka, vT)


def _fox_sample_kernel(q_ref, kn_ref, vn_ref, ck_ref, cv_ref, lfT_ref, up_ref, ex_ref,
                       o_ref, kx_ref, vx_ref):
    P = ck_ref.shape[1]
    Tn = q_ref.shape[0]
    KP = kx_ref.shape[0]
    HQ = FOX_HEADS * Tn
    nchunk = KP // LANES

    kx_ref[:P, :] = ck_ref[0].astype(BF16)
    vx_ref[:P, :] = cv_ref[0].astype(BF16)
    kx_ref[P:P + Tn, :] = kn_ref[...].astype(BF16)
    vx_ref[P:P + Tn, :] = vn_ref[...].astype(BF16)
    kx_ref[P + Tn:, :] = jnp.zeros((KP - P - Tn, FOX_W), BF16)
    vx_ref[P + Tn:, :] = jnp.zeros((KP - P - Tn, FOX_W), BF16)

    def stack3(x):
        parts3 = [t.astype(F32) for t in _split3(x)] + [jnp.zeros_like(x)]
        return jnp.concatenate(parts3, axis=0).astype(BF16)

    x3 = stack3(lfT_ref[0])
    up = up_ref[...]
    run = jnp.zeros((4 * FOX_HEADS, 1), F32)
    parts = []
    for cidx in range(nchunk):
        y = _dot(x3[:, cidx * LANES:(cidx + 1) * LANES], up) + run
        parts.append(y)
        run = y[:, LANES - 1:LANES]
    y = jnp.concatenate(parts, axis=1)
    cT = (y[:FOX_HEADS] + y[FOX_HEADS:2 * FOX_HEADS] + y[2 * FOX_HEADS:3 * FOX_HEADS]) * LOG2E
    ckx = _dot(ex_ref[...], stack3(cT))

    tail = ckx[:, P:P + LANES]
    rowq = lax.broadcasted_iota(jnp.int32, tail.shape, 0) % Tn
    lanek = lax.broadcasted_iota(jnp.int32, tail.shape, 1)
    cq = jnp.sum(jnp.where(lanek == rowq, tail, 0.0), axis=1, keepdims=True)

    q = q_ref[...]
    qt = jnp.concatenate([q] * FOX_HEADS, axis=0)
    rh = lax.broadcasted_iota(jnp.int32, qt.shape, 0) // Tn
    lh = lax.broadcasted_iota(jnp.int32, qt.shape, 1) // FOX_HEAD_DIM
    qbd = jnp.where(rh == lh, qt, jnp.zeros_like(qt))

    s = lax.dot_general(qbd, kx_ref[...], (((1,), (1,)), ((), ())), preferred_element_type=F32)
    s = s + cq - ckx
    key = lax.broadcasted_iota(jnp.int32, s.shape, 1)
    qpos = P + lax.broadcasted_iota(jnp.int32, s.shape, 0) % Tn
    s = jnp.where(key > qpos, NEG, s)
    m = jnp.max(s, axis=1, keepdims=True)
    p = jnp.exp2(s - m)
    l = jnp.sum(p, axis=1, keepdims=True)
    z = _dot(p.astype(BF16), vx_ref[...]) / l
    zh = lax.broadcasted_iota(jnp.int32, (Tn, FOX_W), 1) // FOX_HEAD_DIM
    o = jnp.zeros((Tn, FOX_W), F32)
    for hh in range(FOX_HEADS):
        o = o + jnp.where(zh == hh, z[hh * Tn:(hh + 1) * Tn, :], 0.0)
    o_ref[...] = o.astype(BF16)


def _fox_sample(q, kn, vn, cache_k, cache_v, lfT, B, Tn):
    P = cache_k.shape[1]
    KP = lfT.shape[2]
    HQ = FOX_HEADS * Tn
    up = jnp.asarray(np.triu(np.ones((LANES, LANES), np.float32)), BF16)
    ex = np.zeros((HQ, 4 * FOX_HEADS), np.float32)
    for part in range(3):
        for hh in range(FOX_HEADS):
            ex[hh * Tn:(hh + 1) * Tn, part * FOX_HEADS + hh] = 1.0
    ex = jnp.asarray(ex, BF16)
    rowb = lambda w: pl.BlockSpec((Tn, w), lambda b: (b, 0))
    return pl.pallas_call(
        _fox_sample_kernel,
        grid=(B,),
        in_specs=[rowb(FOX_W), rowb(FOX_W), rowb(FOX_W),
                  pl.BlockSpec((1, P, FOX_W), lambda b: (b, 0, 0)),
                  pl.BlockSpec((1, P, FOX_W), lambda b: (b, 0, 0)),
                  pl.BlockSpec((1, FOX_HEADS, KP), lambda b: (b, 0, 0)),
                  _const_spec(up.shape), _const_spec(ex.shape)],
        out_specs=rowb(FOX_W),
        out_shape=jax.ShapeDtypeStruct((B * Tn, FOX_W), BF16),
        scratch_shapes=[pltpu.VMEM((KP, FOX_W), BF16), pltpu.VMEM((KP, FOX_W), BF16)],
        compiler_params=pltpu.CompilerParams(dimension_semantics=("arbitrary",),
                                             vmem_limit_bytes=VMEM_LIMIT),
        name="fox_sample",
    )(q, kn, vn, cache_k, cache_v, lfT, up, ex)


def _retention_kernel(q_ref, k_ref, v_ref, s0_ref, dmat_ref, xi_ref, zeta_ref, gam_ref,
                      n_ref, sout_ref, st_ref):
    c = pl.program_id(1)

    @pl.when(c == 0)
    def _():
        st_ref[...] = s0_ref[0]

    for hh in range(RET_HEADS):
        q = q_ref[:, hh * RET_KEY_DIM:(hh + 1) * RET_KEY_DIM]
        k = k_ref[:, hh * RET_KEY_DIM:(hh + 1) * RET_KEY_DIM]
        v = v_ref[:, hh * RET_VAL_DIM:(hh + 1) * RET_VAL_DIM]
        st = st_ref[hh]
        sc = lax.dot_general(q, k, (((1,), (1,)), ((), ())), preferred_element_type=F32)
        sc = sc * dmat_ref[hh]
        o = _dot(sc.astype(BF16), v) + _dot(q, st.astype(BF16)) * xi_ref[hh]
        kz = (k.astype(F32) * zeta_ref[hh]).astype(BF16)
        upd = lax.dot_general(kz, v, (((0,), (0,)), ((), ())), preferred_element_type=F32)
        st_ref[hh] = gam_ref[hh] * st + upd
        mu = jnp.mean(o, axis=-1, keepdims=True)
        d = o - mu
        var = jnp.mean(d * d, axis=-1, keepdims=True)
        n_ref[:, hh * RET_VAL_DIM:(hh + 1) * RET_VAL_DIM] = (d * lax.rsqrt(var + EPS)).astype(BF16)

    @pl.when(c == pl.num_programs(1) - 1)
    def _():
        sout_ref[0] = st_ref[...]


def _ret_log_gamma():
    return jnp.log(1.0 - jnp.exp2(-5.0 - jnp.arange(RET_HEADS, dtype=F32)))


def _retention(q, k, v, state0, B, L, C):
    nc = L // C
    lg = _ret_log_gamma()
    idx = jnp.arange(C, dtype=F32)
    diff = idx[:, None] - idx[None, :]
    dmat = jnp.where(diff[None] >= 0, jnp.exp(jnp.maximum(diff, 0.0)[None] * lg[:, None, None]), 0.0)
    xi = jnp.exp((idx[None, :] + 1.0) * lg[:, None])
    zeta = jnp.exp((C - 1.0 - idx[None, :]) * lg[:, None])
    xi = jnp.broadcast_to(xi[:, :, None], (RET_HEADS, C, RET_VAL_DIM))
    zeta = jnp.broadcast_to(zeta[:, :, None], (RET_HEADS, C, RET_KEY_DIM))
    gam = jnp.broadcast_to(jnp.exp(C * lg)[:, None, None], (RET_HEADS, 1, RET_VAL_DIM))
    rowc = lambda w: pl.BlockSpec((C, w), lambda b, c: (b * nc + c, 0))
    st_spec = pl.BlockSpec((1, RET_HEADS, RET_KEY_DIM, RET_VAL_DIM), lambda b, c: (b, 0, 0, 0))
    return pl.pallas_call(
        _retention_kernel,
        grid=(B, nc),
        in_specs=[rowc(RET_KW), rowc(RET_KW), rowc(RET_VW), st_spec,
                  _const_spec(dmat.shape), _const_spec(xi.shape), _const_spec(zeta.shape),
                  _const_spec(gam.shape)],
        out_specs=(rowc(RET_VW), st_spec),
        out_shape=(jax.ShapeDtypeStruct((B * L, RET_VW), BF16),
                   jax.ShapeDtypeStruct((B, RET_HEADS, RET_KEY_DIM, RET_VAL_DIM), F32)),
        scratch_shapes=[pltpu.VMEM((RET_HEADS, RET_KEY_DIM, RET_VAL_DIM), F32)],
        compiler_params=pltpu.CompilerParams(dimension_semantics=("arbitrary", "arbitrary"),
                                             vmem_limit_bytes=VMEM_LIMIT),
        name="retention",
    )(q, k, v, state0, dmat, xi, zeta, gam)


def _mixer_ffn_kernel(x_ref, oa_ref, nb_ref, prev_ref, gmix_ref, wg_ref, gng_ref, wpa_ref, wpb_ref,
                      wo_ref, gffn_ref, wup_ref, cw_ref, cb_ref, wdn_ref, gfin_ref,
                      y_ref, conv_ref, carry_ref, ua_ref, ub_ref, acc_ref, h2_ref,
                      *, nseg, seglen):
    i = pl.program_id(0)
    NC = N_FFN_CHUNKS
    PADR = 8
    H0 = PADR - (CONV_WIDTH - 1)

    @pl.when(i == 0)
    def _():
        carry_ref[...] = prev_ref[...]

    x = x_ref[...]
    h = _rmsnorm(x, gmix_ref[...]).astype(BF16)
    zg = _dot(h, wg_ref[...])
    gb = zg[:, :RET_VW]
    gma = zg[:, RET_VW:RET_VW + D_MODEL]
    gmb = zg[:, RET_VW + D_MODEL:]
    oa = jnp.concatenate([oa_ref[p] for p in range(FOX_HEADS // 2)], axis=1)
    ya = _dot(oa, wpa_ref[...])
    nn = nb_ref[...].astype(F32) * gng_ref[...] * (gb * jax.nn.sigmoid(gb))
    yb = _dot(nn.astype(BF16), wpb_ref[...])
    y = jax.nn.sigmoid(gma) * ya + jax.nn.sigmoid(gmb) * yb
    x1 = x + _dot(y.astype(BF16), wo_ref[...])
    h2_ref[...] = _rmsnorm(x1, gffn_ref[...]).astype(BF16)
    acc_ref[...] = x1

    def conv_half(u, u_ref, cidx):
        w = cw_ref[cidx]
        b = cb_ref[cidx]
        outs = []
        for s in range(nseg):
            u_ref[s, PADR:PADR + seglen, :] = u[s * seglen:(s + 1) * seglen, :]
            u_ref[s, H0:PADR, :] = carry_ref[cidx, s, H0:PADR, :]
            acc = w[0:1] * u_ref[s, H0:H0 + seglen, :]
            for jj in range(1, CONV_WIDTH):
                acc = acc + w[jj:jj + 1] * u_ref[s, H0 + jj:H0 + jj + seglen, :]
            outs.append(b + acc)
            carry_ref[cidx, s, H0:PADR, :] = u_ref[s, H0 + seglen:PADR + seglen, :]
        return outs[0] if nseg == 1 else jnp.concatenate(outs, axis=0)

    def chunk(cidx, carry):
        h2 = h2_ref[...]
        a = conv_half(_dot(h2, wup_ref[cidx]), ua_ref, cidx)
        b = conv_half(_dot(h2, wup_ref[NC + cidx]), ub_ref, NC + cidx)
        g = (jax.nn.gelu(a) * b).astype(BF16)
        acc_ref[...] += _dot(g, wdn_ref[cidx])
        return carry

    lax.fori_loop(0, NC, chunk, 0)
    y_ref[...] = _rmsnorm(acc_ref[...], gfin_ref[...])

    @pl.when(i == pl.num_programs(0) - 1)
    def _():
        conv_ref[...] = carry_ref[:, :, H0:PADR, :]


def _mixer_ffn(x, oa, nb, prev, weights, tm, nseg, seglen):
    M = x.shape[0]
    gmix, wg, gng, wpa, wpb, wo, gffn, wup, cw, cb, wdn, gfin = weights
    FC = FFN_CHUNK
    NC = N_FFN_CHUNKS
    row = lambda w: pl.BlockSpec((tm, w), lambda i: (i, 0))
    wspec = lambda a: pl.BlockSpec(a.shape, lambda i, n=a.ndim: (0,) * n,
                                   pipeline_mode=pl.Buffered(1))
    in_specs = [row(D_MODEL),
                pl.BlockSpec((FOX_HEADS // 2, tm, LANES), lambda i: (0, i, 0)),
                row(RET_VW), wspec(prev)] + [wspec(w) for w in weights]
    out_shape = (jax.ShapeDtypeStruct((M, D_MODEL), F32),
                 jax.ShapeDtypeStruct((2 * NC, nseg, CONV_WIDTH - 1, FC), F32))
    out_specs = (row(D_MODEL), _const_spec(out_shape[1].shape))
    return pl.pallas_call(
        functools.partial(_mixer_ffn_kernel, nseg=nseg, seglen=seglen),
        grid=(M // tm,),
        in_specs=in_specs,
        out_specs=out_specs,
        out_shape=out_shape,
        scratch_shapes=[pltpu.VMEM((2 * NC, nseg, 8, FC), F32),
                        pltpu.VMEM((nseg, 8 + seglen, FC), F32),
                        pltpu.VMEM((nseg, 8 + seglen, FC), F32),
                        pltpu.VMEM((tm, D_MODEL), F32),
                        pltpu.VMEM((tm, D_MODEL), BF16)],
        compiler_params=pltpu.CompilerParams(dimension_semantics=("arbitrary",),
                                             vmem_limit_bytes=VMEM_LIMIT),
        name="mixer_ffn",
    )(x, oa, nb, prev, *weights)


def _rotary_tables(pos):
    half = RET_KEY_DIM // 2
    inv = 1.0 / (ROPE_BASE ** jnp.linspace(0.0, 1.0, half, dtype=F32))
    ang = pos.astype(F32)[:, None] * inv[None, :]
    cos = jnp.cos(ang)
    sin = jnp.sin(ang)
    return jnp.concatenate([cos, cos], axis=1), jnp.concatenate([-sin, sin], axis=1)


def _pad_heads(w, pad):
    d = w.shape[0]
    w = w.reshape(d, FOX_HEADS, FOX_HEAD_DIM)
    w = jnp.pad(w, ((0, 0), (0, 0), (0, pad - FOX_HEAD_DIM)))
    return w.reshape(d, FOX_HEADS * pad)


def _prompt_consts(tm):
    tri = np.tril(np.ones((tm, tm), np.float32))
    eq = np.zeros((3 * LANES, FOX_HEADS * HEAD_PAD), np.float32)
    ek = np.zeros((3 * LANES, FOX_HEADS * HEAD_PAD), np.float32)
    oneq = np.zeros((1, FOX_HEADS * HEAD_PAD), np.float32)
    onek = np.zeros((1, FOX_HEADS * HEAD_PAD), np.float32)
    onev = np.zeros((1, FOX_HEADS * V_PAD), np.float32)
    for hh in range(FOX_HEADS):
        base = hh * HEAD_PAD + BIAS_COL
        for part in range(3):
            eq[part * LANES + hh, base + part] = 1.0
            ek[part * LANES + hh, base + 3 + part] = -1.0
            onek[0, base + part] = 1.0
            oneq[0, base + 3 + part] = 1.0
        onev[0, hh * V_PAD + FOX_HEAD_DIM] = 1.0
    return (jnp.asarray(tri, BF16), jnp.asarray(eq, BF16), jnp.asarray(ek, BF16),
            jnp.asarray(oneq), jnp.asarray(onek), jnp.asarray(onev))


def _chunk_cols(a):
    lead = a.shape[:-1]
    a = a.reshape(lead + (2 * N_FFN_CHUNKS, FFN_CHUNK))
    return jnp.moveaxis(a, -2, 0)


def _tile(n, pref):
    t = min(n, pref)
    while n % t:
        t //= 2
    return t


def kernel(x_prompt, x_sample, cache_fox_k, cache_fox_v, cache_fox_logf, state_ret, state_ffn_conv,
           norm_mix_g, w_in, b_fox_f, gn_ret_g, w_pa, w_pb, w_o, norm_ffn_g, w_up, conv_w, conv_b,
           w_down, norm_final_g):
    depth = w_in.shape[0]
    Bp, S, _ = x_prompt.shape
    Bs, Ts, _ = x_sample.shape
    P = cache_fox_k.shape[2]
    assert depth == 1 and Bp == 1, "kernel handles the single-layer, single-prompt configuration"
    l = 0

    w = w_in[l]
    o0 = 3 * FOX_W
    o1 = o0 + FOX_HEADS
    o2 = o1 + 2 * RET_KW + RET_VW
    w_q, w_k, w_v = w[:, :FOX_W], w[:, FOX_W:2 * FOX_W], w[:, 2 * FOX_W:o0]
    wq_aug = _pad_heads(w_q, HEAD_PAD).astype(BF16)
    wk_aug = _pad_heads(w_k, HEAD_PAD).astype(BF16)
    wv_aug = _pad_heads(w_v, V_PAD).astype(BF16)
    wkv = w[:, FOX_W:o0].astype(BF16)
    wqkv = w[:, :o0].astype(BF16)
    wf = jnp.pad(w[:, o0:o1], ((0, 0), (0, LANES - FOX_HEADS))).astype(BF16)
    bf = jnp.pad(b_fox_f[l].astype(F32), (0, LANES - FOX_HEADS))[None, :]
    wb = w[:, o1:o2].astype(BF16)
    wg = w[:, o2:].astype(BF16)
    gmix = norm_mix_g[l].astype(F32)[None, :]
    mix_weights = (
        gmix, wg, gn_ret_g[l].astype(F32)[None, :], w_pa[l].astype(BF16), w_pb[l].astype(BF16),
        w_o[l].astype(BF16), norm_ffn_g[l].astype(F32)[None, :],
        _chunk_cols(w_up[l]).astype(BF16),
        jnp.pad(_chunk_cols(conv_w[l].astype(F32)), ((0, 0), (0, 8 - CONV_WIDTH), (0, 0))),
        _chunk_cols(conv_b[l].astype(F32)[None, :]),
        w_down[l].reshape(N_FFN_CHUNKS, FFN_CHUNK, D_MODEL).astype(BF16),
        norm_final_g.astype(F32)[None, :],
    )

    tm_a = _tile(S, 256)
    cos_p, sin_p = _rotary_tables(jnp.arange(S))
    (qT, ka, vT, k_p, v_p, logf_p, qb, kb, vb) = _inproj_prompt(
        x_prompt[0], gmix, wq_aug, wk_aug, wv_aug, wkv, wf, bf, wb, cos_p, sin_p,
        _prompt_consts(tm_a), tm_a)
    oa_p = _fox_prompt(qT, ka, vT, _tile(S, 1024))
    zero_state = jnp.zeros((1, RET_HEADS, RET_KEY_DIM, RET_VAL_DIM), F32)
    nb_p, ret_p = _retention(qb, kb, vb, zero_state, 1, S, _tile(S, 256))
    tm_d = _tile(S, 256)
    zero_prev = jnp.zeros((2 * N_FFN_CHUNKS, 1, 8, FFN_CHUNK), F32)
    y_p, conv_p = _mixer_ffn(x_prompt[0], oa_p, nb_p, zero_prev, mix_weights, tm_d, 1, tm_d)

    Ms = Bs * Ts
    cos_s, sin_s = _rotary_tables(P + jnp.arange(Ts))
    cos_s = jnp.tile(cos_s, (Bs, 1))
    sin_s = jnp.tile(sin_s, (Bs, 1))
    (q_s, k_s, v_s, logf_s, qb_s, kb_s, vb_s) = _inproj_sample(
        x_sample.reshape(Ms, D_MODEL), gmix, wqkv, wf, bf, wb, cos_s, sin_s)
    KP = ((P + Ts + LANES - 1) // LANES) * LANES
    lf_all = jnp.concatenate([cache_fox_logf[l].astype(F32), logf_s.reshape(Bs, Ts, FOX_HEADS)], axis=1)
    lfT = jnp.pad(jnp.swapaxes(lf_all, 1, 2), ((0, 0), (0, 0), (0, KP - P - Ts)))
    oa_s = _fox_sample(q_s, k_s, v_s, cache_fox_k[l].reshape(Bs, P, FOX_W),
                       cache_fox_v[l].reshape(Bs, P, FOX_W), lfT, Bs, Ts)
    oa_s = jnp.moveaxis(oa_s.reshape(Ms, FOX_HEADS // 2, LANES), 1, 0)
    nb_s, ret_s = _retention(qb_s, kb_s, vb_s, state_ret[l].astype(F32), Bs, Ts, Ts)
    prev_s = _chunk_cols(state_ffn_conv[l].astype(F32))
    prev_s = jnp.pad(prev_s, ((0, 0), (0, 0), (8 - (CONV_WIDTH - 1), 0), (0, 0)))
    y_s, conv_s = _mixer_ffn(x_sample.reshape(Ms, D_MODEL), oa_s, nb_s, prev_s, mix_weights,
                             Ms, Bs, Ts)

    def unchunk(cv):
        return jnp.moveaxis(cv, 0, 2).reshape(cv.shape[1], CONV_WIDTH - 1, 2 * FFN_DIM)

    hshape = (FOX_HEADS, FOX_HEAD_DIM)
    return (
        y_p[None],
        y_s.reshape(Bs, Ts, D_MODEL),
        k_p.reshape((1, 1, S) + hshape),
        v_p.reshape((1, 1, S) + hshape),
        logf_p.reshape(1, 1, S, FOX_HEADS),
        ret_p[None],
        unchunk(conv_p)[None],
        k_s.reshape((1, Bs, Ts) + hshape),
        v_s.reshape((1, Bs, Ts) + hshape),
        logf_s.reshape(1, Bs, Ts, FOX_HEADS),
        ret_s[None],
        unchunk(conv_s)[None],
    )
```

```python
import functools
import math

import numpy as np
import jax
import jax.numpy as jnp
from jax import lax
from jax.experimental import pallas as pl
from jax.experimental.pallas import tpu as pltpu

F32 = jnp.float32
BF16 = jnp.bfloat16

D_MODEL = 1024
FOX_HEADS = 8
FOX_HEAD_DIM = 64
RET_HEADS = 4
RET_KEY_DIM = 128
RET_VAL_DIM = 256
FFN_DIM = 2816
CONV_WIDTH = 3
EPS = 1e-6
ROPE_BASE = 10000.0

FOX_W = FOX_HEADS * FOX_HEAD_DIM
RET_KW = RET_HEADS * RET_KEY_DIM
RET_VW = RET_HEADS * RET_VAL_DIM

LOG2E = 1.4426950408889634
LANES = 128
HEAD_PAD = LANES
V_PAD = 80
BIAS_COL = FOX_HEAD_DIM
NEG = -1e30
STALE_MAX_GUARD = 2.0 ** 60
FFN_CHUNK = 256
N_FFN_CHUNKS = FFN_DIM // FFN_CHUNK
VMEM_LIMIT = 56 * 1024 * 1024


def _rmsnorm(x, g):
    ms = jnp.mean(x * x, axis=-1, keepdims=True)
    return x * lax.rsqrt(ms + EPS) * g


def _split3(x):
    hi = x.astype(BF16)
    r1 = x - hi.astype(F32)
    mid = r1.astype(BF16)
    lo = (r1 - mid.astype(F32)).astype(BF16)
    return hi, mid, lo


def _log_sigmoid(x):
    return jnp.minimum(x, 0.0) - jnp.log1p(jnp.exp(-jnp.abs(x)))


def _dot(a, b):
    return jnp.dot(a, b, preferred_element_type=F32)


def _rotary(x, cos2, sin2):
    return x * cos2 + pltpu.roll(x, RET_KEY_DIM // 2, 1) * sin2


def _const_spec(shape):
    n = len(shape)
    return pl.BlockSpec(shape, lambda *_: (0,) * n)


def _inproj_prompt_kernel(x_ref, g_ref, wq_ref, wk_ref, wv_ref, wkv_ref, wf_ref, bf_ref, wb_ref,
                          cos_ref, sin_ref, tri_ref, eq_ref, ek_ref, oneq_ref, onek_ref, onev_ref,
                          qT_ref, ka_ref, vT_ref, k32_ref, v32_ref, logf_ref, qb_ref, kb_ref, vb_ref,
                          carry_ref):
    tm = x_ref.shape[0]

    @pl.when(pl.program_id(0) == 0)
    def _():
        carry_ref[...] = jnp.zeros_like(carry_ref)

    h = _rmsnorm(x_ref[...], g_ref[...]).astype(BF16)

    logf = _log_sigmoid(_dot(h, wf_ref[...]) + bf_ref[...])
    logf_ref[...] = logf[:, :FOX_HEADS]
    lane = lax.broadcasted_iota(jnp.int32, logf.shape, 1)
    logf = jnp.where(lane < FOX_HEADS, logf, 0.0)
    r = _dot(tri_ref[...], jnp.concatenate(_split3(logf), axis=1))
    c = r[:, :LANES] + r[:, LANES:2 * LANES] + r[:, 2 * LANES:] + carry_ref[...]
    carry_ref[...] = c[tm - 1:tm, :]
    c3 = jnp.concatenate(_split3(c * LOG2E), axis=1)

    q_aug = (_dot(h, wq_ref[...]) * (FOX_HEAD_DIM ** -0.5 * LOG2E)
             + _dot(c3, eq_ref[...]) + oneq_ref[...])
    qT = q_aug.T.astype(BF16)
    k_aug = (_dot(h, wk_ref[...]) + _dot(c3, ek_ref[...]) + onek_ref[...]).astype(BF16)
    vT = (_dot(h, wv_ref[...]) + onev_ref[...]).T.astype(BF16)
    for hh in range(FOX_HEADS):
        qT_ref[hh] = qT[hh * HEAD_PAD:(hh + 1) * HEAD_PAD, :]
        ka_ref[hh] = k_aug[:, hh * HEAD_PAD:(hh + 1) * HEAD_PAD]
        vT_ref[hh] = vT[hh * V_PAD:(hh + 1) * V_PAD, :]

    zkv = _dot(h, wkv_ref[...])
    k32_ref[...] = zkv[:, :FOX_W]
    v32_ref[...] = zkv[:, FOX_W:]

    zb = _dot(h, wb_ref[...])
    cos2 = cos_ref[...]
    sin2 = sin_ref[...]
    for hh in range(RET_HEADS):
        sl = slice(hh * RET_KEY_DIM, (hh + 1) * RET_KEY_DIM)
        qb_ref[:, sl] = _rotary(zb[:, sl], cos2, sin2).astype(BF16)
        xk = zb[:, RET_KW + hh * RET_KEY_DIM:RET_KW + (hh + 1) * RET_KEY_DIM]
        kb_ref[:, sl] = (_rotary(xk, cos2, sin2) * (RET_KEY_DIM ** -0.5)).astype(BF16)
    vb_ref[...] = zb[:, 2 * RET_KW:].astype(BF16)


def _inproj_prompt(x, g, wq, wk, wv, wkv, wf, bf, wb, cos2, sin2, consts, tm):
    S = x.shape[0]
    tri, eq, ek, oneq, onek, onev = consts
    row = lambda w: pl.BlockSpec((tm, w), lambda i: (i, 0))
    in_specs = [row(D_MODEL), _const_spec(g.shape), _const_spec(wq.shape), _const_spec(wk.shape),
                _const_spec(wv.shape), _const_spec(wkv.shape), _const_spec(wf.shape),
                _const_spec(bf.shape), _const_spec(wb.shape), row(LANES), row(LANES),
                _const_spec(tri.shape), _const_spec(eq.shape), _const_spec(ek.shape),
                _const_spec(oneq.shape), _const_spec(onek.shape), _const_spec(onev.shape)]
    out_shape = (
        jax.ShapeDtypeStruct((FOX_HEADS, HEAD_PAD, S), BF16),
        jax.ShapeDtypeStruct((FOX_HEADS, S, HEAD_PAD), BF16),
        jax.ShapeDtypeStruct((FOX_HEADS, V_PAD, S), BF16),
        jax.ShapeDtypeStruct((S, FOX_W), F32),
        jax.ShapeDtypeStruct((S, FOX_W), F32),
        jax.ShapeDtypeStruct((S, FOX_HEADS), F32),
        jax.ShapeDtypeStruct((S, RET_KW), BF16),
        jax.ShapeDtypeStruct((S, RET_KW), BF16),
        jax.ShapeDtypeStruct((S, RET_VW), BF16),
    )
    out_specs = (
        pl.BlockSpec((FOX_HEADS, HEAD_PAD, tm), lambda i: (0, 0, i)),
        pl.BlockSpec((FOX_HEADS, tm, HEAD_PAD), lambda i: (0, i, 0)),
        pl.BlockSpec((FOX_HEADS, V_PAD, tm), lambda i: (0, 0, i)),
        row(FOX_W), row(FOX_W), row(FOX_HEADS), row(RET_KW), row(RET_KW), row(RET_VW),
    )
    return pl.pallas_call(
        _inproj_prompt_kernel,
        grid=(S // tm,),
        in_specs=in_specs,
        out_specs=out_specs,
        out_shape=out_shape,
        scratch_shapes=[pltpu.VMEM((1, LANES), F32)],
        compiler_params=pltpu.CompilerParams(dimension_semantics=("arbitrary",),
                                             vmem_limit_bytes=VMEM_LIMIT),
        name="inproj_prompt",
    )(x, g, wq, wk, wv, wkv, wf, bf, wb, cos2, sin2, tri, eq, ek, oneq, onek, onev)


def _inproj_sample_kernel(x_ref, g_ref, wqkv_ref, wf_ref, bf_ref, wb_ref, cos_ref, sin_ref,
                          q_ref, k32_ref, v32_ref, logf_ref, qb_ref, kb_ref, vb_ref):
    h = _rmsnorm(x_ref[...], g_ref[...]).astype(BF16)
    logf = _log_sigmoid(_dot(h, wf_ref[...]) + bf_ref[...])
    logf_ref[...] = logf[:, :FOX_HEADS]
    z = _dot(h, wqkv_ref[...])
    q_ref[...] = (z[:, :FOX_W] * (FOX_HEAD_DIM ** -0.5 * LOG2E)).astype(BF16)
    k32_ref[...] = z[:, FOX_W:2 * FOX_W]
    v32_ref[...] = z[:, 2 * FOX_W:]
    zb = _dot(h, wb_ref[...])
    cos2 = cos_ref[...]
    sin2 = sin_ref[...]
    for hh in range(RET_HEADS):
        sl = slice(hh * RET_KEY_DIM, (hh + 1) * RET_KEY_DIM)
        qb_ref[:, sl] = _rotary(zb[:, sl], cos2, sin2).astype(BF16)
        xk = zb[:, RET_KW + hh * RET_KEY_DIM:RET_KW + (hh + 1) * RET_KEY_DIM]
        kb_ref[:, sl] = (_rotary(xk, cos2, sin2) * (RET_KEY_DIM ** -0.5)).astype(BF16)
    vb_ref[...] = zb[:, 2 * RET_KW:].astype(BF16)


def _inproj_sample(x, g, wqkv, wf, bf, wb, cos2, sin2):
    M = x.shape[0]
    args = (x, g, wqkv, wf, bf, wb, cos2, sin2)
    out_shape = (
        jax.ShapeDtypeStruct((M, FOX_W), BF16),
        jax.ShapeDtypeStruct((M, FOX_W), F32),
        jax.ShapeDtypeStruct((M, FOX_W), F32),
        jax.ShapeDtypeStruct((M, FOX_HEADS), F32),
        jax.ShapeDtypeStruct((M, RET_KW), BF16),
        jax.ShapeDtypeStruct((M, RET_KW), BF16),
        jax.ShapeDtypeStruct((M, RET_VW), BF16),
    )
    return pl.pallas_call(
        _inproj_sample_kernel,
        grid=(1,),
        in_specs=[_const_spec(a.shape) for a in args],
        out_specs=tuple(_const_spec(o.shape) for o in out_shape),
        out_shape=out_shape,
        compiler_params=pltpu.CompilerParams(dimension_semantics=("arbitrary",),
                                             vmem_limit_bytes=VMEM_LIMIT),
        name="inproj_sample",
    )(*args)


def _fox_prompt_kernel(it_ref, jt_ref, qT_ref, ka_ref, vT_ref, o_ref, m_ref, acc_ref, pv_ref):
    t = pl.program_id(0)
    i = it_ref[t]
    j = jt_ref[t]
    T = qT_ref.shape[2]

    def heads(fn):
        def body(hh, carry):
            fn(hh)
            return carry
        lax.fori_loop(0, FOX_HEADS, body, 0)

    def scores(hh):
        return _dot(ka_ref[hh], qT_ref[hh])

    def local_mask(s):
        kk = lax.broadcasted_iota(jnp.int32, s.shape, 0)
        qq = lax.broadcasted_iota(jnp.int32, s.shape, 1)
        return jnp.where(kk > qq, NEG, s)

    def exact_head(hh):
        s = scores(hh)
        kk = lax.broadcasted_iota(jnp.int32, s.shape, 0) + j * T
        qq = lax.broadcasted_iota(jnp.int32, s.shape, 1) + i * T
        s = jnp.where(kk > qq, NEG, s)
        m_old = m_ref[hh]
        m_new = jnp.maximum(m_old, jnp.max(s, axis=0, keepdims=True))
        p = jnp.exp2(s - m_new).astype(BF16)
        alpha = jnp.exp2(m_old - m_new)
        acc_ref[hh] = alpha * acc_ref[hh] + _dot(vT_ref[hh], p)
        m_ref[hh] = m_new

    def stale_head(hh, masked):
        s = scores(hh)
        if masked:
            s = local_mask(s)
        p = jnp.exp2(s - m_ref[hh]).astype(BF16)
        pv_ref[hh] = _dot(vT_ref[hh], p)

    def stale_step(masked):
        heads(lambda hh: stale_head(hh, masked))
        lsum = pv_ref[:, FOX_HEAD_DIM:FOX_HEAD_DIM + 1, :]
        bad = jnp.sum(jnp.where(lsum < STALE_MAX_GUARD, 0.0, 1.0)) > 0.0

        @pl.when(jnp.logical_not(bad))
        def _():
            acc_ref[...] += pv_ref[...]

        @pl.when(bad)
        def _():
            heads(exact_head)

    @pl.when(j == 0)
    def _():
        m_ref[...] = jnp.full_like(m_ref, NEG)
        acc_ref[...] = jnp.zeros_like(acc_ref)
        heads(exact_head)

    @pl.when(jnp.logical_and(j > 0, j < i))
    def _():
        stale_step(False)

    @pl.when(jnp.logical_and(j > 0, j == i))
    def _():
        stale_step(True)

    @pl.when(j == i)
    def _():
        for pr in range(FOX_HEADS // 2):
            halves = []
            for hh in (2 * pr, 2 * pr + 1):
                a = acc_ref[hh]
                halves.append(a[:FOX_HEAD_DIM] / a[FOX_HEAD_DIM:FOX_HEAD_DIM + 1])
            o_ref[pr] = jnp.concatenate(halves, axis=0).T.astype(BF16)


def _fox_prompt(qT, ka, vT, T):
    S = ka.shape[1]
    nb = S // T
    it = np.array([i for i in range(nb) for _ in range(i + 1)], np.int32)
    jt = np.array([j for i in range(nb) for j in range(i + 1)], np.int32)
    grid_spec = pltpu.PrefetchScalarGridSpec(
        num_scalar_prefetch=2,
        grid=(len(it),),
        in_specs=[
            pl.BlockSpec((FOX_HEADS, HEAD_PAD, T), lambda t, it, jt: (0, 0, it[t])),
            pl.BlockSpec((FOX_HEADS, T, HEAD_PAD), lambda t, it, jt: (0, jt[t], 0)),
            pl.BlockSpec((FOX_HEADS, V_PAD, T), lambda t, it, jt: (0, 0, jt[t])),
        ],
        out_specs=pl.BlockSpec((FOX_HEADS // 2, T, LANES), lambda t, it, jt: (0, it[t], 0)),
        scratch_shapes=[pltpu.VMEM((FOX_HEADS, 1, T), F32),
                        pltpu.VMEM((FOX_HEADS, V_PAD, T), F32),
                        pltpu.VMEM((FOX_HEADS, V_PAD, T), F32)],
    )
    return pl.pallas_call(
        _fox_prompt_kernel,
        grid_spec=grid_spec,
        out_shape=jax.ShapeDtypeStruct((FOX_HEADS // 2, S, LANES), BF16),
        compiler_params=pltpu.CompilerParams(dimension_semantics=("arbitrary",),
                                             vmem_limit_bytes=VMEM_LIMIT),
        name="fox_prompt",
    )(jnp.asarray(it), jnp.asarray(jt), qT, ka, vT)


def _fox_sample_kernel(q_ref, kn_ref, vn_ref, ck_ref, cv_ref, lfT_ref, up_ref, ex_ref,
                       o_ref, kx_ref, vx_ref):
    P = ck_ref.shape[1]
    Tn = q_ref.shape[0]
    KP = kx_ref.shape[0]
    HQ = FOX_HEADS * Tn
    nchunk = KP // LANES

    kx_ref[:P, :] = ck_ref[0].astype(BF16)
    vx_ref[:P, :] = cv_ref[0].astype(BF16)
    kx_ref[P:P + Tn, :] = kn_ref[...].astype(BF16)
    vx_ref[P:P + Tn, :] = vn_ref[...].astype(BF16)
    kx_ref[P + Tn:, :] = jnp.zeros((KP - P - Tn, FOX_W), BF16)
    vx_ref[P + Tn:, :] = jnp.zeros((KP - P - Tn, FOX_W), BF16)

    def stack3(x):
        parts3 = [t.astype(F32) for t in _split3(x)] + [jnp.zeros_like(x)]
        return jnp.concatenate(parts3, axis=0).astype(BF16)

    x3 = stack3(lfT_ref[0])
    up = up_ref[...]
    run = jnp.zeros((4 * FOX_HEADS, 1), F32)
    parts = []
    for cidx in range(nchunk):
        y = _dot(x3[:, cidx * LANES:(cidx + 1) * LANES], up) + run
        parts.append(y)
        run = y[:, LANES - 1:LANES]
    y = jnp.concatenate(parts, axis=1)
    cT = (y[:FOX_HEADS] + y[FOX_HEADS:2 * FOX_HEADS] + y[2 * FOX_HEADS:3 * FOX_HEADS]) * LOG2E
    ckx = _dot(ex_ref[...], stack3(cT))

    tail = ckx[:, P:P + LANES]
    rowq = lax.broadcasted_iota(jnp.int32, tail.shape, 0) % Tn
    lanek = lax.broadcasted_iota(jnp.int32, tail.shape, 1)
    cq = jnp.sum(jnp.where(lanek == rowq, tail, 0.0), axis=1, keepdims=True)

    q = q_ref[...]
    qt = jnp.concatenate([q] * FOX_HEADS, axis=0)
    rh = lax.broadcasted_iota(jnp.int32, qt.shape, 0) // Tn
    lh = lax.broadcasted_iota(jnp.int32, qt.shape, 1) // FOX_HEAD_DIM
    qbd = jnp.where(rh == lh, qt, jnp.zeros_like(qt))

    s = lax.dot_general(qbd, kx_ref[...], (((1,), (1,)), ((), ())), preferred_element_type=F32)
    s = s + cq - ckx
    key = lax.broadcasted_iota(jnp.int32, s.shape, 1)
    qpos = P + lax.broadcasted_iota(jnp.int32, s.shape, 0) % Tn
    s = jnp.where(key > qpos, NEG, s)
    m = jnp.max(s, axis=1, keepdims=True)
    p = jnp.exp2(s - m)
    l = jnp.sum(p, axis=1, keepdims=True)
    z = _dot(p.astype(BF16), vx_ref[...]) / l
    zh = lax.broadcasted_iota(jnp.int32, (Tn, FOX_W), 1) // FOX_HEAD_DIM
    o = jnp.zeros((Tn, FOX_W), F32)
    for hh in range(FOX_HEADS):
        o = o + jnp.where(zh == hh, z[hh * Tn:(hh + 1) * Tn, :], 0.0)
    o_ref[...] = o.astype(BF16)


def _fox_sample(q, kn, vn, cache_k, cache_v, lfT, B, Tn):
    P = cache_k.shape[1]
    KP = lfT.shape[2]
    HQ = FOX_HEADS * Tn
    up = jnp.asarray(np.triu(np.ones((LANES, LANES), np.float32)), BF16)
    ex = np.zeros((HQ, 4 * FOX_HEADS), np.float32)
    for part in range(3):
        for hh in range(FOX_HEADS):
            ex[hh * Tn:(hh + 1) * Tn, part * FOX_HEADS + hh] = 1.0
    ex = jnp.asarray(ex, BF16)
    rowb = lambda w: pl.BlockSpec((Tn, w), lambda b: (b, 0))
    return pl.pallas_call(
        _fox_sample_kernel,
        grid=(B,),
        in_specs=[rowb(FOX_W), rowb(FOX_W), rowb(FOX_W),
                  pl.BlockSpec((1, P, FOX_W), lambda b: (b, 0, 0)),
                  pl.BlockSpec((1, P, FOX_W), lambda b: (b, 0, 0)),
                  pl.BlockSpec((1, FOX_HEADS, KP), lambda b: (b, 0, 0)),
                  _const_spec(up.shape), _const_spec(ex.shape)],
        out_specs=rowb(FOX_W),
        out_shape=jax.ShapeDtypeStruct((B * Tn, FOX_W), BF16),
        scratch_shapes=[pltpu.VMEM((KP, FOX_W), BF16), pltpu.VMEM((KP, FOX_W), BF16)],
        compiler_params=pltpu.CompilerParams(dimension_semantics=("arbitrary",),
                                             vmem_limit_bytes=VMEM_LIMIT),
        name="fox_sample",
    )(q, kn, vn, cache_k, cache_v, lfT, up, ex)


def _retention_kernel(q_ref, k_ref, v_ref, s0_ref, dmat_ref, xi_ref, zeta_ref, gam_ref,
                      n_ref, sout_ref, st_ref):
    c = pl.program_id(1)

    @pl.when(c == 0)
    def _():
        st_ref[...] = s0_ref[0]

    for hh in range(RET_HEADS):
        q = q_ref[:, hh * RET_KEY_DIM:(hh + 1) * RET_KEY_DIM]
        k = k_ref[:, hh * RET_KEY_DIM:(hh + 1) * RET_KEY_DIM]
        v = v_ref[:, hh * RET_VAL_DIM:(hh + 1) * RET_VAL_DIM]
        st = st_ref[hh]
        sc = lax.dot_general(q, k, (((1,), (1,)), ((), ())), preferred_element_type=F32)
        sc = sc * dmat_ref[hh]
        o = _dot(sc.astype(BF16), v) + _dot(q, st.astype(BF16)) * xi_ref[hh]
        kz = (k.astype(F32) * zeta_ref[hh]).astype(BF16)
        upd = lax.dot_general(kz, v, (((0,), (0,)), ((), ())), preferred_element_type=F32)
        st_ref[hh] = gam_ref[hh] * st + upd
        mu = jnp.mean(o, axis=-1, keepdims=True)
        d = o - mu
        var = jnp.mean(d * d, axis=-1, keepdims=True)
        n_ref[:, hh * RET_VAL_DIM:(hh + 1) * RET_VAL_DIM] = (d * lax.rsqrt(var + EPS)).astype(BF16)

    @pl.when(c == pl.num_programs(1) - 1)
    def _():
        sout_ref[0] = st_ref[...]


def _ret_log_gamma():
    return jnp.log(1.0 - jnp.exp2(-5.0 - jnp.arange(RET_HEADS, dtype=F32)))


def _retention(q, k, v, state0, B, L, C):
    nc = L // C
    lg = _ret_log_gamma()
    idx = jnp.arange(C, dtype=F32)
    diff = idx[:, None] - idx[None, :]
    dmat = jnp.where(diff[None] >= 0, jnp.exp(jnp.maximum(diff, 0.0)[None] * lg[:, None, None]), 0.0)
    xi = jnp.exp((idx[None, :] + 1.0) * lg[:, None])
    zeta = jnp.exp((C - 1.0 - idx[None, :]) * lg[:, None])
    xi = jnp.broadcast_to(xi[:, :, None], (RET_HEADS, C, RET_VAL_DIM))
    zeta = jnp.broadcast_to(zeta[:, :, None], (RET_HEADS, C, RET_KEY_DIM))
    gam = jnp.broadcast_to(jnp.exp(C * lg)[:, None, None], (RET_HEADS, 1, RET_VAL_DIM))
    rowc = lambda w: pl.BlockSpec((C, w), lambda b, c: (b * nc + c, 0))
    st_spec = pl.BlockSpec((1, RET_HEADS, RET_KEY_DIM, RET_VAL_DIM), lambda b, c: (b, 0, 0, 0))
    return pl.pallas_call(
        _retention_kernel,
        grid=(B, nc),
        in_specs=[rowc(RET_KW), rowc(RET_KW), rowc(RET_VW), st_spec,
                  _const_spec(dmat.shape), _const_spec(xi.shape), _const_spec(zeta.shape),
                  _const_spec(gam.shape)],
        out_specs=(rowc(RET_VW), st_spec),
        out_shape=(jax.ShapeDtypeStruct((B * L, RET_VW), BF16),
                   jax.ShapeDtypeStruct((B, RET_HEADS, RET_KEY_DIM, RET_VAL_DIM), F32)),
        scratch_shapes=[pltpu.VMEM((RET_HEADS, RET_KEY_DIM, RET_VAL_DIM), F32)],
        compiler_params=pltpu.CompilerParams(dimension_semantics=("arbitrary", "arbitrary"),
                                             vmem_limit_bytes=VMEM_LIMIT),
        name="retention",
    )(q, k, v, state0, dmat, xi, zeta, gam)


def _mixer_ffn_kernel(x_ref, oa_ref, nb_ref, prev_ref, gmix_ref, wg_ref, gng_ref, wpa_ref, wpb_ref,
                      wo_ref, gffn_ref, wup_ref, cw_ref, cb_ref, wdn_ref, gfin_ref,
                      y_ref, conv_ref, carry_ref, ua_ref, ub_ref, acc_ref, h2_ref,
                      *, nseg, seglen):
    i = pl.program_id(0)
    NC = N_FFN_CHUNKS
    PADR = 8
    H0 = PADR - (CONV_WIDTH - 1)

    @pl.when(i == 0)
    def _():
        carry_ref[...] = prev_ref[...]

    x = x_ref[...]
    h = _rmsnorm(x, gmix_ref[...]).astype(BF16)
    zg = _dot(h, wg_ref[...])
    gb = zg[:, :RET_VW]
    gma = zg[:, RET_VW:RET_VW + D_MODEL]
    gmb = zg[:, RET_VW + D_MODEL:]
    oa = jnp.concatenate([oa_ref[p] for p in range(FOX_HEADS // 2)], axis=1)
    ya = _dot(oa, wpa_ref[...])
    nn = nb_ref[...].astype(F32) * gng_ref[...] * (gb * jax.nn.sigmoid(gb))
    yb = _dot(nn.astype(BF16), wpb_ref[...])
    y = jax.nn.sigmoid(gma) * ya + jax.nn.sigmoid(gmb) * yb
    x1 = x + _dot(y.astype(BF16), wo_ref[...])
    h2_ref[...] = _rmsnorm(x1, gffn_ref[...]).astype(BF16)
    acc_ref[...] = x1

    def conv_half(u, u_ref, cidx):
        w = cw_ref[cidx]
        b = cb_ref[cidx]
        outs = []
        for s in range(nseg):
            u_ref[s, PADR:PADR + seglen, :] = u[s * seglen:(s + 1) * seglen, :]
            u_ref[s, H0:PADR, :] = carry_ref[cidx, s, H0:PADR, :]
            acc = w[0:1] * u_ref[s, H0:H0 + seglen, :]
            for jj in range(1, CONV_WIDTH):
                acc = acc + w[jj:jj + 1] * u_ref[s, H0 + jj:H0 + jj + seglen, :]
            outs.append(b + acc)
            carry_ref[cidx, s, H0:PADR, :] = u_ref[s, H0 + seglen:PADR + seglen, :]
        return outs[0] if nseg == 1 else jnp.concatenate(outs, axis=0)

    def chunk(cidx, carry):
        h2 = h2_ref[...]
        a = conv_half(_dot(h2, wup_ref[cidx]), ua_ref, cidx)
        b = conv_half(_dot(h2, wup_ref[NC + cidx]), ub_ref, NC + cidx)
        g = (jax.nn.gelu(a) * b).astype(BF16)
        acc_ref[...] += _dot(g, wdn_ref[cidx])
        return carry

    lax.fori_loop(0, NC, chunk, 0)
    y_ref[...] = _rmsnorm(acc_ref[...], gfin_ref[...])

    @pl.when(i == pl.num_programs(0) - 1)
    def _():
        conv_ref[...] = carry_ref[:, :, H0:PADR, :]


def _mixer_ffn(x, oa, nb, prev, weights, tm, nseg, seglen):
    M = x.shape[0]
    gmix, wg, gng, wpa, wpb, wo, gffn, wup, cw, cb, wdn, gfin = weights
    FC = FFN_CHUNK
    NC = N_FFN_CHUNKS
    row = lambda w: pl.BlockSpec((tm, w), lambda i: (i, 0))
    wspec = lambda a: pl.BlockSpec(a.shape, lambda i, n=a.ndim: (0,) * n,
                                   pipeline_mode=pl.Buffered(1))
    in_specs = [row(D_MODEL),
                pl.BlockSpec((FOX_HEADS // 2, tm, LANES), lambda i: (0, i, 0)),
                row(RET_VW), wspec(prev)] + [wspec(w) for w in weights]
    out_shape = (jax.ShapeDtypeStruct((M, D_MODEL), F32),
                 jax.ShapeDtypeStruct((2 * NC, nseg, CONV_WIDTH - 1, FC), F32))
    out_specs = (row(D_MODEL), _const_spec(out_shape[1].shape))
    return pl.pallas_call(
        functools.partial(_mixer_ffn_kernel, nseg=nseg, seglen=seglen),
        grid=(M // tm,),
        in_specs=in_specs,
        out_specs=out_specs,
        out_shape=out_shape,
        scratch_shapes=[pltpu.VMEM((2 * NC, nseg, 8, FC), F32),
                        pltpu.VMEM((nseg, 8 + seglen, FC), F32),
                        pltpu.VMEM((nseg, 8 + seglen, FC), F32),
                        pltpu.VMEM((tm, D_MODEL), F32),
                        pltpu.VMEM((tm, D_MODEL), BF16)],
        compiler_params=pltpu.CompilerParams(dimension_semantics=("arbitrary",),
                                             vmem_limit_bytes=VMEM_LIMIT),
        name="mixer_ffn",
    )(x, oa, nb, prev, *weights)


def _rotary_tables(pos):
    half = RET_KEY_DIM // 2
    inv = 1.0 / (ROPE_BASE ** jnp.linspace(0.0, 1.0, half, dtype=F32))
    ang = pos.astype(F32)[:, None] * inv[None, :]
    cos = jnp.cos(ang)
    sin = jnp.sin(ang)
    return jnp.concatenate([cos, cos], axis=1), jnp.concatenate([-sin, sin], axis=1)


def _pad_heads(w, pad):
    d = w.shape[0]
    w = w.reshape(d, FOX_HEADS, FOX_HEAD_DIM)
    w = jnp.pad(w, ((0, 0), (0, 0), (0, pad - FOX_HEAD_DIM)))
    return w.reshape(d, FOX_HEADS * pad)


def _prompt_consts(tm):
    tri = np.tril(np.ones((tm, tm), np.float32))
    eq = np.zeros((3 * LANES, FOX_HEADS * HEAD_PAD), np.float32)
    ek = np.zeros((3 * LANES, FOX_HEADS * HEAD_PAD), np.float32)
    oneq = np.zeros((1, FOX_HEADS * HEAD_PAD), np.float32)
    onek = np.zeros((1, FOX_HEADS * HEAD_PAD), np.float32)
    onev = np.zeros((1, FOX_HEADS * V_PAD), np.float32)
    for hh in range(FOX_HEADS):
        base = hh * HEAD_PAD + BIAS_COL
        for part in range(3):
            eq[part * LANES + hh, base + part] = 1.0
            ek[part * LANES + hh, base + 3 + part] = -1.0
            onek[0, base + part] = 1.0
            oneq[0, base + 3 + part] = 1.0
        onev[0, hh * V_PAD + FOX_HEAD_DIM] = 1.0
    return (jnp.asarray(tri, BF16), jnp.asarray(eq, BF16), jnp.asarray(ek, BF16),
            jnp.asarray(oneq), jnp.asarray(onek), jnp.asarray(onev))


def _chunk_cols(a):
    lead = a.shape[:-1]
    a = a.reshape(lead + (2 * N_FFN_CHUNKS, FFN_CHUNK))
    return jnp.moveaxis(a, -2, 0)


def _tile(n, pref):
    t = min(n, pref)
    while n % t:
        t //= 2
    return t


def kernel(x_prompt, x_sample, cache_fox_k, cache_fox_v, cache_fox_logf, state_ret, state_ffn_conv,
           norm_mix_g, w_in, b_fox_f, gn_ret_g, w_pa, w_pb, w_o, norm_ffn_g, w_up, conv_w, conv_b,
           w_down, norm_final_g):
    depth = w_in.shape[0]
    Bp, S, _ = x_prompt.shape
    Bs, Ts, _ = x_sample.shape
    P = cache_fox_k.shape[2]
    assert depth == 1 and Bp == 1, "kernel handles the single-layer, single-prompt configuration"
    l = 0

    w = w_in[l]
    o0 = 3 * FOX_W
    o1 = o0 + FOX_HEADS
    o2 = o1 + 2 * RET_KW + RET_VW
    w_q, w_k, w_v = w[:, :FOX_W], w[:, FOX_W:2 * FOX_W], w[:, 2 * FOX_W:o0]
    wq_aug = _pad_heads(w_q, HEAD_PAD).astype(BF16)
    wk_aug = _pad_heads(w_k, HEAD_PAD).astype(BF16)
    wv_aug = _pad_heads(w_v, V_PAD).astype(BF16)
    wkv = w[:, FOX_W:o0].astype(BF16)
    wqkv = w[:, :o0].astype(BF16)
    wf = jnp.pad(w[:, o0:o1], ((0, 0), (0, LANES - FOX_HEADS))).astype(BF16)
    bf = jnp.pad(b_fox_f[l].astype(F32), (0, LANES - FOX_HEADS))[None, :]
    wb = w[:, o1:o2].astype(BF16)
    wg = w[:, o2:].astype(BF16)
    gmix = norm_mix_g[l].astype(F32)[None, :]
    mix_weights = (
        gmix, wg, gn_ret_g[l].astype(F32)[None, :], w_pa[l].astype(BF16), w_pb[l].astype(BF16),
        w_o[l].astype(BF16), norm_ffn_g[l].astype(F32)[None, :],
        _chunk_cols(w_up[l]).astype(BF16),
        jnp.pad(_chunk_cols(conv_w[l].astype(F32)), ((0, 0), (0, 8 - CONV_WIDTH), (0, 0))),
        _chunk_cols(conv_b[l].astype(F32)[None, :]),
        w_down[l].reshape(N_FFN_CHUNKS, FFN_CHUNK, D_MODEL).astype(BF16),
        norm_final_g.astype(F32)[None, :],
    )

    tm_a = _tile(S, 256)
    cos_p, sin_p = _rotary_tables(jnp.arange(S))
    (qT, ka, vT, k_p, v_p, logf_p, qb, kb, vb) = _inproj_prompt(
        x_prompt[0], gmix, wq_aug, wk_aug, wv_aug, wkv, wf, bf, wb, cos_p, sin_p,
        _prompt_consts(tm_a), tm_a)
    oa_p = _fox_prompt(qT, ka, vT, _tile(S, 1024))
    zero_state = jnp.zeros((1, RET_HEADS, RET_KEY_DIM, RET_VAL_DIM), F32)
    nb_p, ret_p = _retention(qb, kb, vb, zero_state, 1, S, _tile(S, 256))
    tm_d = _tile(S, 256)
    zero_prev = jnp.zeros((2 * N_FFN_CHUNKS, 1, 8, FFN_CHUNK), F32)
    y_p, conv_p = _mixer_ffn(x_prompt[0], oa_p, nb_p, zero_prev, mix_weights, tm_d, 1, tm_d)

    Ms = Bs * Ts
    cos_s, sin_s = _rotary_tables(P + jnp.arange(Ts))
    cos_s = jnp.tile(cos_s, (Bs, 1))
    sin_s = jnp.tile(sin_s, (Bs, 1))
    (q_s, k_s, v_s, logf_s, qb_s, kb_s, vb_s) = _inproj_sample(
        x_sample.reshape(Ms, D_MODEL), gmix, wqkv, wf, bf, wb, cos_s, sin_s)
    KP = ((P + Ts + LANES - 1) // LANES) * LANES
    lf_all = jnp.concatenate([cache_fox_logf[l].astype(F32), logf_s.reshape(Bs, Ts, FOX_HEADS)], axis=1)
    lfT = jnp.pad(jnp.swapaxes(lf_all, 1, 2), ((0, 0), (0, 0), (0, KP - P - Ts)))
    oa_s = _fox_sample(q_s, k_s, v_s, cache_fox_k[l].reshape(Bs, P, FOX_W),
                       cache_fox_v[l].reshape(Bs, P, FOX_W), lfT, Bs, Ts)
    oa_s = jnp.moveaxis(oa_s.reshape(Ms, FOX_HEADS // 2, LANES), 1, 0)
    nb_s, ret_s = _retention(qb_s, kb_s, vb_s, state_ret[l].astype(F32), Bs, Ts, Ts)
    prev_s = _chunk_cols(state_ffn_conv[l].astype(F32))
    prev_s = jnp.pad(prev_s, ((0, 0), (0, 0), (8 - (CONV_WIDTH - 1), 0), (0, 0)))
    y_s, conv_s = _mixer_ffn(x_sample.reshape(Ms, D_MODEL), oa_s, nb_s, prev_s, mix_weights,
                             Ms, Bs, Ts)

    def unchunk(cv):
        return jnp.moveaxis(cv, 0, 2).reshape(cv.shape[1], CONV_WIDTH - 1, 2 * FFN_DIM)

    hshape = (FOX_HEADS, FOX_HEAD_DIM)
    return (
        y_p[None],
        y_s.reshape(Bs, Ts, D_MODEL),
        k_p.reshape((1, 1, S) + hshape),
        v_p.reshape((1, 1, S) + hshape),
        logf_p.reshape(1, 1, S, FOX_HEADS),
        ret_p[None],
        unchunk(conv_p)[None],
        k_s.reshape((1, Bs, Ts) + hshape),
        v_s.reshape((1, Bs, Ts) + hshape),
        logf_s.reshape(1, Bs, Ts, FOX_HEADS),
        ret_s[None],
        unchunk(conv_s)[None],
    )
```

```python
import functools
import math

import numpy as np
import jax
import jax.numpy as jnp
from jax import lax
from jax.experimental import pallas as pl
from jax.experimental.pallas import tpu as pltpu

F32 = jnp.float32
BF16 = jnp.bfloat16

D_MODEL = 1024
FOX_HEADS = 8
FOX_HEAD_DIM = 64
RET_HEADS = 4
RET_KEY_DIM = 128
RET_VAL_DIM = 256
FFN_DIM = 2816
CONV_WIDTH = 3
EPS = 1e-6
ROPE_BASE = 10000.0

FOX_W = FOX_HEADS * FOX_HEAD_DIM
RET_KW = RET_HEADS * RET_KEY_DIM
RET_VW = RET_HEADS * RET_VAL_DIM

LOG2E = 1.4426950408889634
LANES = 128
HEAD_PAD = LANES
V_PAD = 80
BIAS_COL = FOX_HEAD_DIM
NEG = -1e30
STALE_MAX_GUARD = 2.0 ** 60
FFN_CHUNK = 256
N_FFN_CHUNKS = FFN_DIM // FFN_CHUNK
VMEM_LIMIT = 56 * 1024 * 1024


def _rmsnorm(x, g):
    ms = jnp.mean(x * x, axis=-1, keepdims=True)
    return x * lax.rsqrt(ms + EPS) * g


def _split3(x):
    hi = x.astype(BF16)
    r1 = x - hi.astype(F32)
    mid = r1.astype(BF16)
    lo = (r1 - mid.astype(F32)).astype(BF16)
    return hi, mid, lo


def _log_sigmoid(x):
    return jnp.minimum(x, 0.0) - jnp.log1p(jnp.exp(-jnp.abs(x)))


def _dot(a, b):
    return jnp.dot(a, b, preferred_element_type=F32)


def _rotary(x, cos2, sin2):
    return x * cos2 + pltpu.roll(x, RET_KEY_DIM // 2, 1) * sin2


def _const_spec(shape):
    n = len(shape)
    return pl.BlockSpec(shape, lambda *_: (0,) * n)


def _inproj_prompt_kernel(x_ref, g_ref, wq_ref, wk_ref, wv_ref, wkv_ref, wf_ref, bf_ref, wb_ref,
                          cos_ref, sin_ref, tri_ref, eq_ref, ek_ref, oneq_ref, onek_ref, onev_ref,
                          qT_ref, ka_ref, vT_ref, k32_ref, v32_ref, logf_ref, qb_ref, kb_ref, vb_ref,
                          carry_ref):
    tm = x_ref.shape[0]

    @pl.when(pl.program_id(0) == 0)
    def _():
        carry_ref[...] = jnp.zeros_like(carry_ref)

    h = _rmsnorm(x_ref[...], g_ref[...]).astype(BF16)

    logf = _log_sigmoid(_dot(h, wf_ref[...]) + bf_ref[...])
    logf_ref[...] = logf[:, :FOX_HEADS]
    lane = lax.broadcasted_iota(jnp.int32, logf.shape, 1)
    logf = jnp.where(lane < FOX_HEADS, logf, 0.0)
    r = _dot(tri_ref[...], jnp.concatenate(_split3(logf), axis=1))
    c = r[:, :LANES] + r[:, LANES:2 * LANES] + r[:, 2 * LANES:] + carry_ref[...]
    carry_ref[...] = c[tm - 1:tm, :]
    c3 = jnp.concatenate(_split3(c * LOG2E), axis=1)

    q_aug = (_dot(h, wq_ref[...]) * (FOX_HEAD_DIM ** -0.5 * LOG2E)
             + _dot(c3, eq_ref[...]) + oneq_ref[...])
    qT = q_aug.T.astype(BF16)
    k_aug = (_dot(h, wk_ref[...]) + _dot(c3, ek_ref[...]) + onek_ref[...]).astype(BF16)
    vT = (_dot(h, wv_ref[...]) + onev_ref[...]).T.astype(BF16)
    for hh in range(FOX_HEADS):
        qT_ref[hh] = qT[hh * HEAD_PAD:(hh + 1) * HEAD_PAD, :]
        ka_ref[hh] = k_aug[:, hh * HEAD_PAD:(hh + 1) * HEAD_PAD]
        vT_ref[hh] = vT[hh * V_PAD:(hh + 1) * V_PAD, :]

    zkv = _dot(h, wkv_ref[...])
    k32_ref[...] = zkv[:, :FOX_W]
    v32_ref[...] = zkv[:, FOX_W:]

    zb = _dot(h, wb_ref[...])
    cos2 = cos_ref[...]
    sin2 = sin_ref[...]
    for hh in range(RET_HEADS):
        sl = slice(hh * RET_KEY_DIM, (hh + 1) * RET_KEY_DIM)
        qb_ref[:, sl] = _rotary(zb[:, sl], cos2, sin2).astype(BF16)
        xk = zb[:, RET_KW + hh * RET_KEY_DIM:RET_KW + (hh + 1) * RET_KEY_DIM]
        kb_ref[:, sl] = (_rotary(xk, cos2, sin2) * (RET_KEY_DIM ** -0.5)).astype(BF16)
    vb_ref[...] = zb[:, 2 * RET_KW:].astype(BF16)


def _inproj_prompt(x, g, wq, wk, wv, wkv, wf, bf, wb, cos2, sin2, consts, tm):
    S = x.shape[0]
    tri, eq, ek, oneq, onek, onev = consts
    row = lambda w: pl.BlockSpec((tm, w), lambda i: (i, 0))
    in_specs = [row(D_MODEL), _const_spec(g.shape), _const_spec(wq.shape), _const_spec(wk.shape),
                _const_spec(wv.shape), _const_spec(wkv.shape), _const_spec(wf.shape),
                _const_spec(bf.shape), _const_spec(wb.shape), row(LANES), row(LANES),
                _const_spec(tri.shape), _const_spec(eq.shape), _const_spec(ek.shape),
                _const_spec(oneq.shape), _const_spec(onek.shape), _const_spec(onev.shape)]
    out_shape = (
        jax.ShapeDtypeStruct((FOX_HEADS, HEAD_PAD, S), BF16),
        jax.ShapeDtypeStruct((FOX_HEADS, S, HEAD_PAD), BF16),
        jax.ShapeDtypeStruct((FOX_HEADS, V_PAD, S), BF16),
        jax.ShapeDtypeStruct((S, FOX_W), F32),
        jax.ShapeDtypeStruct((S, FOX_W), F32),
        jax.ShapeDtypeStruct((S, FOX_HEADS), F32),
        jax.ShapeDtypeStruct((S, RET_KW), BF16),
        jax.ShapeDtypeStruct((S, RET_KW), BF16),
        jax.ShapeDtypeStruct((S, RET_VW), BF16),
    )
    out_specs = (
        pl.BlockSpec((FOX_HEADS, HEAD_PAD, tm), lambda i: (0, 0, i)),
        pl.BlockSpec((FOX_HEADS, tm, HEAD_PAD), lambda i: (0, i, 0)),
        pl.BlockSpec((FOX_HEADS, V_PAD, tm), lambda i: (0, 0, i)),
        row(FOX_W), row(FOX_W), row(FOX_HEADS), row(RET_KW), row(RET_KW), row(RET_VW),
    )
    return pl.pallas_call(
        _inproj_prompt_kernel,
        grid=(S // tm,),
        in_specs=in_specs,
        out_specs=out_specs,
        out_shape=out_shape,
        scratch_shapes=[pltpu.VMEM((1, LANES), F32)],
        compiler_params=pltpu.CompilerParams(dimension_semantics=("arbitrary",),
                                             vmem_limit_bytes=VMEM_LIMIT),
        name="inproj_prompt",
    )(x, g, wq, wk, wv, wkv, wf, bf, wb, cos2, sin2, tri, eq, ek, oneq, onek, onev)


def _inproj_sample_kernel(x_ref, g_ref, wqkv_ref, wf_ref, bf_ref, wb_ref, cos_ref, sin_ref,
                          q_ref, k32_ref, v32_ref, logf_ref, qb_ref, kb_ref, vb_ref):
    h = _rmsnorm(x_ref[...], g_ref[...]).astype(BF16)
    logf = _log_sigmoid(_dot(h, wf_ref[...]) + bf_ref[...])
    logf_ref[...] = logf[:, :FOX_HEADS]
    z = _dot(h, wqkv_ref[...])
    q_ref[...] = (z[:, :FOX_W] * (FOX_HEAD_DIM ** -0.5 * LOG2E)).astype(BF16)
    k32_ref[...] = z[:, FOX_W:2 * FOX_W]
    v32_ref[...] = z[:, 2 * FOX_W:]
    zb = _dot(h, wb_ref[...])
    cos2 = cos_ref[...]
    sin2 = sin_ref[...]
    for hh in range(RET_HEADS):
        sl = slice(hh * RET_KEY_DIM, (hh + 1) * RET_KEY_DIM)
        qb_ref[:, sl] = _rotary(zb[:, sl], cos2, sin2).astype(BF16)
        xk = zb[:, RET_KW + hh * RET_KEY_DIM:RET_KW + (hh + 1) * RET_KEY_DIM]
        kb_ref[:, sl] = (_rotary(xk, cos2, sin2) * (RET_KEY_DIM ** -0.5)).astype(BF16)
    vb_ref[...] = zb[:, 2 * RET_KW:].astype(BF16)


def _inproj_sample(x, g, wqkv, wf, bf, wb, cos2, sin2):
    M = x.shape[0]
    args = (x, g, wqkv, wf, bf, wb, cos2, sin2)
    out_shape = (
        jax.ShapeDtypeStruct((M, FOX_W), BF16),
        jax.ShapeDtypeStruct((M, FOX_W), F32),
        jax.ShapeDtypeStruct((M, FOX_W), F32),
        jax.ShapeDtypeStruct((M, FOX_HEADS), F32),
        jax.ShapeDtypeStruct((M, RET_KW), BF16),
        jax.ShapeDtypeStruct((M, RET_KW), BF16),
        jax.ShapeDtypeStruct((M, RET_VW), BF16),
    )
    return pl.pallas_call(
        _inproj_sample_kernel,
        grid=(1,),
        in_specs=[_const_spec(a.shape) for a in args],
        out_specs=tuple(_const_spec(o.shape) for o in out_shape),
        out_shape=out_shape,
        compiler_params=pltpu.CompilerParams(dimension_semantics=("arbitrary",),
                                             vmem_limit_bytes=VMEM_LIMIT),
        name="inproj_sample",
    )(*args)


def _fox_prompt_kernel(it_ref, jt_ref, qT_ref, ka_ref, vT_ref, o_ref, m_ref, acc_ref, pv_ref):
    t = pl.program_id(0)
    i = it_ref[t]
    j = jt_ref[t]
    T = qT_ref.shape[2]

    def heads(fn):
        def body(hh, carry):
            fn(hh)
            return carry
        lax.fori_loop(0, FOX_HEADS, body, 0)

    def scores(hh):
        return _dot(ka_ref[hh], qT_ref[hh])

    def exact_head(hh):
        s = scores(hh)
        kk = lax.broadcasted_iota(jnp.int32, s.shape, 0) + j * T
        qq = lax.broadcasted_iota(jnp.int32, s.shape, 1) + i * T
        s = jnp.where(kk > qq, NEG, s)
        m_old = m_ref[hh]
        m_new = jnp.maximum(m_old, jnp.max(s, axis=0, keepdims=True))
        p = jnp.exp2(s - m_new).astype(BF16)
        alpha = jnp.exp2(m_old - m_new)
        acc_ref[hh] = alpha * acc_ref[hh] + _dot(vT_ref[hh], p)
        m_ref[hh] = m_new

    def stale_head(hh):
        p = jnp.exp2(scores(hh) - m_ref[hh]).astype(BF16)
        pv_ref[hh] = _dot(vT_ref[hh], p)

    def stale_step():
        heads(stale_head)
        lsum = pv_ref[:, FOX_HEAD_DIM:FOX_HEAD_DIM + 1, :]
        bad = jnp.sum(jnp.where(lsum < STALE_MAX_GUARD, 0.0, 1.0)) > 0.0

        @pl.when(jnp.logical_not(bad))
        def _():
            acc_ref[...] += pv_ref[...]

        @pl.when(bad)
        def _():
            heads(exact_head)

    @pl.when(j == i)
    def _():
        m_ref[...] = jnp.full_like(m_ref, NEG)
        acc_ref[...] = jnp.zeros_like(acc_ref)
        heads(exact_head)

    @pl.when(j < i)
    def _():
        stale_step()

    @pl.when(j == 0)
    def _():
        for pr in range(FOX_HEADS // 2):
            halves = []
            for hh in (2 * pr, 2 * pr + 1):
                a = acc_ref[hh]
                halves.append(a[:FOX_HEAD_DIM] / a[FOX_HEAD_DIM:FOX_HEAD_DIM + 1])
            o_ref[pr] = jnp.concatenate(halves, axis=0).T.astype(BF16)


def _fox_prompt(qT, ka, vT, T):
    S = ka.shape[1]
    nb = S // T
    it = np.array([i for i in range(nb) for _ in range(i + 1)], np.int32)
    jt = np.array([j for i in range(nb) for j in range(i, -1, -1)], np.int32)
    grid_spec = pltpu.PrefetchScalarGridSpec(
        num_scalar_prefetch=2,
        grid=(len(it),),
        in_specs=[
            pl.BlockSpec((FOX_HEADS, HEAD_PAD, T), lambda t, it, jt: (0, 0, it[t])),
            pl.BlockSpec((FOX_HEADS, T, HEAD_PAD), lambda t, it, jt: (0, jt[t], 0)),
            pl.BlockSpec((FOX_HEADS, V_PAD, T), lambda t, it, jt: (0, 0, jt[t])),
        ],
        out_specs=pl.BlockSpec((FOX_HEADS // 2, T, LANES), lambda t, it, jt: (0, it[t], 0)),
        scratch_shapes=[pltpu.VMEM((FOX_HEADS, 1, T), F32),
                        pltpu.VMEM((FOX_HEADS, V_PAD, T), F32),
                        pltpu.VMEM((FOX_HEADS, V_PAD, T), F32)],
    )
    return pl.pallas_call(
        _fox_prompt_kernel,
        grid_spec=grid_spec,
        out_shape=jax.ShapeDtypeStruct((FOX_HEADS // 2, S, LANES), BF16),
        compiler_params=pltpu.CompilerParams(dimension_semantics=("arbitrary",),
                                             vmem_limit_bytes=VMEM_LIMIT),
        name="fox_prompt",
    )(jnp.asarray(it), jnp.asarray(jt), qT, ka, vT)


def _fox_sample_kernel(q_ref, kn_ref, vn_ref, ck_ref, cv_ref, lfT_ref, up_ref, ex_ref,
                       o_ref, kx_ref, vx_ref):
    P = ck_ref.shape[1]
    Tn = q_ref.shape[0]
    KP = kx_ref.shape[0]
    HQ = FOX_HEADS * Tn
    nchunk = KP // LANES

    kx_ref[:P, :] = ck_ref[0].astype(BF16)
    vx_ref[:P, :] = cv_ref[0].astype(BF16)
    kx_ref[P:P + Tn, :] = kn_ref[...].astype(BF16)
    vx_ref[P:P + Tn, :] = vn_ref[...].astype(BF16)
    kx_ref[P + Tn:, :] = jnp.zeros((KP - P - Tn, FOX_W), BF16)
    vx_ref[P + Tn:, :] = jnp.zeros((KP - P - Tn, FOX_W), BF16)

    def stack3(x):
        parts3 = [t.astype(F32) for t in _split3(x)] + [jnp.zeros_like(x)]
        return jnp.concatenate(parts3, axis=0).astype(BF16)

    x3 = stack3(lfT_ref[0])
    up = up_ref[...]
    run = jnp.zeros((4 * FOX_HEADS, 1), F32)
    parts = []
    for cidx in range(nchunk):
        y = _dot(x3[:, cidx * LANES:(cidx + 1) * LANES], up) + run
        parts.append(y)
        run = y[:, LANES - 1:LANES]
    y = jnp.concatenate(parts, axis=1)
    cT = (y[:FOX_HEADS] + y[FOX_HEADS:2 * FOX_HEADS] + y[2 * FOX_HEADS:3 * FOX_HEADS]) * LOG2E
    ckx = _dot(ex_ref[...], stack3(cT))

    tail = ckx[:, P:P + LANES]
    rowq = lax.broadcasted_iota(jnp.int32, tail.shape, 0) % Tn
    lanek = lax.broadcasted_iota(jnp.int32, tail.shape, 1)
    cq = jnp.sum(jnp.where(lanek == rowq, tail, 0.0), axis=1, keepdims=True)

    q = q_ref[...]
    qt = jnp.concatenate([q] * FOX_HEADS, axis=0)
    rh = lax.broadcasted_iota(jnp.int32, qt.shape, 0) // Tn
    lh = lax.broadcasted_iota(jnp.int32, qt.shape, 1) // FOX_HEAD_DIM
    qbd = jnp.where(rh == lh, qt, jnp.zeros_like(qt))

    s = lax.dot_general(qbd, kx_ref[...], (((1,), (1,)), ((), ())), preferred_element_type=F32)
    s = s + cq - ckx
    key = lax.broadcasted_iota(jnp.int32, s.shape, 1)
    qpos = P + lax.broadcasted_iota(jnp.int32, s.shape, 0) % Tn
    s = jnp.where(key > qpos, NEG, s)
    m = jnp.max(s, axis=1, keepdims=True)
    p = jnp.exp2(s - m)
    l = jnp.sum(p, axis=1, keepdims=True)
    z = _dot(p.astype(BF16), vx_ref[...]) / l
    zh = lax.broadcasted_iota(jnp.int32, (Tn, FOX_W), 1) // FOX_HEAD_DIM
    o = jnp.zeros((Tn, FOX_W), F32)
    for hh in range(FOX_HEADS):
        o = o + jnp.where(zh == hh, z[hh * Tn:(hh + 1) * Tn, :], 0.0)
    o_ref[...] = o.astype(BF16)


def _fox_sample(q, kn, vn, cache_k, cache_v, lfT, B, Tn):
    P = cache_k.shape[1]
    KP = lfT.shape[2]
    HQ = FOX_HEADS * Tn
    up = jnp.asarray(np.triu(np.ones((LANES, LANES), np.float32)), BF16)
    ex = np.zeros((HQ, 4 * FOX_HEADS), np.float32)
    for part in range(3):
        for hh in range(FOX_HEADS):
            ex[hh * Tn:(hh + 1) * Tn, part * FOX_HEADS + hh] = 1.0
    ex = jnp.asarray(ex, BF16)
    rowb = lambda w: pl.BlockSpec((Tn, w), lambda b: (b, 0))
    return pl.pallas_call(
        _fox_sample_kernel,
        grid=(B,),
        in_specs=[rowb(FOX_W), rowb(FOX_W), rowb(FOX_W),
                  pl.BlockSpec((1, P, FOX_W), lambda b: (b, 0, 0)),
                  pl.BlockSpec((1, P, FOX_W), lambda b: (b, 0, 0)),
                  pl.BlockSpec((1, FOX_HEADS, KP), lambda b: (b, 0, 0)),
                  _const_spec(up.shape), _const_spec(ex.shape)],
        out_specs=rowb(FOX_W),
        out_shape=jax.ShapeDtypeStruct((B * Tn, FOX_W), BF16),
        scratch_shapes=[pltpu.VMEM((KP, FOX_W), BF16), pltpu.VMEM((KP, FOX_W), BF16)],
        compiler_params=pltpu.CompilerParams(dimension_semantics=("arbitrary",),
                                             vmem_limit_bytes=VMEM_LIMIT),
        name="fox_sample",
    )(q, kn, vn, cache_k, cache_v, lfT, up, ex)


def _retention_kernel(q_ref, k_ref, v_ref, s0_ref, dmat_ref, xi_ref, zeta_ref, gam_ref,
                      n_ref, sout_ref, st_ref):
    c = pl.program_id(1)

    @pl.when(c == 0)
    def _():
        st_ref[...] = s0_ref[0]

    for hh in range(RET_HEADS):
        q = q_ref[:, hh * RET_KEY_DIM:(hh + 1) * RET_KEY_DIM]
        k = k_ref[:, hh * RET_KEY_DIM:(hh + 1) * RET_KEY_DIM]
        v = v_ref[:, hh * RET_VAL_DIM:(hh + 1) * RET_VAL_DIM]
        st = st_ref[hh]
        sc = lax.dot_general(q, k, (((1,), (1,)), ((), ())), preferred_element_type=F32)
        sc = sc * dmat_ref[hh]
        o = _dot(sc.astype(BF16), v) + _dot(q, st.astype(BF16)) * xi_ref[hh]
        kz = (k.astype(F32) * zeta_ref[hh]).astype(BF16)
        upd = lax.dot_general(kz, v, (((0,), (0,)), ((), ())), preferred_element_type=F32)
        st_ref[hh] = gam_ref[hh] * st + upd
        mu = jnp.mean(o, axis=-1, keepdims=True)
        d = o - mu
        var = jnp.mean(d * d, axis=-1, keepdims=True)
        n_ref[:, hh * RET_VAL_DIM:(hh + 1) * RET_VAL_DIM] = (d * lax.rsqrt(var + EPS)).astype(BF16)

    @pl.when(c == pl.num_programs(1) - 1)
    def _():
        sout_ref[0] = st_ref[...]


def _ret_log_gamma():
    return jnp.log(1.0 - jnp.exp2(-5.0 - jnp.arange(RET_HEADS, dtype=F32)))


def _retention(q, k, v, state0, B, L, C):
    nc = L // C
    lg = _ret_log_gamma()
    idx = jnp.arange(C, dtype=F32)
    diff = idx[:, None] - idx[None, :]
    dmat = jnp.where(diff[None] >= 0, jnp.exp(jnp.maximum(diff, 0.0)[None] * lg[:, None, None]), 0.0)
    xi = jnp.exp((idx[None, :] + 1.0) * lg[:, None])
    zeta = jnp.exp((C - 1.0 - idx[None, :]) * lg[:, None])
    xi = jnp.broadcast_to(xi[:, :, None], (RET_HEADS, C, RET_VAL_DIM))
    zeta = jnp.broadcast_to(zeta[:, :, None], (RET_HEADS, C, RET_KEY_DIM))
    gam = jnp.broadcast_to(jnp.exp(C * lg)[:, None, None], (RET_HEADS, 1, RET_VAL_DIM))
    rowc = lambda w: pl.BlockSpec((C, w), lambda b, c: (b * nc + c, 0))
    st_spec = pl.BlockSpec((1, RET_HEADS, RET_KEY_DIM, RET_VAL_DIM), lambda b, c: (b, 0, 0, 0))
    return pl.pallas_call(
        _retention_kernel,
        grid=(B, nc),
        in_specs=[rowc(RET_KW), rowc(RET_KW), rowc(RET_VW), st_spec,
                  _const_spec(dmat.shape), _const_spec(xi.shape), _const_spec(zeta.shape),
                  _const_spec(gam.shape)],
        out_specs=(rowc(RET_VW), st_spec),
        out_shape=(jax.ShapeDtypeStruct((B * L, RET_VW), BF16),
                   jax.ShapeDtypeStruct((B, RET_HEADS, RET_KEY_DIM, RET_VAL_DIM), F32)),
        scratch_shapes=[pltpu.VMEM((RET_HEADS, RET_KEY_DIM, RET_VAL_DIM), F32)],
        compiler_params=pltpu.CompilerParams(dimension_semantics=("arbitrary", "arbitrary"),
                                             vmem_limit_bytes=VMEM_LIMIT),
        name="retention",
    )(q, k, v, state0, dmat, xi, zeta, gam)


def _mixer_ffn_kernel(x_ref, oa_ref, nb_ref, prev_ref, gmix_ref, wg_ref, gng_ref, wpa_ref, wpb_ref,
                      wo_ref, gffn_ref, wup_ref, cw_ref, cb_ref, wdn_ref, gfin_ref,
                      y_ref, conv_ref, carry_ref, ua_ref, ub_ref, acc_ref, h2_ref,
                      *, nseg, seglen):
    i = pl.program_id(0)
    NC = N_FFN_CHUNKS
    PADR = 8
    H0 = PADR - (CONV_WIDTH - 1)

    @pl.when(i == 0)
    def _():
        carry_ref[...] = prev_ref[...]

    x = x_ref[...]
    h = _rmsnorm(x, gmix_ref[...]).astype(BF16)
    zg = _dot(h, wg_ref[...])
    gb = zg[:, :RET_VW]
    gma = zg[:, RET_VW:RET_VW + D_MODEL]
    gmb = zg[:, RET_VW + D_MODEL:]
    oa = jnp.concatenate([oa_ref[p] for p in range(FOX_HEADS // 2)], axis=1)
    ya = _dot(oa, wpa_ref[...])
    nn = nb_ref[...].astype(F32) * gng_ref[...] * (gb * jax.nn.sigmoid(gb))
    yb = _dot(nn.astype(BF16), wpb_ref[...])
    y = jax.nn.sigmoid(gma) * ya + jax.nn.sigmoid(gmb) * yb
    x1 = x + _dot(y.astype(BF16), wo_ref[...])
    h2_ref[...] = _rmsnorm(x1, gffn_ref[...]).astype(BF16)
    acc_ref[...] = x1

    def conv_half(u, u_ref, cidx):
        w = cw_ref[cidx]
        b = cb_ref[cidx]
        outs = []
        for s in range(nseg):
            u_ref[s, PADR:PADR + seglen, :] = u[s * seglen:(s + 1) * seglen, :]
            u_ref[s, H0:PADR, :] = carry_ref[cidx, s, H0:PADR, :]
            acc = w[0:1] * u_ref[s, H0:H0 + seglen, :]
            for jj in range(1, CONV_WIDTH):
                acc = acc + w[jj:jj + 1] * u_ref[s, H0 + jj:H0 + jj + seglen, :]
            outs.append(b + acc)
            carry_ref[cidx, s, H0:PADR, :] = u_ref[s, H0 + seglen:PADR + seglen, :]
        return outs[0] if nseg == 1 else jnp.concatenate(outs, axis=0)

    def chunk(cidx, carry):
        h2 = h2_ref[...]
        a = conv_half(_dot(h2, wup_ref[cidx]), ua_ref, cidx)
        b = conv_half(_dot(h2, wup_ref[NC + cidx]), ub_ref, NC + cidx)
        g = (jax.nn.gelu(a) * b).astype(BF16)
        acc_ref[...] += _dot(g, wdn_ref[cidx])
        return carry

    lax.fori_loop(0, NC, chunk, 0)
    y_ref[...] = _rmsnorm(acc_ref[...], gfin_ref[...])

    @pl.when(i == pl.num_programs(0) - 1)
    def _():
        conv_ref[...] = carry_ref[:, :, H0:PADR, :]


def _mixer_ffn(x, oa, nb, prev, weights, tm, nseg, seglen):
    M = x.shape[0]
    gmix, wg, gng, wpa, wpb, wo, gffn, wup, cw, cb, wdn, gfin = weights
    FC = FFN_CHUNK
    NC = N_FFN_CHUNKS
    row = lambda w: pl.BlockSpec((tm, w), lambda i: (i, 0))
    wspec = lambda a: pl.BlockSpec(a.shape, lambda i, n=a.ndim: (0,) * n,
                                   pipeline_mode=pl.Buffered(1))
    in_specs = [row(D_MODEL),
                pl.BlockSpec((FOX_HEADS // 2, tm, LANES), lambda i: (0, i, 0)),
                row(RET_VW), wspec(prev)] + [wspec(w) for w in weights]
    out_shape = (jax.ShapeDtypeStruct((M, D_MODEL), F32),
                 jax.ShapeDtypeStruct((2 * NC, nseg, CONV_WIDTH - 1, FC), F32))
    out_specs = (row(D_MODEL), _const_spec(out_shape[1].shape))
    return pl.pallas_call(
        functools.partial(_mixer_ffn_kernel, nseg=nseg, seglen=seglen),
        grid=(M // tm,),
        in_specs=in_specs,
        out_specs=out_specs,
        out_shape=out_shape,
        scratch_shapes=[pltpu.VMEM((2 * NC, nseg, 8, FC), F32),
                        pltpu.VMEM((nseg, 8 + seglen, FC), F32),
                        pltpu.VMEM((nseg, 8 + seglen, FC), F32),
                        pltpu.VMEM((tm, D_MODEL), F32),
                        pltpu.VMEM((tm, D_MODEL), BF16)],
        compiler_params=pltpu.CompilerParams(dimension_semantics=("arbitrary",),
                                             vmem_limit_bytes=VMEM_LIMIT),
        name="mixer_ffn",
    )(x, oa, nb, prev, *weights)


def _rotary_tables(pos):
    half = RET_KEY_DIM // 2
    inv = 1.0 / (ROPE_BASE ** jnp.linspace(0.0, 1.0, half, dtype=F32))
    ang = pos.astype(F32)[:, None] * inv[None, :]
    cos = jnp.cos(ang)
    sin = jnp.sin(ang)
    return jnp.concatenate([cos, cos], axis=1), jnp.concatenate([-sin, sin], axis=1)


def _pad_heads(w, pad):
    d = w.shape[0]
    w = w.reshape(d, FOX_HEADS, FOX_HEAD_DIM)
    w = jnp.pad(w, ((0, 0), (0, 0), (0, pad - FOX_HEAD_DIM)))
    return w.reshape(d, FOX_HEADS * pad)


def _prompt_consts(tm):
    tri = np.tril(np.ones((tm, tm), np.float32))
    eq = np.zeros((3 * LANES, FOX_HEADS * HEAD_PAD), np.float32)
    ek = np.zeros((3 * LANES, FOX_HEADS * HEAD_PAD), np.float32)
    oneq = np.zeros((1, FOX_HEADS * HEAD_PAD), np.float32)
    onek = np.zeros((1, FOX_HEADS * HEAD_PAD), np.float32)
    onev = np.zeros((1, FOX_HEADS * V_PAD), np.float32)
    for hh in range(FOX_HEADS):
        base = hh * HEAD_PAD + BIAS_COL
        for part in range(3):
            eq[part * LANES + hh, base + part] = 1.0
            ek[part * LANES + hh, base + 3 + part] = -1.0
            onek[0, base + part] = 1.0
            oneq[0, base + 3 + part] = 1.0
        onev[0, hh * V_PAD + FOX_HEAD_DIM] = 1.0
    return (jnp.asarray(tri, BF16), jnp.asarray(eq, BF16), jnp.asarray(ek, BF16),
            jnp.asarray(oneq), jnp.asarray(onek), jnp.asarray(onev))


def _chunk_cols(a):
    lead = a.shape[:-1]
    a = a.reshape(lead + (2 * N_FFN_CHUNKS, FFN_CHUNK))
    return jnp.moveaxis(a, -2, 0)


def _tile(n, pref):
    t = min(n, pref)
    while n % t:
        t //= 2
    return t


def kernel(x_prompt, x_sample, cache_fox_k, cache_fox_v, cache_fox_logf, state_ret, state_ffn_conv,
           norm_mix_g, w_in, b_fox_f, gn_ret_g, w_pa, w_pb, w_o, norm_ffn_g, w_up, conv_w, conv_b,
           w_down, norm_final_g):
    depth = w_in.shape[0]
    Bp, S, _ = x_prompt.shape
    Bs, Ts, _ = x_sample.shape
    P = cache_fox_k.shape[2]
    assert depth == 1 and Bp == 1, "kernel handles the single-layer, single-prompt configuration"
    l = 0

    w = w_in[l]
    o0 = 3 * FOX_W
    o1 = o0 + FOX_HEADS
    o2 = o1 + 2 * RET_KW + RET_VW
    w_q, w_k, w_v = w[:, :FOX_W], w[:, FOX_W:2 * FOX_W], w[:, 2 * FOX_W:o0]
    wq_aug = _pad_heads(w_q, HEAD_PAD).astype(BF16)
    wk_aug = _pad_heads(w_k, HEAD_PAD).astype(BF16)
    wv_aug = _pad_heads(w_v, V_PAD).astype(BF16)
    wkv = w[:, FOX_W:o0].astype(BF16)
    wqkv = w[:, :o0].astype(BF16)
    wf = jnp.pad(w[:, o0:o1], ((0, 0), (0, LANES - FOX_HEADS))).astype(BF16)
    bf = jnp.pad(b_fox_f[l].astype(F32), (0, LANES - FOX_HEADS))[None, :]
    wb = w[:, o1:o2].astype(BF16)
    wg = w[:, o2:].astype(BF16)
    gmix = norm_mix_g[l].astype(F32)[None, :]
    mix_weights = (
        gmix, wg, gn_ret_g[l].astype(F32)[None, :], w_pa[l].astype(BF16), w_pb[l].astype(BF16),
        w_o[l].astype(BF16), norm_ffn_g[l].astype(F32)[None, :],
        _chunk_cols(w_up[l]).astype(BF16),
        jnp.pad(_chunk_cols(conv_w[l].astype(F32)), ((0, 0), (0, 8 - CONV_WIDTH), (0, 0))),
        _chunk_cols(conv_b[l].astype(F32)[None, :]),
        w_down[l].reshape(N_FFN_CHUNKS, FFN_CHUNK, D_MODEL).astype(BF16),
        norm_final_g.astype(F32)[None, :],
    )

    tm_a = _tile(S, 256)
    cos_p, sin_p = _rotary_tables(jnp.arange(S))
    (qT, ka, vT, k_p, v_p, logf_p, qb, kb, vb) = _inproj_prompt(
        x_prompt[0], gmix, wq_aug, wk_aug, wv_aug, wkv, wf, bf, wb, cos_p, sin_p,
        _prompt_consts(tm_a), tm_a)
    oa_p = _fox_prompt(qT, ka, vT, _tile(S, 1024))
    zero_state = jnp.zeros((1, RET_HEADS, RET_KEY_DIM, RET_VAL_DIM), F32)
    nb_p, ret_p = _retention(qb, kb, vb, zero_state, 1, S, _tile(S, 256))
    tm_d = _tile(S, 256)
    zero_prev = jnp.zeros((2 * N_FFN_CHUNKS, 1, 8, FFN_CHUNK), F32)
    y_p, conv_p = _mixer_ffn(x_prompt[0], oa_p, nb_p, zero_prev, mix_weights, tm_d, 1, tm_d)

    Ms = Bs * Ts
    cos_s, sin_s = _rotary_tables(P + jnp.arange(Ts))
    cos_s = jnp.tile(cos_s, (Bs, 1))
    sin_s = jnp.tile(sin_s, (Bs, 1))
    (q_s, k_s, v_s, logf_s, qb_s, kb_s, vb_s) = _inproj_sample(
        x_sample.reshape(Ms, D_MODEL), gmix, wqkv, wf, bf, wb, cos_s, sin_s)
    KP = ((P + Ts + LANES - 1) // LANES) * LANES
    lf_all = jnp.concatenate([cache_fox_logf[l].astype(F32), logf_s.reshape(Bs, Ts, FOX_HEADS)], axis=1)
    lfT = jnp.pad(jnp.swapaxes(lf_all, 1, 2), ((0, 0), (0, 0), (0, KP - P - Ts)))
    oa_s = _fox_sample(q_s, k_s, v_s, cache_fox_k[l].reshape(Bs, P, FOX_W),
                       cache_fox_v[l].reshape(Bs, P, FOX_W), lfT, Bs, Ts)
    oa_s = jnp.moveaxis(oa_s.reshape(Ms, FOX_HEADS // 2, LANES), 1, 0)
    nb_s, ret_s = _retention(qb_s, kb_s, vb_s, state_ret[l].astype(F32), Bs, Ts, Ts)
    prev_s = _chunk_cols(state_ffn_conv[l].astype(F32))
    prev_s = jnp.pad(prev_s, ((0, 0), (0, 0), (8 - (CONV_WIDTH - 1), 0), (0, 0)))
    y_s, conv_s = _mixer_ffn(x_sample.reshape(Ms, D_MODEL), oa_s, nb_s, prev_s, mix_weights,
                             Ms, Bs, Ts)

    def unchunk(cv):
        return jnp.moveaxis(cv, 0, 2).reshape(cv.shape[1], CONV_WIDTH - 1, 2 * FFN_DIM)

    hshape = (FOX_HEADS, FOX_HEAD_DIM)
    return (
        y_p[None],
        y_s.reshape(Bs, Ts, D_MODEL),
        k_p.reshape((1, 1, S) + hshape),
        v_p.reshape((1, 1, S) + hshape),
        logf_p.reshape(1, 1, S, FOX_HEADS),
        ret_p[None],
        unchunk(conv_p)[None],
        k_s.reshape((1, Bs, Ts) + hshape),
        v_s.reshape((1, Bs, Ts) + hshape),
        logf_s.reshape(1, Bs, Ts, FOX_HEADS),
        ret_s[None],
        unchunk(conv_s)[None],
    )
```

```python
import functools
import math

import numpy as np
import jax
import jax.numpy as jnp
from jax import lax
from jax.experimental import pallas as pl
from jax.experimental.pallas import tpu as pltpu

F32 = jnp.float32
BF16 = jnp.bfloat16

D_MODEL = 1024
FOX_HEADS = 8
FOX_HEAD_DIM = 64
RET_HEADS = 4
RET_KEY_DIM = 128
RET_VAL_DIM = 256
FFN_DIM = 2816
CONV_WIDTH = 3
EPS = 1e-6
ROPE_BASE = 10000.0

FOX_W = FOX_HEADS * FOX_HEAD_DIM
RET_KW = RET_HEADS * RET_KEY_DIM
RET_VW = RET_HEADS * RET_VAL_DIM

LOG2E = 1.4426950408889634
LANES = 128
HEAD_PAD = LANES
V_PAD = 80
BIAS_COL = FOX_HEAD_DIM
NEG = -1e30
STALE_MAX_GUARD = 2.0 ** 60
FFN_CHUNK = 256
N_FFN_CHUNKS = FFN_DIM // FFN_CHUNK
VMEM_LIMIT = 56 * 1024 * 1024


def _rmsnorm(x, g):
    ms = jnp.mean(x * x, axis=-1, keepdims=True)
    return x * lax.rsqrt(ms + EPS) * g


def _split3(x):
    hi = x.astype(BF16)
    r1 = x - hi.astype(F32)
    mid = r1.astype(BF16)
    lo = (r1 - mid.astype(F32)).astype(BF16)
    return hi, mid, lo


def _log_sigmoid(x):
    return jnp.minimum(x, 0.0) - jnp.log1p(jnp.exp(-jnp.abs(x)))


def _dot(a, b):
    return jnp.dot(a, b, preferred_element_type=F32)


def _rotary(x, cos2, sin2):
    return x * cos2 + pltpu.roll(x, RET_KEY_DIM // 2, 1) * sin2


def _const_spec(shape):
    n = len(shape)
    return pl.BlockSpec(shape, lambda *_: (0,) * n)


def _inproj_prompt_kernel(x_ref, g_ref, wq_ref, wk_ref, wv_ref, wkv_ref, wf_ref, bf_ref, wb_ref,
                          cos_ref, sin_ref, tri_ref, eq_ref, ek_ref, oneq_ref, onek_ref, onev_ref,
                          qT_ref, ka_ref, vT_ref, k32_ref, v32_ref, logf_ref, qb_ref, kb_ref, vb_ref,
                          carry_ref):
    tm = x_ref.shape[0]

    @pl.when(pl.program_id(0) == 0)
    def _():
        carry_ref[...] = jnp.zeros_like(carry_ref)

    h = _rmsnorm(x_ref[...], g_ref[...]).astype(BF16)

    logf = _log_sigmoid(_dot(h, wf_ref[...]) + bf_ref[...])
    logf_ref[...] = logf[:, :FOX_HEADS]
    lane = lax.broadcasted_iota(jnp.int32, logf.shape, 1)
    logf = jnp.where(lane < FOX_HEADS, logf, 0.0)
    r = _dot(tri_ref[...], jnp.concatenate(_split3(logf), axis=1))
    c = r[:, :LANES] + r[:, LANES:2 * LANES] + r[:, 2 * LANES:] + carry_ref[...]
    carry_ref[...] = c[tm - 1:tm, :]
    c3 = jnp.concatenate(_split3(c * LOG2E), axis=1)

    q_aug = (_dot(h, wq_ref[...]) * (FOX_HEAD_DIM ** -0.5 * LOG2E)
             + _dot(c3, eq_ref[...]) + oneq_ref[...])
    qT = q_aug.T.astype(BF16)
    k_aug = (_dot(h, wk_ref[...]) + _dot(c3, ek_ref[...]) + onek_ref[...]).astype(BF16)
    vT = (_dot(h, wv_ref[...]) + onev_ref[...]).T.astype(BF16)
    for hh in range(FOX_HEADS):
        qT_ref[hh] = qT[hh * HEAD_PAD:(hh + 1) * HEAD_PAD, :]
        ka_ref[hh] = k_aug[:, hh * HEAD_PAD:(hh + 1) * HEAD_PAD]
        vT_ref[hh] = vT[hh * V_PAD:(hh + 1) * V_PAD, :]

    zkv = _dot(h, wkv_ref[...])
    k32_ref[...] = zkv[:, :FOX_W]
    v32_ref[...] = zkv[:, FOX_W:]

    zb = _dot(h, wb_ref[...])
    cos2 = cos_ref[...]
    sin2 = sin_ref[...]
    for hh in range(RET_HEADS):
        sl = slice(hh * RET_KEY_DIM, (hh + 1) * RET_KEY_DIM)
        qb_ref[:, sl] = _rotary(zb[:, sl], cos2, sin2).astype(BF16)
        xk = zb[:, RET_KW + hh * RET_KEY_DIM:RET_KW + (hh + 1) * RET_KEY_DIM]
        kb_ref[:, sl] = (_rotary(xk, cos2, sin2) * (RET_KEY_DIM ** -0.5)).astype(BF16)
    vb_ref[...] = zb[:, 2 * RET_KW:].astype(BF16)


def _inproj_prompt(x, g, wq, wk, wv, wkv, wf, bf, wb, cos2, sin2, consts, tm):
    S = x.shape[0]
    tri, eq, ek, oneq, onek, onev = consts
    row = lambda w: pl.BlockSpec((tm, w), lambda i: (i, 0))
    in_specs = [row(D_MODEL), _const_spec(g.shape), _const_spec(wq.shape), _const_spec(wk.shape),
                _const_spec(wv.shape), _const_spec(wkv.shape), _const_spec(wf.shape),
                _const_spec(bf.shape), _const_spec(wb.shape), row(LANES), row(LANES),
                _const_spec(tri.shape), _const_spec(eq.shape), _const_spec(ek.shape),
                _const_spec(oneq.shape), _const_spec(onek.shape), _const_spec(onev.shape)]
    out_shape = (
        jax.ShapeDtypeStruct((FOX_HEADS, HEAD_PAD, S), BF16),
        jax.ShapeDtypeStruct((FOX_HEADS, S, HEAD_PAD), BF16),
        jax.ShapeDtypeStruct((FOX_HEADS, V_PAD, S), BF16),
        jax.ShapeDtypeStruct((S, FOX_W), F32),
        jax.ShapeDtypeStruct((S, FOX_W), F32),
        jax.ShapeDtypeStruct((S, FOX_HEADS), F32),
        jax.ShapeDtypeStruct((S, RET_KW), BF16),
        jax.ShapeDtypeStruct((S, RET_KW), BF16),
        jax.ShapeDtypeStruct((S, RET_VW), BF16),
    )
    out_specs = (
        pl.BlockSpec((FOX_HEADS, HEAD_PAD, tm), lambda i: (0, 0, i)),
        pl.BlockSpec((FOX_HEADS, tm, HEAD_PAD), lambda i: (0, i, 0)),
        pl.BlockSpec((FOX_HEADS, V_PAD, tm), lambda i: (0, 0, i)),
        row(FOX_W), row(FOX_W), row(FOX_HEADS), row(RET_KW), row(RET_KW), row(RET_VW),
    )
    return pl.pallas_call(
        _inproj_prompt_kernel,
        grid=(S // tm,),
        in_specs=in_specs,
        out_specs=out_specs,
        out_shape=out_shape,
        scratch_shapes=[pltpu.VMEM((1, LANES), F32)],
        compiler_params=pltpu.CompilerParams(dimension_semantics=("arbitrary",),
                                             vmem_limit_bytes=VMEM_LIMIT),
        name="inproj_prompt",
    )(x, g, wq, wk, wv, wkv, wf, bf, wb, cos2, sin2, tri, eq, ek, oneq, onek, onev)


def _inproj_sample_kernel(x_ref, g_ref, wqkv_ref, wf_ref, bf_ref, wb_ref, cos_ref, sin_ref,
                          q_ref, k32_ref, v32_ref, logf_ref, qb_ref, kb_ref, vb_ref):
    h = _rmsnorm(x_ref[...], g_ref[...]).astype(BF16)
    logf = _log_sigmoid(_dot(h, wf_ref[...]) + bf_ref[...])
    logf_ref[...] = logf[:, :FOX_HEADS]
    z = _dot(h, wqkv_ref[...])
    q_ref[...] = (z[:, :FOX_W] * (FOX_HEAD_DIM ** -0.5 * LOG2E)).astype(BF16)
    k32_ref[...] = z[:, FOX_W:2 * FOX_W]
    v32_ref[...] = z[:, 2 * FOX_W:]
    zb = _dot(h, wb_ref[...])
    cos2 = cos_ref[...]
    sin2 = sin_ref[...]
    for hh in range(RET_HEADS):
        sl = slice(hh * RET_KEY_DIM, (hh + 1) * RET_KEY_DIM)
        qb_ref[:, sl] = _rotary(zb[:, sl], cos2, sin2).astype(BF16)
        xk = zb[:, RET_KW + hh * RET_KEY_DIM:RET_KW + (hh + 1) * RET_KEY_DIM]
        kb_ref[:, sl] = (_rotary(xk, cos2, sin2) * (RET_KEY_DIM ** -0.5)).astype(BF16)
    vb_ref[...] = zb[:, 2 * RET_KW:].astype(BF16)


def _inproj_sample(x, g, wqkv, wf, bf, wb, cos2, sin2):
    M = x.shape[0]
    args = (x, g, wqkv, wf, bf, wb, cos2, sin2)
    out_shape = (
        jax.ShapeDtypeStruct((M, FOX_W), BF16),
        jax.ShapeDtypeStruct((M, FOX_W), F32),
        jax.ShapeDtypeStruct((M, FOX_W), F32),
        jax.ShapeDtypeStruct((M, FOX_HEADS), F32),
        jax.ShapeDtypeStruct((M, RET_KW), BF16),
        jax.ShapeDtypeStruct((M, RET_KW), BF16),
        jax.ShapeDtypeStruct((M, RET_VW), BF16),
    )
    return pl.pallas_call(
        _inproj_sample_kernel,
        grid=(1,),
        in_specs=[_const_spec(a.shape) for a in args],
        out_specs=tuple(_const_spec(o.shape) for o in out_shape),
        out_shape=out_shape,
        compiler_params=pltpu.CompilerParams(dimension_semantics=("arbitrary",),
                                             vmem_limit_bytes=VMEM_LIMIT),
        name="inproj_sample",
    )(*args)


def _fox_prompt_kernel(it_ref, jt_ref, qT_ref, ka_ref, vT_ref, o_ref, m_ref, acc_ref, pv_ref):
    t = pl.program_id(0)
    i = it_ref[t]
    j = jt_ref[t]
    T = qT_ref.shape[2]

    def heads(fn):
        def body(hh, carry):
            fn(hh)
            return carry
        lax.fori_loop(0, FOX_HEADS, body, 0)

    def scores(hh):
        return _dot(ka_ref[hh], qT_ref[hh])

    def exact_head(hh):
        s = scores(hh)
        kk = lax.broadcasted_iota(jnp.int32, s.shape, 0) + j * T
        qq = lax.broadcasted_iota(jnp.int32, s.shape, 1) + i * T
        s = jnp.where(kk > qq, NEG, s)
        m_old = m_ref[hh]
        m_new = jnp.maximum(m_old, jnp.max(s, axis=0, keepdims=True))
        p = jnp.exp2(s - m_new).astype(BF16)
        alpha = jnp.exp2(m_old - m_new)
        acc_ref[hh] = alpha * acc_ref[hh] + _dot(vT_ref[hh], p)
        m_ref[hh] = m_new

    def stale_head(hh):
        p = jnp.exp2(scores(hh) - m_ref[hh]).astype(BF16)
        pv_ref[hh] = _dot(vT_ref[hh], p)

    def stale_step():
        heads(stale_head)
        lsum = pv_ref[:, FOX_HEAD_DIM:FOX_HEAD_DIM + 1, :]
        bad = jnp.sum(jnp.where(lsum < STALE_MAX_GUARD, 0.0, 1.0)) > 0.0

        @pl.when(jnp.logical_not(bad))
        def _():
            acc_ref[...] += pv_ref[...]

        @pl.when(bad)
        def _():
            heads(exact_head)

    @pl.when(j == i)
    def _():
        m_ref[...] = jnp.full_like(m_ref, NEG)
        acc_ref[...] = jnp.zeros_like(acc_ref)
        heads(exact_head)

    @pl.when(j < i)
    def _():
        stale_step()

    @pl.when(j == 0)
    def _():
        for pr in range(FOX_HEADS // 2):
            halves = []
            for hh in (2 * pr, 2 * pr + 1):
                a = acc_ref[hh]
                halves.append(a[:FOX_HEAD_DIM] / a[FOX_HEAD_DIM:FOX_HEAD_DIM + 1])
            o_ref[pr] = jnp.concatenate(halves, axis=0).T.astype(BF16)


def _fox_prompt(qT, ka, vT, T):
    S = ka.shape[1]
    nb = S // T
    it = np.array([i for i in range(nb) for _ in range(i + 1)], np.int32)
    jt = np.array([j for i in range(nb) for j in range(i, -1, -1)], np.int32)
    grid_spec = pltpu.PrefetchScalarGridSpec(
        num_scalar_prefetch=2,
        grid=(len(it),),
        in_specs=[
            pl.BlockSpec((FOX_HEADS, HEAD_PAD, T), lambda t, it, jt: (0, 0, it[t])),
            pl.BlockSpec((FOX_HEADS, T, HEAD_PAD), lambda t, it, jt: (0, jt[t], 0)),
            pl.BlockSpec((FOX_HEADS, V_PAD, T), lambda t, it, jt: (0, 0, jt[t])),
        ],
        out_specs=pl.BlockSpec((FOX_HEADS // 2, T, LANES), lambda t, it, jt: (0, it[t], 0)),
        scratch_shapes=[pltpu.VMEM((FOX_HEADS, 1, T), F32),
                        pltpu.VMEM((FOX_HEADS, V_PAD, T), F32),
                        pltpu.VMEM((FOX_HEADS, V_PAD, T), F32)],
    )
    return pl.pallas_call(
        _fox_prompt_kernel,
        grid_spec=grid_spec,
        out_shape=jax.ShapeDtypeStruct((FOX_HEADS // 2, S, LANES), BF16),
        compiler_params=pltpu.CompilerParams(dimension_semantics=("arbitrary",),
                                             vmem_limit_bytes=VMEM_LIMIT),
        name="fox_prompt",
    )(jnp.asarray(it), jnp.asarray(jt), qT, ka, vT)


def _fox_sample_kernel(q_ref, kn_ref, vn_ref, ck_ref, cv_ref, lfT_ref, up_ref, ex_ref,
                       o_ref, kx_ref, vx_ref):
    P = ck_ref.shape[1]
    Tn = q_ref.shape[0]
    KP = kx_ref.shape[0]
    HQ = FOX_HEADS * Tn
    nchunk = KP // LANES

    kx_ref[:P, :] = ck_ref[0].astype(BF16)
    vx_ref[:P, :] = cv_ref[0].astype(BF16)
    kx_ref[P:P + Tn, :] = kn_ref[...].astype(BF16)
    vx_ref[P:P + Tn, :] = vn_ref[...].astype(BF16)
    kx_ref[P + Tn:, :] = jnp.zeros((KP - P - Tn, FOX_W), BF16)
    vx_ref[P + Tn:, :] = jnp.zeros((KP - P - Tn, FOX_W), BF16)

    def stack3(x):
        parts3 = [t.astype(F32) for t in _split3(x)] + [jnp.zeros_like(x)]
        return jnp.concatenate(parts3, axis=0).astype(BF16)

    x3 = stack3(lfT_ref[0])
    up = up_ref[...]
    run = jnp.zeros((4 * FOX_HEADS, 1), F32)
    parts = []
    for cidx in range(nchunk):
        y = _dot(x3[:, cidx * LANES:(cidx + 1) * LANES], up) + run
        parts.append(y)
        run = y[:, LANES - 1:LANES]
    y = jnp.concatenate(parts, axis=1)
    cT = (y[:FOX_HEADS] + y[FOX_HEADS:2 * FOX_HEADS] + y[2 * FOX_HEADS:3 * FOX_HEADS]) * LOG2E
    ckx = _dot(ex_ref[...], stack3(cT))

    tail = ckx[:, P:P + LANES]
    rowq = lax.broadcasted_iota(jnp.int32, tail.shape, 0) % Tn
    lanek = lax.broadcasted_iota(jnp.int32, tail.shape, 1)
    cq = jnp.sum(jnp.where(lanek == rowq, tail, 0.0), axis=1, keepdims=True)

    q = q_ref[...]
    qt = jnp.concatenate([q] * FOX_HEADS, axis=0)
    rh = lax.broadcasted_iota(jnp.int32, qt.shape, 0) // Tn
    lh = lax.broadcasted_iota(jnp.int32, qt.shape, 1) // FOX_HEAD_DIM
    qbd = jnp.where(rh == lh, qt, jnp.zeros_like(qt))

    s = lax.dot_general(qbd, kx_ref[...], (((1,), (1,)), ((), ())), preferred_element_type=F32)
    s = s + cq - ckx
    key = lax.broadcasted_iota(jnp.int32, s.shape, 1)
    qpos = P + lax.broadcasted_iota(jnp.int32, s.shape, 0) % Tn
    s = jnp.where(key > qpos, NEG, s)
    m = jnp.max(s, axis=1, keepdims=True)
    p = jnp.exp2(s - m)
    l = jnp.sum(p, axis=1, keepdims=True)
    z = _dot(p.astype(BF16), vx_ref[...]) / l
    zh = lax.broadcasted_iota(jnp.int32, (Tn, FOX_W), 1) // FOX_HEAD_DIM
    o = jnp.zeros((Tn, FOX_W), F32)
    for hh in range(FOX_HEADS):
        o = o + jnp.where(zh == hh, z[hh * Tn:(hh + 1) * Tn, :], 0.0)
    o_ref[...] = o.astype(BF16)


def _fox_sample(q, kn, vn, cache_k, cache_v, lfT, B, Tn):
    P = cache_k.shape[1]
    KP = lfT.shape[2]
    HQ = FOX_HEADS * Tn
    up = jnp.asarray(np.triu(np.ones((LANES, LANES), np.float32)), BF16)
    ex = np.zeros((HQ, 4 * FOX_HEADS), np.float32)
    for part in range(3):
        for hh in range(FOX_HEADS):
            ex[hh * Tn:(hh + 1) * Tn, part * FOX_HEADS + hh] = 1.0
    ex = jnp.asarray(ex, BF16)
    rowb = lambda w: pl.BlockSpec((Tn, w), lambda b: (b, 0))
    return pl.pallas_call(
        _fox_sample_kernel,
        grid=(B,),
        in_specs=[rowb(FOX_W), rowb(FOX_W), rowb(FOX_W),
                  pl.BlockSpec((1, P, FOX_W), lambda b: (b, 0, 0)),
                  pl.BlockSpec((1, P, FOX_W), lambda b: (b, 0, 0)),
                  pl.BlockSpec((1, FOX_HEADS, KP), lambda b: (b, 0, 0)),
                  _const_spec(up.shape), _const_spec(ex.shape)],
        out_specs=rowb(FOX_W),
        out_shape=jax.ShapeDtypeStruct((B * Tn, FOX_W), BF16),
        scratch_shapes=[pltpu.VMEM((KP, FOX_W), BF16), pltpu.VMEM((KP, FOX_W), BF16)],
        compiler_params=pltpu.CompilerParams(dimension_semantics=("arbitrary",),
                                             vmem_limit_bytes=VMEM_LIMIT),
        name="fox_sample",
    )(q, kn, vn, cache_k, cache_v, lfT, up, ex)


def _retention_kernel(q_ref, k_ref, v_ref, s0_ref, dmat_ref, xi_ref, zeta_ref, gam_ref,
                      n_ref, sout_ref, st_ref):
    c = pl.program_id(1)

    @pl.when(c == 0)
    def _():
        st_ref[...] = s0_ref[0]

    for hh in range(RET_HEADS):
        q = q_ref[:, hh * RET_KEY_DIM:(hh + 1) * RET_KEY_DIM]
        k = k_ref[:, hh * RET_KEY_DIM:(hh + 1) * RET_KEY_DIM]
        v = v_ref[:, hh * RET_VAL_DIM:(hh + 1) * RET_VAL_DIM]
        st = st_ref[hh]
        sc = lax.dot_general(q, k, (((1,), (1,)), ((), ())), preferred_element_type=F32)
        sc = sc * dmat_ref[hh]
        o = _dot(sc.astype(BF16), v) + _dot(q, st.astype(BF16)) * xi_ref[hh]
        kz = (k.astype(F32) * zeta_ref[hh]).astype(BF16)
        upd = lax.dot_general(kz, v, (((0,), (0,)), ((), ())), preferred_element_type=F32)
        st_ref[hh] = gam_ref[hh] * st + upd
        mu = jnp.mean(o, axis=-1, keepdims=True)
        d = o - mu
        var = jnp.mean(d * d, axis=-1, keepdims=True)
        n_ref[:, hh * RET_VAL_DIM:(hh + 1) * RET_VAL_DIM] = (d * lax.rsqrt(var + EPS)).astype(BF16)

    @pl.when(c == pl.num_programs(1) - 1)
    def _():
        sout_ref[0] = st_ref[...]


def _ret_log_gamma():
    return jnp.log(1.0 - jnp.exp2(-5.0 - jnp.arange(RET_HEADS, dtype=F32)))


def _retention(q, k, v, state0, B, L, C):
    nc = L // C
    lg = _ret_log_gamma()
    idx = jnp.arange(C, dtype=F32)
    diff = idx[:, None] - idx[None, :]
    dmat = jnp.where(diff[None] >= 0, jnp.exp(jnp.maximum(diff, 0.0)[None] * lg[:, None, None]), 0.0)
    xi = jnp.exp((idx[None, :] + 1.0) * lg[:, None])
    zeta = jnp.exp((C - 1.0 - idx[None, :]) * lg[:, None])
    xi = jnp.broadcast_to(xi[:, :, None], (RET_HEADS, C, RET_VAL_DIM))
    zeta = jnp.broadcast_to(zeta[:, :, None], (RET_HEADS, C, RET_KEY_DIM))
    gam = jnp.broadcast_to(jnp.exp(C * lg)[:, None, None], (RET_HEADS, 1, RET_VAL_DIM))
    rowc = lambda w: pl.BlockSpec((C, w), lambda b, c: (b * nc + c, 0))
    st_spec = pl.BlockSpec((1, RET_HEADS, RET_KEY_DIM, RET_VAL_DIM), lambda b, c: (b, 0, 0, 0))
    return pl.pallas_call(
        _retention_kernel,
        grid=(B, nc),
        in_specs=[rowc(RET_KW), rowc(RET_KW), rowc(RET_VW), st_spec,
                  _const_spec(dmat.shape), _const_spec(xi.shape), _const_spec(zeta.shape),
                  _const_spec(gam.shape)],
        out_specs=(rowc(RET_VW), st_spec),
        out_shape=(jax.ShapeDtypeStruct((B * L, RET_VW), BF16),
                   jax.ShapeDtypeStruct((B, RET_HEADS, RET_KEY_DIM, RET_VAL_DIM), F32)),
        scratch_shapes=[pltpu.VMEM((RET_HEADS, RET_KEY_DIM, RET_VAL_DIM), F32)],
        compiler_params=pltpu.CompilerParams(dimension_semantics=("arbitrary", "arbitrary"),
                                             vmem_limit_bytes=VMEM_LIMIT),
        name="retention",
    )(q, k, v, state0, dmat, xi, zeta, gam)


def _mixer_ffn_kernel(x_ref, oa_ref, nb_ref, prev_ref, gmix_ref, wg_ref, gng_ref, wpa_ref, wpb_ref,
                      wo_ref, gffn_ref, wup_ref, cw_ref, cb_ref, wdn_ref, gfin_ref,
                      y_ref, conv_ref, carry_ref, ua_ref, ub_ref, acc_ref, h2_ref,
                      *, nseg, seglen):
    i = pl.program_id(0)
    NC = N_FFN_CHUNKS
    PADR = 8
    H0 = PADR - (CONV_WIDTH - 1)

    @pl.when(i == 0)
    def _():
        carry_ref[...] = prev_ref[...]

    x = x_ref[...]
    h = _rmsnorm(x, gmix_ref[...]).astype(BF16)
    zg = _dot(h, wg_ref[...])
    gb = zg[:, :RET_VW]
    gma = zg[:, RET_VW:RET_VW + D_MODEL]
    gmb = zg[:, RET_VW + D_MODEL:]
    oa = jnp.concatenate([oa_ref[p] for p in range(FOX_HEADS // 2)], axis=1)
    ya = _dot(oa, wpa_ref[...])
    nn = nb_ref[...].astype(F32) * gng_ref[...] * (gb * jax.nn.sigmoid(gb))
    yb = _dot(nn.astype(BF16), wpb_ref[...])
    y = jax.nn.sigmoid(gma) * ya + jax.nn.sigmoid(gmb) * yb
    x1 = x + _dot(y.astype(BF16), wo_ref[...])
    h2_ref[...] = _rmsnorm(x1, gffn_ref[...]).astype(BF16)
    acc_ref[...] = x1

    h2 = h2_ref[...]

    def up_half(u_ref, cidx, slot):
        u = _dot(h2, wup_ref[cidx])
        for s in range(nseg):
            u_ref[slot, s, PADR:PADR + seglen, :] = u[s * seglen:(s + 1) * seglen, :]
            u_ref[slot, s, H0:PADR, :] = carry_ref[cidx, s, H0:PADR, :]
            carry_ref[cidx, s, H0:PADR, :] = u[(s + 1) * seglen - (CONV_WIDTH - 1):(s + 1) * seglen, :]

    def conv_half(u_ref, cidx, slot):
        w = cw_ref[cidx]
        b = cb_ref[cidx]
        outs = []
        for s in range(nseg):
            acc = w[0:1] * u_ref[slot, s, H0:H0 + seglen, :]
            for jj in range(1, CONV_WIDTH):
                acc = acc + w[jj:jj + 1] * u_ref[slot, s, H0 + jj:H0 + jj + seglen, :]
            outs.append(b + acc)
        return outs[0] if nseg == 1 else jnp.concatenate(outs, axis=0)

    def stage_up(c):
        up_half(ua_ref, c, c % 2)
        up_half(ub_ref, NC + c, c % 2)

    stage_up(0)
    for c in range(NC):
        if c + 1 < NC:
            stage_up(c + 1)
        a = conv_half(ua_ref, c, c % 2)
        b = conv_half(ub_ref, NC + c, c % 2)
        g = (jax.nn.gelu(a) * b).astype(BF16)
        acc_ref[...] += _dot(g, wdn_ref[c])
    y_ref[...] = _rmsnorm(acc_ref[...], gfin_ref[...])

    @pl.when(i == pl.num_programs(0) - 1)
    def _():
        conv_ref[...] = carry_ref[:, :, H0:PADR, :]


def _mixer_ffn(x, oa, nb, prev, weights, tm, nseg, seglen):
    M = x.shape[0]
    gmix, wg, gng, wpa, wpb, wo, gffn, wup, cw, cb, wdn, gfin = weights
    FC = FFN_CHUNK
    NC = N_FFN_CHUNKS
    row = lambda w: pl.BlockSpec((tm, w), lambda i: (i, 0))
    wspec = lambda a: pl.BlockSpec(a.shape, lambda i, n=a.ndim: (0,) * n,
                                   pipeline_mode=pl.Buffered(1))
    in_specs = [row(D_MODEL),
                pl.BlockSpec((FOX_HEADS // 2, tm, LANES), lambda i: (0, i, 0)),
                row(RET_VW), wspec(prev)] + [wspec(w) for w in weights]
    out_shape = (jax.ShapeDtypeStruct((M, D_MODEL), F32),
                 jax.ShapeDtypeStruct((2 * NC, nseg, CONV_WIDTH - 1, FC), F32))
    out_specs = (row(D_MODEL), _const_spec(out_shape[1].shape))
    return pl.pallas_call(
        functools.partial(_mixer_ffn_kernel, nseg=nseg, seglen=seglen),
        grid=(M // tm,),
        in_specs=in_specs,
        out_specs=out_specs,
        out_shape=out_shape,
        scratch_shapes=[pltpu.VMEM((2 * NC, nseg, 8, FC), F32),
                        pltpu.VMEM((2, nseg, 8 + seglen, FC), F32),
                        pltpu.VMEM((2, nseg, 8 + seglen, FC), F32),
                        pltpu.VMEM((tm, D_MODEL), F32),
                        pltpu.VMEM((tm, D_MODEL), BF16)],
        compiler_params=pltpu.CompilerParams(dimension_semantics=("arbitrary",),
                                             vmem_limit_bytes=VMEM_LIMIT),
        name="mixer_ffn",
    )(x, oa, nb, prev, *weights)


def _rotary_tables(pos):
    half = RET_KEY_DIM // 2
    inv = 1.0 / (ROPE_BASE ** jnp.linspace(0.0, 1.0, half, dtype=F32))
    ang = pos.astype(F32)[:, None] * inv[None, :]
    cos = jnp.cos(ang)
    sin = jnp.sin(ang)
    return jnp.concatenate([cos, cos], axis=1), jnp.concatenate([-sin, sin], axis=1)


def _pad_heads(w, pad):
    d = w.shape[0]
    w = w.reshape(d, FOX_HEADS, FOX_HEAD_DIM)
    w = jnp.pad(w, ((0, 0), (0, 0), (0, pad - FOX_HEAD_DIM)))
    return w.reshape(d, FOX_HEADS * pad)


def _prompt_consts(tm):
    tri = np.tril(np.ones((tm, tm), np.float32))
    eq = np.zeros((3 * LANES, FOX_HEADS * HEAD_PAD), np.float32)
    ek = np.zeros((3 * LANES, FOX_HEADS * HEAD_PAD), np.float32)
    oneq = np.zeros((1, FOX_HEADS * HEAD_PAD), np.float32)
    onek = np.zeros((1, FOX_HEADS * HEAD_PAD), np.float32)
    onev = np.zeros((1, FOX_HEADS * V_PAD), np.float32)
    for hh in range(FOX_HEADS):
        base = hh * HEAD_PAD + BIAS_COL
        for part in range(3):
            eq[part * LANES + hh, base + part] = 1.0
            ek[part * LANES + hh, base + 3 + part] = -1.0
            onek[0, base + part] = 1.0
            oneq[0, base + 3 + part] = 1.0
        onev[0, hh * V_PAD + FOX_HEAD_DIM] = 1.0
    return (jnp.asarray(tri, BF16), jnp.asarray(eq, BF16), jnp.asarray(ek, BF16),
            jnp.asarray(oneq), jnp.asarray(onek), jnp.asarray(onev))


def _chunk_cols(a):
    lead = a.shape[:-1]
    a = a.reshape(lead + (2 * N_FFN_CHUNKS, FFN_CHUNK))
    return jnp.moveaxis(a, -2, 0)


def _tile(n, pref):
    t = min(n, pref)
    while n % t:
        t //= 2
    return t


def kernel(x_prompt, x_sample, cache_fox_k, cache_fox_v, cache_fox_logf, state_ret, state_ffn_conv,
           norm_mix_g, w_in, b_fox_f, gn_ret_g, w_pa, w_pb, w_o, norm_ffn_g, w_up, conv_w, conv_b,
           w_down, norm_final_g):
    depth = w_in.shape[0]
    Bp, S, _ = x_prompt.shape
    Bs, Ts, _ = x_sample.shape
    P = cache_fox_k.shape[2]
    assert depth == 1 and Bp == 1, "kernel handles the single-layer, single-prompt configuration"
    l = 0

    w = w_in[l]
    o0 = 3 * FOX_W
    o1 = o0 + FOX_HEADS
    o2 = o1 + 2 * RET_KW + RET_VW
    w_q, w_k, w_v = w[:, :FOX_W], w[:, FOX_W:2 * FOX_W], w[:, 2 * FOX_W:o0]
    wq_aug = _pad_heads(w_q, HEAD_PAD).astype(BF16)
    wk_aug = _pad_heads(w_k, HEAD_PAD).astype(BF16)
    wv_aug = _pad_heads(w_v, V_PAD).astype(BF16)
    wkv = w[:, FOX_W:o0].astype(BF16)
    wqkv = w[:, :o0].astype(BF16)
    wf = jnp.pad(w[:, o0:o1], ((0, 0), (0, LANES - FOX_HEADS))).astype(BF16)
    bf = jnp.pad(b_fox_f[l].astype(F32), (0, LANES - FOX_HEADS))[None, :]
    wb = w[:, o1:o2].astype(BF16)
    wg = w[:, o2:].astype(BF16)
    gmix = norm_mix_g[l].astype(F32)[None, :]
    mix_weights = (
        gmix, wg, gn_ret_g[l].astype(F32)[None, :], w_pa[l].astype(BF16), w_pb[l].astype(BF16),
        w_o[l].astype(BF16), norm_ffn_g[l].astype(F32)[None, :],
        _chunk_cols(w_up[l]).astype(BF16),
        jnp.pad(_chunk_cols(conv_w[l].astype(F32)), ((0, 0), (0, 8 - CONV_WIDTH), (0, 0))),
        _chunk_cols(conv_b[l].astype(F32)[None, :]),
        w_down[l].reshape(N_FFN_CHUNKS, FFN_CHUNK, D_MODEL).astype(BF16),
        norm_final_g.astype(F32)[None, :],
    )

    tm_a = _tile(S, 256)
    cos_p, sin_p = _rotary_tables(jnp.arange(S))
    (qT, ka, vT, k_p, v_p, logf_p, qb, kb, vb) = _inproj_prompt(
        x_prompt[0], gmix, wq_aug, wk_aug, wv_aug, wkv, wf, bf, wb, cos_p, sin_p,
        _prompt_consts(tm_a), tm_a)
    oa_p = _fox_prompt(qT, ka, vT, _tile(S, 1024))
    zero_state = jnp.zeros((1, RET_HEADS, RET_KEY_DIM, RET_VAL_DIM), F32)
    nb_p, ret_p = _retention(qb, kb, vb, zero_state, 1, S, _tile(S, 256))
    tm_d = _tile(S, 256)
    zero_prev = jnp.zeros((2 * N_FFN_CHUNKS, 1, 8, FFN_CHUNK), F32)
    y_p, conv_p = _mixer_ffn(x_prompt[0], oa_p, nb_p, zero_prev, mix_weights, tm_d, 1, tm_d)

    Ms = Bs * Ts
    cos_s, sin_s = _rotary_tables(P + jnp.arange(Ts))
    cos_s = jnp.tile(cos_s, (Bs, 1))
    sin_s = jnp.tile(sin_s, (Bs, 1))
    (q_s, k_s, v_s, logf_s, qb_s, kb_s, vb_s) = _inproj_sample(
        x_sample.reshape(Ms, D_MODEL), gmix, wqkv, wf, bf, wb, cos_s, sin_s)
    KP = ((P + Ts + LANES - 1) // LANES) * LANES
    lf_all = jnp.concatenate([cache_fox_logf[l].astype(F32), logf_s.reshape(Bs, Ts, FOX_HEADS)], axis=1)
    lfT = jnp.pad(jnp.swapaxes(lf_all, 1, 2), ((0, 0), (0, 0), (0, KP - P - Ts)))
    oa_s = _fox_sample(q_s, k_s, v_s, cache_fox_k[l].reshape(Bs, P, FOX_W),
                       cache_fox_v[l].reshape(Bs, P, FOX_W), lfT, Bs, Ts)
    oa_s = jnp.moveaxis(oa_s.reshape(Ms, FOX_HEADS // 2, LANES), 1, 0)
    nb_s, ret_s = _retention(qb_s, kb_s, vb_s, state_ret[l].astype(F32), Bs, Ts, Ts)
    prev_s = _chunk_cols(state_ffn_conv[l].astype(F32))
    prev_s = jnp.pad(prev_s, ((0, 0), (0, 0), (8 - (CONV_WIDTH - 1), 0), (0, 0)))
    y_s, conv_s = _mixer_ffn(x_sample.reshape(Ms, D_MODEL), oa_s, nb_s, prev_s, mix_weights,
                             Ms, Bs, Ts)

    def unchunk(cv):
        return jnp.moveaxis(cv, 0, 2).reshape(cv.shape[1], CONV_WIDTH - 1, 2 * FFN_DIM)

    hshape = (FOX_HEADS, FOX_HEAD_DIM)
    return (
        y_p[None],
        y_s.reshape(Bs, Ts, D_MODEL),
        k_p.reshape((1, 1, S) + hshape),
        v_p.reshape((1, 1, S) + hshape),
        logf_p.reshape(1, 1, S, FOX_HEADS),
        ret_p[None],
        unchunk(conv_p)[None],
        k_s.reshape((1, Bs, Ts) + hshape),
        v_s.reshape((1, Bs, Ts) + hshape),
        logf_s.reshape(1, Bs, Ts, FOX_HEADS),
        ret_s[None],
        unchunk(conv_s)[None],
    )
```

```python
import functools
import math

import numpy as np
import jax
import jax.numpy as jnp
from jax import lax
from jax.experimental import pallas as pl
from jax.experimental.pallas import tpu as pltpu

F32 = jnp.float32
BF16 = jnp.bfloat16

D_MODEL = 1024
FOX_HEADS = 8
FOX_HEAD_DIM = 64
RET_HEADS = 4
RET_KEY_DIM = 128
RET_VAL_DIM = 256
FFN_DIM = 2816
CONV_WIDTH = 3
EPS = 1e-6
ROPE_BASE = 10000.0

FOX_W = FOX_HEADS * FOX_HEAD_DIM
RET_KW = RET_HEADS * RET_KEY_DIM
RET_VW = RET_HEADS * RET_VAL_DIM

LOG2E = 1.4426950408889634
LANES = 128
HEAD_PAD = LANES
V_PAD = 80
BIAS_COL = FOX_HEAD_DIM
NEG = -1e30
STALE_MAX_GUARD = 2.0 ** 60
PRUNE_LOG2 = -160.0
NORM_SLACK = 1.02
FFN_CHUNK = 256
N_FFN_CHUNKS = FFN_DIM // FFN_CHUNK
VMEM_LIMIT = 56 * 1024 * 1024


def _rmsnorm(x, g):
    ms = jnp.mean(x * x, axis=-1, keepdims=True)
    return x * lax.rsqrt(ms + EPS) * g


def _split3(x):
    hi = x.astype(BF16)
    r1 = x - hi.astype(F32)
    mid = r1.astype(BF16)
    lo = (r1 - mid.astype(F32)).astype(BF16)
    return hi, mid, lo


def _log_sigmoid(x):
    return jnp.minimum(x, 0.0) - jnp.log1p(jnp.exp(-jnp.abs(x)))


def _dot(a, b):
    return jnp.dot(a, b, preferred_element_type=F32)


def _rotary(x, cos2, sin2):
    return x * cos2 + pltpu.roll(x, RET_KEY_DIM // 2, 1) * sin2


def _const_spec(shape):
    n = len(shape)
    return pl.BlockSpec(shape, lambda *_: (0,) * n)


def _inproj_prompt_kernel(x_ref, g_ref, wq_ref, wk_ref, wv_ref, wkv_ref, wf_ref, bf_ref, wb_ref,
                          cos_ref, sin_ref, tri_ref, eq_ref, ek_ref, oneq_ref, onek_ref, onev_ref,
                          sel_ref,
                          qT_ref, ka_ref, vT_ref, k32_ref, v32_ref, logf_ref, qb_ref, kb_ref, vb_ref,
                          stats_ref, carry_ref):
    tm = x_ref.shape[0]

    @pl.when(pl.program_id(0) == 0)
    def _():
        carry_ref[...] = jnp.zeros_like(carry_ref)

    h = _rmsnorm(x_ref[...], g_ref[...]).astype(BF16)

    logf = _log_sigmoid(_dot(h, wf_ref[...]) + bf_ref[...])
    logf_ref[...] = logf[:, :FOX_HEADS]
    lane = lax.broadcasted_iota(jnp.int32, logf.shape, 1)
    logf = jnp.where(lane < FOX_HEADS, logf, 0.0)
    r = _dot(tri_ref[...], jnp.concatenate(_split3(logf), axis=1))
    c = r[:, :LANES] + r[:, LANES:2 * LANES] + r[:, 2 * LANES:] + carry_ref[...]
    carry_ref[...] = c[tm - 1:tm, :]
    c3 = jnp.concatenate(_split3(c * LOG2E), axis=1)

    q_aug = (_dot(h, wq_ref[...]) * (FOX_HEAD_DIM ** -0.5 * LOG2E)
             + _dot(c3, eq_ref[...]) + oneq_ref[...])
    qT = q_aug.T.astype(BF16)
    k_aug = _dot(h, wk_ref[...]) + _dot(c3, ek_ref[...]) + onek_ref[...]

    sel = sel_ref[...]
    nq2 = jnp.max(_dot((q_aug * q_aug).astype(BF16), sel), axis=0, keepdims=True)
    nk2 = jnp.max(_dot((k_aug * k_aug).astype(BF16), sel), axis=0, keepdims=True)
    c2 = c * LOG2E
    stats_ref[0] = jnp.concatenate(
        [nq2, nk2, c2[0:1, :], c2[tm - 1:tm, :], jnp.zeros((4, LANES), F32)], axis=0)

    k_aug = k_aug.astype(BF16)
    vT = (_dot(h, wv_ref[...]) + onev_ref[...]).T.astype(BF16)
    for hh in range(FOX_HEADS):
        qT_ref[hh] = qT[hh * HEAD_PAD:(hh + 1) * HEAD_PAD, :]
        ka_ref[hh] = k_aug[:, hh * HEAD_PAD:(hh + 1) * HEAD_PAD]
        vT_ref[hh] = vT[hh * V_PAD:(hh + 1) * V_PAD, :]

    zkv = _dot(h, wkv_ref[...])
    k32_ref[...] = zkv[:, :FOX_W]
    v32_ref[...] = zkv[:, FOX_W:]

    zb = _dot(h, wb_ref[...])
    cos2 = cos_ref[...]
    sin2 = sin_ref[...]
    for hh in range(RET_HEADS):
        sl = slice(hh * RET_KEY_DIM, (hh + 1) * RET_KEY_DIM)
        qb_ref[:, sl] = _rotary(zb[:, sl], cos2, sin2).astype(BF16)
        xk = zb[:, RET_KW + hh * RET_KEY_DIM:RET_KW + (hh + 1) * RET_KEY_DIM]
        kb_ref[:, sl] = (_rotary(xk, cos2, sin2) * (RET_KEY_DIM ** -0.5)).astype(BF16)
    vb_ref[...] = zb[:, 2 * RET_KW:].astype(BF16)


def _inproj_prompt(x, g, wq, wk, wv, wkv, wf, bf, wb, cos2, sin2, consts, tm):
    S = x.shape[0]
    tri, eq, ek, oneq, onek, onev, sel = consts
    row = lambda w: pl.BlockSpec((tm, w), lambda i: (i, 0))
    in_specs = [row(D_MODEL), _const_spec(g.shape), _const_spec(wq.shape), _const_spec(wk.shape),
                _const_spec(wv.shape), _const_spec(wkv.shape), _const_spec(wf.shape),
                _const_spec(bf.shape), _const_spec(wb.shape), row(LANES), row(LANES),
                _const_spec(tri.shape), _const_spec(eq.shape), _const_spec(ek.shape),
                _const_spec(oneq.shape), _const_spec(onek.shape), _const_spec(onev.shape),
                _const_spec(sel.shape)]
    out_shape = (
        jax.ShapeDtypeStruct((FOX_HEADS, HEAD_PAD, S), BF16),
        jax.ShapeDtypeStruct((FOX_HEADS, S, HEAD_PAD), BF16),
        jax.ShapeDtypeStruct((FOX_HEADS, V_PAD, S), BF16),
        jax.ShapeDtypeStruct((S, FOX_W), F32),
        jax.ShapeDtypeStruct((S, FOX_W), F32),
        jax.ShapeDtypeStruct((S, FOX_HEADS), F32),
        jax.ShapeDtypeStruct((S, RET_KW), BF16),
        jax.ShapeDtypeStruct((S, RET_KW), BF16),
        jax.ShapeDtypeStruct((S, RET_VW), BF16),
        jax.ShapeDtypeStruct((S // tm, 8, LANES), F32),
    )
    out_specs = (
        pl.BlockSpec((FOX_HEADS, HEAD_PAD, tm), lambda i: (0, 0, i)),
        pl.BlockSpec((FOX_HEADS, tm, HEAD_PAD), lambda i: (0, i, 0)),
        pl.BlockSpec((FOX_HEADS, V_PAD, tm), lambda i: (0, 0, i)),
        row(FOX_W), row(FOX_W), row(FOX_HEADS), row(RET_KW), row(RET_KW), row(RET_VW),
        pl.BlockSpec((1, 8, LANES), lambda i: (i, 0, 0)),
    )
    return pl.pallas_call(
        _inproj_prompt_kernel,
        grid=(S // tm,),
        in_specs=in_specs,
        out_specs=out_specs,
        out_shape=out_shape,
        scratch_shapes=[pltpu.VMEM((1, LANES), F32)],
        compiler_params=pltpu.CompilerParams(dimension_semantics=("arbitrary",),
                                             vmem_limit_bytes=VMEM_LIMIT),
        name="inproj_prompt",
    )(x, g, wq, wk, wv, wkv, wf, bf, wb, cos2, sin2, tri, eq, ek, oneq, onek, onev, sel)


def _inproj_sample_kernel(x_ref, g_ref, wqkv_ref, wf_ref, bf_ref, wb_ref, cos_ref, sin_ref,
                          q_ref, k32_ref, v32_ref, logf_ref, qb_ref, kb_ref, vb_ref):
    h = _rmsnorm(x_ref[...], g_ref[...]).astype(BF16)
    logf = _log_sigmoid(_dot(h, wf_ref[...]) + bf_ref[...])
    logf_ref[...] = logf[:, :FOX_HEADS]
    z = _dot(h, wqkv_ref[...])
    q_ref[...] = (z[:, :FOX_W] * (FOX_HEAD_DIM ** -0.5 * LOG2E)).astype(BF16)
    k32_ref[...] = z[:, FOX_W:2 * FOX_W]
    v32_ref[...] = z[:, 2 * FOX_W:]
    zb = _dot(h, wb_ref[...])
    cos2 = cos_ref[...]
    sin2 = sin_ref[...]
    for hh in range(RET_HEADS):
        sl = slice(hh * RET_KEY_DIM, (hh + 1) * RET_KEY_DIM)
        qb_ref[:, sl] = _rotary(zb[:, sl], cos2, sin2).astype(BF16)
        xk = zb[:, RET_KW + hh * RET_KEY_DIM:RET_KW + (hh + 1) * RET_KEY_DIM]
        kb_ref[:, sl] = (_rotary(xk, cos2, sin2) * (RET_KEY_DIM ** -0.5)).astype(BF16)
    vb_ref[...] = zb[:, 2 * RET_KW:].astype(BF16)


def _inproj_sample(x, g, wqkv, wf, bf, wb, cos2, sin2):
    M = x.shape[0]
    args = (x, g, wqkv, wf, bf, wb, cos2, sin2)
    out_shape = (
        jax.ShapeDtypeStruct((M, FOX_W), BF16),
        jax.ShapeDtypeStruct((M, FOX_W), F32),
        jax.ShapeDtypeStruct((M, FOX_W), F32),
        jax.ShapeDtypeStruct((M, FOX_HEADS), F32),
        jax.ShapeDtypeStruct((M, RET_KW), BF16),
        jax.ShapeDtypeStruct((M, RET_KW), BF16),
        jax.ShapeDtypeStruct((M, RET_VW), BF16),
    )
    return pl.pallas_call(
        _inproj_sample_kernel,
        grid=(1,),
        in_specs=[_const_spec(a.shape) for a in args],
        out_specs=tuple(_const_spec(o.shape) for o in out_shape),
        out_shape=out_shape,
        compiler_params=pltpu.CompilerParams(dimension_semantics=("arbitrary",),
                                             vmem_limit_bytes=VMEM_LIMIT),
        name="inproj_sample",
    )(*args)


def _fox_prompt_kernel(it_ref, jt_ref, jfetch_ref, live_ref, qT_ref, ka_ref, vT_ref, o_ref,
                       m_ref, acc_ref, pv_ref):
    del jfetch_ref
    t = pl.program_id(0)
    i = it_ref[t]
    j = jt_ref[t]
    T = qT_ref.shape[2]

    def heads(fn, fn_pruned=None):
        def body(hh, carry):
            is_live = live_ref[t * FOX_HEADS + hh] != 0

            @pl.when(is_live)
            def _():
                fn(hh)

            if fn_pruned is not None:
                @pl.when(jnp.logical_not(is_live))
                def _():
                    fn_pruned(hh)
            return carry
        lax.fori_loop(0, FOX_HEADS, body, 0)

    def scores(hh):
        return _dot(ka_ref[hh], qT_ref[hh])

    def exact_head(hh):
        s = scores(hh)
        kk = lax.broadcasted_iota(jnp.int32, s.shape, 0) + j * T
        qq = lax.broadcasted_iota(jnp.int32, s.shape, 1) + i * T
        s = jnp.where(kk > qq, NEG, s)
        m_old = m_ref[hh]
        m_new = jnp.maximum(m_old, jnp.max(s, axis=0, keepdims=True))
        p = jnp.exp2(s - m_new).astype(BF16)
        alpha = jnp.exp2(m_old - m_new)
        acc_ref[hh] = alpha * acc_ref[hh] + _dot(vT_ref[hh], p)
        m_ref[hh] = m_new

    def stale_head(hh):
        p = jnp.exp2(scores(hh) - m_ref[hh]).astype(BF16)
        pv_ref[hh] = _dot(vT_ref[hh], p)

    def pruned_head(hh):
        pv_ref[hh, FOX_HEAD_DIM:FOX_HEAD_DIM + 1, :] = jnp.zeros((1, T), F32)

    def commit_head(hh):
        acc_ref[hh] += pv_ref[hh]

    def stale_step():
        heads(stale_head, pruned_head)
        lsum = pv_ref[:, FOX_HEAD_DIM:FOX_HEAD_DIM + 1, :]
        bad = jnp.sum(jnp.where(lsum < STALE_MAX_GUARD, 0.0, 1.0)) > 0.0

        @pl.when(jnp.logical_not(bad))
        def _():
            heads(commit_head)

        @pl.when(bad)
        def _():
            heads(exact_head)

    @pl.when(j == i)
    def _():
        m_ref[...] = jnp.full_like(m_ref, NEG)
        acc_ref[...] = jnp.zeros_like(acc_ref)
        heads(exact_head)

    @pl.when(j < i)
    def _():
        stale_step()

    @pl.when(j == 0)
    def _():
        for pr in range(FOX_HEADS // 2):
            halves = []
            for hh in (2 * pr, 2 * pr + 1):
                a = acc_ref[hh]
                halves.append(a[:FOX_HEAD_DIM] / a[FOX_HEAD_DIM:FOX_HEAD_DIM + 1])
            o_ref[pr] = jnp.concatenate(halves, axis=0).T.astype(BF16)


def _prune_tables(stats, it, jt, nb):
    per = stats.shape[0] // nb
    st = stats.reshape(nb, per, 8, LANES)[:, :, :, :FOX_HEADS]
    nq = jnp.sqrt(jnp.max(st[:, :, 0, :], axis=1)) * NORM_SLACK
    nk = jnp.sqrt(jnp.max(st[:, :, 1, :], axis=1)) * NORM_SLACK
    c_first = st[:, 0, 2, :]
    c_last = st[:, per - 1, 3, :]
    bound = (nq[it] * (nk[jt] + nk[it])) - (c_last[jt] - c_first[it])
    live = jnp.logical_or(jnp.asarray(jt == it)[:, None], jnp.logical_not(bound < PRUNE_LOG2))
    steps = jnp.arange(len(it), dtype=jnp.int32)
    last_live = lax.cummax(jnp.where(jnp.any(live, axis=1), steps, 0))
    return jnp.asarray(jt)[last_live], live.astype(jnp.int32).reshape(-1)


def _fox_prompt(qT, ka, vT, stats, T):
    S = ka.shape[1]
    nb = S // T
    it = np.array([i for i in range(nb) for _ in range(i + 1)], np.int32)
    jt = np.array([j for i in range(nb) for j in range(i, -1, -1)], np.int32)
    jfetch, live = _prune_tables(stats, it, jt, nb)
    grid_spec = pltpu.PrefetchScalarGridSpec(
        num_scalar_prefetch=4,
        grid=(len(it),),
        in_specs=[
            pl.BlockSpec((FOX_HEADS, HEAD_PAD, T), lambda t, it, jt, jf, lv: (0, 0, it[t])),
            pl.BlockSpec((FOX_HEADS, T, HEAD_PAD), lambda t, it, jt, jf, lv: (0, jf[t], 0)),
            pl.BlockSpec((FOX_HEADS, V_PAD, T), lambda t, it, jt, jf, lv: (0, 0, jf[t])),
        ],
        out_specs=pl.BlockSpec((FOX_HEADS // 2, T, LANES),
                               lambda t, it, jt, jf, lv: (0, it[t], 0)),
        scratch_shapes=[pltpu.VMEM((FOX_HEADS, 1, T), F32),
                        pltpu.VMEM((FOX_HEADS, V_PAD, T), F32),
                        pltpu.VMEM((FOX_HEADS, V_PAD, T), F32)],
    )
    return pl.pallas_call(
        _fox_prompt_kernel,
        grid_spec=grid_spec,
        out_shape=jax.ShapeDtypeStruct((FOX_HEADS // 2, S, LANES), BF16),
        compiler_params=pltpu.CompilerParams(dimension_semantics=("arbitrary",),
                                             vmem_limit_bytes=VMEM_LIMIT),
        name="fox_prompt",
    )(jnp.asarray(it), jnp.asarray(jt), jfetch, live, qT, ka, vT)


def _fox_sample_kernel(q_ref, kn_ref, vn_ref, ck_ref, cv_ref, lfT_ref, up_ref, ex_ref,
                       o_ref, kx_ref, vx_ref):
    P = ck_ref.shape[1]
    Tn = q_ref.shape[0]
    KP = kx_ref.shape[0]
    HQ = FOX_HEADS * Tn
    nchunk = KP // LANES

    kx_ref[:P, :] = ck_ref[0].astype(BF16)
    vx_ref[:P, :] = cv_ref[0].astype(BF16)
    kx_ref[P:P + Tn, :] = kn_ref[...].astype(BF16)
    vx_ref[P:P + Tn, :] = vn_ref[...].astype(BF16)
    kx_ref[P + Tn:, :] = jnp.zeros((KP - P - Tn, FOX_W), BF16)
    vx_ref[P + Tn:, :] = jnp.zeros((KP - P - Tn, FOX_W), BF16)

    def stack3(x):
        parts3 = [t.astype(F32) for t in _split3(x)] + [jnp.zeros_like(x)]
        return jnp.concatenate(parts3, axis=0).astype(BF16)

    x3 = stack3(lfT_ref[0])
    up = up_ref[...]
    run = jnp.zeros((4 * FOX_HEADS, 1), F32)
    parts = []
    for cidx in range(nchunk):
        y = _dot(x3[:, cidx * LANES:(cidx + 1) * LANES], up) + run
        parts.append(y)
        run = y[:, LANES - 1:LANES]
    y = jnp.concatenate(parts, axis=1)
    cT = (y[:FOX_HEADS] + y[FOX_HEADS:2 * FOX_HEADS] + y[2 * FOX_HEADS:3 * FOX_HEADS]) * LOG2E
    ckx = _dot(ex_ref[...], stack3(cT))

    tail = ckx[:, P:P + LANES]
    rowq = lax.broadcasted_iota(jnp.int32, tail.shape, 0) % Tn
    lanek = lax.broadcasted_iota(jnp.int32, tail.shape, 1)
    cq = jnp.sum(jnp.where(lanek == rowq, tail, 0.0), axis=1, keepdims=True)

    q = q_ref[...]
    qt = jnp.concatenate([q] * FOX_HEADS, axis=0)
    rh = lax.broadcasted_iota(jnp.int32, qt.shape, 0) // Tn
    lh = lax.broadcasted_iota(jnp.int32, qt.shape, 1) // FOX_HEAD_DIM
    qbd = jnp.where(rh == lh, qt, jnp.zeros_like(qt))

    s = lax.dot_general(qbd, kx_ref[...], (((1,), (1,)), ((), ())), preferred_element_type=F32)
    s = s + cq - ckx
    key = lax.broadcasted_iota(jnp.int32, s.shape, 1)
    qpos = P + lax.broadcasted_iota(jnp.int32, s.shape, 0) % Tn
    s = jnp.where(key > qpos, NEG, s)
    m = jnp.max(s, axis=1, keepdims=True)
    p = jnp.exp2(s - m)
    l = jnp.sum(p, axis=1, keepdims=True)
    z = _dot(p.astype(BF16), vx_ref[...]) / l
    zh = lax.broadcasted_iota(jnp.int32, (Tn, FOX_W), 1) // FOX_HEAD_DIM
    o = jnp.zeros((Tn, FOX_W), F32)
    for hh in range(FOX_HEADS):
        o = o + jnp.where(zh == hh, z[hh * Tn:(hh + 1) * Tn, :], 0.0)
    o_ref[...] = o.astype(BF16)


def _fox_sample(q, kn, vn, cache_k, cache_v, lfT, B, Tn):
    P = cache_k.shape[1]
    KP = lfT.shape[2]
    HQ = FOX_HEADS * Tn
    up = jnp.asarray(np.triu(np.ones((LANES, LANES), np.float32)), BF16)
    ex = np.zeros((HQ, 4 * FOX_HEADS), np.float32)
    for part in range(3):
        for hh in range(FOX_HEADS):
            ex[hh * Tn:(hh + 1) * Tn, part * FOX_HEADS + hh] = 1.0
    ex = jnp.asarray(ex, BF16)
    rowb = lambda w: pl.BlockSpec((Tn, w), lambda b: (b, 0))
    return pl.pallas_call(
        _fox_sample_kernel,
        grid=(B,),
        in_specs=[rowb(FOX_W), rowb(FOX_W), rowb(FOX_W),
                  pl.BlockSpec((1, P, FOX_W), lambda b: (b, 0, 0)),
                  pl.BlockSpec((1, P, FOX_W), lambda b: (b, 0, 0)),
                  pl.BlockSpec((1, FOX_HEADS, KP), lambda b: (b, 0, 0)),
                  _const_spec(up.shape), _const_spec(ex.shape)],
        out_specs=rowb(FOX_W),
        out_shape=jax.ShapeDtypeStruct((B * Tn, FOX_W), BF16),
        scratch_shapes=[pltpu.VMEM((KP, FOX_W), BF16), pltpu.VMEM((KP, FOX_W), BF16)],
        compiler_params=pltpu.CompilerParams(dimension_semantics=("arbitrary",),
                                             vmem_limit_bytes=VMEM_LIMIT),
        name="fox_sample",
    )(q, kn, vn, cache_k, cache_v, lfT, up, ex)


def _retention_kernel(q_ref, k_ref, v_ref, s0_ref, dmat_ref, xi_ref, zeta_ref, gam_ref,
                      n_ref, sout_ref, st_ref):
    c = pl.program_id(1)

    @pl.when(c == 0)
    def _():
        st_ref[...] = s0_ref[0]

    for hh in range(RET_HEADS):
        q = q_ref[:, hh * RET_KEY_DIM:(hh + 1) * RET_KEY_DIM]
        k = k_ref[:, hh * RET_KEY_DIM:(hh + 1) * RET_KEY_DIM]
        v = v_ref[:, hh * RET_VAL_DIM:(hh + 1) * RET_VAL_DIM]
        st = st_ref[hh]
        sc = lax.dot_general(q, k, (((1,), (1,)), ((), ())), preferred_element_type=F32)
        sc = sc * dmat_ref[hh]
        o = _dot(sc.astype(BF16), v) + _dot(q, st.astype(BF16)) * xi_ref[hh]
        kz = (k.astype(F32) * zeta_ref[hh]).astype(BF16)
        upd = lax.dot_general(kz, v, (((0,), (0,)), ((), ())), preferred_element_type=F32)
        st_ref[hh] = gam_ref[hh] * st + upd
        mu = jnp.mean(o, axis=-1, keepdims=True)
        d = o - mu
        var = jnp.mean(d * d, axis=-1, keepdims=True)
        n_ref[:, hh * RET_VAL_DIM:(hh + 1) * RET_VAL_DIM] = (d * lax.rsqrt(var + EPS)).astype(BF16)

    @pl.when(c == pl.num_programs(1) - 1)
    def _():
        sout_ref[0] = st_ref[...]


def _ret_log_gamma():
    return jnp.log(1.0 - jnp.exp2(-5.0 - jnp.arange(RET_HEADS, dtype=F32)))


def _retention(q, k, v, state0, B, L, C):
    nc = L // C
    lg = _ret_log_gamma()
    idx = jnp.arange(C, dtype=F32)
    diff = idx[:, None] - idx[None, :]
    dmat = jnp.where(diff[None] >= 0, jnp.exp(jnp.maximum(diff, 0.0)[None] * lg[:, None, None]), 0.0)
    xi = jnp.exp((idx[None, :] + 1.0) * lg[:, None])
    zeta = jnp.exp((C - 1.0 - idx[None, :]) * lg[:, None])
    xi = jnp.broadcast_to(xi[:, :, None], (RET_HEADS, C, RET_VAL_DIM))
    zeta = jnp.broadcast_to(zeta[:, :, None], (RET_HEADS, C, RET_KEY_DIM))
    gam = jnp.broadcast_to(jnp.exp(C * lg)[:, None, None], (RET_HEADS, 1, RET_VAL_DIM))
    rowc = lambda w: pl.BlockSpec((C, w), lambda b, c: (b * nc + c, 0))
    st_spec = pl.BlockSpec((1, RET_HEADS, RET_KEY_DIM, RET_VAL_DIM), lambda b, c: (b, 0, 0, 0))
    return pl.pallas_call(
        _retention_kernel,
        grid=(B, nc),
        in_specs=[rowc(RET_KW), rowc(RET_KW), rowc(RET_VW), st_spec,
                  _const_spec(dmat.shape), _const_spec(xi.shape), _const_spec(zeta.shape),
                  _const_spec(gam.shape)],
        out_specs=(rowc(RET_VW), st_spec),
        out_shape=(jax.ShapeDtypeStruct((B * L, RET_VW), BF16),
                   jax.ShapeDtypeStruct((B, RET_HEADS, RET_KEY_DIM, RET_VAL_DIM), F32)),
        scratch_shapes=[pltpu.VMEM((RET_HEADS, RET_KEY_DIM, RET_VAL_DIM), F32)],
        compiler_params=pltpu.CompilerParams(dimension_semantics=("arbitrary", "arbitrary"),
                                             vmem_limit_bytes=VMEM_LIMIT),
        name="retention",
    )(q, k, v, state0, dmat, xi, zeta, gam)


def _mixer_ffn_kernel(x_ref, oa_ref, nb_ref, prev_ref, gmix_ref, wg_ref, gng_ref, wpa_ref, wpb_ref,
                      wo_ref, gffn_ref, wup_ref, cw_ref, cb_ref, wdn_ref, gfin_ref,
                      y_ref, conv_ref, carry_ref, ua_ref, ub_ref, acc_ref, h2_ref,
                      *, nseg, seglen):
    i = pl.program_id(0)
    NC = N_FFN_CHUNKS
    PADR = 8
    H0 = PADR - (CONV_WIDTH - 1)

    @pl.when(i == 0)
    def _():
        carry_ref[...] = prev_ref[...]

    x = x_ref[...]
    h = _rmsnorm(x, gmix_ref[...]).astype(BF16)
    zg = _dot(h, wg_ref[...])
    gb = zg[:, :RET_VW]
    gma = zg[:, RET_VW:RET_VW + D_MODEL]
    gmb = zg[:, RET_VW + D_MODEL:]
    oa = jnp.concatenate([oa_ref[p] for p in range(FOX_HEADS // 2)], axis=1)
    ya = _dot(oa, wpa_ref[...])
    nn = nb_ref[...].astype(F32) * gng_ref[...] * (gb * jax.nn.sigmoid(gb))
    yb = _dot(nn.astype(BF16), wpb_ref[...])
    y = jax.nn.sigmoid(gma) * ya + jax.nn.sigmoid(gmb) * yb
    x1 = x + _dot(y.astype(BF16), wo_ref[...])
    h2_ref[...] = _rmsnorm(x1, gffn_ref[...]).astype(BF16)
    acc_ref[...] = x1

    h2 = h2_ref[...]

    def up_half(u_ref, cidx, slot):
        u = _dot(h2, wup_ref[cidx])
        for s in range(nseg):
            u_ref[slot, s, PADR:PADR + seglen, :] = u[s * seglen:(s + 1) * seglen, :]
            u_ref[slot, s, H0:PADR, :] = carry_ref[cidx, s, H0:PADR, :]
            carry_ref[cidx, s, H0:PADR, :] = u[(s + 1) * seglen - (CONV_WIDTH - 1):(s + 1) * seglen, :]

    def conv_half(u_ref, cidx, slot):
        w = cw_ref[cidx]
        b = cb_ref[cidx]
        outs = []
        for s in range(nseg):
            acc = w[0:1] * u_ref[slot, s, H0:H0 + seglen, :]
            for jj in range(1, CONV_WIDTH):
                acc = acc + w[jj:jj + 1] * u_ref[slot, s, H0 + jj:H0 + jj + seglen, :]
            outs.append(b + acc)
        return outs[0] if nseg == 1 else jnp.concatenate(outs, axis=0)

    def stage_up(c):
        up_half(ua_ref, c, c % 2)
        up_half(ub_ref, NC + c, c % 2)

    stage_up(0)
    for c in range(NC):
        if c + 1 < NC:
            stage_up(c + 1)
        a = conv_half(ua_ref, c, c % 2)
        b = conv_half(ub_ref, NC + c, c % 2)
        g = (jax.nn.gelu(a) * b).astype(BF16)
        acc_ref[...] += _dot(g, wdn_ref[c])
    y_ref[...] = _rmsnorm(acc_ref[...], gfin_ref[...])

    @pl.when(i == pl.num_programs(0) - 1)
    def _():
        conv_ref[...] = carry_ref[:, :, H0:PADR, :]


def _mixer_ffn(x, oa, nb, prev, weights, tm, nseg, seglen):
    M = x.shape[0]
    gmix, wg, gng, wpa, wpb, wo, gffn, wup, cw, cb, wdn, gfin = weights
    FC = FFN_CHUNK
    NC = N_FFN_CHUNKS
    row = lambda w: pl.BlockSpec((tm, w), lambda i: (i, 0))
    wspec = lambda a: pl.BlockSpec(a.shape, lambda i, n=a.ndim: (0,) * n,
                                   pipeline_mode=pl.Buffered(1))
    in_specs = [row(D_MODEL),
                pl.BlockSpec((FOX_HEADS // 2, tm, LANES), lambda i: (0, i, 0)),
                row(RET_VW), wspec(prev)] + [wspec(w) for w in weights]
    out_shape = (jax.ShapeDtypeStruct((M, D_MODEL), F32),
                 jax.ShapeDtypeStruct((2 * NC, nseg, CONV_WIDTH - 1, FC), F32))
    out_specs = (row(D_MODEL), _const_spec(out_shape[1].shape))
    return pl.pallas_call(
        functools.partial(_mixer_ffn_kernel, nseg=nseg, seglen=seglen),
        grid=(M // tm,),
        in_specs=in_specs,
        out_specs=out_specs,
        out_shape=out_shape,
        scratch_shapes=[pltpu.VMEM((2 * NC, nseg, 8, FC), F32),
                        pltpu.VMEM((2, nseg, 8 + seglen, FC), F32),
                        pltpu.VMEM((2, nseg, 8 + seglen, FC), F32),
                        pltpu.VMEM((tm, D_MODEL), F32),
                        pltpu.VMEM((tm, D_MODEL), BF16)],
        compiler_params=pltpu.CompilerParams(dimension_semantics=("arbitrary",),
                                             vmem_limit_bytes=VMEM_LIMIT),
        name="mixer_ffn",
    )(x, oa, nb, prev, *weights)


def _rotary_tables(pos):
    half = RET_KEY_DIM // 2
    inv = 1.0 / (ROPE_BASE ** jnp.linspace(0.0, 1.0, half, dtype=F32))
    ang = pos.astype(F32)[:, None] * inv[None, :]
    cos = jnp.cos(ang)
    sin = jnp.sin(ang)
    return jnp.concatenate([cos, cos], axis=1), jnp.concatenate([-sin, sin], axis=1)


def _pad_heads(w, pad):
    d = w.shape[0]
    w = w.reshape(d, FOX_HEADS, FOX_HEAD_DIM)
    w = jnp.pad(w, ((0, 0), (0, 0), (0, pad - FOX_HEAD_DIM)))
    return w.reshape(d, FOX_HEADS * pad)


def _prompt_consts(tm):
    tri = np.tril(np.ones((tm, tm), np.float32))
    eq = np.zeros((3 * LANES, FOX_HEADS * HEAD_PAD), np.float32)
    ek = np.zeros((3 * LANES, FOX_HEADS * HEAD_PAD), np.float32)
    oneq = np.zeros((1, FOX_HEADS * HEAD_PAD), np.float32)
    onek = np.zeros((1, FOX_HEADS * HEAD_PAD), np.float32)
    onev = np.zeros((1, FOX_HEADS * V_PAD), np.float32)
    for hh in range(FOX_HEADS):
        base = hh * HEAD_PAD + BIAS_COL
        for part in range(3):
            eq[part * LANES + hh, base + part] = 1.0
            ek[part * LANES + hh, base + 3 + part] = -1.0
            onek[0, base + part] = 1.0
            oneq[0, base + 3 + part] = 1.0
        onev[0, hh * V_PAD + FOX_HEAD_DIM] = 1.0
    sel = np.zeros((FOX_HEADS * HEAD_PAD, LANES), np.float32)
    for hh in range(FOX_HEADS):
        sel[hh * HEAD_PAD:hh * HEAD_PAD + FOX_HEAD_DIM, hh] = 1.0
    return (jnp.asarray(tri, BF16), jnp.asarray(eq, BF16), jnp.asarray(ek, BF16),
            jnp.asarray(oneq), jnp.asarray(onek), jnp.asarray(onev), jnp.asarray(sel, BF16))


def _chunk_cols(a):
    lead = a.shape[:-1]
    a = a.reshape(lead + (2 * N_FFN_CHUNKS, FFN_CHUNK))
    return jnp.moveaxis(a, -2, 0)


def _tile(n, pref):
    t = min(n, pref)
    while n % t:
        t //= 2
    return t


def kernel(x_prompt, x_sample, cache_fox_k, cache_fox_v, cache_fox_logf, state_ret, state_ffn_conv,
           norm_mix_g, w_in, b_fox_f, gn_ret_g, w_pa, w_pb, w_o, norm_ffn_g, w_up, conv_w, conv_b,
           w_down, norm_final_g):
    depth = w_in.shape[0]
    Bp, S, _ = x_prompt.shape
    Bs, Ts, _ = x_sample.shape
    P = cache_fox_k.shape[2]
    assert depth == 1 and Bp == 1, "kernel handles the single-layer, single-prompt configuration"
    l = 0

    w = w_in[l]
    o0 = 3 * FOX_W
    o1 = o0 + FOX_HEADS
    o2 = o1 + 2 * RET_KW + RET_VW
    w_q, w_k, w_v = w[:, :FOX_W], w[:, FOX_W:2 * FOX_W], w[:, 2 * FOX_W:o0]
    wq_aug = _pad_heads(w_q, HEAD_PAD).astype(BF16)
    wk_aug = _pad_heads(w_k, HEAD_PAD).astype(BF16)
    wv_aug = _pad_heads(w_v, V_PAD).astype(BF16)
    wkv = w[:, FOX_W:o0].astype(BF16)
    wqkv = w[:, :o0].astype(BF16)
    wf = jnp.pad(w[:, o0:o1], ((0, 0), (0, LANES - FOX_HEADS))).astype(BF16)
    bf = jnp.pad(b_fox_f[l].astype(F32), (0, LANES - FOX_HEADS))[None, :]
    wb = w[:, o1:o2].astype(BF16)
    wg = w[:, o2:].astype(BF16)
    gmix = norm_mix_g[l].astype(F32)[None, :]
    mix_weights = (
        gmix, wg, gn_ret_g[l].astype(F32)[None, :], w_pa[l].astype(BF16), w_pb[l].astype(BF16),
        w_o[l].astype(BF16), norm_ffn_g[l].astype(F32)[None, :],
        _chunk_cols(w_up[l]).astype(BF16),
        jnp.pad(_chunk_cols(conv_w[l].astype(F32)), ((0, 0), (0, 8 - CONV_WIDTH), (0, 0))),
        _chunk_cols(conv_b[l].astype(F32)[None, :]),
        w_down[l].reshape(N_FFN_CHUNKS, FFN_CHUNK, D_MODEL).astype(BF16),
        norm_final_g.astype(F32)[None, :],
    )

    tm_a = _tile(S, 256)
    cos_p, sin_p = _rotary_tables(jnp.arange(S))
    (qT, ka, vT, k_p, v_p, logf_p, qb, kb, vb, stats) = _inproj_prompt(
        x_prompt[0], gmix, wq_aug, wk_aug, wv_aug, wkv, wf, bf, wb, cos_p, sin_p,
        _prompt_consts(tm_a), tm_a)
    oa_p = _fox_prompt(qT, ka, vT, stats, _tile(S, 1024))
    zero_state = jnp.zeros((1, RET_HEADS, RET_KEY_DIM, RET_VAL_DIM), F32)
    nb_p, ret_p = _retention(qb, kb, vb, zero_state, 1, S, _tile(S, 256))
    tm_d = _tile(S, 256)
    zero_prev = jnp.zeros((2 * N_FFN_CHUNKS, 1, 8, FFN_CHUNK), F32)
    y_p, conv_p = _mixer_ffn(x_prompt[0], oa_p, nb_p, zero_prev, mix_weights, tm_d, 1, tm_d)

    Ms = Bs * Ts
    cos_s, sin_s = _rotary_tables(P + jnp.arange(Ts))
    cos_s = jnp.tile(cos_s, (Bs, 1))
    sin_s = jnp.tile(sin_s, (Bs, 1))
    (q_s, k_s, v_s, logf_s, qb_s, kb_s, vb_s) = _inproj_sample(
        x_sample.reshape(Ms, D_MODEL), gmix, wqkv, wf, bf, wb, cos_s, sin_s)
    KP = ((P + Ts + LANES - 1) // LANES) * LANES
    lf_all = jnp.concatenate([cache_fox_logf[l].astype(F32), logf_s.reshape(Bs, Ts, FOX_HEADS)], axis=1)
    lfT = jnp.pad(jnp.swapaxes(lf_all, 1, 2), ((0, 0), (0, 0), (0, KP - P - Ts)))
    oa_s = _fox_sample(q_s, k_s, v_s, cache_fox_k[l].reshape(Bs, P, FOX_W),
                       cache_fox_v[l].reshape(Bs, P, FOX_W), lfT, Bs, Ts)
    oa_s = jnp.moveaxis(oa_s.reshape(Ms, FOX_HEADS // 2, LANES), 1, 0)
    nb_s, ret_s = _retention(qb_s, kb_s, vb_s, state_ret[l].astype(F32), Bs, Ts, Ts)
    prev_s = _chunk_cols(state_ffn_conv[l].astype(F32))
    prev_s = jnp.pad(prev_s, ((0, 0), (0, 0), (8 - (CONV_WIDTH - 1), 0), (0, 0)))
    y_s, conv_s = _mixer_ffn(x_sample.reshape(Ms, D_MODEL), oa_s, nb_s, prev_s, mix_weights,
                             Ms, Bs, Ts)

    def unchunk(cv):
        return jnp.moveaxis(cv, 0, 2).reshape(cv.shape[1], CONV_WIDTH - 1, 2 * FFN_DIM)

    hshape = (FOX_HEADS, FOX_HEAD_DIM)
    return (
        y_p[None],
        y_s.reshape(Bs, Ts, D_MODEL),
        k_p.reshape((1, 1, S) + hshape),
        v_p.reshape((1, 1, S) + hshape),
        logf_p.reshape(1, 1, S, FOX_HEADS),
        ret_p[None],
        unchunk(conv_p)[None],
        k_s.reshape((1, Bs, Ts) + hshape),
        v_s.reshape((1, Bs, Ts) + hshape),
        logf_s.reshape(1, Bs, Ts, FOX_HEADS),
        ret_s[None],
        unchunk(conv_s)[None],
    )
```

```python
import functools
import math

import numpy as np
import jax
import jax.numpy as jnp
from jax import lax
from jax.experimental import pallas as pl
from jax.experimental.pallas import tpu as pltpu

F32 = jnp.float32
BF16 = jnp.bfloat16

D_MODEL = 1024
FOX_HEADS = 8
FOX_HEAD_DIM = 64
RET_HEADS = 4
RET_KEY_DIM = 128
RET_VAL_DIM = 256
FFN_DIM = 2816
CONV_WIDTH = 3
EPS = 1e-6
ROPE_BASE = 10000.0

FOX_W = FOX_HEADS * FOX_HEAD_DIM
RET_KW = RET_HEADS * RET_KEY_DIM
RET_VW = RET_HEADS * RET_VAL_DIM

LOG2E = 1.4426950408889634
LANES = 128
HEAD_PAD = LANES
V_PAD = 80
BIAS_COL = FOX_HEAD_DIM
NEG = -1e30
STALE_MAX_GUARD = 2.0 ** 60
PRUNE_LOG2 = -160.0
NORM_SLACK = 1.02
FFN_CHUNK = 256
N_FFN_CHUNKS = FFN_DIM // FFN_CHUNK
VMEM_LIMIT = 56 * 1024 * 1024


def _rmsnorm(x, g):
    ms = jnp.mean(x * x, axis=-1, keepdims=True)
    return x * lax.rsqrt(ms + EPS) * g


def _split3(x):
    hi = x.astype(BF16)
    r1 = x - hi.astype(F32)
    mid = r1.astype(BF16)
    lo = (r1 - mid.astype(F32)).astype(BF16)
    return hi, mid, lo


def _log_sigmoid(x):
    return jnp.minimum(x, 0.0) - jnp.log1p(jnp.exp(-jnp.abs(x)))


def _dot(a, b):
    return jnp.dot(a, b, preferred_element_type=F32)


def _rotary(x, cos2, sin2):
    return x * cos2 + pltpu.roll(x, RET_KEY_DIM // 2, 1) * sin2


def _const_spec(shape):
    n = len(shape)
    return pl.BlockSpec(shape, lambda *_: (0,) * n)


def _inproj_prompt_kernel(x_ref, g_ref, wq_ref, wk_ref, wv_ref, wkv_ref, wf_ref, bf_ref, wb_ref,
                          cos_ref, sin_ref, tri_ref, eq_ref, ek_ref, oneq_ref, onek_ref, onev_ref,
                          sel_ref,
                          qT_ref, ka_ref, vT_ref, k32_ref, v32_ref, logf_ref, qb_ref, kb_ref, vb_ref,
                          stats_ref, carry_ref):
    tm = x_ref.shape[0]

    @pl.when(pl.program_id(0) == 0)
    def _():
        carry_ref[...] = jnp.zeros_like(carry_ref)

    h = _rmsnorm(x_ref[...], g_ref[...]).astype(BF16)

    logf = _log_sigmoid(_dot(h, wf_ref[...]) + bf_ref[...])
    logf_ref[...] = logf[:, :FOX_HEADS]
    lane = lax.broadcasted_iota(jnp.int32, logf.shape, 1)
    logf = jnp.where(lane < FOX_HEADS, logf, 0.0)
    r = _dot(tri_ref[...], jnp.concatenate(_split3(logf), axis=1))
    c = r[:, :LANES] + r[:, LANES:2 * LANES] + r[:, 2 * LANES:] + carry_ref[...]
    carry_ref[...] = c[tm - 1:tm, :]
    c3 = jnp.concatenate(_split3(c * LOG2E), axis=1)

    q_aug = (_dot(h, wq_ref[...]) * (FOX_HEAD_DIM ** -0.5 * LOG2E)
             + _dot(c3, eq_ref[...]) + oneq_ref[...])
    qT = q_aug.T.astype(BF16)
    k_aug = _dot(h, wk_ref[...]) + _dot(c3, ek_ref[...]) + onek_ref[...]

    sel = sel_ref[...]
    nq2 = jnp.max(_dot((q_aug * q_aug).astype(BF16), sel), axis=0, keepdims=True)
    nk2 = jnp.max(_dot((k_aug * k_aug).astype(BF16), sel), axis=0, keepdims=True)
    c2 = c * LOG2E
    stats_ref[0] = jnp.concatenate(
        [nq2, nk2, c2[0:1, :], c2[tm - 1:tm, :], jnp.zeros((4, LANES), F32)], axis=0)

    k_aug = k_aug.astype(BF16)
    vT = (_dot(h, wv_ref[...]) + onev_ref[...]).T.astype(BF16)
    for hh in range(FOX_HEADS):
        qT_ref[hh] = qT[hh * HEAD_PAD:(hh + 1) * HEAD_PAD, :]
        ka_ref[hh] = k_aug[:, hh * HEAD_PAD:(hh + 1) * HEAD_PAD]
        vT_ref[hh] = vT[hh * V_PAD:(hh + 1) * V_PAD, :]

    zkv = _dot(h, wkv_ref[...])
    k32_ref[...] = zkv[:, :FOX_W]
    v32_ref[...] = zkv[:, FOX_W:]

    zb = _dot(h, wb_ref[...])
    cos2 = cos_ref[...]
    sin2 = sin_ref[...]
    for hh in range(RET_HEADS):
        sl = slice(hh * RET_KEY_DIM, (hh + 1) * RET_KEY_DIM)
        qb_ref[:, sl] = _rotary(zb[:, sl], cos2, sin2).astype(BF16)
        xk = zb[:, RET_KW + hh * RET_KEY_DIM:RET_KW + (hh + 1) * RET_KEY_DIM]
        kb_ref[:, sl] = (_rotary(xk, cos2, sin2) * (RET_KEY_DIM ** -0.5)).astype(BF16)
    vb_ref[...] = zb[:, 2 * RET_KW:].astype(BF16)


def _inproj_prompt(x, g, wq, wk, wv, wkv, wf, bf, wb, cos2, sin2, consts, tm):
    S = x.shape[0]
    tri, eq, ek, oneq, onek, onev, sel = consts
    row = lambda w: pl.BlockSpec((tm, w), lambda i: (i, 0))
    in_specs = [row(D_MODEL), _const_spec(g.shape), _const_spec(wq.shape), _const_spec(wk.shape),
                _const_spec(wv.shape), _const_spec(wkv.shape), _const_spec(wf.shape),
                _const_spec(bf.shape), _const_spec(wb.shape), row(LANES), row(LANES),
                _const_spec(tri.shape), _const_spec(eq.shape), _const_spec(ek.shape),
                _const_spec(oneq.shape), _const_spec(onek.shape), _const_spec(onev.shape),
                _const_spec(sel.shape)]
    out_shape = (
        jax.ShapeDtypeStruct((FOX_HEADS, HEAD_PAD, S), BF16),
        jax.ShapeDtypeStruct((FOX_HEADS, S, HEAD_PAD), BF16),
        jax.ShapeDtypeStruct((FOX_HEADS, V_PAD, S), BF16),
        jax.ShapeDtypeStruct((S, FOX_W), F32),
        jax.ShapeDtypeStruct((S, FOX_W), F32),
        jax.ShapeDtypeStruct((S, FOX_HEADS), F32),
        jax.ShapeDtypeStruct((S, RET_KW), BF16),
        jax.ShapeDtypeStruct((S, RET_KW), BF16),
        jax.ShapeDtypeStruct((S, RET_VW), BF16),
        jax.ShapeDtypeStruct((S // tm, 8, LANES), F32),
    )
    out_specs = (
        pl.BlockSpec((FOX_HEADS, HEAD_PAD, tm), lambda i: (0, 0, i)),
        pl.BlockSpec((FOX_HEADS, tm, HEAD_PAD), lambda i: (0, i, 0)),
        pl.BlockSpec((FOX_HEADS, V_PAD, tm), lambda i: (0, 0, i)),
        row(FOX_W), row(FOX_W), row(FOX_HEADS), row(RET_KW), row(RET_KW), row(RET_VW),
        pl.BlockSpec((1, 8, LANES), lambda i: (i, 0, 0)),
    )
    return pl.pallas_call(
        _inproj_prompt_kernel,
        grid=(S // tm,),
        in_specs=in_specs,
        out_specs=out_specs,
        out_shape=out_shape,
        scratch_shapes=[pltpu.VMEM((1, LANES), F32)],
        compiler_params=pltpu.CompilerParams(dimension_semantics=("arbitrary",),
                                             vmem_limit_bytes=VMEM_LIMIT),
        name="inproj_prompt",
    )(x, g, wq, wk, wv, wkv, wf, bf, wb, cos2, sin2, tri, eq, ek, oneq, onek, onev, sel)


def _inproj_sample_kernel(x_ref, g_ref, wqkv_ref, wf_ref, bf_ref, wb_ref, cos_ref, sin_ref,
                          q_ref, k32_ref, v32_ref, logf_ref, qb_ref, kb_ref, vb_ref):
    h = _rmsnorm(x_ref[...], g_ref[...]).astype(BF16)
    logf = _log_sigmoid(_dot(h, wf_ref[...]) + bf_ref[...])
    logf_ref[...] = logf[:, :FOX_HEADS]
    z = _dot(h, wqkv_ref[...])
    q_ref[...] = (z[:, :FOX_W] * (FOX_HEAD_DIM ** -0.5 * LOG2E)).astype(BF16)
    k32_ref[...] = z[:, FOX_W:2 * FOX_W]
    v32_ref[...] = z[:, 2 * FOX_W:]
    zb = _dot(h, wb_ref[...])
    cos2 = cos_ref[...]
    sin2 = sin_ref[...]
    for hh in range(RET_HEADS):
        sl = slice(hh * RET_KEY_DIM, (hh + 1) * RET_KEY_DIM)
        qb_ref[:, sl] = _rotary(zb[:, sl], cos2, sin2).astype(BF16)
        xk = zb[:, RET_KW + hh * RET_KEY_DIM:RET_KW + (hh + 1) * RET_KEY_DIM]
        kb_ref[:, sl] = (_rotary(xk, cos2, sin2) * (RET_KEY_DIM ** -0.5)).astype(BF16)
    vb_ref[...] = zb[:, 2 * RET_KW:].astype(BF16)


def _inproj_sample(x, g, wqkv, wf, bf, wb, cos2, sin2):
    M = x.shape[0]
    args = (x, g, wqkv, wf, bf, wb, cos2, sin2)
    out_shape = (
        jax.ShapeDtypeStruct((M, FOX_W), BF16),
        jax.ShapeDtypeStruct((M, FOX_W), F32),
        jax.ShapeDtypeStruct((M, FOX_W), F32),
        jax.ShapeDtypeStruct((M, FOX_HEADS), F32),
        jax.ShapeDtypeStruct((M, RET_KW), BF16),
        jax.ShapeDtypeStruct((M, RET_KW), BF16),
        jax.ShapeDtypeStruct((M, RET_VW), BF16),
    )
    return pl.pallas_call(
        _inproj_sample_kernel,
        grid=(1,),
        in_specs=[_const_spec(a.shape) for a in args],
        out_specs=tuple(_const_spec(o.shape) for o in out_shape),
        out_shape=out_shape,
        compiler_params=pltpu.CompilerParams(dimension_semantics=("arbitrary",),
                                             vmem_limit_bytes=VMEM_LIMIT),
        name="inproj_sample",
    )(*args)


def _fox_prompt_kernel(it_ref, jt_ref, jfetch_ref, live_ref, qT_ref, ka_ref, vT_ref, o_ref,
                       m_ref, acc_ref, pv_ref):
    del jfetch_ref
    t = pl.program_id(0)
    i = it_ref[t]
    j = jt_ref[t]
    T = qT_ref.shape[2]

    def heads(fn, fn_pruned=None):
        def body(hh, carry):
            is_live = live_ref[t * FOX_HEADS + hh] != 0

            @pl.when(is_live)
            def _():
                fn(hh)

            if fn_pruned is not None:
                @pl.when(jnp.logical_not(is_live))
                def _():
                    fn_pruned(hh)
            return carry
        lax.fori_loop(0, FOX_HEADS, body, 0)

    def scores(hh):
        return _dot(ka_ref[hh], qT_ref[hh])

    def exact_head(hh):
        s = scores(hh)
        kk = lax.broadcasted_iota(jnp.int32, s.shape, 0) + j * T
        qq = lax.broadcasted_iota(jnp.int32, s.shape, 1) + i * T
        s = jnp.where(kk > qq, NEG, s)
        m_old = m_ref[hh]
        m_new = jnp.maximum(m_old, jnp.max(s, axis=0, keepdims=True))
        p = jnp.exp2(s - m_new).astype(BF16)
        alpha = jnp.exp2(m_old - m_new)
        acc_ref[hh] = alpha * acc_ref[hh] + _dot(vT_ref[hh], p)
        m_ref[hh] = m_new

    def stale_head(hh):
        p = jnp.exp2(scores(hh) - m_ref[hh]).astype(BF16)
        pv_ref[hh] = _dot(vT_ref[hh], p)

    def pruned_head(hh):
        pv_ref[hh, FOX_HEAD_DIM:FOX_HEAD_DIM + 1, :] = jnp.zeros((1, T), F32)

    def commit_head(hh):
        acc_ref[hh] += pv_ref[hh]

    def stale_step():
        heads(stale_head, pruned_head)
        lsum = pv_ref[:, FOX_HEAD_DIM:FOX_HEAD_DIM + 1, :]
        bad = jnp.sum(jnp.where(lsum < STALE_MAX_GUARD, 0.0, 1.0)) > 0.0

        @pl.when(jnp.logical_not(bad))
        def _():
            heads(commit_head)

        @pl.when(bad)
        def _():
            heads(exact_head)

    @pl.when(j == i)
    def _():
        m_ref[...] = jnp.full_like(m_ref, NEG)
        acc_ref[...] = jnp.zeros_like(acc_ref)
        heads(exact_head)

    @pl.when(j < i)
    def _():
        stale_step()

    @pl.when(j == 0)
    def _():
        for pr in range(FOX_HEADS // 2):
            halves = []
            for hh in (2 * pr, 2 * pr + 1):
                a = acc_ref[hh]
                halves.append(a[:FOX_HEAD_DIM] / a[FOX_HEAD_DIM:FOX_HEAD_DIM + 1])
            o_ref[pr] = jnp.concatenate(halves, axis=0).T.astype(BF16)


def _prune_tables(stats, it, jt, nb):
    per = stats.shape[0] // nb
    st = stats.reshape(nb, per, 8, LANES)[:, :, :, :FOX_HEADS]
    nq = jnp.sqrt(jnp.max(st[:, :, 0, :], axis=1)) * NORM_SLACK
    nk = jnp.sqrt(jnp.max(st[:, :, 1, :], axis=1)) * NORM_SLACK
    c_first = st[:, 0, 2, :]
    c_last = st[:, per - 1, 3, :]
    bound = (nq[it] * (nk[jt] + nk[it])) - (c_last[jt] - c_first[it])
    live = jnp.logical_or(jnp.asarray(jt == it)[:, None], jnp.logical_not(bound < PRUNE_LOG2))
    steps = jnp.arange(len(it), dtype=jnp.int32)
    last_live = lax.cummax(jnp.where(jnp.any(live, axis=1), steps, 0))
    return jnp.asarray(jt)[last_live], live.astype(jnp.int32).reshape(-1)


def _fox_prompt(qT, ka, vT, stats, T):
    S = ka.shape[1]
    nb = S // T
    it = np.array([i for i in range(nb) for _ in range(i + 1)], np.int32)
    jt = np.array([j for i in range(nb) for j in range(i, -1, -1)], np.int32)
    jfetch, live = _prune_tables(stats, it, jt, nb)
    grid_spec = pltpu.PrefetchScalarGridSpec(
        num_scalar_prefetch=4,
        grid=(len(it),),
        in_specs=[
            pl.BlockSpec((FOX_HEADS, HEAD_PAD, T), lambda t, it, jt, jf, lv: (0, 0, it[t])),
            pl.BlockSpec((FOX_HEADS, T, HEAD_PAD), lambda t, it, jt, jf, lv: (0, jf[t], 0)),
            pl.BlockSpec((FOX_HEADS, V_PAD, T), lambda t, it, jt, jf, lv: (0, 0, jf[t])),
        ],
        out_specs=pl.BlockSpec((FOX_HEADS // 2, T, LANES),
                               lambda t, it, jt, jf, lv: (0, it[t], 0)),
        scratch_shapes=[pltpu.VMEM((FOX_HEADS, 1, T), F32),
                        pltpu.VMEM((FOX_HEADS, V_PAD, T), F32),
                        pltpu.VMEM((FOX_HEADS, V_PAD, T), F32)],
    )
    return pl.pallas_call(
        _fox_prompt_kernel,
        grid_spec=grid_spec,
        out_shape=jax.ShapeDtypeStruct((FOX_HEADS // 2, S, LANES), BF16),
        compiler_params=pltpu.CompilerParams(dimension_semantics=("arbitrary",),
                                             vmem_limit_bytes=VMEM_LIMIT),
        name="fox_prompt",
    )(jnp.asarray(it), jnp.asarray(jt), jfetch, live, qT, ka, vT)


def _fox_sample_kernel(q_ref, kn_ref, vn_ref, ckT_ref, cvT_ref, lfT_ref, up_ref, ex_ref, o_ref):
    P = ckT_ref.shape[3]
    Tn = q_ref.shape[0]
    KP = lfT_ref.shape[2]
    HQ = FOX_HEADS * Tn
    nchunk = KP // LANES
    nt = (((1,), (1,)), ((), ()))

    def stack3(x):
        parts3 = [t.astype(F32) for t in _split3(x)] + [jnp.zeros_like(x)]
        return jnp.concatenate(parts3, axis=0).astype(BF16)

    x3 = stack3(lfT_ref[0])
    up = up_ref[...]
    run = jnp.zeros((4 * FOX_HEADS, 1), F32)
    parts = []
    for cidx in range(nchunk):
        y = _dot(x3[:, cidx * LANES:(cidx + 1) * LANES], up) + run
        parts.append(y)
        run = y[:, LANES - 1:LANES]
    y = jnp.concatenate(parts, axis=1)
    cT = (y[:FOX_HEADS] + y[FOX_HEADS:2 * FOX_HEADS] + y[2 * FOX_HEADS:3 * FOX_HEADS]) * LOG2E
    ckx = _dot(ex_ref[...], stack3(cT))

    tail = ckx[:, P:P + LANES]
    rowq = lax.broadcasted_iota(jnp.int32, tail.shape, 0) % Tn
    lanek = lax.broadcasted_iota(jnp.int32, tail.shape, 1)
    cq = jnp.sum(jnp.where(lanek == rowq, tail, 0.0), axis=1, keepdims=True)

    q = q_ref[...]
    qt = jnp.concatenate([q] * FOX_HEADS, axis=0)
    rh = lax.broadcasted_iota(jnp.int32, qt.shape, 0) // Tn
    lh = lax.broadcasted_iota(jnp.int32, qt.shape, 1) // FOX_HEAD_DIM
    qbd = jnp.where(rh == lh, qt, jnp.zeros_like(qt))

    kT = ckT_ref[0].reshape(FOX_W, P).astype(BF16)
    vT = cvT_ref[0].reshape(FOX_W, P).astype(BF16)
    s_c = _dot(qbd, kT) + cq - ckx[:, :P]
    s_n = lax.dot_general(qbd, kn_ref[...].astype(BF16), nt, preferred_element_type=F32)
    s_n = s_n + cq - ckx[:, P:P + Tn]
    key = lax.broadcasted_iota(jnp.int32, s_n.shape, 1)
    qrow = lax.broadcasted_iota(jnp.int32, s_n.shape, 0) % Tn
    s_n = jnp.where(key > qrow, NEG, s_n)
    m = jnp.maximum(jnp.max(s_c, axis=1, keepdims=True), jnp.max(s_n, axis=1, keepdims=True))
    p_c = jnp.exp2(s_c - m)
    p_n = jnp.exp2(s_n - m)
    l = jnp.sum(p_c, axis=1, keepdims=True) + jnp.sum(p_n, axis=1, keepdims=True)
    z = lax.dot_general(p_c.astype(BF16), vT, nt, preferred_element_type=F32)
    z = (z + _dot(p_n.astype(BF16), vn_ref[...].astype(BF16))) / l
    zh = lax.broadcasted_iota(jnp.int32, (Tn, FOX_W), 1) // FOX_HEAD_DIM
    o = jnp.zeros((Tn, FOX_W), F32)
    for hh in range(FOX_HEADS):
        o = o + jnp.where(zh == hh, z[hh * Tn:(hh + 1) * Tn, :], 0.0)
    o_ref[...] = o.astype(BF16)


def _fox_sample(q, kn, vn, cache_kT, cache_vT, lfT, B, Tn):
    P = cache_kT.shape[3]
    KP = lfT.shape[2]
    HQ = FOX_HEADS * Tn
    up = jnp.asarray(np.triu(np.ones((LANES, LANES), np.float32)), BF16)
    ex = np.zeros((HQ, 4 * FOX_HEADS), np.float32)
    for part in range(3):
        for hh in range(FOX_HEADS):
            ex[hh * Tn:(hh + 1) * Tn, part * FOX_HEADS + hh] = 1.0
    ex = jnp.asarray(ex, BF16)
    rowb = lambda w: pl.BlockSpec((Tn, w), lambda b: (b, 0))
    return pl.pallas_call(
        _fox_sample_kernel,
        grid=(B,),
        in_specs=[rowb(FOX_W), rowb(FOX_W), rowb(FOX_W),
                  pl.BlockSpec((1, FOX_HEADS, FOX_HEAD_DIM, P), lambda b: (b, 0, 0, 0)),
                  pl.BlockSpec((1, FOX_HEADS, FOX_HEAD_DIM, P), lambda b: (b, 0, 0, 0)),
                  pl.BlockSpec((1, FOX_HEADS, KP), lambda b: (b, 0, 0)),
                  _const_spec(up.shape), _const_spec(ex.shape)],
        out_specs=rowb(FOX_W),
        out_shape=jax.ShapeDtypeStruct((B * Tn, FOX_W), BF16),
        compiler_params=pltpu.CompilerParams(dimension_semantics=("arbitrary",),
                                             vmem_limit_bytes=VMEM_LIMIT),
        name="fox_sample",
    )(q, kn, vn, cache_kT, cache_vT, lfT, up, ex)


def _retention_kernel(q_ref, k_ref, v_ref, s0_ref, dmat_ref, xi_ref, zeta_ref, gam_ref,
                      n_ref, sout_ref, st_ref):
    c = pl.program_id(1)

    @pl.when(c == 0)
    def _():
        st_ref[...] = s0_ref[0]

    for hh in range(RET_HEADS):
        q = q_ref[:, hh * RET_KEY_DIM:(hh + 1) * RET_KEY_DIM]
        k = k_ref[:, hh * RET_KEY_DIM:(hh + 1) * RET_KEY_DIM]
        v = v_ref[:, hh * RET_VAL_DIM:(hh + 1) * RET_VAL_DIM]
        st = st_ref[hh]
        sc = lax.dot_general(q, k, (((1,), (1,)), ((), ())), preferred_element_type=F32)
        sc = sc * dmat_ref[hh]
        o = _dot(sc.astype(BF16), v) + _dot(q, st.astype(BF16)) * xi_ref[hh]
        kz = (k.astype(F32) * zeta_ref[hh]).astype(BF16)
        upd = lax.dot_general(kz, v, (((0,), (0,)), ((), ())), preferred_element_type=F32)
        st_ref[hh] = gam_ref[hh] * st + upd
        mu = jnp.mean(o, axis=-1, keepdims=True)
        d = o - mu
        var = jnp.mean(d * d, axis=-1, keepdims=True)
        n_ref[:, hh * RET_VAL_DIM:(hh + 1) * RET_VAL_DIM] = (d * lax.rsqrt(var + EPS)).astype(BF16)

    @pl.when(c == pl.num_programs(1) - 1)
    def _():
        sout_ref[0] = st_ref[...]


def _ret_log_gamma():
    return jnp.log(1.0 - jnp.exp2(-5.0 - jnp.arange(RET_HEADS, dtype=F32)))


def _retention(q, k, v, state0, B, L, C):
    nc = L // C
    lg = _ret_log_gamma()
    idx = jnp.arange(C, dtype=F32)
    diff = idx[:, None] - idx[None, :]
    dmat = jnp.where(diff[None] >= 0, jnp.exp(jnp.maximum(diff, 0.0)[None] * lg[:, None, None]), 0.0)
    xi = jnp.exp((idx[None, :] + 1.0) * lg[:, None])
    zeta = jnp.exp((C - 1.0 - idx[None, :]) * lg[:, None])
    xi = jnp.broadcast_to(xi[:, :, None], (RET_HEADS, C, RET_VAL_DIM))
    zeta = jnp.broadcast_to(zeta[:, :, None], (RET_HEADS, C, RET_KEY_DIM))
    gam = jnp.broadcast_to(jnp.exp(C * lg)[:, None, None], (RET_HEADS, 1, RET_VAL_DIM))
    rowc = lambda w: pl.BlockSpec((C, w), lambda b, c: (b * nc + c, 0))
    st_spec = pl.BlockSpec((1, RET_HEADS, RET_KEY_DIM, RET_VAL_DIM), lambda b, c: (b, 0, 0, 0))
    return pl.pallas_call(
        _retention_kernel,
        grid=(B, nc),
        in_specs=[rowc(RET_KW), rowc(RET_KW), rowc(RET_VW), st_spec,
                  _const_spec(dmat.shape), _const_spec(xi.shape), _const_spec(zeta.shape),
                  _const_spec(gam.shape)],
        out_specs=(rowc(RET_VW), st_spec),
        out_shape=(jax.ShapeDtypeStruct((B * L, RET_VW), BF16),
                   jax.ShapeDtypeStruct((B, RET_HEADS, RET_KEY_DIM, RET_VAL_DIM), F32)),
        scratch_shapes=[pltpu.VMEM((RET_HEADS, RET_KEY_DIM, RET_VAL_DIM), F32)],
        compiler_params=pltpu.CompilerParams(dimension_semantics=("arbitrary", "arbitrary"),
                                             vmem_limit_bytes=VMEM_LIMIT),
        name="retention",
    )(q, k, v, state0, dmat, xi, zeta, gam)


def _mixer_ffn_kernel(x_ref, oa_ref, nb_ref, prev_ref, gmix_ref, wg_ref, gng_ref, wpa_ref, wpb_ref,
                      wo_ref, gffn_ref, wup_ref, cw_ref, cb_ref, wdn_ref, gfin_ref,
                      y_ref, conv_ref, carry_ref, ua_ref, ub_ref, acc_ref, h2_ref,
                      *, nseg, seglen):
    i = pl.program_id(0)
    NC = N_FFN_CHUNKS
    PADR = 8
    H0 = PADR - (CONV_WIDTH - 1)

    @pl.when(i == 0)
    def _():
        carry_ref[...] = prev_ref[...]

    x = x_ref[...]
    h = _rmsnorm(x, gmix_ref[...]).astype(BF16)
    zg = _dot(h, wg_ref[...])
    gb = zg[:, :RET_VW]
    gma = zg[:, RET_VW:RET_VW + D_MODEL]
    gmb = zg[:, RET_VW + D_MODEL:]
    oa = jnp.concatenate([oa_ref[p] for p in range(FOX_HEADS // 2)], axis=1)
    ya = _dot(oa, wpa_ref[...])
    nn = nb_ref[...].astype(F32) * gng_ref[...] * (gb * jax.nn.sigmoid(gb))
    yb = _dot(nn.astype(BF16), wpb_ref[...])
    y = jax.nn.sigmoid(gma) * ya + jax.nn.sigmoid(gmb) * yb
    x1 = x + _dot(y.astype(BF16), wo_ref[...])
    h2_ref[...] = _rmsnorm(x1, gffn_ref[...]).astype(BF16)
    acc_ref[...] = x1

    h2 = h2_ref[...]

    def up_half(u_ref, cidx, slot):
        u = _dot(h2, wup_ref[cidx])
        for s in range(nseg):
            u_ref[slot, s, PADR:PADR + seglen, :] = u[s * seglen:(s + 1) * seglen, :]
            u_ref[slot, s, H0:PADR, :] = carry_ref[cidx, s, H0:PADR, :]
            carry_ref[cidx, s, H0:PADR, :] = u[(s + 1) * seglen - (CONV_WIDTH - 1):(s + 1) * seglen, :]

    def conv_half(u_ref, cidx, slot):
        w = cw_ref[cidx]
        b = cb_ref[cidx]
        outs = []
        for s in range(nseg):
            acc = w[0:1] * u_ref[slot, s, H0:H0 + seglen, :]
            for jj in range(1, CONV_WIDTH):
                acc = acc + w[jj:jj + 1] * u_ref[slot, s, H0 + jj:H0 + jj + seglen, :]
            outs.append(b + acc)
        return outs[0] if nseg == 1 else jnp.concatenate(outs, axis=0)

    def stage_up(c):
        up_half(ua_ref, c, c % 2)
        up_half(ub_ref, NC + c, c % 2)

    stage_up(0)
    for c in range(NC):
        if c + 1 < NC:
            stage_up(c + 1)
        a = conv_half(ua_ref, c, c % 2)
        b = conv_half(ub_ref, NC + c, c % 2)
        g = (jax.nn.gelu(a) * b).astype(BF16)
        acc_ref[...] += _dot(g, wdn_ref[c])
    y_ref[...] = _rmsnorm(acc_ref[...], gfin_ref[...])

    @pl.when(i == pl.num_programs(0) - 1)
    def _():
        conv_ref[...] = carry_ref[:, :, H0:PADR, :]


def _mixer_ffn(x, oa, nb, prev, weights, tm, nseg, seglen):
    M = x.shape[0]
    gmix, wg, gng, wpa, wpb, wo, gffn, wup, cw, cb, wdn, gfin = weights
    FC = FFN_CHUNK
    NC = N_FFN_CHUNKS
    row = lambda w: pl.BlockSpec((tm, w), lambda i: (i, 0))
    wspec = lambda a: pl.BlockSpec(a.shape, lambda i, n=a.ndim: (0,) * n,
                                   pipeline_mode=pl.Buffered(1))
    in_specs = [row(D_MODEL),
                pl.BlockSpec((FOX_HEADS // 2, tm, LANES), lambda i: (0, i, 0)),
                row(RET_VW), wspec(prev)] + [wspec(w) for w in weights]
    out_shape = (jax.ShapeDtypeStruct((M, D_MODEL), F32),
                 jax.ShapeDtypeStruct((2 * NC, nseg, CONV_WIDTH - 1, FC), F32))
    out_specs = (row(D_MODEL), _const_spec(out_shape[1].shape))
    return pl.pallas_call(
        functools.partial(_mixer_ffn_kernel, nseg=nseg, seglen=seglen),
        grid=(M // tm,),
        in_specs=in_specs,
        out_specs=out_specs,
        out_shape=out_shape,
        scratch_shapes=[pltpu.VMEM((2 * NC, nseg, 8, FC), F32),
                        pltpu.VMEM((2, nseg, 8 + seglen, FC), F32),
                        pltpu.VMEM((2, nseg, 8 + seglen, FC), F32),
                        pltpu.VMEM((tm, D_MODEL), F32),
                        pltpu.VMEM((tm, D_MODEL), BF16)],
        compiler_params=pltpu.CompilerParams(dimension_semantics=("arbitrary",),
                                             vmem_limit_bytes=VMEM_LIMIT),
        name="mixer_ffn",
    )(x, oa, nb, prev, *weights)


def _rotary_tables(pos):
    half = RET_KEY_DIM // 2
    inv = 1.0 / (ROPE_BASE ** jnp.linspace(0.0, 1.0, half, dtype=F32))
    ang = pos.astype(F32)[:, None] * inv[None, :]
    cos = jnp.cos(ang)
    sin = jnp.sin(ang)
    return jnp.concatenate([cos, cos], axis=1), jnp.concatenate([-sin, sin], axis=1)


def _pad_heads(wt, pad):
    d = wt.shape[1]
    wt = wt.reshape(FOX_HEADS, FOX_HEAD_DIM, d)
    wt = jnp.pad(wt, ((0, 0), (0, pad - FOX_HEAD_DIM), (0, 0)))
    return wt.reshape(FOX_HEADS * pad, d)


def _prompt_consts(tm):
    tri = np.tril(np.ones((tm, tm), np.float32))
    eq = np.zeros((3 * LANES, FOX_HEADS * HEAD_PAD), np.float32)
    ek = np.zeros((3 * LANES, FOX_HEADS * HEAD_PAD), np.float32)
    oneq = np.zeros((1, FOX_HEADS * HEAD_PAD), np.float32)
    onek = np.zeros((1, FOX_HEADS * HEAD_PAD), np.float32)
    onev = np.zeros((1, FOX_HEADS * V_PAD), np.float32)
    for hh in range(FOX_HEADS):
        base = hh * HEAD_PAD + BIAS_COL
        for part in range(3):
            eq[part * LANES + hh, base + part] = 1.0
            ek[part * LANES + hh, base + 3 + part] = -1.0
            onek[0, base + part] = 1.0
            oneq[0, base + 3 + part] = 1.0
        onev[0, hh * V_PAD + FOX_HEAD_DIM] = 1.0
    sel = np.zeros((FOX_HEADS * HEAD_PAD, LANES), np.float32)
    for hh in range(FOX_HEADS):
        sel[hh * HEAD_PAD:hh * HEAD_PAD + FOX_HEAD_DIM, hh] = 1.0
    return (jnp.asarray(tri, BF16), jnp.asarray(eq, BF16), jnp.asarray(ek, BF16),
            jnp.asarray(oneq), jnp.asarray(onek), jnp.asarray(onev), jnp.asarray(sel, BF16))


def _chunk_cols(a):
    lead = a.shape[:-1]
    a = a.reshape(lead + (2 * N_FFN_CHUNKS, FFN_CHUNK))
    return jnp.moveaxis(a, -2, 0)


def _tile(n, pref):
    t = min(n, pref)
    while n % t:
        t //= 2
    return t


def kernel(x_prompt, x_sample, cache_fox_k, cache_fox_v, cache_fox_logf, state_ret, state_ffn_conv,
           norm_mix_g, w_in, b_fox_f, gn_ret_g, w_pa, w_pb, w_o, norm_ffn_g, w_up, conv_w, conv_b,
           w_down, norm_final_g):
    depth = w_in.shape[0]
    Bp, S, _ = x_prompt.shape
    Bs, Ts, _ = x_sample.shape
    P = cache_fox_k.shape[2]
    assert depth == 1 and Bp == 1, "kernel handles the single-layer, single-prompt configuration"
    l = 0

    wt = jnp.swapaxes(w_in[l], 0, 1).astype(BF16)
    o0 = 3 * FOX_W
    o1 = o0 + FOX_HEADS
    o2 = o1 + 2 * RET_KW + RET_VW
    wq_aug = _pad_heads(wt[:FOX_W], HEAD_PAD).T
    wk_aug = _pad_heads(wt[FOX_W:2 * FOX_W], HEAD_PAD).T
    wv_aug = _pad_heads(wt[2 * FOX_W:o0], V_PAD).T
    wkv = wt[FOX_W:o0].T
    wqkv = wt[:o0].T
    wf = jnp.pad(wt[o0:o1], ((0, LANES - FOX_HEADS), (0, 0))).T
    bf = jnp.pad(b_fox_f[l].astype(F32), (0, LANES - FOX_HEADS))[None, :]
    wb = wt[o1:o2].T
    wg = wt[o2:].T
    gmix = norm_mix_g[l].astype(F32)[None, :]
    mix_weights = (
        gmix, wg, gn_ret_g[l].astype(F32)[None, :], w_pa[l].astype(BF16), w_pb[l].astype(BF16),
        w_o[l].astype(BF16), norm_ffn_g[l].astype(F32)[None, :],
        _chunk_cols(w_up[l]).astype(BF16),
        jnp.pad(_chunk_cols(conv_w[l].astype(F32)), ((0, 0), (0, 8 - CONV_WIDTH), (0, 0))),
        _chunk_cols(conv_b[l].astype(F32)[None, :]),
        w_down[l].reshape(N_FFN_CHUNKS, FFN_CHUNK, D_MODEL).astype(BF16),
        norm_final_g.astype(F32)[None, :],
    )

    tm_a = _tile(S, 256)
    cos_p, sin_p = _rotary_tables(jnp.arange(S))
    (qT, ka, vT, k_p, v_p, logf_p, qb, kb, vb, stats) = _inproj_prompt(
        x_prompt[0], gmix, wq_aug, wk_aug, wv_aug, wkv, wf, bf, wb, cos_p, sin_p,
        _prompt_consts(tm_a), tm_a)
    oa_p = _fox_prompt(qT, ka, vT, stats, _tile(S, 1024))
    zero_state = jnp.zeros((1, RET_HEADS, RET_KEY_DIM, RET_VAL_DIM), F32)
    nb_p, ret_p = _retention(qb, kb, vb, zero_state, 1, S, _tile(S, 256))
    tm_d = _tile(S, 256)
    zero_prev = jnp.zeros((2 * N_FFN_CHUNKS, 1, 8, FFN_CHUNK), F32)
    y_p, conv_p = _mixer_ffn(x_prompt[0], oa_p, nb_p, zero_prev, mix_weights, tm_d, 1, tm_d)

    Ms = Bs * Ts
    cos_s, sin_s = _rotary_tables(P + jnp.arange(Ts))
    cos_s = jnp.tile(cos_s, (Bs, 1))
    sin_s = jnp.tile(sin_s, (Bs, 1))
    (q_s, k_s, v_s, logf_s, qb_s, kb_s, vb_s) = _inproj_sample(
        x_sample.reshape(Ms, D_MODEL), gmix, wqkv, wf, bf, wb, cos_s, sin_s)
    KP = ((P + Ts + LANES - 1) // LANES) * LANES
    lf_all = jnp.concatenate([cache_fox_logf[l].astype(F32), logf_s.reshape(Bs, Ts, FOX_HEADS)], axis=1)
    lfT = jnp.pad(jnp.swapaxes(lf_all, 1, 2), ((0, 0), (0, 0), (0, KP - P - Ts)))
    oa_s = _fox_sample(q_s, k_s, v_s, jnp.transpose(cache_fox_k[l], (0, 2, 3, 1)),
                       jnp.transpose(cache_fox_v[l], (0, 2, 3, 1)), lfT, Bs, Ts)
    oa_s = jnp.moveaxis(oa_s.reshape(Ms, FOX_HEADS // 2, LANES), 1, 0)
    nb_s, ret_s = _retention(qb_s, kb_s, vb_s, state_ret[l].astype(F32), Bs, Ts, Ts)
    prev_s = _chunk_cols(state_ffn_conv[l].astype(F32))
    prev_s = jnp.pad(prev_s, ((0, 0), (0, 0), (8 - (CONV_WIDTH - 1), 0), (0, 0)))
    y_s, conv_s = _mixer_ffn(x_sample.reshape(Ms, D_MODEL), oa_s, nb_s, prev_s, mix_weights,
                             Ms, Bs, Ts)

    def unchunk(cv):
        return jnp.moveaxis(cv, 0, 2).reshape(cv.shape[1], CONV_WIDTH - 1, 2 * FFN_DIM)

    hshape = (FOX_HEADS, FOX_HEAD_DIM)
    return (
        y_p[None],
        y_s.reshape(Bs, Ts, D_MODEL),
        k_p.reshape((1, 1, S) + hshape),
        v_p.reshape((1, 1, S) + hshape),
        logf_p.reshape(1, 1, S, FOX_HEADS),
        ret_p[None],
        unchunk(conv_p)[None],
        k_s.reshape((1, Bs, Ts) + hshape),
        v_s.reshape((1, Bs, Ts) + hshape),
        logf_s.reshape(1, Bs, Ts, FOX_HEADS),
        ret_s[None],
        unchunk(conv_s)[None],
    )
```

```python
import functools
import math

import numpy as np
import jax
import jax.numpy as jnp
from jax import lax
from jax.experimental import pallas as pl
from jax.experimental.pallas import tpu as pltpu

F32 = jnp.float32
BF16 = jnp.bfloat16

D_MODEL = 1024
FOX_HEADS = 8
FOX_HEAD_DIM = 64
RET_HEADS = 4
RET_KEY_DIM = 128
RET_VAL_DIM = 256
FFN_DIM = 2816
CONV_WIDTH = 3
EPS = 1e-6
ROPE_BASE = 10000.0

FOX_W = FOX_HEADS * FOX_HEAD_DIM
RET_KW = RET_HEADS * RET_KEY_DIM
RET_VW = RET_HEADS * RET_VAL_DIM

LOG2E = 1.4426950408889634
LANES = 128
HEAD_PAD = LANES
V_PAD = 80
BIAS_COL = FOX_HEAD_DIM
NEG = -1e30
STALE_MAX_GUARD = 2.0 ** 60
PRUNE_LOG2 = -160.0
NORM_SLACK = 1.02
FFN_CHUNK = 256
N_FFN_CHUNKS = FFN_DIM // FFN_CHUNK
VMEM_LIMIT = 56 * 1024 * 1024


def _rmsnorm(x, g):
    ms = jnp.mean(x * x, axis=-1, keepdims=True)
    return x * lax.rsqrt(ms + EPS) * g


def _split3(x):
    hi = x.astype(BF16)
    r1 = x - hi.astype(F32)
    mid = r1.astype(BF16)
    lo = (r1 - mid.astype(F32)).astype(BF16)
    return hi, mid, lo


def _log_sigmoid(x):
    return jnp.minimum(x, 0.0) - jnp.log1p(jnp.exp(-jnp.abs(x)))


def _dot(a, b):
    return jnp.dot(a, b, preferred_element_type=F32)


def _rotary(x, cos2, sin2):
    return x * cos2 + pltpu.roll(x, RET_KEY_DIM // 2, 1) * sin2


def _const_spec(shape):
    n = len(shape)
    return pl.BlockSpec(shape, lambda *_: (0,) * n)


def _inproj_prompt_kernel(x_ref, g_ref, wq_ref, wk_ref, wv_ref, wkv_ref, wf_ref, bf_ref, wb_ref,
                          cos_ref, sin_ref, tri_ref, eq_ref, ek_ref, oneq_ref, onek_ref, onev_ref,
                          sel_ref,
                          qT_ref, ka_ref, vT_ref, k32_ref, v32_ref, logf_ref, qb_ref, kb_ref, vb_ref,
                          stats_ref, carry_ref):
    tm = x_ref.shape[0]

    @pl.when(pl.program_id(0) == 0)
    def _():
        carry_ref[...] = jnp.zeros_like(carry_ref)

    h = _rmsnorm(x_ref[...], g_ref[...]).astype(BF16)

    logf = _log_sigmoid(_dot(h, wf_ref[...]) + bf_ref[...])
    logf_ref[...] = logf[:, :FOX_HEADS]
    lane = lax.broadcasted_iota(jnp.int32, logf.shape, 1)
    logf = jnp.where(lane < FOX_HEADS, logf, 0.0)
    r = _dot(tri_ref[...], jnp.concatenate(_split3(logf), axis=1))
    c = r[:, :LANES] + r[:, LANES:2 * LANES] + r[:, 2 * LANES:] + carry_ref[...]
    carry_ref[...] = c[tm - 1:tm, :]
    c3 = jnp.concatenate(_split3(c * LOG2E), axis=1)

    q_aug = (_dot(h, wq_ref[...]) * (FOX_HEAD_DIM ** -0.5 * LOG2E)
             + _dot(c3, eq_ref[...]) + oneq_ref[...])
    qT = q_aug.T.astype(BF16)
    k_aug = _dot(h, wk_ref[...]) + _dot(c3, ek_ref[...]) + onek_ref[...]

    sel = sel_ref[...]
    nq2 = jnp.max(_dot((q_aug * q_aug).astype(BF16), sel), axis=0, keepdims=True)
    nk2 = jnp.max(_dot((k_aug * k_aug).astype(BF16), sel), axis=0, keepdims=True)
    c2 = c * LOG2E
    stats_ref[0] = jnp.concatenate(
        [nq2, nk2, c2[0:1, :], c2[tm - 1:tm, :], jnp.zeros((4, LANES), F32)], axis=0)

    k_aug = k_aug.astype(BF16)
    vT = (_dot(h, wv_ref[...]) + onev_ref[...]).T.astype(BF16)
    for hh in range(FOX_HEADS):
        qT_ref[hh] = qT[hh * HEAD_PAD:(hh + 1) * HEAD_PAD, :]
        ka_ref[hh] = k_aug[:, hh * HEAD_PAD:(hh + 1) * HEAD_PAD]
        vT_ref[hh] = vT[hh * V_PAD:(hh + 1) * V_PAD, :]

    zkv = _dot(h, wkv_ref[...])
    k32_ref[...] = zkv[:, :FOX_W]
    v32_ref[...] = zkv[:, FOX_W:]

    zb = _dot(h, wb_ref[...])
    cos2 = cos_ref[...]
    sin2 = sin_ref[...]
    for hh in range(RET_HEADS):
        sl = slice(hh * RET_KEY_DIM, (hh + 1) * RET_KEY_DIM)
        qb_ref[:, sl] = _rotary(zb[:, sl], cos2, sin2).astype(BF16)
        xk = zb[:, RET_KW + hh * RET_KEY_DIM:RET_KW + (hh + 1) * RET_KEY_DIM]
        kb_ref[:, sl] = (_rotary(xk, cos2, sin2) * (RET_KEY_DIM ** -0.5)).astype(BF16)
    vb_ref[...] = zb[:, 2 * RET_KW:].astype(BF16)


def _inproj_prompt(x, g, wq, wk, wv, wkv, wf, bf, wb, cos2, sin2, consts, tm):
    S = x.shape[0]
    tri, eq, ek, oneq, onek, onev, sel = consts
    row = lambda w: pl.BlockSpec((tm, w), lambda i: (i, 0))
    in_specs = [row(D_MODEL), _const_spec(g.shape), _const_spec(wq.shape), _const_spec(wk.shape),
                _const_spec(wv.shape), _const_spec(wkv.shape), _const_spec(wf.shape),
                _const_spec(bf.shape), _const_spec(wb.shape), row(LANES), row(LANES),
                _const_spec(tri.shape), _const_spec(eq.shape), _const_spec(ek.shape),
                _const_spec(oneq.shape), _const_spec(onek.shape), _const_spec(onev.shape),
                _const_spec(sel.shape)]
    out_shape = (
        jax.ShapeDtypeStruct((FOX_HEADS, HEAD_PAD, S), BF16),
        jax.ShapeDtypeStruct((FOX_HEADS, S, HEAD_PAD), BF16),
        jax.ShapeDtypeStruct((FOX_HEADS, V_PAD, S), BF16),
        jax.ShapeDtypeStruct((S, FOX_W), F32),
        jax.ShapeDtypeStruct((S, FOX_W), F32),
        jax.ShapeDtypeStruct((S, FOX_HEADS), F32),
        jax.ShapeDtypeStruct((S, RET_KW), BF16),
        jax.ShapeDtypeStruct((S, RET_KW), BF16),
        jax.ShapeDtypeStruct((S, RET_VW), BF16),
        jax.ShapeDtypeStruct((S // tm, 8, LANES), F32),
    )
    out_specs = (
        pl.BlockSpec((FOX_HEADS, HEAD_PAD, tm), lambda i: (0, 0, i)),
        pl.BlockSpec((FOX_HEADS, tm, HEAD_PAD), lambda i: (0, i, 0)),
        pl.BlockSpec((FOX_HEADS, V_PAD, tm), lambda i: (0, 0, i)),
        row(FOX_W), row(FOX_W), row(FOX_HEADS), row(RET_KW), row(RET_KW), row(RET_VW),
        pl.BlockSpec((1, 8, LANES), lambda i: (i, 0, 0)),
    )
    return pl.pallas_call(
        _inproj_prompt_kernel,
        grid=(S // tm,),
        in_specs=in_specs,
        out_specs=out_specs,
        out_shape=out_shape,
        scratch_shapes=[pltpu.VMEM((1, LANES), F32)],
        compiler_params=pltpu.CompilerParams(dimension_semantics=("arbitrary",),
                                             vmem_limit_bytes=VMEM_LIMIT),
        name="inproj_prompt",
    )(x, g, wq, wk, wv, wkv, wf, bf, wb, cos2, sin2, tri, eq, ek, oneq, onek, onev, sel)


def _inproj_sample_kernel(x_ref, g_ref, wqkv_ref, wf_ref, bf_ref, wb_ref, cos_ref, sin_ref,
                          q_ref, k32_ref, v32_ref, logf_ref, qb_ref, kb_ref, vb_ref):
    h = _rmsnorm(x_ref[...], g_ref[...]).astype(BF16)
    logf = _log_sigmoid(_dot(h, wf_ref[...]) + bf_ref[...])
    logf_ref[...] = logf[:, :FOX_HEADS]
    z = _dot(h, wqkv_ref[...])
    q_ref[...] = (z[:, :FOX_W] * (FOX_HEAD_DIM ** -0.5 * LOG2E)).astype(BF16)
    k32_ref[...] = z[:, FOX_W:2 * FOX_W]
    v32_ref[...] = z[:, 2 * FOX_W:]
    zb = _dot(h, wb_ref[...])
    cos2 = cos_ref[...]
    sin2 = sin_ref[...]
    for hh in range(RET_HEADS):
        sl = slice(hh * RET_KEY_DIM, (hh + 1) * RET_KEY_DIM)
        qb_ref[:, sl] = _rotary(zb[:, sl], cos2, sin2).astype(BF16)
        xk = zb[:, RET_KW + hh * RET_KEY_DIM:RET_KW + (hh + 1) * RET_KEY_DIM]
        kb_ref[:, sl] = (_rotary(xk, cos2, sin2) * (RET_KEY_DIM ** -0.5)).astype(BF16)
    vb_ref[...] = zb[:, 2 * RET_KW:].astype(BF16)


def _inproj_sample(x, g, wqkv, wf, bf, wb, cos2, sin2):
    M = x.shape[0]
    args = (x, g, wqkv, wf, bf, wb, cos2, sin2)
    out_shape = (
        jax.ShapeDtypeStruct((M, FOX_W), BF16),
        jax.ShapeDtypeStruct((M, FOX_W), F32),
        jax.ShapeDtypeStruct((M, FOX_W), F32),
        jax.ShapeDtypeStruct((M, FOX_HEADS), F32),
        jax.ShapeDtypeStruct((M, RET_KW), BF16),
        jax.ShapeDtypeStruct((M, RET_KW), BF16),
        jax.ShapeDtypeStruct((M, RET_VW), BF16),
    )
    return pl.pallas_call(
        _inproj_sample_kernel,
        grid=(1,),
        in_specs=[_const_spec(a.shape) for a in args],
        out_specs=tuple(_const_spec(o.shape) for o in out_shape),
        out_shape=out_shape,
        compiler_params=pltpu.CompilerParams(dimension_semantics=("arbitrary",),
                                             vmem_limit_bytes=VMEM_LIMIT),
        name="inproj_sample",
    )(*args)


def _fox_prompt_kernel(it_ref, jt_ref, jfetch_ref, live_ref, qT_ref, ka_ref, vT_ref, o_ref,
                       m_ref, acc_ref, pv_ref):
    del jfetch_ref
    t = pl.program_id(0)
    i = it_ref[t]
    j = jt_ref[t]
    T = qT_ref.shape[2]

    def heads(fn, fn_pruned=None):
        def body(hh, carry):
            is_live = live_ref[t * FOX_HEADS + hh] != 0

            @pl.when(is_live)
            def _():
                fn(hh)

            if fn_pruned is not None:
                @pl.when(jnp.logical_not(is_live))
                def _():
                    fn_pruned(hh)
            return carry
        lax.fori_loop(0, FOX_HEADS, body, 0)

    def scores(hh):
        return _dot(ka_ref[hh], qT_ref[hh])

    def exact_head(hh):
        s = scores(hh)
        kk = lax.broadcasted_iota(jnp.int32, s.shape, 0) + j * T
        qq = lax.broadcasted_iota(jnp.int32, s.shape, 1) + i * T
        s = jnp.where(kk > qq, NEG, s)
        m_old = m_ref[hh]
        m_new = jnp.maximum(m_old, jnp.max(s, axis=0, keepdims=True))
        p = jnp.exp2(s - m_new).astype(BF16)
        alpha = jnp.exp2(m_old - m_new)
        acc_ref[hh] = alpha * acc_ref[hh] + _dot(vT_ref[hh], p)
        m_ref[hh] = m_new

    def stale_head(hh):
        p = jnp.exp2(scores(hh) - m_ref[hh]).astype(BF16)
        pv_ref[hh] = _dot(vT_ref[hh], p)

    def pruned_head(hh):
        pv_ref[hh, FOX_HEAD_DIM:FOX_HEAD_DIM + 1, :] = jnp.zeros((1, T), F32)

    def commit_head(hh):
        acc_ref[hh] += pv_ref[hh]

    def stale_step():
        heads(stale_head, pruned_head)
        lsum = pv_ref[:, FOX_HEAD_DIM:FOX_HEAD_DIM + 1, :]
        bad = jnp.sum(jnp.where(lsum < STALE_MAX_GUARD, 0.0, 1.0)) > 0.0

        @pl.when(jnp.logical_not(bad))
        def _():
            heads(commit_head)

        @pl.when(bad)
        def _():
            heads(exact_head)

    @pl.when(j == i)
    def _():
        m_ref[...] = jnp.full_like(m_ref, NEG)
        acc_ref[...] = jnp.zeros_like(acc_ref)
        heads(exact_head)

    @pl.when(j < i)
    def _():
        stale_step()

    @pl.when(j == 0)
    def _():
        for pr in range(FOX_HEADS // 2):
            halves = []
            for hh in (2 * pr, 2 * pr + 1):
                a = acc_ref[hh]
                halves.append(a[:FOX_HEAD_DIM] / a[FOX_HEAD_DIM:FOX_HEAD_DIM + 1])
            o_ref[pr] = jnp.concatenate(halves, axis=0).T.astype(BF16)


def _prune_tables(stats, it, jt, nb):
    per = stats.shape[0] // nb
    st = stats.reshape(nb, per, 8, LANES)[:, :, :, :FOX_HEADS]
    nq = jnp.sqrt(jnp.max(st[:, :, 0, :], axis=1)) * NORM_SLACK
    nk = jnp.sqrt(jnp.max(st[:, :, 1, :], axis=1)) * NORM_SLACK
    c_first = st[:, 0, 2, :]
    c_last = st[:, per - 1, 3, :]
    bound = (nq[it] * (nk[jt] + nk[it])) - (c_last[jt] - c_first[it])
    live = jnp.logical_or(jnp.asarray(jt == it)[:, None], jnp.logical_not(bound < PRUNE_LOG2))
    steps = jnp.arange(len(it), dtype=jnp.int32)
    last_live = lax.cummax(jnp.where(jnp.any(live, axis=1), steps, 0))
    return jnp.asarray(jt)[last_live], live.astype(jnp.int32).reshape(-1)


def _fox_prompt(qT, ka, vT, stats, T):
    S = ka.shape[1]
    nb = S // T
    it = np.array([i for i in range(nb) for _ in range(i + 1)], np.int32)
    jt = np.array([j for i in range(nb) for j in range(i, -1, -1)], np.int32)
    jfetch, live = _prune_tables(stats, it, jt, nb)
    grid_spec = pltpu.PrefetchScalarGridSpec(
        num_scalar_prefetch=4,
        grid=(len(it),),
        in_specs=[
            pl.BlockSpec((FOX_HEADS, HEAD_PAD, T), lambda t, it, jt, jf, lv: (0, 0, it[t])),
            pl.BlockSpec((FOX_HEADS, T, HEAD_PAD), lambda t, it, jt, jf, lv: (0, jf[t], 0)),
            pl.BlockSpec((FOX_HEADS, V_PAD, T), lambda t, it, jt, jf, lv: (0, 0, jf[t])),
        ],
        out_specs=pl.BlockSpec((FOX_HEADS // 2, T, LANES),
                               lambda t, it, jt, jf, lv: (0, it[t], 0)),
        scratch_shapes=[pltpu.VMEM((FOX_HEADS, 1, T), F32),
                        pltpu.VMEM((FOX_HEADS, V_PAD, T), F32),
                        pltpu.VMEM((FOX_HEADS, V_PAD, T), F32)],
    )
    return pl.pallas_call(
        _fox_prompt_kernel,
        grid_spec=grid_spec,
        out_shape=jax.ShapeDtypeStruct((FOX_HEADS // 2, S, LANES), BF16),
        compiler_params=pltpu.CompilerParams(dimension_semantics=("arbitrary",),
                                             vmem_limit_bytes=VMEM_LIMIT),
        name="fox_prompt",
    )(jnp.asarray(it), jnp.asarray(jt), jfetch, live, qT, ka, vT)


def _fox_sample_kernel(q_ref, kn_ref, vn_ref, ckT_ref, cvT_ref, lfT_ref, up_ref, ex_ref, o_ref):
    P = ckT_ref.shape[3]
    Tn = q_ref.shape[0]
    KP = lfT_ref.shape[2]
    HQ = FOX_HEADS * Tn
    nchunk = KP // LANES
    nt = (((1,), (1,)), ((), ()))

    def stack3(x):
        parts3 = [t.astype(F32) for t in _split3(x)] + [jnp.zeros_like(x)]
        return jnp.concatenate(parts3, axis=0).astype(BF16)

    x3 = stack3(lfT_ref[0])
    up = up_ref[...]
    run = jnp.zeros((4 * FOX_HEADS, 1), F32)
    parts = []
    for cidx in range(nchunk):
        y = _dot(x3[:, cidx * LANES:(cidx + 1) * LANES], up) + run
        parts.append(y)
        run = y[:, LANES - 1:LANES]
    y = jnp.concatenate(parts, axis=1)
    cT = (y[:FOX_HEADS] + y[FOX_HEADS:2 * FOX_HEADS] + y[2 * FOX_HEADS:3 * FOX_HEADS]) * LOG2E
    ckx = _dot(ex_ref[...], stack3(cT))

    tail = ckx[:, P:P + LANES]
    rowq = lax.broadcasted_iota(jnp.int32, tail.shape, 0) % Tn
    lanek = lax.broadcasted_iota(jnp.int32, tail.shape, 1)
    cq = jnp.sum(jnp.where(lanek == rowq, tail, 0.0), axis=1, keepdims=True)

    q = q_ref[...]
    qt = jnp.concatenate([q] * FOX_HEADS, axis=0)
    rh = lax.broadcasted_iota(jnp.int32, qt.shape, 0) // Tn
    lh = lax.broadcasted_iota(jnp.int32, qt.shape, 1) // FOX_HEAD_DIM
    qbd = jnp.where(rh == lh, qt, jnp.zeros_like(qt))

    kT = ckT_ref[0].reshape(FOX_W, P).astype(BF16)
    vT = cvT_ref[0].reshape(FOX_W, P).astype(BF16)
    s_c = _dot(qbd, kT) + cq - ckx[:, :P]
    s_n = lax.dot_general(qbd, kn_ref[...].astype(BF16), nt, preferred_element_type=F32)
    s_n = s_n + cq - ckx[:, P:P + Tn]
    key = lax.broadcasted_iota(jnp.int32, s_n.shape, 1)
    qrow = lax.broadcasted_iota(jnp.int32, s_n.shape, 0) % Tn
    s_n = jnp.where(key > qrow, NEG, s_n)
    m = jnp.maximum(jnp.max(s_c, axis=1, keepdims=True), jnp.max(s_n, axis=1, keepdims=True))
    p_c = jnp.exp2(s_c - m)
    p_n = jnp.exp2(s_n - m)
    l = jnp.sum(p_c, axis=1, keepdims=True) + jnp.sum(p_n, axis=1, keepdims=True)
    z = lax.dot_general(p_c.astype(BF16), vT, nt, preferred_element_type=F32)
    z = (z + _dot(p_n.astype(BF16), vn_ref[...].astype(BF16))) / l
    zh = lax.broadcasted_iota(jnp.int32, (Tn, FOX_W), 1) // FOX_HEAD_DIM
    o = jnp.zeros((Tn, FOX_W), F32)
    for hh in range(FOX_HEADS):
        o = o + jnp.where(zh == hh, z[hh * Tn:(hh + 1) * Tn, :], 0.0)
    o_ref[...] = o.astype(BF16)


def _fox_sample(q, kn, vn, cache_kT, cache_vT, lfT, B, Tn):
    P = cache_kT.shape[3]
    KP = lfT.shape[2]
    HQ = FOX_HEADS * Tn
    up = jnp.asarray(np.triu(np.ones((LANES, LANES), np.float32)), BF16)
    ex = np.zeros((HQ, 4 * FOX_HEADS), np.float32)
    for part in range(3):
        for hh in range(FOX_HEADS):
            ex[hh * Tn:(hh + 1) * Tn, part * FOX_HEADS + hh] = 1.0
    ex = jnp.asarray(ex, BF16)
    rowb = lambda w: pl.BlockSpec((Tn, w), lambda b: (b, 0))
    return pl.pallas_call(
        _fox_sample_kernel,
        grid=(B,),
        in_specs=[rowb(FOX_W), rowb(FOX_W), rowb(FOX_W),
                  pl.BlockSpec((1, FOX_HEADS, FOX_HEAD_DIM, P), lambda b: (b, 0, 0, 0)),
                  pl.BlockSpec((1, FOX_HEADS, FOX_HEAD_DIM, P), lambda b: (b, 0, 0, 0)),
                  pl.BlockSpec((1, FOX_HEADS, KP), lambda b: (b, 0, 0)),
                  _const_spec(up.shape), _const_spec(ex.shape)],
        out_specs=rowb(FOX_W),
        out_shape=jax.ShapeDtypeStruct((B * Tn, FOX_W), BF16),
        compiler_params=pltpu.CompilerParams(dimension_semantics=("arbitrary",),
                                             vmem_limit_bytes=VMEM_LIMIT),
        name="fox_sample",
    )(q, kn, vn, cache_kT, cache_vT, lfT, up, ex)


def _retention_kernel(q_ref, k_ref, v_ref, s0_ref, dmat_ref, xi_ref, zeta_ref, gam_ref,
                      n_ref, sout_ref, st_ref):
    c = pl.program_id(1)

    @pl.when(c == 0)
    def _():
        st_ref[...] = s0_ref[0]

    for hh in range(RET_HEADS):
        q = q_ref[:, hh * RET_KEY_DIM:(hh + 1) * RET_KEY_DIM]
        k = k_ref[:, hh * RET_KEY_DIM:(hh + 1) * RET_KEY_DIM]
        v = v_ref[:, hh * RET_VAL_DIM:(hh + 1) * RET_VAL_DIM]
        st = st_ref[hh]
        sc = lax.dot_general(q, k, (((1,), (1,)), ((), ())), preferred_element_type=F32)
        sc = sc * dmat_ref[hh]
        o = _dot(sc.astype(BF16), v) + _dot(q, st.astype(BF16)) * xi_ref[hh]
        kz = (k.astype(F32) * zeta_ref[hh]).astype(BF16)
        upd = lax.dot_general(kz, v, (((0,), (0,)), ((), ())), preferred_element_type=F32)
        st_ref[hh] = gam_ref[hh] * st + upd
        mu = jnp.mean(o, axis=-1, keepdims=True)
        d = o - mu
        var = jnp.mean(d * d, axis=-1, keepdims=True)
        n_ref[:, hh * RET_VAL_DIM:(hh + 1) * RET_VAL_DIM] = (d * lax.rsqrt(var + EPS)).astype(BF16)

    @pl.when(c == pl.num_programs(1) - 1)
    def _():
        sout_ref[0] = st_ref[...]


def _ret_log_gamma():
    return jnp.log(1.0 - jnp.exp2(-5.0 - jnp.arange(RET_HEADS, dtype=F32)))


def _retention(q, k, v, state0, B, L, C):
    nc = L // C
    lg = _ret_log_gamma()
    idx = jnp.arange(C, dtype=F32)
    diff = idx[:, None] - idx[None, :]
    dmat = jnp.where(diff[None] >= 0, jnp.exp(jnp.maximum(diff, 0.0)[None] * lg[:, None, None]), 0.0)
    xi = jnp.exp((idx[None, :] + 1.0) * lg[:, None])
    zeta = jnp.exp((C - 1.0 - idx[None, :]) * lg[:, None])
    xi = jnp.broadcast_to(xi[:, :, None], (RET_HEADS, C, RET_VAL_DIM))
    zeta = jnp.broadcast_to(zeta[:, :, None], (RET_HEADS, C, RET_KEY_DIM))
    gam = jnp.broadcast_to(jnp.exp(C * lg)[:, None, None], (RET_HEADS, 1, RET_VAL_DIM))
    rowc = lambda w: pl.BlockSpec((C, w), lambda b, c: (b * nc + c, 0))
    st_spec = pl.BlockSpec((1, RET_HEADS, RET_KEY_DIM, RET_VAL_DIM), lambda b, c: (b, 0, 0, 0))
    return pl.pallas_call(
        _retention_kernel,
        grid=(B, nc),
        in_specs=[rowc(RET_KW), rowc(RET_KW), rowc(RET_VW), st_spec,
                  _const_spec(dmat.shape), _const_spec(xi.shape), _const_spec(zeta.shape),
                  _const_spec(gam.shape)],
        out_specs=(rowc(RET_VW), st_spec),
        out_shape=(jax.ShapeDtypeStruct((B * L, RET_VW), BF16),
                   jax.ShapeDtypeStruct((B, RET_HEADS, RET_KEY_DIM, RET_VAL_DIM), F32)),
        scratch_shapes=[pltpu.VMEM((RET_HEADS, RET_KEY_DIM, RET_VAL_DIM), F32)],
        compiler_params=pltpu.CompilerParams(dimension_semantics=("arbitrary", "arbitrary"),
                                             vmem_limit_bytes=VMEM_LIMIT),
        name="retention",
    )(q, k, v, state0, dmat, xi, zeta, gam)


def _mixer_ffn_kernel(x_ref, oa_ref, nb_ref, prev_ref, gmix_ref, wg_ref, gng_ref, wpa_ref, wpb_ref,
                      wo_ref, gffn_ref, wup_ref, cw_ref, cb_ref, wdn_ref, gfin_ref,
                      y_ref, conv_ref, carry_ref, ua_ref, ub_ref, acc_ref, h2_ref,
                      *, nseg, seglen):
    i = pl.program_id(0)
    NC = N_FFN_CHUNKS
    PADR = 8
    H0 = PADR - (CONV_WIDTH - 1)

    @pl.when(i == 0)
    def _():
        carry_ref[...] = prev_ref[...]

    x = x_ref[...]
    h = _rmsnorm(x, gmix_ref[...]).astype(BF16)
    zg = _dot(h, wg_ref[...])
    gb = zg[:, :RET_VW]
    gma = zg[:, RET_VW:RET_VW + D_MODEL]
    gmb = zg[:, RET_VW + D_MODEL:]
    oa = jnp.concatenate([oa_ref[p] for p in range(FOX_HEADS // 2)], axis=1)
    ya = _dot(oa, wpa_ref[...])
    nn = nb_ref[...].astype(F32) * gng_ref[...] * (gb * jax.nn.sigmoid(gb))
    yb = _dot(nn.astype(BF16), wpb_ref[...])
    y = jax.nn.sigmoid(gma) * ya + jax.nn.sigmoid(gmb) * yb
    x1 = x + _dot(y.astype(BF16), wo_ref[...])
    h2_ref[...] = _rmsnorm(x1, gffn_ref[...]).astype(BF16)
    acc_ref[...] = x1

    h2 = h2_ref[...]

    def up_half(u_ref, cidx, slot):
        u = _dot(h2, wup_ref[cidx])
        for s in range(nseg):
            u_ref[slot, s, PADR:PADR + seglen, :] = u[s * seglen:(s + 1) * seglen, :]
            u_ref[slot, s, H0:PADR, :] = carry_ref[cidx, s, H0:PADR, :]
            carry_ref[cidx, s, H0:PADR, :] = u[(s + 1) * seglen - (CONV_WIDTH - 1):(s + 1) * seglen, :]

    def conv_half(u_ref, cidx, slot):
        w = cw_ref[cidx]
        b = cb_ref[cidx]
        outs = []
        for s in range(nseg):
            acc = w[0:1] * u_ref[slot, s, H0:H0 + seglen, :]
            for jj in range(1, CONV_WIDTH):
                acc = acc + w[jj:jj + 1] * u_ref[slot, s, H0 + jj:H0 + jj + seglen, :]
            outs.append(b + acc)
        return outs[0] if nseg == 1 else jnp.concatenate(outs, axis=0)

    def stage_up(c):
        up_half(ua_ref, c, c % 2)
        up_half(ub_ref, NC + c, c % 2)

    stage_up(0)
    for c in range(NC):
        if c + 1 < NC:
            stage_up(c + 1)
        a = conv_half(ua_ref, c, c % 2)
        b = conv_half(ub_ref, NC + c, c % 2)
        g = (jax.nn.gelu(a) * b).astype(BF16)
        acc_ref[...] += _dot(g, wdn_ref[c])
    y_ref[...] = _rmsnorm(acc_ref[...], gfin_ref[...])

    @pl.when(i == pl.num_programs(0) - 1)
    def _():
        conv_ref[...] = carry_ref[:, :, H0:PADR, :]


def _mixer_ffn(x, oa, nb, prev, weights, tm, nseg, seglen):
    M = x.shape[0]
    gmix, wg, gng, wpa, wpb, wo, gffn, wup, cw, cb, wdn, gfin = weights
    FC = FFN_CHUNK
    NC = N_FFN_CHUNKS
    row = lambda w: pl.BlockSpec((tm, w), lambda i: (i, 0))
    wspec = lambda a: pl.BlockSpec(a.shape, lambda i, n=a.ndim: (0,) * n,
                                   pipeline_mode=pl.Buffered(1))
    in_specs = [row(D_MODEL),
                pl.BlockSpec((FOX_HEADS // 2, tm, LANES), lambda i: (0, i, 0)),
                row(RET_VW), wspec(prev)] + [wspec(w) for w in weights]
    out_shape = (jax.ShapeDtypeStruct((M, D_MODEL), F32),
                 jax.ShapeDtypeStruct((2 * NC, nseg, CONV_WIDTH - 1, FC), F32))
    out_specs = (row(D_MODEL), _const_spec(out_shape[1].shape))
    return pl.pallas_call(
        functools.partial(_mixer_ffn_kernel, nseg=nseg, seglen=seglen),
        grid=(M // tm,),
        in_specs=in_specs,
        out_specs=out_specs,
        out_shape=out_shape,
        scratch_shapes=[pltpu.VMEM((2 * NC, nseg, 8, FC), F32),
                        pltpu.VMEM((2, nseg, 8 + seglen, FC), F32),
                        pltpu.VMEM((2, nseg, 8 + seglen, FC), F32),
                        pltpu.VMEM((tm, D_MODEL), F32),
                        pltpu.VMEM((tm, D_MODEL), BF16)],
        compiler_params=pltpu.CompilerParams(dimension_semantics=("arbitrary",),
                                             vmem_limit_bytes=VMEM_LIMIT),
        name="mixer_ffn",
    )(x, oa, nb, prev, *weights)


def _rotary_tables(pos):
    half = RET_KEY_DIM // 2
    inv = 1.0 / (ROPE_BASE ** jnp.linspace(0.0, 1.0, half, dtype=F32))
    ang = pos.astype(F32)[:, None] * inv[None, :]
    cos = jnp.cos(ang)
    sin = jnp.sin(ang)
    return jnp.concatenate([cos, cos], axis=1), jnp.concatenate([-sin, sin], axis=1)


def _pad_heads(wt, pad):
    d = wt.shape[1]
    wt = wt.reshape(FOX_HEADS, FOX_HEAD_DIM, d)
    wt = jnp.pad(wt, ((0, 0), (0, pad - FOX_HEAD_DIM), (0, 0)))
    return wt.reshape(FOX_HEADS * pad, d)


def _prompt_consts(tm):
    tri = np.tril(np.ones((tm, tm), np.float32))
    eq = np.zeros((3 * LANES, FOX_HEADS * HEAD_PAD), np.float32)
    ek = np.zeros((3 * LANES, FOX_HEADS * HEAD_PAD), np.float32)
    oneq = np.zeros((1, FOX_HEADS * HEAD_PAD), np.float32)
    onek = np.zeros((1, FOX_HEADS * HEAD_PAD), np.float32)
    onev = np.zeros((1, FOX_HEADS * V_PAD), np.float32)
    for hh in range(FOX_HEADS):
        base = hh * HEAD_PAD + BIAS_COL
        for part in range(3):
            eq[part * LANES + hh, base + part] = 1.0
            ek[part * LANES + hh, base + 3 + part] = -1.0
            onek[0, base + part] = 1.0
            oneq[0, base + 3 + part] = 1.0
        onev[0, hh * V_PAD + FOX_HEAD_DIM] = 1.0
    sel = np.zeros((FOX_HEADS * HEAD_PAD, LANES), np.float32)
    for hh in range(FOX_HEADS):
        sel[hh * HEAD_PAD:hh * HEAD_PAD + FOX_HEAD_DIM, hh] = 1.0
    return (jnp.asarray(tri, BF16), jnp.asarray(eq, BF16), jnp.asarray(ek, BF16),
            jnp.asarray(oneq), jnp.asarray(onek), jnp.asarray(onev), jnp.asarray(sel, BF16))


def _chunk_cols(a):
    lead = a.shape[:-1]
    a = a.reshape(lead + (2 * N_FFN_CHUNKS, FFN_CHUNK))
    return jnp.moveaxis(a, -2, 0)


def _tile(n, pref):
    t = min(n, pref)
    while n % t:
        t //= 2
    return t


def kernel(x_prompt, x_sample, cache_fox_k, cache_fox_v, cache_fox_logf, state_ret, state_ffn_conv,
           norm_mix_g, w_in, b_fox_f, gn_ret_g, w_pa, w_pb, w_o, norm_ffn_g, w_up, conv_w, conv_b,
           w_down, norm_final_g):
    depth = w_in.shape[0]
    Bp, S, _ = x_prompt.shape
    Bs, Ts, _ = x_sample.shape
    P = cache_fox_k.shape[2]
    assert depth == 1 and Bp == 1, "kernel handles the single-layer, single-prompt configuration"
    l = 0

    wt = jnp.swapaxes(w_in[l], 0, 1).astype(BF16)
    o0 = 3 * FOX_W
    o1 = o0 + FOX_HEADS
    o2 = o1 + 2 * RET_KW + RET_VW
    wq_aug = _pad_heads(wt[:FOX_W], HEAD_PAD).T
    wk_aug = _pad_heads(wt[FOX_W:2 * FOX_W], HEAD_PAD).T
    wv_aug = _pad_heads(wt[2 * FOX_W:o0], V_PAD).T
    wkv = wt[FOX_W:o0].T
    wqkv = wt[:o0].T
    wf = jnp.pad(wt[o0:o1], ((0, LANES - FOX_HEADS), (0, 0))).T
    bf = jnp.pad(b_fox_f[l].astype(F32), (0, LANES - FOX_HEADS))[None, :]
    wb = wt[o1:o2].T
    wg = wt[o2:].T
    gmix = norm_mix_g[l].astype(F32)[None, :]
    mix_weights = (
        gmix, wg, gn_ret_g[l].astype(F32)[None, :], w_pa[l].astype(BF16), w_pb[l].astype(BF16),
        w_o[l].astype(BF16), norm_ffn_g[l].astype(F32)[None, :],
        _chunk_cols(w_up[l]).astype(BF16),
        jnp.pad(_chunk_cols(conv_w[l].astype(F32)), ((0, 0), (0, 8 - CONV_WIDTH), (0, 0))),
        _chunk_cols(conv_b[l].astype(F32)[None, :]),
        w_down[l].reshape(N_FFN_CHUNKS, FFN_CHUNK, D_MODEL).astype(BF16),
        norm_final_g.astype(F32)[None, :],
    )

    tm_a = _tile(S, 256)
    cos_p, sin_p = _rotary_tables(jnp.arange(S))
    (qT, ka, vT, k_p, v_p, logf_p, qb, kb, vb, stats) = _inproj_prompt(
        x_prompt[0], gmix, wq_aug, wk_aug, wv_aug, wkv, wf, bf, wb, cos_p, sin_p,
        _prompt_consts(tm_a), tm_a)
    oa_p = _fox_prompt(qT, ka, vT, stats, _tile(S, 512))
    zero_state = jnp.zeros((1, RET_HEADS, RET_KEY_DIM, RET_VAL_DIM), F32)
    nb_p, ret_p = _retention(qb, kb, vb, zero_state, 1, S, _tile(S, 256))
    tm_d = _tile(S, 256)
    zero_prev = jnp.zeros((2 * N_FFN_CHUNKS, 1, 8, FFN_CHUNK), F32)
    y_p, conv_p = _mixer_ffn(x_prompt[0], oa_p, nb_p, zero_prev, mix_weights, tm_d, 1, tm_d)

    Ms = Bs * Ts
    cos_s, sin_s = _rotary_tables(P + jnp.arange(Ts))
    cos_s = jnp.tile(cos_s, (Bs, 1))
    sin_s = jnp.tile(sin_s, (Bs, 1))
    (q_s, k_s, v_s, logf_s, qb_s, kb_s, vb_s) = _inproj_sample(
        x_sample.reshape(Ms, D_MODEL), gmix, wqkv, wf, bf, wb, cos_s, sin_s)
    KP = ((P + Ts + LANES - 1) // LANES) * LANES
    lf_all = jnp.concatenate([cache_fox_logf[l].astype(F32), logf_s.reshape(Bs, Ts, FOX_HEADS)], axis=1)
    lfT = jnp.pad(jnp.swapaxes(lf_all, 1, 2), ((0, 0), (0, 0), (0, KP - P - Ts)))
    oa_s = _fox_sample(q_s, k_s, v_s, jnp.transpose(cache_fox_k[l], (0, 2, 3, 1)),
                       jnp.transpose(cache_fox_v[l], (0, 2, 3, 1)), lfT, Bs, Ts)
    oa_s = jnp.moveaxis(oa_s.reshape(Ms, FOX_HEADS // 2, LANES), 1, 0)
    nb_s, ret_s = _retention(qb_s, kb_s, vb_s, state_ret[l].astype(F32), Bs, Ts, Ts)
    prev_s = _chunk_cols(state_ffn_conv[l].astype(F32))
    prev_s = jnp.pad(prev_s, ((0, 0), (0, 0), (8 - (CONV_WIDTH - 1), 0), (0, 0)))
    y_s, conv_s = _mixer_ffn(x_sample.reshape(Ms, D_MODEL), oa_s, nb_s, prev_s, mix_weights,
                             Ms, Bs, Ts)

    def unchunk(cv):
        return jnp.moveaxis(cv, 0, 2).reshape(cv.shape[1], CONV_WIDTH - 1, 2 * FFN_DIM)

    hshape = (FOX_HEADS, FOX_HEAD_DIM)
    return (
        y_p[None],
        y_s.reshape(Bs, Ts, D_MODEL),
        k_p.reshape((1, 1, S) + hshape),
        v_p.reshape((1, 1, S) + hshape),
        logf_p.reshape(1, 1, S, FOX_HEADS),
        ret_p[None],
        unchunk(conv_p)[None],
        k_s.reshape((1, Bs, Ts) + hshape),
        v_s.reshape((1, Bs, Ts) + hshape),
        logf_s.reshape(1, Bs, Ts, FOX_HEADS),
        ret_s[None],
        unchunk(conv_s)[None],
    )
```

```python
import functools
import math

import numpy as np
import jax
import jax.numpy as jnp
from jax import lax
from jax.experimental import pallas as pl
from jax.experimental.pallas import tpu as pltpu

F32 = jnp.float32
BF16 = jnp.bfloat16

D_MODEL = 1024
FOX_HEADS = 8
FOX_HEAD_DIM = 64
RET_HEADS = 4
RET_KEY_DIM = 128
RET_VAL_DIM = 256
FFN_DIM = 2816
CONV_WIDTH = 3
EPS = 1e-6
ROPE_BASE = 10000.0

FOX_W = FOX_HEADS * FOX_HEAD_DIM
RET_KW = RET_HEADS * RET_KEY_DIM
RET_VW = RET_HEADS * RET_VAL_DIM

LOG2E = 1.4426950408889634
LANES = 128
HEAD_PAD = LANES
V_PAD = 80
BIAS_COL = FOX_HEAD_DIM
NEG = -1e30
STALE_MAX_GUARD = 2.0 ** 60
PRUNE_LOG2 = -160.0
NORM_SLACK = 1.02
FFN_CHUNK = 256
N_FFN_CHUNKS = FFN_DIM // FFN_CHUNK
VMEM_LIMIT = 56 * 1024 * 1024


def _rmsnorm(x, g):
    ms = jnp.mean(x * x, axis=-1, keepdims=True)
    return x * lax.rsqrt(ms + EPS) * g


def _split3(x):
    hi = x.astype(BF16)
    r1 = x - hi.astype(F32)
    mid = r1.astype(BF16)
    lo = (r1 - mid.astype(F32)).astype(BF16)
    return hi, mid, lo


def _log_sigmoid(x):
    return jnp.minimum(x, 0.0) - jnp.log1p(jnp.exp(-jnp.abs(x)))


def _dot(a, b):
    return jnp.dot(a, b, preferred_element_type=F32)


def _rotary(x, cos2, sin2):
    return x * cos2 + pltpu.roll(x, RET_KEY_DIM // 2, 1) * sin2


def _const_spec(shape):
    n = len(shape)
    return pl.BlockSpec(shape, lambda *_: (0,) * n)


def _inproj_prompt_kernel(x_ref, g_ref, wq_ref, wk_ref, wv_ref, wf_ref, bf_ref, wb_ref,
                          cos_ref, sin_ref, tri_ref, eq_ref, ek_ref, oneq_ref, onek_ref, onev_ref,
                          sel_ref,
                          qT_ref, ka_ref, vT_ref, kT32_ref, vT32_ref, logf_ref, qb_ref, kb_ref, vb_ref,
                          stats_ref, sqq_ref, carry_ref):
    tm = x_ref.shape[0]

    @pl.when(pl.program_id(0) == 0)
    def _():
        carry_ref[...] = jnp.zeros_like(carry_ref)

    h = _rmsnorm(x_ref[...], g_ref[...]).astype(BF16)

    logf = _log_sigmoid(_dot(h, wf_ref[...]) + bf_ref[...])
    logf_ref[...] = logf.T[:FOX_HEADS, :]
    lane = lax.broadcasted_iota(jnp.int32, logf.shape, 1)
    logf = jnp.where(lane < FOX_HEADS, logf, 0.0)
    r = _dot(tri_ref[...], jnp.concatenate(_split3(logf), axis=1))
    c = r[:, :LANES] + r[:, LANES:2 * LANES] + r[:, 2 * LANES:] + carry_ref[...]
    carry_ref[...] = c[tm - 1:tm, :]
    c3 = jnp.concatenate(_split3(c * LOG2E), axis=1)

    q_aug = (_dot(h, wq_ref[...]) * (FOX_HEAD_DIM ** -0.5 * LOG2E)
             + _dot(c3, eq_ref[...]) + oneq_ref[...])
    qT = q_aug.T.astype(BF16)
    k_aug = _dot(h, wk_ref[...]) + _dot(c3, ek_ref[...]) + onek_ref[...]

    sel = sel_ref[...]
    nq2 = jnp.max(_dot((q_aug * q_aug).astype(BF16), sel), axis=0, keepdims=True)
    nk2 = jnp.max(_dot((k_aug * k_aug).astype(BF16), sel), axis=0, keepdims=True)
    c2 = c * LOG2E
    stats_ref[0] = jnp.concatenate(
        [nq2, nk2, c2[0:1, :], c2[tm - 1:tm, :], jnp.zeros((4, LANES), F32)], axis=0)
    sqq = jnp.where(lane < FOX_HEADS, _dot((q_aug * k_aug).astype(BF16), sel), 0.0)
    sqq_ref[...] = sqq.T[:FOX_HEADS, :]

    kT32 = k_aug.T
    k_aug = k_aug.astype(BF16)
    vT32 = (_dot(h, wv_ref[...]) + onev_ref[...]).T
    vT = vT32.astype(BF16)
    for hh in range(FOX_HEADS):
        qT_ref[hh] = qT[hh * HEAD_PAD:(hh + 1) * HEAD_PAD, :]
        ka_ref[hh] = k_aug[:, hh * HEAD_PAD:(hh + 1) * HEAD_PAD]
        vT_ref[hh] = vT[hh * V_PAD:(hh + 1) * V_PAD, :]
        kT32_ref[hh] = kT32[hh * HEAD_PAD:hh * HEAD_PAD + FOX_HEAD_DIM, :]
        vT32_ref[hh] = vT32[hh * V_PAD:hh * V_PAD + FOX_HEAD_DIM, :]

    zb = _dot(h, wb_ref[...])
    cos2 = cos_ref[...]
    sin2 = sin_ref[...]
    for hh in range(RET_HEADS):
        sl = slice(hh * RET_KEY_DIM, (hh + 1) * RET_KEY_DIM)
        qb_ref[:, sl] = _rotary(zb[:, sl], cos2, sin2).astype(BF16)
        xk = zb[:, RET_KW + hh * RET_KEY_DIM:RET_KW + (hh + 1) * RET_KEY_DIM]
        kb_ref[:, sl] = (_rotary(xk, cos2, sin2) * (RET_KEY_DIM ** -0.5)).astype(BF16)
    vb_ref[...] = zb[:, 2 * RET_KW:].astype(BF16)


def _inproj_prompt(x, g, wq, wk, wv, wf, bf, wb, cos2, sin2, consts, tm):
    S = x.shape[0]
    tri, eq, ek, oneq, onek, onev, sel = consts
    row = lambda w: pl.BlockSpec((tm, w), lambda i: (i, 0))
    headT = pl.BlockSpec((FOX_HEADS, FOX_HEAD_DIM, tm), lambda i: (0, 0, i))
    in_specs = [row(D_MODEL), _const_spec(g.shape), _const_spec(wq.shape), _const_spec(wk.shape),
                _const_spec(wv.shape), _const_spec(wf.shape),
                _const_spec(bf.shape), _const_spec(wb.shape), row(LANES), row(LANES),
                _const_spec(tri.shape), _const_spec(eq.shape), _const_spec(ek.shape),
                _const_spec(oneq.shape), _const_spec(onek.shape), _const_spec(onev.shape),
                _const_spec(sel.shape)]
    out_shape = (
        jax.ShapeDtypeStruct((FOX_HEADS, HEAD_PAD, S), BF16),
        jax.ShapeDtypeStruct((FOX_HEADS, S, HEAD_PAD), BF16),
        jax.ShapeDtypeStruct((FOX_HEADS, V_PAD, S), BF16),
        jax.ShapeDtypeStruct((FOX_HEADS, FOX_HEAD_DIM, S), F32),
        jax.ShapeDtypeStruct((FOX_HEADS, FOX_HEAD_DIM, S), F32),
        jax.ShapeDtypeStruct((FOX_HEADS, S), F32),
        jax.ShapeDtypeStruct((S, RET_KW), BF16),
        jax.ShapeDtypeStruct((S, RET_KW), BF16),
        jax.ShapeDtypeStruct((S, RET_VW), BF16),
        jax.ShapeDtypeStruct((S // tm, 8, LANES), F32),
        jax.ShapeDtypeStruct((FOX_HEADS, S), F32),
    )
    out_specs = (
        pl.BlockSpec((FOX_HEADS, HEAD_PAD, tm), lambda i: (0, 0, i)),
        pl.BlockSpec((FOX_HEADS, tm, HEAD_PAD), lambda i: (0, i, 0)),
        pl.BlockSpec((FOX_HEADS, V_PAD, tm), lambda i: (0, 0, i)),
        headT, headT, pl.BlockSpec((FOX_HEADS, tm), lambda i: (0, i)),
        row(RET_KW), row(RET_KW), row(RET_VW),
        pl.BlockSpec((1, 8, LANES), lambda i: (i, 0, 0)),
        pl.BlockSpec((FOX_HEADS, tm), lambda i: (0, i)),
    )
    return pl.pallas_call(
        _inproj_prompt_kernel,
        grid=(S // tm,),
        in_specs=in_specs,
        out_specs=out_specs,
        out_shape=out_shape,
        scratch_shapes=[pltpu.VMEM((1, LANES), F32)],
        compiler_params=pltpu.CompilerParams(dimension_semantics=("arbitrary",),
                                             vmem_limit_bytes=VMEM_LIMIT),
        name="inproj_prompt",
    )(x, g, wq, wk, wv, wf, bf, wb, cos2, sin2, tri, eq, ek, oneq, onek, onev, sel)


def _inproj_sample_kernel(x_ref, g_ref, wqkv_ref, wf_ref, bf_ref, wb_ref, cos_ref, sin_ref,
                          q_ref, k32_ref, v32_ref, logf_ref, qb_ref, kb_ref, vb_ref):
    h = _rmsnorm(x_ref[...], g_ref[...]).astype(BF16)
    logf = _log_sigmoid(_dot(h, wf_ref[...]) + bf_ref[...])
    logf_ref[...] = logf[:, :FOX_HEADS]
    z = _dot(h, wqkv_ref[...])
    q_ref[...] = (z[:, :FOX_W] * (FOX_HEAD_DIM ** -0.5 * LOG2E)).astype(BF16)
    k32_ref[...] = z[:, FOX_W:2 * FOX_W]
    v32_ref[...] = z[:, 2 * FOX_W:]
    zb = _dot(h, wb_ref[...])
    cos2 = cos_ref[...]
    sin2 = sin_ref[...]
    for hh in range(RET_HEADS):
        sl = slice(hh * RET_KEY_DIM, (hh + 1) * RET_KEY_DIM)
        qb_ref[:, sl] = _rotary(zb[:, sl], cos2, sin2).astype(BF16)
        xk = zb[:, RET_KW + hh * RET_KEY_DIM:RET_KW + (hh + 1) * RET_KEY_DIM]
        kb_ref[:, sl] = (_rotary(xk, cos2, sin2) * (RET_KEY_DIM ** -0.5)).astype(BF16)
    vb_ref[...] = zb[:, 2 * RET_KW:].astype(BF16)


def _inproj_sample(x, g, wqkv, wf, bf, wb, cos2, sin2):
    M = x.shape[0]
    args = (x, g, wqkv, wf, bf, wb, cos2, sin2)
    out_shape = (
        jax.ShapeDtypeStruct((M, FOX_W), BF16),
        jax.ShapeDtypeStruct((M, FOX_W), F32),
        jax.ShapeDtypeStruct((M, FOX_W), F32),
        jax.ShapeDtypeStruct((M, FOX_HEADS), F32),
        jax.ShapeDtypeStruct((M, RET_KW), BF16),
        jax.ShapeDtypeStruct((M, RET_KW), BF16),
        jax.ShapeDtypeStruct((M, RET_VW), BF16),
    )
    return pl.pallas_call(
        _inproj_sample_kernel,
        grid=(1,),
        in_specs=[_const_spec(a.shape) for a in args],
        out_specs=tuple(_const_spec(o.shape) for o in out_shape),
        out_shape=out_shape,
        compiler_params=pltpu.CompilerParams(dimension_semantics=("arbitrary",),
                                             vmem_limit_bytes=VMEM_LIMIT),
        name="inproj_sample",
    )(*args)


def _fox_prompt_kernel(it_ref, jt_ref, jfetch_ref, live_ref, qT_ref, ka_ref, vT_ref, sqq_ref, o_ref,
                       m_ref, acc_ref, pv_ref):
    del jfetch_ref
    t = pl.program_id(0)
    i = it_ref[t]
    j = jt_ref[t]
    T = qT_ref.shape[2]

    def heads(fn, fn_pruned=None):
        def body(hh, carry):
            is_live = live_ref[t * FOX_HEADS + hh] != 0

            @pl.when(is_live)
            def _():
                fn(hh)

            if fn_pruned is not None:
                @pl.when(jnp.logical_not(is_live))
                def _():
                    fn_pruned(hh)
            return carry
        lax.fori_loop(0, FOX_HEADS, body, 0)

    def scores(hh):
        return _dot(ka_ref[hh], qT_ref[hh])

    def exact_head(hh):
        s = scores(hh)
        kk = lax.broadcasted_iota(jnp.int32, s.shape, 0) + j * T
        qq = lax.broadcasted_iota(jnp.int32, s.shape, 1) + i * T
        s = jnp.where(kk > qq, NEG, s)
        m_old = m_ref[hh]
        m_new = jnp.maximum(m_old, jnp.max(s, axis=0, keepdims=True))
        p = jnp.exp2(s - m_new).astype(BF16)
        alpha = jnp.exp2(m_old - m_new)
        acc_ref[hh] = alpha * acc_ref[hh] + _dot(vT_ref[hh], p)
        m_ref[hh] = m_new

    def stale_head(hh, diagonal):
        s = scores(hh)
        if diagonal:
            kk = lax.broadcasted_iota(jnp.int32, s.shape, 0)
            qq = lax.broadcasted_iota(jnp.int32, s.shape, 1)
            s = jnp.where(kk > qq, NEG, s)
        p = jnp.exp2(s - m_ref[hh]).astype(BF16)
        pv_ref[hh] = _dot(vT_ref[hh], p)

    def pruned_head(hh):
        pv_ref[hh, FOX_HEAD_DIM:FOX_HEAD_DIM + 1, :] = jnp.zeros((1, T), F32)

    def commit_head(hh):
        acc_ref[hh] += pv_ref[hh]

    def stale_step(diagonal):
        heads(lambda hh: stale_head(hh, diagonal), pruned_head)
        lsum = pv_ref[:, FOX_HEAD_DIM:FOX_HEAD_DIM + 1, :]
        bad = jnp.sum(jnp.where(lsum < STALE_MAX_GUARD, 0.0, 1.0)) > 0.0

        @pl.when(jnp.logical_not(bad))
        def _():
            heads(commit_head)

        @pl.when(bad)
        def _():
            heads(exact_head)

    @pl.when(j == i)
    def _():
        acc_ref[...] = jnp.zeros_like(acc_ref)

        def init_head(hh):
            m_ref[hh] = sqq_ref[pl.ds(hh, 1), :]
        heads(init_head)
        stale_step(True)

    @pl.when(j < i)
    def _():
        stale_step(False)

    @pl.when(j == 0)
    def _():
        for pr in range(FOX_HEADS // 2):
            halves = []
            for hh in (2 * pr, 2 * pr + 1):
                a = acc_ref[hh]
                halves.append(a[:FOX_HEAD_DIM] / a[FOX_HEAD_DIM:FOX_HEAD_DIM + 1])
            o_ref[pr] = jnp.concatenate(halves, axis=0).T.astype(BF16)


def _prune_tables(stats, it, jt, nb):
    per = stats.shape[0] // nb
    st = stats.reshape(nb, per, 8, LANES)[:, :, :, :FOX_HEADS]
    nq = jnp.sqrt(jnp.max(st[:, :, 0, :], axis=1)) * NORM_SLACK
    nk = jnp.sqrt(jnp.max(st[:, :, 1, :], axis=1)) * NORM_SLACK
    c_first = st[:, 0, 2, :]
    c_last = st[:, per - 1, 3, :]
    bound = (nq[it] * (nk[jt] + nk[it])) - (c_last[jt] - c_first[it])
    live = jnp.logical_or(jnp.asarray(jt == it)[:, None], jnp.logical_not(bound < PRUNE_LOG2))
    steps = jnp.arange(len(it), dtype=jnp.int32)
    last_live = lax.cummax(jnp.where(jnp.any(live, axis=1), steps, 0))
    return jnp.asarray(jt)[last_live], live.astype(jnp.int32).reshape(-1)


def _fox_prompt(qT, ka, vT, stats, sqq, T):
    S = ka.shape[1]
    nb = S // T
    it = np.array([i for i in range(nb) for _ in range(i + 1)], np.int32)
    jt = np.array([j for i in range(nb) for j in range(i, -1, -1)], np.int32)
    jfetch, live = _prune_tables(stats, it, jt, nb)
    grid_spec = pltpu.PrefetchScalarGridSpec(
        num_scalar_prefetch=4,
        grid=(len(it),),
        in_specs=[
            pl.BlockSpec((FOX_HEADS, HEAD_PAD, T), lambda t, it, jt, jf, lv: (0, 0, it[t])),
            pl.BlockSpec((FOX_HEADS, T, HEAD_PAD), lambda t, it, jt, jf, lv: (0, jf[t], 0)),
            pl.BlockSpec((FOX_HEADS, V_PAD, T), lambda t, it, jt, jf, lv: (0, 0, jf[t])),
            pl.BlockSpec((FOX_HEADS, T), lambda t, it, jt, jf, lv: (0, it[t])),
        ],
        out_specs=pl.BlockSpec((FOX_HEADS // 2, T, LANES),
                               lambda t, it, jt, jf, lv: (0, it[t], 0)),
        scratch_shapes=[pltpu.VMEM((FOX_HEADS, 1, T), F32),
                        pltpu.VMEM((FOX_HEADS, V_PAD, T), F32),
                        pltpu.VMEM((FOX_HEADS, V_PAD, T), F32)],
    )
    return pl.pallas_call(
        _fox_prompt_kernel,
        grid_spec=grid_spec,
        out_shape=jax.ShapeDtypeStruct((FOX_HEADS // 2, S, LANES), BF16),
        compiler_params=pltpu.CompilerParams(dimension_semantics=("arbitrary",),
                                             vmem_limit_bytes=VMEM_LIMIT),
        name="fox_prompt",
    )(jnp.asarray(it), jnp.asarray(jt), jfetch, live, qT, ka, vT, sqq)


def _fox_sample_kernel(q_ref, kn_ref, vn_ref, ckT_ref, cvT_ref, lfT_ref, up_ref, ex_ref, o_ref):
    P = ckT_ref.shape[3]
    Tn = q_ref.shape[0]
    KP = lfT_ref.shape[2]
    HQ = FOX_HEADS * Tn
    nchunk = KP // LANES
    nt = (((1,), (1,)), ((), ()))

    def stack3(x):
        parts3 = [t.astype(F32) for t in _split3(x)] + [jnp.zeros_like(x)]
        return jnp.concatenate(parts3, axis=0).astype(BF16)

    x3 = stack3(lfT_ref[0])
    up = up_ref[...]
    run = jnp.zeros((4 * FOX_HEADS, 1), F32)
    parts = []
    for cidx in range(nchunk):
        y = _dot(x3[:, cidx * LANES:(cidx + 1) * LANES], up) + run
        parts.append(y)
        run = y[:, LANES - 1:LANES]
    y = jnp.concatenate(parts, axis=1)
    cT = (y[:FOX_HEADS] + y[FOX_HEADS:2 * FOX_HEADS] + y[2 * FOX_HEADS:3 * FOX_HEADS]) * LOG2E
    ckx = _dot(ex_ref[...], stack3(cT))

    tail = ckx[:, P:P + LANES]
    rowq = lax.broadcasted_iota(jnp.int32, tail.shape, 0) % Tn
    lanek = lax.broadcasted_iota(jnp.int32, tail.shape, 1)
    cq = jnp.sum(jnp.where(lanek == rowq, tail, 0.0), axis=1, keepdims=True)

    q = q_ref[...]
    qt = jnp.concatenate([q] * FOX_HEADS, axis=0)
    rh = lax.broadcasted_iota(jnp.int32, qt.shape, 0) // Tn
    lh = lax.broadcasted_iota(jnp.int32, qt.shape, 1) // FOX_HEAD_DIM
    qbd = jnp.where(rh == lh, qt, jnp.zeros_like(qt))

    kT = ckT_ref[0].reshape(FOX_W, P).astype(BF16)
    vT = cvT_ref[0].reshape(FOX_W, P).astype(BF16)
    s_c = _dot(qbd, kT) + cq - ckx[:, :P]
    s_n = lax.dot_general(qbd, kn_ref[...].astype(BF16), nt, preferred_element_type=F32)
    s_n = s_n + cq - ckx[:, P:P + Tn]
    key = lax.broadcasted_iota(jnp.int32, s_n.shape, 1)
    qrow = lax.broadcasted_iota(jnp.int32, s_n.shape, 0) % Tn
    s_n = jnp.where(key > qrow, NEG, s_n)
    m = jnp.maximum(jnp.max(s_c, axis=1, keepdims=True), jnp.max(s_n, axis=1, keepdims=True))
    p_c = jnp.exp2(s_c - m)
    p_n = jnp.exp2(s_n - m)
    l = jnp.sum(p_c, axis=1, keepdims=True) + jnp.sum(p_n, axis=1, keepdims=True)
    z = lax.dot_general(p_c.astype(BF16), vT, nt, preferred_element_type=F32)
    z = (z + _dot(p_n.astype(BF16), vn_ref[...].astype(BF16))) / l
    zh = lax.broadcasted_iota(jnp.int32, (Tn, FOX_W), 1) // FOX_HEAD_DIM
    o = jnp.zeros((Tn, FOX_W), F32)
    for hh in range(FOX_HEADS):
        o = o + jnp.where(zh == hh, z[hh * Tn:(hh + 1) * Tn, :], 0.0)
    o_ref[...] = o.astype(BF16)


def _fox_sample(q, kn, vn, cache_kT, cache_vT, lfT, B, Tn):
    P = cache_kT.shape[3]
    KP = lfT.shape[2]
    HQ = FOX_HEADS * Tn
    up = jnp.asarray(np.triu(np.ones((LANES, LANES), np.float32)), BF16)
    ex = np.zeros((HQ, 4 * FOX_HEADS), np.float32)
    for part in range(3):
        for hh in range(FOX_HEADS):
            ex[hh * Tn:(hh + 1) * Tn, part * FOX_HEADS + hh] = 1.0
    ex = jnp.asarray(ex, BF16)
    rowb = lambda w: pl.BlockSpec((Tn, w), lambda b: (b, 0))
    return pl.pallas_call(
        _fox_sample_kernel,
        grid=(B,),
        in_specs=[rowb(FOX_W), rowb(FOX_W), rowb(FOX_W),
                  pl.BlockSpec((1, FOX_HEADS, FOX_HEAD_DIM, P), lambda b: (b, 0, 0, 0)),
                  pl.BlockSpec((1, FOX_HEADS, FOX_HEAD_DIM, P), lambda b: (b, 0, 0, 0)),
                  pl.BlockSpec((1, FOX_HEADS, KP), lambda b: (b, 0, 0)),
                  _const_spec(up.shape), _const_spec(ex.shape)],
        out_specs=rowb(FOX_W),
        out_shape=jax.ShapeDtypeStruct((B * Tn, FOX_W), BF16),
        compiler_params=pltpu.CompilerParams(dimension_semantics=("arbitrary",),
                                             vmem_limit_bytes=VMEM_LIMIT),
        name="fox_sample",
    )(q, kn, vn, cache_kT, cache_vT, lfT, up, ex)


def _retention_kernel(q_ref, k_ref, v_ref, s0_ref, dmat_ref, xi_ref, zeta_ref, gam_ref,
                      n_ref, sout_ref, st_ref):
    c = pl.program_id(1)

    @pl.when(c == 0)
    def _():
        st_ref[...] = s0_ref[0]

    for hh in range(RET_HEADS):
        q = q_ref[:, hh * RET_KEY_DIM:(hh + 1) * RET_KEY_DIM]
        k = k_ref[:, hh * RET_KEY_DIM:(hh + 1) * RET_KEY_DIM]
        v = v_ref[:, hh * RET_VAL_DIM:(hh + 1) * RET_VAL_DIM]
        st = st_ref[hh]
        sc = lax.dot_general(q, k, (((1,), (1,)), ((), ())), preferred_element_type=F32)
        sc = sc * dmat_ref[hh]
        o = _dot(sc.astype(BF16), v) + _dot(q, st.astype(BF16)) * xi_ref[hh]
        kz = (k.astype(F32) * zeta_ref[hh]).astype(BF16)
        upd = lax.dot_general(kz, v, (((0,), (0,)), ((), ())), preferred_element_type=F32)
        st_ref[hh] = gam_ref[hh] * st + upd
        mu = jnp.mean(o, axis=-1, keepdims=True)
        d = o - mu
        var = jnp.mean(d * d, axis=-1, keepdims=True)
        n_ref[:, hh * RET_VAL_DIM:(hh + 1) * RET_VAL_DIM] = (d * lax.rsqrt(var + EPS)).astype(BF16)

    @pl.when(c == pl.num_programs(1) - 1)
    def _():
        sout_ref[0] = st_ref[...]


def _ret_log_gamma():
    return jnp.log(1.0 - jnp.exp2(-5.0 - jnp.arange(RET_HEADS, dtype=F32)))


def _retention(q, k, v, state0, B, L, C):
    nc = L // C
    lg = _ret_log_gamma()
    idx = jnp.arange(C, dtype=F32)
    diff = idx[:, None] - idx[None, :]
    dmat = jnp.where(diff[None] >= 0, jnp.exp(jnp.maximum(diff, 0.0)[None] * lg[:, None, None]), 0.0)
    xi = jnp.exp((idx[None, :] + 1.0) * lg[:, None])
    zeta = jnp.exp((C - 1.0 - idx[None, :]) * lg[:, None])
    xi = jnp.broadcast_to(xi[:, :, None], (RET_HEADS, C, RET_VAL_DIM))
    zeta = jnp.broadcast_to(zeta[:, :, None], (RET_HEADS, C, RET_KEY_DIM))
    gam = jnp.broadcast_to(jnp.exp(C * lg)[:, None, None], (RET_HEADS, 1, RET_VAL_DIM))
    rowc = lambda w: pl.BlockSpec((C, w), lambda b, c: (b * nc + c, 0))
    st_spec = pl.BlockSpec((1, RET_HEADS, RET_KEY_DIM, RET_VAL_DIM), lambda b, c: (b, 0, 0, 0))
    return pl.pallas_call(
        _retention_kernel,
        grid=(B, nc),
        in_specs=[rowc(RET_KW), rowc(RET_KW), rowc(RET_VW), st_spec,
                  _const_spec(dmat.shape), _const_spec(xi.shape), _const_spec(zeta.shape),
                  _const_spec(gam.shape)],
        out_specs=(rowc(RET_VW), st_spec),
        out_shape=(jax.ShapeDtypeStruct((B * L, RET_VW), BF16),
                   jax.ShapeDtypeStruct((B, RET_HEADS, RET_KEY_DIM, RET_VAL_DIM), F32)),
        scratch_shapes=[pltpu.VMEM((RET_HEADS, RET_KEY_DIM, RET_VAL_DIM), F32)],
        compiler_params=pltpu.CompilerParams(dimension_semantics=("arbitrary", "arbitrary"),
                                             vmem_limit_bytes=VMEM_LIMIT),
        name="retention",
    )(q, k, v, state0, dmat, xi, zeta, gam)


def _mixer_ffn_kernel(x_ref, oa_ref, nb_ref, prev_ref, gmix_ref, wg_ref, gng_ref, wpa_ref, wpb_ref,
                      wo_ref, gffn_ref, wup_ref, cw_ref, cb_ref, wdn_ref, gfin_ref,
                      y_ref, conv_ref, carry_ref, ua_ref, ub_ref, acc_ref, h2_ref,
                      *, nseg, seglen):
    i = pl.program_id(0)
    NC = N_FFN_CHUNKS
    PADR = 8
    H0 = PADR - (CONV_WIDTH - 1)

    @pl.when(i == 0)
    def _():
        carry_ref[...] = prev_ref[...]

    x = x_ref[...]
    h = _rmsnorm(x, gmix_ref[...]).astype(BF16)
    zg = _dot(h, wg_ref[...])
    gb = zg[:, :RET_VW]
    gma = zg[:, RET_VW:RET_VW + D_MODEL]
    gmb = zg[:, RET_VW + D_MODEL:]
    oa = jnp.concatenate([oa_ref[p] for p in range(FOX_HEADS // 2)], axis=1)
    ya = _dot(oa, wpa_ref[...])
    nn = nb_ref[...].astype(F32) * gng_ref[...] * (gb * jax.nn.sigmoid(gb))
    yb = _dot(nn.astype(BF16), wpb_ref[...])
    y = jax.nn.sigmoid(gma) * ya + jax.nn.sigmoid(gmb) * yb
    x1 = x + _dot(y.astype(BF16), wo_ref[...])
    h2_ref[...] = _rmsnorm(x1, gffn_ref[...]).astype(BF16)
    acc_ref[...] = x1

    h2 = h2_ref[...]

    def up_half(u_ref, cidx, slot):
        u = _dot(h2, wup_ref[cidx])
        for s in range(nseg):
            u_ref[slot, s, PADR:PADR + seglen, :] = u[s * seglen:(s + 1) * seglen, :]
            u_ref[slot, s, H0:PADR, :] = carry_ref[cidx, s, H0:PADR, :]
            carry_ref[cidx, s, H0:PADR, :] = u[(s + 1) * seglen - (CONV_WIDTH - 1):(s + 1) * seglen, :]

    def conv_half(u_ref, cidx, slot):
        w = cw_ref[cidx]
        b = cb_ref[cidx]
        outs = []
        for s in range(nseg):
            acc = w[0:1] * u_ref[slot, s, H0:H0 + seglen, :]
            for jj in range(1, CONV_WIDTH):
                acc = acc + w[jj:jj + 1] * u_ref[slot, s, H0 + jj:H0 + jj + seglen, :]
            outs.append(b + acc)
        return outs[0] if nseg == 1 else jnp.concatenate(outs, axis=0)

    def stage_up(c):
        up_half(ua_ref, c, c % 2)
        up_half(ub_ref, NC + c, c % 2)

    stage_up(0)
    for c in range(NC):
        if c + 1 < NC:
            stage_up(c + 1)
        a = conv_half(ua_ref, c, c % 2)
        b = conv_half(ub_ref, NC + c, c % 2)
        g = (jax.nn.gelu(a) * b).astype(BF16)
        acc_ref[...] += _dot(g, wdn_ref[c])
    y_ref[...] = _rmsnorm(acc_ref[...], gfin_ref[...])

    @pl.when(i == pl.num_programs(0) - 1)
    def _():
        conv_ref[...] = carry_ref[:, :, H0:PADR, :]


def _mixer_ffn(x, oa, nb, prev, weights, tm, nseg, seglen):
    M = x.shape[0]
    gmix, wg, gng, wpa, wpb, wo, gffn, wup, cw, cb, wdn, gfin = weights
    FC = FFN_CHUNK
    NC = N_FFN_CHUNKS
    row = lambda w: pl.BlockSpec((tm, w), lambda i: (i, 0))
    wspec = lambda a: pl.BlockSpec(a.shape, lambda i, n=a.ndim: (0,) * n,
                                   pipeline_mode=pl.Buffered(1))
    in_specs = [row(D_MODEL),
                pl.BlockSpec((FOX_HEADS // 2, tm, LANES), lambda i: (0, i, 0)),
                row(RET_VW), wspec(prev)] + [wspec(w) for w in weights]
    out_shape = (jax.ShapeDtypeStruct((M, D_MODEL), F32),
                 jax.ShapeDtypeStruct((2 * NC, nseg, CONV_WIDTH - 1, FC), F32))
    out_specs = (row(D_MODEL), _const_spec(out_shape[1].shape))
    return pl.pallas_call(
        functools.partial(_mixer_ffn_kernel, nseg=nseg, seglen=seglen),
        grid=(M // tm,),
        in_specs=in_specs,
        out_specs=out_specs,
        out_shape=out_shape,
        scratch_shapes=[pltpu.VMEM((2 * NC, nseg, 8, FC), F32),
                        pltpu.VMEM((2, nseg, 8 + seglen, FC), F32),
                        pltpu.VMEM((2, nseg, 8 + seglen, FC), F32),
                        pltpu.VMEM((tm, D_MODEL), F32),
                        pltpu.VMEM((tm, D_MODEL), BF16)],
        compiler_params=pltpu.CompilerParams(dimension_semantics=("arbitrary",),
                                             vmem_limit_bytes=VMEM_LIMIT),
        name="mixer_ffn",
    )(x, oa, nb, prev, *weights)


def _rotary_tables(pos):
    half = RET_KEY_DIM // 2
    inv = 1.0 / (ROPE_BASE ** jnp.linspace(0.0, 1.0, half, dtype=F32))
    ang = pos.astype(F32)[:, None] * inv[None, :]
    cos = jnp.cos(ang)
    sin = jnp.sin(ang)
    return jnp.concatenate([cos, cos], axis=1), jnp.concatenate([-sin, sin], axis=1)


def _pad_heads(wt, pad):
    d = wt.shape[1]
    wt = wt.reshape(FOX_HEADS, FOX_HEAD_DIM, d)
    wt = jnp.pad(wt, ((0, 0), (0, pad - FOX_HEAD_DIM), (0, 0)))
    return wt.reshape(FOX_HEADS * pad, d)


def _prompt_consts(tm):
    tri = np.tril(np.ones((tm, tm), np.float32))
    eq = np.zeros((3 * LANES, FOX_HEADS * HEAD_PAD), np.float32)
    ek = np.zeros((3 * LANES, FOX_HEADS * HEAD_PAD), np.float32)
    oneq = np.zeros((1, FOX_HEADS * HEAD_PAD), np.float32)
    onek = np.zeros((1, FOX_HEADS * HEAD_PAD), np.float32)
    onev = np.zeros((1, FOX_HEADS * V_PAD), np.float32)
    for hh in range(FOX_HEADS):
        base = hh * HEAD_PAD + BIAS_COL
        for part in range(3):
            eq[part * LANES + hh, base + part] = 1.0
            ek[part * LANES + hh, base + 3 + part] = -1.0
            onek[0, base + part] = 1.0
            oneq[0, base + 3 + part] = 1.0
        onev[0, hh * V_PAD + FOX_HEAD_DIM] = 1.0
    sel = np.zeros((FOX_HEADS * HEAD_PAD, LANES), np.float32)
    for hh in range(FOX_HEADS):
        sel[hh * HEAD_PAD:hh * HEAD_PAD + FOX_HEAD_DIM, hh] = 1.0
    return (jnp.asarray(tri, BF16), jnp.asarray(eq, BF16), jnp.asarray(ek, BF16),
            jnp.asarray(oneq), jnp.asarray(onek), jnp.asarray(onev), jnp.asarray(sel, BF16))


def _chunk_cols(a):
    lead = a.shape[:-1]
    a = a.reshape(lead + (2 * N_FFN_CHUNKS, FFN_CHUNK))
    return jnp.moveaxis(a, -2, 0)


def _tile(n, pref):
    t = min(n, pref)
    while n % t:
        t //= 2
    return t


def kernel(x_prompt, x_sample, cache_fox_k, cache_fox_v, cache_fox_logf, state_ret, state_ffn_conv,
           norm_mix_g, w_in, b_fox_f, gn_ret_g, w_pa, w_pb, w_o, norm_ffn_g, w_up, conv_w, conv_b,
           w_down, norm_final_g):
    depth = w_in.shape[0]
    Bp, S, _ = x_prompt.shape
    Bs, Ts, _ = x_sample.shape
    P = cache_fox_k.shape[2]
    assert depth == 1 and Bp == 1, "kernel handles the single-layer, single-prompt configuration"
    l = 0

    wt = jnp.swapaxes(w_in[l], 0, 1).astype(BF16)
    o0 = 3 * FOX_W
    o1 = o0 + FOX_HEADS
    o2 = o1 + 2 * RET_KW + RET_VW
    wq_aug = _pad_heads(wt[:FOX_W], HEAD_PAD).T
    wk_aug = _pad_heads(wt[FOX_W:2 * FOX_W], HEAD_PAD).T
    wv_aug = _pad_heads(wt[2 * FOX_W:o0], V_PAD).T
    wqkv = wt[:o0].T
    wf = jnp.pad(wt[o0:o1], ((0, LANES - FOX_HEADS), (0, 0))).T
    bf = jnp.pad(b_fox_f[l].astype(F32), (0, LANES - FOX_HEADS))[None, :]
    wb = wt[o1:o2].T
    wg = wt[o2:].T
    gmix = norm_mix_g[l].astype(F32)[None, :]
    mix_weights = (
        gmix, wg, gn_ret_g[l].astype(F32)[None, :], w_pa[l].astype(BF16), w_pb[l].astype(BF16),
        w_o[l].astype(BF16), norm_ffn_g[l].astype(F32)[None, :],
        _chunk_cols(w_up[l]).astype(BF16),
        jnp.pad(_chunk_cols(conv_w[l].astype(F32)), ((0, 0), (0, 8 - CONV_WIDTH), (0, 0))),
        _chunk_cols(conv_b[l].astype(F32)[None, :]),
        w_down[l].reshape(N_FFN_CHUNKS, FFN_CHUNK, D_MODEL).astype(BF16),
        norm_final_g.astype(F32)[None, :],
    )

    tm_a = _tile(S, 256)
    cos_p, sin_p = _rotary_tables(jnp.arange(S))
    (qT, ka, vT, kT_p, vT_p, logfT_p, qb, kb, vb, stats, sqq) = _inproj_prompt(
        x_prompt[0], gmix, wq_aug, wk_aug, wv_aug, wf, bf, wb, cos_p, sin_p,
        _prompt_consts(tm_a), tm_a)
    oa_p = _fox_prompt(qT, ka, vT, stats, sqq, _tile(S, 1024))
    zero_state = jnp.zeros((1, RET_HEADS, RET_KEY_DIM, RET_VAL_DIM), F32)
    nb_p, ret_p = _retention(qb, kb, vb, zero_state, 1, S, _tile(S, 256))
    tm_d = _tile(S, 256)
    zero_prev = jnp.zeros((2 * N_FFN_CHUNKS, 1, 8, FFN_CHUNK), F32)
    y_p, conv_p = _mixer_ffn(x_prompt[0], oa_p, nb_p, zero_prev, mix_weights, tm_d, 1, tm_d)

    Ms = Bs * Ts
    cos_s, sin_s = _rotary_tables(P + jnp.arange(Ts))
    cos_s = jnp.tile(cos_s, (Bs, 1))
    sin_s = jnp.tile(sin_s, (Bs, 1))
    (q_s, k_s, v_s, logf_s, qb_s, kb_s, vb_s) = _inproj_sample(
        x_sample.reshape(Ms, D_MODEL), gmix, wqkv, wf, bf, wb, cos_s, sin_s)
    KP = ((P + Ts + LANES - 1) // LANES) * LANES
    lf_all = jnp.concatenate([cache_fox_logf[l].astype(F32), logf_s.reshape(Bs, Ts, FOX_HEADS)], axis=1)
    lfT = jnp.pad(jnp.swapaxes(lf_all, 1, 2), ((0, 0), (0, 0), (0, KP - P - Ts)))
    oa_s = _fox_sample(q_s, k_s, v_s, jnp.transpose(cache_fox_k[l], (0, 2, 3, 1)),
                       jnp.transpose(cache_fox_v[l], (0, 2, 3, 1)), lfT, Bs, Ts)
    oa_s = jnp.moveaxis(oa_s.reshape(Ms, FOX_HEADS // 2, LANES), 1, 0)
    nb_s, ret_s = _retention(qb_s, kb_s, vb_s, state_ret[l].astype(F32), Bs, Ts, Ts)
    prev_s = _chunk_cols(state_ffn_conv[l].astype(F32))
    prev_s = jnp.pad(prev_s, ((0, 0), (0, 0), (8 - (CONV_WIDTH - 1), 0), (0, 0)))
    y_s, conv_s = _mixer_ffn(x_sample.reshape(Ms, D_MODEL), oa_s, nb_s, prev_s, mix_weights,
                             Ms, Bs, Ts)

    def unchunk(cv):
        return jnp.moveaxis(cv, 0, 2).reshape(cv.shape[1], CONV_WIDTH - 1, 2 * FFN_DIM)

    hshape = (FOX_HEADS, FOX_HEAD_DIM)
    return (
        y_p[None],
        y_s.reshape(Bs, Ts, D_MODEL),
        jnp.transpose(kT_p, (2, 0, 1))[None, None],
        jnp.transpose(vT_p, (2, 0, 1))[None, None],
        jnp.transpose(logfT_p, (1, 0))[None, None],
        ret_p[None],
        unchunk(conv_p)[None],
        k_s.reshape((1, Bs, Ts) + hshape),
        v_s.reshape((1, Bs, Ts) + hshape),
        logf_s.reshape(1, Bs, Ts, FOX_HEADS),
        ret_s[None],
        unchunk(conv_s)[None],
    )
```

```python
import functools
import math

import numpy as np
import jax
import jax.numpy as jnp
from jax import lax
from jax.experimental import pallas as pl
from jax.experimental.pallas import tpu as pltpu

F32 = jnp.float32
BF16 = jnp.bfloat16

D_MODEL = 1024
FOX_HEADS = 8
FOX_HEAD_DIM = 64
RET_HEADS = 4
RET_KEY_DIM = 128
RET_VAL_DIM = 256
FFN_DIM = 2816
CONV_WIDTH = 3
EPS = 1e-6
ROPE_BASE = 10000.0

FOX_W = FOX_HEADS * FOX_HEAD_DIM
RET_KW = RET_HEADS * RET_KEY_DIM
RET_VW = RET_HEADS * RET_VAL_DIM

LOG2E = 1.4426950408889634
LANES = 128
HEAD_PAD = LANES
V_PAD = 80
BIAS_COL = FOX_HEAD_DIM
NEG = -1e30
STALE_MAX_GUARD = 2.0 ** 60
PRUNE_LOG2 = -160.0
NORM_SLACK = 1.02
FFN_CHUNK = 256
N_FFN_CHUNKS = FFN_DIM // FFN_CHUNK
VMEM_LIMIT = 56 * 1024 * 1024


def _rmsnorm(x, g):
    ms = jnp.mean(x * x, axis=-1, keepdims=True)
    return x * lax.rsqrt(ms + EPS) * g


def _split3(x):
    hi = x.astype(BF16)
    r1 = x - hi.astype(F32)
    mid = r1.astype(BF16)
    lo = (r1 - mid.astype(F32)).astype(BF16)
    return hi, mid, lo


def _log_sigmoid(x):
    return jnp.minimum(x, 0.0) - jnp.log1p(jnp.exp(-jnp.abs(x)))


def _dot(a, b):
    return jnp.dot(a, b, preferred_element_type=F32)


def _rotary(x, cos2, sin2):
    return x * cos2 + pltpu.roll(x, RET_KEY_DIM // 2, 1) * sin2


def _const_spec(shape):
    n = len(shape)
    return pl.BlockSpec(shape, lambda *_: (0,) * n)


def _inproj_prompt_kernel(x_ref, g_ref, wq_ref, wk_ref, wv_ref, wf_ref, bf_ref, wb_ref,
                          cos_ref, sin_ref, tri_ref, eq_ref, ek_ref, oneq_ref, onek_ref, onev_ref,
                          sel_ref,
                          qT_ref, ka_ref, vT_ref, kT32_ref, vT32_ref, logf_ref, qb_ref, kb_ref, vb_ref,
                          stats_ref, sqq_ref, carry_ref):
    tm = x_ref.shape[0]

    @pl.when(pl.program_id(0) == 0)
    def _():
        carry_ref[...] = jnp.zeros_like(carry_ref)

    h = _rmsnorm(x_ref[...], g_ref[...]).astype(BF16)

    logf = _log_sigmoid(_dot(h, wf_ref[...]) + bf_ref[...])
    logf_ref[...] = logf.T[:FOX_HEADS, :]
    lane = lax.broadcasted_iota(jnp.int32, logf.shape, 1)
    logf = jnp.where(lane < FOX_HEADS, logf, 0.0)
    r = _dot(tri_ref[...], jnp.concatenate(_split3(logf), axis=1))
    c = r[:, :LANES] + r[:, LANES:2 * LANES] + r[:, 2 * LANES:] + carry_ref[...]
    carry_ref[...] = c[tm - 1:tm, :]
    c3 = jnp.concatenate(_split3(c * LOG2E), axis=1)

    q_aug = (_dot(h, wq_ref[...]) * (FOX_HEAD_DIM ** -0.5 * LOG2E)
             + _dot(c3, eq_ref[...]) + oneq_ref[...])
    qT = q_aug.T.astype(BF16)
    k_aug = _dot(h, wk_ref[...]) + _dot(c3, ek_ref[...]) + onek_ref[...]

    sel = sel_ref[...]
    nq2 = jnp.max(_dot((q_aug * q_aug).astype(BF16), sel), axis=0, keepdims=True)
    nk2 = jnp.max(_dot((k_aug * k_aug).astype(BF16), sel), axis=0, keepdims=True)
    c2 = c * LOG2E
    stats_ref[0] = jnp.concatenate(
        [nq2, nk2, c2[0:1, :], c2[tm - 1:tm, :], jnp.zeros((4, LANES), F32)], axis=0)
    sqq = jnp.where(lane < FOX_HEADS, _dot((q_aug * k_aug).astype(BF16), sel), 0.0)
    sqq_ref[...] = sqq.T[:FOX_HEADS, :]

    kT32 = k_aug.T
    k_aug = k_aug.astype(BF16)
    vT32 = (_dot(h, wv_ref[...]) + onev_ref[...]).T
    vT = vT32.astype(BF16)
    for hh in range(FOX_HEADS):
        qT_ref[hh] = qT[hh * HEAD_PAD:(hh + 1) * HEAD_PAD, :]
        ka_ref[hh] = k_aug[:, hh * HEAD_PAD:(hh + 1) * HEAD_PAD]
        vT_ref[hh] = vT[hh * V_PAD:(hh + 1) * V_PAD, :]
        kT32_ref[hh] = kT32[hh * HEAD_PAD:hh * HEAD_PAD + FOX_HEAD_DIM, :]
        vT32_ref[hh] = vT32[hh * V_PAD:hh * V_PAD + FOX_HEAD_DIM, :]

    zb = _dot(h, wb_ref[...])
    cos2 = cos_ref[...]
    sin2 = sin_ref[...]
    for hh in range(RET_HEADS):
        sl = slice(hh * RET_KEY_DIM, (hh + 1) * RET_KEY_DIM)
        qb_ref[:, sl] = _rotary(zb[:, sl], cos2, sin2).astype(BF16)
        xk = zb[:, RET_KW + hh * RET_KEY_DIM:RET_KW + (hh + 1) * RET_KEY_DIM]
        kb_ref[:, sl] = (_rotary(xk, cos2, sin2) * (RET_KEY_DIM ** -0.5)).astype(BF16)
    vb_ref[...] = zb[:, 2 * RET_KW:].astype(BF16)


def _inproj_prompt(x, g, wq, wk, wv, wf, bf, wb, cos2, sin2, consts, tm):
    S = x.shape[0]
    tri, eq, ek, oneq, onek, onev, sel = consts
    row = lambda w: pl.BlockSpec((tm, w), lambda i: (i, 0))
    headT = pl.BlockSpec((FOX_HEADS, FOX_HEAD_DIM, tm), lambda i: (0, 0, i))
    in_specs = [row(D_MODEL), _const_spec(g.shape), _const_spec(wq.shape), _const_spec(wk.shape),
                _const_spec(wv.shape), _const_spec(wf.shape),
                _const_spec(bf.shape), _const_spec(wb.shape), row(LANES), row(LANES),
                _const_spec(tri.shape), _const_spec(eq.shape), _const_spec(ek.shape),
                _const_spec(oneq.shape), _const_spec(onek.shape), _const_spec(onev.shape),
                _const_spec(sel.shape)]
    out_shape = (
        jax.ShapeDtypeStruct((FOX_HEADS, HEAD_PAD, S), BF16),
        jax.ShapeDtypeStruct((FOX_HEADS, S, HEAD_PAD), BF16),
        jax.ShapeDtypeStruct((FOX_HEADS, V_PAD, S), BF16),
        jax.ShapeDtypeStruct((FOX_HEADS, FOX_HEAD_DIM, S), F32),
        jax.ShapeDtypeStruct((FOX_HEADS, FOX_HEAD_DIM, S), F32),
        jax.ShapeDtypeStruct((FOX_HEADS, S), F32),
        jax.ShapeDtypeStruct((S, RET_KW), BF16),
        jax.ShapeDtypeStruct((S, RET_KW), BF16),
        jax.ShapeDtypeStruct((S, RET_VW), BF16),
        jax.ShapeDtypeStruct((S // tm, 8, LANES), F32),
        jax.ShapeDtypeStruct((FOX_HEADS, S), F32),
    )
    out_specs = (
        pl.BlockSpec((FOX_HEADS, HEAD_PAD, tm), lambda i: (0, 0, i)),
        pl.BlockSpec((FOX_HEADS, tm, HEAD_PAD), lambda i: (0, i, 0)),
        pl.BlockSpec((FOX_HEADS, V_PAD, tm), lambda i: (0, 0, i)),
        headT, headT, pl.BlockSpec((FOX_HEADS, tm), lambda i: (0, i)),
        row(RET_KW), row(RET_KW), row(RET_VW),
        pl.BlockSpec((1, 8, LANES), lambda i: (i, 0, 0)),
        pl.BlockSpec((FOX_HEADS, tm), lambda i: (0, i)),
    )
    return pl.pallas_call(
        _inproj_prompt_kernel,
        grid=(S // tm,),
        in_specs=in_specs,
        out_specs=out_specs,
        out_shape=out_shape,
        scratch_shapes=[pltpu.VMEM((1, LANES), F32)],
        compiler_params=pltpu.CompilerParams(dimension_semantics=("arbitrary",),
                                             vmem_limit_bytes=VMEM_LIMIT),
        name="inproj_prompt",
    )(x, g, wq, wk, wv, wf, bf, wb, cos2, sin2, tri, eq, ek, oneq, onek, onev, sel)


def _inproj_sample_kernel(x_ref, g_ref, wqkv_ref, wf_ref, bf_ref, wb_ref, cos_ref, sin_ref,
                          q_ref, k32_ref, v32_ref, logf_ref, qb_ref, kb_ref, vb_ref):
    h = _rmsnorm(x_ref[...], g_ref[...]).astype(BF16)
    logf = _log_sigmoid(_dot(h, wf_ref[...]) + bf_ref[...])
    logf_ref[...] = logf[:, :FOX_HEADS]
    z = _dot(h, wqkv_ref[...])
    q_ref[...] = (z[:, :FOX_W] * (FOX_HEAD_DIM ** -0.5 * LOG2E)).astype(BF16)
    k32_ref[...] = z[:, FOX_W:2 * FOX_W]
    v32_ref[...] = z[:, 2 * FOX_W:]
    zb = _dot(h, wb_ref[...])
    cos2 = cos_ref[...]
    sin2 = sin_ref[...]
    for hh in range(RET_HEADS):
        sl = slice(hh * RET_KEY_DIM, (hh + 1) * RET_KEY_DIM)
        qb_ref[:, sl] = _rotary(zb[:, sl], cos2, sin2).astype(BF16)
        xk = zb[:, RET_KW + hh * RET_KEY_DIM:RET_KW + (hh + 1) * RET_KEY_DIM]
        kb_ref[:, sl] = (_rotary(xk, cos2, sin2) * (RET_KEY_DIM ** -0.5)).astype(BF16)
    vb_ref[...] = zb[:, 2 * RET_KW:].astype(BF16)


def _inproj_sample(x, g, wqkv, wf, bf, wb, cos2, sin2):
    M = x.shape[0]
    args = (x, g, wqkv, wf, bf, wb, cos2, sin2)
    out_shape = (
        jax.ShapeDtypeStruct((M, FOX_W), BF16),
        jax.ShapeDtypeStruct((M, FOX_W), F32),
        jax.ShapeDtypeStruct((M, FOX_W), F32),
        jax.ShapeDtypeStruct((M, FOX_HEADS), F32),
        jax.ShapeDtypeStruct((M, RET_KW), BF16),
        jax.ShapeDtypeStruct((M, RET_KW), BF16),
        jax.ShapeDtypeStruct((M, RET_VW), BF16),
    )
    return pl.pallas_call(
        _inproj_sample_kernel,
        grid=(1,),
        in_specs=[_const_spec(a.shape) for a in args],
        out_specs=tuple(_const_spec(o.shape) for o in out_shape),
        out_shape=out_shape,
        compiler_params=pltpu.CompilerParams(dimension_semantics=("arbitrary",),
                                             vmem_limit_bytes=VMEM_LIMIT),
        name="inproj_sample",
    )(*args)


def _fox_prompt_kernel(it_ref, jt_ref, jfetch_ref, live_ref, strip_ref, qT_ref, ka_ref, vT_ref,
                       sqq_ref, o_ref, m_ref, acc_ref, pv_ref, *, n_strips):
    del jfetch_ref
    t = pl.program_id(0)
    i = it_ref[t]
    j = jt_ref[t]
    T = qT_ref.shape[2]
    SUB = T // n_strips

    def heads(fn, fn_pruned=None):
        def body(hh, carry):
            is_live = live_ref[t * FOX_HEADS + hh] != 0

            @pl.when(is_live)
            def _():
                fn(hh)

            if fn_pruned is not None:
                @pl.when(jnp.logical_not(is_live))
                def _():
                    fn_pruned(hh)
            return carry
        lax.fori_loop(0, FOX_HEADS, body, 0)

    def scores(hh):
        return _dot(ka_ref[hh], qT_ref[hh])

    def exact_head(hh):
        s = scores(hh)
        kk = lax.broadcasted_iota(jnp.int32, s.shape, 0) + j * T
        qq = lax.broadcasted_iota(jnp.int32, s.shape, 1) + i * T
        s = jnp.where(kk > qq, NEG, s)
        m_old = m_ref[hh]
        m_new = jnp.maximum(m_old, jnp.max(s, axis=0, keepdims=True))
        p = jnp.exp2(s - m_new).astype(BF16)
        alpha = jnp.exp2(m_old - m_new)
        acc_ref[hh] = alpha * acc_ref[hh] + _dot(vT_ref[hh], p)
        m_ref[hh] = m_new


    def stale_head_diag(hh):
        s = scores(hh)
        kk = lax.broadcasted_iota(jnp.int32, s.shape, 0)
        qq = lax.broadcasted_iota(jnp.int32, s.shape, 1)
        p = jnp.exp2(jnp.where(kk > qq, NEG, s) - m_ref[hh]).astype(BF16)
        pv_ref[hh] = _dot(vT_ref[hh], p)

    def stale_head_off(hh):
        m = m_ref[hh]
        n_keep = strip_ref[t * FOX_HEADS + hh]
        for nn in range(1, n_strips + 1):
            k0 = (n_strips - nn) * SUB

            @pl.when(n_keep == nn)
            def _():
                p = jnp.exp2(_dot(ka_ref[hh, k0:, :], qT_ref[hh]) - m).astype(BF16)
                pv_ref[hh] = _dot(vT_ref[hh, :, k0:], p)

    def pruned_head(hh):
        pv_ref[hh, FOX_HEAD_DIM:FOX_HEAD_DIM + 1, :] = jnp.zeros((1, T), F32)

    def commit_head(hh):
        acc_ref[hh] += pv_ref[hh]

    def stale_step(diagonal):
        heads(stale_head_diag if diagonal else stale_head_off, pruned_head)
        lsum = pv_ref[:, FOX_HEAD_DIM:FOX_HEAD_DIM + 1, :]
        bad = jnp.sum(jnp.where(lsum < STALE_MAX_GUARD, 0.0, 1.0)) > 0.0

        @pl.when(jnp.logical_not(bad))
        def _():
            heads(commit_head)

        @pl.when(bad)
        def _():
            heads(exact_head)

    @pl.when(j == i)
    def _():
        acc_ref[...] = jnp.zeros_like(acc_ref)

        def init_head(hh):
            m_ref[hh] = sqq_ref[pl.ds(hh, 1), :]
        heads(init_head)
        stale_step(True)

    @pl.when(j < i)
    def _():
        stale_step(False)

    @pl.when(j == 0)
    def _():
        for pr in range(FOX_HEADS // 2):
            halves = []
            for hh in (2 * pr, 2 * pr + 1):
                a = acc_ref[hh]
                halves.append(a[:FOX_HEAD_DIM] / a[FOX_HEAD_DIM:FOX_HEAD_DIM + 1])
            o_ref[pr] = jnp.concatenate(halves, axis=0).T.astype(BF16)


def _prune_tables(stats, it, jt, nb):
    per = stats.shape[0] // nb
    st = stats.reshape(nb, per, 8, LANES)[:, :, :, :FOX_HEADS]
    nq = jnp.sqrt(jnp.max(st[:, :, 0, :], axis=1)) * NORM_SLACK
    nk_strip = jnp.sqrt(st[:, :, 1, :]) * NORM_SLACK
    nk = jnp.max(nk_strip, axis=1)
    c_first = st[:, 0, 2, :]
    c_last = st[:, :, 3, :]
    bound = (nq[it][:, None, :] * (nk_strip[jt] + nk[it][:, None, :])
             - (c_last[jt] - c_first[it][:, None, :]))
    live = jnp.logical_or(jnp.asarray(jt == it)[:, None, None],
                          jnp.logical_not(bound < PRUNE_LOG2))
    strip_no = jnp.arange(per, dtype=jnp.int32)[None, :, None]
    n_keep = per - jnp.min(jnp.where(live, strip_no, per), axis=1)
    live_head = n_keep > 0
    steps = jnp.arange(len(it), dtype=jnp.int32)
    last_live = lax.cummax(jnp.where(jnp.any(live_head, axis=1), steps, 0))
    return (jnp.asarray(jt)[last_live], live_head.astype(jnp.int32).reshape(-1),
            n_keep.astype(jnp.int32).reshape(-1), per)


def _fox_prompt(qT, ka, vT, stats, sqq, T):
    S = ka.shape[1]
    nb = S // T
    it = np.array([i for i in range(nb) for _ in range(i + 1)], np.int32)
    jt = np.array([j for i in range(nb) for j in range(i, -1, -1)], np.int32)
    jfetch, live, live_strip, n_strips = _prune_tables(stats, it, jt, nb)
    grid_spec = pltpu.PrefetchScalarGridSpec(
        num_scalar_prefetch=5,
        grid=(len(it),),
        in_specs=[
            pl.BlockSpec((FOX_HEADS, HEAD_PAD, T), lambda t, it, jt, jf, lv, ls: (0, 0, it[t])),
            pl.BlockSpec((FOX_HEADS, T, HEAD_PAD), lambda t, it, jt, jf, lv, ls: (0, jf[t], 0)),
            pl.BlockSpec((FOX_HEADS, V_PAD, T), lambda t, it, jt, jf, lv, ls: (0, 0, jf[t])),
            pl.BlockSpec((FOX_HEADS, T), lambda t, it, jt, jf, lv, ls: (0, it[t])),
        ],
        out_specs=pl.BlockSpec((FOX_HEADS // 2, T, LANES),
                               lambda t, it, jt, jf, lv, ls: (0, it[t], 0)),
        scratch_shapes=[pltpu.VMEM((FOX_HEADS, 1, T), F32),
                        pltpu.VMEM((FOX_HEADS, V_PAD, T), F32),
                        pltpu.VMEM((FOX_HEADS, V_PAD, T), F32)],
    )
    return pl.pallas_call(
        functools.partial(_fox_prompt_kernel, n_strips=n_strips),
        grid_spec=grid_spec,
        out_shape=jax.ShapeDtypeStruct((FOX_HEADS // 2, S, LANES), BF16),
        compiler_params=pltpu.CompilerParams(dimension_semantics=("arbitrary",),
                                             vmem_limit_bytes=VMEM_LIMIT),
        name="fox_prompt",
    )(jnp.asarray(it), jnp.asarray(jt), jfetch, live, live_strip, qT, ka, vT, sqq)


def _fox_sample_kernel(q_ref, kn_ref, vn_ref, ckT_ref, cvT_ref, lfT_ref, up_ref, ex_ref, o_ref):
    P = ckT_ref.shape[3]
    Tn = q_ref.shape[0]
    KP = lfT_ref.shape[2]
    HQ = FOX_HEADS * Tn
    nchunk = KP // LANES
    nt = (((1,), (1,)), ((), ()))

    def stack3(x):
        parts3 = [t.astype(F32) for t in _split3(x)] + [jnp.zeros_like(x)]
        return jnp.concatenate(parts3, axis=0).astype(BF16)

    x3 = stack3(lfT_ref[0])
    up = up_ref[...]
    run = jnp.zeros((4 * FOX_HEADS, 1), F32)
    parts = []
    for cidx in range(nchunk):
        y = _dot(x3[:, cidx * LANES:(cidx + 1) * LANES], up) + run
        parts.append(y)
        run = y[:, LANES - 1:LANES]
    y = jnp.concatenate(parts, axis=1)
    cT = (y[:FOX_HEADS] + y[FOX_HEADS:2 * FOX_HEADS] + y[2 * FOX_HEADS:3 * FOX_HEADS]) * LOG2E
    ckx = _dot(ex_ref[...], stack3(cT))

    tail = ckx[:, P:P + LANES]
    rowq = lax.broadcasted_iota(jnp.int32, tail.shape, 0) % Tn
    lanek = lax.broadcasted_iota(jnp.int32, tail.shape, 1)
    cq = jnp.sum(jnp.where(lanek == rowq, tail, 0.0), axis=1, keepdims=True)

    q = q_ref[...]
    qt = jnp.concatenate([q] * FOX_HEADS, axis=0)
    rh = lax.broadcasted_iota(jnp.int32, qt.shape, 0) // Tn
    lh = lax.broadcasted_iota(jnp.int32, qt.shape, 1) // FOX_HEAD_DIM
    qbd = jnp.where(rh == lh, qt, jnp.zeros_like(qt))

    kT = ckT_ref[0].reshape(FOX_W, P).astype(BF16)
    vT = cvT_ref[0].reshape(FOX_W, P).astype(BF16)
    s_c = _dot(qbd, kT) + cq - ckx[:, :P]
    s_n = lax.dot_general(qbd, kn_ref[...].astype(BF16), nt, preferred_element_type=F32)
    s_n = s_n + cq - ckx[:, P:P + Tn]
    key = lax.broadcasted_iota(jnp.int32, s_n.shape, 1)
    qrow = lax.broadcasted_iota(jnp.int32, s_n.shape, 0) % Tn
    s_n = jnp.where(key > qrow, NEG, s_n)
    m = jnp.maximum(jnp.max(s_c, axis=1, keepdims=True), jnp.max(s_n, axis=1, keepdims=True))
    p_c = jnp.exp2(s_c - m)
    p_n = jnp.exp2(s_n - m)
    l = jnp.sum(p_c, axis=1, keepdims=True) + jnp.sum(p_n, axis=1, keepdims=True)
    z = lax.dot_general(p_c.astype(BF16), vT, nt, preferred_element_type=F32)
    z = (z + _dot(p_n.astype(BF16), vn_ref[...].astype(BF16))) / l
    zh = lax.broadcasted_iota(jnp.int32, (Tn, FOX_W), 1) // FOX_HEAD_DIM
    o = jnp.zeros((Tn, FOX_W), F32)
    for hh in range(FOX_HEADS):
        o = o + jnp.where(zh == hh, z[hh * Tn:(hh + 1) * Tn, :], 0.0)
    o_ref[...] = o.astype(BF16)


def _fox_sample(q, kn, vn, cache_kT, cache_vT, lfT, B, Tn):
    P = cache_kT.shape[3]
    KP = lfT.shape[2]
    HQ = FOX_HEADS * Tn
    up = jnp.asarray(np.triu(np.ones((LANES, LANES), np.float32)), BF16)
    ex = np.zeros((HQ, 4 * FOX_HEADS), np.float32)
    for part in range(3):
        for hh in range(FOX_HEADS):
            ex[hh * Tn:(hh + 1) * Tn, part * FOX_HEADS + hh] = 1.0
    ex = jnp.asarray(ex, BF16)
    rowb = lambda w: pl.BlockSpec((Tn, w), lambda b: (b, 0))
    return pl.pallas_call(
        _fox_sample_kernel,
        grid=(B,),
        in_specs=[rowb(FOX_W), rowb(FOX_W), rowb(FOX_W),
                  pl.BlockSpec((1, FOX_HEADS, FOX_HEAD_DIM, P), lambda b: (b, 0, 0, 0)),
                  pl.BlockSpec((1, FOX_HEADS, FOX_HEAD_DIM, P), lambda b: (b, 0, 0, 0)),
                  pl.BlockSpec((1, FOX_HEADS, KP), lambda b: (b, 0, 0)),
                  _const_spec(up.shape), _const_spec(ex.shape)],
        out_specs=rowb(FOX_W),
        out_shape=jax.ShapeDtypeStruct((B * Tn, FOX_W), BF16),
        compiler_params=pltpu.CompilerParams(dimension_semantics=("arbitrary",),
                                             vmem_limit_bytes=VMEM_LIMIT),
        name="fox_sample",
    )(q, kn, vn, cache_kT, cache_vT, lfT, up, ex)


def _retention_kernel(q_ref, k_ref, v_ref, s0_ref, dmat_ref, xi_ref, zeta_ref, gam_ref,
                      n_ref, sout_ref, st_ref):
    c = pl.program_id(1)

    @pl.when(c == 0)
    def _():
        st_ref[...] = s0_ref[0]

    for hh in range(RET_HEADS):
        q = q_ref[:, hh * RET_KEY_DIM:(hh + 1) * RET_KEY_DIM]
        k = k_ref[:, hh * RET_KEY_DIM:(hh + 1) * RET_KEY_DIM]
        v = v_ref[:, hh * RET_VAL_DIM:(hh + 1) * RET_VAL_DIM]
        st = st_ref[hh]
        sc = lax.dot_general(q, k, (((1,), (1,)), ((), ())), preferred_element_type=F32)
        sc = sc * dmat_ref[hh]
        o = _dot(sc.astype(BF16), v) + _dot(q, st.astype(BF16)) * xi_ref[hh]
        kz = (k.astype(F32) * zeta_ref[hh]).astype(BF16)
        upd = lax.dot_general(kz, v, (((0,), (0,)), ((), ())), preferred_element_type=F32)
        st_ref[hh] = gam_ref[hh] * st + upd
        mu = jnp.mean(o, axis=-1, keepdims=True)
        d = o - mu
        var = jnp.mean(d * d, axis=-1, keepdims=True)
        n_ref[:, hh * RET_VAL_DIM:(hh + 1) * RET_VAL_DIM] = (d * lax.rsqrt(var + EPS)).astype(BF16)

    @pl.when(c == pl.num_programs(1) - 1)
    def _():
        sout_ref[0] = st_ref[...]


def _ret_log_gamma():
    return jnp.log(1.0 - jnp.exp2(-5.0 - jnp.arange(RET_HEADS, dtype=F32)))


def _retention(q, k, v, state0, B, L, C):
    nc = L // C
    lg = _ret_log_gamma()
    idx = jnp.arange(C, dtype=F32)
    diff = idx[:, None] - idx[None, :]
    dmat = jnp.where(diff[None] >= 0, jnp.exp(jnp.maximum(diff, 0.0)[None] * lg[:, None, None]), 0.0)
    xi = jnp.exp((idx[None, :] + 1.0) * lg[:, None])
    zeta = jnp.exp((C - 1.0 - idx[None, :]) * lg[:, None])
    xi = jnp.broadcast_to(xi[:, :, None], (RET_HEADS, C, RET_VAL_DIM))
    zeta = jnp.broadcast_to(zeta[:, :, None], (RET_HEADS, C, RET_KEY_DIM))
    gam = jnp.broadcast_to(jnp.exp(C * lg)[:, None, None], (RET_HEADS, 1, RET_VAL_DIM))
    rowc = lambda w: pl.BlockSpec((C, w), lambda b, c: (b * nc + c, 0))
    st_spec = pl.BlockSpec((1, RET_HEADS, RET_KEY_DIM, RET_VAL_DIM), lambda b, c: (b, 0, 0, 0))
    return pl.pallas_call(
        _retention_kernel,
        grid=(B, nc),
        in_specs=[rowc(RET_KW), rowc(RET_KW), rowc(RET_VW), st_spec,
                  _const_spec(dmat.shape), _const_spec(xi.shape), _const_spec(zeta.shape),
                  _const_spec(gam.shape)],
        out_specs=(rowc(RET_VW), st_spec),
        out_shape=(jax.ShapeDtypeStruct((B * L, RET_VW), BF16),
                   jax.ShapeDtypeStruct((B, RET_HEADS, RET_KEY_DIM, RET_VAL_DIM), F32)),
        scratch_shapes=[pltpu.VMEM((RET_HEADS, RET_KEY_DIM, RET_VAL_DIM), F32)],
        compiler_params=pltpu.CompilerParams(dimension_semantics=("arbitrary", "arbitrary"),
                                             vmem_limit_bytes=VMEM_LIMIT),
        name="retention",
    )(q, k, v, state0, dmat, xi, zeta, gam)


def _mixer_ffn_kernel(x_ref, oa_ref, nb_ref, prev_ref, gmix_ref, wg_ref, gng_ref, wpa_ref, wpb_ref,
                      wo_ref, gffn_ref, wup_ref, cw_ref, cb_ref, wdn_ref, gfin_ref,
                      y_ref, conv_ref, carry_ref, ua_ref, ub_ref, acc_ref, h2_ref,
                      *, nseg, seglen):
    i = pl.program_id(0)
    NC = N_FFN_CHUNKS
    PADR = 8
    H0 = PADR - (CONV_WIDTH - 1)

    @pl.when(i == 0)
    def _():
        carry_ref[...] = prev_ref[...]

    x = x_ref[...]
    h = _rmsnorm(x, gmix_ref[...]).astype(BF16)
    zg = _dot(h, wg_ref[...])
    gb = zg[:, :RET_VW]
    gma = zg[:, RET_VW:RET_VW + D_MODEL]
    gmb = zg[:, RET_VW + D_MODEL:]
    oa = jnp.concatenate([oa_ref[p] for p in range(FOX_HEADS // 2)], axis=1)
    ya = _dot(oa, wpa_ref[...])
    nn = nb_ref[...].astype(F32) * gng_ref[...] * (gb * jax.nn.sigmoid(gb))
    yb = _dot(nn.astype(BF16), wpb_ref[...])
    y = jax.nn.sigmoid(gma) * ya + jax.nn.sigmoid(gmb) * yb
    x1 = x + _dot(y.astype(BF16), wo_ref[...])
    h2_ref[...] = _rmsnorm(x1, gffn_ref[...]).astype(BF16)
    acc_ref[...] = x1

    h2 = h2_ref[...]

    def up_half(u_ref, cidx, slot):
        u = _dot(h2, wup_ref[cidx])
        for s in range(nseg):
            u_ref[slot, s, PADR:PADR + seglen, :] = u[s * seglen:(s + 1) * seglen, :]
            u_ref[slot, s, H0:PADR, :] = carry_ref[cidx, s, H0:PADR, :]
            carry_ref[cidx, s, H0:PADR, :] = u[(s + 1) * seglen - (CONV_WIDTH - 1):(s + 1) * seglen, :]

    def conv_half(u_ref, cidx, slot):
        w = cw_ref[cidx]
        b = cb_ref[cidx]
        outs = []
        for s in range(nseg):
            acc = w[0:1] * u_ref[slot, s, H0:H0 + seglen, :]
            for jj in range(1, CONV_WIDTH):
                acc = acc + w[jj:jj + 1] * u_ref[slot, s, H0 + jj:H0 + jj + seglen, :]
            outs.append(b + acc)
        return outs[0] if nseg == 1 else jnp.concatenate(outs, axis=0)

    def stage_up(c):
        up_half(ua_ref, c, c % 2)
        up_half(ub_ref, NC + c, c % 2)

    stage_up(0)
    for c in range(NC):
        if c + 1 < NC:
            stage_up(c + 1)
        a = conv_half(ua_ref, c, c % 2)
        b = conv_half(ub_ref, NC + c, c % 2)
        g = (jax.nn.gelu(a) * b).astype(BF16)
        acc_ref[...] += _dot(g, wdn_ref[c])
    y_ref[...] = _rmsnorm(acc_ref[...], gfin_ref[...])

    @pl.when(i == pl.num_programs(0) - 1)
    def _():
        conv_ref[...] = carry_ref[:, :, H0:PADR, :]


def _mixer_ffn(x, oa, nb, prev, weights, tm, nseg, seglen):
    M = x.shape[0]
    gmix, wg, gng, wpa, wpb, wo, gffn, wup, cw, cb, wdn, gfin = weights
    FC = FFN_CHUNK
    NC = N_FFN_CHUNKS
    row = lambda w: pl.BlockSpec((tm, w), lambda i: (i, 0))
    wspec = lambda a: pl.BlockSpec(a.shape, lambda i, n=a.ndim: (0,) * n,
                                   pipeline_mode=pl.Buffered(1))
    in_specs = [row(D_MODEL),
                pl.BlockSpec((FOX_HEADS // 2, tm, LANES), lambda i: (0, i, 0)),
                row(RET_VW), wspec(prev)] + [wspec(w) for w in weights]
    out_shape = (jax.ShapeDtypeStruct((M, D_MODEL), F32),
                 jax.ShapeDtypeStruct((2 * NC, nseg, CONV_WIDTH - 1, FC), F32))
    out_specs = (row(D_MODEL), _const_spec(out_shape[1].shape))
    return pl.pallas_call(
        functools.partial(_mixer_ffn_kernel, nseg=nseg, seglen=seglen),
        grid=(M // tm,),
        in_specs=in_specs,
        out_specs=out_specs,
        out_shape=out_shape,
        scratch_shapes=[pltpu.VMEM((2 * NC, nseg, 8, FC), F32),
                        pltpu.VMEM((2, nseg, 8 + seglen, FC), F32),
                        pltpu.VMEM((2, nseg, 8 + seglen, FC), F32),
                        pltpu.VMEM((tm, D_MODEL), F32),
                        pltpu.VMEM((tm, D_MODEL), BF16)],
        compiler_params=pltpu.CompilerParams(dimension_semantics=("arbitrary",),
                                             vmem_limit_bytes=VMEM_LIMIT),
        name="mixer_ffn",
    )(x, oa, nb, prev, *weights)


def _rotary_tables(pos):
    half = RET_KEY_DIM // 2
    inv = 1.0 / (ROPE_BASE ** jnp.linspace(0.0, 1.0, half, dtype=F32))
    ang = pos.astype(F32)[:, None] * inv[None, :]
    cos = jnp.cos(ang)
    sin = jnp.sin(ang)
    return jnp.concatenate([cos, cos], axis=1), jnp.concatenate([-sin, sin], axis=1)


def _pad_heads(wt, pad):
    d = wt.shape[1]
    wt = wt.reshape(FOX_HEADS, FOX_HEAD_DIM, d)
    wt = jnp.pad(wt, ((0, 0), (0, pad - FOX_HEAD_DIM), (0, 0)))
    return wt.reshape(FOX_HEADS * pad, d)


def _prompt_consts(tm):
    tri = np.tril(np.ones((tm, tm), np.float32))
    eq = np.zeros((3 * LANES, FOX_HEADS * HEAD_PAD), np.float32)
    ek = np.zeros((3 * LANES, FOX_HEADS * HEAD_PAD), np.float32)
    oneq = np.zeros((1, FOX_HEADS * HEAD_PAD), np.float32)
    onek = np.zeros((1, FOX_HEADS * HEAD_PAD), np.float32)
    onev = np.zeros((1, FOX_HEADS * V_PAD), np.float32)
    for hh in range(FOX_HEADS):
        base = hh * HEAD_PAD + BIAS_COL
        for part in range(3):
            eq[part * LANES + hh, base + part] = 1.0
            ek[part * LANES + hh, base + 3 + part] = -1.0
            onek[0, base + part] = 1.0
            oneq[0, base + 3 + part] = 1.0
        onev[0, hh * V_PAD + FOX_HEAD_DIM] = 1.0
    sel = np.zeros((FOX_HEADS * HEAD_PAD, LANES), np.float32)
    for hh in range(FOX_HEADS):
        sel[hh * HEAD_PAD:hh * HEAD_PAD + FOX_HEAD_DIM, hh] = 1.0
    return (jnp.asarray(tri, BF16), jnp.asarray(eq, BF16), jnp.asarray(ek, BF16),
            jnp.asarray(oneq), jnp.asarray(onek), jnp.asarray(onev), jnp.asarray(sel, BF16))


def _chunk_cols(a):
    lead = a.shape[:-1]
    a = a.reshape(lead + (2 * N_FFN_CHUNKS, FFN_CHUNK))
    return jnp.moveaxis(a, -2, 0)


def _tile(n, pref):
    t = min(n, pref)
    while n % t:
        t //= 2
    return t


def kernel(x_prompt, x_sample, cache_fox_k, cache_fox_v, cache_fox_logf, state_ret, state_ffn_conv,
           norm_mix_g, w_in, b_fox_f, gn_ret_g, w_pa, w_pb, w_o, norm_ffn_g, w_up, conv_w, conv_b,
           w_down, norm_final_g):
    depth = w_in.shape[0]
    Bp, S, _ = x_prompt.shape
    Bs, Ts, _ = x_sample.shape
    P = cache_fox_k.shape[2]
    assert depth == 1 and Bp == 1, "kernel handles the single-layer, single-prompt configuration"
    l = 0

    wt = jnp.swapaxes(w_in[l], 0, 1).astype(BF16)
    o0 = 3 * FOX_W
    o1 = o0 + FOX_HEADS
    o2 = o1 + 2 * RET_KW + RET_VW
    wq_aug = _pad_heads(wt[:FOX_W], HEAD_PAD).T
    wk_aug = _pad_heads(wt[FOX_W:2 * FOX_W], HEAD_PAD).T
    wv_aug = _pad_heads(wt[2 * FOX_W:o0], V_PAD).T
    wqkv = wt[:o0].T
    wf = jnp.pad(wt[o0:o1], ((0, LANES - FOX_HEADS), (0, 0))).T
    bf = jnp.pad(b_fox_f[l].astype(F32), (0, LANES - FOX_HEADS))[None, :]
    wb = wt[o1:o2].T
    wg = wt[o2:].T
    gmix = norm_mix_g[l].astype(F32)[None, :]
    mix_weights = (
        gmix, wg, gn_ret_g[l].astype(F32)[None, :], w_pa[l].astype(BF16), w_pb[l].astype(BF16),
        w_o[l].astype(BF16), norm_ffn_g[l].astype(F32)[None, :],
        _chunk_cols(w_up[l]).astype(BF16),
        jnp.pad(_chunk_cols(conv_w[l].astype(F32)), ((0, 0), (0, 8 - CONV_WIDTH), (0, 0))),
        _chunk_cols(conv_b[l].astype(F32)[None, :]),
        w_down[l].reshape(N_FFN_CHUNKS, FFN_CHUNK, D_MODEL).astype(BF16),
        norm_final_g.astype(F32)[None, :],
    )

    tm_a = _tile(S, 256)
    cos_p, sin_p = _rotary_tables(jnp.arange(S))
    (qT, ka, vT, kT_p, vT_p, logfT_p, qb, kb, vb, stats, sqq) = _inproj_prompt(
        x_prompt[0], gmix, wq_aug, wk_aug, wv_aug, wf, bf, wb, cos_p, sin_p,
        _prompt_consts(tm_a), tm_a)
    oa_p = _fox_prompt(qT, ka, vT, stats, sqq, _tile(S, 1024))
    zero_state = jnp.zeros((1, RET_HEADS, RET_KEY_DIM, RET_VAL_DIM), F32)
    nb_p, ret_p = _retention(qb, kb, vb, zero_state, 1, S, _tile(S, 256))
    tm_d = _tile(S, 256)
    zero_prev = jnp.zeros((2 * N_FFN_CHUNKS, 1, 8, FFN_CHUNK), F32)
    y_p, conv_p = _mixer_ffn(x_prompt[0], oa_p, nb_p, zero_prev, mix_weights, tm_d, 1, tm_d)

    Ms = Bs * Ts
    cos_s, sin_s = _rotary_tables(P + jnp.arange(Ts))
    cos_s = jnp.tile(cos_s, (Bs, 1))
    sin_s = jnp.tile(sin_s, (Bs, 1))
    (q_s, k_s, v_s, logf_s, qb_s, kb_s, vb_s) = _inproj_sample(
        x_sample.reshape(Ms, D_MODEL), gmix, wqkv, wf, bf, wb, cos_s, sin_s)
    KP = ((P + Ts + LANES - 1) // LANES) * LANES
    lf_all = jnp.concatenate([cache_fox_logf[l].astype(F32), logf_s.reshape(Bs, Ts, FOX_HEADS)], axis=1)
    lfT = jnp.pad(jnp.swapaxes(lf_all, 1, 2), ((0, 0), (0, 0), (0, KP - P - Ts)))
    oa_s = _fox_sample(q_s, k_s, v_s, jnp.transpose(cache_fox_k[l], (0, 2, 3, 1)),
                       jnp.transpose(cache_fox_v[l], (0, 2, 3, 1)), lfT, Bs, Ts)
    oa_s = jnp.moveaxis(oa_s.reshape(Ms, FOX_HEADS // 2, LANES), 1, 0)
    nb_s, ret_s = _retention(qb_s, kb_s, vb_s, state_ret[l].astype(F32), Bs, Ts, Ts)
    prev_s = _chunk_cols(state_ffn_conv[l].astype(F32))
    prev_s = jnp.pad(prev_s, ((0, 0), (0, 0), (8 - (CONV_WIDTH - 1), 0), (0, 0)))
    y_s, conv_s = _mixer_ffn(x_sample.reshape(Ms, D_MODEL), oa_s, nb_s, prev_s, mix_weights,
                             Ms, Bs, Ts)

    def unchunk(cv):
        return jnp.moveaxis(cv, 0, 2).reshape(cv.shape[1], CONV_WIDTH - 1, 2 * FFN_DIM)

    hshape = (FOX_HEADS, FOX_HEAD_DIM)
    return (
        y_p[None],
        y_s.reshape(Bs, Ts, D_MODEL),
        jnp.transpose(kT_p, (2, 0, 1))[None, None],
        jnp.transpose(vT_p, (2, 0, 1))[None, None],
        jnp.transpose(logfT_p, (1, 0))[None, None],
        ret_p[None],
        unchunk(conv_p)[None],
        k_s.reshape((1, Bs, Ts) + hshape),
        v_s.reshape((1, Bs, Ts) + hshape),
        logf_s.reshape(1, Bs, Ts, FOX_HEADS),
        ret_s[None],
        unchunk(conv_s)[None],
    )
```

```python
import functools
import math

import numpy as np
import jax
import jax.numpy as jnp
from jax import lax
from jax.experimental import pallas as pl
from jax.experimental.pallas import tpu as pltpu

F32 = jnp.float32
BF16 = jnp.bfloat16

D_MODEL = 1024
FOX_HEADS = 8
FOX_HEAD_DIM = 64
RET_HEADS = 4
RET_KEY_DIM = 128
RET_VAL_DIM = 256
FFN_DIM = 2816
CONV_WIDTH = 3
EPS = 1e-6
ROPE_BASE = 10000.0

FOX_W = FOX_HEADS * FOX_HEAD_DIM
RET_KW = RET_HEADS * RET_KEY_DIM
RET_VW = RET_HEADS * RET_VAL_DIM

LOG2E = 1.4426950408889634
LANES = 128
HEAD_PAD = LANES
V_PAD = 80
BIAS_COL = FOX_HEAD_DIM
NEG = -1e30
STALE_SAFE_LOG2 = 64.0
PRUNE_LOG2 = -160.0
NORM_SLACK = 1.02
FFN_CHUNK = 256
N_FFN_CHUNKS = FFN_DIM // FFN_CHUNK
VMEM_LIMIT = 56 * 1024 * 1024


def _rmsnorm(x, g):
    ms = jnp.mean(x * x, axis=-1, keepdims=True)
    return x * lax.rsqrt(ms + EPS) * g


def _split3(x):
    hi = x.astype(BF16)
    r1 = x - hi.astype(F32)
    mid = r1.astype(BF16)
    lo = (r1 - mid.astype(F32)).astype(BF16)
    return hi, mid, lo


def _log_sigmoid(x):
    return jnp.minimum(x, 0.0) - jnp.log1p(jnp.exp(-jnp.abs(x)))


def _dot(a, b):
    return jnp.dot(a, b, preferred_element_type=F32)


def _rotary(x, cos2, sin2):
    return x * cos2 + pltpu.roll(x, RET_KEY_DIM // 2, 1) * sin2


def _const_spec(shape):
    n = len(shape)
    return pl.BlockSpec(shape, lambda *_: (0,) * n)


def _inproj_prompt_kernel(x_ref, g_ref, wq_ref, wk_ref, wv_ref, wf_ref, bf_ref, wb_ref,
                          cos_ref, sin_ref, tri_ref, eq_ref, ek_ref, oneq_ref, onek_ref, onev_ref,
                          sel_ref,
                          qT_ref, ka_ref, vT_ref, kT32_ref, vT32_ref, logf_ref, qb_ref, kb_ref, vb_ref,
                          stats_ref, sqq_ref, carry_ref):
    tm = x_ref.shape[0]

    @pl.when(pl.program_id(0) == 0)
    def _():
        carry_ref[...] = jnp.zeros_like(carry_ref)

    h = _rmsnorm(x_ref[...], g_ref[...]).astype(BF16)

    logf = _log_sigmoid(_dot(h, wf_ref[...]) + bf_ref[...])
    logf_ref[...] = logf.T[:FOX_HEADS, :]
    lane = lax.broadcasted_iota(jnp.int32, logf.shape, 1)
    logf = jnp.where(lane < FOX_HEADS, logf, 0.0)
    r = _dot(tri_ref[...], jnp.concatenate(_split3(logf), axis=1))
    c = r[:, :LANES] + r[:, LANES:2 * LANES] + r[:, 2 * LANES:] + carry_ref[...]
    carry_ref[...] = c[tm - 1:tm, :]
    c3 = jnp.concatenate(_split3(c * LOG2E), axis=1)

    q_aug = (_dot(h, wq_ref[...]) * (FOX_HEAD_DIM ** -0.5 * LOG2E)
             + _dot(c3, eq_ref[...]) + oneq_ref[...])
    qT = q_aug.T.astype(BF16)
    k_aug = _dot(h, wk_ref[...]) + _dot(c3, ek_ref[...]) + onek_ref[...]

    sel = sel_ref[...]
    nq2 = jnp.max(_dot((q_aug * q_aug).astype(BF16), sel), axis=0, keepdims=True)
    nk2 = jnp.max(_dot((k_aug * k_aug).astype(BF16), sel), axis=0, keepdims=True)
    c2 = c * LOG2E
    stats_ref[0] = jnp.concatenate(
        [nq2, nk2, c2[0:1, :], c2[tm - 1:tm, :], jnp.zeros((4, LANES), F32)], axis=0)
    sqq = jnp.where(lane < FOX_HEADS, _dot((q_aug * k_aug).astype(BF16), sel), 0.0)
    sqq_ref[...] = sqq.T[:FOX_HEADS, :]

    kT32 = k_aug.T
    k_aug = k_aug.astype(BF16)
    vT32 = (_dot(h, wv_ref[...]) + onev_ref[...]).T
    vT = vT32.astype(BF16)
    for hh in range(FOX_HEADS):
        qT_ref[hh] = qT[hh * HEAD_PAD:(hh + 1) * HEAD_PAD, :]
        ka_ref[hh] = k_aug[:, hh * HEAD_PAD:(hh + 1) * HEAD_PAD]
        vT_ref[hh] = vT[hh * V_PAD:(hh + 1) * V_PAD, :]
        kT32_ref[hh] = kT32[hh * HEAD_PAD:hh * HEAD_PAD + FOX_HEAD_DIM, :]
        vT32_ref[hh] = vT32[hh * V_PAD:hh * V_PAD + FOX_HEAD_DIM, :]

    zb = _dot(h, wb_ref[...])
    cos2 = cos_ref[...]
    sin2 = sin_ref[...]
    for hh in range(RET_HEADS):
        sl = slice(hh * RET_KEY_DIM, (hh + 1) * RET_KEY_DIM)
        qb_ref[:, sl] = _rotary(zb[:, sl], cos2, sin2).astype(BF16)
        xk = zb[:, RET_KW + hh * RET_KEY_DIM:RET_KW + (hh + 1) * RET_KEY_DIM]
        kb_ref[:, sl] = (_rotary(xk, cos2, sin2) * (RET_KEY_DIM ** -0.5)).astype(BF16)
    vb_ref[...] = zb[:, 2 * RET_KW:].astype(BF16)


def _inproj_prompt(x, g, wq, wk, wv, wf, bf, wb, cos2, sin2, consts, tm):
    S = x.shape[0]
    tri, eq, ek, oneq, onek, onev, sel = consts
    row = lambda w: pl.BlockSpec((tm, w), lambda i: (i, 0))
    headT = pl.BlockSpec((FOX_HEADS, FOX_HEAD_DIM, tm), lambda i: (0, 0, i))
    in_specs = [row(D_MODEL), _const_spec(g.shape), _const_spec(wq.shape), _const_spec(wk.shape),
                _const_spec(wv.shape), _const_spec(wf.shape),
                _const_spec(bf.shape), _const_spec(wb.shape), row(LANES), row(LANES),
                _const_spec(tri.shape), _const_spec(eq.shape), _const_spec(ek.shape),
                _const_spec(oneq.shape), _const_spec(onek.shape), _const_spec(onev.shape),
                _const_spec(sel.shape)]
    out_shape = (
        jax.ShapeDtypeStruct((FOX_HEADS, HEAD_PAD, S), BF16),
        jax.ShapeDtypeStruct((FOX_HEADS, S, HEAD_PAD), BF16),
        jax.ShapeDtypeStruct((FOX_HEADS, V_PAD, S), BF16),
        jax.ShapeDtypeStruct((FOX_HEADS, FOX_HEAD_DIM, S), F32),
        jax.ShapeDtypeStruct((FOX_HEADS, FOX_HEAD_DIM, S), F32),
        jax.ShapeDtypeStruct((FOX_HEADS, S), F32),
        jax.ShapeDtypeStruct((S, RET_KW), BF16),
        jax.ShapeDtypeStruct((S, RET_KW), BF16),
        jax.ShapeDtypeStruct((S, RET_VW), BF16),
        jax.ShapeDtypeStruct((S // tm, 8, LANES), F32),
        jax.ShapeDtypeStruct((FOX_HEADS, S), F32),
    )
    out_specs = (
        pl.BlockSpec((FOX_HEADS, HEAD_PAD, tm), lambda i: (0, 0, i)),
        pl.BlockSpec((FOX_HEADS, tm, HEAD_PAD), lambda i: (0, i, 0)),
        pl.BlockSpec((FOX_HEADS, V_PAD, tm), lambda i: (0, 0, i)),
        headT, headT, pl.BlockSpec((FOX_HEADS, tm), lambda i: (0, i)),
        row(RET_KW), row(RET_KW), row(RET_VW),
        pl.BlockSpec((1, 8, LANES), lambda i: (i, 0, 0)),
        pl.BlockSpec((FOX_HEADS, tm), lambda i: (0, i)),
    )
    return pl.pallas_call(
        _inproj_prompt_kernel,
        grid=(S // tm,),
        in_specs=in_specs,
        out_specs=out_specs,
        out_shape=out_shape,
        scratch_shapes=[pltpu.VMEM((1, LANES), F32)],
        compiler_params=pltpu.CompilerParams(dimension_semantics=("arbitrary",),
                                             vmem_limit_bytes=VMEM_LIMIT),
        name="inproj_prompt",
    )(x, g, wq, wk, wv, wf, bf, wb, cos2, sin2, tri, eq, ek, oneq, onek, onev, sel)


def _inproj_sample_kernel(x_ref, g_ref, wqkv_ref, wf_ref, bf_ref, wb_ref, cos_ref, sin_ref,
                          q_ref, k32_ref, v32_ref, logf_ref, qb_ref, kb_ref, vb_ref):
    h = _rmsnorm(x_ref[...], g_ref[...]).astype(BF16)
    logf = _log_sigmoid(_dot(h, wf_ref[...]) + bf_ref[...])
    logf_ref[...] = logf[:, :FOX_HEADS]
    z = _dot(h, wqkv_ref[...])
    q_ref[...] = (z[:, :FOX_W] * (FOX_HEAD_DIM ** -0.5 * LOG2E)).astype(BF16)
    k32_ref[...] = z[:, FOX_W:2 * FOX_W]
    v32_ref[...] = z[:, 2 * FOX_W:]
    zb = _dot(h, wb_ref[...])
    cos2 = cos_ref[...]
    sin2 = sin_ref[...]
    for hh in range(RET_HEADS):
        sl = slice(hh * RET_KEY_DIM, (hh + 1) * RET_KEY_DIM)
        qb_ref[:, sl] = _rotary(zb[:, sl], cos2, sin2).astype(BF16)
        xk = zb[:, RET_KW + hh * RET_KEY_DIM:RET_KW + (hh + 1) * RET_KEY_DIM]
        kb_ref[:, sl] = (_rotary(xk, cos2, sin2) * (RET_KEY_DIM ** -0.5)).astype(BF16)
    vb_ref[...] = zb[:, 2 * RET_KW:].astype(BF16)


def _inproj_sample(x, g, wqkv, wf, bf, wb, cos2, sin2):
    M = x.shape[0]
    args = (x, g, wqkv, wf, bf, wb, cos2, sin2)
    out_shape = (
        jax.ShapeDtypeStruct((M, FOX_W), BF16),
        jax.ShapeDtypeStruct((M, FOX_W), F32),
        jax.ShapeDtypeStruct((M, FOX_W), F32),
        jax.ShapeDtypeStruct((M, FOX_HEADS), F32),
        jax.ShapeDtypeStruct((M, RET_KW), BF16),
        jax.ShapeDtypeStruct((M, RET_KW), BF16),
        jax.ShapeDtypeStruct((M, RET_VW), BF16),
    )
    return pl.pallas_call(
        _inproj_sample_kernel,
        grid=(1,),
        in_specs=[_const_spec(a.shape) for a in args],
        out_specs=tuple(_const_spec(o.shape) for o in out_shape),
        out_shape=out_shape,
        compiler_params=pltpu.CompilerParams(dimension_semantics=("arbitrary",),
                                             vmem_limit_bytes=VMEM_LIMIT),
        name="inproj_sample",
    )(*args)


def _fox_prompt_kernel(it_ref, jt_ref, jfetch_ref, mode_ref, qT_ref, ka_ref, vT_ref,
                       sqq_ref, o_ref, m_ref, acc_ref, *, n_strips):
    del jfetch_ref
    t = pl.program_id(0)
    i = it_ref[t]
    j = jt_ref[t]
    T = qT_ref.shape[2]
    SUB = T // n_strips
    EXACT = n_strips + 1

    def heads(fn):
        def body(hh, carry):
            fn(hh, mode_ref[t * FOX_HEADS + hh])
            return carry
        lax.fori_loop(0, FOX_HEADS, body, 0)

    def scores(hh):
        return _dot(ka_ref[hh], qT_ref[hh])

    def exact_head(hh):
        s = scores(hh)
        kk = lax.broadcasted_iota(jnp.int32, s.shape, 0) + j * T
        qq = lax.broadcasted_iota(jnp.int32, s.shape, 1) + i * T
        s = jnp.where(kk > qq, NEG, s)
        m_old = m_ref[hh]
        m_new = jnp.maximum(m_old, jnp.max(s, axis=0, keepdims=True))
        p = jnp.exp2(s - m_new).astype(BF16)
        alpha = jnp.exp2(m_old - m_new)
        acc_ref[hh] = alpha * acc_ref[hh] + _dot(vT_ref[hh], p)
        m_ref[hh] = m_new


    def diag_head(hh, mode):
        m_ref[hh] = sqq_ref[pl.ds(hh, 1), :]

        @pl.when(mode != EXACT)
        def _():
            s = scores(hh)
            kk = lax.broadcasted_iota(jnp.int32, s.shape, 0)
            qq = lax.broadcasted_iota(jnp.int32, s.shape, 1)
            p = jnp.exp2(jnp.where(kk > qq, NEG, s) - m_ref[hh]).astype(BF16)
            acc_ref[hh] = _dot(vT_ref[hh], p)

        @pl.when(mode == EXACT)
        def _():
            acc_ref[hh] = jnp.zeros((V_PAD, T), F32)
            exact_head(hh)

    def off_head(hh, mode):
        for nn in range(1, n_strips + 1):
            k0 = (n_strips - nn) * SUB

            @pl.when(mode == nn)
            def _():
                p = jnp.exp2(_dot(ka_ref[hh, k0:, :], qT_ref[hh]) - m_ref[hh]).astype(BF16)
                acc_ref[hh] += _dot(vT_ref[hh, :, k0:], p)

        @pl.when(mode == EXACT)
        def _():
            exact_head(hh)

    @pl.when(j == i)
    def _():
        heads(diag_head)

    @pl.when(j < i)
    def _():
        heads(off_head)

    @pl.when(j == 0)
    def _():
        for pr in range(FOX_HEADS // 2):
            halves = []
            for hh in (2 * pr, 2 * pr + 1):
                a = acc_ref[hh]
                halves.append(a[:FOX_HEAD_DIM] / a[FOX_HEAD_DIM:FOX_HEAD_DIM + 1])
            o_ref[pr] = jnp.concatenate(halves, axis=0).T.astype(BF16)


def _prune_tables(stats, it, jt, nb):
    per = stats.shape[0] // nb
    st = stats.reshape(nb, per, 8, LANES)[:, :, :, :FOX_HEADS]
    nq = jnp.sqrt(jnp.max(st[:, :, 0, :], axis=1)) * NORM_SLACK
    nk_strip = jnp.sqrt(st[:, :, 1, :]) * NORM_SLACK
    nk = jnp.max(nk_strip, axis=1)
    c_first = st[:, 0, 2, :]
    c_last = st[:, :, 3, :]
    bound = (nq[it][:, None, :] * (nk_strip[jt] + nk[it][:, None, :])
             - (c_last[jt] - c_first[it][:, None, :]))
    live = jnp.logical_or(jnp.asarray(jt == it)[:, None, None],
                          jnp.logical_not(bound < PRUNE_LOG2))
    strip_no = jnp.arange(per, dtype=jnp.int32)[None, :, None]
    n_keep = per - jnp.min(jnp.where(live, strip_no, per), axis=1)
    safe = nq[it] * (nk[jt] + nk[it]) < STALE_SAFE_LOG2
    mode = jnp.where(n_keep == 0, 0, jnp.where(safe, n_keep, per + 1))
    steps = jnp.arange(len(it), dtype=jnp.int32)
    last_live = lax.cummax(jnp.where(jnp.any(n_keep > 0, axis=1), steps, 0))
    return jnp.asarray(jt)[last_live], mode.astype(jnp.int32).reshape(-1), per


def _fox_prompt(qT, ka, vT, stats, sqq, T):
    S = ka.shape[1]
    nb = S // T
    it = np.array([i for i in range(nb) for _ in range(i + 1)], np.int32)
    jt = np.array([j for i in range(nb) for j in range(i, -1, -1)], np.int32)
    jfetch, mode, n_strips = _prune_tables(stats, it, jt, nb)
    grid_spec = pltpu.PrefetchScalarGridSpec(
        num_scalar_prefetch=4,
        grid=(len(it),),
        in_specs=[
            pl.BlockSpec((FOX_HEADS, HEAD_PAD, T), lambda t, it, jt, jf, md: (0, 0, it[t])),
            pl.BlockSpec((FOX_HEADS, T, HEAD_PAD), lambda t, it, jt, jf, md: (0, jf[t], 0)),
            pl.BlockSpec((FOX_HEADS, V_PAD, T), lambda t, it, jt, jf, md: (0, 0, jf[t])),
            pl.BlockSpec((FOX_HEADS, T), lambda t, it, jt, jf, md: (0, it[t])),
        ],
        out_specs=pl.BlockSpec((FOX_HEADS // 2, T, LANES),
                               lambda t, it, jt, jf, md: (0, it[t], 0)),
        scratch_shapes=[pltpu.VMEM((FOX_HEADS, 1, T), F32),
                        pltpu.VMEM((FOX_HEADS, V_PAD, T), F32)],
    )
    return pl.pallas_call(
        functools.partial(_fox_prompt_kernel, n_strips=n_strips),
        grid_spec=grid_spec,
        out_shape=jax.ShapeDtypeStruct((FOX_HEADS // 2, S, LANES), BF16),
        compiler_params=pltpu.CompilerParams(dimension_semantics=("arbitrary",),
                                             vmem_limit_bytes=VMEM_LIMIT),
        name="fox_prompt",
    )(jnp.asarray(it), jnp.asarray(jt), jfetch, mode, qT, ka, vT, sqq)


def _fox_sample_kernel(q_ref, kn_ref, vn_ref, ckT_ref, cvT_ref, lfT_ref, up_ref, ex_ref, o_ref):
    P = ckT_ref.shape[3]
    Tn = q_ref.shape[0]
    KP = lfT_ref.shape[2]
    HQ = FOX_HEADS * Tn
    nchunk = KP // LANES
    nt = (((1,), (1,)), ((), ()))

    def stack3(x):
        parts3 = [t.astype(F32) for t in _split3(x)] + [jnp.zeros_like(x)]
        return jnp.concatenate(parts3, axis=0).astype(BF16)

    x3 = stack3(lfT_ref[0])
    up = up_ref[...]
    run = jnp.zeros((4 * FOX_HEADS, 1), F32)
    parts = []
    for cidx in range(nchunk):
        y = _dot(x3[:, cidx * LANES:(cidx + 1) * LANES], up) + run
        parts.append(y)
        run = y[:, LANES - 1:LANES]
    y = jnp.concatenate(parts, axis=1)
    cT = (y[:FOX_HEADS] + y[FOX_HEADS:2 * FOX_HEADS] + y[2 * FOX_HEADS:3 * FOX_HEADS]) * LOG2E
    ckx = _dot(ex_ref[...], stack3(cT))

    tail = ckx[:, P:P + LANES]
    rowq = lax.broadcasted_iota(jnp.int32, tail.shape, 0) % Tn
    lanek = lax.broadcasted_iota(jnp.int32, tail.shape, 1)
    cq = jnp.sum(jnp.where(lanek == rowq, tail, 0.0), axis=1, keepdims=True)

    q = q_ref[...]
    qt = jnp.concatenate([q] * FOX_HEADS, axis=0)
    rh = lax.broadcasted_iota(jnp.int32, qt.shape, 0) // Tn
    lh = lax.broadcasted_iota(jnp.int32, qt.shape, 1) // FOX_HEAD_DIM
    qbd = jnp.where(rh == lh, qt, jnp.zeros_like(qt))

    kT = ckT_ref[0].reshape(FOX_W, P).astype(BF16)
    vT = cvT_ref[0].reshape(FOX_W, P).astype(BF16)
    s_c = _dot(qbd, kT) + cq - ckx[:, :P]
    s_n = lax.dot_general(qbd, kn_ref[...].astype(BF16), nt, preferred_element_type=F32)
    s_n = s_n + cq - ckx[:, P:P + Tn]
    key = lax.broadcasted_iota(jnp.int32, s_n.shape, 1)
    qrow = lax.broadcasted_iota(jnp.int32, s_n.shape, 0) % Tn
    s_n = jnp.where(key > qrow, NEG, s_n)
    m = jnp.maximum(jnp.max(s_c, axis=1, keepdims=True), jnp.max(s_n, axis=1, keepdims=True))
    p_c = jnp.exp2(s_c - m)
    p_n = jnp.exp2(s_n - m)
    l = jnp.sum(p_c, axis=1, keepdims=True) + jnp.sum(p_n, axis=1, keepdims=True)
    z = lax.dot_general(p_c.astype(BF16), vT, nt, preferred_element_type=F32)
    z = (z + _dot(p_n.astype(BF16), vn_ref[...].astype(BF16))) / l
    zh = lax.broadcasted_iota(jnp.int32, (Tn, FOX_W), 1) // FOX_HEAD_DIM
    o = jnp.zeros((Tn, FOX_W), F32)
    for hh in range(FOX_HEADS):
        o = o + jnp.where(zh == hh, z[hh * Tn:(hh + 1) * Tn, :], 0.0)
    o_ref[...] = o.astype(BF16)


def _fox_sample(q, kn, vn, cache_kT, cache_vT, lfT, B, Tn):
    P = cache_kT.shape[3]
    KP = lfT.shape[2]
    HQ = FOX_HEADS * Tn
    up = jnp.asarray(np.triu(np.ones((LANES, LANES), np.float32)), BF16)
    ex = np.zeros((HQ, 4 * FOX_HEADS), np.float32)
    for part in range(3):
        for hh in range(FOX_HEADS):
            ex[hh * Tn:(hh + 1) * Tn, part * FOX_HEADS + hh] = 1.0
    ex = jnp.asarray(ex, BF16)
    rowb = lambda w: pl.BlockSpec((Tn, w), lambda b: (b, 0))
    return pl.pallas_call(
        _fox_sample_kernel,
        grid=(B,),
        in_specs=[rowb(FOX_W), rowb(FOX_W), rowb(FOX_W),
                  pl.BlockSpec((1, FOX_HEADS, FOX_HEAD_DIM, P), lambda b: (b, 0, 0, 0)),
                  pl.BlockSpec((1, FOX_HEADS, FOX_HEAD_DIM, P), lambda b: (b, 0, 0, 0)),
                  pl.BlockSpec((1, FOX_HEADS, KP), lambda b: (b, 0, 0)),
                  _const_spec(up.shape), _const_spec(ex.shape)],
        out_specs=rowb(FOX_W),
        out_shape=jax.ShapeDtypeStruct((B * Tn, FOX_W), BF16),
        compiler_params=pltpu.CompilerParams(dimension_semantics=("arbitrary",),
                                             vmem_limit_bytes=VMEM_LIMIT),
        name="fox_sample",
    )(q, kn, vn, cache_kT, cache_vT, lfT, up, ex)


def _retention_kernel(q_ref, k_ref, v_ref, s0_ref, dmat_ref, xi_ref, zeta_ref, gam_ref,
                      n_ref, sout_ref, st_ref):
    c = pl.program_id(1)

    @pl.when(c == 0)
    def _():
        st_ref[...] = s0_ref[0]

    for hh in range(RET_HEADS):
        q = q_ref[:, hh * RET_KEY_DIM:(hh + 1) * RET_KEY_DIM]
        k = k_ref[:, hh * RET_KEY_DIM:(hh + 1) * RET_KEY_DIM]
        v = v_ref[:, hh * RET_VAL_DIM:(hh + 1) * RET_VAL_DIM]
        st = st_ref[hh]
        sc = lax.dot_general(q, k, (((1,), (1,)), ((), ())), preferred_element_type=F32)
        sc = sc * dmat_ref[hh]
        o = _dot(sc.astype(BF16), v) + _dot(q, st.astype(BF16)) * xi_ref[hh]
        kz = (k.astype(F32) * zeta_ref[hh]).astype(BF16)
        upd = lax.dot_general(kz, v, (((0,), (0,)), ((), ())), preferred_element_type=F32)
        st_ref[hh] = gam_ref[hh] * st + upd
        mu = jnp.mean(o, axis=-1, keepdims=True)
        d = o - mu
        var = jnp.mean(d * d, axis=-1, keepdims=True)
        n_ref[:, hh * RET_VAL_DIM:(hh + 1) * RET_VAL_DIM] = (d * lax.rsqrt(var + EPS)).astype(BF16)

    @pl.when(c == pl.num_programs(1) - 1)
    def _():
        sout_ref[0] = st_ref[...]


def _ret_log_gamma():
    return jnp.log(1.0 - jnp.exp2(-5.0 - jnp.arange(RET_HEADS, dtype=F32)))


def _retention(q, k, v, state0, B, L, C):
    nc = L // C
    lg = _ret_log_gamma()
    idx = jnp.arange(C, dtype=F32)
    diff = idx[:, None] - idx[None, :]
    dmat = jnp.where(diff[None] >= 0, jnp.exp(jnp.maximum(diff, 0.0)[None] * lg[:, None, None]), 0.0)
    xi = jnp.exp((idx[None, :] + 1.0) * lg[:, None])
    zeta = jnp.exp((C - 1.0 - idx[None, :]) * lg[:, None])
    xi = jnp.broadcast_to(xi[:, :, None], (RET_HEADS, C, RET_VAL_DIM))
    zeta = jnp.broadcast_to(zeta[:, :, None], (RET_HEADS, C, RET_KEY_DIM))
    gam = jnp.broadcast_to(jnp.exp(C * lg)[:, None, None], (RET_HEADS, 1, RET_VAL_DIM))
    rowc = lambda w: pl.BlockSpec((C, w), lambda b, c: (b * nc + c, 0))
    st_spec = pl.BlockSpec((1, RET_HEADS, RET_KEY_DIM, RET_VAL_DIM), lambda b, c: (b, 0, 0, 0))
    return pl.pallas_call(
        _retention_kernel,
        grid=(B, nc),
        in_specs=[rowc(RET_KW), rowc(RET_KW), rowc(RET_VW), st_spec,
                  _const_spec(dmat.shape), _const_spec(xi.shape), _const_spec(zeta.shape),
                  _const_spec(gam.shape)],
        out_specs=(rowc(RET_VW), st_spec),
        out_shape=(jax.ShapeDtypeStruct((B * L, RET_VW), BF16),
                   jax.ShapeDtypeStruct((B, RET_HEADS, RET_KEY_DIM, RET_VAL_DIM), F32)),
        scratch_shapes=[pltpu.VMEM((RET_HEADS, RET_KEY_DIM, RET_VAL_DIM), F32)],
        compiler_params=pltpu.CompilerParams(dimension_semantics=("arbitrary", "arbitrary"),
                                             vmem_limit_bytes=VMEM_LIMIT),
        name="retention",
    )(q, k, v, state0, dmat, xi, zeta, gam)


def _mixer_ffn_kernel(x_ref, oa_ref, nb_ref, prev_ref, gmix_ref, wg_ref, gng_ref, wpa_ref, wpb_ref,
                      wo_ref, gffn_ref, wup_ref, cw_ref, cb_ref, wdn_ref, gfin_ref,
                      y_ref, conv_ref, carry_ref, ua_ref, ub_ref, acc_ref, h2_ref,
                      *, nseg, seglen):
    i = pl.program_id(0)
    NC = N_FFN_CHUNKS
    PADR = 8
    H0 = PADR - (CONV_WIDTH - 1)

    @pl.when(i == 0)
    def _():
        carry_ref[...] = prev_ref[...]

    x = x_ref[...]
    h = _rmsnorm(x, gmix_ref[...]).astype(BF16)
    zg = _dot(h, wg_ref[...])
    gb = zg[:, :RET_VW]
    gma = zg[:, RET_VW:RET_VW + D_MODEL]
    gmb = zg[:, RET_VW + D_MODEL:]
    oa = jnp.concatenate([oa_ref[p] for p in range(FOX_HEADS // 2)], axis=1)
    ya = _dot(oa, wpa_ref[...])
    nn = nb_ref[...].astype(F32) * gng_ref[...] * (gb * jax.nn.sigmoid(gb))
    yb = _dot(nn.astype(BF16), wpb_ref[...])
    y = jax.nn.sigmoid(gma) * ya + jax.nn.sigmoid(gmb) * yb
    x1 = x + _dot(y.astype(BF16), wo_ref[...])
    h2_ref[...] = _rmsnorm(x1, gffn_ref[...]).astype(BF16)
    acc_ref[...] = x1

    h2 = h2_ref[...]

    def up_half(u_ref, cidx, slot):
        u = _dot(h2, wup_ref[cidx])
        for s in range(nseg):
            u_ref[slot, s, PADR:PADR + seglen, :] = u[s * seglen:(s + 1) * seglen, :]
            u_ref[slot, s, H0:PADR, :] = carry_ref[cidx, s, H0:PADR, :]
            carry_ref[cidx, s, H0:PADR, :] = u[(s + 1) * seglen - (CONV_WIDTH - 1):(s + 1) * seglen, :]

    def conv_half(u_ref, cidx, slot):
        w = cw_ref[cidx]
        b = cb_ref[cidx]
        outs = []
        for s in range(nseg):
            acc = w[0:1] * u_ref[slot, s, H0:H0 + seglen, :]
            for jj in range(1, CONV_WIDTH):
                acc = acc + w[jj:jj + 1] * u_ref[slot, s, H0 + jj:H0 + jj + seglen, :]
            outs.append(b + acc)
        return outs[0] if nseg == 1 else jnp.concatenate(outs, axis=0)

    def stage_up(c):
        up_half(ua_ref, c, c % 2)
        up_half(ub_ref, NC + c, c % 2)

    stage_up(0)
    for c in range(NC):
        if c + 1 < NC:
            stage_up(c + 1)
        a = conv_half(ua_ref, c, c % 2)
        b = conv_half(ub_ref, NC + c, c % 2)
        g = (jax.nn.gelu(a) * b).astype(BF16)
        acc_ref[...] += _dot(g, wdn_ref[c])
    y_ref[...] = _rmsnorm(acc_ref[...], gfin_ref[...])

    @pl.when(i == pl.num_programs(0) - 1)
    def _():
        conv_ref[...] = carry_ref[:, :, H0:PADR, :]


def _mixer_ffn(x, oa, nb, prev, weights, tm, nseg, seglen):
    M = x.shape[0]
    gmix, wg, gng, wpa, wpb, wo, gffn, wup, cw, cb, wdn, gfin = weights
    FC = FFN_CHUNK
    NC = N_FFN_CHUNKS
    row = lambda w: pl.BlockSpec((tm, w), lambda i: (i, 0))
    wspec = lambda a: pl.BlockSpec(a.shape, lambda i, n=a.ndim: (0,) * n,
                                   pipeline_mode=pl.Buffered(1))
    in_specs = [row(D_MODEL),
                pl.BlockSpec((FOX_HEADS // 2, tm, LANES), lambda i: (0, i, 0)),
                row(RET_VW), wspec(prev)] + [wspec(w) for w in weights]
    out_shape = (jax.ShapeDtypeStruct((M, D_MODEL), F32),
                 jax.ShapeDtypeStruct((2 * NC, nseg, CONV_WIDTH - 1, FC), F32))
    out_specs = (row(D_MODEL), _const_spec(out_shape[1].shape))
    return pl.pallas_call(
        functools.partial(_mixer_ffn_kernel, nseg=nseg, seglen=seglen),
        grid=(M // tm,),
        in_specs=in_specs,
        out_specs=out_specs,
        out_shape=out_shape,
        scratch_shapes=[pltpu.VMEM((2 * NC, nseg, 8, FC), F32),
                        pltpu.VMEM((2, nseg, 8 + seglen, FC), F32),
                        pltpu.VMEM((2, nseg, 8 + seglen, FC), F32),
                        pltpu.VMEM((tm, D_MODEL), F32),
                        pltpu.VMEM((tm, D_MODEL), BF16)],
        compiler_params=pltpu.CompilerParams(dimension_semantics=("arbitrary",),
                                             vmem_limit_bytes=VMEM_LIMIT),
        name="mixer_ffn",
    )(x, oa, nb, prev, *weights)


def _rotary_tables(pos):
    half = RET_KEY_DIM // 2
    inv = 1.0 / (ROPE_BASE ** jnp.linspace(0.0, 1.0, half, dtype=F32))
    ang = pos.astype(F32)[:, None] * inv[None, :]
    cos = jnp.cos(ang)
    sin = jnp.sin(ang)
    return jnp.concatenate([cos, cos], axis=1), jnp.concatenate([-sin, sin], axis=1)


def _pad_heads(wt, pad):
    d = wt.shape[1]
    wt = wt.reshape(FOX_HEADS, FOX_HEAD_DIM, d)
    wt = jnp.pad(wt, ((0, 0), (0, pad - FOX_HEAD_DIM), (0, 0)))
    return wt.reshape(FOX_HEADS * pad, d)


def _prompt_consts(tm):
    tri = np.tril(np.ones((tm, tm), np.float32))
    eq = np.zeros((3 * LANES, FOX_HEADS * HEAD_PAD), np.float32)
    ek = np.zeros((3 * LANES, FOX_HEADS * HEAD_PAD), np.float32)
    oneq = np.zeros((1, FOX_HEADS * HEAD_PAD), np.float32)
    onek = np.zeros((1, FOX_HEADS * HEAD_PAD), np.float32)
    onev = np.zeros((1, FOX_HEADS * V_PAD), np.float32)
    for hh in range(FOX_HEADS):
        base = hh * HEAD_PAD + BIAS_COL
        for part in range(3):
            eq[part * LANES + hh, base + part] = 1.0
            ek[part * LANES + hh, base + 3 + part] = -1.0
            onek[0, base + part] = 1.0
            oneq[0, base + 3 + part] = 1.0
        onev[0, hh * V_PAD + FOX_HEAD_DIM] = 1.0
    sel = np.zeros((FOX_HEADS * HEAD_PAD, LANES), np.float32)
    for hh in range(FOX_HEADS):
        sel[hh * HEAD_PAD:hh * HEAD_PAD + FOX_HEAD_DIM, hh] = 1.0
    return (jnp.asarray(tri, BF16), jnp.asarray(eq, BF16), jnp.asarray(ek, BF16),
            jnp.asarray(oneq), jnp.asarray(onek), jnp.asarray(onev), jnp.asarray(sel, BF16))


def _chunk_cols(a):
    lead = a.shape[:-1]
    a = a.reshape(lead + (2 * N_FFN_CHUNKS, FFN_CHUNK))
    return jnp.moveaxis(a, -2, 0)


def _tile(n, pref):
    t = min(n, pref)
    while n % t:
        t //= 2
    return t


def kernel(x_prompt, x_sample, cache_fox_k, cache_fox_v, cache_fox_logf, state_ret, state_ffn_conv,
           norm_mix_g, w_in, b_fox_f, gn_ret_g, w_pa, w_pb, w_o, norm_ffn_g, w_up, conv_w, conv_b,
           w_down, norm_final_g):
    depth = w_in.shape[0]
    Bp, S, _ = x_prompt.shape
    Bs, Ts, _ = x_sample.shape
    P = cache_fox_k.shape[2]
    assert depth == 1 and Bp == 1, "kernel handles the single-layer, single-prompt configuration"
    l = 0

    wt = jnp.swapaxes(w_in[l], 0, 1).astype(BF16)
    o0 = 3 * FOX_W
    o1 = o0 + FOX_HEADS
    o2 = o1 + 2 * RET_KW + RET_VW
    wq_aug = _pad_heads(wt[:FOX_W], HEAD_PAD).T
    wk_aug = _pad_heads(wt[FOX_W:2 * FOX_W], HEAD_PAD).T
    wv_aug = _pad_heads(wt[2 * FOX_W:o0], V_PAD).T
    wqkv = wt[:o0].T
    wf = jnp.pad(wt[o0:o1], ((0, LANES - FOX_HEADS), (0, 0))).T
    bf = jnp.pad(b_fox_f[l].astype(F32), (0, LANES - FOX_HEADS))[None, :]
    wb = wt[o1:o2].T
    wg = wt[o2:].T
    gmix = norm_mix_g[l].astype(F32)[None, :]
    mix_weights = (
        gmix, wg, gn_ret_g[l].astype(F32)[None, :], w_pa[l].astype(BF16), w_pb[l].astype(BF16),
        w_o[l].astype(BF16), norm_ffn_g[l].astype(F32)[None, :],
        _chunk_cols(w_up[l]).astype(BF16),
        jnp.pad(_chunk_cols(conv_w[l].astype(F32)), ((0, 0), (0, 8 - CONV_WIDTH), (0, 0))),
        _chunk_cols(conv_b[l].astype(F32)[None, :]),
        w_down[l].reshape(N_FFN_CHUNKS, FFN_CHUNK, D_MODEL).astype(BF16),
        norm_final_g.astype(F32)[None, :],
    )

    tm_a = _tile(S, 256)
    cos_p, sin_p = _rotary_tables(jnp.arange(S))
    (qT, ka, vT, kT_p, vT_p, logfT_p, qb, kb, vb, stats, sqq) = _inproj_prompt(
        x_prompt[0], gmix, wq_aug, wk_aug, wv_aug, wf, bf, wb, cos_p, sin_p,
        _prompt_consts(tm_a), tm_a)
    oa_p = _fox_prompt(qT, ka, vT, stats, sqq, _tile(S, 1024))
    zero_state = jnp.zeros((1, RET_HEADS, RET_KEY_DIM, RET_VAL_DIM), F32)
    nb_p, ret_p = _retention(qb, kb, vb, zero_state, 1, S, _tile(S, 256))
    tm_d = _tile(S, 256)
    zero_prev = jnp.zeros((2 * N_FFN_CHUNKS, 1, 8, FFN_CHUNK), F32)
    y_p, conv_p = _mixer_ffn(x_prompt[0], oa_p, nb_p, zero_prev, mix_weights, tm_d, 1, tm_d)

    Ms = Bs * Ts
    cos_s, sin_s = _rotary_tables(P + jnp.arange(Ts))
    cos_s = jnp.tile(cos_s, (Bs, 1))
    sin_s = jnp.tile(sin_s, (Bs, 1))
    (q_s, k_s, v_s, logf_s, qb_s, kb_s, vb_s) = _inproj_sample(
        x_sample.reshape(Ms, D_MODEL), gmix, wqkv, wf, bf, wb, cos_s, sin_s)
    KP = ((P + Ts + LANES - 1) // LANES) * LANES
    lf_all = jnp.concatenate([cache_fox_logf[l].astype(F32), logf_s.reshape(Bs, Ts, FOX_HEADS)], axis=1)
    lfT = jnp.pad(jnp.swapaxes(lf_all, 1, 2), ((0, 0), (0, 0), (0, KP - P - Ts)))
    oa_s = _fox_sample(q_s, k_s, v_s, jnp.transpose(cache_fox_k[l], (0, 2, 3, 1)),
                       jnp.transpose(cache_fox_v[l], (0, 2, 3, 1)), lfT, Bs, Ts)
    oa_s = jnp.moveaxis(oa_s.reshape(Ms, FOX_HEADS // 2, LANES), 1, 0)
    nb_s, ret_s = _retention(qb_s, kb_s, vb_s, state_ret[l].astype(F32), Bs, Ts, Ts)
    prev_s = _chunk_cols(state_ffn_conv[l].astype(F32))
    prev_s = jnp.pad(prev_s, ((0, 0), (0, 0), (8 - (CONV_WIDTH - 1), 0), (0, 0)))
    y_s, conv_s = _mixer_ffn(x_sample.reshape(Ms, D_MODEL), oa_s, nb_s, prev_s, mix_weights,
                             Ms, Bs, Ts)

    def unchunk(cv):
        return jnp.moveaxis(cv, 0, 2).reshape(cv.shape[1], CONV_WIDTH - 1, 2 * FFN_DIM)

    hshape = (FOX_HEADS, FOX_HEAD_DIM)
    return (
        y_p[None],
        y_s.reshape(Bs, Ts, D_MODEL),
        jnp.transpose(kT_p, (2, 0, 1))[None, None],
        jnp.transpose(vT_p, (2, 0, 1))[None, None],
        jnp.transpose(logfT_p, (1, 0))[None, None],
        ret_p[None],
        unchunk(conv_p)[None],
        k_s.reshape((1, Bs, Ts) + hshape),
        v_s.reshape((1, Bs, Ts) + hshape),
        logf_s.reshape(1, Bs, Ts, FOX_HEADS),
        ret_s[None],
        unchunk(conv_s)[None],
    )
```

```python
import functools
import math

import numpy as np
import jax
import jax.numpy as jnp
from jax import lax
from jax.experimental import pallas as pl
from jax.experimental.pallas import tpu as pltpu

F32 = jnp.float32
BF16 = jnp.bfloat16

D_MODEL = 1024
FOX_HEADS = 8
FOX_HEAD_DIM = 64
RET_HEADS = 4
RET_KEY_DIM = 128
RET_VAL_DIM = 256
FFN_DIM = 2816
CONV_WIDTH = 3
EPS = 1e-6
ROPE_BASE = 10000.0

FOX_W = FOX_HEADS * FOX_HEAD_DIM
RET_KW = RET_HEADS * RET_KEY_DIM
RET_VW = RET_HEADS * RET_VAL_DIM

LOG2E = 1.4426950408889634
LANES = 128
HEAD_PAD = LANES
V_PAD = 80
BIAS_COL = FOX_HEAD_DIM
NEG = -1e30
STALE_SAFE_LOG2 = 64.0
PRUNE_LOG2 = -160.0
NORM_SLACK = 1.02
ROT_FINE = 128
FFN_CHUNK = 256
N_FFN_CHUNKS = FFN_DIM // FFN_CHUNK
VMEM_LIMIT = 56 * 1024 * 1024


def _rmsnorm(x, g):
    ms = jnp.mean(x * x, axis=-1, keepdims=True)
    return x * lax.rsqrt(ms + EPS) * g


def _split3(x):
    hi = x.astype(BF16)
    r1 = x - hi.astype(F32)
    mid = r1.astype(BF16)
    lo = (r1 - mid.astype(F32)).astype(BF16)
    return hi, mid, lo


def _log_sigmoid(x):
    return jnp.minimum(x, 0.0) - jnp.log1p(jnp.exp(-jnp.abs(x)))


def _dot(a, b):
    return jnp.dot(a, b, preferred_element_type=F32)


def _rotary(x, cos2, sin2):
    return x * cos2 + pltpu.roll(x, RET_KEY_DIM // 2, 1) * sin2


def _const_spec(shape):
    n = len(shape)
    return pl.BlockSpec(shape, lambda *_: (0,) * n)


def _inproj_prompt_kernel(x_ref, g_ref, wq_ref, wk_ref, wvf_ref, bf_ref, wb_ref,
                          cos_ref, sin_ref, tri_ref, eq_ref, ek_ref, oneq_ref, onek_ref, onev_ref,
                          qT_ref, ka_ref, vT_ref, kT32_ref, vT32_ref, logf_ref, qb_ref, kb_ref, vb_ref,
                          stats_ref, sqq_ref, carry_ref):
    tm = x_ref.shape[0]
    VW = FOX_HEADS * V_PAD

    @pl.when(pl.program_id(0) == 0)
    def _():
        carry_ref[...] = jnp.zeros_like(carry_ref)

    h = _rmsnorm(x_ref[...], g_ref[...]).astype(BF16)
    zvf = _dot(h, wvf_ref[...])

    logf = _log_sigmoid(zvf[:, VW:] + bf_ref[...])
    logf_ref[...] = logf.T[:FOX_HEADS, :]
    lane = lax.broadcasted_iota(jnp.int32, logf.shape, 1)
    logf = jnp.where(lane < FOX_HEADS, logf, 0.0)
    r = _dot(tri_ref[...], jnp.concatenate(_split3(logf), axis=1))
    c = r[:, :LANES] + r[:, LANES:2 * LANES] + r[:, 2 * LANES:] + carry_ref[...]
    carry_ref[...] = c[tm - 1:tm, :]
    c2 = c * LOG2E
    hi, mid, lo = (t.astype(F32) for t in _split3(c2))
    c3 = (hi + pltpu.roll(mid, FOX_HEADS, 1) + pltpu.roll(lo, 2 * FOX_HEADS, 1)).astype(BF16)

    q_aug = (_dot(h, wq_ref[...]) * (FOX_HEAD_DIM ** -0.5 * LOG2E)
             + _dot(c3, eq_ref[...]) + oneq_ref[...])
    qT32 = q_aug.T
    k_aug = _dot(h, wk_ref[...]) + _dot(c3, ek_ref[...]) + onek_ref[...]
    kT32 = k_aug.T
    vT32 = (zvf[:, :VW] + onev_ref[...]).T
    qT = qT32.astype(BF16)
    k_aug = k_aug.astype(BF16)
    vT = vT32.astype(BF16)

    lane1 = lax.broadcasted_iota(jnp.int32, (1, LANES), 1)
    nq2 = jnp.zeros((1, LANES), F32)
    nk2 = jnp.zeros((1, LANES), F32)
    for hh in range(FOX_HEADS):
        qT_ref[hh] = qT[hh * HEAD_PAD:(hh + 1) * HEAD_PAD, :]
        ka_ref[hh] = k_aug[:, hh * HEAD_PAD:(hh + 1) * HEAD_PAD]
        vT_ref[hh] = vT[hh * V_PAD:(hh + 1) * V_PAD, :]
        qh = qT32[hh * HEAD_PAD:hh * HEAD_PAD + FOX_HEAD_DIM, :]
        kh = kT32[hh * HEAD_PAD:hh * HEAD_PAD + FOX_HEAD_DIM, :]
        kT32_ref[hh] = kh
        vT32_ref[hh] = vT32[hh * V_PAD:hh * V_PAD + FOX_HEAD_DIM, :]
        sqq_ref[hh:hh + 1, :] = jnp.sum(qh * kh, axis=0, keepdims=True)
        q2 = jnp.max(jnp.sum(qh * qh, axis=0, keepdims=True), axis=1, keepdims=True)
        k2 = jnp.max(jnp.sum(kh * kh, axis=0, keepdims=True), axis=1, keepdims=True)
        nq2 = jnp.where(lane1 == hh, q2, nq2)
        nk2 = jnp.where(lane1 == hh, k2, nk2)
    stats_ref[0] = jnp.concatenate(
        [nq2, nk2, c2[0:1, :], c2[tm - 1:tm, :], jnp.zeros((4, LANES), F32)], axis=0)

    zb = _dot(h, wb_ref[...])
    cos2 = cos_ref[...]
    sin2 = sin_ref[...]
    for hh in range(RET_HEADS):
        sl = slice(hh * RET_KEY_DIM, (hh + 1) * RET_KEY_DIM)
        qb_ref[:, sl] = _rotary(zb[:, sl], cos2, sin2).astype(BF16)
        xk = zb[:, RET_KW + hh * RET_KEY_DIM:RET_KW + (hh + 1) * RET_KEY_DIM]
        kb_ref[:, sl] = (_rotary(xk, cos2, sin2) * (RET_KEY_DIM ** -0.5)).astype(BF16)
    vb_ref[...] = zb[:, 2 * RET_KW:].astype(BF16)


def _inproj_prompt(x, g, wq, wk, wvf, bf, wb, cos2, sin2, consts, tm):
    S = x.shape[0]
    tri, eq, ek, oneq, onek, onev = consts
    row = lambda w: pl.BlockSpec((tm, w), lambda i: (i, 0))
    headT = pl.BlockSpec((FOX_HEADS, FOX_HEAD_DIM, tm), lambda i: (0, 0, i))
    in_specs = [row(D_MODEL), _const_spec(g.shape), _const_spec(wq.shape), _const_spec(wk.shape),
                _const_spec(wvf.shape),
                _const_spec(bf.shape), _const_spec(wb.shape), row(LANES), row(LANES),
                _const_spec(tri.shape), _const_spec(eq.shape), _const_spec(ek.shape),
                _const_spec(oneq.shape), _const_spec(onek.shape), _const_spec(onev.shape)]
    out_shape = (
        jax.ShapeDtypeStruct((FOX_HEADS, HEAD_PAD, S), BF16),
        jax.ShapeDtypeStruct((FOX_HEADS, S, HEAD_PAD), BF16),
        jax.ShapeDtypeStruct((FOX_HEADS, V_PAD, S), BF16),
        jax.ShapeDtypeStruct((FOX_HEADS, FOX_HEAD_DIM, S), F32),
        jax.ShapeDtypeStruct((FOX_HEADS, FOX_HEAD_DIM, S), F32),
        jax.ShapeDtypeStruct((FOX_HEADS, S), F32),
        jax.ShapeDtypeStruct((S, RET_KW), BF16),
        jax.ShapeDtypeStruct((S, RET_KW), BF16),
        jax.ShapeDtypeStruct((S, RET_VW), BF16),
        jax.ShapeDtypeStruct((S // tm, 8, LANES), F32),
        jax.ShapeDtypeStruct((FOX_HEADS, S), F32),
    )
    out_specs = (
        pl.BlockSpec((FOX_HEADS, HEAD_PAD, tm), lambda i: (0, 0, i)),
        pl.BlockSpec((FOX_HEADS, tm, HEAD_PAD), lambda i: (0, i, 0)),
        pl.BlockSpec((FOX_HEADS, V_PAD, tm), lambda i: (0, 0, i)),
        headT, headT, pl.BlockSpec((FOX_HEADS, tm), lambda i: (0, i)),
        row(RET_KW), row(RET_KW), row(RET_VW),
        pl.BlockSpec((1, 8, LANES), lambda i: (i, 0, 0)),
        pl.BlockSpec((FOX_HEADS, tm), lambda i: (0, i)),
    )
    return pl.pallas_call(
        _inproj_prompt_kernel,
        grid=(S // tm,),
        in_specs=in_specs,
        out_specs=out_specs,
        out_shape=out_shape,
        scratch_shapes=[pltpu.VMEM((1, LANES), F32)],
        compiler_params=pltpu.CompilerParams(dimension_semantics=("arbitrary",),
                                             vmem_limit_bytes=VMEM_LIMIT),
        name="inproj_prompt",
    )(x, g, wq, wk, wvf, bf, wb, cos2, sin2, tri, eq, ek, oneq, onek, onev)


def _inproj_sample_kernel(x_ref, g_ref, wqkv_ref, wf_ref, bf_ref, wb_ref, cos_ref, sin_ref,
                          q_ref, k32_ref, v32_ref, logf_ref, qb_ref, kb_ref, vb_ref):
    h = _rmsnorm(x_ref[...], g_ref[...]).astype(BF16)
    logf = _log_sigmoid(_dot(h, wf_ref[...]) + bf_ref[...])
    logf_ref[...] = logf[:, :FOX_HEADS]
    z = _dot(h, wqkv_ref[...])
    q_ref[...] = (z[:, :FOX_W] * (FOX_HEAD_DIM ** -0.5 * LOG2E)).astype(BF16)
    k32_ref[...] = z[:, FOX_W:2 * FOX_W]
    v32_ref[...] = z[:, 2 * FOX_W:]
    zb = _dot(h, wb_ref[...])
    cos2 = cos_ref[...]
    sin2 = sin_ref[...]
    for hh in range(RET_HEADS):
        sl = slice(hh * RET_KEY_DIM, (hh + 1) * RET_KEY_DIM)
        qb_ref[:, sl] = _rotary(zb[:, sl], cos2, sin2).astype(BF16)
        xk = zb[:, RET_KW + hh * RET_KEY_DIM:RET_KW + (hh + 1) * RET_KEY_DIM]
        kb_ref[:, sl] = (_rotary(xk, cos2, sin2) * (RET_KEY_DIM ** -0.5)).astype(BF16)
    vb_ref[...] = zb[:, 2 * RET_KW:].astype(BF16)


def _inproj_sample(x, g, wqkv, wf, bf, wb, cos2, sin2):
    M = x.shape[0]
    args = (x, g, wqkv, wf, bf, wb, cos2, sin2)
    out_shape = (
        jax.ShapeDtypeStruct((M, FOX_W), BF16),
        jax.ShapeDtypeStruct((M, FOX_W), F32),
        jax.ShapeDtypeStruct((M, FOX_W), F32),
        jax.ShapeDtypeStruct((M, FOX_HEADS), F32),
        jax.ShapeDtypeStruct((M, RET_KW), BF16),
        jax.ShapeDtypeStruct((M, RET_KW), BF16),
        jax.ShapeDtypeStruct((M, RET_VW), BF16),
    )
    return pl.pallas_call(
        _inproj_sample_kernel,
        grid=(1,),
        in_specs=[_const_spec(a.shape) for a in args],
        out_specs=tuple(_const_spec(o.shape) for o in out_shape),
        out_shape=out_shape,
        compiler_params=pltpu.CompilerParams(dimension_semantics=("arbitrary",),
                                             vmem_limit_bytes=VMEM_LIMIT),
        name="inproj_sample",
    )(*args)


def _fox_prompt_kernel(it_ref, jt_ref, jfetch_ref, mode_ref, qT_ref, ka_ref, vT_ref,
                       sqq_ref, o_ref, m_ref, acc_ref, *, n_strips):
    del jfetch_ref
    t = pl.program_id(0)
    i = it_ref[t]
    j = jt_ref[t]
    T = qT_ref.shape[2]
    SUB = T // n_strips
    EXACT = n_strips + 1

    def heads(fn):
        def body(hh, carry):
            fn(hh, mode_ref[t * FOX_HEADS + hh])
            return carry
        lax.fori_loop(0, FOX_HEADS, body, 0)

    def scores(hh):
        return _dot(ka_ref[hh], qT_ref[hh])

    def exact_head(hh):
        s = scores(hh)
        kk = lax.broadcasted_iota(jnp.int32, s.shape, 0) + j * T
        qq = lax.broadcasted_iota(jnp.int32, s.shape, 1) + i * T
        s = jnp.where(kk > qq, NEG, s)
        m_old = m_ref[hh]
        m_new = jnp.maximum(m_old, jnp.max(s, axis=0, keepdims=True))
        p = jnp.exp2(s - m_new).astype(BF16)
        alpha = jnp.exp2(m_old - m_new)
        acc_ref[hh] = alpha * acc_ref[hh] + _dot(vT_ref[hh], p)
        m_ref[hh] = m_new


    def diag_head(hh, mode):
        m_ref[hh] = sqq_ref[pl.ds(hh, 1), :]

        @pl.when(mode != EXACT)
        def _():
            s = scores(hh)
            kk = lax.broadcasted_iota(jnp.int32, s.shape, 0)
            qq = lax.broadcasted_iota(jnp.int32, s.shape, 1)
            p = jnp.exp2(jnp.where(kk > qq, NEG, s) - m_ref[hh]).astype(BF16)
            acc_ref[hh] = _dot(vT_ref[hh], p)

        @pl.when(mode == EXACT)
        def _():
            acc_ref[hh] = jnp.zeros((V_PAD, T), F32)
            exact_head(hh)

    def off_head(hh, mode):
        for nn in range(1, n_strips + 1):
            k0 = (n_strips - nn) * SUB

            @pl.when(mode == nn)
            def _():
                p = jnp.exp2(_dot(ka_ref[hh, k0:, :], qT_ref[hh]) - m_ref[hh]).astype(BF16)
                acc_ref[hh] += _dot(vT_ref[hh, :, k0:], p)

        @pl.when(mode == EXACT)
        def _():
            exact_head(hh)

    @pl.when(j == i)
    def _():
        heads(diag_head)

    @pl.when(j < i)
    def _():
        heads(off_head)

    @pl.when(j == 0)
    def _():
        for pr in range(FOX_HEADS // 2):
            halves = []
            for hh in (2 * pr, 2 * pr + 1):
                a = acc_ref[hh]
                halves.append(a[:FOX_HEAD_DIM] / a[FOX_HEAD_DIM:FOX_HEAD_DIM + 1])
            o_ref[pr] = jnp.concatenate(halves, axis=0).T.astype(BF16)


def _prune_tables(stats, it, jt, nb):
    per = stats.shape[0] // nb
    st = stats.reshape(nb, per, 8, LANES)[:, :, :, :FOX_HEADS]
    nq = jnp.sqrt(jnp.max(st[:, :, 0, :], axis=1)) * NORM_SLACK
    nk_strip = jnp.sqrt(st[:, :, 1, :]) * NORM_SLACK
    nk = jnp.max(nk_strip, axis=1)
    c_first = st[:, 0, 2, :]
    c_last = st[:, :, 3, :]
    bound = (nq[it][:, None, :] * (nk_strip[jt] + nk[it][:, None, :])
             - (c_last[jt] - c_first[it][:, None, :]))
    live = jnp.logical_or(jnp.asarray(jt == it)[:, None, None],
                          jnp.logical_not(bound < PRUNE_LOG2))
    strip_no = jnp.arange(per, dtype=jnp.int32)[None, :, None]
    n_keep = per - jnp.min(jnp.where(live, strip_no, per), axis=1)
    safe = nq[it] * (nk[jt] + nk[it]) < STALE_SAFE_LOG2
    mode = jnp.where(n_keep == 0, 0, jnp.where(safe, n_keep, per + 1))
    steps = jnp.arange(len(it), dtype=jnp.int32)
    last_live = lax.cummax(jnp.where(jnp.any(n_keep > 0, axis=1), steps, 0))
    return jnp.asarray(jt)[last_live], mode.astype(jnp.int32).reshape(-1), per


def _fox_prompt(qT, ka, vT, stats, sqq, T):
    S = ka.shape[1]
    nb = S // T
    it = np.array([i for i in range(nb) for _ in range(i + 1)], np.int32)
    jt = np.array([j for i in range(nb) for j in range(i, -1, -1)], np.int32)
    jfetch, mode, n_strips = _prune_tables(stats, it, jt, nb)
    grid_spec = pltpu.PrefetchScalarGridSpec(
        num_scalar_prefetch=4,
        grid=(len(it),),
        in_specs=[
            pl.BlockSpec((FOX_HEADS, HEAD_PAD, T), lambda t, it, jt, jf, md: (0, 0, it[t])),
            pl.BlockSpec((FOX_HEADS, T, HEAD_PAD), lambda t, it, jt, jf, md: (0, jf[t], 0)),
            pl.BlockSpec((FOX_HEADS, V_PAD, T), lambda t, it, jt, jf, md: (0, 0, jf[t])),
            pl.BlockSpec((FOX_HEADS, T), lambda t, it, jt, jf, md: (0, it[t])),
        ],
        out_specs=pl.BlockSpec((FOX_HEADS // 2, T, LANES),
                               lambda t, it, jt, jf, md: (0, it[t], 0)),
        scratch_shapes=[pltpu.VMEM((FOX_HEADS, 1, T), F32),
                        pltpu.VMEM((FOX_HEADS, V_PAD, T), F32)],
    )
    return pl.pallas_call(
        functools.partial(_fox_prompt_kernel, n_strips=n_strips),
        grid_spec=grid_spec,
        out_shape=jax.ShapeDtypeStruct((FOX_HEADS // 2, S, LANES), BF16),
        compiler_params=pltpu.CompilerParams(dimension_semantics=("arbitrary",),
                                             vmem_limit_bytes=VMEM_LIMIT),
        name="fox_prompt",
    )(jnp.asarray(it), jnp.asarray(jt), jfetch, mode, qT, ka, vT, sqq)


def _fox_sample_kernel(q_ref, kn_ref, vn_ref, ckT_ref, cvT_ref, lfT_ref, up_ref, ex_ref, o_ref):
    P = ckT_ref.shape[3]
    Tn = q_ref.shape[0]
    KP = lfT_ref.shape[2]
    HQ = FOX_HEADS * Tn
    nchunk = KP // LANES
    nt = (((1,), (1,)), ((), ()))

    def stack3(x):
        parts3 = [t.astype(F32) for t in _split3(x)] + [jnp.zeros_like(x)]
        return jnp.concatenate(parts3, axis=0).astype(BF16)

    x3 = stack3(lfT_ref[0])
    up = up_ref[...]
    run = jnp.zeros((4 * FOX_HEADS, 1), F32)
    parts = []
    for cidx in range(nchunk):
        y = _dot(x3[:, cidx * LANES:(cidx + 1) * LANES], up) + run
        parts.append(y)
        run = y[:, LANES - 1:LANES]
    y = jnp.concatenate(parts, axis=1)
    cT = (y[:FOX_HEADS] + y[FOX_HEADS:2 * FOX_HEADS] + y[2 * FOX_HEADS:3 * FOX_HEADS]) * LOG2E
    ckx = _dot(ex_ref[...], stack3(cT))

    tail = ckx[:, P:P + LANES]
    rowq = lax.broadcasted_iota(jnp.int32, tail.shape, 0) % Tn
    lanek = lax.broadcasted_iota(jnp.int32, tail.shape, 1)
    cq = jnp.sum(jnp.where(lanek == rowq, tail, 0.0), axis=1, keepdims=True)

    q = q_ref[...]
    qt = jnp.concatenate([q] * FOX_HEADS, axis=0)
    rh = lax.broadcasted_iota(jnp.int32, qt.shape, 0) // Tn
    lh = lax.broadcasted_iota(jnp.int32, qt.shape, 1) // FOX_HEAD_DIM
    qbd = jnp.where(rh == lh, qt, jnp.zeros_like(qt))

    kT = ckT_ref[0].reshape(FOX_W, P).astype(BF16)
    vT = cvT_ref[0].reshape(FOX_W, P).astype(BF16)
    s_c = _dot(qbd, kT) + cq - ckx[:, :P]
    s_n = lax.dot_general(qbd, kn_ref[...].astype(BF16), nt, preferred_element_type=F32)
    s_n = s_n + cq - ckx[:, P:P + Tn]
    key = lax.broadcasted_iota(jnp.int32, s_n.shape, 1)
    qrow = lax.broadcasted_iota(jnp.int32, s_n.shape, 0) % Tn
    s_n = jnp.where(key > qrow, NEG, s_n)
    m = jnp.maximum(jnp.max(s_c, axis=1, keepdims=True), jnp.max(s_n, axis=1, keepdims=True))
    p_c = jnp.exp2(s_c - m)
    p_n = jnp.exp2(s_n - m)
    l = jnp.sum(p_c, axis=1, keepdims=True) + jnp.sum(p_n, axis=1, keepdims=True)
    z = lax.dot_general(p_c.astype(BF16), vT, nt, preferred_element_type=F32)
    z = (z + _dot(p_n.astype(BF16), vn_ref[...].astype(BF16))) / l
    zh = lax.broadcasted_iota(jnp.int32, (Tn, FOX_W), 1) // FOX_HEAD_DIM
    o = jnp.zeros((Tn, FOX_W), F32)
    for hh in range(FOX_HEADS):
        o = o + jnp.where(zh == hh, z[hh * Tn:(hh + 1) * Tn, :], 0.0)
    o_ref[...] = o.astype(BF16)


def _fox_sample(q, kn, vn, cache_kT, cache_vT, lfT, B, Tn):
    P = cache_kT.shape[3]
    KP = lfT.shape[2]
    HQ = FOX_HEADS * Tn
    up = jnp.asarray(np.triu(np.ones((LANES, LANES), np.float32)), BF16)
    ex = np.zeros((HQ, 4 * FOX_HEADS), np.float32)
    for part in range(3):
        for hh in range(FOX_HEADS):
            ex[hh * Tn:(hh + 1) * Tn, part * FOX_HEADS + hh] = 1.0
    ex = jnp.asarray(ex, BF16)
    rowb = lambda w: pl.BlockSpec((Tn, w), lambda b: (b, 0))
    return pl.pallas_call(
        _fox_sample_kernel,
        grid=(B,),
        in_specs=[rowb(FOX_W), rowb(FOX_W), rowb(FOX_W),
                  pl.BlockSpec((1, FOX_HEADS, FOX_HEAD_DIM, P), lambda b: (b, 0, 0, 0)),
                  pl.BlockSpec((1, FOX_HEADS, FOX_HEAD_DIM, P), lambda b: (b, 0, 0, 0)),
                  pl.BlockSpec((1, FOX_HEADS, KP), lambda b: (b, 0, 0)),
                  _const_spec(up.shape), _const_spec(ex.shape)],
        out_specs=rowb(FOX_W),
        out_shape=jax.ShapeDtypeStruct((B * Tn, FOX_W), BF16),
        compiler_params=pltpu.CompilerParams(dimension_semantics=("arbitrary",),
                                             vmem_limit_bytes=VMEM_LIMIT),
        name="fox_sample",
    )(q, kn, vn, cache_kT, cache_vT, lfT, up, ex)


def _retention_kernel(q_ref, k_ref, v_ref, s0_ref, dmat_ref, xi_ref, zeta_ref, gam_ref,
                      n_ref, sout_ref, st_ref):
    c = pl.program_id(1)

    @pl.when(c == 0)
    def _():
        st_ref[...] = s0_ref[0]

    for hh in range(RET_HEADS):
        q = q_ref[:, hh * RET_KEY_DIM:(hh + 1) * RET_KEY_DIM]
        k = k_ref[:, hh * RET_KEY_DIM:(hh + 1) * RET_KEY_DIM]
        v = v_ref[:, hh * RET_VAL_DIM:(hh + 1) * RET_VAL_DIM]
        st = st_ref[hh]
        sc = lax.dot_general(q, k, (((1,), (1,)), ((), ())), preferred_element_type=F32)
        sc = sc * dmat_ref[hh]
        o = _dot(sc.astype(BF16), v) + _dot(q, st.astype(BF16)) * xi_ref[hh]
        kz = (k.astype(F32) * zeta_ref[hh]).astype(BF16)
        upd = lax.dot_general(kz, v, (((0,), (0,)), ((), ())), preferred_element_type=F32)
        st_ref[hh] = gam_ref[hh] * st + upd
        mu = jnp.mean(o, axis=-1, keepdims=True)
        d = o - mu
        var = jnp.mean(d * d, axis=-1, keepdims=True)
        n_ref[:, hh * RET_VAL_DIM:(hh + 1) * RET_VAL_DIM] = (d * lax.rsqrt(var + EPS)).astype(BF16)

    @pl.when(c == pl.num_programs(1) - 1)
    def _():
        sout_ref[0] = st_ref[...]


def _ret_log_gamma():
    return jnp.log(1.0 - jnp.exp2(-5.0 - jnp.arange(RET_HEADS, dtype=F32)))


def _retention(q, k, v, state0, B, L, C):
    nc = L // C
    lg = _ret_log_gamma()
    idx = jnp.arange(C, dtype=F32)
    diff = idx[:, None] - idx[None, :]
    dmat = jnp.where(diff[None] >= 0, jnp.exp(jnp.maximum(diff, 0.0)[None] * lg[:, None, None]), 0.0)
    xi = jnp.exp((idx[None, :] + 1.0) * lg[:, None])
    zeta = jnp.exp((C - 1.0 - idx[None, :]) * lg[:, None])
    xi = jnp.broadcast_to(xi[:, :, None], (RET_HEADS, C, RET_VAL_DIM))
    zeta = jnp.broadcast_to(zeta[:, :, None], (RET_HEADS, C, RET_KEY_DIM))
    gam = jnp.broadcast_to(jnp.exp(C * lg)[:, None, None], (RET_HEADS, 1, RET_VAL_DIM))
    rowc = lambda w: pl.BlockSpec((C, w), lambda b, c: (b * nc + c, 0))
    st_spec = pl.BlockSpec((1, RET_HEADS, RET_KEY_DIM, RET_VAL_DIM), lambda b, c: (b, 0, 0, 0))
    return pl.pallas_call(
        _retention_kernel,
        grid=(B, nc),
        in_specs=[rowc(RET_KW), rowc(RET_KW), rowc(RET_VW), st_spec,
                  _const_spec(dmat.shape), _const_spec(xi.shape), _const_spec(zeta.shape),
                  _const_spec(gam.shape)],
        out_specs=(rowc(RET_VW), st_spec),
        out_shape=(jax.ShapeDtypeStruct((B * L, RET_VW), BF16),
                   jax.ShapeDtypeStruct((B, RET_HEADS, RET_KEY_DIM, RET_VAL_DIM), F32)),
        scratch_shapes=[pltpu.VMEM((RET_HEADS, RET_KEY_DIM, RET_VAL_DIM), F32)],
        compiler_params=pltpu.CompilerParams(dimension_semantics=("arbitrary", "arbitrary"),
                                             vmem_limit_bytes=VMEM_LIMIT),
        name="retention",
    )(q, k, v, state0, dmat, xi, zeta, gam)


def _mixer_ffn_kernel(x_ref, oa_ref, nb_ref, prev_ref, gmix_ref, wg_ref, gng_ref, wpa_ref, wpb_ref,
                      wo_ref, gffn_ref, wup_ref, cw_ref, cb_ref, wdn_ref, gfin_ref,
                      y_ref, conv_ref, carry_ref, ua_ref, ub_ref, acc_ref, h2_ref,
                      *, nseg, seglen):
    i = pl.program_id(0)
    NC = N_FFN_CHUNKS
    PADR = 8
    H0 = PADR - (CONV_WIDTH - 1)

    @pl.when(i == 0)
    def _():
        carry_ref[...] = prev_ref[...]

    x = x_ref[...]
    h = _rmsnorm(x, gmix_ref[...]).astype(BF16)
    zg = _dot(h, wg_ref[...])
    gb = zg[:, :RET_VW]
    gma = zg[:, RET_VW:RET_VW + D_MODEL]
    gmb = zg[:, RET_VW + D_MODEL:]
    oa = jnp.concatenate([oa_ref[p] for p in range(FOX_HEADS // 2)], axis=1)
    ya = _dot(oa, wpa_ref[...])
    nn = nb_ref[...].astype(F32) * gng_ref[...] * (gb * jax.nn.sigmoid(gb))
    yb = _dot(nn.astype(BF16), wpb_ref[...])
    y = jax.nn.sigmoid(gma) * ya + jax.nn.sigmoid(gmb) * yb
    x1 = x + _dot(y.astype(BF16), wo_ref[...])
    h2_ref[...] = _rmsnorm(x1, gffn_ref[...]).astype(BF16)
    acc_ref[...] = x1

    h2 = h2_ref[...]

    def up_half(u_ref, cidx, slot):
        u = _dot(h2, wup_ref[cidx])
        for s in range(nseg):
            u_ref[slot, s, PADR:PADR + seglen, :] = u[s * seglen:(s + 1) * seglen, :]
            u_ref[slot, s, H0:PADR, :] = carry_ref[cidx, s, H0:PADR, :]
            carry_ref[cidx, s, H0:PADR, :] = u[(s + 1) * seglen - (CONV_WIDTH - 1):(s + 1) * seglen, :]

    def conv_half(u_ref, cidx, slot):
        w = cw_ref[cidx]
        b = cb_ref[cidx]
        outs = []
        for s in range(nseg):
            acc = w[0:1] * u_ref[slot, s, H0:H0 + seglen, :]
            for jj in range(1, CONV_WIDTH):
                acc = acc + w[jj:jj + 1] * u_ref[slot, s, H0 + jj:H0 + jj + seglen, :]
            outs.append(b + acc)
        return outs[0] if nseg == 1 else jnp.concatenate(outs, axis=0)

    def stage_up(c):
        up_half(ua_ref, c, c % 2)
        up_half(ub_ref, NC + c, c % 2)

    stage_up(0)
    for c in range(NC):
        if c + 1 < NC:
            stage_up(c + 1)
        a = conv_half(ua_ref, c, c % 2)
        b = conv_half(ub_ref, NC + c, c % 2)
        g = (jax.nn.gelu(a) * b).astype(BF16)
        acc_ref[...] += _dot(g, wdn_ref[c])
    y_ref[...] = _rmsnorm(acc_ref[...], gfin_ref[...])

    @pl.when(i == pl.num_programs(0) - 1)
    def _():
        conv_ref[...] = carry_ref[:, :, H0:PADR, :]


def _mixer_ffn(x, oa, nb, prev, weights, tm, nseg, seglen):
    M = x.shape[0]
    gmix, wg, gng, wpa, wpb, wo, gffn, wup, cw, cb, wdn, gfin = weights
    FC = FFN_CHUNK
    NC = N_FFN_CHUNKS
    row = lambda w: pl.BlockSpec((tm, w), lambda i: (i, 0))
    wspec = lambda a: pl.BlockSpec(a.shape, lambda i, n=a.ndim: (0,) * n,
                                   pipeline_mode=pl.Buffered(1))
    in_specs = [row(D_MODEL),
                pl.BlockSpec((FOX_HEADS // 2, tm, LANES), lambda i: (0, i, 0)),
                row(RET_VW), wspec(prev)] + [wspec(w) for w in weights]
    out_shape = (jax.ShapeDtypeStruct((M, D_MODEL), F32),
                 jax.ShapeDtypeStruct((2 * NC, nseg, CONV_WIDTH - 1, FC), F32))
    out_specs = (row(D_MODEL), _const_spec(out_shape[1].shape))
    return pl.pallas_call(
        functools.partial(_mixer_ffn_kernel, nseg=nseg, seglen=seglen),
        grid=(M // tm,),
        in_specs=in_specs,
        out_specs=out_specs,
        out_shape=out_shape,
        scratch_shapes=[pltpu.VMEM((2 * NC, nseg, 8, FC), F32),
                        pltpu.VMEM((2, nseg, 8 + seglen, FC), F32),
                        pltpu.VMEM((2, nseg, 8 + seglen, FC), F32),
                        pltpu.VMEM((tm, D_MODEL), F32),
                        pltpu.VMEM((tm, D_MODEL), BF16)],
        compiler_params=pltpu.CompilerParams(dimension_semantics=("arbitrary",),
                                             vmem_limit_bytes=VMEM_LIMIT),
        name="mixer_ffn",
    )(x, oa, nb, prev, *weights)


def _rotary_tables(start, n):
    half = RET_KEY_DIM // 2
    inv = 1.0 / (ROPE_BASE ** jnp.linspace(0.0, 1.0, half, dtype=F32))
    fine = min(n, ROT_FINE)
    assert n % fine == 0
    a_hi = (start + fine * jnp.arange(n // fine)).astype(F32)[:, None] * inv[None, :]
    a_lo = jnp.arange(fine).astype(F32)[:, None] * inv[None, :]
    ch, sh = jnp.cos(a_hi)[:, None, :], jnp.sin(a_hi)[:, None, :]
    cl, sl = jnp.cos(a_lo)[None, :, :], jnp.sin(a_lo)[None, :, :]
    cos = (ch * cl - sh * sl).reshape(n, half)
    sin = (sh * cl + ch * sl).reshape(n, half)
    return jnp.concatenate([cos, cos], axis=1), jnp.concatenate([-sin, sin], axis=1)


def _pad_heads(wt, pad):
    d = wt.shape[1]
    wt = wt.reshape(FOX_HEADS, FOX_HEAD_DIM, d)
    wt = jnp.pad(wt, ((0, 0), (0, pad - FOX_HEAD_DIM), (0, 0)))
    return wt.reshape(FOX_HEADS * pad, d)


def _prompt_consts(tm):
    tri = np.tril(np.ones((tm, tm), np.float32))
    eq = np.zeros((LANES, FOX_HEADS * HEAD_PAD), np.float32)
    ek = np.zeros((LANES, FOX_HEADS * HEAD_PAD), np.float32)
    oneq = np.zeros((1, FOX_HEADS * HEAD_PAD), np.float32)
    onek = np.zeros((1, FOX_HEADS * HEAD_PAD), np.float32)
    onev = np.zeros((1, FOX_HEADS * V_PAD), np.float32)
    for hh in range(FOX_HEADS):
        base = hh * HEAD_PAD + BIAS_COL
        for part in range(3):
            eq[part * FOX_HEADS + hh, base + part] = 1.0
            ek[part * FOX_HEADS + hh, base + 3 + part] = -1.0
            onek[0, base + part] = 1.0
            oneq[0, base + 3 + part] = 1.0
        onev[0, hh * V_PAD + FOX_HEAD_DIM] = 1.0
    return (jnp.asarray(tri, BF16), jnp.asarray(eq, BF16), jnp.asarray(ek, BF16),
            jnp.asarray(oneq), jnp.asarray(onek), jnp.asarray(onev))


def _chunk_cols(a):
    lead = a.shape[:-1]
    a = a.reshape(lead + (2 * N_FFN_CHUNKS, FFN_CHUNK))
    return jnp.moveaxis(a, -2, 0)


def _tile(n, pref):
    t = min(n, pref)
    while n % t:
        t //= 2
    return t


def kernel(x_prompt, x_sample, cache_fox_k, cache_fox_v, cache_fox_logf, state_ret, state_ffn_conv,
           norm_mix_g, w_in, b_fox_f, gn_ret_g, w_pa, w_pb, w_o, norm_ffn_g, w_up, conv_w, conv_b,
           w_down, norm_final_g):
    depth = w_in.shape[0]
    Bp, S, _ = x_prompt.shape
    Bs, Ts, _ = x_sample.shape
    P = cache_fox_k.shape[2]
    assert depth == 1 and Bp == 1, "kernel handles the single-layer, single-prompt configuration"
    l = 0

    wt = jnp.swapaxes(w_in[l], 0, 1).astype(BF16)
    o0 = 3 * FOX_W
    o1 = o0 + FOX_HEADS
    o2 = o1 + 2 * RET_KW + RET_VW
    wq_aug = _pad_heads(wt[:FOX_W], HEAD_PAD).T
    wk_aug = _pad_heads(wt[FOX_W:2 * FOX_W], HEAD_PAD).T
    wf_t = jnp.pad(wt[o0:o1], ((0, LANES - FOX_HEADS), (0, 0)))
    wvf = jnp.concatenate([_pad_heads(wt[2 * FOX_W:o0], V_PAD), wf_t], axis=0).T
    wqkv = wt[:o0].T
    wf = wf_t.T
    bf = jnp.pad(b_fox_f[l].astype(F32), (0, LANES - FOX_HEADS))[None, :]
    wb = wt[o1:o2].T
    wg = wt[o2:].T
    gmix = norm_mix_g[l].astype(F32)[None, :]
    mix_weights = (
        gmix, wg, gn_ret_g[l].astype(F32)[None, :], w_pa[l].astype(BF16), w_pb[l].astype(BF16),
        w_o[l].astype(BF16), norm_ffn_g[l].astype(F32)[None, :],
        _chunk_cols(w_up[l]).astype(BF16),
        jnp.pad(_chunk_cols(conv_w[l].astype(F32)), ((0, 0), (0, 8 - CONV_WIDTH), (0, 0))),
        _chunk_cols(conv_b[l].astype(F32)[None, :]),
        w_down[l].reshape(N_FFN_CHUNKS, FFN_CHUNK, D_MODEL).astype(BF16),
        norm_final_g.astype(F32)[None, :],
    )

    tm_a = _tile(S, 256)
    cos_p, sin_p = _rotary_tables(0, S)
    (qT, ka, vT, kT_p, vT_p, logfT_p, qb, kb, vb, stats, sqq) = _inproj_prompt(
        x_prompt[0], gmix, wq_aug, wk_aug, wvf, bf, wb, cos_p, sin_p,
        _prompt_consts(tm_a), tm_a)
    oa_p = _fox_prompt(qT, ka, vT, stats, sqq, _tile(S, 1024))
    zero_state = jnp.zeros((1, RET_HEADS, RET_KEY_DIM, RET_VAL_DIM), F32)
    nb_p, ret_p = _retention(qb, kb, vb, zero_state, 1, S, _tile(S, 256))
    tm_d = _tile(S, 256)
    zero_prev = jnp.zeros((2 * N_FFN_CHUNKS, 1, 8, FFN_CHUNK), F32)
    y_p, conv_p = _mixer_ffn(x_prompt[0], oa_p, nb_p, zero_prev, mix_weights, tm_d, 1, tm_d)

    Ms = Bs * Ts
    cos_s, sin_s = _rotary_tables(P, Ts)
    cos_s = jnp.tile(cos_s, (Bs, 1))
    sin_s = jnp.tile(sin_s, (Bs, 1))
    (q_s, k_s, v_s, logf_s, qb_s, kb_s, vb_s) = _inproj_sample(
        x_sample.reshape(Ms, D_MODEL), gmix, wqkv, wf, bf, wb, cos_s, sin_s)
    KP = ((P + Ts + LANES - 1) // LANES) * LANES
    lf_all = jnp.concatenate([cache_fox_logf[l].astype(F32), logf_s.reshape(Bs, Ts, FOX_HEADS)], axis=1)
    lfT = jnp.pad(jnp.swapaxes(lf_all, 1, 2), ((0, 0), (0, 0), (0, KP - P - Ts)))
    oa_s = _fox_sample(q_s, k_s, v_s, jnp.transpose(cache_fox_k[l], (0, 2, 3, 1)),
                       jnp.transpose(cache_fox_v[l], (0, 2, 3, 1)), lfT, Bs, Ts)
    oa_s = jnp.moveaxis(oa_s.reshape(Ms, FOX_HEADS // 2, LANES), 1, 0)
    nb_s, ret_s = _retention(qb_s, kb_s, vb_s, state_ret[l].astype(F32), Bs, Ts, Ts)
    prev_s = _chunk_cols(state_ffn_conv[l].astype(F32))
    prev_s = jnp.pad(prev_s, ((0, 0), (0, 0), (8 - (CONV_WIDTH - 1), 0), (0, 0)))
    y_s, conv_s = _mixer_ffn(x_sample.reshape(Ms, D_MODEL), oa_s, nb_s, prev_s, mix_weights,
                             Ms, Bs, Ts)

    def unchunk(cv):
        return jnp.moveaxis(cv, 0, 2).reshape(cv.shape[1], CONV_WIDTH - 1, 2 * FFN_DIM)

    hshape = (FOX_HEADS, FOX_HEAD_DIM)
    return (
        y_p[None],
        y_s.reshape(Bs, Ts, D_MODEL),
        jnp.transpose(kT_p, (2, 0, 1))[None, None],
        jnp.transpose(vT_p, (2, 0, 1))[None, None],
        jnp.transpose(logfT_p, (1, 0))[None, None],
        ret_p[None],
        unchunk(conv_p)[None],
        k_s.reshape((1, Bs, Ts) + hshape),
        v_s.reshape((1, Bs, Ts) + hshape),
        logf_s.reshape(1, Bs, Ts, FOX_HEADS),
        ret_s[None],
        unchunk(conv_s)[None],
    )
```

```python
import functools
import math

import numpy as np
import jax
import jax.numpy as jnp
from jax import lax
from jax.experimental import pallas as pl
from jax.experimental.pallas import tpu as pltpu

F32 = jnp.float32
BF16 = jnp.bfloat16

D_MODEL = 1024
FOX_HEADS = 8
FOX_HEAD_DIM = 64
RET_HEADS = 4
RET_KEY_DIM = 128
RET_VAL_DIM = 256
FFN_DIM = 2816
CONV_WIDTH = 3
EPS = 1e-6
ROPE_BASE = 10000.0

FOX_W = FOX_HEADS * FOX_HEAD_DIM
RET_KW = RET_HEADS * RET_KEY_DIM
RET_VW = RET_HEADS * RET_VAL_DIM

LOG2E = 1.4426950408889634
LANES = 128
HEAD_PAD = LANES
V_PAD = 80
BIAS_COL = FOX_HEAD_DIM
NEG = -1e30
STALE_SAFE_LOG2 = 64.0
PRUNE_LOG2 = -160.0
NORM_SLACK = 1.02
ROT_FINE = 128
FFN_CHUNK = 256
N_FFN_CHUNKS = FFN_DIM // FFN_CHUNK
DOWN_GROUP = 4
VMEM_LIMIT = 56 * 1024 * 1024


def _rmsnorm(x, g):
    ms = jnp.mean(x * x, axis=-1, keepdims=True)
    return x * lax.rsqrt(ms + EPS) * g


def _split3(x):
    hi = x.astype(BF16)
    r1 = x - hi.astype(F32)
    mid = r1.astype(BF16)
    lo = (r1 - mid.astype(F32)).astype(BF16)
    return hi, mid, lo


def _log_sigmoid(x):
    return jnp.minimum(x, 0.0) - jnp.log1p(jnp.exp(-jnp.abs(x)))


def _dot(a, b):
    return jnp.dot(a, b, preferred_element_type=F32)


def _gelu_tanh(x):
    c0 = math.sqrt(2.0 / math.pi)
    hx = 0.5 * x
    return hx + hx * jnp.tanh(x * (c0 + (c0 * 0.044715) * (x * x)))


def _rotary(x, cos2, sin2):
    return x * cos2 + pltpu.roll(x, RET_KEY_DIM // 2, 1) * sin2


def _const_spec(shape):
    n = len(shape)
    return pl.BlockSpec(shape, lambda *_: (0,) * n)


def _inproj_prompt_kernel(x_ref, g_ref, wq_ref, wk_ref, wvf_ref, bf_ref, wb_ref,
                          cos_ref, sin_ref, tri_ref, eq_ref, ek_ref, oneq_ref, onek_ref, onev_ref,
                          qT_ref, ka_ref, vT_ref, kT32_ref, vT32_ref, logf_ref, qb_ref, kb_ref, vb_ref,
                          stats_ref, sqq_ref, carry_ref):
    tm = x_ref.shape[0]
    VW = FOX_HEADS * V_PAD

    @pl.when(pl.program_id(0) == 0)
    def _():
        carry_ref[...] = jnp.zeros_like(carry_ref)

    h = _rmsnorm(x_ref[...], g_ref[...]).astype(BF16)
    zvf = _dot(h, wvf_ref[...])

    logf = _log_sigmoid(zvf[:, VW:] + bf_ref[...])
    logf_ref[...] = logf.T[:FOX_HEADS, :]
    lane = lax.broadcasted_iota(jnp.int32, logf.shape, 1)
    logf = jnp.where(lane < FOX_HEADS, logf, 0.0)
    r = _dot(tri_ref[...], jnp.concatenate(_split3(logf), axis=1))
    c = r[:, :LANES] + r[:, LANES:2 * LANES] + r[:, 2 * LANES:] + carry_ref[...]
    carry_ref[...] = c[tm - 1:tm, :]
    c2 = c * LOG2E
    hi, mid, lo = (t.astype(F32) for t in _split3(c2))
    c3 = (hi + pltpu.roll(mid, FOX_HEADS, 1) + pltpu.roll(lo, 2 * FOX_HEADS, 1)).astype(BF16)

    q_aug = (_dot(h, wq_ref[...]) * (FOX_HEAD_DIM ** -0.5 * LOG2E)
             + _dot(c3, eq_ref[...]) + oneq_ref[...])
    qT32 = q_aug.T
    k_aug = _dot(h, wk_ref[...]) + _dot(c3, ek_ref[...]) + onek_ref[...]
    kT32 = k_aug.T
    vT32 = (zvf[:, :VW] + onev_ref[...]).T
    qT = qT32.astype(BF16)
    k_aug = k_aug.astype(BF16)
    vT = vT32.astype(BF16)

    lane1 = lax.broadcasted_iota(jnp.int32, (1, LANES), 1)
    nq2 = jnp.zeros((1, LANES), F32)
    nk2 = jnp.zeros((1, LANES), F32)
    for hh in range(FOX_HEADS):
        qT_ref[hh] = qT[hh * HEAD_PAD:(hh + 1) * HEAD_PAD, :]
        ka_ref[hh] = k_aug[:, hh * HEAD_PAD:(hh + 1) * HEAD_PAD]
        vT_ref[hh] = vT[hh * V_PAD:(hh + 1) * V_PAD, :]
        qh = qT32[hh * HEAD_PAD:hh * HEAD_PAD + FOX_HEAD_DIM, :]
        kh = kT32[hh * HEAD_PAD:hh * HEAD_PAD + FOX_HEAD_DIM, :]
        kT32_ref[hh] = kh
        vT32_ref[hh] = vT32[hh * V_PAD:hh * V_PAD + FOX_HEAD_DIM, :]
        sqq_ref[hh:hh + 1, :] = jnp.sum(qh * kh, axis=0, keepdims=True)
        q2 = jnp.max(jnp.sum(qh * qh, axis=0, keepdims=True), axis=1, keepdims=True)
        k2 = jnp.max(jnp.sum(kh * kh, axis=0, keepdims=True), axis=1, keepdims=True)
        nq2 = jnp.where(lane1 == hh, q2, nq2)
        nk2 = jnp.where(lane1 == hh, k2, nk2)
    stats_ref[0] = jnp.concatenate(
        [nq2, nk2, c2[0:1, :], c2[tm - 1:tm, :], jnp.zeros((4, LANES), F32)], axis=0)

    zb = _dot(h, wb_ref[...])
    cos2 = cos_ref[...]
    sin2 = sin_ref[...]
    for hh in range(RET_HEADS):
        sl = slice(hh * RET_KEY_DIM, (hh + 1) * RET_KEY_DIM)
        qb_ref[:, sl] = _rotary(zb[:, sl], cos2, sin2).astype(BF16)
        xk = zb[:, RET_KW + hh * RET_KEY_DIM:RET_KW + (hh + 1) * RET_KEY_DIM]
        kb_ref[:, sl] = (_rotary(xk, cos2, sin2) * (RET_KEY_DIM ** -0.5)).astype(BF16)
    vb_ref[...] = zb[:, 2 * RET_KW:].astype(BF16)


def _inproj_prompt(x, g, wq, wk, wvf, bf, wb, cos2, sin2, consts, tm):
    S = x.shape[0]
    tri, eq, ek, oneq, onek, onev = consts
    row = lambda w: pl.BlockSpec((tm, w), lambda i: (i, 0))
    headT = pl.BlockSpec((FOX_HEADS, FOX_HEAD_DIM, tm), lambda i: (0, 0, i))
    in_specs = [row(D_MODEL), _const_spec(g.shape), _const_spec(wq.shape), _const_spec(wk.shape),
                _const_spec(wvf.shape),
                _const_spec(bf.shape), _const_spec(wb.shape), row(LANES), row(LANES),
                _const_spec(tri.shape), _const_spec(eq.shape), _const_spec(ek.shape),
                _const_spec(oneq.shape), _const_spec(onek.shape), _const_spec(onev.shape)]
    out_shape = (
        jax.ShapeDtypeStruct((FOX_HEADS, HEAD_PAD, S), BF16),
        jax.ShapeDtypeStruct((FOX_HEADS, S, HEAD_PAD), BF16),
        jax.ShapeDtypeStruct((FOX_HEADS, V_PAD, S), BF16),
        jax.ShapeDtypeStruct((FOX_HEADS, FOX_HEAD_DIM, S), F32),
        jax.ShapeDtypeStruct((FOX_HEADS, FOX_HEAD_DIM, S), F32),
        jax.ShapeDtypeStruct((FOX_HEADS, S), F32),
        jax.ShapeDtypeStruct((S, RET_KW), BF16),
        jax.ShapeDtypeStruct((S, RET_KW), BF16),
        jax.ShapeDtypeStruct((S, RET_VW), BF16),
        jax.ShapeDtypeStruct((S // tm, 8, LANES), F32),
        jax.ShapeDtypeStruct((FOX_HEADS, S), F32),
    )
    out_specs = (
        pl.BlockSpec((FOX_HEADS, HEAD_PAD, tm), lambda i: (0, 0, i)),
        pl.BlockSpec((FOX_HEADS, tm, HEAD_PAD), lambda i: (0, i, 0)),
        pl.BlockSpec((FOX_HEADS, V_PAD, tm), lambda i: (0, 0, i)),
        headT, headT, pl.BlockSpec((FOX_HEADS, tm), lambda i: (0, i)),
        row(RET_KW), row(RET_KW), row(RET_VW),
        pl.BlockSpec((1, 8, LANES), lambda i: (i, 0, 0)),
        pl.BlockSpec((FOX_HEADS, tm), lambda i: (0, i)),
    )
    return pl.pallas_call(
        _inproj_prompt_kernel,
        grid=(S // tm,),
        in_specs=in_specs,
        out_specs=out_specs,
        out_shape=out_shape,
        scratch_shapes=[pltpu.VMEM((1, LANES), F32)],
        compiler_params=pltpu.CompilerParams(dimension_semantics=("arbitrary",),
                                             vmem_limit_bytes=VMEM_LIMIT),
        name="inproj_prompt",
    )(x, g, wq, wk, wvf, bf, wb, cos2, sin2, tri, eq, ek, oneq, onek, onev)


def _inproj_sample_kernel(x_ref, g_ref, wqkv_ref, wf_ref, bf_ref, wb_ref, cos_ref, sin_ref,
                          q_ref, k32_ref, v32_ref, logf_ref, qb_ref, kb_ref, vb_ref):
    h = _rmsnorm(x_ref[...], g_ref[...]).astype(BF16)
    logf = _log_sigmoid(_dot(h, wf_ref[...]) + bf_ref[...])
    logf_ref[...] = logf[:, :FOX_HEADS]
    z = _dot(h, wqkv_ref[...])
    q_ref[...] = (z[:, :FOX_W] * (FOX_HEAD_DIM ** -0.5 * LOG2E)).astype(BF16)
    k32_ref[...] = z[:, FOX_W:2 * FOX_W]
    v32_ref[...] = z[:, 2 * FOX_W:]
    zb = _dot(h, wb_ref[...])
    cos2 = cos_ref[...]
    sin2 = sin_ref[...]
    for hh in range(RET_HEADS):
        sl = slice(hh * RET_KEY_DIM, (hh + 1) * RET_KEY_DIM)
        qb_ref[:, sl] = _rotary(zb[:, sl], cos2, sin2).astype(BF16)
        xk = zb[:, RET_KW + hh * RET_KEY_DIM:RET_KW + (hh + 1) * RET_KEY_DIM]
        kb_ref[:, sl] = (_rotary(xk, cos2, sin2) * (RET_KEY_DIM ** -0.5)).astype(BF16)
    vb_ref[...] = zb[:, 2 * RET_KW:].astype(BF16)


def _inproj_sample(x, g, wqkv, wf, bf, wb, cos2, sin2):
    M = x.shape[0]
    args = (x, g, wqkv, wf, bf, wb, cos2, sin2)
    out_shape = (
        jax.ShapeDtypeStruct((M, FOX_W), BF16),
        jax.ShapeDtypeStruct((M, FOX_W), F32),
        jax.ShapeDtypeStruct((M, FOX_W), F32),
        jax.ShapeDtypeStruct((M, FOX_HEADS), F32),
        jax.ShapeDtypeStruct((M, RET_KW), BF16),
        jax.ShapeDtypeStruct((M, RET_KW), BF16),
        jax.ShapeDtypeStruct((M, RET_VW), BF16),
    )
    return pl.pallas_call(
        _inproj_sample_kernel,
        grid=(1,),
        in_specs=[_const_spec(a.shape) for a in args],
        out_specs=tuple(_const_spec(o.shape) for o in out_shape),
        out_shape=out_shape,
        compiler_params=pltpu.CompilerParams(dimension_semantics=("arbitrary",),
                                             vmem_limit_bytes=VMEM_LIMIT),
        name="inproj_sample",
    )(*args)


def _fox_prompt_kernel(it_ref, jt_ref, jfetch_ref, mode_ref, qT_ref, ka_ref, vT_ref,
                       sqq_ref, o_ref, m_ref, acc_ref, *, n_strips):
    del jfetch_ref
    t = pl.program_id(0)
    i = it_ref[t]
    j = jt_ref[t]
    T = qT_ref.shape[2]
    SUB = T // n_strips
    EXACT = n_strips + 1

    def heads(fn):
        def body(hh, carry):
            fn(hh, mode_ref[t * FOX_HEADS + hh])
            return carry
        lax.fori_loop(0, FOX_HEADS, body, 0)

    def scores(hh):
        return _dot(ka_ref[hh], qT_ref[hh])

    def exact_head(hh):
        s = scores(hh)
        kk = lax.broadcasted_iota(jnp.int32, s.shape, 0) + j * T
        qq = lax.broadcasted_iota(jnp.int32, s.shape, 1) + i * T
        s = jnp.where(kk > qq, NEG, s)
        m_old = m_ref[hh]
        m_new = jnp.maximum(m_old, jnp.max(s, axis=0, keepdims=True))
        p = jnp.exp2(s - m_new).astype(BF16)
        alpha = jnp.exp2(m_old - m_new)
        acc_ref[hh] = alpha * acc_ref[hh] + _dot(vT_ref[hh], p)
        m_ref[hh] = m_new


    def diag_head(hh, mode):
        m_ref[hh] = sqq_ref[pl.ds(hh, 1), :]

        @pl.when(mode != EXACT)
        def _():
            s = scores(hh)
            kk = lax.broadcasted_iota(jnp.int32, s.shape, 0)
            qq = lax.broadcasted_iota(jnp.int32, s.shape, 1)
            p = jnp.exp2(jnp.where(kk > qq, NEG, s) - m_ref[hh]).astype(BF16)
            acc_ref[hh] = _dot(vT_ref[hh], p)

        @pl.when(mode == EXACT)
        def _():
            acc_ref[hh] = jnp.zeros((V_PAD, T), F32)
            exact_head(hh)

    def off_head(hh, mode):
        for nn in range(1, n_strips + 1):
            k0 = (n_strips - nn) * SUB

            @pl.when(mode == nn)
            def _():
                p = jnp.exp2(_dot(ka_ref[hh, k0:, :], qT_ref[hh]) - m_ref[hh]).astype(BF16)
                acc_ref[hh] += _dot(vT_ref[hh, :, k0:], p)

        @pl.when(mode == EXACT)
        def _():
            exact_head(hh)

    @pl.when(j == i)
    def _():
        heads(diag_head)

    @pl.when(j < i)
    def _():
        heads(off_head)

    @pl.when(j == 0)
    def _():
        for pr in range(FOX_HEADS // 2):
            halves = []
            for hh in (2 * pr, 2 * pr + 1):
                a = acc_ref[hh]
                halves.append(a[:FOX_HEAD_DIM] / a[FOX_HEAD_DIM:FOX_HEAD_DIM + 1])
            o_ref[pr] = jnp.concatenate(halves, axis=0).T.astype(BF16)


def _prune_tables(stats, it, jt, nb):
    per = stats.shape[0] // nb
    st = stats.reshape(nb, per, 8, LANES)[:, :, :, :FOX_HEADS]
    nq = jnp.sqrt(jnp.max(st[:, :, 0, :], axis=1)) * NORM_SLACK
    nk_strip = jnp.sqrt(st[:, :, 1, :]) * NORM_SLACK
    nk = jnp.max(nk_strip, axis=1)
    c_first = st[:, 0, 2, :]
    c_last = st[:, :, 3, :]
    bound = (nq[it][:, None, :] * (nk_strip[jt] + nk[it][:, None, :])
             - (c_last[jt] - c_first[it][:, None, :]))
    live = jnp.logical_or(jnp.asarray(jt == it)[:, None, None],
                          jnp.logical_not(bound < PRUNE_LOG2))
    strip_no = jnp.arange(per, dtype=jnp.int32)[None, :, None]
    n_keep = per - jnp.min(jnp.where(live, strip_no, per), axis=1)
    safe = nq[it] * (nk[jt] + nk[it]) < STALE_SAFE_LOG2
    mode = jnp.where(n_keep == 0, 0, jnp.where(safe, n_keep, per + 1))
    steps = jnp.arange(len(it), dtype=jnp.int32)
    last_live = lax.cummax(jnp.where(jnp.any(n_keep > 0, axis=1), steps, 0))
    return jnp.asarray(jt)[last_live], mode.astype(jnp.int32).reshape(-1), per


def _fox_prompt(qT, ka, vT, stats, sqq, T):
    S = ka.shape[1]
    nb = S // T
    it = np.array([i for i in range(nb) for _ in range(i + 1)], np.int32)
    jt = np.array([j for i in range(nb) for j in range(i, -1, -1)], np.int32)
    jfetch, mode, n_strips = _prune_tables(stats, it, jt, nb)
    grid_spec = pltpu.PrefetchScalarGridSpec(
        num_scalar_prefetch=4,
        grid=(len(it),),
        in_specs=[
            pl.BlockSpec((FOX_HEADS, HEAD_PAD, T), lambda t, it, jt, jf, md: (0, 0, it[t])),
            pl.BlockSpec((FOX_HEADS, T, HEAD_PAD), lambda t, it, jt, jf, md: (0, jf[t], 0)),
            pl.BlockSpec((FOX_HEADS, V_PAD, T), lambda t, it, jt, jf, md: (0, 0, jf[t])),
            pl.BlockSpec((FOX_HEADS, T), lambda t, it, jt, jf, md: (0, it[t])),
        ],
        out_specs=pl.BlockSpec((FOX_HEADS // 2, T, LANES),
                               lambda t, it, jt, jf, md: (0, it[t], 0)),
        scratch_shapes=[pltpu.VMEM((FOX_HEADS, 1, T), F32),
                        pltpu.VMEM((FOX_HEADS, V_PAD, T), F32)],
    )
    return pl.pallas_call(
        functools.partial(_fox_prompt_kernel, n_strips=n_strips),
        grid_spec=grid_spec,
        out_shape=jax.ShapeDtypeStruct((FOX_HEADS // 2, S, LANES), BF16),
        compiler_params=pltpu.CompilerParams(dimension_semantics=("arbitrary",),
                                             vmem_limit_bytes=VMEM_LIMIT),
        name="fox_prompt",
    )(jnp.asarray(it), jnp.asarray(jt), jfetch, mode, qT, ka, vT, sqq)


def _fox_sample_kernel(q_ref, kn_ref, vn_ref, ckT_ref, cvT_ref, lfT_ref, up_ref, ex_ref, o_ref):
    P = ckT_ref.shape[3]
    Tn = q_ref.shape[0]
    KP = lfT_ref.shape[2]
    HQ = FOX_HEADS * Tn
    nchunk = KP // LANES
    nt = (((1,), (1,)), ((), ()))

    def stack3(x):
        parts3 = [t.astype(F32) for t in _split3(x)] + [jnp.zeros_like(x)]
        return jnp.concatenate(parts3, axis=0).astype(BF16)

    x3 = stack3(lfT_ref[0])
    up = up_ref[...]
    run = jnp.zeros((4 * FOX_HEADS, 1), F32)
    parts = []
    for cidx in range(nchunk):
        y = _dot(x3[:, cidx * LANES:(cidx + 1) * LANES], up) + run
        parts.append(y)
        run = y[:, LANES - 1:LANES]
    y = jnp.concatenate(parts, axis=1)
    cT = (y[:FOX_HEADS] + y[FOX_HEADS:2 * FOX_HEADS] + y[2 * FOX_HEADS:3 * FOX_HEADS]) * LOG2E
    ckx = _dot(ex_ref[...], stack3(cT))

    tail = ckx[:, P:P + LANES]
    rowq = lax.broadcasted_iota(jnp.int32, tail.shape, 0) % Tn
    lanek = lax.broadcasted_iota(jnp.int32, tail.shape, 1)
    cq = jnp.sum(jnp.where(lanek == rowq, tail, 0.0), axis=1, keepdims=True)

    q = q_ref[...]
    qt = jnp.concatenate([q] * FOX_HEADS, axis=0)
    rh = lax.broadcasted_iota(jnp.int32, qt.shape, 0) // Tn
    lh = lax.broadcasted_iota(jnp.int32, qt.shape, 1) // FOX_HEAD_DIM
    qbd = jnp.where(rh == lh, qt, jnp.zeros_like(qt))

    kT = ckT_ref[0].reshape(FOX_W, P).astype(BF16)
    vT = cvT_ref[0].reshape(FOX_W, P).astype(BF16)
    s_c = _dot(qbd, kT) + cq - ckx[:, :P]
    s_n = lax.dot_general(qbd, kn_ref[...].astype(BF16), nt, preferred_element_type=F32)
    s_n = s_n + cq - ckx[:, P:P + Tn]
    key = lax.broadcasted_iota(jnp.int32, s_n.shape, 1)
    qrow = lax.broadcasted_iota(jnp.int32, s_n.shape, 0) % Tn
    s_n = jnp.where(key > qrow, NEG, s_n)
    m = jnp.maximum(jnp.max(s_c, axis=1, keepdims=True), jnp.max(s_n, axis=1, keepdims=True))
    p_c = jnp.exp2(s_c - m)
    p_n = jnp.exp2(s_n - m)
    l = jnp.sum(p_c, axis=1, keepdims=True) + jnp.sum(p_n, axis=1, keepdims=True)
    z = lax.dot_general(p_c.astype(BF16), vT, nt, preferred_element_type=F32)
    z = (z + _dot(p_n.astype(BF16), vn_ref[...].astype(BF16))) / l
    zh = lax.broadcasted_iota(jnp.int32, (Tn, FOX_W), 1) // FOX_HEAD_DIM
    o = jnp.zeros((Tn, FOX_W), F32)
    for hh in range(FOX_HEADS):
        o = o + jnp.where(zh == hh, z[hh * Tn:(hh + 1) * Tn, :], 0.0)
    o_ref[...] = o.astype(BF16)


def _fox_sample(q, kn, vn, cache_kT, cache_vT, lfT, B, Tn):
    P = cache_kT.shape[3]
    KP = lfT.shape[2]
    HQ = FOX_HEADS * Tn
    up = jnp.asarray(np.triu(np.ones((LANES, LANES), np.float32)), BF16)
    ex = np.zeros((HQ, 4 * FOX_HEADS), np.float32)
    for part in range(3):
        for hh in range(FOX_HEADS):
            ex[hh * Tn:(hh + 1) * Tn, part * FOX_HEADS + hh] = 1.0
    ex = jnp.asarray(ex, BF16)
    rowb = lambda w: pl.BlockSpec((Tn, w), lambda b: (b, 0))
    return pl.pallas_call(
        _fox_sample_kernel,
        grid=(B,),
        in_specs=[rowb(FOX_W), rowb(FOX_W), rowb(FOX_W),
                  pl.BlockSpec((1, FOX_HEADS, FOX_HEAD_DIM, P), lambda b: (b, 0, 0, 0)),
                  pl.BlockSpec((1, FOX_HEADS, FOX_HEAD_DIM, P), lambda b: (b, 0, 0, 0)),
                  pl.BlockSpec((1, FOX_HEADS, KP), lambda b: (b, 0, 0)),
                  _const_spec(up.shape), _const_spec(ex.shape)],
        out_specs=rowb(FOX_W),
        out_shape=jax.ShapeDtypeStruct((B * Tn, FOX_W), BF16),
        compiler_params=pltpu.CompilerParams(dimension_semantics=("arbitrary",),
                                             vmem_limit_bytes=VMEM_LIMIT),
        name="fox_sample",
    )(q, kn, vn, cache_kT, cache_vT, lfT, up, ex)


def _retention_kernel(q_ref, k_ref, v_ref, s0_ref, dmat_ref, xi_ref, zeta_ref, gam_ref,
                      n_ref, sout_ref, st_ref):
    c = pl.program_id(1)

    @pl.when(c == 0)
    def _():
        st_ref[...] = s0_ref[0]

    for hh in range(RET_HEADS):
        q = q_ref[:, hh * RET_KEY_DIM:(hh + 1) * RET_KEY_DIM]
        k = k_ref[:, hh * RET_KEY_DIM:(hh + 1) * RET_KEY_DIM]
        v = v_ref[:, hh * RET_VAL_DIM:(hh + 1) * RET_VAL_DIM]
        st = st_ref[hh]
        sc = lax.dot_general(q, k, (((1,), (1,)), ((), ())), preferred_element_type=F32)
        sc = sc * dmat_ref[hh]
        o = _dot(sc.astype(BF16), v) + _dot(q, st.astype(BF16)) * xi_ref[hh]
        kz = (k.astype(F32) * zeta_ref[hh]).astype(BF16)
        upd = lax.dot_general(kz, v, (((0,), (0,)), ((), ())), preferred_element_type=F32)
        st_ref[hh] = gam_ref[hh] * st + upd
        mu = jnp.mean(o, axis=-1, keepdims=True)
        d = o - mu
        var = jnp.mean(d * d, axis=-1, keepdims=True)
        n_ref[:, hh * RET_VAL_DIM:(hh + 1) * RET_VAL_DIM] = (d * lax.rsqrt(var + EPS)).astype(BF16)

    @pl.when(c == pl.num_programs(1) - 1)
    def _():
        sout_ref[0] = st_ref[...]


def _ret_log_gamma():
    return jnp.log(1.0 - jnp.exp2(-5.0 - jnp.arange(RET_HEADS, dtype=F32)))


def _retention(q, k, v, state0, B, L, C):
    nc = L // C
    lg = _ret_log_gamma()
    idx = jnp.arange(C, dtype=F32)
    diff = idx[:, None] - idx[None, :]
    dmat = jnp.where(diff[None] >= 0, jnp.exp(jnp.maximum(diff, 0.0)[None] * lg[:, None, None]), 0.0)
    xi = jnp.exp((idx[None, :] + 1.0) * lg[:, None])
    zeta = jnp.exp((C - 1.0 - idx[None, :]) * lg[:, None])
    xi = jnp.broadcast_to(xi[:, :, None], (RET_HEADS, C, RET_VAL_DIM))
    zeta = jnp.broadcast_to(zeta[:, :, None], (RET_HEADS, C, RET_KEY_DIM))
    gam = jnp.broadcast_to(jnp.exp(C * lg)[:, None, None], (RET_HEADS, 1, RET_VAL_DIM))
    rowc = lambda w: pl.BlockSpec((C, w), lambda b, c: (b * nc + c, 0))
    st_spec = pl.BlockSpec((1, RET_HEADS, RET_KEY_DIM, RET_VAL_DIM), lambda b, c: (b, 0, 0, 0))
    return pl.pallas_call(
        _retention_kernel,
        grid=(B, nc),
        in_specs=[rowc(RET_KW), rowc(RET_KW), rowc(RET_VW), st_spec,
                  _const_spec(dmat.shape), _const_spec(xi.shape), _const_spec(zeta.shape),
                  _const_spec(gam.shape)],
        out_specs=(rowc(RET_VW), st_spec),
        out_shape=(jax.ShapeDtypeStruct((B * L, RET_VW), BF16),
                   jax.ShapeDtypeStruct((B, RET_HEADS, RET_KEY_DIM, RET_VAL_DIM), F32)),
        scratch_shapes=[pltpu.VMEM((RET_HEADS, RET_KEY_DIM, RET_VAL_DIM), F32)],
        compiler_params=pltpu.CompilerParams(dimension_semantics=("arbitrary", "arbitrary"),
                                             vmem_limit_bytes=VMEM_LIMIT),
        name="retention",
    )(q, k, v, state0, dmat, xi, zeta, gam)


def _mixer_ffn_kernel(x_ref, oa_ref, nb_ref, prev_ref, gmix_ref, wg_ref, gng_ref, wpa_ref, wpb_ref,
                      wo_ref, gffn_ref, wup_ref, cw_ref, cb_ref, wdn_ref, gfin_ref,
                      y_ref, conv_ref, carry_ref, ua_ref, ub_ref, acc_ref, h2_ref,
                      *, nseg, seglen):
    i = pl.program_id(0)
    NC = N_FFN_CHUNKS
    PADR = 8
    H0 = PADR - (CONV_WIDTH - 1)

    @pl.when(i == 0)
    def _():
        carry_ref[...] = prev_ref[...]

    x = x_ref[...]
    h = _rmsnorm(x, gmix_ref[...]).astype(BF16)
    zg = _dot(h, wg_ref[...])
    gb = zg[:, :RET_VW]
    gma = zg[:, RET_VW:RET_VW + D_MODEL]
    gmb = zg[:, RET_VW + D_MODEL:]
    oa = jnp.concatenate([oa_ref[p] for p in range(FOX_HEADS // 2)], axis=1)
    ya = _dot(oa, wpa_ref[...])
    nn = nb_ref[...].astype(F32) * gng_ref[...] * (gb * jax.nn.sigmoid(gb))
    yb = _dot(nn.astype(BF16), wpb_ref[...])
    y = jax.nn.sigmoid(gma) * ya + jax.nn.sigmoid(gmb) * yb
    x1 = x + _dot(y.astype(BF16), wo_ref[...])
    h2_ref[...] = _rmsnorm(x1, gffn_ref[...]).astype(BF16)
    acc_ref[...] = x1

    h2 = h2_ref[...]

    def up_half(u_ref, cidx, slot):
        u = _dot(h2, wup_ref[cidx])
        for s in range(nseg):
            u_ref[slot, s, PADR:PADR + seglen, :] = u[s * seglen:(s + 1) * seglen, :]
            u_ref[slot, s, H0:PADR, :] = carry_ref[cidx, s, H0:PADR, :]
            carry_ref[cidx, s, H0:PADR, :] = u[(s + 1) * seglen - (CONV_WIDTH - 1):(s + 1) * seglen, :]

    def conv_half(u_ref, cidx, slot):
        w = cw_ref[cidx]
        b = cb_ref[cidx]
        outs = []
        for s in range(nseg):
            acc = w[0:1] * u_ref[slot, s, H0:H0 + seglen, :]
            for jj in range(1, CONV_WIDTH):
                acc = acc + w[jj:jj + 1] * u_ref[slot, s, H0 + jj:H0 + jj + seglen, :]
            outs.append(b + acc)
        return outs[0] if nseg == 1 else jnp.concatenate(outs, axis=0)

    def stage_up(c):
        up_half(ua_ref, c, c % 2)
        up_half(ub_ref, NC + c, c % 2)

    stage_up(0)
    group = []
    for c in range(NC):
        if c + 1 < NC:
            stage_up(c + 1)
        a = conv_half(ua_ref, c, c % 2)
        b = conv_half(ub_ref, NC + c, c % 2)
        group.append((_gelu_tanh(a) * b).astype(BF16))
        if len(group) == DOWN_GROUP or c + 1 == NC:
            c0 = c + 1 - len(group)
            wd = wdn_ref[c0:c + 1].reshape(len(group) * FFN_CHUNK, D_MODEL)
            acc_ref[...] += _dot(jnp.concatenate(group, axis=1), wd)
            group = []
    y_ref[...] = _rmsnorm(acc_ref[...], gfin_ref[...])

    @pl.when(i == pl.num_programs(0) - 1)
    def _():
        conv_ref[...] = carry_ref[:, :, H0:PADR, :]


def _mixer_ffn(x, oa, nb, prev, weights, tm, nseg, seglen):
    M = x.shape[0]
    gmix, wg, gng, wpa, wpb, wo, gffn, wup, cw, cb, wdn, gfin = weights
    FC = FFN_CHUNK
    NC = N_FFN_CHUNKS
    row = lambda w: pl.BlockSpec((tm, w), lambda i: (i, 0))
    wspec = lambda a: pl.BlockSpec(a.shape, lambda i, n=a.ndim: (0,) * n,
                                   pipeline_mode=pl.Buffered(1))
    in_specs = [row(D_MODEL),
                pl.BlockSpec((FOX_HEADS // 2, tm, LANES), lambda i: (0, i, 0)),
                row(RET_VW), wspec(prev)] + [wspec(w) for w in weights]
    out_shape = (jax.ShapeDtypeStruct((M, D_MODEL), F32),
                 jax.ShapeDtypeStruct((2 * NC, nseg, CONV_WIDTH - 1, FC), F32))
    out_specs = (row(D_MODEL), _const_spec(out_shape[1].shape))
    return pl.pallas_call(
        functools.partial(_mixer_ffn_kernel, nseg=nseg, seglen=seglen),
        grid=(M // tm,),
        in_specs=in_specs,
        out_specs=out_specs,
        out_shape=out_shape,
        scratch_shapes=[pltpu.VMEM((2 * NC, nseg, 8, FC), F32),
                        pltpu.VMEM((2, nseg, 8 + seglen, FC), F32),
                        pltpu.VMEM((2, nseg, 8 + seglen, FC), F32),
                        pltpu.VMEM((tm, D_MODEL), F32),
                        pltpu.VMEM((tm, D_MODEL), BF16)],
        compiler_params=pltpu.CompilerParams(dimension_semantics=("arbitrary",),
                                             vmem_limit_bytes=VMEM_LIMIT),
        name="mixer_ffn",
    )(x, oa, nb, prev, *weights)


def _rotary_tables(start, n):
    half = RET_KEY_DIM // 2
    inv = 1.0 / (ROPE_BASE ** jnp.linspace(0.0, 1.0, half, dtype=F32))
    fine = min(n, ROT_FINE)
    assert n % fine == 0
    a_hi = (start + fine * jnp.arange(n // fine)).astype(F32)[:, None] * inv[None, :]
    a_lo = jnp.arange(fine).astype(F32)[:, None] * inv[None, :]
    ch, sh = jnp.cos(a_hi)[:, None, :], jnp.sin(a_hi)[:, None, :]
    cl, sl = jnp.cos(a_lo)[None, :, :], jnp.sin(a_lo)[None, :, :]
    cos = (ch * cl - sh * sl).reshape(n, half)
    sin = (sh * cl + ch * sl).reshape(n, half)
    return jnp.concatenate([cos, cos], axis=1), jnp.concatenate([-sin, sin], axis=1)


def _pad_heads(wt, pad):
    d = wt.shape[1]
    wt = wt.reshape(FOX_HEADS, FOX_HEAD_DIM, d)
    wt = jnp.pad(wt, ((0, 0), (0, pad - FOX_HEAD_DIM), (0, 0)))
    return wt.reshape(FOX_HEADS * pad, d)


def _prompt_consts(tm):
    tri = np.tril(np.ones((tm, tm), np.float32))
    eq = np.zeros((LANES, FOX_HEADS * HEAD_PAD), np.float32)
    ek = np.zeros((LANES, FOX_HEADS * HEAD_PAD), np.float32)
    oneq = np.zeros((1, FOX_HEADS * HEAD_PAD), np.float32)
    onek = np.zeros((1, FOX_HEADS * HEAD_PAD), np.float32)
    onev = np.zeros((1, FOX_HEADS * V_PAD), np.float32)
    for hh in range(FOX_HEADS):
        base = hh * HEAD_PAD + BIAS_COL
        for part in range(3):
            eq[part * FOX_HEADS + hh, base + part] = 1.0
            ek[part * FOX_HEADS + hh, base + 3 + part] = -1.0
            onek[0, base + part] = 1.0
            oneq[0, base + 3 + part] = 1.0
        onev[0, hh * V_PAD + FOX_HEAD_DIM] = 1.0
    return (jnp.asarray(tri, BF16), jnp.asarray(eq, BF16), jnp.asarray(ek, BF16),
            jnp.asarray(oneq), jnp.asarray(onek), jnp.asarray(onev))


def _chunk_cols(a):
    lead = a.shape[:-1]
    a = a.reshape(lead + (2 * N_FFN_CHUNKS, FFN_CHUNK))
    return jnp.moveaxis(a, -2, 0)


def _tile(n, pref):
    t = min(n, pref)
    while n % t:
        t //= 2
    return t


def kernel(x_prompt, x_sample, cache_fox_k, cache_fox_v, cache_fox_logf, state_ret, state_ffn_conv,
           norm_mix_g, w_in, b_fox_f, gn_ret_g, w_pa, w_pb, w_o, norm_ffn_g, w_up, conv_w, conv_b,
           w_down, norm_final_g):
    depth = w_in.shape[0]
    Bp, S, _ = x_prompt.shape
    Bs, Ts, _ = x_sample.shape
    P = cache_fox_k.shape[2]
    assert depth == 1 and Bp == 1, "kernel handles the single-layer, single-prompt configuration"
    l = 0

    wt = jnp.swapaxes(w_in[l], 0, 1).astype(BF16)
    o0 = 3 * FOX_W
    o1 = o0 + FOX_HEADS
    o2 = o1 + 2 * RET_KW + RET_VW
    wq_aug = _pad_heads(wt[:FOX_W], HEAD_PAD).T
    wk_aug = _pad_heads(wt[FOX_W:2 * FOX_W], HEAD_PAD).T
    wf_t = jnp.pad(wt[o0:o1], ((0, LANES - FOX_HEADS), (0, 0)))
    wvf = jnp.concatenate([_pad_heads(wt[2 * FOX_W:o0], V_PAD), wf_t], axis=0).T
    wqkv = wt[:o0].T
    wf = wf_t.T
    bf = jnp.pad(b_fox_f[l].astype(F32), (0, LANES - FOX_HEADS))[None, :]
    wb = wt[o1:o2].T
    wg = wt[o2:].T
    gmix = norm_mix_g[l].astype(F32)[None, :]
    mix_weights = (
        gmix, wg, gn_ret_g[l].astype(F32)[None, :], w_pa[l].astype(BF16), w_pb[l].astype(BF16),
        w_o[l].astype(BF16), norm_ffn_g[l].astype(F32)[None, :],
        _chunk_cols(w_up[l]).astype(BF16),
        jnp.pad(_chunk_cols(conv_w[l].astype(F32)), ((0, 0), (0, 8 - CONV_WIDTH), (0, 0))),
        _chunk_cols(conv_b[l].astype(F32)[None, :]),
        w_down[l].reshape(N_FFN_CHUNKS, FFN_CHUNK, D_MODEL).astype(BF16),
        norm_final_g.astype(F32)[None, :],
    )

    tm_a = _tile(S, 256)
    cos_p, sin_p = _rotary_tables(0, S)
    (qT, ka, vT, kT_p, vT_p, logfT_p, qb, kb, vb, stats, sqq) = _inproj_prompt(
        x_prompt[0], gmix, wq_aug, wk_aug, wvf, bf, wb, cos_p, sin_p,
        _prompt_consts(tm_a), tm_a)
    oa_p = _fox_prompt(qT, ka, vT, stats, sqq, _tile(S, 1024))
    zero_state = jnp.zeros((1, RET_HEADS, RET_KEY_DIM, RET_VAL_DIM), F32)
    nb_p, ret_p = _retention(qb, kb, vb, zero_state, 1, S, _tile(S, 512))
    tm_d = _tile(S, 256)
    zero_prev = jnp.zeros((2 * N_FFN_CHUNKS, 1, 8, FFN_CHUNK), F32)
    y_p, conv_p = _mixer_ffn(x_prompt[0], oa_p, nb_p, zero_prev, mix_weights, tm_d, 1, tm_d)

    Ms = Bs * Ts
    cos_s, sin_s = _rotary_tables(P, Ts)
    cos_s = jnp.tile(cos_s, (Bs, 1))
    sin_s = jnp.tile(sin_s, (Bs, 1))
    (q_s, k_s, v_s, logf_s, qb_s, kb_s, vb_s) = _inproj_sample(
        x_sample.reshape(Ms, D_MODEL), gmix, wqkv, wf, bf, wb, cos_s, sin_s)
    KP = ((P + Ts + LANES - 1) // LANES) * LANES
    lf_all = jnp.concatenate([cache_fox_logf[l].astype(F32), logf_s.reshape(Bs, Ts, FOX_HEADS)], axis=1)
    lfT = jnp.pad(jnp.swapaxes(lf_all, 1, 2), ((0, 0), (0, 0), (0, KP - P - Ts)))
    oa_s = _fox_sample(q_s, k_s, v_s, jnp.transpose(cache_fox_k[l], (0, 2, 3, 1)),
                       jnp.transpose(cache_fox_v[l], (0, 2, 3, 1)), lfT, Bs, Ts)
    oa_s = jnp.moveaxis(oa_s.reshape(Ms, FOX_HEADS // 2, LANES), 1, 0)
    nb_s, ret_s = _retention(qb_s, kb_s, vb_s, state_ret[l].astype(F32), Bs, Ts, Ts)
    prev_s = _chunk_cols(state_ffn_conv[l].astype(F32))
    prev_s = jnp.pad(prev_s, ((0, 0), (0, 0), (8 - (CONV_WIDTH - 1), 0), (0, 0)))
    y_s, conv_s = _mixer_ffn(x_sample.reshape(Ms, D_MODEL), oa_s, nb_s, prev_s, mix_weights,
                             Ms, Bs, Ts)

    def unchunk(cv):
        return jnp.moveaxis(cv, 0, 2).reshape(cv.shape[1], CONV_WIDTH - 1, 2 * FFN_DIM)

    hshape = (FOX_HEADS, FOX_HEAD_DIM)
    return (
        y_p[None],
        y_s.reshape(Bs, Ts, D_MODEL),
        jnp.transpose(kT_p, (2, 0, 1))[None, None],
        jnp.transpose(vT_p, (2, 0, 1))[None, None],
        jnp.transpose(logfT_p, (1, 0))[None, None],
        ret_p[None],
        unchunk(conv_p)[None],
        k_s.reshape((1, Bs, Ts) + hshape),
        v_s.reshape((1, Bs, Ts) + hshape),
        logf_s.reshape(1, Bs, Ts, FOX_HEADS),
        ret_s[None],
        unchunk(conv_s)[None],
    )
```

```python
import functools
import math

import numpy as np
import jax
import jax.numpy as jnp
from jax import lax
from jax.experimental import pallas as pl
from jax.experimental.pallas import tpu as pltpu

F32 = jnp.float32
BF16 = jnp.bfloat16

D_MODEL = 1024
FOX_HEADS = 8
FOX_HEAD_DIM = 64
RET_HEADS = 4
RET_KEY_DIM = 128
RET_VAL_DIM = 256
FFN_DIM = 2816
CONV_WIDTH = 3
EPS = 1e-6
ROPE_BASE = 10000.0

FOX_W = FOX_HEADS * FOX_HEAD_DIM
RET_KW = RET_HEADS * RET_KEY_DIM
RET_VW = RET_HEADS * RET_VAL_DIM

LOG2E = 1.4426950408889634
LANES = 128
HEAD_PAD = LANES
V_PAD = 80
BIAS_COL = FOX_HEAD_DIM
NEG = -1e30
STALE_SAFE_LOG2 = 64.0
PRUNE_LOG2 = -160.0
NORM_SLACK = 1.02
ROT_FINE = 128
FFN_CHUNK = 256
N_FFN_CHUNKS = FFN_DIM // FFN_CHUNK
DOWN_GROUP = 4
VMEM_LIMIT = 56 * 1024 * 1024


def _rmsnorm(x, g):
    ms = jnp.mean(x * x, axis=-1, keepdims=True)
    return x * lax.rsqrt(ms + EPS) * g


def _split3(x):
    hi = x.astype(BF16)
    r1 = x - hi.astype(F32)
    mid = r1.astype(BF16)
    lo = (r1 - mid.astype(F32)).astype(BF16)
    return hi, mid, lo


def _log_sigmoid(x):
    return jnp.minimum(x, 0.0) - jnp.log1p(jnp.exp(-jnp.abs(x)))


def _dot(a, b):
    return jnp.dot(a, b, preferred_element_type=F32)


def _dot_t(a, bt):
    return lax.dot_general(a, bt, (((1,), (1,)), ((), ())), preferred_element_type=F32)


def _gelu_tanh(x):
    c0 = math.sqrt(2.0 / math.pi)
    hx = 0.5 * x
    return hx + hx * jnp.tanh(x * (c0 + (c0 * 0.044715) * (x * x)))


def _rotary(x, cos2, sin2):
    return x * cos2 + pltpu.roll(x, RET_KEY_DIM // 2, 1) * sin2


def _const_spec(shape):
    n = len(shape)
    return pl.BlockSpec(shape, lambda *_: (0,) * n)


def _inproj_prompt_kernel(x_ref, g_ref, wq_ref, wk_ref, wvf_ref, bf_ref, wb_ref,
                          cos_ref, sin_ref, tri_ref, eq_ref, ek_ref, oneq_ref, onek_ref, onev_ref,
                          qT_ref, ka_ref, vT_ref, kT32_ref, vT32_ref, logf_ref, qb_ref, kb_ref, vb_ref,
                          stats_ref, sqq_ref, carry_ref):
    tm = x_ref.shape[0]
    VW = FOX_HEADS * V_PAD

    @pl.when(pl.program_id(0) == 0)
    def _():
        carry_ref[...] = jnp.zeros_like(carry_ref)

    h = _rmsnorm(x_ref[...], g_ref[...]).astype(BF16)
    zvf = _dot_t(h, wvf_ref[...])

    logf = _log_sigmoid(zvf[:, VW:] + bf_ref[...])
    logf_ref[...] = logf.T[:FOX_HEADS, :]
    lane = lax.broadcasted_iota(jnp.int32, logf.shape, 1)
    logf = jnp.where(lane < FOX_HEADS, logf, 0.0)
    r = _dot(tri_ref[...], jnp.concatenate(_split3(logf), axis=1))
    c = r[:, :LANES] + r[:, LANES:2 * LANES] + r[:, 2 * LANES:] + carry_ref[...]
    carry_ref[...] = c[tm - 1:tm, :]
    c2 = c * LOG2E
    hi, mid, lo = (t.astype(F32) for t in _split3(c2))
    c3 = (hi + pltpu.roll(mid, FOX_HEADS, 1) + pltpu.roll(lo, 2 * FOX_HEADS, 1)).astype(BF16)

    q_aug = (_dot_t(h, wq_ref[...]) * (FOX_HEAD_DIM ** -0.5 * LOG2E)
             + _dot(c3, eq_ref[...]) + oneq_ref[...])
    qT32 = q_aug.T
    k_aug = _dot_t(h, wk_ref[...]) + _dot(c3, ek_ref[...]) + onek_ref[...]
    kT32 = k_aug.T
    vT32 = (zvf[:, :VW] + onev_ref[...]).T
    qT = qT32.astype(BF16)
    k_aug = k_aug.astype(BF16)
    vT = vT32.astype(BF16)

    lane1 = lax.broadcasted_iota(jnp.int32, (1, LANES), 1)
    nq2 = jnp.zeros((1, LANES), F32)
    nk2 = jnp.zeros((1, LANES), F32)
    for hh in range(FOX_HEADS):
        qT_ref[hh] = qT[hh * HEAD_PAD:(hh + 1) * HEAD_PAD, :]
        ka_ref[hh] = k_aug[:, hh * HEAD_PAD:(hh + 1) * HEAD_PAD]
        vT_ref[hh] = vT[hh * V_PAD:(hh + 1) * V_PAD, :]
        qh = qT32[hh * HEAD_PAD:hh * HEAD_PAD + FOX_HEAD_DIM, :]
        kh = kT32[hh * HEAD_PAD:hh * HEAD_PAD + FOX_HEAD_DIM, :]
        kT32_ref[hh] = kh
        vT32_ref[hh] = vT32[hh * V_PAD:hh * V_PAD + FOX_HEAD_DIM, :]
        sqq_ref[hh:hh + 1, :] = jnp.sum(qh * kh, axis=0, keepdims=True)
        q2 = jnp.max(jnp.sum(qh * qh, axis=0, keepdims=True), axis=1, keepdims=True)
        k2 = jnp.max(jnp.sum(kh * kh, axis=0, keepdims=True), axis=1, keepdims=True)
        nq2 = jnp.where(lane1 == hh, q2, nq2)
        nk2 = jnp.where(lane1 == hh, k2, nk2)
    stats_ref[0] = jnp.concatenate(
        [nq2, nk2, c2[0:1, :], c2[tm - 1:tm, :], jnp.zeros((4, LANES), F32)], axis=0)

    zb = _dot_t(h, wb_ref[...])
    cos2 = cos_ref[...]
    sin2 = sin_ref[...]
    for hh in range(RET_HEADS):
        sl = slice(hh * RET_KEY_DIM, (hh + 1) * RET_KEY_DIM)
        qb_ref[:, sl] = _rotary(zb[:, sl], cos2, sin2).astype(BF16)
        xk = zb[:, RET_KW + hh * RET_KEY_DIM:RET_KW + (hh + 1) * RET_KEY_DIM]
        kb_ref[:, sl] = (_rotary(xk, cos2, sin2) * (RET_KEY_DIM ** -0.5)).astype(BF16)
    vb_ref[...] = zb[:, 2 * RET_KW:].astype(BF16)


def _inproj_prompt(x, g, wq, wk, wvf, bf, wb, cos2, sin2, consts, tm):
    S = x.shape[0]
    tri, eq, ek, oneq, onek, onev = consts
    row = lambda w: pl.BlockSpec((tm, w), lambda i: (i, 0))
    headT = pl.BlockSpec((FOX_HEADS, FOX_HEAD_DIM, tm), lambda i: (0, 0, i))
    in_specs = [row(D_MODEL), _const_spec(g.shape), _const_spec(wq.shape), _const_spec(wk.shape),
                _const_spec(wvf.shape),
                _const_spec(bf.shape), _const_spec(wb.shape), row(LANES), row(LANES),
                _const_spec(tri.shape), _const_spec(eq.shape), _const_spec(ek.shape),
                _const_spec(oneq.shape), _const_spec(onek.shape), _const_spec(onev.shape)]
    out_shape = (
        jax.ShapeDtypeStruct((FOX_HEADS, HEAD_PAD, S), BF16),
        jax.ShapeDtypeStruct((FOX_HEADS, S, HEAD_PAD), BF16),
        jax.ShapeDtypeStruct((FOX_HEADS, V_PAD, S), BF16),
        jax.ShapeDtypeStruct((FOX_HEADS, FOX_HEAD_DIM, S), F32),
        jax.ShapeDtypeStruct((FOX_HEADS, FOX_HEAD_DIM, S), F32),
        jax.ShapeDtypeStruct((FOX_HEADS, S), F32),
        jax.ShapeDtypeStruct((S, RET_KW), BF16),
        jax.ShapeDtypeStruct((S, RET_KW), BF16),
        jax.ShapeDtypeStruct((S, RET_VW), BF16),
        jax.ShapeDtypeStruct((S // tm, 8, LANES), F32),
        jax.ShapeDtypeStruct((FOX_HEADS, S), F32),
    )
    out_specs = (
        pl.BlockSpec((FOX_HEADS, HEAD_PAD, tm), lambda i: (0, 0, i)),
        pl.BlockSpec((FOX_HEADS, tm, HEAD_PAD), lambda i: (0, i, 0)),
        pl.BlockSpec((FOX_HEADS, V_PAD, tm), lambda i: (0, 0, i)),
        headT, headT, pl.BlockSpec((FOX_HEADS, tm), lambda i: (0, i)),
        row(RET_KW), row(RET_KW), row(RET_VW),
        pl.BlockSpec((1, 8, LANES), lambda i: (i, 0, 0)),
        pl.BlockSpec((FOX_HEADS, tm), lambda i: (0, i)),
    )
    return pl.pallas_call(
        _inproj_prompt_kernel,
        grid=(S // tm,),
        in_specs=in_specs,
        out_specs=out_specs,
        out_shape=out_shape,
        scratch_shapes=[pltpu.VMEM((1, LANES), F32)],
        compiler_params=pltpu.CompilerParams(dimension_semantics=("arbitrary",),
                                             vmem_limit_bytes=VMEM_LIMIT),
        name="inproj_prompt",
    )(x, g, wq, wk, wvf, bf, wb, cos2, sin2, tri, eq, ek, oneq, onek, onev)


def _inproj_sample_kernel(x_ref, g_ref, wqkv_ref, wf_ref, bf_ref, wb_ref, cos_ref, sin_ref,
                          q_ref, k32_ref, v32_ref, logf_ref, qb_ref, kb_ref, vb_ref):
    h = _rmsnorm(x_ref[...], g_ref[...]).astype(BF16)
    logf = _log_sigmoid(_dot_t(h, wf_ref[...]) + bf_ref[...])
    logf_ref[...] = logf[:, :FOX_HEADS]
    z = _dot_t(h, wqkv_ref[...])
    q_ref[...] = (z[:, :FOX_W] * (FOX_HEAD_DIM ** -0.5 * LOG2E)).astype(BF16)
    k32_ref[...] = z[:, FOX_W:2 * FOX_W]
    v32_ref[...] = z[:, 2 * FOX_W:]
    zb = _dot_t(h, wb_ref[...])
    cos2 = cos_ref[...]
    sin2 = sin_ref[...]
    for hh in range(RET_HEADS):
        sl = slice(hh * RET_KEY_DIM, (hh + 1) * RET_KEY_DIM)
        qb_ref[:, sl] = _rotary(zb[:, sl], cos2, sin2).astype(BF16)
        xk = zb[:, RET_KW + hh * RET_KEY_DIM:RET_KW + (hh + 1) * RET_KEY_DIM]
        kb_ref[:, sl] = (_rotary(xk, cos2, sin2) * (RET_KEY_DIM ** -0.5)).astype(BF16)
    vb_ref[...] = zb[:, 2 * RET_KW:].astype(BF16)


def _inproj_sample(x, g, wqkv, wf, bf, wb, cos2, sin2):
    M = x.shape[0]
    args = (x, g, wqkv, wf, bf, wb, cos2, sin2)
    out_shape = (
        jax.ShapeDtypeStruct((M, FOX_W), BF16),
        jax.ShapeDtypeStruct((M, FOX_W), F32),
        jax.ShapeDtypeStruct((M, FOX_W), F32),
        jax.ShapeDtypeStruct((M, FOX_HEADS), F32),
        jax.ShapeDtypeStruct((M, RET_KW), BF16),
        jax.ShapeDtypeStruct((M, RET_KW), BF16),
        jax.ShapeDtypeStruct((M, RET_VW), BF16),
    )
    return pl.pallas_call(
        _inproj_sample_kernel,
        grid=(1,),
        in_specs=[_const_spec(a.shape) for a in args],
        out_specs=tuple(_const_spec(o.shape) for o in out_shape),
        out_shape=out_shape,
        compiler_params=pltpu.CompilerParams(dimension_semantics=("arbitrary",),
                                             vmem_limit_bytes=VMEM_LIMIT),
        name="inproj_sample",
    )(*args)


def _fox_prompt_kernel(it_ref, jt_ref, jfetch_ref, mode_ref, qT_ref, ka_ref, vT_ref,
                       sqq_ref, o_ref, m_ref, acc_ref, *, n_strips):
    del jfetch_ref
    t = pl.program_id(0)
    i = it_ref[t]
    j = jt_ref[t]
    T = qT_ref.shape[2]
    SUB = T // n_strips
    EXACT = n_strips + 1

    def heads(fn):
        def body(hh, carry):
            fn(hh, mode_ref[t * FOX_HEADS + hh])
            return carry
        lax.fori_loop(0, FOX_HEADS, body, 0)

    def scores(hh):
        return _dot(ka_ref[hh], qT_ref[hh])

    def exact_head(hh):
        s = scores(hh)
        kk = lax.broadcasted_iota(jnp.int32, s.shape, 0) + j * T
        qq = lax.broadcasted_iota(jnp.int32, s.shape, 1) + i * T
        s = jnp.where(kk > qq, NEG, s)
        m_old = m_ref[hh]
        m_new = jnp.maximum(m_old, jnp.max(s, axis=0, keepdims=True))
        p = jnp.exp2(s - m_new).astype(BF16)
        alpha = jnp.exp2(m_old - m_new)
        acc_ref[hh] = alpha * acc_ref[hh] + _dot(vT_ref[hh], p)
        m_ref[hh] = m_new


    def diag_head(hh, mode):
        m_ref[hh] = sqq_ref[pl.ds(hh, 1), :]

        @pl.when(mode != EXACT)
        def _():
            s = scores(hh)
            kk = lax.broadcasted_iota(jnp.int32, s.shape, 0)
            qq = lax.broadcasted_iota(jnp.int32, s.shape, 1)
            p = jnp.exp2(jnp.where(kk > qq, NEG, s) - m_ref[hh]).astype(BF16)
            acc_ref[hh] = _dot(vT_ref[hh], p)

        @pl.when(mode == EXACT)
        def _():
            acc_ref[hh] = jnp.zeros((V_PAD, T), F32)
            exact_head(hh)

    def off_head(hh, mode):
        for nn in range(1, n_strips + 1):
            k0 = (n_strips - nn) * SUB

            @pl.when(mode == nn)
            def _():
                p = jnp.exp2(_dot(ka_ref[hh, k0:, :], qT_ref[hh]) - m_ref[hh]).astype(BF16)
                acc_ref[hh] += _dot(vT_ref[hh, :, k0:], p)

        @pl.when(mode == EXACT)
        def _():
            exact_head(hh)

    @pl.when(j == i)
    def _():
        heads(diag_head)

    @pl.when(j < i)
    def _():
        heads(off_head)

    @pl.when(j == 0)
    def _():
        for pr in range(FOX_HEADS // 2):
            halves = []
            for hh in (2 * pr, 2 * pr + 1):
                a = acc_ref[hh]
                halves.append(a[:FOX_HEAD_DIM] / a[FOX_HEAD_DIM:FOX_HEAD_DIM + 1])
            o_ref[pr] = jnp.concatenate(halves, axis=0).T.astype(BF16)


def _prune_tables(stats, it, jt, nb):
    per = stats.shape[0] // nb
    st = stats.reshape(nb, per, 8, LANES)[:, :, :, :FOX_HEADS]
    nq = jnp.sqrt(jnp.max(st[:, :, 0, :], axis=1)) * NORM_SLACK
    nk_strip = jnp.sqrt(st[:, :, 1, :]) * NORM_SLACK
    nk = jnp.max(nk_strip, axis=1)
    c_first = st[:, 0, 2, :]
    c_last = st[:, :, 3, :]
    bound = (nq[it][:, None, :] * (nk_strip[jt] + nk[it][:, None, :])
             - (c_last[jt] - c_first[it][:, None, :]))
    live = jnp.logical_or(jnp.asarray(jt == it)[:, None, None],
                          jnp.logical_not(bound < PRUNE_LOG2))
    strip_no = jnp.arange(per, dtype=jnp.int32)[None, :, None]
    n_keep = per - jnp.min(jnp.where(live, strip_no, per), axis=1)
    safe = nq[it] * (nk[jt] + nk[it]) < STALE_SAFE_LOG2
    mode = jnp.where(n_keep == 0, 0, jnp.where(safe, n_keep, per + 1))
    steps = jnp.arange(len(it), dtype=jnp.int32)
    last_live = lax.cummax(jnp.where(jnp.any(n_keep > 0, axis=1), steps, 0))
    return jnp.asarray(jt)[last_live], mode.astype(jnp.int32).reshape(-1), per


def _fox_prompt(qT, ka, vT, stats, sqq, T):
    S = ka.shape[1]
    nb = S // T
    it = np.array([i for i in range(nb) for _ in range(i + 1)], np.int32)
    jt = np.array([j for i in range(nb) for j in range(i, -1, -1)], np.int32)
    jfetch, mode, n_strips = _prune_tables(stats, it, jt, nb)
    grid_spec = pltpu.PrefetchScalarGridSpec(
        num_scalar_prefetch=4,
        grid=(len(it),),
        in_specs=[
            pl.BlockSpec((FOX_HEADS, HEAD_PAD, T), lambda t, it, jt, jf, md: (0, 0, it[t])),
            pl.BlockSpec((FOX_HEADS, T, HEAD_PAD), lambda t, it, jt, jf, md: (0, jf[t], 0)),
            pl.BlockSpec((FOX_HEADS, V_PAD, T), lambda t, it, jt, jf, md: (0, 0, jf[t])),
            pl.BlockSpec((FOX_HEADS, T), lambda t, it, jt, jf, md: (0, it[t])),
        ],
        out_specs=pl.BlockSpec((FOX_HEADS // 2, T, LANES),
                               lambda t, it, jt, jf, md: (0, it[t], 0)),
        scratch_shapes=[pltpu.VMEM((FOX_HEADS, 1, T), F32),
                        pltpu.VMEM((FOX_HEADS, V_PAD, T), F32)],
    )
    return pl.pallas_call(
        functools.partial(_fox_prompt_kernel, n_strips=n_strips),
        grid_spec=grid_spec,
        out_shape=jax.ShapeDtypeStruct((FOX_HEADS // 2, S, LANES), BF16),
        compiler_params=pltpu.CompilerParams(dimension_semantics=("arbitrary",),
                                             vmem_limit_bytes=VMEM_LIMIT),
        name="fox_prompt",
    )(jnp.asarray(it), jnp.asarray(jt), jfetch, mode, qT, ka, vT, sqq)


def _fox_sample_kernel(q_ref, kn_ref, vn_ref, ckT_ref, cvT_ref, lfT_ref, up_ref, ex_ref, o_ref):
    P = ckT_ref.shape[3]
    Tn = q_ref.shape[0]
    KP = lfT_ref.shape[2]
    HQ = FOX_HEADS * Tn
    nchunk = KP // LANES
    nt = (((1,), (1,)), ((), ()))

    def stack3(x):
        parts3 = [t.astype(F32) for t in _split3(x)] + [jnp.zeros_like(x)]
        return jnp.concatenate(parts3, axis=0).astype(BF16)

    x3 = stack3(lfT_ref[0])
    up = up_ref[...]
    run = jnp.zeros((4 * FOX_HEADS, 1), F32)
    parts = []
    for cidx in range(nchunk):
        y = _dot(x3[:, cidx * LANES:(cidx + 1) * LANES], up) + run
        parts.append(y)
        run = y[:, LANES - 1:LANES]
    y = jnp.concatenate(parts, axis=1)
    cT = (y[:FOX_HEADS] + y[FOX_HEADS:2 * FOX_HEADS] + y[2 * FOX_HEADS:3 * FOX_HEADS]) * LOG2E
    ckx = _dot(ex_ref[...], stack3(cT))

    tail = ckx[:, P:P + LANES]
    rowq = lax.broadcasted_iota(jnp.int32, tail.shape, 0) % Tn
    lanek = lax.broadcasted_iota(jnp.int32, tail.shape, 1)
    cq = jnp.sum(jnp.where(lanek == rowq, tail, 0.0), axis=1, keepdims=True)

    q = q_ref[...]
    qt = jnp.concatenate([q] * FOX_HEADS, axis=0)
    rh = lax.broadcasted_iota(jnp.int32, qt.shape, 0) // Tn
    lh = lax.broadcasted_iota(jnp.int32, qt.shape, 1) // FOX_HEAD_DIM
    qbd = jnp.where(rh == lh, qt, jnp.zeros_like(qt))

    kT = ckT_ref[0].reshape(FOX_W, P).astype(BF16)
    vT = cvT_ref[0].reshape(FOX_W, P).astype(BF16)
    s_c = _dot(qbd, kT) + cq - ckx[:, :P]
    s_n = lax.dot_general(qbd, kn_ref[...].astype(BF16), nt, preferred_element_type=F32)
    s_n = s_n + cq - ckx[:, P:P + Tn]
    key = lax.broadcasted_iota(jnp.int32, s_n.shape, 1)
    qrow = lax.broadcasted_iota(jnp.int32, s_n.shape, 0) % Tn
    s_n = jnp.where(key > qrow, NEG, s_n)
    m = jnp.maximum(jnp.max(s_c, axis=1, keepdims=True), jnp.max(s_n, axis=1, keepdims=True))
    p_c = jnp.exp2(s_c - m)
    p_n = jnp.exp2(s_n - m)
    l = jnp.sum(p_c, axis=1, keepdims=True) + jnp.sum(p_n, axis=1, keepdims=True)
    z = lax.dot_general(p_c.astype(BF16), vT, nt, preferred_element_type=F32)
    z = (z + _dot(p_n.astype(BF16), vn_ref[...].astype(BF16))) / l
    zh = lax.broadcasted_iota(jnp.int32, (Tn, FOX_W), 1) // FOX_HEAD_DIM
    o = jnp.zeros((Tn, FOX_W), F32)
    for hh in range(FOX_HEADS):
        o = o + jnp.where(zh == hh, z[hh * Tn:(hh + 1) * Tn, :], 0.0)
    o_ref[...] = o.astype(BF16)


def _fox_sample(q, kn, vn, cache_kT, cache_vT, lfT, B, Tn):
    P = cache_kT.shape[3]
    KP = lfT.shape[2]
    HQ = FOX_HEADS * Tn
    up = jnp.asarray(np.triu(np.ones((LANES, LANES), np.float32)), BF16)
    ex = np.zeros((HQ, 4 * FOX_HEADS), np.float32)
    for part in range(3):
        for hh in range(FOX_HEADS):
            ex[hh * Tn:(hh + 1) * Tn, part * FOX_HEADS + hh] = 1.0
    ex = jnp.asarray(ex, BF16)
    rowb = lambda w: pl.BlockSpec((Tn, w), lambda b: (b, 0))
    return pl.pallas_call(
        _fox_sample_kernel,
        grid=(B,),
        in_specs=[rowb(FOX_W), rowb(FOX_W), rowb(FOX_W),
                  pl.BlockSpec((1, FOX_HEADS, FOX_HEAD_DIM, P), lambda b: (b, 0, 0, 0)),
                  pl.BlockSpec((1, FOX_HEADS, FOX_HEAD_DIM, P), lambda b: (b, 0, 0, 0)),
                  pl.BlockSpec((1, FOX_HEADS, KP), lambda b: (b, 0, 0)),
                  _const_spec(up.shape), _const_spec(ex.shape)],
        out_specs=rowb(FOX_W),
        out_shape=jax.ShapeDtypeStruct((B * Tn, FOX_W), BF16),
        compiler_params=pltpu.CompilerParams(dimension_semantics=("arbitrary",),
                                             vmem_limit_bytes=VMEM_LIMIT),
        name="fox_sample",
    )(q, kn, vn, cache_kT, cache_vT, lfT, up, ex)


def _retention_kernel(q_ref, k_ref, v_ref, s0_ref, dmat_ref, xi_ref, zeta_ref, gam_ref,
                      n_ref, sout_ref, st_ref):
    c = pl.program_id(1)

    @pl.when(c == 0)
    def _():
        st_ref[...] = s0_ref[0]

    for hh in range(RET_HEADS):
        q = q_ref[:, hh * RET_KEY_DIM:(hh + 1) * RET_KEY_DIM]
        k = k_ref[:, hh * RET_KEY_DIM:(hh + 1) * RET_KEY_DIM]
        v = v_ref[:, hh * RET_VAL_DIM:(hh + 1) * RET_VAL_DIM]
        st = st_ref[hh]
        sc = lax.dot_general(q, k, (((1,), (1,)), ((), ())), preferred_element_type=F32)
        sc = sc * dmat_ref[hh]
        o = _dot(sc.astype(BF16), v) + _dot(q, st.astype(BF16)) * xi_ref[hh]
        kz = (k.astype(F32) * zeta_ref[hh]).astype(BF16)
        upd = lax.dot_general(kz, v, (((0,), (0,)), ((), ())), preferred_element_type=F32)
        st_ref[hh] = gam_ref[hh] * st + upd
        mu = jnp.mean(o, axis=-1, keepdims=True)
        d = o - mu
        var = jnp.mean(d * d, axis=-1, keepdims=True)
        n_ref[:, hh * RET_VAL_DIM:(hh + 1) * RET_VAL_DIM] = (d * lax.rsqrt(var + EPS)).astype(BF16)

    @pl.when(c == pl.num_programs(1) - 1)
    def _():
        sout_ref[0] = st_ref[...]


def _ret_log_gamma():
    return jnp.log(1.0 - jnp.exp2(-5.0 - jnp.arange(RET_HEADS, dtype=F32)))


def _retention(q, k, v, state0, B, L, C):
    nc = L // C
    lg = _ret_log_gamma()
    idx = jnp.arange(C, dtype=F32)
    diff = idx[:, None] - idx[None, :]
    dmat = jnp.where(diff[None] >= 0, jnp.exp(jnp.maximum(diff, 0.0)[None] * lg[:, None, None]), 0.0)
    xi = jnp.exp((idx[None, :] + 1.0) * lg[:, None])
    zeta = jnp.exp((C - 1.0 - idx[None, :]) * lg[:, None])
    xi = jnp.broadcast_to(xi[:, :, None], (RET_HEADS, C, RET_VAL_DIM))
    zeta = jnp.broadcast_to(zeta[:, :, None], (RET_HEADS, C, RET_KEY_DIM))
    gam = jnp.broadcast_to(jnp.exp(C * lg)[:, None, None], (RET_HEADS, 1, RET_VAL_DIM))
    rowc = lambda w: pl.BlockSpec((C, w), lambda b, c: (b * nc + c, 0))
    st_spec = pl.BlockSpec((1, RET_HEADS, RET_KEY_DIM, RET_VAL_DIM), lambda b, c: (b, 0, 0, 0))
    return pl.pallas_call(
        _retention_kernel,
        grid=(B, nc),
        in_specs=[rowc(RET_KW), rowc(RET_KW), rowc(RET_VW), st_spec,
                  _const_spec(dmat.shape), _const_spec(xi.shape), _const_spec(zeta.shape),
                  _const_spec(gam.shape)],
        out_specs=(rowc(RET_VW), st_spec),
        out_shape=(jax.ShapeDtypeStruct((B * L, RET_VW), BF16),
                   jax.ShapeDtypeStruct((B, RET_HEADS, RET_KEY_DIM, RET_VAL_DIM), F32)),
        scratch_shapes=[pltpu.VMEM((RET_HEADS, RET_KEY_DIM, RET_VAL_DIM), F32)],
        compiler_params=pltpu.CompilerParams(dimension_semantics=("arbitrary", "arbitrary"),
                                             vmem_limit_bytes=VMEM_LIMIT),
        name="retention",
    )(q, k, v, state0, dmat, xi, zeta, gam)


def _mixer_ffn_kernel(x_ref, oa_ref, nb_ref, prev_ref, gmix_ref, wg_ref, gng_ref, wpa_ref, wpb_ref,
                      wo_ref, gffn_ref, wup_ref, cw_ref, cb_ref, wdn_ref, gfin_ref,
                      y_ref, conv_ref, carry_ref, ua_ref, ub_ref, acc_ref, h2_ref,
                      *, nseg, seglen):
    i = pl.program_id(0)
    NC = N_FFN_CHUNKS
    PADR = 8
    H0 = PADR - (CONV_WIDTH - 1)

    @pl.when(i == 0)
    def _():
        carry_ref[...] = prev_ref[...]

    x = x_ref[...]
    h = _rmsnorm(x, gmix_ref[...]).astype(BF16)
    zg = _dot_t(h, wg_ref[...])
    gb = zg[:, :RET_VW]
    gma = zg[:, RET_VW:RET_VW + D_MODEL]
    gmb = zg[:, RET_VW + D_MODEL:]
    oa = jnp.concatenate([oa_ref[p] for p in range(FOX_HEADS // 2)], axis=1)
    ya = _dot(oa, wpa_ref[...])
    nn = nb_ref[...].astype(F32) * gng_ref[...] * (gb * jax.nn.sigmoid(gb))
    yb = _dot(nn.astype(BF16), wpb_ref[...])
    y = jax.nn.sigmoid(gma) * ya + jax.nn.sigmoid(gmb) * yb
    x1 = x + _dot(y.astype(BF16), wo_ref[...])
    h2_ref[...] = _rmsnorm(x1, gffn_ref[...]).astype(BF16)
    acc_ref[...] = x1

    h2 = h2_ref[...]

    def up_half(u_ref, cidx, slot):
        u = _dot(h2, wup_ref[cidx])
        for s in range(nseg):
            u_ref[slot, s, PADR:PADR + seglen, :] = u[s * seglen:(s + 1) * seglen, :]
            u_ref[slot, s, H0:PADR, :] = carry_ref[cidx, s, H0:PADR, :]
            carry_ref[cidx, s, H0:PADR, :] = u[(s + 1) * seglen - (CONV_WIDTH - 1):(s + 1) * seglen, :]

    def conv_half(u_ref, cidx, slot):
        w = cw_ref[cidx]
        b = cb_ref[cidx]
        outs = []
        for s in range(nseg):
            acc = w[0:1] * u_ref[slot, s, H0:H0 + seglen, :]
            for jj in range(1, CONV_WIDTH):
                acc = acc + w[jj:jj + 1] * u_ref[slot, s, H0 + jj:H0 + jj + seglen, :]
            outs.append(b + acc)
        return outs[0] if nseg == 1 else jnp.concatenate(outs, axis=0)

    def stage_up(c):
        up_half(ua_ref, c, c % 2)
        up_half(ub_ref, NC + c, c % 2)

    stage_up(0)
    group = []
    for c in range(NC):
        if c + 1 < NC:
            stage_up(c + 1)
        a = conv_half(ua_ref, c, c % 2)
        b = conv_half(ub_ref, NC + c, c % 2)
        group.append((_gelu_tanh(a) * b).astype(BF16))
        if len(group) == DOWN_GROUP or c + 1 == NC:
            c0 = c + 1 - len(group)
            wd = wdn_ref[c0:c + 1].reshape(len(group) * FFN_CHUNK, D_MODEL)
            acc_ref[...] += _dot(jnp.concatenate(group, axis=1), wd)
            group = []
    y_ref[...] = _rmsnorm(acc_ref[...], gfin_ref[...])

    @pl.when(i == pl.num_programs(0) - 1)
    def _():
        conv_ref[...] = carry_ref[:, :, H0:PADR, :]


def _mixer_ffn(x, oa, nb, prev, weights, tm, nseg, seglen):
    M = x.shape[0]
    gmix, wg, gng, wpa, wpb, wo, gffn, wup, cw, cb, wdn, gfin = weights
    FC = FFN_CHUNK
    NC = N_FFN_CHUNKS
    row = lambda w: pl.BlockSpec((tm, w), lambda i: (i, 0))
    wspec = lambda a: pl.BlockSpec(a.shape, lambda i, n=a.ndim: (0,) * n,
                                   pipeline_mode=pl.Buffered(1))
    in_specs = [row(D_MODEL),
                pl.BlockSpec((FOX_HEADS // 2, tm, LANES), lambda i: (0, i, 0)),
                row(RET_VW), wspec(prev)] + [wspec(w) for w in weights]
    out_shape = (jax.ShapeDtypeStruct((M, D_MODEL), F32),
                 jax.ShapeDtypeStruct((2 * NC, nseg, CONV_WIDTH - 1, FC), F32))
    out_specs = (row(D_MODEL), _const_spec(out_shape[1].shape))
    return pl.pallas_call(
        functools.partial(_mixer_ffn_kernel, nseg=nseg, seglen=seglen),
        grid=(M // tm,),
        in_specs=in_specs,
        out_specs=out_specs,
        out_shape=out_shape,
        scratch_shapes=[pltpu.VMEM((2 * NC, nseg, 8, FC), F32),
                        pltpu.VMEM((2, nseg, 8 + seglen, FC), F32),
                        pltpu.VMEM((2, nseg, 8 + seglen, FC), F32),
                        pltpu.VMEM((tm, D_MODEL), F32),
                        pltpu.VMEM((tm, D_MODEL), BF16)],
        compiler_params=pltpu.CompilerParams(dimension_semantics=("arbitrary",),
                                             vmem_limit_bytes=VMEM_LIMIT),
        name="mixer_ffn",
    )(x, oa, nb, prev, *weights)


def _rotary_tables(start, n):
    half = RET_KEY_DIM // 2
    inv = 1.0 / (ROPE_BASE ** jnp.linspace(0.0, 1.0, half, dtype=F32))
    fine = min(n, ROT_FINE)
    assert n % fine == 0
    a_hi = (start + fine * jnp.arange(n // fine)).astype(F32)[:, None] * inv[None, :]
    a_lo = jnp.arange(fine).astype(F32)[:, None] * inv[None, :]
    ch, sh = jnp.cos(a_hi)[:, None, :], jnp.sin(a_hi)[:, None, :]
    cl, sl = jnp.cos(a_lo)[None, :, :], jnp.sin(a_lo)[None, :, :]
    cos = (ch * cl - sh * sl).reshape(n, half)
    sin = (sh * cl + ch * sl).reshape(n, half)
    return jnp.concatenate([cos, cos], axis=1), jnp.concatenate([-sin, sin], axis=1)


def _pad_heads(wt, pad):
    d = wt.shape[1]
    wt = wt.reshape(FOX_HEADS, FOX_HEAD_DIM, d)
    wt = jnp.pad(wt, ((0, 0), (0, pad - FOX_HEAD_DIM), (0, 0)))
    return wt.reshape(FOX_HEADS * pad, d)


def _prompt_consts(tm):
    tri = np.tril(np.ones((tm, tm), np.float32))
    eq = np.zeros((LANES, FOX_HEADS * HEAD_PAD), np.float32)
    ek = np.zeros((LANES, FOX_HEADS * HEAD_PAD), np.float32)
    oneq = np.zeros((1, FOX_HEADS * HEAD_PAD), np.float32)
    onek = np.zeros((1, FOX_HEADS * HEAD_PAD), np.float32)
    onev = np.zeros((1, FOX_HEADS * V_PAD), np.float32)
    for hh in range(FOX_HEADS):
        base = hh * HEAD_PAD + BIAS_COL
        for part in range(3):
            eq[part * FOX_HEADS + hh, base + part] = 1.0
            ek[part * FOX_HEADS + hh, base + 3 + part] = -1.0
            onek[0, base + part] = 1.0
            oneq[0, base + 3 + part] = 1.0
        onev[0, hh * V_PAD + FOX_HEAD_DIM] = 1.0
    return (jnp.asarray(tri, BF16), jnp.asarray(eq, BF16), jnp.asarray(ek, BF16),
            jnp.asarray(oneq), jnp.asarray(onek), jnp.asarray(onev))


def _chunk_cols(a):
    lead = a.shape[:-1]
    a = a.reshape(lead + (2 * N_FFN_CHUNKS, FFN_CHUNK))
    return jnp.moveaxis(a, -2, 0)


def _tile(n, pref):
    t = min(n, pref)
    while n % t:
        t //= 2
    return t


def kernel(x_prompt, x_sample, cache_fox_k, cache_fox_v, cache_fox_logf, state_ret, state_ffn_conv,
           norm_mix_g, w_in, b_fox_f, gn_ret_g, w_pa, w_pb, w_o, norm_ffn_g, w_up, conv_w, conv_b,
           w_down, norm_final_g):
    depth = w_in.shape[0]
    Bp, S, _ = x_prompt.shape
    Bs, Ts, _ = x_sample.shape
    P = cache_fox_k.shape[2]
    assert depth == 1 and Bp == 1, "kernel handles the single-layer, single-prompt configuration"
    l = 0

    wt = jnp.swapaxes(w_in[l], 0, 1).astype(BF16)
    o0 = 3 * FOX_W
    o1 = o0 + FOX_HEADS
    o2 = o1 + 2 * RET_KW + RET_VW
    wq_aug = _pad_heads(wt[:FOX_W], HEAD_PAD)
    wk_aug = _pad_heads(wt[FOX_W:2 * FOX_W], HEAD_PAD)
    wf = jnp.pad(wt[o0:o1], ((0, LANES - FOX_HEADS), (0, 0)))
    wvf = jnp.concatenate([_pad_heads(wt[2 * FOX_W:o0], V_PAD), wf], axis=0)
    wqkv = wt[:o0]
    bf = jnp.pad(b_fox_f[l].astype(F32), (0, LANES - FOX_HEADS))[None, :]
    wb = wt[o1:o2]
    wg = wt[o2:]
    gmix = norm_mix_g[l].astype(F32)[None, :]
    mix_weights = (
        gmix, wg, gn_ret_g[l].astype(F32)[None, :], w_pa[l].astype(BF16), w_pb[l].astype(BF16),
        w_o[l].astype(BF16), norm_ffn_g[l].astype(F32)[None, :],
        _chunk_cols(w_up[l]).astype(BF16),
        jnp.pad(_chunk_cols(conv_w[l].astype(F32)), ((0, 0), (0, 8 - CONV_WIDTH), (0, 0))),
        _chunk_cols(conv_b[l].astype(F32)[None, :]),
        w_down[l].reshape(N_FFN_CHUNKS, FFN_CHUNK, D_MODEL).astype(BF16),
        norm_final_g.astype(F32)[None, :],
    )

    tm_a = _tile(S, 256)
    cos_p, sin_p = _rotary_tables(0, S)
    (qT, ka, vT, kT_p, vT_p, logfT_p, qb, kb, vb, stats, sqq) = _inproj_prompt(
        x_prompt[0], gmix, wq_aug, wk_aug, wvf, bf, wb, cos_p, sin_p,
        _prompt_consts(tm_a), tm_a)
    oa_p = _fox_prompt(qT, ka, vT, stats, sqq, _tile(S, 1024))
    zero_state = jnp.zeros((1, RET_HEADS, RET_KEY_DIM, RET_VAL_DIM), F32)
    nb_p, ret_p = _retention(qb, kb, vb, zero_state, 1, S, _tile(S, 512))
    tm_d = _tile(S, 256)
    zero_prev = jnp.zeros((2 * N_FFN_CHUNKS, 1, 8, FFN_CHUNK), F32)
    y_p, conv_p = _mixer_ffn(x_prompt[0], oa_p, nb_p, zero_prev, mix_weights, tm_d, 1, tm_d)

    Ms = Bs * Ts
    cos_s, sin_s = _rotary_tables(P, Ts)
    cos_s = jnp.tile(cos_s, (Bs, 1))
    sin_s = jnp.tile(sin_s, (Bs, 1))
    (q_s, k_s, v_s, logf_s, qb_s, kb_s, vb_s) = _inproj_sample(
        x_sample.reshape(Ms, D_MODEL), gmix, wqkv, wf, bf, wb, cos_s, sin_s)
    KP = ((P + Ts + LANES - 1) // LANES) * LANES
    lf_all = jnp.concatenate([cache_fox_logf[l].astype(F32), logf_s.reshape(Bs, Ts, FOX_HEADS)], axis=1)
    lfT = jnp.pad(jnp.swapaxes(lf_all, 1, 2), ((0, 0), (0, 0), (0, KP - P - Ts)))
    oa_s = _fox_sample(q_s, k_s, v_s, jnp.transpose(cache_fox_k[l], (0, 2, 3, 1)),
                       jnp.transpose(cache_fox_v[l], (0, 2, 3, 1)), lfT, Bs, Ts)
    oa_s = jnp.moveaxis(oa_s.reshape(Ms, FOX_HEADS // 2, LANES), 1, 0)
    nb_s, ret_s = _retention(qb_s, kb_s, vb_s, state_ret[l].astype(F32), Bs, Ts, Ts)
    prev_s = _chunk_cols(state_ffn_conv[l].astype(F32))
    prev_s = jnp.pad(prev_s, ((0, 0), (0, 0), (8 - (CONV_WIDTH - 1), 0), (0, 0)))
    y_s, conv_s = _mixer_ffn(x_sample.reshape(Ms, D_MODEL), oa_s, nb_s, prev_s, mix_weights,
                             Ms, Bs, Ts)

    def unchunk(cv):
        return jnp.moveaxis(cv, 0, 2).reshape(cv.shape[1], CONV_WIDTH - 1, 2 * FFN_DIM)

    hshape = (FOX_HEADS, FOX_HEAD_DIM)
    return (
        y_p[None],
        y_s.reshape(Bs, Ts, D_MODEL),
        jnp.transpose(kT_p, (2, 0, 1))[None, None],
        jnp.transpose(vT_p, (2, 0, 1))[None, None],
        jnp.transpose(logfT_p, (1, 0))[None, None],
        ret_p[None],
        unchunk(conv_p)[None],
        k_s.reshape((1, Bs, Ts) + hshape),
        v_s.reshape((1, Bs, Ts) + hshape),
        logf_s.reshape(1, Bs, Ts, FOX_HEADS),
        ret_s[None],
        unchunk(conv_s)[None],
    )
```

```python
import functools
import math

import numpy as np
import jax
import jax.numpy as jnp
from jax import lax
from jax.experimental import pallas as pl
from jax.experimental.pallas import tpu as pltpu

F32 = jnp.float32
BF16 = jnp.bfloat16

D_MODEL = 1024
FOX_HEADS = 8
FOX_HEAD_DIM = 64
RET_HEADS = 4
RET_KEY_DIM = 128
RET_VAL_DIM = 256
FFN_DIM = 2816
CONV_WIDTH = 3
EPS = 1e-6
ROPE_BASE = 10000.0

FOX_W = FOX_HEADS * FOX_HEAD_DIM
RET_KW = RET_HEADS * RET_KEY_DIM
RET_VW = RET_HEADS * RET_VAL_DIM

LOG2E = 1.4426950408889634
LANES = 128
HEAD_PAD = LANES
V_PAD = 80
BIAS_COL = FOX_HEAD_DIM
NEG = -1e30
STALE_SAFE_LOG2 = 64.0
PRUNE_LOG2 = -160.0
NORM_SLACK = 1.02
ROT_FINE = 128
FFN_CHUNK = 256
N_FFN_CHUNKS = FFN_DIM // FFN_CHUNK
DOWN_GROUP = 4
VMEM_LIMIT = 56 * 1024 * 1024


def _rmsnorm(x, g):
    ms = jnp.mean(x * x, axis=-1, keepdims=True)
    return x * lax.rsqrt(ms + EPS) * g


def _split3(x):
    hi = x.astype(BF16)
    r1 = x - hi.astype(F32)
    mid = r1.astype(BF16)
    lo = (r1 - mid.astype(F32)).astype(BF16)
    return hi, mid, lo


def _log_sigmoid(x):
    return jnp.minimum(x, 0.0) - jnp.log1p(jnp.exp(-jnp.abs(x)))


def _dot(a, b):
    return jnp.dot(a, b, preferred_element_type=F32)


def _dot_t(a, bt):
    return lax.dot_general(a, bt, (((1,), (1,)), ((), ())), preferred_element_type=F32)


def _gelu_tanh(x):
    c0 = math.sqrt(2.0 / math.pi)
    hx = 0.5 * x
    return hx + hx * jnp.tanh(x * (c0 + (c0 * 0.044715) * (x * x)))


def _rotary(x, cos2, sin2):
    return x * cos2 + pltpu.roll(x, RET_KEY_DIM // 2, 1) * sin2


def _const_spec(shape):
    n = len(shape)
    return pl.BlockSpec(shape, lambda *_: (0,) * n)


def _inproj_prompt_kernel(x_ref, g_ref, wq_ref, wk_ref, wvf_ref, bf_ref, wb_ref,
                          cos_ref, sin_ref, tri_ref, eq_ref, ek_ref, oneq_ref, onek_ref, onev_ref,
                          qT_ref, ka_ref, vT_ref, kT32_ref, vT32_ref, logf_ref, qb_ref, kb_ref, vb_ref,
                          stats_ref, sqq_ref, carry_ref):
    tm = x_ref.shape[0]
    VW = FOX_HEADS * V_PAD

    @pl.when(pl.program_id(0) == 0)
    def _():
        carry_ref[...] = jnp.zeros_like(carry_ref)

    h = _rmsnorm(x_ref[...], g_ref[...]).astype(BF16)
    zvf = _dot_t(h, wvf_ref[...])

    logf = _log_sigmoid(zvf[:, VW:] + bf_ref[...])
    logf_ref[...] = logf.T[:FOX_HEADS, :]
    lane = lax.broadcasted_iota(jnp.int32, logf.shape, 1)
    logf = jnp.where(lane < FOX_HEADS, logf, 0.0)
    r = _dot(tri_ref[...], jnp.concatenate(_split3(logf), axis=1))
    c = r[:, :LANES] + r[:, LANES:2 * LANES] + r[:, 2 * LANES:] + carry_ref[...]
    carry_ref[...] = c[tm - 1:tm, :]
    c2 = c * LOG2E
    hi, mid, lo = (t.astype(F32) for t in _split3(c2))
    c3 = (hi + pltpu.roll(mid, FOX_HEADS, 1) + pltpu.roll(lo, 2 * FOX_HEADS, 1)).astype(BF16)

    q_aug = (_dot_t(h, wq_ref[...]) * (FOX_HEAD_DIM ** -0.5 * LOG2E)
             + _dot(c3, eq_ref[...]) + oneq_ref[...])
    qT32 = q_aug.T
    k_aug = _dot_t(h, wk_ref[...]) + _dot(c3, ek_ref[...]) + onek_ref[...]
    kT32 = k_aug.T
    vT32 = (zvf[:, :VW] + onev_ref[...]).T
    qT = qT32.astype(BF16)
    k_aug = k_aug.astype(BF16)
    vT = vT32.astype(BF16)

    lane1 = lax.broadcasted_iota(jnp.int32, (1, LANES), 1)
    nq2 = jnp.zeros((1, LANES), F32)
    nk2 = jnp.zeros((1, LANES), F32)
    for hh in range(FOX_HEADS):
        qT_ref[hh] = qT[hh * HEAD_PAD:(hh + 1) * HEAD_PAD, :]
        ka_ref[hh] = k_aug[:, hh * HEAD_PAD:(hh + 1) * HEAD_PAD]
        vT_ref[hh] = vT[hh * V_PAD:(hh + 1) * V_PAD, :]
        qh = qT32[hh * HEAD_PAD:hh * HEAD_PAD + FOX_HEAD_DIM, :]
        kh = kT32[hh * HEAD_PAD:hh * HEAD_PAD + FOX_HEAD_DIM, :]
        kT32_ref[hh] = kh
        vT32_ref[hh] = vT32[hh * V_PAD:hh * V_PAD + FOX_HEAD_DIM, :]
        sqq_ref[hh:hh + 1, :] = jnp.sum(qh * kh, axis=0, keepdims=True)
        q2 = jnp.max(jnp.sum(qh * qh, axis=0, keepdims=True), axis=1, keepdims=True)
        k2 = jnp.max(jnp.sum(kh * kh, axis=0, keepdims=True), axis=1, keepdims=True)
        nq2 = jnp.where(lane1 == hh, q2, nq2)
        nk2 = jnp.where(lane1 == hh, k2, nk2)
    stats_ref[0] = jnp.concatenate(
        [nq2, nk2, c2[0:1, :], c2[tm - 1:tm, :], jnp.zeros((4, LANES), F32)], axis=0)

    zb = _dot_t(h, wb_ref[...])
    cos2 = cos_ref[...]
    sin2 = sin_ref[...]
    for hh in range(RET_HEADS):
        sl = slice(hh * RET_KEY_DIM, (hh + 1) * RET_KEY_DIM)
        qb_ref[:, sl] = _rotary(zb[:, sl], cos2, sin2).astype(BF16)
        xk = zb[:, RET_KW + hh * RET_KEY_DIM:RET_KW + (hh + 1) * RET_KEY_DIM]
        kb_ref[:, sl] = (_rotary(xk, cos2, sin2) * (RET_KEY_DIM ** -0.5)).astype(BF16)
    vb_ref[...] = zb[:, 2 * RET_KW:].astype(BF16)


def _inproj_prompt(x, g, wq, wk, wvf, bf, wb, cos2, sin2, consts, tm):
    S = x.shape[0]
    tri, eq, ek, oneq, onek, onev = consts
    row = lambda w: pl.BlockSpec((tm, w), lambda i: (i, 0))
    headT = pl.BlockSpec((FOX_HEADS, FOX_HEAD_DIM, tm), lambda i: (0, 0, i))
    in_specs = [row(D_MODEL), _const_spec(g.shape), _const_spec(wq.shape), _const_spec(wk.shape),
                _const_spec(wvf.shape),
                _const_spec(bf.shape), _const_spec(wb.shape), row(LANES), row(LANES),
                _const_spec(tri.shape), _const_spec(eq.shape), _const_spec(ek.shape),
                _const_spec(oneq.shape), _const_spec(onek.shape), _const_spec(onev.shape)]
    out_shape = (
        jax.ShapeDtypeStruct((FOX_HEADS, HEAD_PAD, S), BF16),
        jax.ShapeDtypeStruct((FOX_HEADS, S, HEAD_PAD), BF16),
        jax.ShapeDtypeStruct((FOX_HEADS, V_PAD, S), BF16),
        jax.ShapeDtypeStruct((FOX_HEADS, FOX_HEAD_DIM, S), F32),
        jax.ShapeDtypeStruct((FOX_HEADS, FOX_HEAD_DIM, S), F32),
        jax.ShapeDtypeStruct((FOX_HEADS, S), F32),
        jax.ShapeDtypeStruct((S, RET_KW), BF16),
        jax.ShapeDtypeStruct((S, RET_KW), BF16),
        jax.ShapeDtypeStruct((S, RET_VW), BF16),
        jax.ShapeDtypeStruct((S // tm, 8, LANES), F32),
        jax.ShapeDtypeStruct((FOX_HEADS, S), F32),
    )
    out_specs = (
        pl.BlockSpec((FOX_HEADS, HEAD_PAD, tm), lambda i: (0, 0, i)),
        pl.BlockSpec((FOX_HEADS, tm, HEAD_PAD), lambda i: (0, i, 0)),
        pl.BlockSpec((FOX_HEADS, V_PAD, tm), lambda i: (0, 0, i)),
        headT, headT, pl.BlockSpec((FOX_HEADS, tm), lambda i: (0, i)),
        row(RET_KW), row(RET_KW), row(RET_VW),
        pl.BlockSpec((1, 8, LANES), lambda i: (i, 0, 0)),
        pl.BlockSpec((FOX_HEADS, tm), lambda i: (0, i)),
    )
    return pl.pallas_call(
        _inproj_prompt_kernel,
        grid=(S // tm,),
        in_specs=in_specs,
        out_specs=out_specs,
        out_shape=out_shape,
        scratch_shapes=[pltpu.VMEM((1, LANES), F32)],
        compiler_params=pltpu.CompilerParams(dimension_semantics=("arbitrary",),
                                             vmem_limit_bytes=VMEM_LIMIT),
        name="inproj_prompt",
    )(x, g, wq, wk, wvf, bf, wb, cos2, sin2, tri, eq, ek, oneq, onek, onev)


def _inproj_sample_kernel(x_ref, g_ref, wqkv_ref, wf_ref, bf_ref, wb_ref, cos_ref, sin_ref,
                          q_ref, k32_ref, v32_ref, logf_ref, qb_ref, kb_ref, vb_ref):
    h = _rmsnorm(x_ref[...], g_ref[...]).astype(BF16)
    logf = _log_sigmoid(_dot_t(h, wf_ref[...]) + bf_ref[...])
    logf_ref[...] = logf[:, :FOX_HEADS]
    z = _dot_t(h, wqkv_ref[...])
    q_ref[...] = (z[:, :FOX_W] * (FOX_HEAD_DIM ** -0.5 * LOG2E)).astype(BF16)
    k32_ref[...] = z[:, FOX_W:2 * FOX_W]
    v32_ref[...] = z[:, 2 * FOX_W:]
    zb = _dot_t(h, wb_ref[...])
    cos2 = cos_ref[...]
    sin2 = sin_ref[...]
    for hh in range(RET_HEADS):
        sl = slice(hh * RET_KEY_DIM, (hh + 1) * RET_KEY_DIM)
        qb_ref[:, sl] = _rotary(zb[:, sl], cos2, sin2).astype(BF16)
        xk = zb[:, RET_KW + hh * RET_KEY_DIM:RET_KW + (hh + 1) * RET_KEY_DIM]
        kb_ref[:, sl] = (_rotary(xk, cos2, sin2) * (RET_KEY_DIM ** -0.5)).astype(BF16)
    vb_ref[...] = zb[:, 2 * RET_KW:].astype(BF16)


def _inproj_sample(x, g, wqkv, wf, bf, wb, cos2, sin2):
    M = x.shape[0]
    args = (x, g, wqkv, wf, bf, wb, cos2, sin2)
    out_shape = (
        jax.ShapeDtypeStruct((M, FOX_W), BF16),
        jax.ShapeDtypeStruct((M, FOX_W), F32),
        jax.ShapeDtypeStruct((M, FOX_W), F32),
        jax.ShapeDtypeStruct((M, FOX_HEADS), F32),
        jax.ShapeDtypeStruct((M, RET_KW), BF16),
        jax.ShapeDtypeStruct((M, RET_KW), BF16),
        jax.ShapeDtypeStruct((M, RET_VW), BF16),
    )
    return pl.pallas_call(
        _inproj_sample_kernel,
        grid=(1,),
        in_specs=[_const_spec(a.shape) for a in args],
        out_specs=tuple(_const_spec(o.shape) for o in out_shape),
        out_shape=out_shape,
        compiler_params=pltpu.CompilerParams(dimension_semantics=("arbitrary",),
                                             vmem_limit_bytes=VMEM_LIMIT),
        name="inproj_sample",
    )(*args)


def _fox_prompt_kernel(it_ref, jt_ref, jfetch_ref, mode_ref, qT_ref, ka_ref, vT_ref,
                       sqq_ref, o_ref, m_ref, acc_ref, *, n_strips):
    del jfetch_ref
    t = pl.program_id(0)
    i = it_ref[t]
    j = jt_ref[t]
    T = qT_ref.shape[2]
    SUB = T // n_strips
    EXACT = n_strips + 1

    def heads(fn):
        def body(hh, carry):
            fn(hh, mode_ref[t * FOX_HEADS + hh])
            return carry
        lax.fori_loop(0, FOX_HEADS, body, 0)

    def scores(hh):
        return _dot(ka_ref[hh], qT_ref[hh])

    def exact_head(hh):
        s = scores(hh)
        kk = lax.broadcasted_iota(jnp.int32, s.shape, 0) + j * T
        qq = lax.broadcasted_iota(jnp.int32, s.shape, 1) + i * T
        s = jnp.where(kk > qq, NEG, s)
        m_old = m_ref[hh]
        m_new = jnp.maximum(m_old, jnp.max(s, axis=0, keepdims=True))
        p = jnp.exp2(s - m_new).astype(BF16)
        alpha = jnp.exp2(m_old - m_new)
        acc_ref[hh] = alpha * acc_ref[hh] + _dot(vT_ref[hh], p)
        m_ref[hh] = m_new


    def diag_head(hh, mode):
        m_ref[hh] = sqq_ref[pl.ds(hh, 1), :]

        @pl.when(mode != EXACT)
        def _():
            s = scores(hh)
            kk = lax.broadcasted_iota(jnp.int32, s.shape, 0)
            qq = lax.broadcasted_iota(jnp.int32, s.shape, 1)
            p = jnp.exp2(jnp.where(kk > qq, NEG, s) - m_ref[hh]).astype(BF16)
            acc_ref[hh] = _dot(vT_ref[hh], p)

        @pl.when(mode == EXACT)
        def _():
            acc_ref[hh] = jnp.zeros((V_PAD, T), F32)
            exact_head(hh)

    def off_head(hh, mode):
        for nn in range(1, n_strips + 1):
            k0 = (n_strips - nn) * SUB

            @pl.when(mode == nn)
            def _():
                p = jnp.exp2(_dot(ka_ref[hh, k0:, :], qT_ref[hh]) - m_ref[hh]).astype(BF16)
                acc_ref[hh] += _dot(vT_ref[hh, :, k0:], p)

        @pl.when(mode == EXACT)
        def _():
            exact_head(hh)

    @pl.when(j == i)
    def _():
        heads(diag_head)

    @pl.when(j < i)
    def _():
        heads(off_head)

    @pl.when(j == 0)
    def _():
        for pr in range(FOX_HEADS // 2):
            halves = []
            for hh in (2 * pr, 2 * pr + 1):
                a = acc_ref[hh]
                halves.append(a[:FOX_HEAD_DIM] / a[FOX_HEAD_DIM:FOX_HEAD_DIM + 1])
            o_ref[pr] = jnp.concatenate(halves, axis=0).T.astype(BF16)


def _prune_tables(stats, it, jt, nb):
    per = stats.shape[0] // nb
    st = stats.reshape(nb, per, 8, LANES)[:, :, :, :FOX_HEADS]
    nq = jnp.sqrt(jnp.max(st[:, :, 0, :], axis=1)) * NORM_SLACK
    nk_strip = jnp.sqrt(st[:, :, 1, :]) * NORM_SLACK
    nk = jnp.max(nk_strip, axis=1)
    c_first = st[:, 0, 2, :]
    c_last = st[:, :, 3, :]
    bound = (nq[it][:, None, :] * (nk_strip[jt] + nk[it][:, None, :])
             - (c_last[jt] - c_first[it][:, None, :]))
    live = jnp.logical_or(jnp.asarray(jt == it)[:, None, None],
                          jnp.logical_not(bound < PRUNE_LOG2))
    strip_no = jnp.arange(per, dtype=jnp.int32)[None, :, None]
    n_keep = per - jnp.min(jnp.where(live, strip_no, per), axis=1)
    safe = nq[it] * (nk[jt] + nk[it]) < STALE_SAFE_LOG2
    mode = jnp.where(n_keep == 0, 0, jnp.where(safe, n_keep, per + 1))
    steps = jnp.arange(len(it), dtype=jnp.int32)
    last_live = lax.cummax(jnp.where(jnp.any(n_keep > 0, axis=1), steps, 0))
    return jnp.asarray(jt)[last_live], mode.astype(jnp.int32).reshape(-1), per


def _fox_prompt(qT, ka, vT, stats, sqq, T):
    S = ka.shape[1]
    nb = S // T
    it = np.array([i for i in range(nb) for _ in range(i + 1)], np.int32)
    jt = np.array([j for i in range(nb) for j in range(i, -1, -1)], np.int32)
    jfetch, mode, n_strips = _prune_tables(stats, it, jt, nb)
    grid_spec = pltpu.PrefetchScalarGridSpec(
        num_scalar_prefetch=4,
        grid=(len(it),),
        in_specs=[
            pl.BlockSpec((FOX_HEADS, HEAD_PAD, T), lambda t, it, jt, jf, md: (0, 0, it[t])),
            pl.BlockSpec((FOX_HEADS, T, HEAD_PAD), lambda t, it, jt, jf, md: (0, jf[t], 0)),
            pl.BlockSpec((FOX_HEADS, V_PAD, T), lambda t, it, jt, jf, md: (0, 0, jf[t])),
            pl.BlockSpec((FOX_HEADS, T), lambda t, it, jt, jf, md: (0, it[t])),
        ],
        out_specs=pl.BlockSpec((FOX_HEADS // 2, T, LANES),
                               lambda t, it, jt, jf, md: (0, it[t], 0)),
        scratch_shapes=[pltpu.VMEM((FOX_HEADS, 1, T), F32),
                        pltpu.VMEM((FOX_HEADS, V_PAD, T), F32)],
    )
    return pl.pallas_call(
        functools.partial(_fox_prompt_kernel, n_strips=n_strips),
        grid_spec=grid_spec,
        out_shape=jax.ShapeDtypeStruct((FOX_HEADS // 2, S, LANES), BF16),
        compiler_params=pltpu.CompilerParams(dimension_semantics=("arbitrary",),
                                             vmem_limit_bytes=VMEM_LIMIT),
        name="fox_prompt",
    )(jnp.asarray(it), jnp.asarray(jt), jfetch, mode, qT, ka, vT, sqq)


def _fox_sample_kernel(q_ref, kn_ref, vn_ref, ckT_ref, cvT_ref, lfT_ref, up_ref, ex_ref, o_ref):
    P = ckT_ref.shape[3]
    Tn = q_ref.shape[0]
    KP = lfT_ref.shape[2]
    HQ = FOX_HEADS * Tn
    nchunk = KP // LANES
    nt = (((1,), (1,)), ((), ()))

    def stack3(x):
        parts3 = [t.astype(F32) for t in _split3(x)] + [jnp.zeros_like(x)]
        return jnp.concatenate(parts3, axis=0).astype(BF16)

    x3 = stack3(lfT_ref[0])
    up = up_ref[...]
    run = jnp.zeros((4 * FOX_HEADS, 1), F32)
    parts = []
    for cidx in range(nchunk):
        y = _dot(x3[:, cidx * LANES:(cidx + 1) * LANES], up) + run
        parts.append(y)
        run = y[:, LANES - 1:LANES]
    y = jnp.concatenate(parts, axis=1)
    cT = (y[:FOX_HEADS] + y[FOX_HEADS:2 * FOX_HEADS] + y[2 * FOX_HEADS:3 * FOX_HEADS]) * LOG2E
    ckx = _dot(ex_ref[...], stack3(cT))

    tail = ckx[:, P:P + LANES]
    rowq = lax.broadcasted_iota(jnp.int32, tail.shape, 0) % Tn
    lanek = lax.broadcasted_iota(jnp.int32, tail.shape, 1)
    cq = jnp.sum(jnp.where(lanek == rowq, tail, 0.0), axis=1, keepdims=True)

    q = q_ref[...]
    qt = jnp.concatenate([q] * FOX_HEADS, axis=0)
    rh = lax.broadcasted_iota(jnp.int32, qt.shape, 0) // Tn
    lh = lax.broadcasted_iota(jnp.int32, qt.shape, 1) // FOX_HEAD_DIM
    qbd = jnp.where(rh == lh, qt, jnp.zeros_like(qt))

    kT = ckT_ref[0].reshape(FOX_W, P).astype(BF16)
    vT = cvT_ref[0].reshape(FOX_W, P).astype(BF16)
    s_c = _dot(qbd, kT) + cq - ckx[:, :P]
    s_n = lax.dot_general(qbd, kn_ref[...].astype(BF16), nt, preferred_element_type=F32)
    s_n = s_n + cq - ckx[:, P:P + Tn]
    key = lax.broadcasted_iota(jnp.int32, s_n.shape, 1)
    qrow = lax.broadcasted_iota(jnp.int32, s_n.shape, 0) % Tn
    s_n = jnp.where(key > qrow, NEG, s_n)
    m = jnp.maximum(jnp.max(s_c, axis=1, keepdims=True), jnp.max(s_n, axis=1, keepdims=True))
    p_c = jnp.exp2(s_c - m)
    p_n = jnp.exp2(s_n - m)
    l = jnp.sum(p_c, axis=1, keepdims=True) + jnp.sum(p_n, axis=1, keepdims=True)
    z = lax.dot_general(p_c.astype(BF16), vT, nt, preferred_element_type=F32)
    z = (z + _dot(p_n.astype(BF16), vn_ref[...].astype(BF16))) / l
    zh = lax.broadcasted_iota(jnp.int32, (Tn, FOX_W), 1) // FOX_HEAD_DIM
    o = jnp.zeros((Tn, FOX_W), F32)
    for hh in range(FOX_HEADS):
        o = o + jnp.where(zh == hh, z[hh * Tn:(hh + 1) * Tn, :], 0.0)
    o_ref[...] = o.astype(BF16)


def _fox_sample(q, kn, vn, cache_kT, cache_vT, lfT, B, Tn):
    P = cache_kT.shape[3]
    KP = lfT.shape[2]
    HQ = FOX_HEADS * Tn
    up = jnp.asarray(np.triu(np.ones((LANES, LANES), np.float32)), BF16)
    ex = np.zeros((HQ, 4 * FOX_HEADS), np.float32)
    for part in range(3):
        for hh in range(FOX_HEADS):
            ex[hh * Tn:(hh + 1) * Tn, part * FOX_HEADS + hh] = 1.0
    ex = jnp.asarray(ex, BF16)
    rowb = lambda w: pl.BlockSpec((Tn, w), lambda b: (b, 0))
    return pl.pallas_call(
        _fox_sample_kernel,
        grid=(B,),
        in_specs=[rowb(FOX_W), rowb(FOX_W), rowb(FOX_W),
                  pl.BlockSpec((1, FOX_HEADS, FOX_HEAD_DIM, P), lambda b: (b, 0, 0, 0)),
                  pl.BlockSpec((1, FOX_HEADS, FOX_HEAD_DIM, P), lambda b: (b, 0, 0, 0)),
                  pl.BlockSpec((1, FOX_HEADS, KP), lambda b: (b, 0, 0)),
                  _const_spec(up.shape), _const_spec(ex.shape)],
        out_specs=rowb(FOX_W),
        out_shape=jax.ShapeDtypeStruct((B * Tn, FOX_W), BF16),
        compiler_params=pltpu.CompilerParams(dimension_semantics=("arbitrary",),
                                             vmem_limit_bytes=VMEM_LIMIT),
        name="fox_sample",
    )(q, kn, vn, cache_kT, cache_vT, lfT, up, ex)


def _retention_kernel(q_ref, k_ref, v_ref, s0_ref, dmat_ref, xi_ref, zeta_ref, gam_ref,
                      n_ref, sout_ref, st_ref):
    c = pl.program_id(1)

    @pl.when(c == 0)
    def _():
        st_ref[...] = s0_ref[0]

    for hh in range(RET_HEADS):
        q = q_ref[:, hh * RET_KEY_DIM:(hh + 1) * RET_KEY_DIM]
        k = k_ref[:, hh * RET_KEY_DIM:(hh + 1) * RET_KEY_DIM]
        v = v_ref[:, hh * RET_VAL_DIM:(hh + 1) * RET_VAL_DIM]
        st = st_ref[hh]
        sc = lax.dot_general(q, k, (((1,), (1,)), ((), ())), preferred_element_type=F32)
        sc = sc * dmat_ref[hh]
        o = _dot(sc.astype(BF16), v) + _dot(q, st.astype(BF16)) * xi_ref[hh]
        kz = (k.astype(F32) * zeta_ref[hh]).astype(BF16)
        upd = lax.dot_general(kz, v, (((0,), (0,)), ((), ())), preferred_element_type=F32)
        st_ref[hh] = gam_ref[hh] * st + upd
        mu = jnp.mean(o, axis=-1, keepdims=True)
        d = o - mu
        var = jnp.mean(d * d, axis=-1, keepdims=True)
        n_ref[:, hh * RET_VAL_DIM:(hh + 1) * RET_VAL_DIM] = (d * lax.rsqrt(var + EPS)).astype(BF16)

    @pl.when(c == pl.num_programs(1) - 1)
    def _():
        sout_ref[0] = st_ref[...]


def _ret_log_gamma():
    return jnp.log(1.0 - jnp.exp2(-5.0 - jnp.arange(RET_HEADS, dtype=F32)))


def _retention(q, k, v, state0, B, L, C):
    nc = L // C
    lg = _ret_log_gamma()
    idx = jnp.arange(C, dtype=F32)
    diff = idx[:, None] - idx[None, :]
    dmat = jnp.where(diff[None] >= 0, jnp.exp(jnp.maximum(diff, 0.0)[None] * lg[:, None, None]), 0.0)
    xi = jnp.exp((idx[None, :] + 1.0) * lg[:, None])
    zeta = jnp.exp((C - 1.0 - idx[None, :]) * lg[:, None])
    xi = jnp.broadcast_to(xi[:, :, None], (RET_HEADS, C, RET_VAL_DIM))
    zeta = jnp.broadcast_to(zeta[:, :, None], (RET_HEADS, C, RET_KEY_DIM))
    gam = jnp.broadcast_to(jnp.exp(C * lg)[:, None, None], (RET_HEADS, 1, RET_VAL_DIM))
    rowc = lambda w: pl.BlockSpec((C, w), lambda b, c: (b * nc + c, 0))
    st_spec = pl.BlockSpec((1, RET_HEADS, RET_KEY_DIM, RET_VAL_DIM), lambda b, c: (b, 0, 0, 0))
    return pl.pallas_call(
        _retention_kernel,
        grid=(B, nc),
        in_specs=[rowc(RET_KW), rowc(RET_KW), rowc(RET_VW), st_spec,
                  _const_spec(dmat.shape), _const_spec(xi.shape), _const_spec(zeta.shape),
                  _const_spec(gam.shape)],
        out_specs=(rowc(RET_VW), st_spec),
        out_shape=(jax.ShapeDtypeStruct((B * L, RET_VW), BF16),
                   jax.ShapeDtypeStruct((B, RET_HEADS, RET_KEY_DIM, RET_VAL_DIM), F32)),
        scratch_shapes=[pltpu.VMEM((RET_HEADS, RET_KEY_DIM, RET_VAL_DIM), F32)],
        compiler_params=pltpu.CompilerParams(dimension_semantics=("arbitrary", "arbitrary"),
                                             vmem_limit_bytes=VMEM_LIMIT),
        name="retention",
    )(q, k, v, state0, dmat, xi, zeta, gam)


def _mixer_ffn_kernel(x_ref, oa_ref, nb_ref, prev_ref, gmix_ref, wg_ref, gng_ref, wpa_ref, wpb_ref,
                      wo_ref, gffn_ref, wup_ref, cw_ref, cb_ref, wdn_ref, gfin_ref,
                      y_ref, conv_ref, carry_ref, ua_ref, ub_ref, acc_ref, h2_ref,
                      *, nseg, seglen):
    i = pl.program_id(0)
    NC = N_FFN_CHUNKS
    PADR = 8
    H0 = PADR - (CONV_WIDTH - 1)

    @pl.when(i == 0)
    def _():
        carry_ref[...] = prev_ref[...]

    x = x_ref[...]
    h = _rmsnorm(x, gmix_ref[...]).astype(BF16)
    zg = _dot_t(h, wg_ref[...])
    gb = zg[:, :RET_VW]
    gma = zg[:, RET_VW:RET_VW + D_MODEL]
    gmb = zg[:, RET_VW + D_MODEL:]
    oa = jnp.concatenate([oa_ref[p] for p in range(FOX_HEADS // 2)], axis=1)
    ya = _dot(oa, wpa_ref[...])
    nn = nb_ref[...].astype(F32) * gng_ref[...] * (gb * jax.nn.sigmoid(gb))
    yb = _dot(nn.astype(BF16), wpb_ref[...])
    y = jax.nn.sigmoid(gma) * ya + jax.nn.sigmoid(gmb) * yb
    x1 = x + _dot(y.astype(BF16), wo_ref[...])
    h2_ref[...] = _rmsnorm(x1, gffn_ref[...]).astype(BF16)
    acc_ref[...] = x1

    h2 = h2_ref[...]

    def up_half(u_ref, cidx, slot):
        u = _dot(h2, wup_ref[cidx])
        for s in range(nseg):
            u_ref[slot, s, PADR:PADR + seglen, :] = u[s * seglen:(s + 1) * seglen, :]
            u_ref[slot, s, H0:PADR, :] = carry_ref[cidx, s, H0:PADR, :]
            carry_ref[cidx, s, H0:PADR, :] = u[(s + 1) * seglen - (CONV_WIDTH - 1):(s + 1) * seglen, :]

    def conv_half(u_ref, cidx, slot):
        w = cw_ref[cidx]
        b = cb_ref[cidx]
        outs = []
        for s in range(nseg):
            acc = w[0:1] * u_ref[slot, s, H0:H0 + seglen, :]
            for jj in range(1, CONV_WIDTH):
                acc = acc + w[jj:jj + 1] * u_ref[slot, s, H0 + jj:H0 + jj + seglen, :]
            outs.append(b + acc)
        return outs[0] if nseg == 1 else jnp.concatenate(outs, axis=0)

    def stage_up(c):
        up_half(ua_ref, c, c % 2)
        up_half(ub_ref, NC + c, c % 2)

    stage_up(0)
    group = []
    for c in range(NC):
        if c + 1 < NC:
            stage_up(c + 1)
        a = conv_half(ua_ref, c, c % 2)
        b = conv_half(ub_ref, NC + c, c % 2)
        group.append((_gelu_tanh(a) * b).astype(BF16))
        if len(group) == DOWN_GROUP or c + 1 == NC:
            c0 = c + 1 - len(group)
            wd = wdn_ref[c0:c + 1].reshape(len(group) * FFN_CHUNK, D_MODEL)
            acc_ref[...] += _dot(jnp.concatenate(group, axis=1), wd)
            group = []
    y_ref[...] = _rmsnorm(acc_ref[...], gfin_ref[...])

    @pl.when(i == pl.num_programs(0) - 1)
    def _():
        conv_ref[...] = carry_ref[:, :, H0:PADR, :]


def _mixer_ffn(x, oa, nb, prev, weights, tm, nseg, seglen):
    M = x.shape[0]
    gmix, wg, gng, wpa, wpb, wo, gffn, wup, cw, cb, wdn, gfin = weights
    FC = FFN_CHUNK
    NC = N_FFN_CHUNKS
    row = lambda w: pl.BlockSpec((tm, w), lambda i: (i, 0))
    wspec = lambda a: pl.BlockSpec(a.shape, lambda i, n=a.ndim: (0,) * n,
                                   pipeline_mode=pl.Buffered(1))
    in_specs = [row(D_MODEL),
                pl.BlockSpec((FOX_HEADS // 2, tm, LANES), lambda i: (0, i, 0)),
                row(RET_VW), wspec(prev)] + [wspec(w) for w in weights]
    out_shape = (jax.ShapeDtypeStruct((M, D_MODEL), F32),
                 jax.ShapeDtypeStruct((2 * NC, nseg, CONV_WIDTH - 1, FC), F32))
    out_specs = (row(D_MODEL), _const_spec(out_shape[1].shape))
    return pl.pallas_call(
        functools.partial(_mixer_ffn_kernel, nseg=nseg, seglen=seglen),
        grid=(M // tm,),
        in_specs=in_specs,
        out_specs=out_specs,
        out_shape=out_shape,
        scratch_shapes=[pltpu.VMEM((2 * NC, nseg, 8, FC), F32),
                        pltpu.VMEM((2, nseg, 8 + seglen, FC), F32),
                        pltpu.VMEM((2, nseg, 8 + seglen, FC), F32),
                        pltpu.VMEM((tm, D_MODEL), F32),
                        pltpu.VMEM((tm, D_MODEL), BF16)],
        compiler_params=pltpu.CompilerParams(dimension_semantics=("arbitrary",),
                                             vmem_limit_bytes=VMEM_LIMIT),
        name="mixer_ffn",
    )(x, oa, nb, prev, *weights)


def _rotary_tables(start, n):
    half = RET_KEY_DIM // 2
    inv = 1.0 / (ROPE_BASE ** jnp.linspace(0.0, 1.0, half, dtype=F32))
    fine = min(n, ROT_FINE)
    assert n % fine == 0
    a_hi = (start + fine * jnp.arange(n // fine)).astype(F32)[:, None] * inv[None, :]
    a_lo = jnp.arange(fine).astype(F32)[:, None] * inv[None, :]
    ch, sh = jnp.cos(a_hi)[:, None, :], jnp.sin(a_hi)[:, None, :]
    cl, sl = jnp.cos(a_lo)[None, :, :], jnp.sin(a_lo)[None, :, :]
    cos = (ch * cl - sh * sl).reshape(n, half)
    sin = (sh * cl + ch * sl).reshape(n, half)
    return jnp.concatenate([cos, cos], axis=1), jnp.concatenate([-sin, sin], axis=1)


def _pad_heads(wt, pad):
    d = wt.shape[1]
    wt = wt.reshape(FOX_HEADS, FOX_HEAD_DIM, d)
    wt = jnp.pad(wt, ((0, 0), (0, pad - FOX_HEAD_DIM), (0, 0)))
    return wt.reshape(FOX_HEADS * pad, d)


def _prompt_consts(tm):
    tri = np.tril(np.ones((tm, tm), np.float32))
    eq = np.zeros((LANES, FOX_HEADS * HEAD_PAD), np.float32)
    ek = np.zeros((LANES, FOX_HEADS * HEAD_PAD), np.float32)
    oneq = np.zeros((1, FOX_HEADS * HEAD_PAD), np.float32)
    onek = np.zeros((1, FOX_HEADS * HEAD_PAD), np.float32)
    onev = np.zeros((1, FOX_HEADS * V_PAD), np.float32)
    for hh in range(FOX_HEADS):
        base = hh * HEAD_PAD + BIAS_COL
        for part in range(3):
            eq[part * FOX_HEADS + hh, base + part] = 1.0
            ek[part * FOX_HEADS + hh, base + 3 + part] = -1.0
            onek[0, base + part] = 1.0
            oneq[0, base + 3 + part] = 1.0
        onev[0, hh * V_PAD + FOX_HEAD_DIM] = 1.0
    return (jnp.asarray(tri, BF16), jnp.asarray(eq, BF16), jnp.asarray(ek, BF16),
            jnp.asarray(oneq), jnp.asarray(onek), jnp.asarray(onev))


def _chunk_cols(a):
    lead = a.shape[:-1]
    a = a.reshape(lead + (2 * N_FFN_CHUNKS, FFN_CHUNK))
    return jnp.moveaxis(a, -2, 0)


def _tile(n, pref):
    t = min(n, pref)
    while n % t:
        t //= 2
    return t


def kernel(x_prompt, x_sample, cache_fox_k, cache_fox_v, cache_fox_logf, state_ret, state_ffn_conv,
           norm_mix_g, w_in, b_fox_f, gn_ret_g, w_pa, w_pb, w_o, norm_ffn_g, w_up, conv_w, conv_b,
           w_down, norm_final_g):
    depth = w_in.shape[0]
    Bp, S, _ = x_prompt.shape
    Bs, Ts, _ = x_sample.shape
    P = cache_fox_k.shape[2]
    assert depth == 1 and Bp == 1, "kernel handles the single-layer, single-prompt configuration"
    l = 0

    wt = jnp.swapaxes(w_in[l], 0, 1).astype(BF16)
    o0 = 3 * FOX_W
    o1 = o0 + FOX_HEADS
    o2 = o1 + 2 * RET_KW + RET_VW
    wq_aug = _pad_heads(wt[:FOX_W], HEAD_PAD)
    wk_aug = _pad_heads(wt[FOX_W:2 * FOX_W], HEAD_PAD)
    wf = jnp.pad(wt[o0:o1], ((0, LANES - FOX_HEADS), (0, 0)))
    wvf = jnp.concatenate([_pad_heads(wt[2 * FOX_W:o0], V_PAD), wf], axis=0)
    wqkv = wt[:o0]
    bf = jnp.pad(b_fox_f[l].astype(F32), (0, LANES - FOX_HEADS))[None, :]
    wb = wt[o1:o2]
    wg = wt[o2:]
    gmix = norm_mix_g[l].astype(F32)[None, :]
    mix_weights = (
        gmix, wg, gn_ret_g[l].astype(F32)[None, :], w_pa[l].astype(BF16), w_pb[l].astype(BF16),
        w_o[l].astype(BF16), norm_ffn_g[l].astype(F32)[None, :],
        _chunk_cols(w_up[l]).astype(BF16),
        jnp.pad(_chunk_cols(conv_w[l].astype(F32)), ((0, 0), (0, 8 - CONV_WIDTH), (0, 0))),
        _chunk_cols(conv_b[l].astype(F32)[None, :]),
        w_down[l].reshape(N_FFN_CHUNKS, FFN_CHUNK, D_MODEL).astype(BF16),
        norm_final_g.astype(F32)[None, :],
    )

    tm_a = _tile(S, 512)
    cos_p, sin_p = _rotary_tables(0, S)
    (qT, ka, vT, kT_p, vT_p, logfT_p, qb, kb, vb, stats, sqq) = _inproj_prompt(
        x_prompt[0], gmix, wq_aug, wk_aug, wvf, bf, wb, cos_p, sin_p,
        _prompt_consts(tm_a), tm_a)
    oa_p = _fox_prompt(qT, ka, vT, stats, sqq, _tile(S, 1024))
    zero_state = jnp.zeros((1, RET_HEADS, RET_KEY_DIM, RET_VAL_DIM), F32)
    nb_p, ret_p = _retention(qb, kb, vb, zero_state, 1, S, _tile(S, 512))
    tm_d = _tile(S, 256)
    zero_prev = jnp.zeros((2 * N_FFN_CHUNKS, 1, 8, FFN_CHUNK), F32)
    y_p, conv_p = _mixer_ffn(x_prompt[0], oa_p, nb_p, zero_prev, mix_weights, tm_d, 1, tm_d)

    Ms = Bs * Ts
    cos_s, sin_s = _rotary_tables(P, Ts)
    cos_s = jnp.tile(cos_s, (Bs, 1))
    sin_s = jnp.tile(sin_s, (Bs, 1))
    (q_s, k_s, v_s, logf_s, qb_s, kb_s, vb_s) = _inproj_sample(
        x_sample.reshape(Ms, D_MODEL), gmix, wqkv, wf, bf, wb, cos_s, sin_s)
    KP = ((P + Ts + LANES - 1) // LANES) * LANES
    lf_all = jnp.concatenate([cache_fox_logf[l].astype(F32), logf_s.reshape(Bs, Ts, FOX_HEADS)], axis=1)
    lfT = jnp.pad(jnp.swapaxes(lf_all, 1, 2), ((0, 0), (0, 0), (0, KP - P - Ts)))
    oa_s = _fox_sample(q_s, k_s, v_s, jnp.transpose(cache_fox_k[l], (0, 2, 3, 1)),
                       jnp.transpose(cache_fox_v[l], (0, 2, 3, 1)), lfT, Bs, Ts)
    oa_s = jnp.moveaxis(oa_s.reshape(Ms, FOX_HEADS // 2, LANES), 1, 0)
    nb_s, ret_s = _retention(qb_s, kb_s, vb_s, state_ret[l].astype(F32), Bs, Ts, Ts)
    prev_s = _chunk_cols(state_ffn_conv[l].astype(F32))
    prev_s = jnp.pad(prev_s, ((0, 0), (0, 0), (8 - (CONV_WIDTH - 1), 0), (0, 0)))
    y_s, conv_s = _mixer_ffn(x_sample.reshape(Ms, D_MODEL), oa_s, nb_s, prev_s, mix_weights,
                             Ms, Bs, Ts)

    def unchunk(cv):
        return jnp.moveaxis(cv, 0, 2).reshape(cv.shape[1], CONV_WIDTH - 1, 2 * FFN_DIM)

    hshape = (FOX_HEADS, FOX_HEAD_DIM)
    return (
        y_p[None],
        y_s.reshape(Bs, Ts, D_MODEL),
        jnp.transpose(kT_p, (2, 0, 1))[None, None],
        jnp.transpose(vT_p, (2, 0, 1))[None, None],
        jnp.transpose(logfT_p, (1, 0))[None, None],
        ret_p[None],
        unchunk(conv_p)[None],
        k_s.reshape((1, Bs, Ts) + hshape),
        v_s.reshape((1, Bs, Ts) + hshape),
        logf_s.reshape(1, Bs, Ts, FOX_HEADS),
        ret_s[None],
        unchunk(conv_s)[None],
    )
```

```python
import functools
import math

import numpy as np
import jax
import jax.numpy as jnp
from jax import lax
from jax.experimental import pallas as pl
from jax.experimental.pallas import tpu as pltpu

F32 = jnp.float32
BF16 = jnp.bfloat16

D_MODEL = 1024
FOX_HEADS = 8
FOX_HEAD_DIM = 64
RET_HEADS = 4
RET_KEY_DIM = 128
RET_VAL_DIM = 256
FFN_DIM = 2816
CONV_WIDTH = 3
EPS = 1e-6
ROPE_BASE = 10000.0

FOX_W = FOX_HEADS * FOX_HEAD_DIM
RET_KW = RET_HEADS * RET_KEY_DIM
RET_VW = RET_HEADS * RET_VAL_DIM

LOG2E = 1.4426950408889634
LANES = 128
HEAD_PAD = LANES
V_PAD = 80
BIAS_COL = FOX_HEAD_DIM
NEG = -1e30
STALE_SAFE_LOG2 = 64.0
PRUNE_LOG2 = -160.0
NORM_SLACK = 1.02
ROT_FINE = 128
FFN_CHUNK = 256
N_FFN_CHUNKS = FFN_DIM // FFN_CHUNK
DOWN_GROUP = 4
VMEM_LIMIT = 56 * 1024 * 1024


def _rmsnorm(x, g):
    ms = jnp.mean(x * x, axis=-1, keepdims=True)
    return x * lax.rsqrt(ms + EPS) * g


def _split3(x):
    hi = x.astype(BF16)
    r1 = x - hi.astype(F32)
    mid = r1.astype(BF16)
    lo = (r1 - mid.astype(F32)).astype(BF16)
    return hi, mid, lo


def _log_sigmoid(x):
    return jnp.minimum(x, 0.0) - jnp.log1p(jnp.exp(-jnp.abs(x)))


def _dot(a, b):
    return jnp.dot(a, b, preferred_element_type=F32)


def _dot_t(a, bt):
    return lax.dot_general(a, bt, (((1,), (1,)), ((), ())), preferred_element_type=F32)


def _gelu_tanh(x):
    c0 = math.sqrt(2.0 / math.pi)
    hx = 0.5 * x
    return hx + hx * jnp.tanh(x * (c0 + (c0 * 0.044715) * (x * x)))


def _rotary(x, cos2, sin2):
    return x * cos2 + pltpu.roll(x, RET_KEY_DIM // 2, 1) * sin2


def _const_spec(shape):
    n = len(shape)
    return pl.BlockSpec(shape, lambda *_: (0,) * n)


def _inproj_prompt_kernel(x_ref, g_ref, wq_ref, wk_ref, wvf_ref, bf_ref, wb_ref,
                          cos_ref, sin_ref, tri_ref, eq_ref, ek_ref, oneq_ref, onek_ref, onev_ref,
                          qT_ref, ka_ref, vT_ref, kT32_ref, vT32_ref, logf_ref, qb_ref, kb_ref, vb_ref,
                          stats_ref, sqq_ref, carry_ref):
    tm = x_ref.shape[0]
    VW = FOX_HEADS * V_PAD

    @pl.when(pl.program_id(0) == 0)
    def _():
        carry_ref[...] = jnp.zeros_like(carry_ref)

    h = _rmsnorm(x_ref[...], g_ref[...]).astype(BF16)
    zvf = _dot_t(h, wvf_ref[...])

    logf = _log_sigmoid(zvf[:, VW:] + bf_ref[...])
    logf_ref[...] = logf.T[:FOX_HEADS, :]
    lane = lax.broadcasted_iota(jnp.int32, logf.shape, 1)
    logf = jnp.where(lane < FOX_HEADS, logf, 0.0)
    r = _dot(tri_ref[...], jnp.concatenate(_split3(logf), axis=1))
    c = r[:, :LANES] + r[:, LANES:2 * LANES] + r[:, 2 * LANES:] + carry_ref[...]
    carry_ref[...] = c[tm - 1:tm, :]
    c2 = c * LOG2E
    hi, mid, lo = (t.astype(F32) for t in _split3(c2))
    c3 = (hi + pltpu.roll(mid, FOX_HEADS, 1) + pltpu.roll(lo, 2 * FOX_HEADS, 1)).astype(BF16)

    q_aug = (_dot_t(h, wq_ref[...]) * (FOX_HEAD_DIM ** -0.5 * LOG2E)
             + _dot(c3, eq_ref[...]) + oneq_ref[...])
    qT32 = q_aug.T
    k_aug = _dot_t(h, wk_ref[...]) + _dot(c3, ek_ref[...]) + onek_ref[...]
    kT32 = k_aug.T
    vT32 = (zvf[:, :VW] + onev_ref[...]).T
    qT = qT32.astype(BF16)
    k_aug = k_aug.astype(BF16)
    vT = vT32.astype(BF16)

    lane1 = lax.broadcasted_iota(jnp.int32, (1, LANES), 1)
    nq2 = jnp.zeros((1, LANES), F32)
    nk2 = jnp.zeros((1, LANES), F32)
    for hh in range(FOX_HEADS):
        qT_ref[hh] = qT[hh * HEAD_PAD:(hh + 1) * HEAD_PAD, :]
        ka_ref[hh] = k_aug[:, hh * HEAD_PAD:(hh + 1) * HEAD_PAD]
        vT_ref[hh] = vT[hh * V_PAD:(hh + 1) * V_PAD, :]
        qh = qT32[hh * HEAD_PAD:hh * HEAD_PAD + FOX_HEAD_DIM, :]
        kh = kT32[hh * HEAD_PAD:hh * HEAD_PAD + FOX_HEAD_DIM, :]
        kT32_ref[hh] = kh
        vT32_ref[hh] = vT32[hh * V_PAD:hh * V_PAD + FOX_HEAD_DIM, :]
        sqq_ref[hh:hh + 1, :] = jnp.sum(qh * kh, axis=0, keepdims=True)
        q2 = jnp.max(jnp.sum(qh * qh, axis=0, keepdims=True), axis=1, keepdims=True)
        k2 = jnp.max(jnp.sum(kh * kh, axis=0, keepdims=True), axis=1, keepdims=True)
        nq2 = jnp.where(lane1 == hh, q2, nq2)
        nk2 = jnp.where(lane1 == hh, k2, nk2)
    stats_ref[0] = jnp.concatenate(
        [nq2, nk2, c2[0:1, :], c2[tm - 1:tm, :], jnp.zeros((4, LANES), F32)], axis=0)

    zb = _dot_t(h, wb_ref[...])
    cos2 = cos_ref[...]
    sin2 = sin_ref[...]
    for hh in range(RET_HEADS):
        sl = slice(hh * RET_KEY_DIM, (hh + 1) * RET_KEY_DIM)
        qb_ref[:, sl] = _rotary(zb[:, sl], cos2, sin2).astype(BF16)
        xk = zb[:, RET_KW + hh * RET_KEY_DIM:RET_KW + (hh + 1) * RET_KEY_DIM]
        kb_ref[:, sl] = (_rotary(xk, cos2, sin2) * (RET_KEY_DIM ** -0.5)).astype(BF16)
    vb_ref[...] = zb[:, 2 * RET_KW:].astype(BF16)


def _inproj_prompt(x, g, wq, wk, wvf, bf, wb, cos2, sin2, consts, tm):
    S = x.shape[0]
    tri, eq, ek, oneq, onek, onev = consts
    row = lambda w: pl.BlockSpec((tm, w), lambda i: (i, 0))
    headT = pl.BlockSpec((FOX_HEADS, FOX_HEAD_DIM, tm), lambda i: (0, 0, i))
    in_specs = [row(D_MODEL), _const_spec(g.shape), _const_spec(wq.shape), _const_spec(wk.shape),
                _const_spec(wvf.shape),
                _const_spec(bf.shape), _const_spec(wb.shape), row(LANES), row(LANES),
                _const_spec(tri.shape), _const_spec(eq.shape), _const_spec(ek.shape),
                _const_spec(oneq.shape), _const_spec(onek.shape), _const_spec(onev.shape)]
    out_shape = (
        jax.ShapeDtypeStruct((FOX_HEADS, HEAD_PAD, S), BF16),
        jax.ShapeDtypeStruct((FOX_HEADS, S, HEAD_PAD), BF16),
        jax.ShapeDtypeStruct((FOX_HEADS, V_PAD, S), BF16),
        jax.ShapeDtypeStruct((FOX_HEADS, FOX_HEAD_DIM, S), F32),
        jax.ShapeDtypeStruct((FOX_HEADS, FOX_HEAD_DIM, S), F32),
        jax.ShapeDtypeStruct((FOX_HEADS, S), F32),
        jax.ShapeDtypeStruct((S, RET_KW), BF16),
        jax.ShapeDtypeStruct((S, RET_KW), BF16),
        jax.ShapeDtypeStruct((S, RET_VW), BF16),
        jax.ShapeDtypeStruct((S // tm, 8, LANES), F32),
        jax.ShapeDtypeStruct((FOX_HEADS, S), F32),
    )
    out_specs = (
        pl.BlockSpec((FOX_HEADS, HEAD_PAD, tm), lambda i: (0, 0, i)),
        pl.BlockSpec((FOX_HEADS, tm, HEAD_PAD), lambda i: (0, i, 0)),
        pl.BlockSpec((FOX_HEADS, V_PAD, tm), lambda i: (0, 0, i)),
        headT, headT, pl.BlockSpec((FOX_HEADS, tm), lambda i: (0, i)),
        row(RET_KW), row(RET_KW), row(RET_VW),
        pl.BlockSpec((1, 8, LANES), lambda i: (i, 0, 0)),
        pl.BlockSpec((FOX_HEADS, tm), lambda i: (0, i)),
    )
    return pl.pallas_call(
        _inproj_prompt_kernel,
        grid=(S // tm,),
        in_specs=in_specs,
        out_specs=out_specs,
        out_shape=out_shape,
        scratch_shapes=[pltpu.VMEM((1, LANES), F32)],
        compiler_params=pltpu.CompilerParams(dimension_semantics=("arbitrary",),
                                             vmem_limit_bytes=VMEM_LIMIT),
        name="inproj_prompt",
    )(x, g, wq, wk, wvf, bf, wb, cos2, sin2, tri, eq, ek, oneq, onek, onev)


def _inproj_sample_kernel(x_ref, g_ref, wqkv_ref, wf_ref, bf_ref, wb_ref, cos_ref, sin_ref,
                          q_ref, k32_ref, v32_ref, logf_ref, qb_ref, kb_ref, vb_ref):
    h = _rmsnorm(x_ref[...], g_ref[...]).astype(BF16)
    logf = _log_sigmoid(_dot_t(h, wf_ref[...]) + bf_ref[...])
    logf_ref[...] = logf[:, :FOX_HEADS]
    z = _dot_t(h, wqkv_ref[...])
    q_ref[...] = (z[:, :FOX_W] * (FOX_HEAD_DIM ** -0.5 * LOG2E)).astype(BF16)
    k32_ref[...] = z[:, FOX_W:2 * FOX_W]
    v32_ref[...] = z[:, 2 * FOX_W:]
    zb = _dot_t(h, wb_ref[...])
    cos2 = cos_ref[...]
    sin2 = sin_ref[...]
    for hh in range(RET_HEADS):
        sl = slice(hh * RET_KEY_DIM, (hh + 1) * RET_KEY_DIM)
        qb_ref[:, sl] = _rotary(zb[:, sl], cos2, sin2).astype(BF16)
        xk = zb[:, RET_KW + hh * RET_KEY_DIM:RET_KW + (hh + 1) * RET_KEY_DIM]
        kb_ref[:, sl] = (_rotary(xk, cos2, sin2) * (RET_KEY_DIM ** -0.5)).astype(BF16)
    vb_ref[...] = zb[:, 2 * RET_KW:].astype(BF16)


def _inproj_sample(x, g, wqkv, wf, bf, wb, cos2, sin2):
    M = x.shape[0]
    args = (x, g, wqkv, wf, bf, wb, cos2, sin2)
    out_shape = (
        jax.ShapeDtypeStruct((M, FOX_W), BF16),
        jax.ShapeDtypeStruct((M, FOX_W), F32),
        jax.ShapeDtypeStruct((M, FOX_W), F32),
        jax.ShapeDtypeStruct((M, FOX_HEADS), F32),
        jax.ShapeDtypeStruct((M, RET_KW), BF16),
        jax.ShapeDtypeStruct((M, RET_KW), BF16),
        jax.ShapeDtypeStruct((M, RET_VW), BF16),
    )
    return pl.pallas_call(
        _inproj_sample_kernel,
        grid=(1,),
        in_specs=[_const_spec(a.shape) for a in args],
        out_specs=tuple(_const_spec(o.shape) for o in out_shape),
        out_shape=out_shape,
        compiler_params=pltpu.CompilerParams(dimension_semantics=("arbitrary",),
                                             vmem_limit_bytes=VMEM_LIMIT),
        name="inproj_sample",
    )(*args)


def _fox_prompt_kernel(it_ref, jt_ref, jfetch_ref, mode_ref, qT_ref, ka_ref, vT_ref,
                       sqq_ref, o_ref, m_ref, acc_ref, *, n_strips):
    del jfetch_ref
    t = pl.program_id(0)
    i = it_ref[t]
    j = jt_ref[t]
    T = qT_ref.shape[2]
    SUB = T // n_strips
    EXACT = n_strips + 1

    def heads(fn):
        def body(hh, carry):
            fn(hh, mode_ref[t * FOX_HEADS + hh])
            return carry
        lax.fori_loop(0, FOX_HEADS, body, 0)

    def scores(hh):
        return _dot(ka_ref[hh], qT_ref[hh])

    def exact_head(hh):
        s = scores(hh)
        kk = lax.broadcasted_iota(jnp.int32, s.shape, 0) + j * T
        qq = lax.broadcasted_iota(jnp.int32, s.shape, 1) + i * T
        s = jnp.where(kk > qq, NEG, s)
        m_old = m_ref[hh]
        m_new = jnp.maximum(m_old, jnp.max(s, axis=0, keepdims=True))
        p = jnp.exp2(s - m_new).astype(BF16)
        alpha = jnp.exp2(m_old - m_new)
        acc_ref[hh] = alpha * acc_ref[hh] + _dot(vT_ref[hh], p)
        m_ref[hh] = m_new


    def diag_head(hh, mode):
        m_ref[hh] = sqq_ref[pl.ds(hh, 1), :]

        @pl.when(mode != EXACT)
        def _():
            s = scores(hh)
            kk = lax.broadcasted_iota(jnp.int32, s.shape, 0)
            qq = lax.broadcasted_iota(jnp.int32, s.shape, 1)
            p = jnp.exp2(jnp.where(kk > qq, NEG, s) - m_ref[hh]).astype(BF16)
            acc_ref[hh] = _dot(vT_ref[hh], p)

        @pl.when(mode == EXACT)
        def _():
            acc_ref[hh] = jnp.zeros((V_PAD, T), F32)
            exact_head(hh)

    def off_head(hh, mode):
        for nn in range(1, n_strips + 1):
            k0 = (n_strips - nn) * SUB

            @pl.when(mode == nn)
            def _():
                p = jnp.exp2(_dot(ka_ref[hh, k0:, :], qT_ref[hh]) - m_ref[hh]).astype(BF16)
                acc_ref[hh] += _dot(vT_ref[hh, :, k0:], p)

        @pl.when(mode == EXACT)
        def _():
            exact_head(hh)

    @pl.when(j == i)
    def _():
        heads(diag_head)

    @pl.when(j < i)
    def _():
        heads(off_head)

    @pl.when(j == 0)
    def _():
        for pr in range(FOX_HEADS // 2):
            halves = []
            for hh in (2 * pr, 2 * pr + 1):
                a = acc_ref[hh]
                halves.append(a[:FOX_HEAD_DIM] / a[FOX_HEAD_DIM:FOX_HEAD_DIM + 1])
            o_ref[pr] = jnp.concatenate(halves, axis=0).T.astype(BF16)


def _prune_tables(stats, it, jt, nb):
    per = stats.shape[0] // nb
    st = stats.reshape(nb, per, 8, LANES)[:, :, :, :FOX_HEADS]
    nq = jnp.sqrt(jnp.max(st[:, :, 0, :], axis=1)) * NORM_SLACK
    nk_strip = jnp.sqrt(st[:, :, 1, :]) * NORM_SLACK
    nk = jnp.max(nk_strip, axis=1)
    c_first = st[:, 0, 2, :]
    c_last = st[:, :, 3, :]
    bound = (nq[it][:, None, :] * (nk_strip[jt] + nk[it][:, None, :])
             - (c_last[jt] - c_first[it][:, None, :]))
    live = jnp.logical_or(jnp.asarray(jt == it)[:, None, None],
                          jnp.logical_not(bound < PRUNE_LOG2))
    strip_no = jnp.arange(per, dtype=jnp.int32)[None, :, None]
    n_keep = per - jnp.min(jnp.where(live, strip_no, per), axis=1)
    safe = nq[it] * (nk[jt] + nk[it]) < STALE_SAFE_LOG2
    mode = jnp.where(n_keep == 0, 0, jnp.where(safe, n_keep, per + 1))
    steps = jnp.arange(len(it), dtype=jnp.int32)
    last_live = lax.cummax(jnp.where(jnp.any(n_keep > 0, axis=1), steps, 0))
    return jnp.asarray(jt)[last_live], mode.astype(jnp.int32).reshape(-1), per


def _fox_prompt(qT, ka, vT, stats, sqq, T):
    S = ka.shape[1]
    nb = S // T
    it = np.array([i for i in range(nb) for _ in range(i + 1)], np.int32)
    jt = np.array([j for i in range(nb) for j in range(i, -1, -1)], np.int32)
    jfetch, mode, n_strips = _prune_tables(stats, it, jt, nb)
    grid_spec = pltpu.PrefetchScalarGridSpec(
        num_scalar_prefetch=4,
        grid=(len(it),),
        in_specs=[
            pl.BlockSpec((FOX_HEADS, HEAD_PAD, T), lambda t, it, jt, jf, md: (0, 0, it[t])),
            pl.BlockSpec((FOX_HEADS, T, HEAD_PAD), lambda t, it, jt, jf, md: (0, jf[t], 0)),
            pl.BlockSpec((FOX_HEADS, V_PAD, T), lambda t, it, jt, jf, md: (0, 0, jf[t])),
            pl.BlockSpec((FOX_HEADS, T), lambda t, it, jt, jf, md: (0, it[t])),
        ],
        out_specs=pl.BlockSpec((FOX_HEADS // 2, T, LANES),
                               lambda t, it, jt, jf, md: (0, it[t], 0)),
        scratch_shapes=[pltpu.VMEM((FOX_HEADS, 1, T), F32),
                        pltpu.VMEM((FOX_HEADS, V_PAD, T), F32)],
    )
    return pl.pallas_call(
        functools.partial(_fox_prompt_kernel, n_strips=n_strips),
        grid_spec=grid_spec,
        out_shape=jax.ShapeDtypeStruct((FOX_HEADS // 2, S, LANES), BF16),
        compiler_params=pltpu.CompilerParams(dimension_semantics=("arbitrary",),
                                             vmem_limit_bytes=VMEM_LIMIT),
        name="fox_prompt",
    )(jnp.asarray(it), jnp.asarray(jt), jfetch, mode, qT, ka, vT, sqq)


def _fox_sample_kernel(q_ref, kn_ref, vn_ref, ckT_ref, cvT_ref, lfT_ref, up_ref, ex_ref, o_ref):
    P = ckT_ref.shape[3]
    Tn = q_ref.shape[0]
    KP = lfT_ref.shape[2]
    HQ = FOX_HEADS * Tn
    nchunk = KP // LANES
    nt = (((1,), (1,)), ((), ()))

    def stack3(x):
        parts3 = [t.astype(F32) for t in _split3(x)] + [jnp.zeros_like(x)]
        return jnp.concatenate(parts3, axis=0).astype(BF16)

    x3 = stack3(lfT_ref[0])
    up = up_ref[...]
    run = jnp.zeros((4 * FOX_HEADS, 1), F32)
    parts = []
    for cidx in range(nchunk):
        y = _dot(x3[:, cidx * LANES:(cidx + 1) * LANES], up) + run
        parts.append(y)
        run = y[:, LANES - 1:LANES]
    y = jnp.concatenate(parts, axis=1)
    cT = (y[:FOX_HEADS] + y[FOX_HEADS:2 * FOX_HEADS] + y[2 * FOX_HEADS:3 * FOX_HEADS]) * LOG2E
    ckx = _dot(ex_ref[...], stack3(cT))

    tail = ckx[:, P:P + LANES]
    rowq = lax.broadcasted_iota(jnp.int32, tail.shape, 0) % Tn
    lanek = lax.broadcasted_iota(jnp.int32, tail.shape, 1)
    cq = jnp.sum(jnp.where(lanek == rowq, tail, 0.0), axis=1, keepdims=True)

    q = q_ref[...]
    qt = jnp.concatenate([q] * FOX_HEADS, axis=0)
    rh = lax.broadcasted_iota(jnp.int32, qt.shape, 0) // Tn
    lh = lax.broadcasted_iota(jnp.int32, qt.shape, 1) // FOX_HEAD_DIM
    qbd = jnp.where(rh == lh, qt, jnp.zeros_like(qt))

    kT = ckT_ref[0].reshape(FOX_W, P).astype(BF16)
    vT = cvT_ref[0].reshape(FOX_W, P).astype(BF16)
    s_c = _dot(qbd, kT) + cq - ckx[:, :P]
    s_n = lax.dot_general(qbd, kn_ref[...].astype(BF16), nt, preferred_element_type=F32)
    s_n = s_n + cq - ckx[:, P:P + Tn]
    key = lax.broadcasted_iota(jnp.int32, s_n.shape, 1)
    qrow = lax.broadcasted_iota(jnp.int32, s_n.shape, 0) % Tn
    s_n = jnp.where(key > qrow, NEG, s_n)
    m = jnp.maximum(jnp.max(s_c, axis=1, keepdims=True), jnp.max(s_n, axis=1, keepdims=True))
    p_c = jnp.exp2(s_c - m)
    p_n = jnp.exp2(s_n - m)
    l = jnp.sum(p_c, axis=1, keepdims=True) + jnp.sum(p_n, axis=1, keepdims=True)
    z = lax.dot_general(p_c.astype(BF16), vT, nt, preferred_element_type=F32)
    z = (z + _dot(p_n.astype(BF16), vn_ref[...].astype(BF16))) / l
    zh = lax.broadcasted_iota(jnp.int32, (Tn, FOX_W), 1) // FOX_HEAD_DIM
    o = jnp.zeros((Tn, FOX_W), F32)
    for hh in range(FOX_HEADS):
        o = o + jnp.where(zh == hh, z[hh * Tn:(hh + 1) * Tn, :], 0.0)
    o_ref[...] = o.astype(BF16)


def _fox_sample(q, kn, vn, cache_kT, cache_vT, lfT, B, Tn):
    P = cache_kT.shape[3]
    KP = lfT.shape[2]
    HQ = FOX_HEADS * Tn
    up = jnp.asarray(np.triu(np.ones((LANES, LANES), np.float32)), BF16)
    ex = np.zeros((HQ, 4 * FOX_HEADS), np.float32)
    for part in range(3):
        for hh in range(FOX_HEADS):
            ex[hh * Tn:(hh + 1) * Tn, part * FOX_HEADS + hh] = 1.0
    ex = jnp.asarray(ex, BF16)
    rowb = lambda w: pl.BlockSpec((Tn, w), lambda b: (b, 0))
    return pl.pallas_call(
        _fox_sample_kernel,
        grid=(B,),
        in_specs=[rowb(FOX_W), rowb(FOX_W), rowb(FOX_W),
                  pl.BlockSpec((1, FOX_HEADS, FOX_HEAD_DIM, P), lambda b: (b, 0, 0, 0)),
                  pl.BlockSpec((1, FOX_HEADS, FOX_HEAD_DIM, P), lambda b: (b, 0, 0, 0)),
                  pl.BlockSpec((1, FOX_HEADS, KP), lambda b: (b, 0, 0)),
                  _const_spec(up.shape), _const_spec(ex.shape)],
        out_specs=rowb(FOX_W),
        out_shape=jax.ShapeDtypeStruct((B * Tn, FOX_W), BF16),
        compiler_params=pltpu.CompilerParams(dimension_semantics=("arbitrary",),
                                             vmem_limit_bytes=VMEM_LIMIT),
        name="fox_sample",
    )(q, kn, vn, cache_kT, cache_vT, lfT, up, ex)


def _retention_kernel(q_ref, k_ref, v_ref, s0_ref, dmat_ref, xi_ref, zeta_ref, gam_ref,
                      n_ref, sout_ref, st_ref):
    c = pl.program_id(1)
    G = st_ref.shape[0]
    C = q_ref.shape[0] // G

    @pl.when(c == 0)
    def _():
        st_ref[...] = s0_ref[...]

    for g in range(G):
        rows = slice(g * C, (g + 1) * C)
        for hh in range(RET_HEADS):
            q = q_ref[rows, hh * RET_KEY_DIM:(hh + 1) * RET_KEY_DIM]
            k = k_ref[rows, hh * RET_KEY_DIM:(hh + 1) * RET_KEY_DIM]
            v = v_ref[rows, hh * RET_VAL_DIM:(hh + 1) * RET_VAL_DIM]
            st = st_ref[g, hh]
            sc = lax.dot_general(q, k, (((1,), (1,)), ((), ())), preferred_element_type=F32)
            sc = sc * dmat_ref[hh]
            o = _dot(sc.astype(BF16), v) + _dot(q, st.astype(BF16)) * xi_ref[hh]
            kz = (k.astype(F32) * zeta_ref[hh]).astype(BF16)
            upd = lax.dot_general(kz, v, (((0,), (0,)), ((), ())), preferred_element_type=F32)
            st_ref[g, hh] = gam_ref[hh] * st + upd
            mu = jnp.mean(o, axis=-1, keepdims=True)
            d = o - mu
            var = jnp.mean(d * d, axis=-1, keepdims=True)
            n_ref[rows, hh * RET_VAL_DIM:(hh + 1) * RET_VAL_DIM] = (
                d * lax.rsqrt(var + EPS)).astype(BF16)

    @pl.when(c == pl.num_programs(1) - 1)
    def _():
        sout_ref[...] = st_ref[...]


def _retention(q, k, v, state0, B, L, C, G=1):
    nc = L // C
    assert B % G == 0 and (G == 1 or nc == 1)
    f32 = np.float32
    lg = np.log(f32(1.0) - np.exp2(f32(-5.0) - np.arange(RET_HEADS, dtype=f32))).astype(f32)
    idx = np.arange(C, dtype=f32)
    diff = idx[:, None] - idx[None, :]
    dmat = np.where(diff[None] >= 0, np.exp(np.maximum(diff, 0)[None] * lg[:, None, None]), 0)
    xi = np.exp((idx[None, :] + f32(1.0)) * lg[:, None])
    zeta = np.exp((f32(C) - f32(1.0) - idx[None, :]) * lg[:, None])
    xi = np.broadcast_to(xi[:, :, None], (RET_HEADS, C, RET_VAL_DIM))
    zeta = np.broadcast_to(zeta[:, :, None], (RET_HEADS, C, RET_KEY_DIM))
    gam = np.broadcast_to(np.exp(f32(C) * lg)[:, None, None], (RET_HEADS, 1, RET_VAL_DIM))
    dmat, xi, zeta, gam = (jnp.asarray(a, F32) for a in (dmat, xi, zeta, gam))
    rowc = lambda w: pl.BlockSpec((G * C, w), lambda b, c: (b * nc + c, 0))
    st_spec = pl.BlockSpec((G, RET_HEADS, RET_KEY_DIM, RET_VAL_DIM), lambda b, c: (b, 0, 0, 0))
    return pl.pallas_call(
        _retention_kernel,
        grid=(B // G, nc),
        in_specs=[rowc(RET_KW), rowc(RET_KW), rowc(RET_VW), st_spec,
                  _const_spec(dmat.shape), _const_spec(xi.shape), _const_spec(zeta.shape),
                  _const_spec(gam.shape)],
        out_specs=(rowc(RET_VW), st_spec),
        out_shape=(jax.ShapeDtypeStruct((B * L, RET_VW), BF16),
                   jax.ShapeDtypeStruct((B, RET_HEADS, RET_KEY_DIM, RET_VAL_DIM), F32)),
        scratch_shapes=[pltpu.VMEM((G, RET_HEADS, RET_KEY_DIM, RET_VAL_DIM), F32)],
        compiler_params=pltpu.CompilerParams(dimension_semantics=("arbitrary", "arbitrary"),
                                             vmem_limit_bytes=VMEM_LIMIT),
        name="retention",
    )(q, k, v, state0, dmat, xi, zeta, gam)


def _mixer_ffn_kernel(x_ref, oa_ref, nb_ref, prev_ref, gmix_ref, wg_ref, gng_ref, wpa_ref, wpb_ref,
                      wo_ref, gffn_ref, wup_ref, cw_ref, cb_ref, wdn_ref, gfin_ref,
                      y_ref, conv_ref, carry_ref, ua_ref, ub_ref, acc_ref, h2_ref,
                      *, nseg, seglen):
    i = pl.program_id(0)
    NC = N_FFN_CHUNKS
    PADR = 8
    H0 = PADR - (CONV_WIDTH - 1)

    @pl.when(i == 0)
    def _():
        carry_ref[...] = prev_ref[...]

    x = x_ref[...]
    h = _rmsnorm(x, gmix_ref[...]).astype(BF16)
    zg = _dot_t(h, wg_ref[...])
    gb = zg[:, :RET_VW]
    gma = zg[:, RET_VW:RET_VW + D_MODEL]
    gmb = zg[:, RET_VW + D_MODEL:]
    oa = jnp.concatenate([oa_ref[p] for p in range(FOX_HEADS // 2)], axis=1)
    ya = _dot(oa, wpa_ref[...])
    nn = nb_ref[...].astype(F32) * gng_ref[...] * (gb * jax.nn.sigmoid(gb))
    yb = _dot(nn.astype(BF16), wpb_ref[...])
    y = jax.nn.sigmoid(gma) * ya + jax.nn.sigmoid(gmb) * yb
    x1 = x + _dot(y.astype(BF16), wo_ref[...])
    h2_ref[...] = _rmsnorm(x1, gffn_ref[...]).astype(BF16)
    acc_ref[...] = x1

    h2 = h2_ref[...]

    def up_half(u_ref, cidx, slot):
        u = _dot(h2, wup_ref[cidx])
        for s in range(nseg):
            u_ref[slot, s, PADR:PADR + seglen, :] = u[s * seglen:(s + 1) * seglen, :]
            u_ref[slot, s, H0:PADR, :] = carry_ref[cidx, s, H0:PADR, :]
            carry_ref[cidx, s, H0:PADR, :] = u[(s + 1) * seglen - (CONV_WIDTH - 1):(s + 1) * seglen, :]

    def conv_half(u_ref, cidx, slot):
        w = cw_ref[cidx]
        b = cb_ref[cidx]
        outs = []
        for s in range(nseg):
            acc = w[0:1] * u_ref[slot, s, H0:H0 + seglen, :]
            for jj in range(1, CONV_WIDTH):
                acc = acc + w[jj:jj + 1] * u_ref[slot, s, H0 + jj:H0 + jj + seglen, :]
            outs.append(b + acc)
        return outs[0] if nseg == 1 else jnp.concatenate(outs, axis=0)

    def stage_up(c):
        up_half(ua_ref, c, c % 2)
        up_half(ub_ref, NC + c, c % 2)

    stage_up(0)
    group = []
    for c in range(NC):
        if c + 1 < NC:
            stage_up(c + 1)
        a = conv_half(ua_ref, c, c % 2)
        b = conv_half(ub_ref, NC + c, c % 2)
        group.append((_gelu_tanh(a) * b).astype(BF16))
        if len(group) == DOWN_GROUP or c + 1 == NC:
            c0 = c + 1 - len(group)
            wd = wdn_ref[c0:c + 1].reshape(len(group) * FFN_CHUNK, D_MODEL)
            acc_ref[...] += _dot(jnp.concatenate(group, axis=1), wd)
            group = []
    y_ref[...] = _rmsnorm(acc_ref[...], gfin_ref[...])

    @pl.when(i == pl.num_programs(0) - 1)
    def _():
        conv_ref[...] = carry_ref[:, :, H0:PADR, :]


def _mixer_ffn(x, oa, nb, prev, weights, tm, nseg, seglen):
    M = x.shape[0]
    gmix, wg, gng, wpa, wpb, wo, gffn, wup, cw, cb, wdn, gfin = weights
    FC = FFN_CHUNK
    NC = N_FFN_CHUNKS
    row = lambda w: pl.BlockSpec((tm, w), lambda i: (i, 0))
    wspec = lambda a: pl.BlockSpec(a.shape, lambda i, n=a.ndim: (0,) * n,
                                   pipeline_mode=pl.Buffered(1))
    in_specs = [row(D_MODEL),
                pl.BlockSpec((FOX_HEADS // 2, tm, LANES), lambda i: (0, i, 0)),
                row(RET_VW), wspec(prev)] + [wspec(w) for w in weights]
    out_shape = (jax.ShapeDtypeStruct((M, D_MODEL), F32),
                 jax.ShapeDtypeStruct((2 * NC, nseg, CONV_WIDTH - 1, FC), F32))
    out_specs = (row(D_MODEL), _const_spec(out_shape[1].shape))
    return pl.pallas_call(
        functools.partial(_mixer_ffn_kernel, nseg=nseg, seglen=seglen),
        grid=(M // tm,),
        in_specs=in_specs,
        out_specs=out_specs,
        out_shape=out_shape,
        scratch_shapes=[pltpu.VMEM((2 * NC, nseg, 8, FC), F32),
                        pltpu.VMEM((2, nseg, 8 + seglen, FC), F32),
                        pltpu.VMEM((2, nseg, 8 + seglen, FC), F32),
                        pltpu.VMEM((tm, D_MODEL), F32),
                        pltpu.VMEM((tm, D_MODEL), BF16)],
        compiler_params=pltpu.CompilerParams(dimension_semantics=("arbitrary",),
                                             vmem_limit_bytes=VMEM_LIMIT),
        name="mixer_ffn",
    )(x, oa, nb, prev, *weights)


def _rotary_tables(start, n):
    half = RET_KEY_DIM // 2
    inv = 1.0 / (ROPE_BASE ** jnp.linspace(0.0, 1.0, half, dtype=F32))
    fine = min(n, ROT_FINE)
    assert n % fine == 0
    a_hi = (start + fine * jnp.arange(n // fine)).astype(F32)[:, None] * inv[None, :]
    a_lo = jnp.arange(fine).astype(F32)[:, None] * inv[None, :]
    ch, sh = jnp.cos(a_hi)[:, None, :], jnp.sin(a_hi)[:, None, :]
    cl, sl = jnp.cos(a_lo)[None, :, :], jnp.sin(a_lo)[None, :, :]
    cos = (ch * cl - sh * sl).reshape(n, half)
    sin = (sh * cl + ch * sl).reshape(n, half)
    return jnp.concatenate([cos, cos], axis=1), jnp.concatenate([-sin, sin], axis=1)


def _pad_heads(wt, pad):
    d = wt.shape[1]
    wt = wt.reshape(FOX_HEADS, FOX_HEAD_DIM, d)
    wt = jnp.pad(wt, ((0, 0), (0, pad - FOX_HEAD_DIM), (0, 0)))
    return wt.reshape(FOX_HEADS * pad, d)


def _prompt_consts(tm):
    tri = np.tril(np.ones((tm, tm), np.float32))
    eq = np.zeros((LANES, FOX_HEADS * HEAD_PAD), np.float32)
    ek = np.zeros((LANES, FOX_HEADS * HEAD_PAD), np.float32)
    oneq = np.zeros((1, FOX_HEADS * HEAD_PAD), np.float32)
    onek = np.zeros((1, FOX_HEADS * HEAD_PAD), np.float32)
    onev = np.zeros((1, FOX_HEADS * V_PAD), np.float32)
    for hh in range(FOX_HEADS):
        base = hh * HEAD_PAD + BIAS_COL
        for part in range(3):
            eq[part * FOX_HEADS + hh, base + part] = 1.0
            ek[part * FOX_HEADS + hh, base + 3 + part] = -1.0
            onek[0, base + part] = 1.0
            oneq[0, base + 3 + part] = 1.0
        onev[0, hh * V_PAD + FOX_HEAD_DIM] = 1.0
    return (jnp.asarray(tri, BF16), jnp.asarray(eq, BF16), jnp.asarray(ek, BF16),
            jnp.asarray(oneq), jnp.asarray(onek), jnp.asarray(onev))


def _chunk_cols(a):
    lead = a.shape[:-1]
    a = a.reshape(lead + (2 * N_FFN_CHUNKS, FFN_CHUNK))
    return jnp.moveaxis(a, -2, 0)


def _tile(n, pref):
    t = min(n, pref)
    while n % t:
        t //= 2
    return t


def kernel(x_prompt, x_sample, cache_fox_k, cache_fox_v, cache_fox_logf, state_ret, state_ffn_conv,
           norm_mix_g, w_in, b_fox_f, gn_ret_g, w_pa, w_pb, w_o, norm_ffn_g, w_up, conv_w, conv_b,
           w_down, norm_final_g):
    depth = w_in.shape[0]
    Bp, S, _ = x_prompt.shape
    Bs, Ts, _ = x_sample.shape
    P = cache_fox_k.shape[2]
    assert depth == 1 and Bp == 1, "kernel handles the single-layer, single-prompt configuration"
    l = 0

    wt = jnp.swapaxes(w_in[l], 0, 1).astype(BF16)
    o0 = 3 * FOX_W
    o1 = o0 + FOX_HEADS
    o2 = o1 + 2 * RET_KW + RET_VW
    wq_aug = _pad_heads(wt[:FOX_W], HEAD_PAD)
    wk_aug = _pad_heads(wt[FOX_W:2 * FOX_W], HEAD_PAD)
    wf = jnp.pad(wt[o0:o1], ((0, LANES - FOX_HEADS), (0, 0)))
    wvf = jnp.concatenate([_pad_heads(wt[2 * FOX_W:o0], V_PAD), wf], axis=0)
    wqkv = wt[:o0]
    bf = jnp.pad(b_fox_f[l].astype(F32), (0, LANES - FOX_HEADS))[None, :]
    wb = wt[o1:o2]
    wg = wt[o2:]
    gmix = norm_mix_g[l].astype(F32)[None, :]
    mix_weights = (
        gmix, wg, gn_ret_g[l].astype(F32)[None, :], w_pa[l].astype(BF16), w_pb[l].astype(BF16),
        w_o[l].astype(BF16), norm_ffn_g[l].astype(F32)[None, :],
        _chunk_cols(w_up[l]).astype(BF16),
        jnp.pad(_chunk_cols(conv_w[l].astype(F32)), ((0, 0), (0, 8 - CONV_WIDTH), (0, 0))),
        _chunk_cols(conv_b[l].astype(F32)[None, :]),
        w_down[l].reshape(N_FFN_CHUNKS, FFN_CHUNK, D_MODEL).astype(BF16),
        norm_final_g.astype(F32)[None, :],
    )

    tm_a = _tile(S, 512)
    cos_p, sin_p = _rotary_tables(0, S)
    (qT, ka, vT, kT_p, vT_p, logfT_p, qb, kb, vb, stats, sqq) = _inproj_prompt(
        x_prompt[0], gmix, wq_aug, wk_aug, wvf, bf, wb, cos_p, sin_p,
        _prompt_consts(tm_a), tm_a)
    oa_p = _fox_prompt(qT, ka, vT, stats, sqq, _tile(S, 1024))
    zero_state = jnp.zeros((1, RET_HEADS, RET_KEY_DIM, RET_VAL_DIM), F32)
    nb_p, ret_p = _retention(qb, kb, vb, zero_state, 1, S, _tile(S, 512))
    tm_d = _tile(S, 256)
    zero_prev = jnp.zeros((2 * N_FFN_CHUNKS, 1, 8, FFN_CHUNK), F32)
    y_p, conv_p = _mixer_ffn(x_prompt[0], oa_p, nb_p, zero_prev, mix_weights, tm_d, 1, tm_d)

    Ms = Bs * Ts
    cos_s, sin_s = _rotary_tables(P, Ts)
    cos_s = jnp.tile(cos_s, (Bs, 1))
    sin_s = jnp.tile(sin_s, (Bs, 1))
    (q_s, k_s, v_s, logf_s, qb_s, kb_s, vb_s) = _inproj_sample(
        x_sample.reshape(Ms, D_MODEL), gmix, wqkv, wf, bf, wb, cos_s, sin_s)
    KP = ((P + Ts + LANES - 1) // LANES) * LANES
    lf_all = jnp.concatenate([cache_fox_logf[l].astype(F32), logf_s.reshape(Bs, Ts, FOX_HEADS)], axis=1)
    lfT = jnp.pad(jnp.swapaxes(lf_all, 1, 2), ((0, 0), (0, 0), (0, KP - P - Ts)))
    oa_s = _fox_sample(q_s, k_s, v_s, jnp.transpose(cache_fox_k[l], (0, 2, 3, 1)),
                       jnp.transpose(cache_fox_v[l], (0, 2, 3, 1)), lfT, Bs, Ts)
    oa_s = jnp.moveaxis(oa_s.reshape(Ms, FOX_HEADS // 2, LANES), 1, 0)
    nb_s, ret_s = _retention(qb_s, kb_s, vb_s, state_ret[l].astype(F32), Bs, Ts, Ts,
                             G=math.gcd(Bs, 4))
    prev_s = _chunk_cols(state_ffn_conv[l].astype(F32))
    prev_s = jnp.pad(prev_s, ((0, 0), (0, 0), (8 - (CONV_WIDTH - 1), 0), (0, 0)))
    y_s, conv_s = _mixer_ffn(x_sample.reshape(Ms, D_MODEL), oa_s, nb_s, prev_s, mix_weights,
                             Ms, Bs, Ts)

    def unchunk(cv):
        return jnp.moveaxis(cv, 0, 2).reshape(cv.shape[1], CONV_WIDTH - 1, 2 * FFN_DIM)

    hshape = (FOX_HEADS, FOX_HEAD_DIM)
    return (
        y_p[None],
        y_s.reshape(Bs, Ts, D_MODEL),
        jnp.transpose(kT_p, (2, 0, 1))[None, None],
        jnp.transpose(vT_p, (2, 0, 1))[None, None],
        jnp.transpose(logfT_p, (1, 0))[None, None],
        ret_p[None],
        unchunk(conv_p)[None],
        k_s.reshape((1, Bs, Ts) + hshape),
        v_s.reshape((1, Bs, Ts) + hshape),
        logf_s.reshape(1, Bs, Ts, FOX_HEADS),
        ret_s[None],
        unchunk(conv_s)[None],
    )
```

```python
import functools
import math

import numpy as np
import jax
import jax.numpy as jnp
from jax import lax
from jax.experimental import pallas as pl
from jax.experimental.pallas import tpu as pltpu

F32 = jnp.float32
BF16 = jnp.bfloat16

D_MODEL = 1024
FOX_HEADS = 8
FOX_HEAD_DIM = 64
RET_HEADS = 4
RET_KEY_DIM = 128
RET_VAL_DIM = 256
FFN_DIM = 2816
CONV_WIDTH = 3
EPS = 1e-6
ROPE_BASE = 10000.0

FOX_W = FOX_HEADS * FOX_HEAD_DIM
RET_KW = RET_HEADS * RET_KEY_DIM
RET_VW = RET_HEADS * RET_VAL_DIM

LOG2E = 1.4426950408889634
LANES = 128
HEAD_PAD = LANES
V_PAD = 80
BIAS_COL = FOX_HEAD_DIM
NEG = -1e30
STALE_SAFE_LOG2 = 64.0
PRUNE_LOG2 = -160.0
NORM_SLACK = 1.02
ROT_FINE = 128
FFN_CHUNK = 256
N_FFN_CHUNKS = FFN_DIM // FFN_CHUNK
DOWN_GROUP = 4
VMEM_LIMIT = 56 * 1024 * 1024


def _rmsnorm(x, g):
    ms = jnp.mean(x * x, axis=-1, keepdims=True)
    return x * lax.rsqrt(ms + EPS) * g


def _split3(x):
    hi = x.astype(BF16)
    r1 = x - hi.astype(F32)
    mid = r1.astype(BF16)
    lo = (r1 - mid.astype(F32)).astype(BF16)
    return hi, mid, lo


def _log_sigmoid(x):
    return jnp.minimum(x, 0.0) - jnp.log1p(jnp.exp(-jnp.abs(x)))


def _dot(a, b):
    return jnp.dot(a, b, preferred_element_type=F32)


def _dot_t(a, bt):
    return lax.dot_general(a, bt, (((1,), (1,)), ((), ())), preferred_element_type=F32)


def _gelu_tanh(x):
    c0 = math.sqrt(2.0 / math.pi)
    hx = 0.5 * x
    return hx + hx * jnp.tanh(x * (c0 + (c0 * 0.044715) * (x * x)))


def _rotary(x, cos2, sin2):
    return x * cos2 + pltpu.roll(x, RET_KEY_DIM // 2, 1) * sin2


def _const_spec(shape):
    n = len(shape)
    return pl.BlockSpec(shape, lambda *_: (0,) * n)


def _inproj_prompt_kernel(x_ref, g_ref, wq_ref, wk_ref, wvf_ref, bf_ref, wb_ref,
                          cos_ref, sin_ref, tri_ref, eq_ref, ek_ref, oneq_ref, onek_ref, onev_ref,
                          qT_ref, ka_ref, vT_ref, kT32_ref, vT32_ref, logf_ref, qb_ref, kb_ref, vb_ref,
                          stats_ref, sqq_ref, carry_ref):
    tm = x_ref.shape[0]
    VW = FOX_HEADS * V_PAD

    @pl.when(pl.program_id(0) == 0)
    def _():
        carry_ref[...] = jnp.zeros_like(carry_ref)

    h = _rmsnorm(x_ref[...], g_ref[...]).astype(BF16)
    zvf = _dot_t(h, wvf_ref[...])

    logf = _log_sigmoid(zvf[:, VW:] + bf_ref[...])
    logf_ref[...] = logf.T[:FOX_HEADS, :]
    lane = lax.broadcasted_iota(jnp.int32, logf.shape, 1)
    logf = jnp.where(lane < FOX_HEADS, logf, 0.0)
    r = _dot(tri_ref[...], jnp.concatenate(_split3(logf), axis=1))
    c = r[:, :LANES] + r[:, LANES:2 * LANES] + r[:, 2 * LANES:] + carry_ref[...]
    carry_ref[...] = c[tm - 1:tm, :]
    c2 = c * LOG2E
    hi, mid, lo = (t.astype(F32) for t in _split3(c2))
    c3 = (hi + pltpu.roll(mid, FOX_HEADS, 1) + pltpu.roll(lo, 2 * FOX_HEADS, 1)).astype(BF16)

    q_aug = (_dot_t(h, wq_ref[...]) * (FOX_HEAD_DIM ** -0.5 * LOG2E)
             + _dot(c3, eq_ref[...]) + oneq_ref[...])
    qT32 = q_aug.T
    k_aug = _dot_t(h, wk_ref[...]) + _dot(c3, ek_ref[...]) + onek_ref[...]
    kT32 = k_aug.T
    vT32 = (zvf[:, :VW] + onev_ref[...]).T
    qT = qT32.astype(BF16)
    k_aug = k_aug.astype(BF16)
    vT = vT32.astype(BF16)

    lane1 = lax.broadcasted_iota(jnp.int32, (1, LANES), 1)
    nq2 = jnp.zeros((1, LANES), F32)
    nk2 = jnp.zeros((1, LANES), F32)
    for hh in range(FOX_HEADS):
        qT_ref[hh] = qT[hh * HEAD_PAD:(hh + 1) * HEAD_PAD, :]
        ka_ref[hh] = k_aug[:, hh * HEAD_PAD:(hh + 1) * HEAD_PAD]
        vT_ref[hh] = vT[hh * V_PAD:(hh + 1) * V_PAD, :]
        qh = qT32[hh * HEAD_PAD:hh * HEAD_PAD + FOX_HEAD_DIM, :]
        kh = kT32[hh * HEAD_PAD:hh * HEAD_PAD + FOX_HEAD_DIM, :]
        kT32_ref[hh] = kh
        vT32_ref[hh] = vT32[hh * V_PAD:hh * V_PAD + FOX_HEAD_DIM, :]
        sqq_ref[hh:hh + 1, :] = jnp.sum(qh * kh, axis=0, keepdims=True)
        q2 = jnp.max(jnp.sum(qh * qh, axis=0, keepdims=True), axis=1, keepdims=True)
        k2 = jnp.max(jnp.sum(kh * kh, axis=0, keepdims=True), axis=1, keepdims=True)
        nq2 = jnp.where(lane1 == hh, q2, nq2)
        nk2 = jnp.where(lane1 == hh, k2, nk2)
    stats_ref[0] = jnp.concatenate(
        [nq2, nk2, c2[0:1, :], c2[tm - 1:tm, :], jnp.zeros((4, LANES), F32)], axis=0)

    zb = _dot_t(h, wb_ref[...])
    cos2 = cos_ref[...]
    sin2 = sin_ref[...]
    for hh in range(RET_HEADS):
        sl = slice(hh * RET_KEY_DIM, (hh + 1) * RET_KEY_DIM)
        qb_ref[:, sl] = _rotary(zb[:, sl], cos2, sin2).astype(BF16)
        xk = zb[:, RET_KW + hh * RET_KEY_DIM:RET_KW + (hh + 1) * RET_KEY_DIM]
        kb_ref[:, sl] = (_rotary(xk, cos2, sin2) * (RET_KEY_DIM ** -0.5)).astype(BF16)
    vb_ref[...] = zb[:, 2 * RET_KW:].astype(BF16)


def _inproj_prompt(x, g, wq, wk, wvf, bf, wb, cos2, sin2, consts, tm):
    S = x.shape[0]
    tri, eq, ek, oneq, onek, onev = consts
    row = lambda w: pl.BlockSpec((tm, w), lambda i: (i, 0))
    headT = pl.BlockSpec((FOX_HEADS, FOX_HEAD_DIM, tm), lambda i: (0, 0, i))
    in_specs = [row(D_MODEL), _const_spec(g.shape), _const_spec(wq.shape), _const_spec(wk.shape),
                _const_spec(wvf.shape),
                _const_spec(bf.shape), _const_spec(wb.shape), row(LANES), row(LANES),
                _const_spec(tri.shape), _const_spec(eq.shape), _const_spec(ek.shape),
                _const_spec(oneq.shape), _const_spec(onek.shape), _const_spec(onev.shape)]
    out_shape = (
        jax.ShapeDtypeStruct((FOX_HEADS, HEAD_PAD, S), BF16),
        jax.ShapeDtypeStruct((FOX_HEADS, S, HEAD_PAD), BF16),
        jax.ShapeDtypeStruct((FOX_HEADS, V_PAD, S), BF16),
        jax.ShapeDtypeStruct((FOX_HEADS, FOX_HEAD_DIM, S), F32),
        jax.ShapeDtypeStruct((FOX_HEADS, FOX_HEAD_DIM, S), F32),
        jax.ShapeDtypeStruct((FOX_HEADS, S), F32),
        jax.ShapeDtypeStruct((S, RET_KW), BF16),
        jax.ShapeDtypeStruct((S, RET_KW), BF16),
        jax.ShapeDtypeStruct((S, RET_VW), BF16),
        jax.ShapeDtypeStruct((S // tm, 8, LANES), F32),
        jax.ShapeDtypeStruct((FOX_HEADS, S), F32),
    )
    out_specs = (
        pl.BlockSpec((FOX_HEADS, HEAD_PAD, tm), lambda i: (0, 0, i)),
        pl.BlockSpec((FOX_HEADS, tm, HEAD_PAD), lambda i: (0, i, 0)),
        pl.BlockSpec((FOX_HEADS, V_PAD, tm), lambda i: (0, 0, i)),
        headT, headT, pl.BlockSpec((FOX_HEADS, tm), lambda i: (0, i)),
        row(RET_KW), row(RET_KW), row(RET_VW),
        pl.BlockSpec((1, 8, LANES), lambda i: (i, 0, 0)),
        pl.BlockSpec((FOX_HEADS, tm), lambda i: (0, i)),
    )
    return pl.pallas_call(
        _inproj_prompt_kernel,
        grid=(S // tm,),
        in_specs=in_specs,
        out_specs=out_specs,
        out_shape=out_shape,
        scratch_shapes=[pltpu.VMEM((1, LANES), F32)],
        compiler_params=pltpu.CompilerParams(dimension_semantics=("arbitrary",),
                                             vmem_limit_bytes=VMEM_LIMIT),
        name="inproj_prompt",
    )(x, g, wq, wk, wvf, bf, wb, cos2, sin2, tri, eq, ek, oneq, onek, onev)


def _inproj_sample_kernel(x_ref, g_ref, wqkv_ref, wf_ref, bf_ref, wb_ref, cos_ref, sin_ref,
                          q_ref, k32_ref, v32_ref, logf_ref, qb_ref, kb_ref, vb_ref):
    h = _rmsnorm(x_ref[...], g_ref[...]).astype(BF16)
    logf = _log_sigmoid(_dot_t(h, wf_ref[...]) + bf_ref[...])
    logf_ref[...] = logf[:, :FOX_HEADS]
    z = _dot_t(h, wqkv_ref[...])
    q_ref[...] = (z[:, :FOX_W] * (FOX_HEAD_DIM ** -0.5 * LOG2E)).astype(BF16)
    k32_ref[...] = z[:, FOX_W:2 * FOX_W]
    v32_ref[...] = z[:, 2 * FOX_W:]
    zb = _dot_t(h, wb_ref[...])
    cos2 = cos_ref[...]
    sin2 = sin_ref[...]
    for hh in range(RET_HEADS):
        sl = slice(hh * RET_KEY_DIM, (hh + 1) * RET_KEY_DIM)
        qb_ref[:, sl] = _rotary(zb[:, sl], cos2, sin2).astype(BF16)
        xk = zb[:, RET_KW + hh * RET_KEY_DIM:RET_KW + (hh + 1) * RET_KEY_DIM]
        kb_ref[:, sl] = (_rotary(xk, cos2, sin2) * (RET_KEY_DIM ** -0.5)).astype(BF16)
    vb_ref[...] = zb[:, 2 * RET_KW:].astype(BF16)


def _inproj_sample(x, g, wqkv, wf, bf, wb, cos2, sin2):
    M = x.shape[0]
    args = (x, g, wqkv, wf, bf, wb, cos2, sin2)
    out_shape = (
        jax.ShapeDtypeStruct((M, FOX_W), BF16),
        jax.ShapeDtypeStruct((M, FOX_W), F32),
        jax.ShapeDtypeStruct((M, FOX_W), F32),
        jax.ShapeDtypeStruct((M, FOX_HEADS), F32),
        jax.ShapeDtypeStruct((M, RET_KW), BF16),
        jax.ShapeDtypeStruct((M, RET_KW), BF16),
        jax.ShapeDtypeStruct((M, RET_VW), BF16),
    )
    return pl.pallas_call(
        _inproj_sample_kernel,
        grid=(1,),
        in_specs=[_const_spec(a.shape) for a in args],
        out_specs=tuple(_const_spec(o.shape) for o in out_shape),
        out_shape=out_shape,
        compiler_params=pltpu.CompilerParams(dimension_semantics=("arbitrary",),
                                             vmem_limit_bytes=VMEM_LIMIT),
        name="inproj_sample",
    )(*args)


def _fox_prompt_kernel(it_ref, jt_ref, jfetch_ref, mode_ref, qT_ref, ka_ref, vT_ref,
                       sqq_ref, o_ref, m_ref, acc_ref, *, n_strips):
    del jfetch_ref
    t = pl.program_id(0)
    i = it_ref[t]
    j = jt_ref[t]
    T = qT_ref.shape[2]
    SUB = T // n_strips
    EXACT = n_strips + 1

    def heads(fn):
        def body(hh, carry):
            fn(hh, mode_ref[t * FOX_HEADS + hh])
            return carry
        lax.fori_loop(0, FOX_HEADS, body, 0)

    def scores(hh):
        return _dot(ka_ref[hh], qT_ref[hh])

    def exact_head(hh):
        s = scores(hh)
        kk = lax.broadcasted_iota(jnp.int32, s.shape, 0) + j * T
        qq = lax.broadcasted_iota(jnp.int32, s.shape, 1) + i * T
        s = jnp.where(kk > qq, NEG, s)
        m_old = m_ref[hh]
        m_new = jnp.maximum(m_old, jnp.max(s, axis=0, keepdims=True))
        p = jnp.exp2(s - m_new).astype(BF16)
        alpha = jnp.exp2(m_old - m_new)
        acc_ref[hh] = alpha * acc_ref[hh] + _dot(vT_ref[hh], p)
        m_ref[hh] = m_new


    def diag_head(hh, mode):
        m_ref[hh] = sqq_ref[pl.ds(hh, 1), :]

        @pl.when(mode != EXACT)
        def _():
            half = T // 2
            m = m_ref[hh]
            kk = lax.broadcasted_iota(jnp.int32, (half, half), 0)
            qq = lax.broadcasted_iota(jnp.int32, (half, half), 1)
            tri = kk > qq
            s_lo = _dot(ka_ref[hh, :half, :], qT_ref[hh])
            s_lo = jnp.concatenate([jnp.where(tri, NEG, s_lo[:, :half]), s_lo[:, half:]], axis=1)
            acc_ref[hh] = _dot(vT_ref[hh, :, :half], jnp.exp2(s_lo - m).astype(BF16))
            s_hi = jnp.where(tri, NEG, _dot(ka_ref[hh, half:, :], qT_ref[hh, :, half:]))
            p_hi = jnp.exp2(s_hi - m[:, half:]).astype(BF16)
            acc_ref[hh, :, half:] += _dot(vT_ref[hh, :, half:], p_hi)

        @pl.when(mode == EXACT)
        def _():
            acc_ref[hh] = jnp.zeros((V_PAD, T), F32)
            exact_head(hh)

    def off_head(hh, mode):
        for nn in range(1, n_strips + 1):
            k0 = (n_strips - nn) * SUB

            @pl.when(mode == nn)
            def _():
                p = jnp.exp2(_dot(ka_ref[hh, k0:, :], qT_ref[hh]) - m_ref[hh]).astype(BF16)
                acc_ref[hh] += _dot(vT_ref[hh, :, k0:], p)

        @pl.when(mode == EXACT)
        def _():
            exact_head(hh)

    @pl.when(j == i)
    def _():
        heads(diag_head)

    @pl.when(j < i)
    def _():
        heads(off_head)

    @pl.when(j == 0)
    def _():
        for pr in range(FOX_HEADS // 2):
            halves = []
            for hh in (2 * pr, 2 * pr + 1):
                a = acc_ref[hh]
                halves.append(a[:FOX_HEAD_DIM] / a[FOX_HEAD_DIM:FOX_HEAD_DIM + 1])
            o_ref[pr] = jnp.concatenate(halves, axis=0).T.astype(BF16)


def _prune_tables(stats, it, jt, nb):
    per = stats.shape[0] // nb
    st = stats.reshape(nb, per, 8, LANES)[:, :, :, :FOX_HEADS]
    nq = jnp.sqrt(jnp.max(st[:, :, 0, :], axis=1)) * NORM_SLACK
    nk_strip = jnp.sqrt(st[:, :, 1, :]) * NORM_SLACK
    nk = jnp.max(nk_strip, axis=1)
    c_first = st[:, 0, 2, :]
    c_last = st[:, :, 3, :]
    bound = (nq[it][:, None, :] * (nk_strip[jt] + nk[it][:, None, :])
             - (c_last[jt] - c_first[it][:, None, :]))
    live = jnp.logical_or(jnp.asarray(jt == it)[:, None, None],
                          jnp.logical_not(bound < PRUNE_LOG2))
    strip_no = jnp.arange(per, dtype=jnp.int32)[None, :, None]
    n_keep = per - jnp.min(jnp.where(live, strip_no, per), axis=1)
    safe = nq[it] * (nk[jt] + nk[it]) < STALE_SAFE_LOG2
    mode = jnp.where(n_keep == 0, 0, jnp.where(safe, n_keep, per + 1))
    steps = jnp.arange(len(it), dtype=jnp.int32)
    last_live = lax.cummax(jnp.where(jnp.any(n_keep > 0, axis=1), steps, 0))
    return jnp.asarray(jt)[last_live], mode.astype(jnp.int32).reshape(-1), per


def _fox_prompt(qT, ka, vT, stats, sqq, T):
    S = ka.shape[1]
    nb = S // T
    it = np.array([i for i in range(nb) for _ in range(i + 1)], np.int32)
    jt = np.array([j for i in range(nb) for j in range(i, -1, -1)], np.int32)
    jfetch, mode, n_strips = _prune_tables(stats, it, jt, nb)
    grid_spec = pltpu.PrefetchScalarGridSpec(
        num_scalar_prefetch=4,
        grid=(len(it),),
        in_specs=[
            pl.BlockSpec((FOX_HEADS, HEAD_PAD, T), lambda t, it, jt, jf, md: (0, 0, it[t])),
            pl.BlockSpec((FOX_HEADS, T, HEAD_PAD), lambda t, it, jt, jf, md: (0, jf[t], 0)),
            pl.BlockSpec((FOX_HEADS, V_PAD, T), lambda t, it, jt, jf, md: (0, 0, jf[t])),
            pl.BlockSpec((FOX_HEADS, T), lambda t, it, jt, jf, md: (0, it[t])),
        ],
        out_specs=pl.BlockSpec((FOX_HEADS // 2, T, LANES),
                               lambda t, it, jt, jf, md: (0, it[t], 0)),
        scratch_shapes=[pltpu.VMEM((FOX_HEADS, 1, T), F32),
                        pltpu.VMEM((FOX_HEADS, V_PAD, T), F32)],
    )
    return pl.pallas_call(
        functools.partial(_fox_prompt_kernel, n_strips=n_strips),
        grid_spec=grid_spec,
        out_shape=jax.ShapeDtypeStruct((FOX_HEADS // 2, S, LANES), BF16),
        compiler_params=pltpu.CompilerParams(dimension_semantics=("arbitrary",),
                                             vmem_limit_bytes=VMEM_LIMIT),
        name="fox_prompt",
    )(jnp.asarray(it), jnp.asarray(jt), jfetch, mode, qT, ka, vT, sqq)


def _fox_sample_kernel(q_ref, kn_ref, vn_ref, ckT_ref, cvT_ref, lfT_ref, up_ref, ex_ref, o_ref):
    P = ckT_ref.shape[3]
    Tn = q_ref.shape[0]
    KP = lfT_ref.shape[2]
    HQ = FOX_HEADS * Tn
    nchunk = KP // LANES
    nt = (((1,), (1,)), ((), ()))

    def stack3(x):
        parts3 = [t.astype(F32) for t in _split3(x)] + [jnp.zeros_like(x)]
        return jnp.concatenate(parts3, axis=0).astype(BF16)

    x3 = stack3(lfT_ref[0])
    up = up_ref[...]
    run = jnp.zeros((4 * FOX_HEADS, 1), F32)
    parts = []
    for cidx in range(nchunk):
        y = _dot(x3[:, cidx * LANES:(cidx + 1) * LANES], up) + run
        parts.append(y)
        run = y[:, LANES - 1:LANES]
    y = jnp.concatenate(parts, axis=1)
    cT = (y[:FOX_HEADS] + y[FOX_HEADS:2 * FOX_HEADS] + y[2 * FOX_HEADS:3 * FOX_HEADS]) * LOG2E
    ckx = _dot(ex_ref[...], stack3(cT))

    tail = ckx[:, P:P + LANES]
    rowq = lax.broadcasted_iota(jnp.int32, tail.shape, 0) % Tn
    lanek = lax.broadcasted_iota(jnp.int32, tail.shape, 1)
    cq = jnp.sum(jnp.where(lanek == rowq, tail, 0.0), axis=1, keepdims=True)

    q = q_ref[...]
    qt = jnp.concatenate([q] * FOX_HEADS, axis=0)
    rh = lax.broadcasted_iota(jnp.int32, qt.shape, 0) // Tn
    lh = lax.broadcasted_iota(jnp.int32, qt.shape, 1) // FOX_HEAD_DIM
    qbd = jnp.where(rh == lh, qt, jnp.zeros_like(qt))

    kT = ckT_ref[0].reshape(FOX_W, P).astype(BF16)
    vT = cvT_ref[0].reshape(FOX_W, P).astype(BF16)
    s_c = _dot(qbd, kT) + cq - ckx[:, :P]
    s_n = lax.dot_general(qbd, kn_ref[...].astype(BF16), nt, preferred_element_type=F32)
    s_n = s_n + cq - ckx[:, P:P + Tn]
    key = lax.broadcasted_iota(jnp.int32, s_n.shape, 1)
    qrow = lax.broadcasted_iota(jnp.int32, s_n.shape, 0) % Tn
    s_n = jnp.where(key > qrow, NEG, s_n)
    m = jnp.maximum(jnp.max(s_c, axis=1, keepdims=True), jnp.max(s_n, axis=1, keepdims=True))
    p_c = jnp.exp2(s_c - m)
    p_n = jnp.exp2(s_n - m)
    l = jnp.sum(p_c, axis=1, keepdims=True) + jnp.sum(p_n, axis=1, keepdims=True)
    z = lax.dot_general(p_c.astype(BF16), vT, nt, preferred_element_type=F32)
    z = (z + _dot(p_n.astype(BF16), vn_ref[...].astype(BF16))) / l
    zh = lax.broadcasted_iota(jnp.int32, (Tn, FOX_W), 1) // FOX_HEAD_DIM
    o = jnp.zeros((Tn, FOX_W), F32)
    for hh in range(FOX_HEADS):
        o = o + jnp.where(zh == hh, z[hh * Tn:(hh + 1) * Tn, :], 0.0)
    o_ref[...] = o.astype(BF16)


def _fox_sample(q, kn, vn, cache_kT, cache_vT, lfT, B, Tn):
    P = cache_kT.shape[3]
    KP = lfT.shape[2]
    HQ = FOX_HEADS * Tn
    up = jnp.asarray(np.triu(np.ones((LANES, LANES), np.float32)), BF16)
    ex = np.zeros((HQ, 4 * FOX_HEADS), np.float32)
    for part in range(3):
        for hh in range(FOX_HEADS):
            ex[hh * Tn:(hh + 1) * Tn, part * FOX_HEADS + hh] = 1.0
    ex = jnp.asarray(ex, BF16)
    rowb = lambda w: pl.BlockSpec((Tn, w), lambda b: (b, 0))
    return pl.pallas_call(
        _fox_sample_kernel,
        grid=(B,),
        in_specs=[rowb(FOX_W), rowb(FOX_W), rowb(FOX_W),
                  pl.BlockSpec((1, FOX_HEADS, FOX_HEAD_DIM, P), lambda b: (b, 0, 0, 0)),
                  pl.BlockSpec((1, FOX_HEADS, FOX_HEAD_DIM, P), lambda b: (b, 0, 0, 0)),
                  pl.BlockSpec((1, FOX_HEADS, KP), lambda b: (b, 0, 0)),
                  _const_spec(up.shape), _const_spec(ex.shape)],
        out_specs=rowb(FOX_W),
        out_shape=jax.ShapeDtypeStruct((B * Tn, FOX_W), BF16),
        compiler_params=pltpu.CompilerParams(dimension_semantics=("arbitrary",),
                                             vmem_limit_bytes=VMEM_LIMIT),
        name="fox_sample",
    )(q, kn, vn, cache_kT, cache_vT, lfT, up, ex)


def _retention_kernel(q_ref, k_ref, v_ref, s0_ref, dmat_ref, xi_ref, zeta_ref, gam_ref,
                      n_ref, sout_ref, st_ref):
    c = pl.program_id(1)
    G = st_ref.shape[0]
    C = q_ref.shape[0] // G

    @pl.when(c == 0)
    def _():
        st_ref[...] = s0_ref[...]

    for g in range(G):
        rows = slice(g * C, (g + 1) * C)
        for hh in range(RET_HEADS):
            q = q_ref[rows, hh * RET_KEY_DIM:(hh + 1) * RET_KEY_DIM]
            k = k_ref[rows, hh * RET_KEY_DIM:(hh + 1) * RET_KEY_DIM]
            v = v_ref[rows, hh * RET_VAL_DIM:(hh + 1) * RET_VAL_DIM]
            st = st_ref[g, hh]
            sc = lax.dot_general(q, k, (((1,), (1,)), ((), ())), preferred_element_type=F32)
            sc = sc * dmat_ref[hh]
            o = _dot(sc.astype(BF16), v) + _dot(q, st.astype(BF16)) * xi_ref[hh]
            kz = (k.astype(F32) * zeta_ref[hh]).astype(BF16)
            upd = lax.dot_general(kz, v, (((0,), (0,)), ((), ())), preferred_element_type=F32)
            st_ref[g, hh] = gam_ref[hh] * st + upd
            mu = jnp.mean(o, axis=-1, keepdims=True)
            d = o - mu
            var = jnp.mean(d * d, axis=-1, keepdims=True)
            n_ref[rows, hh * RET_VAL_DIM:(hh + 1) * RET_VAL_DIM] = (
                d * lax.rsqrt(var + EPS)).astype(BF16)

    @pl.when(c == pl.num_programs(1) - 1)
    def _():
        sout_ref[...] = st_ref[...]


def _retention(q, k, v, state0, B, L, C, G=1):
    nc = L // C
    assert B % G == 0 and (G == 1 or nc == 1)
    f32 = np.float32
    lg = np.log(f32(1.0) - np.exp2(f32(-5.0) - np.arange(RET_HEADS, dtype=f32))).astype(f32)
    idx = np.arange(C, dtype=f32)
    diff = idx[:, None] - idx[None, :]
    dmat = np.where(diff[None] >= 0, np.exp(np.maximum(diff, 0)[None] * lg[:, None, None]), 0)
    xi = np.exp((idx[None, :] + f32(1.0)) * lg[:, None])
    zeta = np.exp((f32(C) - f32(1.0) - idx[None, :]) * lg[:, None])
    xi = np.broadcast_to(xi[:, :, None], (RET_HEADS, C, RET_VAL_DIM))
    zeta = np.broadcast_to(zeta[:, :, None], (RET_HEADS, C, RET_KEY_DIM))
    gam = np.broadcast_to(np.exp(f32(C) * lg)[:, None, None], (RET_HEADS, 1, RET_VAL_DIM))
    dmat, xi, zeta, gam = (jnp.asarray(a, F32) for a in (dmat, xi, zeta, gam))
    rowc = lambda w: pl.BlockSpec((G * C, w), lambda b, c: (b * nc + c, 0))
    st_spec = pl.BlockSpec((G, RET_HEADS, RET_KEY_DIM, RET_VAL_DIM), lambda b, c: (b, 0, 0, 0))
    return pl.pallas_call(
        _retention_kernel,
        grid=(B // G, nc),
        in_specs=[rowc(RET_KW), rowc(RET_KW), rowc(RET_VW), st_spec,
                  _const_spec(dmat.shape), _const_spec(xi.shape), _const_spec(zeta.shape),
                  _const_spec(gam.shape)],
        out_specs=(rowc(RET_VW), st_spec),
        out_shape=(jax.ShapeDtypeStruct((B * L, RET_VW), BF16),
                   jax.ShapeDtypeStruct((B, RET_HEADS, RET_KEY_DIM, RET_VAL_DIM), F32)),
        scratch_shapes=[pltpu.VMEM((G, RET_HEADS, RET_KEY_DIM, RET_VAL_DIM), F32)],
        compiler_params=pltpu.CompilerParams(dimension_semantics=("arbitrary", "arbitrary"),
                                             vmem_limit_bytes=VMEM_LIMIT),
        name="retention",
    )(q, k, v, state0, dmat, xi, zeta, gam)


def _mixer_ffn_kernel(x_ref, oa_ref, nb_ref, prev_ref, gmix_ref, wg_ref, gng_ref, wpa_ref, wpb_ref,
                      wo_ref, gffn_ref, wup_ref, cw_ref, cb_ref, wdn_ref, gfin_ref,
                      y_ref, conv_ref, carry_ref, ua_ref, ub_ref, acc_ref, h2_ref,
                      *, nseg, seglen):
    i = pl.program_id(0)
    NC = N_FFN_CHUNKS
    PADR = 8
    H0 = PADR - (CONV_WIDTH - 1)

    @pl.when(i == 0)
    def _():
        carry_ref[...] = prev_ref[...]

    x = x_ref[...]
    h = _rmsnorm(x, gmix_ref[...]).astype(BF16)
    zg = _dot_t(h, wg_ref[...])
    gb = zg[:, :RET_VW]
    gma = zg[:, RET_VW:RET_VW + D_MODEL]
    gmb = zg[:, RET_VW + D_MODEL:]
    oa = jnp.concatenate([oa_ref[p] for p in range(FOX_HEADS // 2)], axis=1)
    ya = _dot(oa, wpa_ref[...])
    nn = nb_ref[...].astype(F32) * gng_ref[...] * (gb * jax.nn.sigmoid(gb))
    yb = _dot(nn.astype(BF16), wpb_ref[...])
    y = jax.nn.sigmoid(gma) * ya + jax.nn.sigmoid(gmb) * yb
    x1 = x + _dot(y.astype(BF16), wo_ref[...])
    h2_ref[...] = _rmsnorm(x1, gffn_ref[...]).astype(BF16)
    acc_ref[...] = x1

    h2 = h2_ref[...]

    def cols(cidx):
        return slice(cidx * FFN_CHUNK, (cidx + 1) * FFN_CHUNK)

    def up_half(u_ref, cidx, slot):
        u = _dot(h2, wup_ref[:, cols(cidx)])
        for s in range(nseg):
            u_ref[slot, s, PADR:PADR + seglen, :] = u[s * seglen:(s + 1) * seglen, :]
            u_ref[slot, s, H0:PADR, :] = carry_ref[s, H0:PADR, cols(cidx)]
            carry_ref[s, H0:PADR, cols(cidx)] = (
                u[(s + 1) * seglen - (CONV_WIDTH - 1):(s + 1) * seglen, :])

    def conv_half(u_ref, cidx, slot):
        w = cw_ref[:, cols(cidx)]
        b = cb_ref[:, cols(cidx)]
        outs = []
        for s in range(nseg):
            acc = w[0:1] * u_ref[slot, s, H0:H0 + seglen, :]
            for jj in range(1, CONV_WIDTH):
                acc = acc + w[jj:jj + 1] * u_ref[slot, s, H0 + jj:H0 + jj + seglen, :]
            outs.append(b + acc)
        return outs[0] if nseg == 1 else jnp.concatenate(outs, axis=0)

    def stage_up(c):
        up_half(ua_ref, c, c % 2)
        up_half(ub_ref, NC + c, c % 2)

    stage_up(0)
    group = []
    for c in range(NC):
        if c + 1 < NC:
            stage_up(c + 1)
        a = conv_half(ua_ref, c, c % 2)
        b = conv_half(ub_ref, NC + c, c % 2)
        group.append((_gelu_tanh(a) * b).astype(BF16))
        if len(group) == DOWN_GROUP or c + 1 == NC:
            r0 = (c + 1 - len(group)) * FFN_CHUNK
            wd = wdn_ref[r0:(c + 1) * FFN_CHUNK, :]
            acc_ref[...] += _dot(jnp.concatenate(group, axis=1), wd)
            group = []
    y_ref[...] = _rmsnorm(acc_ref[...], gfin_ref[...])

    @pl.when(i == pl.num_programs(0) - 1)
    def _():
        conv_ref[...] = carry_ref[:, H0:PADR, :]


def _mixer_ffn(x, oa, nb, prev, weights, tm, nseg, seglen):
    M = x.shape[0]
    gmix, wg, gng, wpa, wpb, wo, gffn, wup, cw, cb, wdn, gfin = weights
    FC = FFN_CHUNK
    NC = N_FFN_CHUNKS
    row = lambda w: pl.BlockSpec((tm, w), lambda i: (i, 0))
    wspec = lambda a: pl.BlockSpec(a.shape, lambda i, n=a.ndim: (0,) * n,
                                   pipeline_mode=pl.Buffered(1))
    in_specs = [row(D_MODEL),
                pl.BlockSpec((FOX_HEADS // 2, tm, LANES), lambda i: (0, i, 0)),
                row(RET_VW), wspec(prev)] + [wspec(w) for w in weights]
    out_shape = (jax.ShapeDtypeStruct((M, D_MODEL), F32),
                 jax.ShapeDtypeStruct((nseg, CONV_WIDTH - 1, 2 * FFN_DIM), F32))
    out_specs = (row(D_MODEL), _const_spec(out_shape[1].shape))
    return pl.pallas_call(
        functools.partial(_mixer_ffn_kernel, nseg=nseg, seglen=seglen),
        grid=(M // tm,),
        in_specs=in_specs,
        out_specs=out_specs,
        out_shape=out_shape,
        scratch_shapes=[pltpu.VMEM((nseg, 8, 2 * FFN_DIM), F32),
                        pltpu.VMEM((2, nseg, 8 + seglen, FC), F32),
                        pltpu.VMEM((2, nseg, 8 + seglen, FC), F32),
                        pltpu.VMEM((tm, D_MODEL), F32),
                        pltpu.VMEM((tm, D_MODEL), BF16)],
        compiler_params=pltpu.CompilerParams(dimension_semantics=("arbitrary",),
                                             vmem_limit_bytes=VMEM_LIMIT),
        name="mixer_ffn",
    )(x, oa, nb, prev, *weights)


def _rotary_tables(start, n):
    half = RET_KEY_DIM // 2
    inv = 1.0 / (ROPE_BASE ** jnp.linspace(0.0, 1.0, half, dtype=F32))
    fine = min(n, ROT_FINE)
    assert n % fine == 0
    a_hi = (start + fine * jnp.arange(n // fine)).astype(F32)[:, None] * inv[None, :]
    a_lo = jnp.arange(fine).astype(F32)[:, None] * inv[None, :]
    ch, sh = jnp.cos(a_hi)[:, None, :], jnp.sin(a_hi)[:, None, :]
    cl, sl = jnp.cos(a_lo)[None, :, :], jnp.sin(a_lo)[None, :, :]
    cos = (ch * cl - sh * sl).reshape(n, half)
    sin = (sh * cl + ch * sl).reshape(n, half)
    return jnp.concatenate([cos, cos], axis=1), jnp.concatenate([-sin, sin], axis=1)


def _pad_heads(wt, pad):
    d = wt.shape[1]
    wt = wt.reshape(FOX_HEADS, FOX_HEAD_DIM, d)
    wt = jnp.pad(wt, ((0, 0), (0, pad - FOX_HEAD_DIM), (0, 0)))
    return wt.reshape(FOX_HEADS * pad, d)


def _prompt_consts(tm):
    tri = np.tril(np.ones((tm, tm), np.float32))
    eq = np.zeros((LANES, FOX_HEADS * HEAD_PAD), np.float32)
    ek = np.zeros((LANES, FOX_HEADS * HEAD_PAD), np.float32)
    oneq = np.zeros((1, FOX_HEADS * HEAD_PAD), np.float32)
    onek = np.zeros((1, FOX_HEADS * HEAD_PAD), np.float32)
    onev = np.zeros((1, FOX_HEADS * V_PAD), np.float32)
    for hh in range(FOX_HEADS):
        base = hh * HEAD_PAD + BIAS_COL
        for part in range(3):
            eq[part * FOX_HEADS + hh, base + part] = 1.0
            ek[part * FOX_HEADS + hh, base + 3 + part] = -1.0
            onek[0, base + part] = 1.0
            oneq[0, base + 3 + part] = 1.0
        onev[0, hh * V_PAD + FOX_HEAD_DIM] = 1.0
    return (jnp.asarray(tri, BF16), jnp.asarray(eq, BF16), jnp.asarray(ek, BF16),
            jnp.asarray(oneq), jnp.asarray(onek), jnp.asarray(onev))


def _tile(n, pref):
    t = min(n, pref)
    while n % t:
        t //= 2
    return t


def kernel(x_prompt, x_sample, cache_fox_k, cache_fox_v, cache_fox_logf, state_ret, state_ffn_conv,
           norm_mix_g, w_in, b_fox_f, gn_ret_g, w_pa, w_pb, w_o, norm_ffn_g, w_up, conv_w, conv_b,
           w_down, norm_final_g):
    depth = w_in.shape[0]
    Bp, S, _ = x_prompt.shape
    Bs, Ts, _ = x_sample.shape
    P = cache_fox_k.shape[2]
    assert depth == 1 and Bp == 1, "kernel handles the single-layer, single-prompt configuration"
    l = 0

    wt = jnp.swapaxes(w_in[l], 0, 1).astype(BF16)
    o0 = 3 * FOX_W
    o1 = o0 + FOX_HEADS
    o2 = o1 + 2 * RET_KW + RET_VW
    wq_aug = _pad_heads(wt[:FOX_W], HEAD_PAD)
    wk_aug = _pad_heads(wt[FOX_W:2 * FOX_W], HEAD_PAD)
    wf = jnp.pad(wt[o0:o1], ((0, LANES - FOX_HEADS), (0, 0)))
    wvf = jnp.concatenate([_pad_heads(wt[2 * FOX_W:o0], V_PAD), wf], axis=0)
    wqkv = wt[:o0]
    bf = jnp.pad(b_fox_f[l].astype(F32), (0, LANES - FOX_HEADS))[None, :]
    wb = wt[o1:o2]
    wg = wt[o2:]
    gmix = norm_mix_g[l].astype(F32)[None, :]
    mix_weights = (
        gmix, wg, gn_ret_g[l].astype(F32)[None, :], w_pa[l].astype(BF16), w_pb[l].astype(BF16),
        w_o[l].astype(BF16), norm_ffn_g[l].astype(F32)[None, :],
        w_up[l].astype(BF16),
        jnp.pad(conv_w[l].astype(F32), ((0, 8 - CONV_WIDTH), (0, 0))),
        conv_b[l].astype(F32)[None, :],
        w_down[l].astype(BF16),
        norm_final_g.astype(F32)[None, :],
    )
    hist_pad = ((0, 0), (8 - (CONV_WIDTH - 1), 0), (0, 0))

    tm_a = _tile(S, 512)
    cos_p, sin_p = _rotary_tables(0, S)
    (qT, ka, vT, kT_p, vT_p, logfT_p, qb, kb, vb, stats, sqq) = _inproj_prompt(
        x_prompt[0], gmix, wq_aug, wk_aug, wvf, bf, wb, cos_p, sin_p,
        _prompt_consts(tm_a), tm_a)
    oa_p = _fox_prompt(qT, ka, vT, stats, sqq, _tile(S, 1024))
    zero_state = jnp.zeros((1, RET_HEADS, RET_KEY_DIM, RET_VAL_DIM), F32)
    nb_p, ret_p = _retention(qb, kb, vb, zero_state, 1, S, _tile(S, 512))
    tm_d = _tile(S, 256)
    zero_prev = jnp.zeros((1, 8, 2 * FFN_DIM), F32)
    y_p, conv_p = _mixer_ffn(x_prompt[0], oa_p, nb_p, zero_prev, mix_weights, tm_d, 1, tm_d)

    Ms = Bs * Ts
    cos_s, sin_s = _rotary_tables(P, Ts)
    cos_s = jnp.tile(cos_s, (Bs, 1))
    sin_s = jnp.tile(sin_s, (Bs, 1))
    (q_s, k_s, v_s, logf_s, qb_s, kb_s, vb_s) = _inproj_sample(
        x_sample.reshape(Ms, D_MODEL), gmix, wqkv, wf, bf, wb, cos_s, sin_s)
    KP = ((P + Ts + LANES - 1) // LANES) * LANES
    lf_all = jnp.concatenate([cache_fox_logf[l].astype(F32), logf_s.reshape(Bs, Ts, FOX_HEADS)], axis=1)
    lfT = jnp.pad(jnp.swapaxes(lf_all, 1, 2), ((0, 0), (0, 0), (0, KP - P - Ts)))
    oa_s = _fox_sample(q_s, k_s, v_s, jnp.transpose(cache_fox_k[l], (0, 2, 3, 1)),
                       jnp.transpose(cache_fox_v[l], (0, 2, 3, 1)), lfT, Bs, Ts)
    oa_s = jnp.moveaxis(oa_s.reshape(Ms, FOX_HEADS // 2, LANES), 1, 0)
    nb_s, ret_s = _retention(qb_s, kb_s, vb_s, state_ret[l].astype(F32), Bs, Ts, Ts,
                             G=math.gcd(Bs, 4))
    prev_s = jnp.pad(state_ffn_conv[l].astype(F32), hist_pad)
    y_s, conv_s = _mixer_ffn(x_sample.reshape(Ms, D_MODEL), oa_s, nb_s, prev_s, mix_weights,
                             Ms, Bs, Ts)

    hshape = (FOX_HEADS, FOX_HEAD_DIM)
    return (
        y_p[None],
        y_s.reshape(Bs, Ts, D_MODEL),
        jnp.transpose(kT_p, (2, 0, 1))[None, None],
        jnp.transpose(vT_p, (2, 0, 1))[None, None],
        jnp.transpose(logfT_p, (1, 0))[None, None],
        ret_p[None],
        conv_p[None],
        k_s.reshape((1, Bs, Ts) + hshape),
        v_s.reshape((1, Bs, Ts) + hshape),
        logf_s.reshape(1, Bs, Ts, FOX_HEADS),
        ret_s[None],
        conv_s[None],
    )
```

```python
import functools
import math

import numpy as np
import jax
import jax.numpy as jnp
from jax import lax
from jax.experimental import pallas as pl
from jax.experimental.pallas import tpu as pltpu

F32 = jnp.float32
BF16 = jnp.bfloat16

D_MODEL = 1024
FOX_HEADS = 8
FOX_HEAD_DIM = 64
RET_HEADS = 4
RET_KEY_DIM = 128
RET_VAL_DIM = 256
FFN_DIM = 2816
CONV_WIDTH = 3
EPS = 1e-6
ROPE_BASE = 10000.0

FOX_W = FOX_HEADS * FOX_HEAD_DIM
RET_KW = RET_HEADS * RET_KEY_DIM
RET_VW = RET_HEADS * RET_VAL_DIM

LOG2E = 1.4426950408889634
LANES = 128
HEAD_PAD = LANES
V_PAD = 80
BIAS_COL = FOX_HEAD_DIM
NEG = -1e30
STALE_SAFE_LOG2 = 64.0
PRUNE_LOG2 = -160.0
NORM_SLACK = 1.02
ROT_FINE = 128
FFN_CHUNK = 256
N_FFN_CHUNKS = FFN_DIM // FFN_CHUNK
DOWN_GROUP = 4
VMEM_LIMIT = 56 * 1024 * 1024


def _rmsnorm(x, g):
    ms = jnp.mean(x * x, axis=-1, keepdims=True)
    return x * lax.rsqrt(ms + EPS) * g


def _split3(x):
    hi = x.astype(BF16)
    r1 = x - hi.astype(F32)
    mid = r1.astype(BF16)
    lo = (r1 - mid.astype(F32)).astype(BF16)
    return hi, mid, lo


def _log_sigmoid(x):
    return jnp.minimum(x, 0.0) - jnp.log1p(jnp.exp(-jnp.abs(x)))


def _dot(a, b):
    return jnp.dot(a, b, preferred_element_type=F32)


def _dot_t(a, bt):
    return lax.dot_general(a, bt, (((1,), (1,)), ((), ())), preferred_element_type=F32)


def _gelu_tanh(x):
    c0 = math.sqrt(2.0 / math.pi)
    hx = 0.5 * x
    return hx + hx * jnp.tanh(x * (c0 + (c0 * 0.044715) * (x * x)))


def _rotary(x, cos2, sin2):
    return x * cos2 + pltpu.roll(x, RET_KEY_DIM // 2, 1) * sin2


def _const_spec(shape):
    n = len(shape)
    return pl.BlockSpec(shape, lambda *_: (0,) * n)


def _inproj_prompt_kernel(x_ref, g_ref, wq_ref, wk_ref, wvf_ref, bf_ref, wb_ref,
                          cos_ref, sin_ref, tri_ref, eq_ref, ek_ref, oneq_ref, onek_ref, onev_ref,
                          qT_ref, ka_ref, vT_ref, kT32_ref, vT32_ref, logf_ref, qb_ref, kb_ref, vb_ref,
                          stats_ref, sqq_ref, carry_ref):
    tm = x_ref.shape[0]
    VW = FOX_HEADS * V_PAD

    @pl.when(pl.program_id(0) == 0)
    def _():
        carry_ref[...] = jnp.zeros_like(carry_ref)

    h = _rmsnorm(x_ref[...], g_ref[...]).astype(BF16)
    zvf = _dot_t(h, wvf_ref[...])

    logf = _log_sigmoid(zvf[:, VW:] + bf_ref[...])
    logf_ref[...] = logf.T[:FOX_HEADS, :]
    lane = lax.broadcasted_iota(jnp.int32, logf.shape, 1)
    logf = jnp.where(lane < FOX_HEADS, logf, 0.0)
    r = _dot(tri_ref[...], jnp.concatenate(_split3(logf), axis=1))
    c = r[:, :LANES] + r[:, LANES:2 * LANES] + r[:, 2 * LANES:] + carry_ref[...]
    carry_ref[...] = c[tm - 1:tm, :]
    c2 = c * LOG2E
    hi, mid, lo = (t.astype(F32) for t in _split3(c2))
    c3 = (hi + pltpu.roll(mid, FOX_HEADS, 1) + pltpu.roll(lo, 2 * FOX_HEADS, 1)).astype(BF16)

    q_aug = (_dot_t(h, wq_ref[...]) * (FOX_HEAD_DIM ** -0.5 * LOG2E)
             + _dot(c3, eq_ref[...]) + oneq_ref[...])
    qT32 = q_aug.T
    k_aug = _dot_t(h, wk_ref[...]) + _dot(c3, ek_ref[...]) + onek_ref[...]
    kT32 = k_aug.T
    vT32 = (zvf[:, :VW] + onev_ref[...]).T
    qT = qT32.astype(BF16)
    k_aug = k_aug.astype(BF16)
    vT = vT32.astype(BF16)

    lane1 = lax.broadcasted_iota(jnp.int32, (1, LANES), 1)
    nq2 = jnp.zeros((1, LANES), F32)
    nk2 = jnp.zeros((1, LANES), F32)
    for hh in range(FOX_HEADS):
        qT_ref[hh] = qT[hh * HEAD_PAD:(hh + 1) * HEAD_PAD, :]
        ka_ref[hh] = k_aug[:, hh * HEAD_PAD:(hh + 1) * HEAD_PAD]
        vT_ref[hh] = vT[hh * V_PAD:(hh + 1) * V_PAD, :]
        qh = qT32[hh * HEAD_PAD:hh * HEAD_PAD + FOX_HEAD_DIM, :]
        kh = kT32[hh * HEAD_PAD:hh * HEAD_PAD + FOX_HEAD_DIM, :]
        kT32_ref[hh] = kh
        vT32_ref[hh] = vT32[hh * V_PAD:hh * V_PAD + FOX_HEAD_DIM, :]
        sqq_ref[hh:hh + 1, :] = jnp.sum(qh * kh, axis=0, keepdims=True)
        q2 = jnp.max(jnp.sum(qh * qh, axis=0, keepdims=True), axis=1, keepdims=True)
        k2 = jnp.max(jnp.sum(kh * kh, axis=0, keepdims=True), axis=1, keepdims=True)
        nq2 = jnp.where(lane1 == hh, q2, nq2)
        nk2 = jnp.where(lane1 == hh, k2, nk2)
    stats_ref[0] = jnp.concatenate(
        [nq2, nk2, c2[0:1, :], c2[tm - 1:tm, :], jnp.zeros((4, LANES), F32)], axis=0)

    zb = _dot_t(h, wb_ref[...])
    cos2 = cos_ref[...]
    sin2 = sin_ref[...]
    for hh in range(RET_HEADS):
        sl = slice(hh * RET_KEY_DIM, (hh + 1) * RET_KEY_DIM)
        qb_ref[:, sl] = _rotary(zb[:, sl], cos2, sin2).astype(BF16)
        xk = zb[:, RET_KW + hh * RET_KEY_DIM:RET_KW + (hh + 1) * RET_KEY_DIM]
        kb_ref[:, sl] = (_rotary(xk, cos2, sin2) * (RET_KEY_DIM ** -0.5)).astype(BF16)
    vb_ref[...] = zb[:, 2 * RET_KW:].astype(BF16)


def _inproj_prompt(x, g, wq, wk, wvf, bf, wb, cos2, sin2, consts, tm):
    S = x.shape[0]
    tri, eq, ek, oneq, onek, onev = consts
    row = lambda w: pl.BlockSpec((tm, w), lambda i: (i, 0))
    headT = pl.BlockSpec((FOX_HEADS, FOX_HEAD_DIM, tm), lambda i: (0, 0, i))
    in_specs = [row(D_MODEL), _const_spec(g.shape), _const_spec(wq.shape), _const_spec(wk.shape),
                _const_spec(wvf.shape),
                _const_spec(bf.shape), _const_spec(wb.shape), row(LANES), row(LANES),
                _const_spec(tri.shape), _const_spec(eq.shape), _const_spec(ek.shape),
                _const_spec(oneq.shape), _const_spec(onek.shape), _const_spec(onev.shape)]
    out_shape = (
        jax.ShapeDtypeStruct((FOX_HEADS, HEAD_PAD, S), BF16),
        jax.ShapeDtypeStruct((FOX_HEADS, S, HEAD_PAD), BF16),
        jax.ShapeDtypeStruct((FOX_HEADS, V_PAD, S), BF16),
        jax.ShapeDtypeStruct((FOX_HEADS, FOX_HEAD_DIM, S), F32),
        jax.ShapeDtypeStruct((FOX_HEADS, FOX_HEAD_DIM, S), F32),
        jax.ShapeDtypeStruct((FOX_HEADS, S), F32),
        jax.ShapeDtypeStruct((S, RET_KW), BF16),
        jax.ShapeDtypeStruct((S, RET_KW), BF16),
        jax.ShapeDtypeStruct((S, RET_VW), BF16),
        jax.ShapeDtypeStruct((S // tm, 8, LANES), F32),
        jax.ShapeDtypeStruct((FOX_HEADS, S), F32),
    )
    out_specs = (
        pl.BlockSpec((FOX_HEADS, HEAD_PAD, tm), lambda i: (0, 0, i)),
        pl.BlockSpec((FOX_HEADS, tm, HEAD_PAD), lambda i: (0, i, 0)),
        pl.BlockSpec((FOX_HEADS, V_PAD, tm), lambda i: (0, 0, i)),
        headT, headT, pl.BlockSpec((FOX_HEADS, tm), lambda i: (0, i)),
        row(RET_KW), row(RET_KW), row(RET_VW),
        pl.BlockSpec((1, 8, LANES), lambda i: (i, 0, 0)),
        pl.BlockSpec((FOX_HEADS, tm), lambda i: (0, i)),
    )
    return pl.pallas_call(
        _inproj_prompt_kernel,
        grid=(S // tm,),
        in_specs=in_specs,
        out_specs=out_specs,
        out_shape=out_shape,
        scratch_shapes=[pltpu.VMEM((1, LANES), F32)],
        compiler_params=pltpu.CompilerParams(dimension_semantics=("arbitrary",),
                                             vmem_limit_bytes=VMEM_LIMIT),
        name="inproj_prompt",
    )(x, g, wq, wk, wvf, bf, wb, cos2, sin2, tri, eq, ek, oneq, onek, onev)


def _inproj_sample_kernel(x_ref, g_ref, wqkv_ref, wf_ref, bf_ref, wb_ref, cos_ref, sin_ref,
                          q_ref, k32_ref, v32_ref, logf_ref, qb_ref, kb_ref, vb_ref):
    h = _rmsnorm(x_ref[...], g_ref[...]).astype(BF16)
    logf = _log_sigmoid(_dot_t(h, wf_ref[...]) + bf_ref[...])
    logf_ref[...] = logf[:, :FOX_HEADS]
    z = _dot_t(h, wqkv_ref[...])
    q_ref[...] = (z[:, :FOX_W] * (FOX_HEAD_DIM ** -0.5 * LOG2E)).astype(BF16)
    k32_ref[...] = z[:, FOX_W:2 * FOX_W]
    v32_ref[...] = z[:, 2 * FOX_W:]
    zb = _dot_t(h, wb_ref[...])
    cos2 = cos_ref[...]
    sin2 = sin_ref[...]
    for hh in range(RET_HEADS):
        sl = slice(hh * RET_KEY_DIM, (hh + 1) * RET_KEY_DIM)
        qb_ref[:, sl] = _rotary(zb[:, sl], cos2, sin2).astype(BF16)
        xk = zb[:, RET_KW + hh * RET_KEY_DIM:RET_KW + (hh + 1) * RET_KEY_DIM]
        kb_ref[:, sl] = (_rotary(xk, cos2, sin2) * (RET_KEY_DIM ** -0.5)).astype(BF16)
    vb_ref[...] = zb[:, 2 * RET_KW:].astype(BF16)


def _inproj_sample(x, g, wqkv, wf, bf, wb, cos2, sin2):
    M = x.shape[0]
    args = (x, g, wqkv, wf, bf, wb, cos2, sin2)
    out_shape = (
        jax.ShapeDtypeStruct((M, FOX_W), BF16),
        jax.ShapeDtypeStruct((M, FOX_W), F32),
        jax.ShapeDtypeStruct((M, FOX_W), F32),
        jax.ShapeDtypeStruct((M, FOX_HEADS), F32),
        jax.ShapeDtypeStruct((M, RET_KW), BF16),
        jax.ShapeDtypeStruct((M, RET_KW), BF16),
        jax.ShapeDtypeStruct((M, RET_VW), BF16),
    )
    return pl.pallas_call(
        _inproj_sample_kernel,
        grid=(1,),
        in_specs=[_const_spec(a.shape) for a in args],
        out_specs=tuple(_const_spec(o.shape) for o in out_shape),
        out_shape=out_shape,
        compiler_params=pltpu.CompilerParams(dimension_semantics=("arbitrary",),
                                             vmem_limit_bytes=VMEM_LIMIT),
        name="inproj_sample",
    )(*args)


def _fox_prompt_kernel(it_ref, jt_ref, jfetch_ref, mode_ref, qT_ref, ka_ref, vT_ref,
                       sqq_ref, o_ref, m_ref, acc_ref, *, n_strips):
    del jfetch_ref
    t = pl.program_id(0)
    i = it_ref[t]
    j = jt_ref[t]
    T = qT_ref.shape[2]
    SUB = T // n_strips
    EXACT = n_strips + 1

    def heads(fn):
        def body(hh, carry):
            fn(hh, mode_ref[t * FOX_HEADS + hh])
            return carry
        lax.fori_loop(0, FOX_HEADS, body, 0)

    def scores(hh):
        return _dot(ka_ref[hh], qT_ref[hh])

    def exact_head(hh):
        s = scores(hh)
        kk = lax.broadcasted_iota(jnp.int32, s.shape, 0) + j * T
        qq = lax.broadcasted_iota(jnp.int32, s.shape, 1) + i * T
        s = jnp.where(kk > qq, NEG, s)
        m_old = m_ref[hh]
        m_new = jnp.maximum(m_old, jnp.max(s, axis=0, keepdims=True))
        p = jnp.exp2(s - m_new).astype(BF16)
        alpha = jnp.exp2(m_old - m_new)
        acc_ref[hh] = alpha * acc_ref[hh] + _dot(vT_ref[hh], p)
        m_ref[hh] = m_new


    def diag_head(hh, mode):
        m_ref[hh] = sqq_ref[pl.ds(hh, 1), :]

        @pl.when(mode != EXACT)
        def _():
            half = T // 2
            m = m_ref[hh]
            kk = lax.broadcasted_iota(jnp.int32, (half, half), 0)
            qq = lax.broadcasted_iota(jnp.int32, (half, half), 1)
            tri = kk > qq
            s_lo = _dot(ka_ref[hh, :half, :], qT_ref[hh])
            s_lo = jnp.concatenate([jnp.where(tri, NEG, s_lo[:, :half]), s_lo[:, half:]], axis=1)
            acc_ref[hh] = _dot(vT_ref[hh, :, :half], jnp.exp2(s_lo - m).astype(BF16))
            s_hi = jnp.where(tri, NEG, _dot(ka_ref[hh, half:, :], qT_ref[hh, :, half:]))
            p_hi = jnp.exp2(s_hi - m[:, half:]).astype(BF16)
            acc_ref[hh, :, half:] += _dot(vT_ref[hh, :, half:], p_hi)

        @pl.when(mode == EXACT)
        def _():
            acc_ref[hh] = jnp.zeros((V_PAD, T), F32)
            exact_head(hh)

    def off_head(hh, mode):
        for nn in range(1, n_strips + 1):
            k0 = (n_strips - nn) * SUB

            @pl.when(mode == nn)
            def _():
                p = jnp.exp2(_dot(ka_ref[hh, k0:, :], qT_ref[hh]) - m_ref[hh]).astype(BF16)
                acc_ref[hh] += _dot(vT_ref[hh, :, k0:], p)

        @pl.when(mode == EXACT)
        def _():
            exact_head(hh)

    @pl.when(j == i)
    def _():
        heads(diag_head)

    @pl.when(j < i)
    def _():
        heads(off_head)

    @pl.when(j == 0)
    def _():
        for pr in range(FOX_HEADS // 2):
            halves = []
            for hh in (2 * pr, 2 * pr + 1):
                a = acc_ref[hh]
                halves.append(a[:FOX_HEAD_DIM] / a[FOX_HEAD_DIM:FOX_HEAD_DIM + 1])
            o_ref[pr] = jnp.concatenate(halves, axis=0).T.astype(BF16)


def _prune_tables(stats, it, jt, nb):
    per = stats.shape[0] // nb
    st = stats.reshape(nb, per, 8, LANES)[:, :, :, :FOX_HEADS]
    nq = jnp.sqrt(jnp.max(st[:, :, 0, :], axis=1)) * NORM_SLACK
    nk_strip = jnp.sqrt(st[:, :, 1, :]) * NORM_SLACK
    nk = jnp.max(nk_strip, axis=1)
    c_first = st[:, 0, 2, :]
    c_last = st[:, :, 3, :]
    bound = (nq[it][:, None, :] * (nk_strip[jt] + nk[it][:, None, :])
             - (c_last[jt] - c_first[it][:, None, :]))
    live = jnp.logical_or(jnp.asarray(jt == it)[:, None, None],
                          jnp.logical_not(bound < PRUNE_LOG2))
    strip_no = jnp.arange(per, dtype=jnp.int32)[None, :, None]
    n_keep = per - jnp.min(jnp.where(live, strip_no, per), axis=1)
    safe = nq[it] * (nk[jt] + nk[it]) < STALE_SAFE_LOG2
    mode = jnp.where(n_keep == 0, 0, jnp.where(safe, n_keep, per + 1))
    steps = jnp.arange(len(it), dtype=jnp.int32)
    last_live = lax.cummax(jnp.where(jnp.any(n_keep > 0, axis=1), steps, 0))
    return jnp.asarray(jt)[last_live], mode.astype(jnp.int32).reshape(-1), per


def _fox_prompt(qT, ka, vT, stats, sqq, T):
    S = ka.shape[1]
    nb = S // T
    it = np.array([i for i in range(nb) for _ in range(i + 1)], np.int32)
    jt = np.array([j for i in range(nb) for j in range(i, -1, -1)], np.int32)
    jfetch, mode, n_strips = _prune_tables(stats, it, jt, nb)
    grid_spec = pltpu.PrefetchScalarGridSpec(
        num_scalar_prefetch=4,
        grid=(len(it),),
        in_specs=[
            pl.BlockSpec((FOX_HEADS, HEAD_PAD, T), lambda t, it, jt, jf, md: (0, 0, it[t])),
            pl.BlockSpec((FOX_HEADS, T, HEAD_PAD), lambda t, it, jt, jf, md: (0, jf[t], 0)),
            pl.BlockSpec((FOX_HEADS, V_PAD, T), lambda t, it, jt, jf, md: (0, 0, jf[t])),
            pl.BlockSpec((FOX_HEADS, T), lambda t, it, jt, jf, md: (0, it[t])),
        ],
        out_specs=pl.BlockSpec((FOX_HEADS // 2, T, LANES),
                               lambda t, it, jt, jf, md: (0, it[t], 0)),
        scratch_shapes=[pltpu.VMEM((FOX_HEADS, 1, T), F32),
                        pltpu.VMEM((FOX_HEADS, V_PAD, T), F32)],
    )
    return pl.pallas_call(
        functools.partial(_fox_prompt_kernel, n_strips=n_strips),
        grid_spec=grid_spec,
        out_shape=jax.ShapeDtypeStruct((FOX_HEADS // 2, S, LANES), BF16),
        compiler_params=pltpu.CompilerParams(dimension_semantics=("arbitrary",),
                                             vmem_limit_bytes=VMEM_LIMIT),
        name="fox_prompt",
    )(jnp.asarray(it), jnp.asarray(jt), jfetch, mode, qT, ka, vT, sqq)


def _fox_sample_kernel(q_ref, kn_ref, vn_ref, ckT_ref, cvT_ref, lfT_ref, up_ref, ex_ref, o_ref):
    P = ckT_ref.shape[3]
    Tn = q_ref.shape[0]
    KP = lfT_ref.shape[2]
    HQ = FOX_HEADS * Tn
    nchunk = KP // LANES
    nt = (((1,), (1,)), ((), ()))

    def stack3(x):
        parts3 = [t.astype(F32) for t in _split3(x)] + [jnp.zeros_like(x)]
        return jnp.concatenate(parts3, axis=0).astype(BF16)

    x3 = stack3(lfT_ref[0])
    up = up_ref[...]
    parts = [_dot(x3[:, cidx * LANES:(cidx + 1) * LANES], up) for cidx in range(nchunk)]
    run = jnp.zeros((4 * FOX_HEADS, 1), F32)
    for cidx in range(nchunk):
        total = parts[cidx][:, LANES - 1:LANES]
        parts[cidx] = parts[cidx] + run
        run = run + total
    y = jnp.concatenate(parts, axis=1)
    cT = (y[:FOX_HEADS] + y[FOX_HEADS:2 * FOX_HEADS] + y[2 * FOX_HEADS:3 * FOX_HEADS]) * LOG2E
    ckx = _dot(ex_ref[...], stack3(cT))

    tail = ckx[:, P:P + LANES]
    rowq = lax.broadcasted_iota(jnp.int32, tail.shape, 0) % Tn
    lanek = lax.broadcasted_iota(jnp.int32, tail.shape, 1)
    cq = jnp.sum(jnp.where(lanek == rowq, tail, 0.0), axis=1, keepdims=True)

    q = q_ref[...]
    qt = jnp.concatenate([q] * FOX_HEADS, axis=0)
    rh = lax.broadcasted_iota(jnp.int32, qt.shape, 0) // Tn
    lh = lax.broadcasted_iota(jnp.int32, qt.shape, 1) // FOX_HEAD_DIM
    qbd = jnp.where(rh == lh, qt, jnp.zeros_like(qt))

    kT = ckT_ref[0].reshape(FOX_W, P).astype(BF16)
    vT = cvT_ref[0].reshape(FOX_W, P).astype(BF16)
    s_c = _dot(qbd, kT) + cq - ckx[:, :P]
    s_n = lax.dot_general(qbd, kn_ref[...].astype(BF16), nt, preferred_element_type=F32)
    s_n = s_n + cq - ckx[:, P:P + Tn]
    key = lax.broadcasted_iota(jnp.int32, s_n.shape, 1)
    qrow = lax.broadcasted_iota(jnp.int32, s_n.shape, 0) % Tn
    s_n = jnp.where(key > qrow, NEG, s_n)
    m = jnp.maximum(jnp.max(s_c, axis=1, keepdims=True), jnp.max(s_n, axis=1, keepdims=True))
    p_c = jnp.exp2(s_c - m)
    p_n = jnp.exp2(s_n - m)
    l = jnp.sum(p_c, axis=1, keepdims=True) + jnp.sum(p_n, axis=1, keepdims=True)
    z = lax.dot_general(p_c.astype(BF16), vT, nt, preferred_element_type=F32)
    z = (z + _dot(p_n.astype(BF16), vn_ref[...].astype(BF16))) / l
    zh = lax.broadcasted_iota(jnp.int32, (Tn, FOX_W), 1) // FOX_HEAD_DIM
    o = jnp.zeros((Tn, FOX_W), F32)
    for hh in range(FOX_HEADS):
        o = o + jnp.where(zh == hh, z[hh * Tn:(hh + 1) * Tn, :], 0.0)
    o_ref[...] = o.astype(BF16)


def _fox_sample(q, kn, vn, cache_kT, cache_vT, lfT, B, Tn):
    P = cache_kT.shape[3]
    KP = lfT.shape[2]
    HQ = FOX_HEADS * Tn
    up = jnp.asarray(np.triu(np.ones((LANES, LANES), np.float32)), BF16)
    ex = np.zeros((HQ, 4 * FOX_HEADS), np.float32)
    for part in range(3):
        for hh in range(FOX_HEADS):
            ex[hh * Tn:(hh + 1) * Tn, part * FOX_HEADS + hh] = 1.0
    ex = jnp.asarray(ex, BF16)
    rowb = lambda w: pl.BlockSpec((Tn, w), lambda b: (b, 0))
    return pl.pallas_call(
        _fox_sample_kernel,
        grid=(B,),
        in_specs=[rowb(FOX_W), rowb(FOX_W), rowb(FOX_W),
                  pl.BlockSpec((1, FOX_HEADS, FOX_HEAD_DIM, P), lambda b: (b, 0, 0, 0)),
                  pl.BlockSpec((1, FOX_HEADS, FOX_HEAD_DIM, P), lambda b: (b, 0, 0, 0)),
                  pl.BlockSpec((1, FOX_HEADS, KP), lambda b: (b, 0, 0)),
                  _const_spec(up.shape), _const_spec(ex.shape)],
        out_specs=rowb(FOX_W),
        out_shape=jax.ShapeDtypeStruct((B * Tn, FOX_W), BF16),
        compiler_params=pltpu.CompilerParams(dimension_semantics=("arbitrary",),
                                             vmem_limit_bytes=VMEM_LIMIT),
        name="fox_sample",
    )(q, kn, vn, cache_kT, cache_vT, lfT, up, ex)


def _retention_kernel(q_ref, k_ref, v_ref, s0_ref, dmat_ref, xi_ref, zeta_ref, gam_ref,
                      n_ref, sout_ref, st_ref):
    c = pl.program_id(1)
    G = st_ref.shape[0]
    C = q_ref.shape[0] // G

    @pl.when(c == 0)
    def _():
        st_ref[...] = s0_ref[...]

    for g in range(G):
        rows = slice(g * C, (g + 1) * C)
        for hh in range(RET_HEADS):
            q = q_ref[rows, hh * RET_KEY_DIM:(hh + 1) * RET_KEY_DIM]
            k = k_ref[rows, hh * RET_KEY_DIM:(hh + 1) * RET_KEY_DIM]
            v = v_ref[rows, hh * RET_VAL_DIM:(hh + 1) * RET_VAL_DIM]
            st = st_ref[g, hh]
            sc = lax.dot_general(q, k, (((1,), (1,)), ((), ())), preferred_element_type=F32)
            sc = sc * dmat_ref[hh]
            o = _dot(sc.astype(BF16), v) + _dot(q, st.astype(BF16)) * xi_ref[hh]
            kz = (k.astype(F32) * zeta_ref[hh]).astype(BF16)
            upd = lax.dot_general(kz, v, (((0,), (0,)), ((), ())), preferred_element_type=F32)
            st_ref[g, hh] = gam_ref[hh] * st + upd
            mu = jnp.mean(o, axis=-1, keepdims=True)
            d = o - mu
            var = jnp.mean(d * d, axis=-1, keepdims=True)
            n_ref[rows, hh * RET_VAL_DIM:(hh + 1) * RET_VAL_DIM] = (
                d * lax.rsqrt(var + EPS)).astype(BF16)

    @pl.when(c == pl.num_programs(1) - 1)
    def _():
        sout_ref[...] = st_ref[...]


def _retention(q, k, v, state0, B, L, C, G=1):
    nc = L // C
    assert B % G == 0 and (G == 1 or nc == 1)
    f32 = np.float32
    lg = np.log(f32(1.0) - np.exp2(f32(-5.0) - np.arange(RET_HEADS, dtype=f32))).astype(f32)
    idx = np.arange(C, dtype=f32)
    diff = idx[:, None] - idx[None, :]
    dmat = np.where(diff[None] >= 0, np.exp(np.maximum(diff, 0)[None] * lg[:, None, None]), 0)
    xi = np.exp((idx[None, :] + f32(1.0)) * lg[:, None])
    zeta = np.exp((f32(C) - f32(1.0) - idx[None, :]) * lg[:, None])
    xi = np.broadcast_to(xi[:, :, None], (RET_HEADS, C, RET_VAL_DIM))
    zeta = np.broadcast_to(zeta[:, :, None], (RET_HEADS, C, RET_KEY_DIM))
    gam = np.broadcast_to(np.exp(f32(C) * lg)[:, None, None], (RET_HEADS, 1, RET_VAL_DIM))
    dmat, xi, zeta, gam = (jnp.asarray(a, F32) for a in (dmat, xi, zeta, gam))
    rowc = lambda w: pl.BlockSpec((G * C, w), lambda b, c: (b * nc + c, 0))
    st_spec = pl.BlockSpec((G, RET_HEADS, RET_KEY_DIM, RET_VAL_DIM), lambda b, c: (b, 0, 0, 0))
    return pl.pallas_call(
        _retention_kernel,
        grid=(B // G, nc),
        in_specs=[rowc(RET_KW), rowc(RET_KW), rowc(RET_VW), st_spec,
                  _const_spec(dmat.shape), _const_spec(xi.shape), _const_spec(zeta.shape),
                  _const_spec(gam.shape)],
        out_specs=(rowc(RET_VW), st_spec),
        out_shape=(jax.ShapeDtypeStruct((B * L, RET_VW), BF16),
                   jax.ShapeDtypeStruct((B, RET_HEADS, RET_KEY_DIM, RET_VAL_DIM), F32)),
        scratch_shapes=[pltpu.VMEM((G, RET_HEADS, RET_KEY_DIM, RET_VAL_DIM), F32)],
        compiler_params=pltpu.CompilerParams(dimension_semantics=("arbitrary", "arbitrary"),
                                             vmem_limit_bytes=VMEM_LIMIT),
        name="retention",
    )(q, k, v, state0, dmat, xi, zeta, gam)


def _mixer_ffn_kernel(x_ref, oa_ref, nb_ref, prev_ref, gmix_ref, wg_ref, gng_ref, wpa_ref, wpb_ref,
                      wo_ref, gffn_ref, wup_ref, cw_ref, cb_ref, wdn_ref, gfin_ref,
                      y_ref, conv_ref, carry_ref, ua_ref, ub_ref, acc_ref, h2_ref,
                      *, nseg, seglen):
    i = pl.program_id(0)
    NC = N_FFN_CHUNKS
    PADR = 8
    H0 = PADR - (CONV_WIDTH - 1)

    @pl.when(i == 0)
    def _():
        carry_ref[...] = prev_ref[...]

    x = x_ref[...]
    h = _rmsnorm(x, gmix_ref[...]).astype(BF16)
    zg = _dot_t(h, wg_ref[...])
    gb = zg[:, :RET_VW]
    gma = zg[:, RET_VW:RET_VW + D_MODEL]
    gmb = zg[:, RET_VW + D_MODEL:]
    oa = jnp.concatenate([oa_ref[p] for p in range(FOX_HEADS // 2)], axis=1)
    ya = _dot(oa, wpa_ref[...])
    nn = nb_ref[...].astype(F32) * gng_ref[...] * (gb * jax.nn.sigmoid(gb))
    yb = _dot(nn.astype(BF16), wpb_ref[...])
    y = jax.nn.sigmoid(gma) * ya + jax.nn.sigmoid(gmb) * yb
    x1 = x + _dot(y.astype(BF16), wo_ref[...])
    h2_ref[...] = _rmsnorm(x1, gffn_ref[...]).astype(BF16)
    acc_ref[...] = x1

    h2 = h2_ref[...]

    def cols(cidx):
        return slice(cidx * FFN_CHUNK, (cidx + 1) * FFN_CHUNK)

    def up_half(u_ref, cidx, slot):
        u = _dot(h2, wup_ref[:, cols(cidx)])
        for s in range(nseg):
            u_ref[slot, s, PADR:PADR + seglen, :] = u[s * seglen:(s + 1) * seglen, :]
            u_ref[slot, s, H0:PADR, :] = carry_ref[s, H0:PADR, cols(cidx)]
            carry_ref[s, H0:PADR, cols(cidx)] = (
                u[(s + 1) * seglen - (CONV_WIDTH - 1):(s + 1) * seglen, :])

    def conv_half(u_ref, cidx, slot):
        w = cw_ref[:, cols(cidx)]
        b = cb_ref[:, cols(cidx)]
        outs = []
        for s in range(nseg):
            acc = w[0:1] * u_ref[slot, s, H0:H0 + seglen, :]
            for jj in range(1, CONV_WIDTH):
                acc = acc + w[jj:jj + 1] * u_ref[slot, s, H0 + jj:H0 + jj + seglen, :]
            outs.append(b + acc)
        return outs[0] if nseg == 1 else jnp.concatenate(outs, axis=0)

    def stage_up(c):
        up_half(ua_ref, c, c % 2)
        up_half(ub_ref, NC + c, c % 2)

    stage_up(0)
    group = []
    for c in range(NC):
        if c + 1 < NC:
            stage_up(c + 1)
        a = conv_half(ua_ref, c, c % 2)
        b = conv_half(ub_ref, NC + c, c % 2)
        group.append((_gelu_tanh(a) * b).astype(BF16))
        if len(group) == DOWN_GROUP or c + 1 == NC:
            r0 = (c + 1 - len(group)) * FFN_CHUNK
            wd = wdn_ref[r0:(c + 1) * FFN_CHUNK, :]
            acc_ref[...] += _dot(jnp.concatenate(group, axis=1), wd)
            group = []
    y_ref[...] = _rmsnorm(acc_ref[...], gfin_ref[...])

    @pl.when(i == pl.num_programs(0) - 1)
    def _():
        conv_ref[...] = carry_ref[:, H0:PADR, :]


def _mixer_ffn(x, oa, nb, prev, weights, tm, nseg, seglen):
    M = x.shape[0]
    gmix, wg, gng, wpa, wpb, wo, gffn, wup, cw, cb, wdn, gfin = weights
    FC = FFN_CHUNK
    NC = N_FFN_CHUNKS
    row = lambda w: pl.BlockSpec((tm, w), lambda i: (i, 0))
    wspec = lambda a: pl.BlockSpec(a.shape, lambda i, n=a.ndim: (0,) * n,
                                   pipeline_mode=pl.Buffered(1))
    in_specs = [row(D_MODEL),
                pl.BlockSpec((FOX_HEADS // 2, tm, LANES), lambda i: (0, i, 0)),
                row(RET_VW), wspec(prev)] + [wspec(w) for w in weights]
    out_shape = (jax.ShapeDtypeStruct((M, D_MODEL), F32),
                 jax.ShapeDtypeStruct((nseg, CONV_WIDTH - 1, 2 * FFN_DIM), F32))
    out_specs = (row(D_MODEL), _const_spec(out_shape[1].shape))
    return pl.pallas_call(
        functools.partial(_mixer_ffn_kernel, nseg=nseg, seglen=seglen),
        grid=(M // tm,),
        in_specs=in_specs,
        out_specs=out_specs,
        out_shape=out_shape,
        scratch_shapes=[pltpu.VMEM((nseg, 8, 2 * FFN_DIM), F32),
                        pltpu.VMEM((2, nseg, 8 + seglen, FC), F32),
                        pltpu.VMEM((2, nseg, 8 + seglen, FC), F32),
                        pltpu.VMEM((tm, D_MODEL), F32),
                        pltpu.VMEM((tm, D_MODEL), BF16)],
        compiler_params=pltpu.CompilerParams(dimension_semantics=("arbitrary",),
                                             vmem_limit_bytes=VMEM_LIMIT),
        name="mixer_ffn",
    )(x, oa, nb, prev, *weights)


def _rotary_tables(start, n):
    half = RET_KEY_DIM // 2
    inv = 1.0 / (ROPE_BASE ** jnp.linspace(0.0, 1.0, half, dtype=F32))
    fine = min(n, ROT_FINE)
    assert n % fine == 0
    a_hi = (start + fine * jnp.arange(n // fine)).astype(F32)[:, None] * inv[None, :]
    a_lo = jnp.arange(fine).astype(F32)[:, None] * inv[None, :]
    ch, sh = jnp.cos(a_hi)[:, None, :], jnp.sin(a_hi)[:, None, :]
    cl, sl = jnp.cos(a_lo)[None, :, :], jnp.sin(a_lo)[None, :, :]
    cos = (ch * cl - sh * sl).reshape(n, half)
    sin = (sh * cl + ch * sl).reshape(n, half)
    return jnp.concatenate([cos, cos], axis=1), jnp.concatenate([-sin, sin], axis=1)


def _pad_heads(wt, pad):
    d = wt.shape[1]
    wt = wt.reshape(FOX_HEADS, FOX_HEAD_DIM, d)
    wt = jnp.pad(wt, ((0, 0), (0, pad - FOX_HEAD_DIM), (0, 0)))
    return wt.reshape(FOX_HEADS * pad, d)


def _prompt_consts(tm):
    tri = np.tril(np.ones((tm, tm), np.float32))
    eq = np.zeros((LANES, FOX_HEADS * HEAD_PAD), np.float32)
    ek = np.zeros((LANES, FOX_HEADS * HEAD_PAD), np.float32)
    oneq = np.zeros((1, FOX_HEADS * HEAD_PAD), np.float32)
    onek = np.zeros((1, FOX_HEADS * HEAD_PAD), np.float32)
    onev = np.zeros((1, FOX_HEADS * V_PAD), np.float32)
    for hh in range(FOX_HEADS):
        base = hh * HEAD_PAD + BIAS_COL
        for part in range(3):
            eq[part * FOX_HEADS + hh, base + part] = 1.0
            ek[part * FOX_HEADS + hh, base + 3 + part] = -1.0
            onek[0, base + part] = 1.0
            oneq[0, base + 3 + part] = 1.0
        onev[0, hh * V_PAD + FOX_HEAD_DIM] = 1.0
    return (jnp.asarray(tri, BF16), jnp.asarray(eq, BF16), jnp.asarray(ek, BF16),
            jnp.asarray(oneq), jnp.asarray(onek), jnp.asarray(onev))


def _tile(n, pref):
    t = min(n, pref)
    while n % t:
        t //= 2
    return t


def kernel(x_prompt, x_sample, cache_fox_k, cache_fox_v, cache_fox_logf, state_ret, state_ffn_conv,
           norm_mix_g, w_in, b_fox_f, gn_ret_g, w_pa, w_pb, w_o, norm_ffn_g, w_up, conv_w, conv_b,
           w_down, norm_final_g):
    depth = w_in.shape[0]
    Bp, S, _ = x_prompt.shape
    Bs, Ts, _ = x_sample.shape
    P = cache_fox_k.shape[2]
    assert depth == 1 and Bp == 1, "kernel handles the single-layer, single-prompt configuration"
    l = 0

    wt = jnp.swapaxes(w_in[l], 0, 1).astype(BF16)
    o0 = 3 * FOX_W
    o1 = o0 + FOX_HEADS
    o2 = o1 + 2 * RET_KW + RET_VW
    wq_aug = _pad_heads(wt[:FOX_W], HEAD_PAD)
    wk_aug = _pad_heads(wt[FOX_W:2 * FOX_W], HEAD_PAD)
    wf = jnp.pad(wt[o0:o1], ((0, LANES - FOX_HEADS), (0, 0)))
    wvf = jnp.concatenate([_pad_heads(wt[2 * FOX_W:o0], V_PAD), wf], axis=0)
    wqkv = wt[:o0]
    bf = jnp.pad(b_fox_f[l].astype(F32), (0, LANES - FOX_HEADS))[None, :]
    wb = wt[o1:o2]
    wg = wt[o2:]
    gmix = norm_mix_g[l].astype(F32)[None, :]
    mix_weights = (
        gmix, wg, gn_ret_g[l].astype(F32)[None, :], w_pa[l].astype(BF16), w_pb[l].astype(BF16),
        w_o[l].astype(BF16), norm_ffn_g[l].astype(F32)[None, :],
        w_up[l].astype(BF16),
        jnp.pad(conv_w[l].astype(F32), ((0, 8 - CONV_WIDTH), (0, 0))),
        conv_b[l].astype(F32)[None, :],
        w_down[l].astype(BF16),
        norm_final_g.astype(F32)[None, :],
    )
    hist_pad = ((0, 0), (8 - (CONV_WIDTH - 1), 0), (0, 0))

    tm_a = _tile(S, 512)
    cos_p, sin_p = _rotary_tables(0, S)
    (qT, ka, vT, kT_p, vT_p, logfT_p, qb, kb, vb, stats, sqq) = _inproj_prompt(
        x_prompt[0], gmix, wq_aug, wk_aug, wvf, bf, wb, cos_p, sin_p,
        _prompt_consts(tm_a), tm_a)
    oa_p = _fox_prompt(qT, ka, vT, stats, sqq, _tile(S, 1024))
    zero_state = jnp.zeros((1, RET_HEADS, RET_KEY_DIM, RET_VAL_DIM), F32)
    nb_p, ret_p = _retention(qb, kb, vb, zero_state, 1, S, _tile(S, 512))
    tm_d = _tile(S, 256)
    zero_prev = jnp.zeros((1, 8, 2 * FFN_DIM), F32)
    y_p, conv_p = _mixer_ffn(x_prompt[0], oa_p, nb_p, zero_prev, mix_weights, tm_d, 1, tm_d)

    Ms = Bs * Ts
    cos_s, sin_s = _rotary_tables(P, Ts)
    cos_s = jnp.tile(cos_s, (Bs, 1))
    sin_s = jnp.tile(sin_s, (Bs, 1))
    (q_s, k_s, v_s, logf_s, qb_s, kb_s, vb_s) = _inproj_sample(
        x_sample.reshape(Ms, D_MODEL), gmix, wqkv, wf, bf, wb, cos_s, sin_s)
    KP = ((P + Ts + LANES - 1) // LANES) * LANES
    lf_all = jnp.concatenate([cache_fox_logf[l].astype(F32), logf_s.reshape(Bs, Ts, FOX_HEADS)], axis=1)
    lfT = jnp.pad(jnp.swapaxes(lf_all, 1, 2), ((0, 0), (0, 0), (0, KP - P - Ts)))
    oa_s = _fox_sample(q_s, k_s, v_s, jnp.transpose(cache_fox_k[l], (0, 2, 3, 1)),
                       jnp.transpose(cache_fox_v[l], (0, 2, 3, 1)), lfT, Bs, Ts)
    oa_s = jnp.moveaxis(oa_s.reshape(Ms, FOX_HEADS // 2, LANES), 1, 0)
    nb_s, ret_s = _retention(qb_s, kb_s, vb_s, state_ret[l].astype(F32), Bs, Ts, Ts,
                             G=math.gcd(Bs, 4))
    prev_s = jnp.pad(state_ffn_conv[l].astype(F32), hist_pad)
    y_s, conv_s = _mixer_ffn(x_sample.reshape(Ms, D_MODEL), oa_s, nb_s, prev_s, mix_weights,
                             Ms, Bs, Ts)

    hshape = (FOX_HEADS, FOX_HEAD_DIM)
    return (
        y_p[None],
        y_s.reshape(Bs, Ts, D_MODEL),
        jnp.transpose(kT_p, (2, 0, 1))[None, None],
        jnp.transpose(vT_p, (2, 0, 1))[None, None],
        jnp.transpose(logfT_p, (1, 0))[None, None],
        ret_p[None],
        conv_p[None],
        k_s.reshape((1, Bs, Ts) + hshape),
        v_s.reshape((1, Bs, Ts) + hshape),
        logf_s.reshape(1, Bs, Ts, FOX_HEADS),
        ret_s[None],
        conv_s[None],
    )
```

```python
import functools
import math

import numpy as np
import jax
import jax.numpy as jnp
from jax import lax
from jax.experimental import pallas as pl
from jax.experimental.pallas import tpu as pltpu

F32 = jnp.float32
BF16 = jnp.bfloat16

D_MODEL = 1024
FOX_HEADS = 8
FOX_HEAD_DIM = 64
RET_HEADS = 4
RET_KEY_DIM = 128
RET_VAL_DIM = 256
FFN_DIM = 2816
CONV_WIDTH = 3
EPS = 1e-6
ROPE_BASE = 10000.0

FOX_W = FOX_HEADS * FOX_HEAD_DIM
RET_KW = RET_HEADS * RET_KEY_DIM
RET_VW = RET_HEADS * RET_VAL_DIM

LOG2E = 1.4426950408889634
LANES = 128
HEAD_PAD = LANES
V_PAD = 80
BIAS_COL = FOX_HEAD_DIM
NEG = -1e30
STALE_SAFE_LOG2 = 64.0
PRUNE_LOG2 = -160.0
NORM_SLACK = 1.02
ROT_FINE = 128
FFN_CHUNK = 256
N_FFN_CHUNKS = FFN_DIM // FFN_CHUNK
DOWN_GROUP = 4
UP_AHEAD_SLOTS = 3
VMEM_LIMIT = 56 * 1024 * 1024


def _rmsnorm(x, g):
    ms = jnp.mean(x * x, axis=-1, keepdims=True)
    return x * lax.rsqrt(ms + EPS) * g


def _split3(x):
    hi = x.astype(BF16)
    r1 = x - hi.astype(F32)
    mid = r1.astype(BF16)
    lo = (r1 - mid.astype(F32)).astype(BF16)
    return hi, mid, lo


def _log_sigmoid(x):
    return jnp.minimum(x, 0.0) - jnp.log1p(jnp.exp(-jnp.abs(x)))


def _dot(a, b):
    return jnp.dot(a, b, preferred_element_type=F32)


def _dot_t(a, bt):
    return lax.dot_general(a, bt, (((1,), (1,)), ((), ())), preferred_element_type=F32)


def _gelu_tanh(x):
    c0 = math.sqrt(2.0 / math.pi)
    hx = 0.5 * x
    return hx + hx * jnp.tanh(x * (c0 + (c0 * 0.044715) * (x * x)))


def _rotary(x, cos2, sin2):
    return x * cos2 + pltpu.roll(x, RET_KEY_DIM // 2, 1) * sin2


def _const_spec(shape):
    n = len(shape)
    return pl.BlockSpec(shape, lambda *_: (0,) * n)


def _inproj_prompt_kernel(x_ref, g_ref, wq_ref, wk_ref, wvf_ref, bf_ref, wb_ref,
                          cos_ref, sin_ref, tri_ref, eq_ref, ek_ref, oneq_ref, onek_ref, onev_ref,
                          qT_ref, ka_ref, vT_ref, kT32_ref, vT32_ref, logf_ref, qb_ref, kb_ref, vb_ref,
                          stats_ref, sqq_ref, carry_ref):
    tm = x_ref.shape[0]
    VW = FOX_HEADS * V_PAD

    @pl.when(pl.program_id(0) == 0)
    def _():
        carry_ref[...] = jnp.zeros_like(carry_ref)

    h = _rmsnorm(x_ref[...], g_ref[...]).astype(BF16)
    zvf = _dot_t(h, wvf_ref[...])

    logf = _log_sigmoid(zvf[:, VW:] + bf_ref[...])
    logf_ref[...] = logf.T[:FOX_HEADS, :]
    lane = lax.broadcasted_iota(jnp.int32, logf.shape, 1)
    logf = jnp.where(lane < FOX_HEADS, logf, 0.0)
    r = _dot(tri_ref[...], jnp.concatenate(_split3(logf), axis=1))
    c = r[:, :LANES] + r[:, LANES:2 * LANES] + r[:, 2 * LANES:] + carry_ref[...]
    carry_ref[...] = c[tm - 1:tm, :]
    c2 = c * LOG2E
    hi, mid, lo = (t.astype(F32) for t in _split3(c2))
    c3 = (hi + pltpu.roll(mid, FOX_HEADS, 1) + pltpu.roll(lo, 2 * FOX_HEADS, 1)).astype(BF16)

    q_aug = (_dot_t(h, wq_ref[...]) * (FOX_HEAD_DIM ** -0.5 * LOG2E)
             + _dot(c3, eq_ref[...]) + oneq_ref[...])
    qT32 = q_aug.T
    k_aug = _dot_t(h, wk_ref[...]) + _dot(c3, ek_ref[...]) + onek_ref[...]
    kT32 = k_aug.T
    vT32 = (zvf[:, :VW] + onev_ref[...]).T
    qT = qT32.astype(BF16)
    k_aug = k_aug.astype(BF16)
    vT = vT32.astype(BF16)

    lane1 = lax.broadcasted_iota(jnp.int32, (1, LANES), 1)
    nq2 = jnp.zeros((1, LANES), F32)
    nk2 = jnp.zeros((1, LANES), F32)
    for hh in range(FOX_HEADS):
        qT_ref[hh] = qT[hh * HEAD_PAD:(hh + 1) * HEAD_PAD, :]
        ka_ref[hh] = k_aug[:, hh * HEAD_PAD:(hh + 1) * HEAD_PAD]
        vT_ref[hh] = vT[hh * V_PAD:(hh + 1) * V_PAD, :]
        qh = qT32[hh * HEAD_PAD:hh * HEAD_PAD + FOX_HEAD_DIM, :]
        kh = kT32[hh * HEAD_PAD:hh * HEAD_PAD + FOX_HEAD_DIM, :]
        kT32_ref[hh] = kh
        vT32_ref[hh] = vT32[hh * V_PAD:hh * V_PAD + FOX_HEAD_DIM, :]
        sqq_ref[hh:hh + 1, :] = jnp.sum(qh * kh, axis=0, keepdims=True)
        q2 = jnp.max(jnp.sum(qh * qh, axis=0, keepdims=True), axis=1, keepdims=True)
        k2 = jnp.max(jnp.sum(kh * kh, axis=0, keepdims=True), axis=1, keepdims=True)
        nq2 = jnp.where(lane1 == hh, q2, nq2)
        nk2 = jnp.where(lane1 == hh, k2, nk2)
    stats_ref[0] = jnp.concatenate(
        [nq2, nk2, c2[0:1, :], c2[tm - 1:tm, :], jnp.zeros((4, LANES), F32)], axis=0)

    zb = _dot_t(h, wb_ref[...])
    cos2 = cos_ref[...]
    sin2 = sin_ref[...]
    for hh in range(RET_HEADS):
        sl = slice(hh * RET_KEY_DIM, (hh + 1) * RET_KEY_DIM)
        qb_ref[:, sl] = _rotary(zb[:, sl], cos2, sin2).astype(BF16)
        xk = zb[:, RET_KW + hh * RET_KEY_DIM:RET_KW + (hh + 1) * RET_KEY_DIM]
        kb_ref[:, sl] = (_rotary(xk, cos2, sin2) * (RET_KEY_DIM ** -0.5)).astype(BF16)
    vb_ref[...] = zb[:, 2 * RET_KW:].astype(BF16)


def _inproj_prompt(x, g, wq, wk, wvf, bf, wb, cos2, sin2, consts, tm):
    S = x.shape[0]
    tri, eq, ek, oneq, onek, onev = consts
    row = lambda w: pl.BlockSpec((tm, w), lambda i: (i, 0))
    headT = pl.BlockSpec((FOX_HEADS, FOX_HEAD_DIM, tm), lambda i: (0, 0, i))
    in_specs = [row(D_MODEL), _const_spec(g.shape), _const_spec(wq.shape), _const_spec(wk.shape),
                _const_spec(wvf.shape),
                _const_spec(bf.shape), _const_spec(wb.shape), row(LANES), row(LANES),
                _const_spec(tri.shape), _const_spec(eq.shape), _const_spec(ek.shape),
                _const_spec(oneq.shape), _const_spec(onek.shape), _const_spec(onev.shape)]
    out_shape = (
        jax.ShapeDtypeStruct((FOX_HEADS, HEAD_PAD, S), BF16),
        jax.ShapeDtypeStruct((FOX_HEADS, S, HEAD_PAD), BF16),
        jax.ShapeDtypeStruct((FOX_HEADS, V_PAD, S), BF16),
        jax.ShapeDtypeStruct((FOX_HEADS, FOX_HEAD_DIM, S), F32),
        jax.ShapeDtypeStruct((FOX_HEADS, FOX_HEAD_DIM, S), F32),
        jax.ShapeDtypeStruct((FOX_HEADS, S), F32),
        jax.ShapeDtypeStruct((S, RET_KW), BF16),
        jax.ShapeDtypeStruct((S, RET_KW), BF16),
        jax.ShapeDtypeStruct((S, RET_VW), BF16),
        jax.ShapeDtypeStruct((S // tm, 8, LANES), F32),
        jax.ShapeDtypeStruct((FOX_HEADS, S), F32),
    )
    out_specs = (
        pl.BlockSpec((FOX_HEADS, HEAD_PAD, tm), lambda i: (0, 0, i)),
        pl.BlockSpec((FOX_HEADS, tm, HEAD_PAD), lambda i: (0, i, 0)),
        pl.BlockSpec((FOX_HEADS, V_PAD, tm), lambda i: (0, 0, i)),
        headT, headT, pl.BlockSpec((FOX_HEADS, tm), lambda i: (0, i)),
        row(RET_KW), row(RET_KW), row(RET_VW),
        pl.BlockSpec((1, 8, LANES), lambda i: (i, 0, 0)),
        pl.BlockSpec((FOX_HEADS, tm), lambda i: (0, i)),
    )
    return pl.pallas_call(
        _inproj_prompt_kernel,
        grid=(S // tm,),
        in_specs=in_specs,
        out_specs=out_specs,
        out_shape=out_shape,
        scratch_shapes=[pltpu.VMEM((1, LANES), F32)],
        compiler_params=pltpu.CompilerParams(dimension_semantics=("arbitrary",),
                                             vmem_limit_bytes=VMEM_LIMIT),
        name="inproj_prompt",
    )(x, g, wq, wk, wvf, bf, wb, cos2, sin2, tri, eq, ek, oneq, onek, onev)


def _inproj_sample_kernel(x_ref, g_ref, wqkv_ref, wf_ref, bf_ref, wb_ref, cos_ref, sin_ref,
                          q_ref, k32_ref, v32_ref, logf_ref, qb_ref, kb_ref, vb_ref):
    h = _rmsnorm(x_ref[...], g_ref[...]).astype(BF16)
    logf = _log_sigmoid(_dot_t(h, wf_ref[...]) + bf_ref[...])
    logf_ref[...] = logf[:, :FOX_HEADS]
    z = _dot_t(h, wqkv_ref[...])
    q_ref[...] = (z[:, :FOX_W] * (FOX_HEAD_DIM ** -0.5 * LOG2E)).astype(BF16)
    k32_ref[...] = z[:, FOX_W:2 * FOX_W]
    v32_ref[...] = z[:, 2 * FOX_W:]
    zb = _dot_t(h, wb_ref[...])
    cos2 = cos_ref[...]
    sin2 = sin_ref[...]
    for hh in range(RET_HEADS):
        sl = slice(hh * RET_KEY_DIM, (hh + 1) * RET_KEY_DIM)
        qb_ref[:, sl] = _rotary(zb[:, sl], cos2, sin2).astype(BF16)
        xk = zb[:, RET_KW + hh * RET_KEY_DIM:RET_KW + (hh + 1) * RET_KEY_DIM]
        kb_ref[:, sl] = (_rotary(xk, cos2, sin2) * (RET_KEY_DIM ** -0.5)).astype(BF16)
    vb_ref[...] = zb[:, 2 * RET_KW:].astype(BF16)


def _inproj_sample(x, g, wqkv, wf, bf, wb, cos2, sin2):
    M = x.shape[0]
    args = (x, g, wqkv, wf, bf, wb, cos2, sin2)
    out_shape = (
        jax.ShapeDtypeStruct((M, FOX_W), BF16),
        jax.ShapeDtypeStruct((M, FOX_W), F32),
        jax.ShapeDtypeStruct((M, FOX_W), F32),
        jax.ShapeDtypeStruct((M, FOX_HEADS), F32),
        jax.ShapeDtypeStruct((M, RET_KW), BF16),
        jax.ShapeDtypeStruct((M, RET_KW), BF16),
        jax.ShapeDtypeStruct((M, RET_VW), BF16),
    )
    return pl.pallas_call(
        _inproj_sample_kernel,
        grid=(1,),
        in_specs=[_const_spec(a.shape) for a in args],
        out_specs=tuple(_const_spec(o.shape) for o in out_shape),
        out_shape=out_shape,
        compiler_params=pltpu.CompilerParams(dimension_semantics=("arbitrary",),
                                             vmem_limit_bytes=VMEM_LIMIT),
        name="inproj_sample",
    )(*args)


def _fox_prompt_kernel(it_ref, jt_ref, jfetch_ref, mode_ref, qT_ref, ka_ref, vT_ref,
                       sqq_ref, o_ref, m_ref, acc_ref, *, n_strips):
    del jfetch_ref
    t = pl.program_id(0)
    i = it_ref[t]
    j = jt_ref[t]
    T = qT_ref.shape[2]
    SUB = T // n_strips
    EXACT = n_strips + 1

    def heads(fn):
        def body(hh, carry):
            fn(hh, mode_ref[t * FOX_HEADS + hh])
            return carry
        lax.fori_loop(0, FOX_HEADS, body, 0)

    def scores(hh):
        return _dot(ka_ref[hh], qT_ref[hh])

    def exact_head(hh):
        s = scores(hh)
        kk = lax.broadcasted_iota(jnp.int32, s.shape, 0) + j * T
        qq = lax.broadcasted_iota(jnp.int32, s.shape, 1) + i * T
        s = jnp.where(kk > qq, NEG, s)
        m_old = m_ref[hh]
        m_new = jnp.maximum(m_old, jnp.max(s, axis=0, keepdims=True))
        p = jnp.exp2(s - m_new).astype(BF16)
        alpha = jnp.exp2(m_old - m_new)
        acc_ref[hh] = alpha * acc_ref[hh] + _dot(vT_ref[hh], p)
        m_ref[hh] = m_new


    def diag_head(hh, mode):
        m_ref[hh] = sqq_ref[pl.ds(hh, 1), :]

        @pl.when(mode != EXACT)
        def _():
            half = T // 2
            m = m_ref[hh]
            kk = lax.broadcasted_iota(jnp.int32, (half, half), 0)
            qq = lax.broadcasted_iota(jnp.int32, (half, half), 1)
            tri = kk > qq
            s_lo = _dot(ka_ref[hh, :half, :], qT_ref[hh])
            s_lo = jnp.concatenate([jnp.where(tri, NEG, s_lo[:, :half]), s_lo[:, half:]], axis=1)
            acc_ref[hh] = _dot(vT_ref[hh, :, :half], jnp.exp2(s_lo - m).astype(BF16))
            s_hi = jnp.where(tri, NEG, _dot(ka_ref[hh, half:, :], qT_ref[hh, :, half:]))
            p_hi = jnp.exp2(s_hi - m[:, half:]).astype(BF16)
            acc_ref[hh, :, half:] += _dot(vT_ref[hh, :, half:], p_hi)

        @pl.when(mode == EXACT)
        def _():
            acc_ref[hh] = jnp.zeros((V_PAD, T), F32)
            exact_head(hh)

    def off_head(hh, mode):
        for nn in range(1, n_strips + 1):
            k0 = (n_strips - nn) * SUB

            @pl.when(mode == nn)
            def _():
                p = jnp.exp2(_dot(ka_ref[hh, k0:, :], qT_ref[hh]) - m_ref[hh]).astype(BF16)
                acc_ref[hh] += _dot(vT_ref[hh, :, k0:], p)

        @pl.when(mode == EXACT)
        def _():
            exact_head(hh)

    @pl.when(j == i)
    def _():
        heads(diag_head)

    @pl.when(j < i)
    def _():
        heads(off_head)

    @pl.when(j == 0)
    def _():
        for pr in range(FOX_HEADS // 2):
            halves = []
            for hh in (2 * pr, 2 * pr + 1):
                a = acc_ref[hh]
                halves.append(a[:FOX_HEAD_DIM] / a[FOX_HEAD_DIM:FOX_HEAD_DIM + 1])
            o_ref[pr] = jnp.concatenate(halves, axis=0).T.astype(BF16)


def _prune_tables(stats, it, jt, nb):
    per = stats.shape[0] // nb
    st = stats.reshape(nb, per, 8, LANES)[:, :, :, :FOX_HEADS]
    nq = jnp.sqrt(jnp.max(st[:, :, 0, :], axis=1)) * NORM_SLACK
    nk_strip = jnp.sqrt(st[:, :, 1, :]) * NORM_SLACK
    nk = jnp.max(nk_strip, axis=1)
    c_first = st[:, 0, 2, :]
    c_last = st[:, :, 3, :]
    bound = (nq[it][:, None, :] * (nk_strip[jt] + nk[it][:, None, :])
             - (c_last[jt] - c_first[it][:, None, :]))
    live = jnp.logical_or(jnp.asarray(jt == it)[:, None, None],
                          jnp.logical_not(bound < PRUNE_LOG2))
    strip_no = jnp.arange(per, dtype=jnp.int32)[None, :, None]
    n_keep = per - jnp.min(jnp.where(live, strip_no, per), axis=1)
    safe = nq[it] * (nk[jt] + nk[it]) < STALE_SAFE_LOG2
    mode = jnp.where(n_keep == 0, 0, jnp.where(safe, n_keep, per + 1))
    steps = jnp.arange(len(it), dtype=jnp.int32)
    last_live = lax.cummax(jnp.where(jnp.any(n_keep > 0, axis=1), steps, 0))
    return jnp.asarray(jt)[last_live], mode.astype(jnp.int32).reshape(-1), per


def _fox_prompt(qT, ka, vT, stats, sqq, T):
    S = ka.shape[1]
    nb = S // T
    it = np.array([i for i in range(nb) for _ in range(i + 1)], np.int32)
    jt = np.array([j for i in range(nb) for j in range(i, -1, -1)], np.int32)
    jfetch, mode, n_strips = _prune_tables(stats, it, jt, nb)
    grid_spec = pltpu.PrefetchScalarGridSpec(
        num_scalar_prefetch=4,
        grid=(len(it),),
        in_specs=[
            pl.BlockSpec((FOX_HEADS, HEAD_PAD, T), lambda t, it, jt, jf, md: (0, 0, it[t])),
            pl.BlockSpec((FOX_HEADS, T, HEAD_PAD), lambda t, it, jt, jf, md: (0, jf[t], 0)),
            pl.BlockSpec((FOX_HEADS, V_PAD, T), lambda t, it, jt, jf, md: (0, 0, jf[t])),
            pl.BlockSpec((FOX_HEADS, T), lambda t, it, jt, jf, md: (0, it[t])),
        ],
        out_specs=pl.BlockSpec((FOX_HEADS // 2, T, LANES),
                               lambda t, it, jt, jf, md: (0, it[t], 0)),
        scratch_shapes=[pltpu.VMEM((FOX_HEADS, 1, T), F32),
                        pltpu.VMEM((FOX_HEADS, V_PAD, T), F32)],
    )
    return pl.pallas_call(
        functools.partial(_fox_prompt_kernel, n_strips=n_strips),
        grid_spec=grid_spec,
        out_shape=jax.ShapeDtypeStruct((FOX_HEADS // 2, S, LANES), BF16),
        compiler_params=pltpu.CompilerParams(dimension_semantics=("arbitrary",),
                                             vmem_limit_bytes=VMEM_LIMIT),
        name="fox_prompt",
    )(jnp.asarray(it), jnp.asarray(jt), jfetch, mode, qT, ka, vT, sqq)


def _fox_sample_kernel(q_ref, kn_ref, vn_ref, ckT_ref, cvT_ref, lfT_ref, up_ref, ex_ref, o_ref):
    P = ckT_ref.shape[3]
    Tn = q_ref.shape[0]
    KP = lfT_ref.shape[2]
    HQ = FOX_HEADS * Tn
    nchunk = KP // LANES
    nt = (((1,), (1,)), ((), ()))

    def stack3(x):
        parts3 = [t.astype(F32) for t in _split3(x)] + [jnp.zeros_like(x)]
        return jnp.concatenate(parts3, axis=0).astype(BF16)

    x3 = stack3(lfT_ref[0])
    up = up_ref[...]
    parts = [_dot(x3[:, cidx * LANES:(cidx + 1) * LANES], up) for cidx in range(nchunk)]
    run = jnp.zeros((4 * FOX_HEADS, 1), F32)
    for cidx in range(nchunk):
        total = parts[cidx][:, LANES - 1:LANES]
        parts[cidx] = parts[cidx] + run
        run = run + total
    y = jnp.concatenate(parts, axis=1)
    cT = (y[:FOX_HEADS] + y[FOX_HEADS:2 * FOX_HEADS] + y[2 * FOX_HEADS:3 * FOX_HEADS]) * LOG2E
    ckx = _dot(ex_ref[...], stack3(cT))

    tail = ckx[:, P:P + LANES]
    rowq = lax.broadcasted_iota(jnp.int32, tail.shape, 0) % Tn
    lanek = lax.broadcasted_iota(jnp.int32, tail.shape, 1)
    cq = jnp.sum(jnp.where(lanek == rowq, tail, 0.0), axis=1, keepdims=True)

    q = q_ref[...]
    qt = jnp.concatenate([q] * FOX_HEADS, axis=0)
    rh = lax.broadcasted_iota(jnp.int32, qt.shape, 0) // Tn
    lh = lax.broadcasted_iota(jnp.int32, qt.shape, 1) // FOX_HEAD_DIM
    qbd = jnp.where(rh == lh, qt, jnp.zeros_like(qt))

    kT = ckT_ref[0].reshape(FOX_W, P).astype(BF16)
    vT = cvT_ref[0].reshape(FOX_W, P).astype(BF16)
    s_c = _dot(qbd, kT) + cq - ckx[:, :P]
    s_n = lax.dot_general(qbd, kn_ref[...].astype(BF16), nt, preferred_element_type=F32)
    s_n = s_n + cq - ckx[:, P:P + Tn]
    key = lax.broadcasted_iota(jnp.int32, s_n.shape, 1)
    qrow = lax.broadcasted_iota(jnp.int32, s_n.shape, 0) % Tn
    s_n = jnp.where(key > qrow, NEG, s_n)
    m = jnp.maximum(jnp.max(s_c, axis=1, keepdims=True), jnp.max(s_n, axis=1, keepdims=True))
    p_c = jnp.exp2(s_c - m)
    p_n = jnp.exp2(s_n - m)
    l = jnp.sum(p_c, axis=1, keepdims=True) + jnp.sum(p_n, axis=1, keepdims=True)
    z = lax.dot_general(p_c.astype(BF16), vT, nt, preferred_element_type=F32)
    z = (z + _dot(p_n.astype(BF16), vn_ref[...].astype(BF16))) / l
    zh = lax.broadcasted_iota(jnp.int32, (Tn, FOX_W), 1) // FOX_HEAD_DIM
    o = jnp.zeros((Tn, FOX_W), F32)
    for hh in range(FOX_HEADS):
        o = o + jnp.where(zh == hh, z[hh * Tn:(hh + 1) * Tn, :], 0.0)
    o_ref[...] = o.astype(BF16)


def _fox_sample(q, kn, vn, cache_kT, cache_vT, lfT, B, Tn):
    P = cache_kT.shape[3]
    KP = lfT.shape[2]
    HQ = FOX_HEADS * Tn
    up = jnp.asarray(np.triu(np.ones((LANES, LANES), np.float32)), BF16)
    ex = np.zeros((HQ, 4 * FOX_HEADS), np.float32)
    for part in range(3):
        for hh in range(FOX_HEADS):
            ex[hh * Tn:(hh + 1) * Tn, part * FOX_HEADS + hh] = 1.0
    ex = jnp.asarray(ex, BF16)
    rowb = lambda w: pl.BlockSpec((Tn, w), lambda b: (b, 0))
    return pl.pallas_call(
        _fox_sample_kernel,
        grid=(B,),
        in_specs=[rowb(FOX_W), rowb(FOX_W), rowb(FOX_W),
                  pl.BlockSpec((1, FOX_HEADS, FOX_HEAD_DIM, P), lambda b: (b, 0, 0, 0)),
                  pl.BlockSpec((1, FOX_HEADS, FOX_HEAD_DIM, P), lambda b: (b, 0, 0, 0)),
                  pl.BlockSpec((1, FOX_HEADS, KP), lambda b: (b, 0, 0)),
                  _const_spec(up.shape), _const_spec(ex.shape)],
        out_specs=rowb(FOX_W),
        out_shape=jax.ShapeDtypeStruct((B * Tn, FOX_W), BF16),
        compiler_params=pltpu.CompilerParams(dimension_semantics=("arbitrary",),
                                             vmem_limit_bytes=VMEM_LIMIT),
        name="fox_sample",
    )(q, kn, vn, cache_kT, cache_vT, lfT, up, ex)


def _retention_kernel(q_ref, k_ref, v_ref, s0_ref, dmat_ref, xi_ref, zeta_ref, gam_ref,
                      n_ref, sout_ref, st_ref):
    c = pl.program_id(1)
    G = st_ref.shape[0]
    C = q_ref.shape[0] // G

    @pl.when(c == 0)
    def _():
        st_ref[...] = s0_ref[...]

    for g in range(G):
        rows = slice(g * C, (g + 1) * C)
        for hh in range(RET_HEADS):
            q = q_ref[rows, hh * RET_KEY_DIM:(hh + 1) * RET_KEY_DIM]
            k = k_ref[rows, hh * RET_KEY_DIM:(hh + 1) * RET_KEY_DIM]
            v = v_ref[rows, hh * RET_VAL_DIM:(hh + 1) * RET_VAL_DIM]
            st = st_ref[g, hh]
            sc = lax.dot_general(q, k, (((1,), (1,)), ((), ())), preferred_element_type=F32)
            sc = sc * dmat_ref[hh]
            o = _dot(sc.astype(BF16), v) + _dot(q, st.astype(BF16)) * xi_ref[hh]
            kz = (k.astype(F32) * zeta_ref[hh]).astype(BF16)
            upd = lax.dot_general(kz, v, (((0,), (0,)), ((), ())), preferred_element_type=F32)
            st_ref[g, hh] = gam_ref[hh] * st + upd
            mu = jnp.mean(o, axis=-1, keepdims=True)
            d = o - mu
            var = jnp.mean(d * d, axis=-1, keepdims=True)
            n_ref[rows, hh * RET_VAL_DIM:(hh + 1) * RET_VAL_DIM] = (
                d * lax.rsqrt(var + EPS)).astype(BF16)

    @pl.when(c == pl.num_programs(1) - 1)
    def _():
        sout_ref[...] = st_ref[...]


def _retention(q, k, v, state0, B, L, C, G=1):
    nc = L // C
    assert B % G == 0 and (G == 1 or nc == 1)
    f32 = np.float32
    lg = np.log(f32(1.0) - np.exp2(f32(-5.0) - np.arange(RET_HEADS, dtype=f32))).astype(f32)
    idx = np.arange(C, dtype=f32)
    diff = idx[:, None] - idx[None, :]
    dmat = np.where(diff[None] >= 0, np.exp(np.maximum(diff, 0)[None] * lg[:, None, None]), 0)
    xi = np.exp((idx[None, :] + f32(1.0)) * lg[:, None])
    zeta = np.exp((f32(C) - f32(1.0) - idx[None, :]) * lg[:, None])
    xi = np.broadcast_to(xi[:, :, None], (RET_HEADS, C, RET_VAL_DIM))
    zeta = np.broadcast_to(zeta[:, :, None], (RET_HEADS, C, RET_KEY_DIM))
    gam = np.broadcast_to(np.exp(f32(C) * lg)[:, None, None], (RET_HEADS, 1, RET_VAL_DIM))
    dmat, xi, zeta, gam = (jnp.asarray(a, F32) for a in (dmat, xi, zeta, gam))
    rowc = lambda w: pl.BlockSpec((G * C, w), lambda b, c: (b * nc + c, 0))
    st_spec = pl.BlockSpec((G, RET_HEADS, RET_KEY_DIM, RET_VAL_DIM), lambda b, c: (b, 0, 0, 0))
    return pl.pallas_call(
        _retention_kernel,
        grid=(B // G, nc),
        in_specs=[rowc(RET_KW), rowc(RET_KW), rowc(RET_VW), st_spec,
                  _const_spec(dmat.shape), _const_spec(xi.shape), _const_spec(zeta.shape),
                  _const_spec(gam.shape)],
        out_specs=(rowc(RET_VW), st_spec),
        out_shape=(jax.ShapeDtypeStruct((B * L, RET_VW), BF16),
                   jax.ShapeDtypeStruct((B, RET_HEADS, RET_KEY_DIM, RET_VAL_DIM), F32)),
        scratch_shapes=[pltpu.VMEM((G, RET_HEADS, RET_KEY_DIM, RET_VAL_DIM), F32)],
        compiler_params=pltpu.CompilerParams(dimension_semantics=("arbitrary", "arbitrary"),
                                             vmem_limit_bytes=VMEM_LIMIT),
        name="retention",
    )(q, k, v, state0, dmat, xi, zeta, gam)


def _mixer_ffn_kernel(x_ref, oa_ref, nb_ref, prev_ref, gmix_ref, wg_ref, gng_ref, wpa_ref, wpb_ref,
                      wo_ref, gffn_ref, wup_ref, cw_ref, cb_ref, wdn_ref, gfin_ref,
                      y_ref, conv_ref, carry_ref, ua_ref, ub_ref, acc_ref, h2_ref,
                      *, nseg, seglen):
    i = pl.program_id(0)
    NC = N_FFN_CHUNKS
    PADR = 8
    H0 = PADR - (CONV_WIDTH - 1)

    @pl.when(i == 0)
    def _():
        carry_ref[...] = prev_ref[...]

    x = x_ref[...]
    h = _rmsnorm(x, gmix_ref[...]).astype(BF16)
    zg = _dot_t(h, wg_ref[...])
    gb = zg[:, :RET_VW]
    gma = zg[:, RET_VW:RET_VW + D_MODEL]
    gmb = zg[:, RET_VW + D_MODEL:]
    oa = jnp.concatenate([oa_ref[p] for p in range(FOX_HEADS // 2)], axis=1)
    ya = _dot(oa, wpa_ref[...])
    nn = nb_ref[...].astype(F32) * gng_ref[...] * (gb * jax.nn.sigmoid(gb))
    yb = _dot(nn.astype(BF16), wpb_ref[...])
    y = jax.nn.sigmoid(gma) * ya + jax.nn.sigmoid(gmb) * yb
    x1 = x + _dot(y.astype(BF16), wo_ref[...])
    h2_ref[...] = _rmsnorm(x1, gffn_ref[...]).astype(BF16)
    acc_ref[...] = x1

    h2 = h2_ref[...]

    def cols(cidx):
        return slice(cidx * FFN_CHUNK, (cidx + 1) * FFN_CHUNK)

    def up_half(u_ref, cidx, slot):
        u = _dot(h2, wup_ref[:, cols(cidx)])
        for s in range(nseg):
            u_ref[slot, s, PADR:PADR + seglen, :] = u[s * seglen:(s + 1) * seglen, :]
            u_ref[slot, s, H0:PADR, :] = carry_ref[s, H0:PADR, cols(cidx)]
            carry_ref[s, H0:PADR, cols(cidx)] = (
                u[(s + 1) * seglen - (CONV_WIDTH - 1):(s + 1) * seglen, :])

    def conv_half(u_ref, cidx, slot):
        w = cw_ref[:, cols(cidx)]
        b = cb_ref[:, cols(cidx)]
        outs = []
        for s in range(nseg):
            acc = w[0:1] * u_ref[slot, s, H0:H0 + seglen, :]
            for jj in range(1, CONV_WIDTH):
                acc = acc + w[jj:jj + 1] * u_ref[slot, s, H0 + jj:H0 + jj + seglen, :]
            outs.append(b + acc)
        return outs[0] if nseg == 1 else jnp.concatenate(outs, axis=0)

    def stage_up(c):
        up_half(ua_ref, c, c % UP_AHEAD_SLOTS)
        up_half(ub_ref, NC + c, c % UP_AHEAD_SLOTS)

    for c in range(UP_AHEAD_SLOTS - 1):
        stage_up(c)
    group = []
    for c in range(NC):
        if c + UP_AHEAD_SLOTS - 1 < NC:
            stage_up(c + UP_AHEAD_SLOTS - 1)
        a = conv_half(ua_ref, c, c % UP_AHEAD_SLOTS)
        b = conv_half(ub_ref, NC + c, c % UP_AHEAD_SLOTS)
        group.append((_gelu_tanh(a) * b).astype(BF16))
        if len(group) == DOWN_GROUP or c + 1 == NC:
            r0 = (c + 1 - len(group)) * FFN_CHUNK
            wd = wdn_ref[r0:(c + 1) * FFN_CHUNK, :]
            acc_ref[...] += _dot(jnp.concatenate(group, axis=1), wd)
            group = []
    y_ref[...] = _rmsnorm(acc_ref[...], gfin_ref[...])

    @pl.when(i == pl.num_programs(0) - 1)
    def _():
        conv_ref[...] = carry_ref[:, H0:PADR, :]


def _mixer_ffn(x, oa, nb, prev, weights, tm, nseg, seglen):
    M = x.shape[0]
    gmix, wg, gng, wpa, wpb, wo, gffn, wup, cw, cb, wdn, gfin = weights
    FC = FFN_CHUNK
    NC = N_FFN_CHUNKS
    row = lambda w: pl.BlockSpec((tm, w), lambda i: (i, 0))
    wspec = lambda a: pl.BlockSpec(a.shape, lambda i, n=a.ndim: (0,) * n,
                                   pipeline_mode=pl.Buffered(1))
    in_specs = [row(D_MODEL),
                pl.BlockSpec((FOX_HEADS // 2, tm, LANES), lambda i: (0, i, 0)),
                row(RET_VW), wspec(prev)] + [wspec(w) for w in weights]
    out_shape = (jax.ShapeDtypeStruct((M, D_MODEL), F32),
                 jax.ShapeDtypeStruct((nseg, CONV_WIDTH - 1, 2 * FFN_DIM), F32))
    out_specs = (row(D_MODEL), _const_spec(out_shape[1].shape))
    return pl.pallas_call(
        functools.partial(_mixer_ffn_kernel, nseg=nseg, seglen=seglen),
        grid=(M // tm,),
        in_specs=in_specs,
        out_specs=out_specs,
        out_shape=out_shape,
        scratch_shapes=[pltpu.VMEM((nseg, 8, 2 * FFN_DIM), F32),
                        pltpu.VMEM((UP_AHEAD_SLOTS, nseg, 8 + seglen, FC), F32),
                        pltpu.VMEM((UP_AHEAD_SLOTS, nseg, 8 + seglen, FC), F32),
                        pltpu.VMEM((tm, D_MODEL), F32),
                        pltpu.VMEM((tm, D_MODEL), BF16)],
        compiler_params=pltpu.CompilerParams(dimension_semantics=("arbitrary",),
                                             vmem_limit_bytes=VMEM_LIMIT),
        name="mixer_ffn",
    )(x, oa, nb, prev, *weights)


def _rotary_tables(start, n):
    half = RET_KEY_DIM // 2
    inv = 1.0 / (ROPE_BASE ** jnp.linspace(0.0, 1.0, half, dtype=F32))
    fine = min(n, ROT_FINE)
    assert n % fine == 0
    a_hi = (start + fine * jnp.arange(n // fine)).astype(F32)[:, None] * inv[None, :]
    a_lo = jnp.arange(fine).astype(F32)[:, None] * inv[None, :]
    ch, sh = jnp.cos(a_hi)[:, None, :], jnp.sin(a_hi)[:, None, :]
    cl, sl = jnp.cos(a_lo)[None, :, :], jnp.sin(a_lo)[None, :, :]
    cos = (ch * cl - sh * sl).reshape(n, half)
    sin = (sh * cl + ch * sl).reshape(n, half)
    return jnp.concatenate([cos, cos], axis=1), jnp.concatenate([-sin, sin], axis=1)


def _pad_heads(wt, pad):
    d = wt.shape[1]
    wt = wt.reshape(FOX_HEADS, FOX_HEAD_DIM, d)
    wt = jnp.pad(wt, ((0, 0), (0, pad - FOX_HEAD_DIM), (0, 0)))
    return wt.reshape(FOX_HEADS * pad, d)


def _prompt_consts(tm):
    tri = np.tril(np.ones((tm, tm), np.float32))
    eq = np.zeros((LANES, FOX_HEADS * HEAD_PAD), np.float32)
    ek = np.zeros((LANES, FOX_HEADS * HEAD_PAD), np.float32)
    oneq = np.zeros((1, FOX_HEADS * HEAD_PAD), np.float32)
    onek = np.zeros((1, FOX_HEADS * HEAD_PAD), np.float32)
    onev = np.zeros((1, FOX_HEADS * V_PAD), np.float32)
    for hh in range(FOX_HEADS):
        base = hh * HEAD_PAD + BIAS_COL
        for part in range(3):
            eq[part * FOX_HEADS + hh, base + part] = 1.0
            ek[part * FOX_HEADS + hh, base + 3 + part] = -1.0
            onek[0, base + part] = 1.0
            oneq[0, base + 3 + part] = 1.0
        onev[0, hh * V_PAD + FOX_HEAD_DIM] = 1.0
    return (jnp.asarray(tri, BF16), jnp.asarray(eq, BF16), jnp.asarray(ek, BF16),
            jnp.asarray(oneq), jnp.asarray(onek), jnp.asarray(onev))


def _tile(n, pref):
    t = min(n, pref)
    while n % t:
        t //= 2
    return t


def kernel(x_prompt, x_sample, cache_fox_k, cache_fox_v, cache_fox_logf, state_ret, state_ffn_conv,
           norm_mix_g, w_in, b_fox_f, gn_ret_g, w_pa, w_pb, w_o, norm_ffn_g, w_up, conv_w, conv_b,
           w_down, norm_final_g):
    depth = w_in.shape[0]
    Bp, S, _ = x_prompt.shape
    Bs, Ts, _ = x_sample.shape
    P = cache_fox_k.shape[2]
    assert depth == 1 and Bp == 1, "kernel handles the single-layer, single-prompt configuration"
    l = 0

    wt = jnp.swapaxes(w_in[l], 0, 1).astype(BF16)
    o0 = 3 * FOX_W
    o1 = o0 + FOX_HEADS
    o2 = o1 + 2 * RET_KW + RET_VW
    wq_aug = _pad_heads(wt[:FOX_W], HEAD_PAD)
    wk_aug = _pad_heads(wt[FOX_W:2 * FOX_W], HEAD_PAD)
    wf = jnp.pad(wt[o0:o1], ((0, LANES - FOX_HEADS), (0, 0)))
    wvf = jnp.concatenate([_pad_heads(wt[2 * FOX_W:o0], V_PAD), wf], axis=0)
    wqkv = wt[:o0]
    bf = jnp.pad(b_fox_f[l].astype(F32), (0, LANES - FOX_HEADS))[None, :]
    wb = wt[o1:o2]
    wg = wt[o2:]
    gmix = norm_mix_g[l].astype(F32)[None, :]
    mix_weights = (
        gmix, wg, gn_ret_g[l].astype(F32)[None, :], w_pa[l].astype(BF16), w_pb[l].astype(BF16),
        w_o[l].astype(BF16), norm_ffn_g[l].astype(F32)[None, :],
        w_up[l].astype(BF16),
        jnp.pad(conv_w[l].astype(F32), ((0, 8 - CONV_WIDTH), (0, 0))),
        conv_b[l].astype(F32)[None, :],
        w_down[l].astype(BF16),
        norm_final_g.astype(F32)[None, :],
    )
    hist_pad = ((0, 0), (8 - (CONV_WIDTH - 1), 0), (0, 0))

    tm_a = _tile(S, 512)
    cos_p, sin_p = _rotary_tables(0, S)
    (qT, ka, vT, kT_p, vT_p, logfT_p, qb, kb, vb, stats, sqq) = _inproj_prompt(
        x_prompt[0], gmix, wq_aug, wk_aug, wvf, bf, wb, cos_p, sin_p,
        _prompt_consts(tm_a), tm_a)
    oa_p = _fox_prompt(qT, ka, vT, stats, sqq, _tile(S, 1024))
    zero_state = jnp.zeros((1, RET_HEADS, RET_KEY_DIM, RET_VAL_DIM), F32)
    nb_p, ret_p = _retention(qb, kb, vb, zero_state, 1, S, _tile(S, 512))
    tm_d = _tile(S, 256)
    zero_prev = jnp.zeros((1, 8, 2 * FFN_DIM), F32)
    y_p, conv_p = _mixer_ffn(x_prompt[0], oa_p, nb_p, zero_prev, mix_weights, tm_d, 1, tm_d)

    Ms = Bs * Ts
    cos_s, sin_s = _rotary_tables(P, Ts)
    cos_s = jnp.tile(cos_s, (Bs, 1))
    sin_s = jnp.tile(sin_s, (Bs, 1))
    (q_s, k_s, v_s, logf_s, qb_s, kb_s, vb_s) = _inproj_sample(
        x_sample.reshape(Ms, D_MODEL), gmix, wqkv, wf, bf, wb, cos_s, sin_s)
    KP = ((P + Ts + LANES - 1) // LANES) * LANES
    lf_all = jnp.concatenate([cache_fox_logf[l].astype(F32), logf_s.reshape(Bs, Ts, FOX_HEADS)], axis=1)
    lfT = jnp.pad(jnp.swapaxes(lf_all, 1, 2), ((0, 0), (0, 0), (0, KP - P - Ts)))
    oa_s = _fox_sample(q_s, k_s, v_s, jnp.transpose(cache_fox_k[l], (0, 2, 3, 1)),
                       jnp.transpose(cache_fox_v[l], (0, 2, 3, 1)), lfT, Bs, Ts)
    oa_s = jnp.moveaxis(oa_s.reshape(Ms, FOX_HEADS // 2, LANES), 1, 0)
    nb_s, ret_s = _retention(qb_s, kb_s, vb_s, state_ret[l].astype(F32), Bs, Ts, Ts,
                             G=math.gcd(Bs, 4))
    prev_s = jnp.pad(state_ffn_conv[l].astype(F32), hist_pad)
    y_s, conv_s = _mixer_ffn(x_sample.reshape(Ms, D_MODEL), oa_s, nb_s, prev_s, mix_weights,
                             Ms, Bs, Ts)

    hshape = (FOX_HEADS, FOX_HEAD_DIM)
    return (
        y_p[None],
        y_s.reshape(Bs, Ts, D_MODEL),
        jnp.transpose(kT_p, (2, 0, 1))[None, None],
        jnp.transpose(vT_p, (2, 0, 1))[None, None],
        jnp.transpose(logfT_p, (1, 0))[None, None],
        ret_p[None],
        conv_p[None],
        k_s.reshape((1, Bs, Ts) + hshape),
        v_s.reshape((1, Bs, Ts) + hshape),
        logf_s.reshape(1, Bs, Ts, FOX_HEADS),
        ret_s[None],
        conv_s[None],
    )
```

```python
import functools
import math

import numpy as np
import jax
import jax.numpy as jnp
from jax import lax
from jax.experimental import pallas as pl
from jax.experimental.pallas import tpu as pltpu

F32 = jnp.float32
BF16 = jnp.bfloat16

D_MODEL = 1024
FOX_HEADS = 8
FOX_HEAD_DIM = 64
RET_HEADS = 4
RET_KEY_DIM = 128
RET_VAL_DIM = 256
FFN_DIM = 2816
CONV_WIDTH = 3
EPS = 1e-6
ROPE_BASE = 10000.0

FOX_W = FOX_HEADS * FOX_HEAD_DIM
RET_KW = RET_HEADS * RET_KEY_DIM
RET_VW = RET_HEADS * RET_VAL_DIM

LOG2E = 1.4426950408889634
LANES = 128
HEAD_PAD = LANES
V_PAD = 80
BIAS_COL = FOX_HEAD_DIM
NEG = -1e30
STALE_SAFE_LOG2 = 64.0
PRUNE_LOG2 = -160.0
NORM_SLACK = 1.02
ROT_FINE = 128
FFN_CHUNK = 256
N_FFN_CHUNKS = FFN_DIM // FFN_CHUNK
DOWN_GROUP = 4
UP_AHEAD_SLOTS = 3
VMEM_LIMIT = 56 * 1024 * 1024


def _rmsnorm(x, g):
    ms = jnp.mean(x * x, axis=-1, keepdims=True)
    return x * lax.rsqrt(ms + EPS) * g


def _split3(x):
    hi = x.astype(BF16)
    r1 = x - hi.astype(F32)
    mid = r1.astype(BF16)
    lo = (r1 - mid.astype(F32)).astype(BF16)
    return hi, mid, lo


def _log_sigmoid(x):
    return jnp.minimum(x, 0.0) - jnp.log1p(jnp.exp(-jnp.abs(x)))


def _dot(a, b):
    return jnp.dot(a, b, preferred_element_type=F32)


def _dot_t(a, bt):
    return lax.dot_general(a, bt, (((1,), (1,)), ((), ())), preferred_element_type=F32)


def _gelu_tanh(x):
    c0 = math.sqrt(2.0 / math.pi)
    hx = 0.5 * x
    return hx + hx * jnp.tanh(x * (c0 + (c0 * 0.044715) * (x * x)))


def _rotary(x, cos2, sin2):
    return x * cos2 + pltpu.roll(x, RET_KEY_DIM // 2, 1) * sin2


def _const_spec(shape):
    n = len(shape)
    return pl.BlockSpec(shape, lambda *_: (0,) * n)


def _inproj_prompt_kernel(x_ref, g_ref, wq_ref, wk_ref, wvf_ref, bf_ref, wb_ref,
                          cos_ref, sin_ref, tri_ref, eq_ref, ek_ref, oneq_ref, onek_ref, onev_ref,
                          qT_ref, ka_ref, vT_ref, kT32_ref, vT32_ref, logf_ref, qb_ref, kb_ref, vb_ref,
                          stats_ref, sqq_ref, carry_ref):
    tm = x_ref.shape[0]
    VW = FOX_HEADS * V_PAD

    @pl.when(pl.program_id(0) == 0)
    def _():
        carry_ref[...] = jnp.zeros_like(carry_ref)

    h = _rmsnorm(x_ref[...], g_ref[...]).astype(BF16)
    zvf = _dot_t(h, wvf_ref[...])

    logf = _log_sigmoid(zvf[:, VW:] + bf_ref[...])
    logf_ref[...] = logf.T[:FOX_HEADS, :]
    lane = lax.broadcasted_iota(jnp.int32, logf.shape, 1)
    logf = jnp.where(lane < FOX_HEADS, logf, 0.0)
    r = _dot(tri_ref[...], jnp.concatenate(_split3(logf), axis=1))
    c = r[:, :LANES] + r[:, LANES:2 * LANES] + r[:, 2 * LANES:] + carry_ref[...]
    carry_ref[...] = c[tm - 1:tm, :]
    c2 = c * LOG2E
    hi, mid, lo = (t.astype(F32) for t in _split3(c2))
    c3 = (hi + pltpu.roll(mid, FOX_HEADS, 1) + pltpu.roll(lo, 2 * FOX_HEADS, 1)).astype(BF16)

    q_aug = (_dot_t(h, wq_ref[...]) * (FOX_HEAD_DIM ** -0.5 * LOG2E)
             + _dot(c3, eq_ref[...]) + oneq_ref[...])
    qT32 = q_aug.T
    k_aug = _dot_t(h, wk_ref[...]) + _dot(c3, ek_ref[...]) + onek_ref[...]
    kT32 = k_aug.T
    vT32 = (zvf[:, :VW] + onev_ref[...]).T
    qT = qT32.astype(BF16)
    k_aug = k_aug.astype(BF16)
    vT = vT32.astype(BF16)

    lane1 = lax.broadcasted_iota(jnp.int32, (1, LANES), 1)
    nq2 = jnp.zeros((1, LANES), F32)
    nk2 = jnp.zeros((1, LANES), F32)
    for hh in range(FOX_HEADS):
        qT_ref[hh] = qT[hh * HEAD_PAD:(hh + 1) * HEAD_PAD, :]
        ka_ref[hh] = k_aug[:, hh * HEAD_PAD:(hh + 1) * HEAD_PAD]
        vT_ref[hh] = vT[hh * V_PAD:(hh + 1) * V_PAD, :]
        qh = qT32[hh * HEAD_PAD:hh * HEAD_PAD + FOX_HEAD_DIM, :]
        kh = kT32[hh * HEAD_PAD:hh * HEAD_PAD + FOX_HEAD_DIM, :]
        kT32_ref[hh] = kh
        vT32_ref[hh] = vT32[hh * V_PAD:hh * V_PAD + FOX_HEAD_DIM, :]
        sqq_ref[hh:hh + 1, :] = jnp.sum(qh * kh, axis=0, keepdims=True)
        q2 = jnp.max(jnp.sum(qh * qh, axis=0, keepdims=True), axis=1, keepdims=True)
        k2 = jnp.max(jnp.sum(kh * kh, axis=0, keepdims=True), axis=1, keepdims=True)
        nq2 = jnp.where(lane1 == hh, q2, nq2)
        nk2 = jnp.where(lane1 == hh, k2, nk2)
    stats_ref[0] = jnp.concatenate(
        [nq2, nk2, c2[0:1, :], c2[tm - 1:tm, :], jnp.zeros((4, LANES), F32)], axis=0)

    zb = _dot_t(h, wb_ref[...])
    cos2 = cos_ref[...]
    sin2 = sin_ref[...]
    for hh in range(RET_HEADS):
        sl = slice(hh * RET_KEY_DIM, (hh + 1) * RET_KEY_DIM)
        qb_ref[:, sl] = _rotary(zb[:, sl], cos2, sin2).astype(BF16)
        xk = zb[:, RET_KW + hh * RET_KEY_DIM:RET_KW + (hh + 1) * RET_KEY_DIM]
        kb_ref[:, sl] = (_rotary(xk, cos2, sin2) * (RET_KEY_DIM ** -0.5)).astype(BF16)
    vb_ref[...] = zb[:, 2 * RET_KW:].astype(BF16)


def _inproj_prompt(x, g, wq, wk, wvf, bf, wb, cos2, sin2, consts, tm):
    S = x.shape[0]
    tri, eq, ek, oneq, onek, onev = consts
    row = lambda w: pl.BlockSpec((tm, w), lambda i: (i, 0))
    headT = pl.BlockSpec((FOX_HEADS, FOX_HEAD_DIM, tm), lambda i: (0, 0, i))
    in_specs = [row(D_MODEL), _const_spec(g.shape), _const_spec(wq.shape), _const_spec(wk.shape),
                _const_spec(wvf.shape),
                _const_spec(bf.shape), _const_spec(wb.shape), row(LANES), row(LANES),
                _const_spec(tri.shape), _const_spec(eq.shape), _const_spec(ek.shape),
                _const_spec(oneq.shape), _const_spec(onek.shape), _const_spec(onev.shape)]
    out_shape = (
        jax.ShapeDtypeStruct((FOX_HEADS, HEAD_PAD, S), BF16),
        jax.ShapeDtypeStruct((FOX_HEADS, S, HEAD_PAD), BF16),
        jax.ShapeDtypeStruct((FOX_HEADS, V_PAD, S), BF16),
        jax.ShapeDtypeStruct((FOX_HEADS, FOX_HEAD_DIM, S), F32),
        jax.ShapeDtypeStruct((FOX_HEADS, FOX_HEAD_DIM, S), F32),
        jax.ShapeDtypeStruct((FOX_HEADS, S), F32),
        jax.ShapeDtypeStruct((S, RET_KW), BF16),
        jax.ShapeDtypeStruct((S, RET_KW), BF16),
        jax.ShapeDtypeStruct((S, RET_VW), BF16),
        jax.ShapeDtypeStruct((S // tm, 8, LANES), F32),
        jax.ShapeDtypeStruct((FOX_HEADS, S), F32),
    )
    out_specs = (
        pl.BlockSpec((FOX_HEADS, HEAD_PAD, tm), lambda i: (0, 0, i)),
        pl.BlockSpec((FOX_HEADS, tm, HEAD_PAD), lambda i: (0, i, 0)),
        pl.BlockSpec((FOX_HEADS, V_PAD, tm), lambda i: (0, 0, i)),
        headT, headT, pl.BlockSpec((FOX_HEADS, tm), lambda i: (0, i)),
        row(RET_KW), row(RET_KW), row(RET_VW),
        pl.BlockSpec((1, 8, LANES), lambda i: (i, 0, 0)),
        pl.BlockSpec((FOX_HEADS, tm), lambda i: (0, i)),
    )
    return pl.pallas_call(
        _inproj_prompt_kernel,
        grid=(S // tm,),
        in_specs=in_specs,
        out_specs=out_specs,
        out_shape=out_shape,
        scratch_shapes=[pltpu.VMEM((1, LANES), F32)],
        compiler_params=pltpu.CompilerParams(dimension_semantics=("arbitrary",),
                                             vmem_limit_bytes=VMEM_LIMIT),
        name="inproj_prompt",
    )(x, g, wq, wk, wvf, bf, wb, cos2, sin2, tri, eq, ek, oneq, onek, onev)


def _inproj_sample_kernel(x_ref, g_ref, wqkv_ref, wf_ref, bf_ref, wb_ref, cos_ref, sin_ref,
                          q_ref, k32_ref, v32_ref, logf_ref, qb_ref, kb_ref, vb_ref):
    h = _rmsnorm(x_ref[...], g_ref[...]).astype(BF16)
    logf = _log_sigmoid(_dot_t(h, wf_ref[...]) + bf_ref[...])
    logf_ref[...] = logf[:, :FOX_HEADS]
    z = _dot_t(h, wqkv_ref[...])
    q_ref[...] = (z[:, :FOX_W] * (FOX_HEAD_DIM ** -0.5 * LOG2E)).astype(BF16)
    k32_ref[...] = z[:, FOX_W:2 * FOX_W]
    v32_ref[...] = z[:, 2 * FOX_W:]
    zb = _dot_t(h, wb_ref[...])
    cos2 = cos_ref[...]
    sin2 = sin_ref[...]
    for hh in range(RET_HEADS):
        sl = slice(hh * RET_KEY_DIM, (hh + 1) * RET_KEY_DIM)
        qb_ref[:, sl] = _rotary(zb[:, sl], cos2, sin2).astype(BF16)
        xk = zb[:, RET_KW + hh * RET_KEY_DIM:RET_KW + (hh + 1) * RET_KEY_DIM]
        kb_ref[:, sl] = (_rotary(xk, cos2, sin2) * (RET_KEY_DIM ** -0.5)).astype(BF16)
    vb_ref[...] = zb[:, 2 * RET_KW:].astype(BF16)


def _inproj_sample(x, g, wqkv, wf, bf, wb, cos2, sin2):
    M = x.shape[0]
    args = (x, g, wqkv, wf, bf, wb, cos2, sin2)
    out_shape = (
        jax.ShapeDtypeStruct((M, FOX_W), BF16),
        jax.ShapeDtypeStruct((M, FOX_W), F32),
        jax.ShapeDtypeStruct((M, FOX_W), F32),
        jax.ShapeDtypeStruct((M, FOX_HEADS), F32),
        jax.ShapeDtypeStruct((M, RET_KW), BF16),
        jax.ShapeDtypeStruct((M, RET_KW), BF16),
        jax.ShapeDtypeStruct((M, RET_VW), BF16),
    )
    return pl.pallas_call(
        _inproj_sample_kernel,
        grid=(1,),
        in_specs=[_const_spec(a.shape) for a in args],
        out_specs=tuple(_const_spec(o.shape) for o in out_shape),
        out_shape=out_shape,
        compiler_params=pltpu.CompilerParams(dimension_semantics=("arbitrary",),
                                             vmem_limit_bytes=VMEM_LIMIT),
        name="inproj_sample",
    )(*args)


def _fox_prompt_kernel(it_ref, jt_ref, jfetch_ref, mode_ref, qT_ref, ka_ref, vT_ref,
                       sqq_ref, o_ref, m_ref, acc_ref, *, n_strips):
    del jfetch_ref
    t = pl.program_id(0)
    i = it_ref[t]
    j = jt_ref[t]
    T = qT_ref.shape[2]
    SUB = T // n_strips
    EXACT = n_strips + 1

    def heads(fn):
        def body(hh, carry):
            fn(hh, mode_ref[t * FOX_HEADS + hh])
            return carry
        lax.fori_loop(0, FOX_HEADS, body, 0)

    def scores(hh):
        return _dot(ka_ref[hh], qT_ref[hh])

    def exact_head(hh):
        s = scores(hh)
        kk = lax.broadcasted_iota(jnp.int32, s.shape, 0) + j * T
        qq = lax.broadcasted_iota(jnp.int32, s.shape, 1) + i * T
        s = jnp.where(kk > qq, NEG, s)
        m_old = m_ref[hh]
        m_new = jnp.maximum(m_old, jnp.max(s, axis=0, keepdims=True))
        p = jnp.exp2(s - m_new).astype(BF16)
        alpha = jnp.exp2(m_old - m_new)
        acc_ref[hh] = alpha * acc_ref[hh] + _dot(vT_ref[hh], p)
        m_ref[hh] = m_new


    def diag_head(hh, mode):
        m_ref[hh] = sqq_ref[pl.ds(hh, 1), :]

        @pl.when(mode != EXACT)
        def _():
            half = T // 2
            m = m_ref[hh]
            kk = lax.broadcasted_iota(jnp.int32, (half, half), 0)
            qq = lax.broadcasted_iota(jnp.int32, (half, half), 1)
            tri = kk > qq
            s_lo = _dot(ka_ref[hh, :half, :], qT_ref[hh])
            s_lo = jnp.concatenate([jnp.where(tri, NEG, s_lo[:, :half]), s_lo[:, half:]], axis=1)
            acc_ref[hh] = _dot(vT_ref[hh, :, :half], jnp.exp2(s_lo - m).astype(BF16))
            s_hi = jnp.where(tri, NEG, _dot(ka_ref[hh, half:, :], qT_ref[hh, :, half:]))
            p_hi = jnp.exp2(s_hi - m[:, half:]).astype(BF16)
            acc_ref[hh, :, half:] += _dot(vT_ref[hh, :, half:], p_hi)

        @pl.when(mode == EXACT)
        def _():
            acc_ref[hh] = jnp.zeros((V_PAD, T), F32)
            exact_head(hh)

    def off_head(hh, mode):
        for nn in range(1, n_strips + 1):
            k0 = (n_strips - nn) * SUB

            @pl.when(mode == nn)
            def _():
                p = jnp.exp2(_dot(ka_ref[hh, k0:, :], qT_ref[hh]) - m_ref[hh]).astype(BF16)
                acc_ref[hh] += _dot(vT_ref[hh, :, k0:], p)

        @pl.when(mode == EXACT)
        def _():
            exact_head(hh)

    @pl.when(j == i)
    def _():
        heads(diag_head)

    @pl.when(j < i)
    def _():
        heads(off_head)

    @pl.when(j == 0)
    def _():
        for pr in range(FOX_HEADS // 2):
            halves = []
            for hh in (2 * pr, 2 * pr + 1):
                a = acc_ref[hh]
                halves.append(a[:FOX_HEAD_DIM] / a[FOX_HEAD_DIM:FOX_HEAD_DIM + 1])
            o_ref[pr] = jnp.concatenate(halves, axis=0).T.astype(BF16)


def _prune_tables(stats, it, jt, nb):
    per = stats.shape[0] // nb
    st = stats.reshape(nb, per, 8, LANES)[:, :, :, :FOX_HEADS]
    nq = jnp.sqrt(jnp.max(st[:, :, 0, :], axis=1)) * NORM_SLACK
    nk_strip = jnp.sqrt(st[:, :, 1, :]) * NORM_SLACK
    nk = jnp.max(nk_strip, axis=1)
    c_first = st[:, 0, 2, :]
    c_last = st[:, :, 3, :]
    bound = (nq[it][:, None, :] * (nk_strip[jt] + nk[it][:, None, :])
             - (c_last[jt] - c_first[it][:, None, :]))
    live = jnp.logical_or(jnp.asarray(jt == it)[:, None, None],
                          jnp.logical_not(bound < PRUNE_LOG2))
    strip_no = jnp.arange(per, dtype=jnp.int32)[None, :, None]
    n_keep = per - jnp.min(jnp.where(live, strip_no, per), axis=1)
    safe = nq[it] * (nk[jt] + nk[it]) < STALE_SAFE_LOG2
    mode = jnp.where(n_keep == 0, 0, jnp.where(safe, n_keep, per + 1))
    steps = jnp.arange(len(it), dtype=jnp.int32)
    last_live = lax.cummax(jnp.where(jnp.any(n_keep > 0, axis=1), steps, 0))
    return jnp.asarray(jt)[last_live], mode.astype(jnp.int32).reshape(-1), per


def _fox_prompt(qT, ka, vT, stats, sqq, T):
    S = ka.shape[1]
    nb = S // T
    it = np.array([i for i in range(nb) for _ in range(i + 1)], np.int32)
    jt = np.array([j for i in range(nb) for j in range(i, -1, -1)], np.int32)
    jfetch, mode, n_strips = _prune_tables(stats, it, jt, nb)
    grid_spec = pltpu.PrefetchScalarGridSpec(
        num_scalar_prefetch=4,
        grid=(len(it),),
        in_specs=[
            pl.BlockSpec((FOX_HEADS, HEAD_PAD, T), lambda t, it, jt, jf, md: (0, 0, it[t])),
            pl.BlockSpec((FOX_HEADS, T, HEAD_PAD), lambda t, it, jt, jf, md: (0, jf[t], 0)),
            pl.BlockSpec((FOX_HEADS, V_PAD, T), lambda t, it, jt, jf, md: (0, 0, jf[t])),
            pl.BlockSpec((FOX_HEADS, T), lambda t, it, jt, jf, md: (0, it[t])),
        ],
        out_specs=pl.BlockSpec((FOX_HEADS // 2, T, LANES),
                               lambda t, it, jt, jf, md: (0, it[t], 0)),
        scratch_shapes=[pltpu.VMEM((FOX_HEADS, 1, T), F32),
                        pltpu.VMEM((FOX_HEADS, V_PAD, T), F32)],
    )
    return pl.pallas_call(
        functools.partial(_fox_prompt_kernel, n_strips=n_strips),
        grid_spec=grid_spec,
        out_shape=jax.ShapeDtypeStruct((FOX_HEADS // 2, S, LANES), BF16),
        compiler_params=pltpu.CompilerParams(dimension_semantics=("arbitrary",),
                                             vmem_limit_bytes=VMEM_LIMIT),
        name="fox_prompt",
    )(jnp.asarray(it), jnp.asarray(jt), jfetch, mode, qT, ka, vT, sqq)


def _fox_sample_kernel(q_ref, kn_ref, vn_ref, ckT_ref, cvT_ref, lfT_ref, up_ref, ex_ref, o_ref):
    P = ckT_ref.shape[3]
    Tn = q_ref.shape[0]
    KP = lfT_ref.shape[2]
    HQ = FOX_HEADS * Tn
    nchunk = KP // LANES
    nt = (((1,), (1,)), ((), ()))

    def stack3(x):
        parts3 = [t.astype(F32) for t in _split3(x)] + [jnp.zeros_like(x)]
        return jnp.concatenate(parts3, axis=0).astype(BF16)

    x3 = stack3(lfT_ref[0])
    up = up_ref[...]
    parts = [_dot(x3[:, cidx * LANES:(cidx + 1) * LANES], up) for cidx in range(nchunk)]
    run = jnp.zeros((4 * FOX_HEADS, 1), F32)
    for cidx in range(nchunk):
        total = parts[cidx][:, LANES - 1:LANES]
        parts[cidx] = parts[cidx] + run
        run = run + total
    y = jnp.concatenate(parts, axis=1)
    cT = (y[:FOX_HEADS] + y[FOX_HEADS:2 * FOX_HEADS] + y[2 * FOX_HEADS:3 * FOX_HEADS]) * LOG2E
    ckx = _dot(ex_ref[...], stack3(cT))

    tail = ckx[:, P:P + LANES]
    rowq = lax.broadcasted_iota(jnp.int32, tail.shape, 0) % Tn
    lanek = lax.broadcasted_iota(jnp.int32, tail.shape, 1)
    cq = jnp.sum(jnp.where(lanek == rowq, tail, 0.0), axis=1, keepdims=True)

    q = q_ref[...]
    qt = jnp.concatenate([q] * FOX_HEADS, axis=0)
    rh = lax.broadcasted_iota(jnp.int32, qt.shape, 0) // Tn
    lh = lax.broadcasted_iota(jnp.int32, qt.shape, 1) // FOX_HEAD_DIM
    qbd = jnp.where(rh == lh, qt, jnp.zeros_like(qt))

    kT = ckT_ref[0].reshape(FOX_W, P).astype(BF16)
    vT = cvT_ref[0].reshape(FOX_W, P).astype(BF16)
    s_c = _dot(qbd, kT) + cq - ckx[:, :P]
    s_n = lax.dot_general(qbd, kn_ref[...].astype(BF16), nt, preferred_element_type=F32)
    s_n = s_n + cq - ckx[:, P:P + Tn]
    key = lax.broadcasted_iota(jnp.int32, s_n.shape, 1)
    qrow = lax.broadcasted_iota(jnp.int32, s_n.shape, 0) % Tn
    s_n = jnp.where(key > qrow, NEG, s_n)
    m = jnp.maximum(jnp.max(s_c, axis=1, keepdims=True), jnp.max(s_n, axis=1, keepdims=True))
    p_c = jnp.exp2(s_c - m)
    p_n = jnp.exp2(s_n - m)
    l = jnp.sum(p_c, axis=1, keepdims=True) + jnp.sum(p_n, axis=1, keepdims=True)
    z = lax.dot_general(p_c.astype(BF16), vT, nt, preferred_element_type=F32)
    z = (z + _dot(p_n.astype(BF16), vn_ref[...].astype(BF16))) / l
    zh = lax.broadcasted_iota(jnp.int32, (Tn, FOX_W), 1) // FOX_HEAD_DIM
    o = jnp.zeros((Tn, FOX_W), F32)
    for hh in range(FOX_HEADS):
        o = o + jnp.where(zh == hh, z[hh * Tn:(hh + 1) * Tn, :], 0.0)
    o_ref[...] = o.astype(BF16)


def _fox_sample(q, kn, vn, cache_kT, cache_vT, lfT, B, Tn):
    P = cache_kT.shape[3]
    KP = lfT.shape[2]
    HQ = FOX_HEADS * Tn
    up = jnp.asarray(np.triu(np.ones((LANES, LANES), np.float32)), BF16)
    ex = np.zeros((HQ, 4 * FOX_HEADS), np.float32)
    for part in range(3):
        for hh in range(FOX_HEADS):
            ex[hh * Tn:(hh + 1) * Tn, part * FOX_HEADS + hh] = 1.0
    ex = jnp.asarray(ex, BF16)
    rowb = lambda w: pl.BlockSpec((Tn, w), lambda b: (b, 0))
    return pl.pallas_call(
        _fox_sample_kernel,
        grid=(B,),
        in_specs=[rowb(FOX_W), rowb(FOX_W), rowb(FOX_W),
                  pl.BlockSpec((1, FOX_HEADS, FOX_HEAD_DIM, P), lambda b: (b, 0, 0, 0)),
                  pl.BlockSpec((1, FOX_HEADS, FOX_HEAD_DIM, P), lambda b: (b, 0, 0, 0)),
                  pl.BlockSpec((1, FOX_HEADS, KP), lambda b: (b, 0, 0)),
                  _const_spec(up.shape), _const_spec(ex.shape)],
        out_specs=rowb(FOX_W),
        out_shape=jax.ShapeDtypeStruct((B * Tn, FOX_W), BF16),
        compiler_params=pltpu.CompilerParams(dimension_semantics=("arbitrary",),
                                             vmem_limit_bytes=VMEM_LIMIT),
        name="fox_sample",
    )(q, kn, vn, cache_kT, cache_vT, lfT, up, ex)


def _retention_kernel(q_ref, k_ref, v_ref, s0_ref, dmat_ref, xi_ref, zeta_ref, gam_ref,
                      n_ref, sout_ref, st_ref):
    c = pl.program_id(1)
    G = st_ref.shape[0]
    C = dmat_ref.shape[1]
    n_units = q_ref.shape[0] // C
    per_stream = n_units // G

    @pl.when(c == 0)
    def _():
        st_ref[...] = s0_ref[...]

    for unit in range(n_units):
        g = unit // per_stream
        rows = slice(unit * C, (unit + 1) * C)
        for hh in range(RET_HEADS):
            q = q_ref[rows, hh * RET_KEY_DIM:(hh + 1) * RET_KEY_DIM]
            k = k_ref[rows, hh * RET_KEY_DIM:(hh + 1) * RET_KEY_DIM]
            v = v_ref[rows, hh * RET_VAL_DIM:(hh + 1) * RET_VAL_DIM]
            st = st_ref[g, hh]
            sc = lax.dot_general(q, k, (((1,), (1,)), ((), ())), preferred_element_type=F32)
            sc = sc * dmat_ref[hh]
            o = _dot(sc.astype(BF16), v) + _dot(q, st.astype(BF16)) * xi_ref[hh]
            kz = (k.astype(F32) * zeta_ref[hh]).astype(BF16)
            upd = lax.dot_general(kz, v, (((0,), (0,)), ((), ())), preferred_element_type=F32)
            st_ref[g, hh] = gam_ref[hh] * st + upd
            mu = jnp.mean(o, axis=-1, keepdims=True)
            d = o - mu
            var = jnp.mean(d * d, axis=-1, keepdims=True)
            n_ref[rows, hh * RET_VAL_DIM:(hh + 1) * RET_VAL_DIM] = (
                d * lax.rsqrt(var + EPS)).astype(BF16)

    @pl.when(c == pl.num_programs(1) - 1)
    def _():
        sout_ref[...] = st_ref[...]


def _retention(q, k, v, state0, B, L, C, G=1, per_step=1):
    assert L % (C * per_step) == 0
    nc = L // (C * per_step)
    assert B % G == 0 and (G == 1 or (nc == 1 and per_step == 1))
    f32 = np.float32
    lg = np.log(f32(1.0) - np.exp2(f32(-5.0) - np.arange(RET_HEADS, dtype=f32))).astype(f32)
    idx = np.arange(C, dtype=f32)
    diff = idx[:, None] - idx[None, :]
    dmat = np.where(diff[None] >= 0, np.exp(np.maximum(diff, 0)[None] * lg[:, None, None]), 0)
    xi = np.exp((idx[None, :] + f32(1.0)) * lg[:, None])
    zeta = np.exp((f32(C) - f32(1.0) - idx[None, :]) * lg[:, None])
    xi = np.broadcast_to(xi[:, :, None], (RET_HEADS, C, RET_VAL_DIM))
    zeta = np.broadcast_to(zeta[:, :, None], (RET_HEADS, C, RET_KEY_DIM))
    gam = np.broadcast_to(np.exp(f32(C) * lg)[:, None, None], (RET_HEADS, 1, RET_VAL_DIM))
    dmat, xi, zeta, gam = (jnp.asarray(a, F32) for a in (dmat, xi, zeta, gam))
    rowc = lambda w: pl.BlockSpec((G * per_step * C, w), lambda b, c: (b * nc + c, 0))
    st_spec = pl.BlockSpec((G, RET_HEADS, RET_KEY_DIM, RET_VAL_DIM), lambda b, c: (b, 0, 0, 0))
    return pl.pallas_call(
        _retention_kernel,
        grid=(B // G, nc),
        in_specs=[rowc(RET_KW), rowc(RET_KW), rowc(RET_VW), st_spec,
                  _const_spec(dmat.shape), _const_spec(xi.shape), _const_spec(zeta.shape),
                  _const_spec(gam.shape)],
        out_specs=(rowc(RET_VW), st_spec),
        out_shape=(jax.ShapeDtypeStruct((B * L, RET_VW), BF16),
                   jax.ShapeDtypeStruct((B, RET_HEADS, RET_KEY_DIM, RET_VAL_DIM), F32)),
        scratch_shapes=[pltpu.VMEM((G, RET_HEADS, RET_KEY_DIM, RET_VAL_DIM), F32)],
        compiler_params=pltpu.CompilerParams(dimension_semantics=("arbitrary", "arbitrary"),
                                             vmem_limit_bytes=VMEM_LIMIT),
        name="retention",
    )(q, k, v, state0, dmat, xi, zeta, gam)


def _mixer_ffn_kernel(x_ref, oa_ref, nb_ref, prev_ref, gmix_ref, wg_ref, gng_ref, wpa_ref, wpb_ref,
                      wo_ref, gffn_ref, wup_ref, cw_ref, cb_ref, wdn_ref, gfin_ref,
                      y_ref, conv_ref, carry_ref, ua_ref, ub_ref, acc_ref, h2_ref,
                      *, nseg, seglen):
    i = pl.program_id(0)
    NC = N_FFN_CHUNKS
    PADR = 8
    H0 = PADR - (CONV_WIDTH - 1)

    @pl.when(i == 0)
    def _():
        carry_ref[...] = prev_ref[...]

    x = x_ref[...]
    h = _rmsnorm(x, gmix_ref[...]).astype(BF16)
    zg = _dot_t(h, wg_ref[...])
    gb = zg[:, :RET_VW]
    gma = zg[:, RET_VW:RET_VW + D_MODEL]
    gmb = zg[:, RET_VW + D_MODEL:]
    oa = jnp.concatenate([oa_ref[p] for p in range(FOX_HEADS // 2)], axis=1)
    ya = _dot(oa, wpa_ref[...])
    nn = nb_ref[...].astype(F32) * gng_ref[...] * (gb * jax.nn.sigmoid(gb))
    yb = _dot(nn.astype(BF16), wpb_ref[...])
    y = jax.nn.sigmoid(gma) * ya + jax.nn.sigmoid(gmb) * yb
    x1 = x + _dot(y.astype(BF16), wo_ref[...])
    h2_ref[...] = _rmsnorm(x1, gffn_ref[...]).astype(BF16)
    acc_ref[...] = x1

    h2 = h2_ref[...]

    def cols(cidx):
        return slice(cidx * FFN_CHUNK, (cidx + 1) * FFN_CHUNK)

    def up_half(u_ref, cidx, slot):
        u = _dot(h2, wup_ref[:, cols(cidx)])
        for s in range(nseg):
            u_ref[slot, s, PADR:PADR + seglen, :] = u[s * seglen:(s + 1) * seglen, :]
            u_ref[slot, s, H0:PADR, :] = carry_ref[s, H0:PADR, cols(cidx)]
            carry_ref[s, H0:PADR, cols(cidx)] = (
                u[(s + 1) * seglen - (CONV_WIDTH - 1):(s + 1) * seglen, :])

    def conv_half(u_ref, cidx, slot):
        w = cw_ref[:, cols(cidx)]
        b = cb_ref[:, cols(cidx)]
        outs = []
        for s in range(nseg):
            acc = w[0:1] * u_ref[slot, s, H0:H0 + seglen, :]
            for jj in range(1, CONV_WIDTH):
                acc = acc + w[jj:jj + 1] * u_ref[slot, s, H0 + jj:H0 + jj + seglen, :]
            outs.append(b + acc)
        return outs[0] if nseg == 1 else jnp.concatenate(outs, axis=0)

    def stage_up(c):
        up_half(ua_ref, c, c % UP_AHEAD_SLOTS)
        up_half(ub_ref, NC + c, c % UP_AHEAD_SLOTS)

    for c in range(UP_AHEAD_SLOTS - 1):
        stage_up(c)
    group = []
    for c in range(NC):
        if c + UP_AHEAD_SLOTS - 1 < NC:
            stage_up(c + UP_AHEAD_SLOTS - 1)
        a = conv_half(ua_ref, c, c % UP_AHEAD_SLOTS)
        b = conv_half(ub_ref, NC + c, c % UP_AHEAD_SLOTS)
        group.append((_gelu_tanh(a) * b).astype(BF16))
        if len(group) == DOWN_GROUP or c + 1 == NC:
            r0 = (c + 1 - len(group)) * FFN_CHUNK
            wd = wdn_ref[r0:(c + 1) * FFN_CHUNK, :]
            acc_ref[...] += _dot(jnp.concatenate(group, axis=1), wd)
            group = []
    y_ref[...] = _rmsnorm(acc_ref[...], gfin_ref[...])

    @pl.when(i == pl.num_programs(0) - 1)
    def _():
        conv_ref[...] = carry_ref[:, H0:PADR, :]


def _mixer_ffn(x, oa, nb, prev, weights, tm, nseg, seglen):
    M = x.shape[0]
    gmix, wg, gng, wpa, wpb, wo, gffn, wup, cw, cb, wdn, gfin = weights
    FC = FFN_CHUNK
    NC = N_FFN_CHUNKS
    row = lambda w: pl.BlockSpec((tm, w), lambda i: (i, 0))
    wspec = lambda a: pl.BlockSpec(a.shape, lambda i, n=a.ndim: (0,) * n,
                                   pipeline_mode=pl.Buffered(1))
    in_specs = [row(D_MODEL),
                pl.BlockSpec((FOX_HEADS // 2, tm, LANES), lambda i: (0, i, 0)),
                row(RET_VW), wspec(prev)] + [wspec(w) for w in weights]
    out_shape = (jax.ShapeDtypeStruct((M, D_MODEL), F32),
                 jax.ShapeDtypeStruct((nseg, CONV_WIDTH - 1, 2 * FFN_DIM), F32))
    out_specs = (row(D_MODEL), _const_spec(out_shape[1].shape))
    return pl.pallas_call(
        functools.partial(_mixer_ffn_kernel, nseg=nseg, seglen=seglen),
        grid=(M // tm,),
        in_specs=in_specs,
        out_specs=out_specs,
        out_shape=out_shape,
        scratch_shapes=[pltpu.VMEM((nseg, 8, 2 * FFN_DIM), F32),
                        pltpu.VMEM((UP_AHEAD_SLOTS, nseg, 8 + seglen, FC), F32),
                        pltpu.VMEM((UP_AHEAD_SLOTS, nseg, 8 + seglen, FC), F32),
                        pltpu.VMEM((tm, D_MODEL), F32),
                        pltpu.VMEM((tm, D_MODEL), BF16)],
        compiler_params=pltpu.CompilerParams(dimension_semantics=("arbitrary",),
                                             vmem_limit_bytes=VMEM_LIMIT),
        name="mixer_ffn",
    )(x, oa, nb, prev, *weights)


def _rotary_tables(start, n):
    half = RET_KEY_DIM // 2
    inv = 1.0 / (ROPE_BASE ** jnp.linspace(0.0, 1.0, half, dtype=F32))
    fine = min(n, ROT_FINE)
    assert n % fine == 0
    a_hi = (start + fine * jnp.arange(n // fine)).astype(F32)[:, None] * inv[None, :]
    a_lo = jnp.arange(fine).astype(F32)[:, None] * inv[None, :]
    ch, sh = jnp.cos(a_hi)[:, None, :], jnp.sin(a_hi)[:, None, :]
    cl, sl = jnp.cos(a_lo)[None, :, :], jnp.sin(a_lo)[None, :, :]
    cos = (ch * cl - sh * sl).reshape(n, half)
    sin = (sh * cl + ch * sl).reshape(n, half)
    return jnp.concatenate([cos, cos], axis=1), jnp.concatenate([-sin, sin], axis=1)


def _pad_heads(wt, pad):
    d = wt.shape[1]
    wt = wt.reshape(FOX_HEADS, FOX_HEAD_DIM, d)
    wt = jnp.pad(wt, ((0, 0), (0, pad - FOX_HEAD_DIM), (0, 0)))
    return wt.reshape(FOX_HEADS * pad, d)


def _prompt_consts(tm):
    tri = np.tril(np.ones((tm, tm), np.float32))
    eq = np.zeros((LANES, FOX_HEADS * HEAD_PAD), np.float32)
    ek = np.zeros((LANES, FOX_HEADS * HEAD_PAD), np.float32)
    oneq = np.zeros((1, FOX_HEADS * HEAD_PAD), np.float32)
    onek = np.zeros((1, FOX_HEADS * HEAD_PAD), np.float32)
    onev = np.zeros((1, FOX_HEADS * V_PAD), np.float32)
    for hh in range(FOX_HEADS):
        base = hh * HEAD_PAD + BIAS_COL
        for part in range(3):
            eq[part * FOX_HEADS + hh, base + part] = 1.0
            ek[part * FOX_HEADS + hh, base + 3 + part] = -1.0
            onek[0, base + part] = 1.0
            oneq[0, base + 3 + part] = 1.0
        onev[0, hh * V_PAD + FOX_HEAD_DIM] = 1.0
    return (jnp.asarray(tri, BF16), jnp.asarray(eq, BF16), jnp.asarray(ek, BF16),
            jnp.asarray(oneq), jnp.asarray(onek), jnp.asarray(onev))


def _tile(n, pref):
    t = min(n, pref)
    while n % t:
        t //= 2
    return t


def kernel(x_prompt, x_sample, cache_fox_k, cache_fox_v, cache_fox_logf, state_ret, state_ffn_conv,
           norm_mix_g, w_in, b_fox_f, gn_ret_g, w_pa, w_pb, w_o, norm_ffn_g, w_up, conv_w, conv_b,
           w_down, norm_final_g):
    depth = w_in.shape[0]
    Bp, S, _ = x_prompt.shape
    Bs, Ts, _ = x_sample.shape
    P = cache_fox_k.shape[2]
    assert depth == 1 and Bp == 1, "kernel handles the single-layer, single-prompt configuration"
    l = 0

    wt = jnp.swapaxes(w_in[l], 0, 1).astype(BF16)
    o0 = 3 * FOX_W
    o1 = o0 + FOX_HEADS
    o2 = o1 + 2 * RET_KW + RET_VW
    wq_aug = _pad_heads(wt[:FOX_W], HEAD_PAD)
    wk_aug = _pad_heads(wt[FOX_W:2 * FOX_W], HEAD_PAD)
    wf = jnp.pad(wt[o0:o1], ((0, LANES - FOX_HEADS), (0, 0)))
    wvf = jnp.concatenate([_pad_heads(wt[2 * FOX_W:o0], V_PAD), wf], axis=0)
    wqkv = wt[:o0]
    bf = jnp.pad(b_fox_f[l].astype(F32), (0, LANES - FOX_HEADS))[None, :]
    wb = wt[o1:o2]
    wg = wt[o2:]
    gmix = norm_mix_g[l].astype(F32)[None, :]
    mix_weights = (
        gmix, wg, gn_ret_g[l].astype(F32)[None, :], w_pa[l].astype(BF16), w_pb[l].astype(BF16),
        w_o[l].astype(BF16), norm_ffn_g[l].astype(F32)[None, :],
        w_up[l].astype(BF16),
        jnp.pad(conv_w[l].astype(F32), ((0, 8 - CONV_WIDTH), (0, 0))),
        conv_b[l].astype(F32)[None, :],
        w_down[l].astype(BF16),
        norm_final_g.astype(F32)[None, :],
    )
    hist_pad = ((0, 0), (8 - (CONV_WIDTH - 1), 0), (0, 0))

    tm_a = _tile(S, 512)
    cos_p, sin_p = _rotary_tables(0, S)
    (qT, ka, vT, kT_p, vT_p, logfT_p, qb, kb, vb, stats, sqq) = _inproj_prompt(
        x_prompt[0], gmix, wq_aug, wk_aug, wvf, bf, wb, cos_p, sin_p,
        _prompt_consts(tm_a), tm_a)
    oa_p = _fox_prompt(qT, ka, vT, stats, sqq, _tile(S, 1024))
    zero_state = jnp.zeros((1, RET_HEADS, RET_KEY_DIM, RET_VAL_DIM), F32)
    c_ret = _tile(S, 256)
    nb_p, ret_p = _retention(qb, kb, vb, zero_state, 1, S, c_ret,
                             per_step=math.gcd(S // c_ret, 4))
    tm_d = _tile(S, 256)
    zero_prev = jnp.zeros((1, 8, 2 * FFN_DIM), F32)
    y_p, conv_p = _mixer_ffn(x_prompt[0], oa_p, nb_p, zero_prev, mix_weights, tm_d, 1, tm_d)

    Ms = Bs * Ts
    cos_s, sin_s = _rotary_tables(P, Ts)
    cos_s = jnp.tile(cos_s, (Bs, 1))
    sin_s = jnp.tile(sin_s, (Bs, 1))
    (q_s, k_s, v_s, logf_s, qb_s, kb_s, vb_s) = _inproj_sample(
        x_sample.reshape(Ms, D_MODEL), gmix, wqkv, wf, bf, wb, cos_s, sin_s)
    KP = ((P + Ts + LANES - 1) // LANES) * LANES
    lf_all = jnp.concatenate([cache_fox_logf[l].astype(F32), logf_s.reshape(Bs, Ts, FOX_HEADS)], axis=1)
    lfT = jnp.pad(jnp.swapaxes(lf_all, 1, 2), ((0, 0), (0, 0), (0, KP - P - Ts)))
    oa_s = _fox_sample(q_s, k_s, v_s, jnp.transpose(cache_fox_k[l], (0, 2, 3, 1)),
                       jnp.transpose(cache_fox_v[l], (0, 2, 3, 1)), lfT, Bs, Ts)
    oa_s = jnp.moveaxis(oa_s.reshape(Ms, FOX_HEADS // 2, LANES), 1, 0)
    nb_s, ret_s = _retention(qb_s, kb_s, vb_s, state_ret[l].astype(F32), Bs, Ts, Ts,
                             G=math.gcd(Bs, 4))
    prev_s = jnp.pad(state_ffn_conv[l].astype(F32), hist_pad)
    y_s, conv_s = _mixer_ffn(x_sample.reshape(Ms, D_MODEL), oa_s, nb_s, prev_s, mix_weights,
                             Ms, Bs, Ts)

    hshape = (FOX_HEADS, FOX_HEAD_DIM)
    return (
        y_p[None],
        y_s.reshape(Bs, Ts, D_MODEL),
        jnp.transpose(kT_p, (2, 0, 1))[None, None],
        jnp.transpose(vT_p, (2, 0, 1))[None, None],
        jnp.transpose(logfT_p, (1, 0))[None, None],
        ret_p[None],
        conv_p[None],
        k_s.reshape((1, Bs, Ts) + hshape),
        v_s.reshape((1, Bs, Ts) + hshape),
        logf_s.reshape(1, Bs, Ts, FOX_HEADS),
        ret_s[None],
        conv_s[None],
    )
```

```python
import functools
import math

import numpy as np
import jax
import jax.numpy as jnp
from jax import lax
from jax.experimental import pallas as pl
from jax.experimental.pallas import tpu as pltpu

F32 = jnp.float32
BF16 = jnp.bfloat16

D_MODEL = 1024
FOX_HEADS = 8
FOX_HEAD_DIM = 64
RET_HEADS = 4
RET_KEY_DIM = 128
RET_VAL_DIM = 256
FFN_DIM = 2816
CONV_WIDTH = 3
EPS = 1e-6
ROPE_BASE = 10000.0

FOX_W = FOX_HEADS * FOX_HEAD_DIM
RET_KW = RET_HEADS * RET_KEY_DIM
RET_VW = RET_HEADS * RET_VAL_DIM

LOG2E = 1.4426950408889634
LANES = 128
HEAD_PAD = LANES
V_PAD = 80
BIAS_COL = FOX_HEAD_DIM
NEG = -1e30
STALE_SAFE_LOG2 = 64.0
PRUNE_LOG2 = -160.0
NORM_SLACK = 1.02
HEAD_GROUP = 2
ROT_FINE = 128
FFN_CHUNK = 256
N_FFN_CHUNKS = FFN_DIM // FFN_CHUNK
DOWN_GROUP = 4
UP_AHEAD_SLOTS = 3
VMEM_LIMIT = 56 * 1024 * 1024


def _rmsnorm(x, g):
    ms = jnp.mean(x * x, axis=-1, keepdims=True)
    return x * lax.rsqrt(ms + EPS) * g


def _split3(x):
    hi = x.astype(BF16)
    r1 = x - hi.astype(F32)
    mid = r1.astype(BF16)
    lo = (r1 - mid.astype(F32)).astype(BF16)
    return hi, mid, lo


def _log_sigmoid(x):
    return jnp.minimum(x, 0.0) - jnp.log1p(jnp.exp(-jnp.abs(x)))


def _dot(a, b):
    return jnp.dot(a, b, preferred_element_type=F32)


def _dot_t(a, bt):
    return lax.dot_general(a, bt, (((1,), (1,)), ((), ())), preferred_element_type=F32)


def _gelu_tanh(x):
    c0 = math.sqrt(2.0 / math.pi)
    hx = 0.5 * x
    return hx + hx * jnp.tanh(x * (c0 + (c0 * 0.044715) * (x * x)))


def _rotary(x, cos2, sin2):
    return x * cos2 + pltpu.roll(x, RET_KEY_DIM // 2, 1) * sin2


def _const_spec(shape):
    n = len(shape)
    return pl.BlockSpec(shape, lambda *_: (0,) * n)


def _inproj_prompt_kernel(x_ref, g_ref, wq_ref, wk_ref, wvf_ref, bf_ref, wb_ref,
                          cos_ref, sin_ref, tri_ref, eq_ref, ek_ref, oneq_ref, onek_ref, onev_ref,
                          qT_ref, ka_ref, vT_ref, kT32_ref, vT32_ref, logf_ref, qb_ref, kb_ref, vb_ref,
                          stats_ref, sqq_ref, carry_ref):
    tm = x_ref.shape[0]
    VW = FOX_HEADS * V_PAD

    @pl.when(pl.program_id(0) == 0)
    def _():
        carry_ref[...] = jnp.zeros_like(carry_ref)

    h = _rmsnorm(x_ref[...], g_ref[...]).astype(BF16)
    zvf = _dot_t(h, wvf_ref[...])

    logf = _log_sigmoid(zvf[:, VW:] + bf_ref[...])
    logf_ref[...] = logf.T[:FOX_HEADS, :]
    lane = lax.broadcasted_iota(jnp.int32, logf.shape, 1)
    logf = jnp.where(lane < FOX_HEADS, logf, 0.0)
    r = _dot(tri_ref[...], jnp.concatenate(_split3(logf), axis=1))
    c = r[:, :LANES] + r[:, LANES:2 * LANES] + r[:, 2 * LANES:] + carry_ref[...]
    carry_ref[...] = c[tm - 1:tm, :]
    c2 = c * LOG2E
    hi, mid, lo = (t.astype(F32) for t in _split3(c2))
    c3 = (hi + pltpu.roll(mid, FOX_HEADS, 1) + pltpu.roll(lo, 2 * FOX_HEADS, 1)).astype(BF16)

    q_aug = (_dot_t(h, wq_ref[...]) * (FOX_HEAD_DIM ** -0.5 * LOG2E)
             + _dot(c3, eq_ref[...]) + oneq_ref[...])
    qT32 = q_aug.T
    k_aug = _dot_t(h, wk_ref[...]) + _dot(c3, ek_ref[...]) + onek_ref[...]
    kT32 = k_aug.T
    vT32 = (zvf[:, :VW] + onev_ref[...]).T
    qT = qT32.astype(BF16)
    k_aug = k_aug.astype(BF16)
    vT = vT32.astype(BF16)

    lane1 = lax.broadcasted_iota(jnp.int32, (1, LANES), 1)
    nq2 = jnp.zeros((1, LANES), F32)
    nk2 = jnp.zeros((1, LANES), F32)
    for hh in range(FOX_HEADS):
        qT_ref[hh] = qT[hh * HEAD_PAD:(hh + 1) * HEAD_PAD, :]
        ka_ref[hh] = k_aug[:, hh * HEAD_PAD:(hh + 1) * HEAD_PAD]
        vT_ref[hh] = vT[hh * V_PAD:(hh + 1) * V_PAD, :]
        qh = qT32[hh * HEAD_PAD:hh * HEAD_PAD + FOX_HEAD_DIM, :]
        kh = kT32[hh * HEAD_PAD:hh * HEAD_PAD + FOX_HEAD_DIM, :]
        kT32_ref[hh] = kh
        vT32_ref[hh] = vT32[hh * V_PAD:hh * V_PAD + FOX_HEAD_DIM, :]
        sqq_ref[hh:hh + 1, :] = jnp.sum(qh * kh, axis=0, keepdims=True)
        q2 = jnp.max(jnp.sum(qh * qh, axis=0, keepdims=True), axis=1, keepdims=True)
        k2 = jnp.max(jnp.sum(kh * kh, axis=0, keepdims=True), axis=1, keepdims=True)
        nq2 = jnp.where(lane1 == hh, q2, nq2)
        nk2 = jnp.where(lane1 == hh, k2, nk2)
    stats_ref[0] = jnp.concatenate(
        [nq2, nk2, c2[0:1, :], c2[tm - 1:tm, :], jnp.zeros((4, LANES), F32)], axis=0)

    zb = _dot_t(h, wb_ref[...])
    cos2 = cos_ref[...]
    sin2 = sin_ref[...]
    for hh in range(RET_HEADS):
        sl = slice(hh * RET_KEY_DIM, (hh + 1) * RET_KEY_DIM)
        qb_ref[:, sl] = _rotary(zb[:, sl], cos2, sin2).astype(BF16)
        xk = zb[:, RET_KW + hh * RET_KEY_DIM:RET_KW + (hh + 1) * RET_KEY_DIM]
        kb_ref[:, sl] = (_rotary(xk, cos2, sin2) * (RET_KEY_DIM ** -0.5)).astype(BF16)
    vb_ref[...] = zb[:, 2 * RET_KW:].astype(BF16)


def _inproj_prompt(x, g, wq, wk, wvf, bf, wb, cos2, sin2, consts, tm):
    S = x.shape[0]
    tri, eq, ek, oneq, onek, onev = consts
    row = lambda w: pl.BlockSpec((tm, w), lambda i: (i, 0))
    headT = pl.BlockSpec((FOX_HEADS, FOX_HEAD_DIM, tm), lambda i: (0, 0, i))
    in_specs = [row(D_MODEL), _const_spec(g.shape), _const_spec(wq.shape), _const_spec(wk.shape),
                _const_spec(wvf.shape),
                _const_spec(bf.shape), _const_spec(wb.shape), row(LANES), row(LANES),
                _const_spec(tri.shape), _const_spec(eq.shape), _const_spec(ek.shape),
                _const_spec(oneq.shape), _const_spec(onek.shape), _const_spec(onev.shape)]
    out_shape = (
        jax.ShapeDtypeStruct((FOX_HEADS, HEAD_PAD, S), BF16),
        jax.ShapeDtypeStruct((FOX_HEADS, S, HEAD_PAD), BF16),
        jax.ShapeDtypeStruct((FOX_HEADS, V_PAD, S), BF16),
        jax.ShapeDtypeStruct((FOX_HEADS, FOX_HEAD_DIM, S), F32),
        jax.ShapeDtypeStruct((FOX_HEADS, FOX_HEAD_DIM, S), F32),
        jax.ShapeDtypeStruct((FOX_HEADS, S), F32),
        jax.ShapeDtypeStruct((S, RET_KW), BF16),
        jax.ShapeDtypeStruct((S, RET_KW), BF16),
        jax.ShapeDtypeStruct((S, RET_VW), BF16),
        jax.ShapeDtypeStruct((S // tm, 8, LANES), F32),
        jax.ShapeDtypeStruct((FOX_HEADS, S), F32),
    )
    out_specs = (
        pl.BlockSpec((FOX_HEADS, HEAD_PAD, tm), lambda i: (0, 0, i)),
        pl.BlockSpec((FOX_HEADS, tm, HEAD_PAD), lambda i: (0, i, 0)),
        pl.BlockSpec((FOX_HEADS, V_PAD, tm), lambda i: (0, 0, i)),
        headT, headT, pl.BlockSpec((FOX_HEADS, tm), lambda i: (0, i)),
        row(RET_KW), row(RET_KW), row(RET_VW),
        pl.BlockSpec((1, 8, LANES), lambda i: (i, 0, 0)),
        pl.BlockSpec((FOX_HEADS, tm), lambda i: (0, i)),
    )
    return pl.pallas_call(
        _inproj_prompt_kernel,
        grid=(S // tm,),
        in_specs=in_specs,
        out_specs=out_specs,
        out_shape=out_shape,
        scratch_shapes=[pltpu.VMEM((1, LANES), F32)],
        compiler_params=pltpu.CompilerParams(dimension_semantics=("arbitrary",),
                                             vmem_limit_bytes=VMEM_LIMIT),
        name="inproj_prompt",
    )(x, g, wq, wk, wvf, bf, wb, cos2, sin2, tri, eq, ek, oneq, onek, onev)


def _inproj_sample_kernel(x_ref, g_ref, wqkv_ref, wf_ref, bf_ref, wb_ref, cos_ref, sin_ref,
                          q_ref, k32_ref, v32_ref, logf_ref, qb_ref, kb_ref, vb_ref):
    h = _rmsnorm(x_ref[...], g_ref[...]).astype(BF16)
    logf = _log_sigmoid(_dot_t(h, wf_ref[...]) + bf_ref[...])
    logf_ref[...] = logf[:, :FOX_HEADS]
    z = _dot_t(h, wqkv_ref[...])
    q_ref[...] = (z[:, :FOX_W] * (FOX_HEAD_DIM ** -0.5 * LOG2E)).astype(BF16)
    k32_ref[...] = z[:, FOX_W:2 * FOX_W]
    v32_ref[...] = z[:, 2 * FOX_W:]
    zb = _dot_t(h, wb_ref[...])
    cos2 = cos_ref[...]
    sin2 = sin_ref[...]
    for hh in range(RET_HEADS):
        sl = slice(hh * RET_KEY_DIM, (hh + 1) * RET_KEY_DIM)
        qb_ref[:, sl] = _rotary(zb[:, sl], cos2, sin2).astype(BF16)
        xk = zb[:, RET_KW + hh * RET_KEY_DIM:RET_KW + (hh + 1) * RET_KEY_DIM]
        kb_ref[:, sl] = (_rotary(xk, cos2, sin2) * (RET_KEY_DIM ** -0.5)).astype(BF16)
    vb_ref[...] = zb[:, 2 * RET_KW:].astype(BF16)


def _inproj_sample(x, g, wqkv, wf, bf, wb, cos2, sin2):
    M = x.shape[0]
    args = (x, g, wqkv, wf, bf, wb, cos2, sin2)
    out_shape = (
        jax.ShapeDtypeStruct((M, FOX_W), BF16),
        jax.ShapeDtypeStruct((M, FOX_W), F32),
        jax.ShapeDtypeStruct((M, FOX_W), F32),
        jax.ShapeDtypeStruct((M, FOX_HEADS), F32),
        jax.ShapeDtypeStruct((M, RET_KW), BF16),
        jax.ShapeDtypeStruct((M, RET_KW), BF16),
        jax.ShapeDtypeStruct((M, RET_VW), BF16),
    )
    return pl.pallas_call(
        _inproj_sample_kernel,
        grid=(1,),
        in_specs=[_const_spec(a.shape) for a in args],
        out_specs=tuple(_const_spec(o.shape) for o in out_shape),
        out_shape=out_shape,
        compiler_params=pltpu.CompilerParams(dimension_semantics=("arbitrary",),
                                             vmem_limit_bytes=VMEM_LIMIT),
        name="inproj_sample",
    )(*args)


def _fox_prompt_kernel(it_ref, jt_ref, jfetch_ref, mode_ref, qT_ref, ka_ref, vT_ref,
                       sqq_ref, o_ref, m_ref, acc_ref, *, n_strips):
    del jfetch_ref
    t = pl.program_id(0)
    i = it_ref[t]
    j = jt_ref[t]
    T = qT_ref.shape[2]
    SUB = T // n_strips
    EXACT = n_strips + 1

    def heads(fn):
        def body(hh, carry):
            fn(hh, mode_ref[t * FOX_HEADS + hh])
            return carry
        lax.fori_loop(0, FOX_HEADS, body, 0)

    def scores(hh):
        return _dot(ka_ref[hh], qT_ref[hh])

    def exact_head(hh):
        s = scores(hh)
        kk = lax.broadcasted_iota(jnp.int32, s.shape, 0) + j * T
        qq = lax.broadcasted_iota(jnp.int32, s.shape, 1) + i * T
        s = jnp.where(kk > qq, NEG, s)
        m_old = m_ref[hh]
        m_new = jnp.maximum(m_old, jnp.max(s, axis=0, keepdims=True))
        p = jnp.exp2(s - m_new).astype(BF16)
        alpha = jnp.exp2(m_old - m_new)
        acc_ref[hh] = alpha * acc_ref[hh] + _dot(vT_ref[hh], p)
        m_ref[hh] = m_new


    def diag_head(hh, mode):
        m_ref[hh] = sqq_ref[pl.ds(hh, 1), :]

        @pl.when(mode != EXACT)
        def _():
            half = T // 2
            m = m_ref[hh]
            kk = lax.broadcasted_iota(jnp.int32, (half, half), 0)
            qq = lax.broadcasted_iota(jnp.int32, (half, half), 1)
            tri = kk > qq
            s_lo = _dot(ka_ref[hh, :half, :], qT_ref[hh])
            s_lo = jnp.concatenate([jnp.where(tri, NEG, s_lo[:, :half]), s_lo[:, half:]], axis=1)
            acc_ref[hh] = _dot(vT_ref[hh, :, :half], jnp.exp2(s_lo - m).astype(BF16))
            s_hi = jnp.where(tri, NEG, _dot(ka_ref[hh, half:, :], qT_ref[hh, :, half:]))
            p_hi = jnp.exp2(s_hi - m[:, half:]).astype(BF16)
            acc_ref[hh, :, half:] += _dot(vT_ref[hh, :, half:], p_hi)

        @pl.when(mode == EXACT)
        def _():
            acc_ref[hh] = jnp.zeros((V_PAD, T), F32)
            exact_head(hh)

    def off_head(hh, mode):
        for nn in range(1, n_strips + 1):
            k0 = (n_strips - nn) * SUB

            @pl.when(mode == nn)
            def _():
                p = jnp.exp2(_dot(ka_ref[hh, k0:, :], qT_ref[hh]) - m_ref[hh]).astype(BF16)
                acc_ref[hh] += _dot(vT_ref[hh, :, k0:], p)

        @pl.when(mode == EXACT)
        def _():
            exact_head(hh)

    @pl.when(j == i)
    def _():
        heads(diag_head)

    def off_group(gi, carry):
        h0 = gi * HEAD_GROUP
        all_full = mode_ref[t * FOX_HEADS + h0] == n_strips
        for d in range(1, HEAD_GROUP):
            all_full = jnp.logical_and(all_full, mode_ref[t * FOX_HEADS + h0 + d] == n_strips)

        @pl.when(all_full)
        def _():
            for d in range(HEAD_GROUP):
                hh = h0 + d
                p = jnp.exp2(scores(hh) - m_ref[hh]).astype(BF16)
                acc_ref[hh] += _dot(vT_ref[hh], p)

        @pl.when(jnp.logical_not(all_full))
        def _():
            def body(hh, c):
                off_head(hh, mode_ref[t * FOX_HEADS + hh])
                return c
            lax.fori_loop(h0, h0 + HEAD_GROUP, body, 0)
        return carry

    @pl.when(j < i)
    def _():
        lax.fori_loop(0, FOX_HEADS // HEAD_GROUP, off_group, 0)

    @pl.when(j == 0)
    def _():
        for pr in range(FOX_HEADS // 2):
            halves = []
            for hh in (2 * pr, 2 * pr + 1):
                a = acc_ref[hh]
                halves.append(a[:FOX_HEAD_DIM] / a[FOX_HEAD_DIM:FOX_HEAD_DIM + 1])
            o_ref[pr] = jnp.concatenate(halves, axis=0).T.astype(BF16)


def _prune_tables(stats, it, jt, nb):
    per = stats.shape[0] // nb
    st = stats.reshape(nb, per, 8, LANES)[:, :, :, :FOX_HEADS]
    nq = jnp.sqrt(jnp.max(st[:, :, 0, :], axis=1)) * NORM_SLACK
    nk_strip = jnp.sqrt(st[:, :, 1, :]) * NORM_SLACK
    nk = jnp.max(nk_strip, axis=1)
    c_first = st[:, 0, 2, :]
    c_last = st[:, :, 3, :]
    bound = (nq[it][:, None, :] * (nk_strip[jt] + nk[it][:, None, :])
             - (c_last[jt] - c_first[it][:, None, :]))
    live = jnp.logical_or(jnp.asarray(jt == it)[:, None, None],
                          jnp.logical_not(bound < PRUNE_LOG2))
    strip_no = jnp.arange(per, dtype=jnp.int32)[None, :, None]
    n_keep = per - jnp.min(jnp.where(live, strip_no, per), axis=1)
    safe = nq[it] * (nk[jt] + nk[it]) < STALE_SAFE_LOG2
    mode = jnp.where(n_keep == 0, 0, jnp.where(safe, n_keep, per + 1))
    steps = jnp.arange(len(it), dtype=jnp.int32)
    last_live = lax.cummax(jnp.where(jnp.any(n_keep > 0, axis=1), steps, 0))
    return jnp.asarray(jt)[last_live], mode.astype(jnp.int32).reshape(-1), per


def _fox_prompt(qT, ka, vT, stats, sqq, T):
    S = ka.shape[1]
    nb = S // T
    it = np.array([i for i in range(nb) for _ in range(i + 1)], np.int32)
    jt = np.array([j for i in range(nb) for j in range(i, -1, -1)], np.int32)
    jfetch, mode, n_strips = _prune_tables(stats, it, jt, nb)
    grid_spec = pltpu.PrefetchScalarGridSpec(
        num_scalar_prefetch=4,
        grid=(len(it),),
        in_specs=[
            pl.BlockSpec((FOX_HEADS, HEAD_PAD, T), lambda t, it, jt, jf, md: (0, 0, it[t])),
            pl.BlockSpec((FOX_HEADS, T, HEAD_PAD), lambda t, it, jt, jf, md: (0, jf[t], 0)),
            pl.BlockSpec((FOX_HEADS, V_PAD, T), lambda t, it, jt, jf, md: (0, 0, jf[t])),
            pl.BlockSpec((FOX_HEADS, T), lambda t, it, jt, jf, md: (0, it[t])),
        ],
        out_specs=pl.BlockSpec((FOX_HEADS // 2, T, LANES),
                               lambda t, it, jt, jf, md: (0, it[t], 0)),
        scratch_shapes=[pltpu.VMEM((FOX_HEADS, 1, T), F32),
                        pltpu.VMEM((FOX_HEADS, V_PAD, T), F32)],
    )
    return pl.pallas_call(
        functools.partial(_fox_prompt_kernel, n_strips=n_strips),
        grid_spec=grid_spec,
        out_shape=jax.ShapeDtypeStruct((FOX_HEADS // 2, S, LANES), BF16),
        compiler_params=pltpu.CompilerParams(dimension_semantics=("arbitrary",),
                                             vmem_limit_bytes=VMEM_LIMIT),
        name="fox_prompt",
    )(jnp.asarray(it), jnp.asarray(jt), jfetch, mode, qT, ka, vT, sqq)


def _fox_sample_kernel(q_ref, kn_ref, vn_ref, ckT_ref, cvT_ref, lfT_ref, up_ref, ex_ref, o_ref):
    P = ckT_ref.shape[3]
    Tn = q_ref.shape[0]
    KP = lfT_ref.shape[2]
    HQ = FOX_HEADS * Tn
    nchunk = KP // LANES
    nt = (((1,), (1,)), ((), ()))

    def stack3(x):
        parts3 = [t.astype(F32) for t in _split3(x)] + [jnp.zeros_like(x)]
        return jnp.concatenate(parts3, axis=0).astype(BF16)

    x3 = stack3(lfT_ref[0])
    up = up_ref[...]
    parts = [_dot(x3[:, cidx * LANES:(cidx + 1) * LANES], up) for cidx in range(nchunk)]
    run = jnp.zeros((4 * FOX_HEADS, 1), F32)
    for cidx in range(nchunk):
        total = parts[cidx][:, LANES - 1:LANES]
        parts[cidx] = parts[cidx] + run
        run = run + total
    y = jnp.concatenate(parts, axis=1)
    cT = (y[:FOX_HEADS] + y[FOX_HEADS:2 * FOX_HEADS] + y[2 * FOX_HEADS:3 * FOX_HEADS]) * LOG2E
    ckx = _dot(ex_ref[...], stack3(cT))

    tail = ckx[:, P:P + LANES]
    rowq = lax.broadcasted_iota(jnp.int32, tail.shape, 0) % Tn
    lanek = lax.broadcasted_iota(jnp.int32, tail.shape, 1)
    cq = jnp.sum(jnp.where(lanek == rowq, tail, 0.0), axis=1, keepdims=True)

    q = q_ref[...]
    qt = jnp.concatenate([q] * FOX_HEADS, axis=0)
    rh = lax.broadcasted_iota(jnp.int32, qt.shape, 0) // Tn
    lh = lax.broadcasted_iota(jnp.int32, qt.shape, 1) // FOX_HEAD_DIM
    qbd = jnp.where(rh == lh, qt, jnp.zeros_like(qt))

    kT = ckT_ref[0].reshape(FOX_W, P).astype(BF16)
    vT = cvT_ref[0].reshape(FOX_W, P).astype(BF16)
    s_c = _dot(qbd, kT) + cq - ckx[:, :P]
    s_n = lax.dot_general(qbd, kn_ref[...].astype(BF16), nt, preferred_element_type=F32)
    s_n = s_n + cq - ckx[:, P:P + Tn]
    key = lax.broadcasted_iota(jnp.int32, s_n.shape, 1)
    qrow = lax.broadcasted_iota(jnp.int32, s_n.shape, 0) % Tn
    s_n = jnp.where(key > qrow, NEG, s_n)
    m = jnp.maximum(jnp.max(s_c, axis=1, keepdims=True), jnp.max(s_n, axis=1, keepdims=True))
    p_c = jnp.exp2(s_c - m)
    p_n = jnp.exp2(s_n - m)
    l = jnp.sum(p_c, axis=1, keepdims=True) + jnp.sum(p_n, axis=1, keepdims=True)
    z = lax.dot_general(p_c.astype(BF16), vT, nt, preferred_element_type=F32)
    z = (z + _dot(p_n.astype(BF16), vn_ref[...].astype(BF16))) / l
    zh = lax.broadcasted_iota(jnp.int32, (Tn, FOX_W), 1) // FOX_HEAD_DIM
    o = jnp.zeros((Tn, FOX_W), F32)
    for hh in range(FOX_HEADS):
        o = o + jnp.where(zh == hh, z[hh * Tn:(hh + 1) * Tn, :], 0.0)
    o_ref[...] = o.astype(BF16)


def _fox_sample(q, kn, vn, cache_kT, cache_vT, lfT, B, Tn):
    P = cache_kT.shape[3]
    KP = lfT.shape[2]
    HQ = FOX_HEADS * Tn
    up = jnp.asarray(np.triu(np.ones((LANES, LANES), np.float32)), BF16)
    ex = np.zeros((HQ, 4 * FOX_HEADS), np.float32)
    for part in range(3):
        for hh in range(FOX_HEADS):
            ex[hh * Tn:(hh + 1) * Tn, part * FOX_HEADS + hh] = 1.0
    ex = jnp.asarray(ex, BF16)
    rowb = lambda w: pl.BlockSpec((Tn, w), lambda b: (b, 0))
    return pl.pallas_call(
        _fox_sample_kernel,
        grid=(B,),
        in_specs=[rowb(FOX_W), rowb(FOX_W), rowb(FOX_W),
                  pl.BlockSpec((1, FOX_HEADS, FOX_HEAD_DIM, P), lambda b: (b, 0, 0, 0)),
                  pl.BlockSpec((1, FOX_HEADS, FOX_HEAD_DIM, P), lambda b: (b, 0, 0, 0)),
                  pl.BlockSpec((1, FOX_HEADS, KP), lambda b: (b, 0, 0)),
                  _const_spec(up.shape), _const_spec(ex.shape)],
        out_specs=rowb(FOX_W),
        out_shape=jax.ShapeDtypeStruct((B * Tn, FOX_W), BF16),
        compiler_params=pltpu.CompilerParams(dimension_semantics=("arbitrary",),
                                             vmem_limit_bytes=VMEM_LIMIT),
        name="fox_sample",
    )(q, kn, vn, cache_kT, cache_vT, lfT, up, ex)


def _retention_kernel(q_ref, k_ref, v_ref, s0_ref, dmat_ref, xi_ref, zeta_ref, gam_ref,
                      n_ref, sout_ref, st_ref):
    c = pl.program_id(1)
    G = st_ref.shape[0]
    C = dmat_ref.shape[1]
    n_units = q_ref.shape[0] // C
    per_stream = n_units // G

    @pl.when(c == 0)
    def _():
        st_ref[...] = s0_ref[...]

    for unit in range(n_units):
        g = unit // per_stream
        rows = slice(unit * C, (unit + 1) * C)
        for hh in range(RET_HEADS):
            q = q_ref[rows, hh * RET_KEY_DIM:(hh + 1) * RET_KEY_DIM]
            k = k_ref[rows, hh * RET_KEY_DIM:(hh + 1) * RET_KEY_DIM]
            v = v_ref[rows, hh * RET_VAL_DIM:(hh + 1) * RET_VAL_DIM]
            st = st_ref[g, hh]
            sc = lax.dot_general(q, k, (((1,), (1,)), ((), ())), preferred_element_type=F32)
            sc = sc * dmat_ref[hh]
            o = _dot(sc.astype(BF16), v) + _dot(q, st.astype(BF16)) * xi_ref[hh]
            kz = (k.astype(F32) * zeta_ref[hh]).astype(BF16)
            upd = lax.dot_general(kz, v, (((0,), (0,)), ((), ())), preferred_element_type=F32)
            st_ref[g, hh] = gam_ref[hh] * st + upd
            mu = jnp.mean(o, axis=-1, keepdims=True)
            d = o - mu
            var = jnp.mean(d * d, axis=-1, keepdims=True)
            n_ref[rows, hh * RET_VAL_DIM:(hh + 1) * RET_VAL_DIM] = (
                d * lax.rsqrt(var + EPS)).astype(BF16)

    @pl.when(c == pl.num_programs(1) - 1)
    def _():
        sout_ref[...] = st_ref[...]


def _retention(q, k, v, state0, B, L, C, G=1, per_step=1):
    assert L % (C * per_step) == 0
    nc = L // (C * per_step)
    assert B % G == 0 and (G == 1 or (nc == 1 and per_step == 1))
    f32 = np.float32
    lg = np.log(f32(1.0) - np.exp2(f32(-5.0) - np.arange(RET_HEADS, dtype=f32))).astype(f32)
    idx = np.arange(C, dtype=f32)
    diff = idx[:, None] - idx[None, :]
    dmat = np.where(diff[None] >= 0, np.exp(np.maximum(diff, 0)[None] * lg[:, None, None]), 0)
    xi = np.exp((idx[None, :] + f32(1.0)) * lg[:, None])
    zeta = np.exp((f32(C) - f32(1.0) - idx[None, :]) * lg[:, None])
    xi = np.broadcast_to(xi[:, :, None], (RET_HEADS, C, RET_VAL_DIM))
    zeta = np.broadcast_to(zeta[:, :, None], (RET_HEADS, C, RET_KEY_DIM))
    gam = np.broadcast_to(np.exp(f32(C) * lg)[:, None, None], (RET_HEADS, 1, RET_VAL_DIM))
    dmat, xi, zeta, gam = (jnp.asarray(a, F32) for a in (dmat, xi, zeta, gam))
    rowc = lambda w: pl.BlockSpec((G * per_step * C, w), lambda b, c: (b * nc + c, 0))
    st_spec = pl.BlockSpec((G, RET_HEADS, RET_KEY_DIM, RET_VAL_DIM), lambda b, c: (b, 0, 0, 0))
    return pl.pallas_call(
        _retention_kernel,
        grid=(B // G, nc),
        in_specs=[rowc(RET_KW), rowc(RET_KW), rowc(RET_VW), st_spec,
                  _const_spec(dmat.shape), _const_spec(xi.shape), _const_spec(zeta.shape),
                  _const_spec(gam.shape)],
        out_specs=(rowc(RET_VW), st_spec),
        out_shape=(jax.ShapeDtypeStruct((B * L, RET_VW), BF16),
                   jax.ShapeDtypeStruct((B, RET_HEADS, RET_KEY_DIM, RET_VAL_DIM), F32)),
        scratch_shapes=[pltpu.VMEM((G, RET_HEADS, RET_KEY_DIM, RET_VAL_DIM), F32)],
        compiler_params=pltpu.CompilerParams(dimension_semantics=("arbitrary", "arbitrary"),
                                             vmem_limit_bytes=VMEM_LIMIT),
        name="retention",
    )(q, k, v, state0, dmat, xi, zeta, gam)


def _mixer_ffn_kernel(x_ref, oa_ref, nb_ref, prev_ref, gmix_ref, wg_ref, gng_ref, wpa_ref, wpb_ref,
                      wo_ref, gffn_ref, wup_ref, cw_ref, cb_ref, wdn_ref, gfin_ref,
                      y_ref, conv_ref, carry_ref, ua_ref, ub_ref, acc_ref, h2_ref,
                      *, nseg, seglen):
    i = pl.program_id(0)
    NC = N_FFN_CHUNKS
    PADR = 8
    H0 = PADR - (CONV_WIDTH - 1)

    @pl.when(i == 0)
    def _():
        carry_ref[...] = prev_ref[...]

    x = x_ref[...]
    h = _rmsnorm(x, gmix_ref[...]).astype(BF16)
    zg = _dot_t(h, wg_ref[...])
    gb = zg[:, :RET_VW]
    gma = zg[:, RET_VW:RET_VW + D_MODEL]
    gmb = zg[:, RET_VW + D_MODEL:]
    oa = jnp.concatenate([oa_ref[p] for p in range(FOX_HEADS // 2)], axis=1)
    ya = _dot(oa, wpa_ref[...])
    nn = nb_ref[...].astype(F32) * gng_ref[...] * (gb * jax.nn.sigmoid(gb))
    yb = _dot(nn.astype(BF16), wpb_ref[...])
    y = jax.nn.sigmoid(gma) * ya + jax.nn.sigmoid(gmb) * yb
    x1 = x + _dot(y.astype(BF16), wo_ref[...])
    h2_ref[...] = _rmsnorm(x1, gffn_ref[...]).astype(BF16)
    acc_ref[...] = x1

    h2 = h2_ref[...]

    def cols(cidx):
        return slice(cidx * FFN_CHUNK, (cidx + 1) * FFN_CHUNK)

    def up_half(u_ref, cidx, slot):
        u = _dot(h2, wup_ref[:, cols(cidx)])
        for s in range(nseg):
            u_ref[slot, s, PADR:PADR + seglen, :] = u[s * seglen:(s + 1) * seglen, :]
            u_ref[slot, s, H0:PADR, :] = carry_ref[s, H0:PADR, cols(cidx)]
            carry_ref[s, H0:PADR, cols(cidx)] = (
                u[(s + 1) * seglen - (CONV_WIDTH - 1):(s + 1) * seglen, :])

    def conv_half(u_ref, cidx, slot):
        w = cw_ref[:, cols(cidx)]
        b = cb_ref[:, cols(cidx)]
        outs = []
        for s in range(nseg):
            acc = w[0:1] * u_ref[slot, s, H0:H0 + seglen, :]
            for jj in range(1, CONV_WIDTH):
                acc = acc + w[jj:jj + 1] * u_ref[slot, s, H0 + jj:H0 + jj + seglen, :]
            outs.append(b + acc)
        return outs[0] if nseg == 1 else jnp.concatenate(outs, axis=0)

    def stage_up(c):
        up_half(ua_ref, c, c % UP_AHEAD_SLOTS)
        up_half(ub_ref, NC + c, c % UP_AHEAD_SLOTS)

    for c in range(UP_AHEAD_SLOTS - 1):
        stage_up(c)
    group = []
    for c in range(NC):
        if c + UP_AHEAD_SLOTS - 1 < NC:
            stage_up(c + UP_AHEAD_SLOTS - 1)
        a = conv_half(ua_ref, c, c % UP_AHEAD_SLOTS)
        b = conv_half(ub_ref, NC + c, c % UP_AHEAD_SLOTS)
        group.append((_gelu_tanh(a) * b).astype(BF16))
        if len(group) == DOWN_GROUP or c + 1 == NC:
            r0 = (c + 1 - len(group)) * FFN_CHUNK
            wd = wdn_ref[r0:(c + 1) * FFN_CHUNK, :]
            acc_ref[...] += _dot(jnp.concatenate(group, axis=1), wd)
            group = []
    y_ref[...] = _rmsnorm(acc_ref[...], gfin_ref[...])

    @pl.when(i == pl.num_programs(0) - 1)
    def _():
        conv_ref[...] = carry_ref[:, H0:PADR, :]


def _mixer_ffn(x, oa, nb, prev, weights, tm, nseg, seglen):
    M = x.shape[0]
    gmix, wg, gng, wpa, wpb, wo, gffn, wup, cw, cb, wdn, gfin = weights
    FC = FFN_CHUNK
    NC = N_FFN_CHUNKS
    row = lambda w: pl.BlockSpec((tm, w), lambda i: (i, 0))
    wspec = lambda a: pl.BlockSpec(a.shape, lambda i, n=a.ndim: (0,) * n,
                                   pipeline_mode=pl.Buffered(1))
    in_specs = [row(D_MODEL),
                pl.BlockSpec((FOX_HEADS // 2, tm, LANES), lambda i: (0, i, 0)),
                row(RET_VW), wspec(prev)] + [wspec(w) for w in weights]
    out_shape = (jax.ShapeDtypeStruct((M, D_MODEL), F32),
                 jax.ShapeDtypeStruct((nseg, CONV_WIDTH - 1, 2 * FFN_DIM), F32))
    out_specs = (row(D_MODEL), _const_spec(out_shape[1].shape))
    return pl.pallas_call(
        functools.partial(_mixer_ffn_kernel, nseg=nseg, seglen=seglen),
        grid=(M // tm,),
        in_specs=in_specs,
        out_specs=out_specs,
        out_shape=out_shape,
        scratch_shapes=[pltpu.VMEM((nseg, 8, 2 * FFN_DIM), F32),
                        pltpu.VMEM((UP_AHEAD_SLOTS, nseg, 8 + seglen, FC), F32),
                        pltpu.VMEM((UP_AHEAD_SLOTS, nseg, 8 + seglen, FC), F32),
                        pltpu.VMEM((tm, D_MODEL), F32),
                        pltpu.VMEM((tm, D_MODEL), BF16)],
        compiler_params=pltpu.CompilerParams(dimension_semantics=("arbitrary",),
                                             vmem_limit_bytes=VMEM_LIMIT),
        name="mixer_ffn",
    )(x, oa, nb, prev, *weights)


def _rotary_tables(start, n):
    half = RET_KEY_DIM // 2
    inv = 1.0 / (ROPE_BASE ** jnp.linspace(0.0, 1.0, half, dtype=F32))
    fine = min(n, ROT_FINE)
    assert n % fine == 0
    a_hi = (start + fine * jnp.arange(n // fine)).astype(F32)[:, None] * inv[None, :]
    a_lo = jnp.arange(fine).astype(F32)[:, None] * inv[None, :]
    ch, sh = jnp.cos(a_hi)[:, None, :], jnp.sin(a_hi)[:, None, :]
    cl, sl = jnp.cos(a_lo)[None, :, :], jnp.sin(a_lo)[None, :, :]
    cos = (ch * cl - sh * sl).reshape(n, half)
    sin = (sh * cl + ch * sl).reshape(n, half)
    return jnp.concatenate([cos, cos], axis=1), jnp.concatenate([-sin, sin], axis=1)


def _pad_heads(wt, pad):
    d = wt.shape[1]
    wt = wt.reshape(FOX_HEADS, FOX_HEAD_DIM, d)
    wt = jnp.pad(wt, ((0, 0), (0, pad - FOX_HEAD_DIM), (0, 0)))
    return wt.reshape(FOX_HEADS * pad, d)


def _prompt_consts(tm):
    tri = np.tril(np.ones((tm, tm), np.float32))
    eq = np.zeros((LANES, FOX_HEADS * HEAD_PAD), np.float32)
    ek = np.zeros((LANES, FOX_HEADS * HEAD_PAD), np.float32)
    oneq = np.zeros((1, FOX_HEADS * HEAD_PAD), np.float32)
    onek = np.zeros((1, FOX_HEADS * HEAD_PAD), np.float32)
    onev = np.zeros((1, FOX_HEADS * V_PAD), np.float32)
    for hh in range(FOX_HEADS):
        base = hh * HEAD_PAD + BIAS_COL
        for part in range(3):
            eq[part * FOX_HEADS + hh, base + part] = 1.0
            ek[part * FOX_HEADS + hh, base + 3 + part] = -1.0
            onek[0, base + part] = 1.0
            oneq[0, base + 3 + part] = 1.0
        onev[0, hh * V_PAD + FOX_HEAD_DIM] = 1.0
    return (jnp.asarray(tri, BF16), jnp.asarray(eq, BF16), jnp.asarray(ek, BF16),
            jnp.asarray(oneq), jnp.asarray(onek), jnp.asarray(onev))


def _tile(n, pref):
    t = min(n, pref)
    while n % t:
        t //= 2
    return t


def kernel(x_prompt, x_sample, cache_fox_k, cache_fox_v, cache_fox_logf, state_ret, state_ffn_conv,
           norm_mix_g, w_in, b_fox_f, gn_ret_g, w_pa, w_pb, w_o, norm_ffn_g, w_up, conv_w, conv_b,
           w_down, norm_final_g):
    depth = w_in.shape[0]
    Bp, S, _ = x_prompt.shape
    Bs, Ts, _ = x_sample.shape
    P = cache_fox_k.shape[2]
    assert depth == 1 and Bp == 1, "kernel handles the single-layer, single-prompt configuration"
    l = 0

    wt = jnp.swapaxes(w_in[l], 0, 1).astype(BF16)
    o0 = 3 * FOX_W
    o1 = o0 + FOX_HEADS
    o2 = o1 + 2 * RET_KW + RET_VW
    wq_aug = _pad_heads(wt[:FOX_W], HEAD_PAD)
    wk_aug = _pad_heads(wt[FOX_W:2 * FOX_W], HEAD_PAD)
    wf = jnp.pad(wt[o0:o1], ((0, LANES - FOX_HEADS), (0, 0)))
    wvf = jnp.concatenate([_pad_heads(wt[2 * FOX_W:o0], V_PAD), wf], axis=0)
    wqkv = wt[:o0]
    bf = jnp.pad(b_fox_f[l].astype(F32), (0, LANES - FOX_HEADS))[None, :]
    wb = wt[o1:o2]
    wg = wt[o2:]
    gmix = norm_mix_g[l].astype(F32)[None, :]
    mix_weights = (
        gmix, wg, gn_ret_g[l].astype(F32)[None, :], w_pa[l].astype(BF16), w_pb[l].astype(BF16),
        w_o[l].astype(BF16), norm_ffn_g[l].astype(F32)[None, :],
        w_up[l].astype(BF16),
        jnp.pad(conv_w[l].astype(F32), ((0, 8 - CONV_WIDTH), (0, 0))),
        conv_b[l].astype(F32)[None, :],
        w_down[l].astype(BF16),
        norm_final_g.astype(F32)[None, :],
    )
    hist_pad = ((0, 0), (8 - (CONV_WIDTH - 1), 0), (0, 0))

    tm_a = _tile(S, 512)
    cos_p, sin_p = _rotary_tables(0, S)
    (qT, ka, vT, kT_p, vT_p, logfT_p, qb, kb, vb, stats, sqq) = _inproj_prompt(
        x_prompt[0], gmix, wq_aug, wk_aug, wvf, bf, wb, cos_p, sin_p,
        _prompt_consts(tm_a), tm_a)
    oa_p = _fox_prompt(qT, ka, vT, stats, sqq, _tile(S, 1024))
    zero_state = jnp.zeros((1, RET_HEADS, RET_KEY_DIM, RET_VAL_DIM), F32)
    c_ret = _tile(S, 256)
    nb_p, ret_p = _retention(qb, kb, vb, zero_state, 1, S, c_ret,
                             per_step=math.gcd(S // c_ret, 4))
    tm_d = _tile(S, 256)
    zero_prev = jnp.zeros((1, 8, 2 * FFN_DIM), F32)
    y_p, conv_p = _mixer_ffn(x_prompt[0], oa_p, nb_p, zero_prev, mix_weights, tm_d, 1, tm_d)

    Ms = Bs * Ts
    cos_s, sin_s = _rotary_tables(P, Ts)
    cos_s = jnp.tile(cos_s, (Bs, 1))
    sin_s = jnp.tile(sin_s, (Bs, 1))
    (q_s, k_s, v_s, logf_s, qb_s, kb_s, vb_s) = _inproj_sample(
        x_sample.reshape(Ms, D_MODEL), gmix, wqkv, wf, bf, wb, cos_s, sin_s)
    KP = ((P + Ts + LANES - 1) // LANES) * LANES
    lf_all = jnp.concatenate([cache_fox_logf[l].astype(F32), logf_s.reshape(Bs, Ts, FOX_HEADS)], axis=1)
    lfT = jnp.pad(jnp.swapaxes(lf_all, 1, 2), ((0, 0), (0, 0), (0, KP - P - Ts)))
    oa_s = _fox_sample(q_s, k_s, v_s, jnp.transpose(cache_fox_k[l], (0, 2, 3, 1)),
                       jnp.transpose(cache_fox_v[l], (0, 2, 3, 1)), lfT, Bs, Ts)
    oa_s = jnp.moveaxis(oa_s.reshape(Ms, FOX_HEADS // 2, LANES), 1, 0)
    nb_s, ret_s = _retention(qb_s, kb_s, vb_s, state_ret[l].astype(F32), Bs, Ts, Ts,
                             G=math.gcd(Bs, 4))
    prev_s = jnp.pad(state_ffn_conv[l].astype(F32), hist_pad)
    y_s, conv_s = _mixer_ffn(x_sample.reshape(Ms, D_MODEL), oa_s, nb_s, prev_s, mix_weights,
                             Ms, Bs, Ts)

    hshape = (FOX_HEADS, FOX_HEAD_DIM)
    return (
        y_p[None],
        y_s.reshape(Bs, Ts, D_MODEL),
        jnp.transpose(kT_p, (2, 0, 1))[None, None],
        jnp.transpose(vT_p, (2, 0, 1))[None, None],
        jnp.transpose(logfT_p, (1, 0))[None, None],
        ret_p[None],
        conv_p[None],
        k_s.reshape((1, Bs, Ts) + hshape),
        v_s.reshape((1, Bs, Ts) + hshape),
        logf_s.reshape(1, Bs, Ts, FOX_HEADS),
        ret_s[None],
        conv_s[None],
    )
```

```python
import functools
import math

import numpy as np
import jax
import jax.numpy as jnp
from jax import lax
from jax.experimental import pallas as pl
from jax.experimental.pallas import tpu as pltpu

F32 = jnp.float32
BF16 = jnp.bfloat16

D_MODEL = 1024
FOX_HEADS = 8
FOX_HEAD_DIM = 64
RET_HEADS = 4
RET_KEY_DIM = 128
RET_VAL_DIM = 256
FFN_DIM = 2816
CONV_WIDTH = 3
EPS = 1e-6
ROPE_BASE = 10000.0

FOX_W = FOX_HEADS * FOX_HEAD_DIM
RET_KW = RET_HEADS * RET_KEY_DIM
RET_VW = RET_HEADS * RET_VAL_DIM

LOG2E = 1.4426950408889634
LANES = 128
HEAD_PAD = LANES
V_PAD = 80
BIAS_COL = FOX_HEAD_DIM
NEG = -1e30
STALE_SAFE_LOG2 = 64.0
PRUNE_LOG2 = -160.0
NORM_SLACK = 1.02
HEAD_GROUP = 2
ROT_FINE = 128
FFN_CHUNK = 256
N_FFN_CHUNKS = FFN_DIM // FFN_CHUNK
DOWN_GROUP = 4
UP_AHEAD_SLOTS = 3
VMEM_LIMIT = 56 * 1024 * 1024


def _rmsnorm(x, g):
    ms = jnp.mean(x * x, axis=-1, keepdims=True)
    return x * lax.rsqrt(ms + EPS) * g


def _split3(x):
    hi = x.astype(BF16)
    r1 = x - hi.astype(F32)
    mid = r1.astype(BF16)
    lo = (r1 - mid.astype(F32)).astype(BF16)
    return hi, mid, lo


def _log_sigmoid(x):
    return jnp.minimum(x, 0.0) - jnp.log1p(jnp.exp(-jnp.abs(x)))


def _dot(a, b):
    return jnp.dot(a, b, preferred_element_type=F32)


def _dot_t(a, bt):
    return lax.dot_general(a, bt, (((1,), (1,)), ((), ())), preferred_element_type=F32)


def _gelu_tanh(x):
    c0 = math.sqrt(2.0 / math.pi)
    hx = 0.5 * x
    return hx + hx * jnp.tanh(x * (c0 + (c0 * 0.044715) * (x * x)))


def _rotary(x, cos2, sin2):
    return x * cos2 + pltpu.roll(x, RET_KEY_DIM // 2, 1) * sin2


def _const_spec(shape):
    n = len(shape)
    return pl.BlockSpec(shape, lambda *_: (0,) * n)


def _inproj_prompt_kernel(x_ref, g_ref, wq_ref, wk_ref, wvf_ref, bf_ref, wb_ref,
                          cos_ref, sin_ref, tri_ref, eq_ref, ek_ref, oneq_ref, onek_ref, onev_ref,
                          qT_ref, ka_ref, vT_ref, kT32_ref, vT32_ref, logf_ref, qb_ref, kb_ref, vb_ref,
                          stats_ref, sqq_ref, carry_ref):
    tm = x_ref.shape[0]
    VW = FOX_HEADS * V_PAD

    @pl.when(pl.program_id(0) == 0)
    def _():
        carry_ref[...] = jnp.zeros_like(carry_ref)

    h = _rmsnorm(x_ref[...], g_ref[...]).astype(BF16)
    zvf = _dot_t(h, wvf_ref[...])

    logf = _log_sigmoid(zvf[:, VW:] + bf_ref[...])
    logf_ref[...] = logf.T[:FOX_HEADS, :]
    lane = lax.broadcasted_iota(jnp.int32, logf.shape, 1)
    logf = jnp.where(lane < FOX_HEADS, logf, 0.0)
    r = _dot(tri_ref[...], jnp.concatenate(_split3(logf), axis=1))
    c = r[:, :LANES] + r[:, LANES:2 * LANES] + r[:, 2 * LANES:] + carry_ref[...]
    carry_ref[...] = c[tm - 1:tm, :]
    c2 = c * LOG2E
    hi, mid, lo = (t.astype(F32) for t in _split3(c2))
    c3 = (hi + pltpu.roll(mid, FOX_HEADS, 1) + pltpu.roll(lo, 2 * FOX_HEADS, 1)).astype(BF16)

    q_aug = (_dot_t(h, wq_ref[...]) * (FOX_HEAD_DIM ** -0.5 * LOG2E)
             + _dot(c3, eq_ref[...]) + oneq_ref[...])
    qT32 = q_aug.T
    k_aug = _dot_t(h, wk_ref[...]) + _dot(c3, ek_ref[...]) + onek_ref[...]
    kT32 = k_aug.T
    vT32 = (zvf[:, :VW] + onev_ref[...]).T
    qT = qT32.astype(BF16)
    k_aug = k_aug.astype(BF16)
    vT = vT32.astype(BF16)

    lane1 = lax.broadcasted_iota(jnp.int32, (1, LANES), 1)
    nq2 = jnp.zeros((1, LANES), F32)
    nk2 = jnp.zeros((1, LANES), F32)
    for hh in range(FOX_HEADS):
        qT_ref[hh] = qT[hh * HEAD_PAD:(hh + 1) * HEAD_PAD, :]
        ka_ref[hh] = k_aug[:, hh * HEAD_PAD:(hh + 1) * HEAD_PAD]
        vT_ref[hh] = vT[hh * V_PAD:(hh + 1) * V_PAD, :]
        qh = qT32[hh * HEAD_PAD:hh * HEAD_PAD + FOX_HEAD_DIM, :]
        kh = kT32[hh * HEAD_PAD:hh * HEAD_PAD + FOX_HEAD_DIM, :]
        kT32_ref[hh] = kh
        vT32_ref[hh] = vT32[hh * V_PAD:hh * V_PAD + FOX_HEAD_DIM, :]
        sqq_ref[hh:hh + 1, :] = jnp.sum(qh * kh, axis=0, keepdims=True)
        q2 = jnp.max(jnp.sum(qh * qh, axis=0, keepdims=True), axis=1, keepdims=True)
        k2 = jnp.max(jnp.sum(kh * kh, axis=0, keepdims=True), axis=1, keepdims=True)
        nq2 = jnp.where(lane1 == hh, q2, nq2)
        nk2 = jnp.where(lane1 == hh, k2, nk2)
    stats_ref[0] = jnp.concatenate(
        [nq2, nk2, c2[0:1, :], c2[tm - 1:tm, :], jnp.zeros((4, LANES), F32)], axis=0)

    zb = _dot_t(h, wb_ref[...])
    cos2 = cos_ref[...]
    sin2 = sin_ref[...]
    for hh in range(RET_HEADS):
        sl = slice(hh * RET_KEY_DIM, (hh + 1) * RET_KEY_DIM)
        qb_ref[:, sl] = _rotary(zb[:, sl], cos2, sin2).astype(BF16)
        xk = zb[:, RET_KW + hh * RET_KEY_DIM:RET_KW + (hh + 1) * RET_KEY_DIM]
        kb_ref[:, sl] = (_rotary(xk, cos2, sin2) * (RET_KEY_DIM ** -0.5)).astype(BF16)
    vb_ref[...] = zb[:, 2 * RET_KW:].astype(BF16)


def _inproj_prompt(x, g, wq, wk, wvf, bf, wb, cos2, sin2, consts, tm):
    S = x.shape[0]
    tri, eq, ek, oneq, onek, onev = consts
    row = lambda w: pl.BlockSpec((tm, w), lambda i: (i, 0))
    headT = pl.BlockSpec((FOX_HEADS, FOX_HEAD_DIM, tm), lambda i: (0, 0, i))
    in_specs = [row(D_MODEL), _const_spec(g.shape), _const_spec(wq.shape), _const_spec(wk.shape),
                _const_spec(wvf.shape),
                _const_spec(bf.shape), _const_spec(wb.shape), row(LANES), row(LANES),
                _const_spec(tri.shape), _const_spec(eq.shape), _const_spec(ek.shape),
                _const_spec(oneq.shape), _const_spec(onek.shape), _const_spec(onev.shape)]
    out_shape = (
        jax.ShapeDtypeStruct((FOX_HEADS, HEAD_PAD, S), BF16),
        jax.ShapeDtypeStruct((FOX_HEADS, S, HEAD_PAD), BF16),
        jax.ShapeDtypeStruct((FOX_HEADS, V_PAD, S), BF16),
        jax.ShapeDtypeStruct((FOX_HEADS, FOX_HEAD_DIM, S), F32),
        jax.ShapeDtypeStruct((FOX_HEADS, FOX_HEAD_DIM, S), F32),
        jax.ShapeDtypeStruct((FOX_HEADS, S), F32),
        jax.ShapeDtypeStruct((S, RET_KW), BF16),
        jax.ShapeDtypeStruct((S, RET_KW), BF16),
        jax.ShapeDtypeStruct((S, RET_VW), BF16),
        jax.ShapeDtypeStruct((S // tm, 8, LANES), F32),
        jax.ShapeDtypeStruct((FOX_HEADS, S), F32),
    )
    out_specs = (
        pl.BlockSpec((FOX_HEADS, HEAD_PAD, tm), lambda i: (0, 0, i)),
        pl.BlockSpec((FOX_HEADS, tm, HEAD_PAD), lambda i: (0, i, 0)),
        pl.BlockSpec((FOX_HEADS, V_PAD, tm), lambda i: (0, 0, i)),
        headT, headT, pl.BlockSpec((FOX_HEADS, tm), lambda i: (0, i)),
        row(RET_KW), row(RET_KW), row(RET_VW),
        pl.BlockSpec((1, 8, LANES), lambda i: (i, 0, 0)),
        pl.BlockSpec((FOX_HEADS, tm), lambda i: (0, i)),
    )
    return pl.pallas_call(
        _inproj_prompt_kernel,
        grid=(S // tm,),
        in_specs=in_specs,
        out_specs=out_specs,
        out_shape=out_shape,
        scratch_shapes=[pltpu.VMEM((1, LANES), F32)],
        compiler_params=pltpu.CompilerParams(dimension_semantics=("arbitrary",),
                                             vmem_limit_bytes=VMEM_LIMIT),
        name="inproj_prompt",
    )(x, g, wq, wk, wvf, bf, wb, cos2, sin2, tri, eq, ek, oneq, onek, onev)


def _inproj_sample_kernel(x_ref, g_ref, wqkv_ref, wf_ref, bf_ref, wb_ref, cos_ref, sin_ref,
                          q_ref, k32_ref, v32_ref, logf_ref, qb_ref, kb_ref, vb_ref):
    h = _rmsnorm(x_ref[...], g_ref[...]).astype(BF16)
    logf = _log_sigmoid(_dot_t(h, wf_ref[...]) + bf_ref[...])
    logf_ref[...] = logf[:, :FOX_HEADS]
    z = _dot_t(h, wqkv_ref[...])
    q_ref[...] = (z[:, :FOX_W] * (FOX_HEAD_DIM ** -0.5 * LOG2E)).astype(BF16)
    k32_ref[...] = z[:, FOX_W:2 * FOX_W]
    v32_ref[...] = z[:, 2 * FOX_W:]
    zb = _dot_t(h, wb_ref[...])
    cos2 = cos_ref[...]
    sin2 = sin_ref[...]
    for hh in range(RET_HEADS):
        sl = slice(hh * RET_KEY_DIM, (hh + 1) * RET_KEY_DIM)
        qb_ref[:, sl] = _rotary(zb[:, sl], cos2, sin2).astype(BF16)
        xk = zb[:, RET_KW + hh * RET_KEY_DIM:RET_KW + (hh + 1) * RET_KEY_DIM]
        kb_ref[:, sl] = (_rotary(xk, cos2, sin2) * (RET_KEY_DIM ** -0.5)).astype(BF16)
    vb_ref[...] = zb[:, 2 * RET_KW:].astype(BF16)


def _inproj_sample(x, g, wqkv, wf, bf, wb, cos2, sin2):
    M = x.shape[0]
    args = (x, g, wqkv, wf, bf, wb, cos2, sin2)
    out_shape = (
        jax.ShapeDtypeStruct((M, FOX_W), BF16),
        jax.ShapeDtypeStruct((M, FOX_W), F32),
        jax.ShapeDtypeStruct((M, FOX_W), F32),
        jax.ShapeDtypeStruct((M, FOX_HEADS), F32),
        jax.ShapeDtypeStruct((M, RET_KW), BF16),
        jax.ShapeDtypeStruct((M, RET_KW), BF16),
        jax.ShapeDtypeStruct((M, RET_VW), BF16),
    )
    return pl.pallas_call(
        _inproj_sample_kernel,
        grid=(1,),
        in_specs=[_const_spec(a.shape) for a in args],
        out_specs=tuple(_const_spec(o.shape) for o in out_shape),
        out_shape=out_shape,
        compiler_params=pltpu.CompilerParams(dimension_semantics=("arbitrary",),
                                             vmem_limit_bytes=VMEM_LIMIT),
        name="inproj_sample",
    )(*args)


def _fox_prompt_kernel(it_ref, jt_ref, jfetch_ref, mode_ref, qT_ref, ka_ref, vT_ref,
                       sqq_ref, o_ref, m_ref, acc_ref, *, n_strips):
    del jfetch_ref
    t = pl.program_id(0)
    i = it_ref[t]
    j = jt_ref[t]
    T = qT_ref.shape[2]
    SUB = T // n_strips
    EXACT = n_strips + 1

    def scores(hh):
        return _dot(ka_ref[hh], qT_ref[hh])

    def exact_head(hh):
        s = scores(hh)
        kk = lax.broadcasted_iota(jnp.int32, s.shape, 0) + j * T
        qq = lax.broadcasted_iota(jnp.int32, s.shape, 1) + i * T
        s = jnp.where(kk > qq, NEG, s)
        m_old = m_ref[hh]
        m_new = jnp.maximum(m_old, jnp.max(s, axis=0, keepdims=True))
        p = jnp.exp2(s - m_new).astype(BF16)
        alpha = jnp.exp2(m_old - m_new)
        acc_ref[hh] = alpha * acc_ref[hh] + _dot(vT_ref[hh], p)
        m_ref[hh] = m_new


    def diag_pass(hh):
        half = T // 2
        m = sqq_ref[pl.ds(hh, 1), :]
        m_ref[hh] = m
        kk = lax.broadcasted_iota(jnp.int32, (half, half), 0)
        qq = lax.broadcasted_iota(jnp.int32, (half, half), 1)
        tri = kk > qq
        s_lo = _dot(ka_ref[hh, :half, :], qT_ref[hh])
        s_lo = jnp.concatenate([jnp.where(tri, NEG, s_lo[:, :half]), s_lo[:, half:]], axis=1)
        acc_ref[hh] = _dot(vT_ref[hh, :, :half], jnp.exp2(s_lo - m).astype(BF16))
        s_hi = jnp.where(tri, NEG, _dot(ka_ref[hh, half:, :], qT_ref[hh, :, half:]))
        p_hi = jnp.exp2(s_hi - m[:, half:]).astype(BF16)
        acc_ref[hh, :, half:] += _dot(vT_ref[hh, :, half:], p_hi)

    def diag_head(hh, mode):
        @pl.when(mode != EXACT)
        def _():
            diag_pass(hh)

        @pl.when(mode == EXACT)
        def _():
            m_ref[hh] = sqq_ref[pl.ds(hh, 1), :]
            acc_ref[hh] = jnp.zeros((V_PAD, T), F32)
            exact_head(hh)

    def full_pass(hh):
        p = jnp.exp2(scores(hh) - m_ref[hh]).astype(BF16)
        acc_ref[hh] += _dot(vT_ref[hh], p)

    def off_head(hh, mode):
        for nn in range(1, n_strips + 1):
            k0 = (n_strips - nn) * SUB

            @pl.when(mode == nn)
            def _():
                p = jnp.exp2(_dot(ka_ref[hh, k0:, :], qT_ref[hh]) - m_ref[hh]).astype(BF16)
                acc_ref[hh] += _dot(vT_ref[hh, :, k0:], p)

        @pl.when(mode == EXACT)
        def _():
            exact_head(hh)

    def head_groups(is_common, common_pass, per_head):
        def group(gi, carry):
            h0 = gi * HEAD_GROUP
            all_common = is_common(mode_ref[t * FOX_HEADS + h0])
            for d in range(1, HEAD_GROUP):
                all_common = jnp.logical_and(all_common,
                                             is_common(mode_ref[t * FOX_HEADS + h0 + d]))

            @pl.when(all_common)
            def _():
                for d in range(HEAD_GROUP):
                    common_pass(h0 + d)

            @pl.when(jnp.logical_not(all_common))
            def _():
                def body(hh, c):
                    per_head(hh, mode_ref[t * FOX_HEADS + hh])
                    return c
                lax.fori_loop(h0, h0 + HEAD_GROUP, body, 0)
            return carry
        lax.fori_loop(0, FOX_HEADS // HEAD_GROUP, group, 0)

    @pl.when(j == i)
    def _():
        head_groups(lambda md: md != EXACT, diag_pass, diag_head)

    @pl.when(j < i)
    def _():
        head_groups(lambda md: md == n_strips, full_pass, off_head)

    @pl.when(j == 0)
    def _():
        for pr in range(FOX_HEADS // 2):
            halves = []
            for hh in (2 * pr, 2 * pr + 1):
                a = acc_ref[hh]
                halves.append(a[:FOX_HEAD_DIM] / a[FOX_HEAD_DIM:FOX_HEAD_DIM + 1])
            o_ref[pr] = jnp.concatenate(halves, axis=0).T.astype(BF16)


def _prune_tables(stats, it, jt, nb):
    per = stats.shape[0] // nb
    st = stats.reshape(nb, per, 8, LANES)[:, :, :, :FOX_HEADS]
    nq = jnp.sqrt(jnp.max(st[:, :, 0, :], axis=1)) * NORM_SLACK
    nk_strip = jnp.sqrt(st[:, :, 1, :]) * NORM_SLACK
    nk = jnp.max(nk_strip, axis=1)
    c_first = st[:, 0, 2, :]
    c_last = st[:, :, 3, :]
    bound = (nq[it][:, None, :] * (nk_strip[jt] + nk[it][:, None, :])
             - (c_last[jt] - c_first[it][:, None, :]))
    live = jnp.logical_or(jnp.asarray(jt == it)[:, None, None],
                          jnp.logical_not(bound < PRUNE_LOG2))
    strip_no = jnp.arange(per, dtype=jnp.int32)[None, :, None]
    n_keep = per - jnp.min(jnp.where(live, strip_no, per), axis=1)
    safe = nq[it] * (nk[jt] + nk[it]) < STALE_SAFE_LOG2
    mode = jnp.where(n_keep == 0, 0, jnp.where(safe, n_keep, per + 1))
    steps = jnp.arange(len(it), dtype=jnp.int32)
    last_live = lax.cummax(jnp.where(jnp.any(n_keep > 0, axis=1), steps, 0))
    return jnp.asarray(jt)[last_live], mode.astype(jnp.int32).reshape(-1), per


def _fox_prompt(qT, ka, vT, stats, sqq, T):
    S = ka.shape[1]
    nb = S // T
    it = np.array([i for i in range(nb) for _ in range(i + 1)], np.int32)
    jt = np.array([j for i in range(nb) for j in range(i, -1, -1)], np.int32)
    jfetch, mode, n_strips = _prune_tables(stats, it, jt, nb)
    grid_spec = pltpu.PrefetchScalarGridSpec(
        num_scalar_prefetch=4,
        grid=(len(it),),
        in_specs=[
            pl.BlockSpec((FOX_HEADS, HEAD_PAD, T), lambda t, it, jt, jf, md: (0, 0, it[t])),
            pl.BlockSpec((FOX_HEADS, T, HEAD_PAD), lambda t, it, jt, jf, md: (0, jf[t], 0)),
            pl.BlockSpec((FOX_HEADS, V_PAD, T), lambda t, it, jt, jf, md: (0, 0, jf[t])),
            pl.BlockSpec((FOX_HEADS, T), lambda t, it, jt, jf, md: (0, it[t])),
        ],
        out_specs=pl.BlockSpec((FOX_HEADS // 2, T, LANES),
                               lambda t, it, jt, jf, md: (0, it[t], 0)),
        scratch_shapes=[pltpu.VMEM((FOX_HEADS, 1, T), F32),
                        pltpu.VMEM((FOX_HEADS, V_PAD, T), F32)],
    )
    return pl.pallas_call(
        functools.partial(_fox_prompt_kernel, n_strips=n_strips),
        grid_spec=grid_spec,
        out_shape=jax.ShapeDtypeStruct((FOX_HEADS // 2, S, LANES), BF16),
        compiler_params=pltpu.CompilerParams(dimension_semantics=("arbitrary",),
                                             vmem_limit_bytes=VMEM_LIMIT),
        name="fox_prompt",
    )(jnp.asarray(it), jnp.asarray(jt), jfetch, mode, qT, ka, vT, sqq)


def _fox_sample_kernel(q_ref, kn_ref, vn_ref, ckT_ref, cvT_ref, lfT_ref, up_ref, ex_ref, o_ref):
    P = ckT_ref.shape[3]
    Tn = q_ref.shape[0]
    KP = lfT_ref.shape[2]
    HQ = FOX_HEADS * Tn
    nchunk = KP // LANES
    nt = (((1,), (1,)), ((), ()))

    def stack3(x):
        parts3 = [t.astype(F32) for t in _split3(x)] + [jnp.zeros_like(x)]
        return jnp.concatenate(parts3, axis=0).astype(BF16)

    x3 = stack3(lfT_ref[0])
    up = up_ref[...]
    parts = [_dot(x3[:, cidx * LANES:(cidx + 1) * LANES], up) for cidx in range(nchunk)]
    run = jnp.zeros((4 * FOX_HEADS, 1), F32)
    for cidx in range(nchunk):
        total = parts[cidx][:, LANES - 1:LANES]
        parts[cidx] = parts[cidx] + run
        run = run + total
    y = jnp.concatenate(parts, axis=1)
    cT = (y[:FOX_HEADS] + y[FOX_HEADS:2 * FOX_HEADS] + y[2 * FOX_HEADS:3 * FOX_HEADS]) * LOG2E
    ckx = _dot(ex_ref[...], stack3(cT))

    tail = ckx[:, P:P + LANES]
    rowq = lax.broadcasted_iota(jnp.int32, tail.shape, 0) % Tn
    lanek = lax.broadcasted_iota(jnp.int32, tail.shape, 1)
    cq = jnp.sum(jnp.where(lanek == rowq, tail, 0.0), axis=1, keepdims=True)

    q = q_ref[...]
    qt = jnp.concatenate([q] * FOX_HEADS, axis=0)
    rh = lax.broadcasted_iota(jnp.int32, qt.shape, 0) // Tn
    lh = lax.broadcasted_iota(jnp.int32, qt.shape, 1) // FOX_HEAD_DIM
    qbd = jnp.where(rh == lh, qt, jnp.zeros_like(qt))

    kT = ckT_ref[0].reshape(FOX_W, P).astype(BF16)
    vT = cvT_ref[0].reshape(FOX_W, P).astype(BF16)
    s_c = _dot(qbd, kT) + cq - ckx[:, :P]
    s_n = lax.dot_general(qbd, kn_ref[...].astype(BF16), nt, preferred_element_type=F32)
    s_n = s_n + cq - ckx[:, P:P + Tn]
    key = lax.broadcasted_iota(jnp.int32, s_n.shape, 1)
    qrow = lax.broadcasted_iota(jnp.int32, s_n.shape, 0) % Tn
    s_n = jnp.where(key > qrow, NEG, s_n)
    m = jnp.maximum(jnp.max(s_c, axis=1, keepdims=True), jnp.max(s_n, axis=1, keepdims=True))
    p_c = jnp.exp2(s_c - m)
    p_n = jnp.exp2(s_n - m)
    l = jnp.sum(p_c, axis=1, keepdims=True) + jnp.sum(p_n, axis=1, keepdims=True)
    z = lax.dot_general(p_c.astype(BF16), vT, nt, preferred_element_type=F32)
    z = (z + _dot(p_n.astype(BF16), vn_ref[...].astype(BF16))) / l
    zh = lax.broadcasted_iota(jnp.int32, (Tn, FOX_W), 1) // FOX_HEAD_DIM
    o = jnp.zeros((Tn, FOX_W), F32)
    for hh in range(FOX_HEADS):
        o = o + jnp.where(zh == hh, z[hh * Tn:(hh + 1) * Tn, :], 0.0)
    o_ref[...] = o.astype(BF16)


def _fox_sample(q, kn, vn, cache_kT, cache_vT, lfT, B, Tn):
    P = cache_kT.shape[3]
    KP = lfT.shape[2]
    HQ = FOX_HEADS * Tn
    up = jnp.asarray(np.triu(np.ones((LANES, LANES), np.float32)), BF16)
    ex = np.zeros((HQ, 4 * FOX_HEADS), np.float32)
    for part in range(3):
        for hh in range(FOX_HEADS):
            ex[hh * Tn:(hh + 1) * Tn, part * FOX_HEADS + hh] = 1.0
    ex = jnp.asarray(ex, BF16)
    rowb = lambda w: pl.BlockSpec((Tn, w), lambda b: (b, 0))
    return pl.pallas_call(
        _fox_sample_kernel,
        grid=(B,),
        in_specs=[rowb(FOX_W), rowb(FOX_W), rowb(FOX_W),
                  pl.BlockSpec((1, FOX_HEADS, FOX_HEAD_DIM, P), lambda b: (b, 0, 0, 0)),
                  pl.BlockSpec((1, FOX_HEADS, FOX_HEAD_DIM, P), lambda b: (b, 0, 0, 0)),
                  pl.BlockSpec((1, FOX_HEADS, KP), lambda b: (b, 0, 0)),
                  _const_spec(up.shape), _const_spec(ex.shape)],
        out_specs=rowb(FOX_W),
        out_shape=jax.ShapeDtypeStruct((B * Tn, FOX_W), BF16),
        compiler_params=pltpu.CompilerParams(dimension_semantics=("arbitrary",),
                                             vmem_limit_bytes=VMEM_LIMIT),
        name="fox_sample",
    )(q, kn, vn, cache_kT, cache_vT, lfT, up, ex)


def _retention_kernel(q_ref, k_ref, v_ref, s0_ref, dmat_ref, xi_ref, zeta_ref, gam_ref,
                      n_ref, sout_ref, st_ref):
    c = pl.program_id(1)
    G = st_ref.shape[0]
    C = dmat_ref.shape[1]
    n_units = q_ref.shape[0] // C
    per_stream = n_units // G

    @pl.when(c == 0)
    def _():
        st_ref[...] = s0_ref[...]

    for unit in range(n_units):
        g = unit // per_stream
        rows = slice(unit * C, (unit + 1) * C)
        for hh in range(RET_HEADS):
            q = q_ref[rows, hh * RET_KEY_DIM:(hh + 1) * RET_KEY_DIM]
            k = k_ref[rows, hh * RET_KEY_DIM:(hh + 1) * RET_KEY_DIM]
            v = v_ref[rows, hh * RET_VAL_DIM:(hh + 1) * RET_VAL_DIM]
            st = st_ref[g, hh]
            sc = lax.dot_general(q, k, (((1,), (1,)), ((), ())), preferred_element_type=F32)
            sc = sc * dmat_ref[hh]
            o = _dot(sc.astype(BF16), v) + _dot(q, st.astype(BF16)) * xi_ref[hh]
            kz = (k.astype(F32) * zeta_ref[hh]).astype(BF16)
            upd = lax.dot_general(kz, v, (((0,), (0,)), ((), ())), preferred_element_type=F32)
            st_ref[g, hh] = gam_ref[hh] * st + upd
            mu = jnp.mean(o, axis=-1, keepdims=True)
            d = o - mu
            var = jnp.mean(d * d, axis=-1, keepdims=True)
            n_ref[rows, hh * RET_VAL_DIM:(hh + 1) * RET_VAL_DIM] = (
                d * lax.rsqrt(var + EPS)).astype(BF16)

    @pl.when(c == pl.num_programs(1) - 1)
    def _():
        sout_ref[...] = st_ref[...]


def _retention(q, k, v, state0, B, L, C, G=1, per_step=1):
    assert L % (C * per_step) == 0
    nc = L // (C * per_step)
    assert B % G == 0 and (G == 1 or (nc == 1 and per_step == 1))
    f32 = np.float32
    lg = np.log(f32(1.0) - np.exp2(f32(-5.0) - np.arange(RET_HEADS, dtype=f32))).astype(f32)
    idx = np.arange(C, dtype=f32)
    diff = idx[:, None] - idx[None, :]
    dmat = np.where(diff[None] >= 0, np.exp(np.maximum(diff, 0)[None] * lg[:, None, None]), 0)
    xi = np.exp((idx[None, :] + f32(1.0)) * lg[:, None])
    zeta = np.exp((f32(C) - f32(1.0) - idx[None, :]) * lg[:, None])
    xi = np.broadcast_to(xi[:, :, None], (RET_HEADS, C, RET_VAL_DIM))
    zeta = np.broadcast_to(zeta[:, :, None], (RET_HEADS, C, RET_KEY_DIM))
    gam = np.broadcast_to(np.exp(f32(C) * lg)[:, None, None], (RET_HEADS, 1, RET_VAL_DIM))
    dmat, xi, zeta, gam = (jnp.asarray(a, F32) for a in (dmat, xi, zeta, gam))
    rowc = lambda w: pl.BlockSpec((G * per_step * C, w), lambda b, c: (b * nc + c, 0))
    st_spec = pl.BlockSpec((G, RET_HEADS, RET_KEY_DIM, RET_VAL_DIM), lambda b, c: (b, 0, 0, 0))
    return pl.pallas_call(
        _retention_kernel,
        grid=(B // G, nc),
        in_specs=[rowc(RET_KW), rowc(RET_KW), rowc(RET_VW), st_spec,
                  _const_spec(dmat.shape), _const_spec(xi.shape), _const_spec(zeta.shape),
                  _const_spec(gam.shape)],
        out_specs=(rowc(RET_VW), st_spec),
        out_shape=(jax.ShapeDtypeStruct((B * L, RET_VW), BF16),
                   jax.ShapeDtypeStruct((B, RET_HEADS, RET_KEY_DIM, RET_VAL_DIM), F32)),
        scratch_shapes=[pltpu.VMEM((G, RET_HEADS, RET_KEY_DIM, RET_VAL_DIM), F32)],
        compiler_params=pltpu.CompilerParams(dimension_semantics=("arbitrary", "arbitrary"),
                                             vmem_limit_bytes=VMEM_LIMIT),
        name="retention",
    )(q, k, v, state0, dmat, xi, zeta, gam)


def _mixer_ffn_kernel(x_ref, oa_ref, nb_ref, prev_ref, gmix_ref, wg_ref, gng_ref, wpa_ref, wpb_ref,
                      wo_ref, gffn_ref, wup_ref, cw_ref, cb_ref, wdn_ref, gfin_ref,
                      y_ref, conv_ref, carry_ref, ua_ref, ub_ref, acc_ref, h2_ref,
                      *, nseg, seglen):
    i = pl.program_id(0)
    NC = N_FFN_CHUNKS
    PADR = 8
    H0 = PADR - (CONV_WIDTH - 1)

    @pl.when(i == 0)
    def _():
        carry_ref[...] = prev_ref[...]

    x = x_ref[...]
    h = _rmsnorm(x, gmix_ref[...]).astype(BF16)
    zg = _dot_t(h, wg_ref[...])
    gb = zg[:, :RET_VW]
    gma = zg[:, RET_VW:RET_VW + D_MODEL]
    gmb = zg[:, RET_VW + D_MODEL:]
    oa = jnp.concatenate([oa_ref[p] for p in range(FOX_HEADS // 2)], axis=1)
    ya = _dot(oa, wpa_ref[...])
    nn = nb_ref[...].astype(F32) * gng_ref[...] * (gb * jax.nn.sigmoid(gb))
    yb = _dot(nn.astype(BF16), wpb_ref[...])
    y = jax.nn.sigmoid(gma) * ya + jax.nn.sigmoid(gmb) * yb
    x1 = x + _dot(y.astype(BF16), wo_ref[...])
    h2_ref[...] = _rmsnorm(x1, gffn_ref[...]).astype(BF16)
    acc_ref[...] = x1

    h2 = h2_ref[...]

    def cols(cidx):
        return slice(cidx * FFN_CHUNK, (cidx + 1) * FFN_CHUNK)

    def up_half(u_ref, cidx, slot):
        u = _dot(h2, wup_ref[:, cols(cidx)])
        for s in range(nseg):
            u_ref[slot, s, PADR:PADR + seglen, :] = u[s * seglen:(s + 1) * seglen, :]
            u_ref[slot, s, H0:PADR, :] = carry_ref[s, H0:PADR, cols(cidx)]
            carry_ref[s, H0:PADR, cols(cidx)] = (
                u[(s + 1) * seglen - (CONV_WIDTH - 1):(s + 1) * seglen, :])

    def conv_half(u_ref, cidx, slot):
        w = cw_ref[:, cols(cidx)]
        b = cb_ref[:, cols(cidx)]
        outs = []
        for s in range(nseg):
            acc = w[0:1] * u_ref[slot, s, H0:H0 + seglen, :]
            for jj in range(1, CONV_WIDTH):
                acc = acc + w[jj:jj + 1] * u_ref[slot, s, H0 + jj:H0 + jj + seglen, :]
            outs.append(b + acc)
        return outs[0] if nseg == 1 else jnp.concatenate(outs, axis=0)

    def stage_up(c):
        up_half(ua_ref, c, c % UP_AHEAD_SLOTS)
        up_half(ub_ref, NC + c, c % UP_AHEAD_SLOTS)

    for c in range(UP_AHEAD_SLOTS - 1):
        stage_up(c)
    group = []
    for c in range(NC):
        if c + UP_AHEAD_SLOTS - 1 < NC:
            stage_up(c + UP_AHEAD_SLOTS - 1)
        a = conv_half(ua_ref, c, c % UP_AHEAD_SLOTS)
        b = conv_half(ub_ref, NC + c, c % UP_AHEAD_SLOTS)
        group.append((_gelu_tanh(a) * b).astype(BF16))
        if len(group) == DOWN_GROUP or c + 1 == NC:
            r0 = (c + 1 - len(group)) * FFN_CHUNK
            wd = wdn_ref[r0:(c + 1) * FFN_CHUNK, :]
            acc_ref[...] += _dot(jnp.concatenate(group, axis=1), wd)
            group = []
    y_ref[...] = _rmsnorm(acc_ref[...], gfin_ref[...])

    @pl.when(i == pl.num_programs(0) - 1)
    def _():
        conv_ref[...] = carry_ref[:, H0:PADR, :]


def _mixer_ffn(x, oa, nb, prev, weights, tm, nseg, seglen):
    M = x.shape[0]
    gmix, wg, gng, wpa, wpb, wo, gffn, wup, cw, cb, wdn, gfin = weights
    FC = FFN_CHUNK
    NC = N_FFN_CHUNKS
    row = lambda w: pl.BlockSpec((tm, w), lambda i: (i, 0))
    wspec = lambda a: pl.BlockSpec(a.shape, lambda i, n=a.ndim: (0,) * n,
                                   pipeline_mode=pl.Buffered(1))
    in_specs = [row(D_MODEL),
                pl.BlockSpec((FOX_HEADS // 2, tm, LANES), lambda i: (0, i, 0)),
                row(RET_VW), wspec(prev)] + [wspec(w) for w in weights]
    out_shape = (jax.ShapeDtypeStruct((M, D_MODEL), F32),
                 jax.ShapeDtypeStruct((nseg, CONV_WIDTH - 1, 2 * FFN_DIM), F32))
    out_specs = (row(D_MODEL), _const_spec(out_shape[1].shape))
    return pl.pallas_call(
        functools.partial(_mixer_ffn_kernel, nseg=nseg, seglen=seglen),
        grid=(M // tm,),
        in_specs=in_specs,
        out_specs=out_specs,
        out_shape=out_shape,
        scratch_shapes=[pltpu.VMEM((nseg, 8, 2 * FFN_DIM), F32),
                        pltpu.VMEM((UP_AHEAD_SLOTS, nseg, 8 + seglen, FC), F32),
                        pltpu.VMEM((UP_AHEAD_SLOTS, nseg, 8 + seglen, FC), F32),
                        pltpu.VMEM((tm, D_MODEL), F32),
                        pltpu.VMEM((tm, D_MODEL), BF16)],
        compiler_params=pltpu.CompilerParams(dimension_semantics=("arbitrary",),
                                             vmem_limit_bytes=VMEM_LIMIT),
        name="mixer_ffn",
    )(x, oa, nb, prev, *weights)


def _rotary_tables(start, n):
    half = RET_KEY_DIM // 2
    inv = 1.0 / (ROPE_BASE ** jnp.linspace(0.0, 1.0, half, dtype=F32))
    fine = min(n, ROT_FINE)
    assert n % fine == 0
    a_hi = (start + fine * jnp.arange(n // fine)).astype(F32)[:, None] * inv[None, :]
    a_lo = jnp.arange(fine).astype(F32)[:, None] * inv[None, :]
    ch, sh = jnp.cos(a_hi)[:, None, :], jnp.sin(a_hi)[:, None, :]
    cl, sl = jnp.cos(a_lo)[None, :, :], jnp.sin(a_lo)[None, :, :]
    cos = (ch * cl - sh * sl).reshape(n, half)
    sin = (sh * cl + ch * sl).reshape(n, half)
    return jnp.concatenate([cos, cos], axis=1), jnp.concatenate([-sin, sin], axis=1)


def _pad_heads(wt, pad):
    d = wt.shape[1]
    wt = wt.reshape(FOX_HEADS, FOX_HEAD_DIM, d)
    wt = jnp.pad(wt, ((0, 0), (0, pad - FOX_HEAD_DIM), (0, 0)))
    return wt.reshape(FOX_HEADS * pad, d)


def _prompt_consts(tm):
    tri = np.tril(np.ones((tm, tm), np.float32))
    eq = np.zeros((LANES, FOX_HEADS * HEAD_PAD), np.float32)
    ek = np.zeros((LANES, FOX_HEADS * HEAD_PAD), np.float32)
    oneq = np.zeros((1, FOX_HEADS * HEAD_PAD), np.float32)
    onek = np.zeros((1, FOX_HEADS * HEAD_PAD), np.float32)
    onev = np.zeros((1, FOX_HEADS * V_PAD), np.float32)
    for hh in range(FOX_HEADS):
        base = hh * HEAD_PAD + BIAS_COL
        for part in range(3):
            eq[part * FOX_HEADS + hh, base + part] = 1.0
            ek[part * FOX_HEADS + hh, base + 3 + part] = -1.0
            onek[0, base + part] = 1.0
            oneq[0, base + 3 + part] = 1.0
        onev[0, hh * V_PAD + FOX_HEAD_DIM] = 1.0
    return (jnp.asarray(tri, BF16), jnp.asarray(eq, BF16), jnp.asarray(ek, BF16),
            jnp.asarray(oneq), jnp.asarray(onek), jnp.asarray(onev))


def _tile(n, pref):
    t = min(n, pref)
    while n % t:
        t //= 2
    return t


def kernel(x_prompt, x_sample, cache_fox_k, cache_fox_v, cache_fox_logf, state_ret, state_ffn_conv,
           norm_mix_g, w_in, b_fox_f, gn_ret_g, w_pa, w_pb, w_o, norm_ffn_g, w_up, conv_w, conv_b,
           w_down, norm_final_g):
    depth = w_in.shape[0]
    Bp, S, _ = x_prompt.shape
    Bs, Ts, _ = x_sample.shape
    P = cache_fox_k.shape[2]
    assert depth == 1 and Bp == 1, "kernel handles the single-layer, single-prompt configuration"
    l = 0

    wt = jnp.swapaxes(w_in[l], 0, 1).astype(BF16)
    o0 = 3 * FOX_W
    o1 = o0 + FOX_HEADS
    o2 = o1 + 2 * RET_KW + RET_VW
    wq_aug = _pad_heads(wt[:FOX_W], HEAD_PAD)
    wk_aug = _pad_heads(wt[FOX_W:2 * FOX_W], HEAD_PAD)
    wf = jnp.pad(wt[o0:o1], ((0, LANES - FOX_HEADS), (0, 0)))
    wvf = jnp.concatenate([_pad_heads(wt[2 * FOX_W:o0], V_PAD), wf], axis=0)
    wqkv = wt[:o0]
    bf = jnp.pad(b_fox_f[l].astype(F32), (0, LANES - FOX_HEADS))[None, :]
    wb = wt[o1:o2]
    wg = wt[o2:]
    gmix = norm_mix_g[l].astype(F32)[None, :]
    mix_weights = (
        gmix, wg, gn_ret_g[l].astype(F32)[None, :], w_pa[l].astype(BF16), w_pb[l].astype(BF16),
        w_o[l].astype(BF16), norm_ffn_g[l].astype(F32)[None, :],
        w_up[l].astype(BF16),
        jnp.pad(conv_w[l].astype(F32), ((0, 8 - CONV_WIDTH), (0, 0))),
        conv_b[l].astype(F32)[None, :],
        w_down[l].astype(BF16),
        norm_final_g.astype(F32)[None, :],
    )
    hist_pad = ((0, 0), (8 - (CONV_WIDTH - 1), 0), (0, 0))

    tm_a = _tile(S, 512)
    cos_p, sin_p = _rotary_tables(0, S)
    (qT, ka, vT, kT_p, vT_p, logfT_p, qb, kb, vb, stats, sqq) = _inproj_prompt(
        x_prompt[0], gmix, wq_aug, wk_aug, wvf, bf, wb, cos_p, sin_p,
        _prompt_consts(tm_a), tm_a)
    oa_p = _fox_prompt(qT, ka, vT, stats, sqq, _tile(S, 1024))
    zero_state = jnp.zeros((1, RET_HEADS, RET_KEY_DIM, RET_VAL_DIM), F32)
    c_ret = _tile(S, 256)
    nb_p, ret_p = _retention(qb, kb, vb, zero_state, 1, S, c_ret,
                             per_step=math.gcd(S // c_ret, 4))
    tm_d = _tile(S, 256)
    zero_prev = jnp.zeros((1, 8, 2 * FFN_DIM), F32)
    y_p, conv_p = _mixer_ffn(x_prompt[0], oa_p, nb_p, zero_prev, mix_weights, tm_d, 1, tm_d)

    Ms = Bs * Ts
    cos_s, sin_s = _rotary_tables(P, Ts)
    cos_s = jnp.tile(cos_s, (Bs, 1))
    sin_s = jnp.tile(sin_s, (Bs, 1))
    (q_s, k_s, v_s, logf_s, qb_s, kb_s, vb_s) = _inproj_sample(
        x_sample.reshape(Ms, D_MODEL), gmix, wqkv, wf, bf, wb, cos_s, sin_s)
    KP = ((P + Ts + LANES - 1) // LANES) * LANES
    lf_all = jnp.concatenate([cache_fox_logf[l].astype(F32), logf_s.reshape(Bs, Ts, FOX_HEADS)], axis=1)
    lfT = jnp.pad(jnp.swapaxes(lf_all, 1, 2), ((0, 0), (0, 0), (0, KP - P - Ts)))
    oa_s = _fox_sample(q_s, k_s, v_s, jnp.transpose(cache_fox_k[l], (0, 2, 3, 1)),
                       jnp.transpose(cache_fox_v[l], (0, 2, 3, 1)), lfT, Bs, Ts)
    oa_s = jnp.moveaxis(oa_s.reshape(Ms, FOX_HEADS // 2, LANES), 1, 0)
    nb_s, ret_s = _retention(qb_s, kb_s, vb_s, state_ret[l].astype(F32), Bs, Ts, Ts,
                             G=math.gcd(Bs, 4))
    prev_s = jnp.pad(state_ffn_conv[l].astype(F32), hist_pad)
    y_s, conv_s = _mixer_ffn(x_sample.reshape(Ms, D_MODEL), oa_s, nb_s, prev_s, mix_weights,
                             Ms, Bs, Ts)

    hshape = (FOX_HEADS, FOX_HEAD_DIM)
    return (
        y_p[None],
        y_s.reshape(Bs, Ts, D_MODEL),
        jnp.transpose(kT_p, (2, 0, 1))[None, None],
        jnp.transpose(vT_p, (2, 0, 1))[None, None],
        jnp.transpose(logfT_p, (1, 0))[None, None],
        ret_p[None],
        conv_p[None],
        k_s.reshape((1, Bs, Ts) + hshape),
        v_s.reshape((1, Bs, Ts) + hshape),
        logf_s.reshape(1, Bs, Ts, FOX_HEADS),
        ret_s[None],
        conv_s[None],
    )
```

```python
import functools
import math

import numpy as np
import jax
import jax.numpy as jnp
from jax import lax
from jax.experimental import pallas as pl
from jax.experimental.pallas import tpu as pltpu

F32 = jnp.float32
BF16 = jnp.bfloat16

D_MODEL = 1024
FOX_HEADS = 8
FOX_HEAD_DIM = 64
RET_HEADS = 4
RET_KEY_DIM = 128
RET_VAL_DIM = 256
FFN_DIM = 2816
CONV_WIDTH = 3
EPS = 1e-6
ROPE_BASE = 10000.0

FOX_W = FOX_HEADS * FOX_HEAD_DIM
RET_KW = RET_HEADS * RET_KEY_DIM
RET_VW = RET_HEADS * RET_VAL_DIM

LOG2E = 1.4426950408889634
LANES = 128
HEAD_PAD = LANES
V_PAD = 80
BIAS_COL = FOX_HEAD_DIM
NEG = -1e30
STALE_SAFE_LOG2 = 64.0
PRUNE_LOG2 = -160.0
NORM_SLACK = 1.02
HEAD_GROUP = 2
ROT_FINE = 128
FFN_CHUNK = 256
N_FFN_CHUNKS = FFN_DIM // FFN_CHUNK
DOWN_GROUP = 4
UP_AHEAD_SLOTS = 3
VMEM_LIMIT = 56 * 1024 * 1024


def _rmsnorm(x, g):
    ms = jnp.mean(x * x, axis=-1, keepdims=True)
    return x * lax.rsqrt(ms + EPS) * g


def _split3(x):
    hi = x.astype(BF16)
    r1 = x - hi.astype(F32)
    mid = r1.astype(BF16)
    lo = (r1 - mid.astype(F32)).astype(BF16)
    return hi, mid, lo


def _log_sigmoid(x):
    return jnp.minimum(x, 0.0) - jnp.log1p(jnp.exp(-jnp.abs(x)))


def _dot(a, b):
    return jnp.dot(a, b, preferred_element_type=F32)


def _dot_t(a, bt):
    return lax.dot_general(a, bt, (((1,), (1,)), ((), ())), preferred_element_type=F32)


def _gelu_tanh(x):
    c0 = math.sqrt(2.0 / math.pi)
    hx = 0.5 * x
    return hx + hx * jnp.tanh(x * (c0 + (c0 * 0.044715) * (x * x)))


def _rotary(x, cos2, sin2):
    return x * cos2 + pltpu.roll(x, RET_KEY_DIM // 2, 1) * sin2


def _const_spec(shape):
    n = len(shape)
    return pl.BlockSpec(shape, lambda *_: (0,) * n)


def _inproj_prompt_kernel(x_ref, g_ref, wq_ref, wk_ref, wvf_ref, bf_ref, wb_ref,
                          cos_ref, sin_ref, tri_ref, eq_ref, ek_ref, oneq_ref, onek_ref, onev_ref,
                          qT_ref, ka_ref, vT_ref, kT32_ref, vT32_ref, logf_ref, qb_ref, kb_ref, vb_ref,
                          stats_ref, sqq_ref, carry_ref):
    tm = x_ref.shape[0]
    VW = FOX_HEADS * V_PAD

    @pl.when(pl.program_id(0) == 0)
    def _():
        carry_ref[...] = jnp.zeros_like(carry_ref)

    h = _rmsnorm(x_ref[...], g_ref[...]).astype(BF16)
    zvf = _dot_t(h, wvf_ref[...])

    logf = _log_sigmoid(zvf[:, VW:] + bf_ref[...])
    logf_ref[...] = logf.T[:FOX_HEADS, :]
    lane = lax.broadcasted_iota(jnp.int32, logf.shape, 1)
    logf = jnp.where(lane < FOX_HEADS, logf, 0.0)
    r = _dot(tri_ref[...], jnp.concatenate(_split3(logf), axis=1))
    c = r[:, :LANES] + r[:, LANES:2 * LANES] + r[:, 2 * LANES:] + carry_ref[...]
    carry_ref[...] = c[tm - 1:tm, :]
    c2 = c * LOG2E
    hi, mid, lo = (t.astype(F32) for t in _split3(c2))
    c3 = (hi + pltpu.roll(mid, FOX_HEADS, 1) + pltpu.roll(lo, 2 * FOX_HEADS, 1)).astype(BF16)

    q_aug = (_dot_t(h, wq_ref[...]) * (FOX_HEAD_DIM ** -0.5 * LOG2E)
             + _dot(c3, eq_ref[...]) + oneq_ref[...])
    qT32 = q_aug.T
    k_aug = _dot_t(h, wk_ref[...]) + _dot(c3, ek_ref[...]) + onek_ref[...]
    kT32 = k_aug.T
    vT32 = (zvf[:, :VW] + onev_ref[...]).T
    qT = qT32.astype(BF16)
    k_aug = k_aug.astype(BF16)
    vT = vT32.astype(BF16)

    lane1 = lax.broadcasted_iota(jnp.int32, (1, LANES), 1)
    nq2 = jnp.zeros((1, LANES), F32)
    nk2 = jnp.zeros((1, LANES), F32)
    for hh in range(FOX_HEADS):
        qT_ref[hh] = qT[hh * HEAD_PAD:(hh + 1) * HEAD_PAD, :]
        ka_ref[hh] = k_aug[:, hh * HEAD_PAD:(hh + 1) * HEAD_PAD]
        vT_ref[hh] = vT[hh * V_PAD:(hh + 1) * V_PAD, :]
        qh = qT32[hh * HEAD_PAD:hh * HEAD_PAD + FOX_HEAD_DIM, :]
        kh = kT32[hh * HEAD_PAD:hh * HEAD_PAD + FOX_HEAD_DIM, :]
        kT32_ref[hh] = kh
        vT32_ref[hh] = vT32[hh * V_PAD:hh * V_PAD + FOX_HEAD_DIM, :]
        sqq_ref[hh:hh + 1, :] = jnp.sum(qh * kh, axis=0, keepdims=True)
        q2 = jnp.max(jnp.sum(qh * qh, axis=0, keepdims=True), axis=1, keepdims=True)
        k2 = jnp.max(jnp.sum(kh * kh, axis=0, keepdims=True), axis=1, keepdims=True)
        nq2 = jnp.where(lane1 == hh, q2, nq2)
        nk2 = jnp.where(lane1 == hh, k2, nk2)
    stats_ref[0] = jnp.concatenate(
        [nq2, nk2, c2[0:1, :], c2[tm - 1:tm, :], jnp.zeros((4, LANES), F32)], axis=0)

    zb = _dot_t(h, wb_ref[...])
    cos2 = cos_ref[...]
    sin2 = sin_ref[...]
    for hh in range(RET_HEADS):
        sl = slice(hh * RET_KEY_DIM, (hh + 1) * RET_KEY_DIM)
        qb_ref[:, sl] = _rotary(zb[:, sl], cos2, sin2).astype(BF16)
        xk = zb[:, RET_KW + hh * RET_KEY_DIM:RET_KW + (hh + 1) * RET_KEY_DIM]
        kb_ref[:, sl] = (_rotary(xk, cos2, sin2) * (RET_KEY_DIM ** -0.5)).astype(BF16)
    vb_ref[...] = zb[:, 2 * RET_KW:].astype(BF16)


def _inproj_prompt(x, g, wq, wk, wvf, bf, wb, cos2, sin2, consts, tm):
    S = x.shape[0]
    tri, eq, ek, oneq, onek, onev = consts
    row = lambda w: pl.BlockSpec((tm, w), lambda i: (i, 0))
    headT = pl.BlockSpec((FOX_HEADS, FOX_HEAD_DIM, tm), lambda i: (0, 0, i))
    in_specs = [row(D_MODEL), _const_spec(g.shape), _const_spec(wq.shape), _const_spec(wk.shape),
                _const_spec(wvf.shape),
                _const_spec(bf.shape), _const_spec(wb.shape), row(LANES), row(LANES),
                _const_spec(tri.shape), _const_spec(eq.shape), _const_spec(ek.shape),
                _const_spec(oneq.shape), _const_spec(onek.shape), _const_spec(onev.shape)]
    out_shape = (
        jax.ShapeDtypeStruct((FOX_HEADS, HEAD_PAD, S), BF16),
        jax.ShapeDtypeStruct((FOX_HEADS, S, HEAD_PAD), BF16),
        jax.ShapeDtypeStruct((FOX_HEADS, V_PAD, S), BF16),
        jax.ShapeDtypeStruct((FOX_HEADS, FOX_HEAD_DIM, S), F32),
        jax.ShapeDtypeStruct((FOX_HEADS, FOX_HEAD_DIM, S), F32),
        jax.ShapeDtypeStruct((FOX_HEADS, S), F32),
        jax.ShapeDtypeStruct((S, RET_KW), BF16),
        jax.ShapeDtypeStruct((S, RET_KW), BF16),
        jax.ShapeDtypeStruct((S, RET_VW), BF16),
        jax.ShapeDtypeStruct((S // tm, 8, LANES), F32),
        jax.ShapeDtypeStruct((FOX_HEADS, S), F32),
    )
    out_specs = (
        pl.BlockSpec((FOX_HEADS, HEAD_PAD, tm), lambda i: (0, 0, i)),
        pl.BlockSpec((FOX_HEADS, tm, HEAD_PAD), lambda i: (0, i, 0)),
        pl.BlockSpec((FOX_HEADS, V_PAD, tm), lambda i: (0, 0, i)),
        headT, headT, pl.BlockSpec((FOX_HEADS, tm), lambda i: (0, i)),
        row(RET_KW), row(RET_KW), row(RET_VW),
        pl.BlockSpec((1, 8, LANES), lambda i: (i, 0, 0)),
        pl.BlockSpec((FOX_HEADS, tm), lambda i: (0, i)),
    )
    return pl.pallas_call(
        _inproj_prompt_kernel,
        grid=(S // tm,),
        in_specs=in_specs,
        out_specs=out_specs,
        out_shape=out_shape,
        scratch_shapes=[pltpu.VMEM((1, LANES), F32)],
        compiler_params=pltpu.CompilerParams(dimension_semantics=("arbitrary",),
                                             vmem_limit_bytes=VMEM_LIMIT),
        name="inproj_prompt",
    )(x, g, wq, wk, wvf, bf, wb, cos2, sin2, tri, eq, ek, oneq, onek, onev)


def _inproj_sample_kernel(x_ref, g_ref, wqkv_ref, wf_ref, bf_ref, wb_ref, cos_ref, sin_ref,
                          q_ref, k32_ref, v32_ref, logf_ref, qb_ref, kb_ref, vb_ref):
    h = _rmsnorm(x_ref[...], g_ref[...]).astype(BF16)
    logf = _log_sigmoid(_dot_t(h, wf_ref[...]) + bf_ref[...])
    logf_ref[...] = logf[:, :FOX_HEADS]
    z = _dot_t(h, wqkv_ref[...])
    q_ref[...] = (z[:, :FOX_W] * (FOX_HEAD_DIM ** -0.5 * LOG2E)).astype(BF16)
    k32_ref[...] = z[:, FOX_W:2 * FOX_W]
    v32_ref[...] = z[:, 2 * FOX_W:]
    zb = _dot_t(h, wb_ref[...])
    cos2 = cos_ref[...]
    sin2 = sin_ref[...]
    for hh in range(RET_HEADS):
        sl = slice(hh * RET_KEY_DIM, (hh + 1) * RET_KEY_DIM)
        qb_ref[:, sl] = _rotary(zb[:, sl], cos2, sin2).astype(BF16)
        xk = zb[:, RET_KW + hh * RET_KEY_DIM:RET_KW + (hh + 1) * RET_KEY_DIM]
        kb_ref[:, sl] = (_rotary(xk, cos2, sin2) * (RET_KEY_DIM ** -0.5)).astype(BF16)
    vb_ref[...] = zb[:, 2 * RET_KW:].astype(BF16)


def _inproj_sample(x, g, wqkv, wf, bf, wb, cos2, sin2):
    M = x.shape[0]
    args = (x, g, wqkv, wf, bf, wb, cos2, sin2)
    out_shape = (
        jax.ShapeDtypeStruct((M, FOX_W), BF16),
        jax.ShapeDtypeStruct((M, FOX_W), F32),
        jax.ShapeDtypeStruct((M, FOX_W), F32),
        jax.ShapeDtypeStruct((M, FOX_HEADS), F32),
        jax.ShapeDtypeStruct((M, RET_KW), BF16),
        jax.ShapeDtypeStruct((M, RET_KW), BF16),
        jax.ShapeDtypeStruct((M, RET_VW), BF16),
    )
    return pl.pallas_call(
        _inproj_sample_kernel,
        grid=(1,),
        in_specs=[_const_spec(a.shape) for a in args],
        out_specs=tuple(_const_spec(o.shape) for o in out_shape),
        out_shape=out_shape,
        compiler_params=pltpu.CompilerParams(dimension_semantics=("arbitrary",),
                                             vmem_limit_bytes=VMEM_LIMIT),
        name="inproj_sample",
    )(*args)


def _fox_prompt_kernel(it_ref, jt_ref, jfetch_ref, mode_ref, qT_ref, ka_ref, vT_ref,
                       sqq_ref, o_ref, m_ref, acc_ref, *, n_strips):
    del jfetch_ref
    t = pl.program_id(0)
    i = it_ref[t]
    j = jt_ref[t]
    T = qT_ref.shape[2]
    SUB = T // n_strips
    EXACT = n_strips + 1

    def scores(hh):
        return _dot(ka_ref[hh], qT_ref[hh])

    def exact_head(hh):
        s = scores(hh)
        kk = lax.broadcasted_iota(jnp.int32, s.shape, 0) + j * T
        qq = lax.broadcasted_iota(jnp.int32, s.shape, 1) + i * T
        s = jnp.where(kk > qq, NEG, s)
        m_old = m_ref[hh]
        m_new = jnp.maximum(m_old, jnp.max(s, axis=0, keepdims=True))
        p = jnp.exp2(s - m_new).astype(BF16)
        alpha = jnp.exp2(m_old - m_new)
        acc_ref[hh] = alpha * acc_ref[hh] + _dot(vT_ref[hh], p)
        m_ref[hh] = m_new


    def diag_pass(hh):
        half = T // 2
        m = sqq_ref[pl.ds(hh, 1), :]
        m_ref[hh] = m
        kk = lax.broadcasted_iota(jnp.int32, (half, half), 0)
        qq = lax.broadcasted_iota(jnp.int32, (half, half), 1)
        tri = kk > qq
        s_lo = _dot(ka_ref[hh, :half, :], qT_ref[hh])
        s_lo = jnp.concatenate([jnp.where(tri, NEG, s_lo[:, :half]), s_lo[:, half:]], axis=1)
        acc_ref[hh] = _dot(vT_ref[hh, :, :half], jnp.exp2(s_lo - m).astype(BF16))
        s_hi = jnp.where(tri, NEG, _dot(ka_ref[hh, half:, :], qT_ref[hh, :, half:]))
        p_hi = jnp.exp2(s_hi - m[:, half:]).astype(BF16)
        acc_ref[hh, :, half:] += _dot(vT_ref[hh, :, half:], p_hi)

    def diag_head(hh, mode):
        @pl.when(mode != EXACT)
        def _():
            diag_pass(hh)

        @pl.when(mode == EXACT)
        def _():
            m_ref[hh] = sqq_ref[pl.ds(hh, 1), :]
            acc_ref[hh] = jnp.zeros((V_PAD, T), F32)
            exact_head(hh)

    def full_pass(hh):
        p = jnp.exp2(scores(hh) - m_ref[hh]).astype(BF16)
        acc_ref[hh] += _dot(vT_ref[hh], p)

    def off_head(hh, mode):
        for nn in range(1, n_strips + 1):
            k0 = (n_strips - nn) * SUB

            @pl.when(mode == nn)
            def _():
                p = jnp.exp2(_dot(ka_ref[hh, k0:, :], qT_ref[hh]) - m_ref[hh]).astype(BF16)
                acc_ref[hh] += _dot(vT_ref[hh, :, k0:], p)

        @pl.when(mode == EXACT)
        def _():
            exact_head(hh)

    def head_groups(is_common, common_pass, per_head):
        def group(gi, carry):
            h0 = gi * HEAD_GROUP
            all_common = is_common(mode_ref[t * FOX_HEADS + h0])
            for d in range(1, HEAD_GROUP):
                all_common = jnp.logical_and(all_common,
                                             is_common(mode_ref[t * FOX_HEADS + h0 + d]))

            @pl.when(all_common)
            def _():
                for d in range(HEAD_GROUP):
                    common_pass(h0 + d)

            @pl.when(jnp.logical_not(all_common))
            def _():
                def body(hh, c):
                    per_head(hh, mode_ref[t * FOX_HEADS + hh])
                    return c
                lax.fori_loop(h0, h0 + HEAD_GROUP, body, 0)
            return carry
        lax.fori_loop(0, FOX_HEADS // HEAD_GROUP, group, 0)

    @pl.when(j == i)
    def _():
        head_groups(lambda md: md != EXACT, diag_pass, diag_head)

    @pl.when(j < i)
    def _():
        head_groups(lambda md: md == n_strips, full_pass, off_head)

    @pl.when(j == 0)
    def _():
        for pr in range(FOX_HEADS // 2):
            halves = []
            for hh in (2 * pr, 2 * pr + 1):
                a = acc_ref[hh]
                halves.append(a[:FOX_HEAD_DIM] / a[FOX_HEAD_DIM:FOX_HEAD_DIM + 1])
            o_ref[pr] = jnp.concatenate(halves, axis=0).T.astype(BF16)


def _prune_tables(stats, it, jt, nb):
    per = stats.shape[0] // nb
    st = stats.reshape(nb, per, 8, LANES)[:, :, :, :FOX_HEADS]
    nq = jnp.sqrt(jnp.max(st[:, :, 0, :], axis=1)) * NORM_SLACK
    nk_strip = jnp.sqrt(st[:, :, 1, :]) * NORM_SLACK
    nk = jnp.max(nk_strip, axis=1)
    c_first = st[:, 0, 2, :]
    c_last = st[:, :, 3, :]
    bound = (nq[it][:, None, :] * (nk_strip[jt] + nk[it][:, None, :])
             - (c_last[jt] - c_first[it][:, None, :]))
    live = jnp.logical_or(jnp.asarray(jt == it)[:, None, None],
                          jnp.logical_not(bound < PRUNE_LOG2))
    strip_no = jnp.arange(per, dtype=jnp.int32)[None, :, None]
    n_keep = per - jnp.min(jnp.where(live, strip_no, per), axis=1)
    safe = nq[it] * (nk[jt] + nk[it]) < STALE_SAFE_LOG2
    mode = jnp.where(n_keep == 0, 0, jnp.where(safe, n_keep, per + 1))
    steps = jnp.arange(len(it), dtype=jnp.int32)
    last_live = lax.cummax(jnp.where(jnp.any(n_keep > 0, axis=1), steps, 0))
    return jnp.asarray(jt)[last_live], mode.astype(jnp.int32).reshape(-1), per


def _fox_prompt(qT, ka, vT, stats, sqq, T):
    S = ka.shape[1]
    nb = S // T
    it = np.array([i for i in range(nb) for _ in range(i + 1)], np.int32)
    jt = np.array([j for i in range(nb) for j in range(i, -1, -1)], np.int32)
    jfetch, mode, n_strips = _prune_tables(stats, it, jt, nb)
    grid_spec = pltpu.PrefetchScalarGridSpec(
        num_scalar_prefetch=4,
        grid=(len(it),),
        in_specs=[
            pl.BlockSpec((FOX_HEADS, HEAD_PAD, T), lambda t, it, jt, jf, md: (0, 0, it[t])),
            pl.BlockSpec((FOX_HEADS, T, HEAD_PAD), lambda t, it, jt, jf, md: (0, jf[t], 0)),
            pl.BlockSpec((FOX_HEADS, V_PAD, T), lambda t, it, jt, jf, md: (0, 0, jf[t])),
            pl.BlockSpec((FOX_HEADS, T), lambda t, it, jt, jf, md: (0, it[t])),
        ],
        out_specs=pl.BlockSpec((FOX_HEADS // 2, T, LANES),
                               lambda t, it, jt, jf, md: (0, it[t], 0)),
        scratch_shapes=[pltpu.VMEM((FOX_HEADS, 1, T), F32),
                        pltpu.VMEM((FOX_HEADS, V_PAD, T), F32)],
    )
    return pl.pallas_call(
        functools.partial(_fox_prompt_kernel, n_strips=n_strips),
        grid_spec=grid_spec,
        out_shape=jax.ShapeDtypeStruct((FOX_HEADS // 2, S, LANES), BF16),
        compiler_params=pltpu.CompilerParams(dimension_semantics=("arbitrary",),
                                             vmem_limit_bytes=VMEM_LIMIT),
        name="fox_prompt",
    )(jnp.asarray(it), jnp.asarray(jt), jfetch, mode, qT, ka, vT, sqq)


def _fox_sample_kernel(q_ref, kn_ref, vn_ref, ckT_ref, cvT_ref, lfT_ref, up_ref, ex_ref, o_ref):
    P = ckT_ref.shape[3]
    Tn = q_ref.shape[0]
    KP = lfT_ref.shape[2]
    HQ = FOX_HEADS * Tn
    nchunk = KP // LANES
    nt = (((1,), (1,)), ((), ()))

    def stack3(x):
        parts3 = [t.astype(F32) for t in _split3(x)] + [jnp.zeros_like(x)]
        return jnp.concatenate(parts3, axis=0).astype(BF16)

    x3 = stack3(lfT_ref[0])
    up = up_ref[...]
    parts = [_dot(x3[:, cidx * LANES:(cidx + 1) * LANES], up) for cidx in range(nchunk)]
    run = jnp.zeros((4 * FOX_HEADS, 1), F32)
    for cidx in range(nchunk):
        total = parts[cidx][:, LANES - 1:LANES]
        parts[cidx] = parts[cidx] + run
        run = run + total
    y = jnp.concatenate(parts, axis=1)
    cT = (y[:FOX_HEADS] + y[FOX_HEADS:2 * FOX_HEADS] + y[2 * FOX_HEADS:3 * FOX_HEADS]) * LOG2E
    ckx = _dot(ex_ref[...], stack3(cT))

    tail = ckx[:, P:P + LANES]
    rowq = lax.broadcasted_iota(jnp.int32, tail.shape, 0) % Tn
    lanek = lax.broadcasted_iota(jnp.int32, tail.shape, 1)
    cq = jnp.sum(jnp.where(lanek == rowq, tail, 0.0), axis=1, keepdims=True)

    q = q_ref[...]
    qt = jnp.concatenate([q] * FOX_HEADS, axis=0)
    rh = lax.broadcasted_iota(jnp.int32, qt.shape, 0) // Tn
    lh = lax.broadcasted_iota(jnp.int32, qt.shape, 1) // FOX_HEAD_DIM
    qbd = jnp.where(rh == lh, qt, jnp.zeros_like(qt))

    kT = ckT_ref[0].reshape(FOX_W, P).astype(BF16)
    vT = cvT_ref[0].reshape(FOX_W, P).astype(BF16)
    s_c = _dot(qbd, kT) + cq - ckx[:, :P]
    s_n = lax.dot_general(qbd, kn_ref[...].astype(BF16), nt, preferred_element_type=F32)
    s_n = s_n + cq - ckx[:, P:P + Tn]
    key = lax.broadcasted_iota(jnp.int32, s_n.shape, 1)
    qrow = lax.broadcasted_iota(jnp.int32, s_n.shape, 0) % Tn
    s_n = jnp.where(key > qrow, NEG, s_n)
    m = jnp.maximum(jnp.max(s_c, axis=1, keepdims=True), jnp.max(s_n, axis=1, keepdims=True))
    p_c = jnp.exp2(s_c - m)
    p_n = jnp.exp2(s_n - m)
    l = jnp.sum(p_c, axis=1, keepdims=True) + jnp.sum(p_n, axis=1, keepdims=True)
    z = lax.dot_general(p_c.astype(BF16), vT, nt, preferred_element_type=F32)
    z = (z + _dot(p_n.astype(BF16), vn_ref[...].astype(BF16))) / l
    zh = lax.broadcasted_iota(jnp.int32, (Tn, FOX_W), 1) // FOX_HEAD_DIM
    o = jnp.zeros((Tn, FOX_W), F32)
    for hh in range(FOX_HEADS):
        o = o + jnp.where(zh == hh, z[hh * Tn:(hh + 1) * Tn, :], 0.0)
    o_ref[...] = o.astype(BF16)


def _fox_sample(q, kn, vn, cache_kT, cache_vT, lfT, B, Tn):
    P = cache_kT.shape[3]
    KP = lfT.shape[2]
    HQ = FOX_HEADS * Tn
    up = jnp.asarray(np.triu(np.ones((LANES, LANES), np.float32)), BF16)
    ex = np.zeros((HQ, 4 * FOX_HEADS), np.float32)
    for part in range(3):
        for hh in range(FOX_HEADS):
            ex[hh * Tn:(hh + 1) * Tn, part * FOX_HEADS + hh] = 1.0
    ex = jnp.asarray(ex, BF16)
    rowb = lambda w: pl.BlockSpec((Tn, w), lambda b: (b, 0))
    return pl.pallas_call(
        _fox_sample_kernel,
        grid=(B,),
        in_specs=[rowb(FOX_W), rowb(FOX_W), rowb(FOX_W),
                  pl.BlockSpec((1, FOX_HEADS, FOX_HEAD_DIM, P), lambda b: (b, 0, 0, 0)),
                  pl.BlockSpec((1, FOX_HEADS, FOX_HEAD_DIM, P), lambda b: (b, 0, 0, 0)),
                  pl.BlockSpec((1, FOX_HEADS, KP), lambda b: (b, 0, 0)),
                  _const_spec(up.shape), _const_spec(ex.shape)],
        out_specs=rowb(FOX_W),
        out_shape=jax.ShapeDtypeStruct((B * Tn, FOX_W), BF16),
        compiler_params=pltpu.CompilerParams(dimension_semantics=("arbitrary",),
                                             vmem_limit_bytes=VMEM_LIMIT),
        name="fox_sample",
    )(q, kn, vn, cache_kT, cache_vT, lfT, up, ex)


def _retention_kernel(q_ref, k_ref, v_ref, s0_ref, dmat_ref, xi_ref, zeta_ref, gam_ref,
                      n_ref, sout_ref, st_ref):
    c = pl.program_id(1)
    G = st_ref.shape[0]
    C = dmat_ref.shape[1]
    n_units = q_ref.shape[0] // C
    per_stream = n_units // G

    @pl.when(c == 0)
    def _():
        st_ref[...] = s0_ref[...]

    for unit in range(n_units):
        g = unit // per_stream
        rows = slice(unit * C, (unit + 1) * C)
        for hh in range(RET_HEADS):
            q = q_ref[rows, hh * RET_KEY_DIM:(hh + 1) * RET_KEY_DIM]
            k = k_ref[rows, hh * RET_KEY_DIM:(hh + 1) * RET_KEY_DIM]
            v = v_ref[rows, hh * RET_VAL_DIM:(hh + 1) * RET_VAL_DIM]
            st = st_ref[g, hh]
            sc = lax.dot_general(q, k, (((1,), (1,)), ((), ())), preferred_element_type=F32)
            sc = sc * dmat_ref[hh]
            o = _dot(sc.astype(BF16), v) + _dot(q, st.astype(BF16)) * xi_ref[hh]
            kz = (k.astype(F32) * zeta_ref[hh]).astype(BF16)
            upd = lax.dot_general(kz, v, (((0,), (0,)), ((), ())), preferred_element_type=F32)
            st_ref[g, hh] = gam_ref[hh] * st + upd
            mu = jnp.mean(o, axis=-1, keepdims=True)
            d = o - mu
            var = jnp.mean(d * d, axis=-1, keepdims=True)
            n_ref[rows, hh * RET_VAL_DIM:(hh + 1) * RET_VAL_DIM] = (
                d * lax.rsqrt(var + EPS)).astype(BF16)

    @pl.when(c == pl.num_programs(1) - 1)
    def _():
        sout_ref[...] = st_ref[...]


def _retention(q, k, v, state0, B, L, C, G=1, per_step=1):
    assert L % (C * per_step) == 0
    nc = L // (C * per_step)
    assert B % G == 0 and (G == 1 or (nc == 1 and per_step == 1))
    f32 = np.float32
    lg = np.log(f32(1.0) - np.exp2(f32(-5.0) - np.arange(RET_HEADS, dtype=f32))).astype(f32)
    idx = np.arange(C, dtype=f32)
    diff = idx[:, None] - idx[None, :]
    dmat = np.where(diff[None] >= 0, np.exp(np.maximum(diff, 0)[None] * lg[:, None, None]), 0)
    xi = np.exp((idx[None, :] + f32(1.0)) * lg[:, None])
    zeta = np.exp((f32(C) - f32(1.0) - idx[None, :]) * lg[:, None])
    xi = np.broadcast_to(xi[:, :, None], (RET_HEADS, C, RET_VAL_DIM))
    zeta = np.broadcast_to(zeta[:, :, None], (RET_HEADS, C, RET_KEY_DIM))
    gam = np.broadcast_to(np.exp(f32(C) * lg)[:, None, None], (RET_HEADS, 1, RET_VAL_DIM))
    dmat, xi, zeta, gam = (jnp.asarray(a, F32) for a in (dmat, xi, zeta, gam))
    rowc = lambda w: pl.BlockSpec((G * per_step * C, w), lambda b, c: (b * nc + c, 0))
    st_spec = pl.BlockSpec((G, RET_HEADS, RET_KEY_DIM, RET_VAL_DIM), lambda b, c: (b, 0, 0, 0))
    return pl.pallas_call(
        _retention_kernel,
        grid=(B // G, nc),
        in_specs=[rowc(RET_KW), rowc(RET_KW), rowc(RET_VW), st_spec,
                  _const_spec(dmat.shape), _const_spec(xi.shape), _const_spec(zeta.shape),
                  _const_spec(gam.shape)],
        out_specs=(rowc(RET_VW), st_spec),
        out_shape=(jax.ShapeDtypeStruct((B * L, RET_VW), BF16),
                   jax.ShapeDtypeStruct((B, RET_HEADS, RET_KEY_DIM, RET_VAL_DIM), F32)),
        scratch_shapes=[pltpu.VMEM((G, RET_HEADS, RET_KEY_DIM, RET_VAL_DIM), F32)],
        compiler_params=pltpu.CompilerParams(dimension_semantics=("arbitrary", "arbitrary"),
                                             vmem_limit_bytes=VMEM_LIMIT),
        name="retention",
    )(q, k, v, state0, dmat, xi, zeta, gam)


def _mixer_ffn_kernel(x_ref, oa_ref, nb_ref, prev_ref, gmix_ref, wg_ref, gng_ref, wpa_ref, wpb_ref,
                      wo_ref, gffn_ref, wup_ref, cw_ref, cb_ref, wdn_ref, gfin_ref,
                      y_ref, conv_ref, carry_ref, ua_ref, ub_ref, acc_ref, h2_ref,
                      *, nseg, seglen):
    i = pl.program_id(0)
    NC = N_FFN_CHUNKS
    PADR = 8
    H0 = PADR - (CONV_WIDTH - 1)

    @pl.when(i == 0)
    def _():
        carry_ref[...] = prev_ref[...]

    x = x_ref[...]
    h = _rmsnorm(x, gmix_ref[...]).astype(BF16)
    zg = _dot_t(h, wg_ref[...])
    gb = zg[:, :RET_VW]
    gma = zg[:, RET_VW:RET_VW + D_MODEL]
    gmb = zg[:, RET_VW + D_MODEL:]
    oa = jnp.concatenate([oa_ref[p] for p in range(FOX_HEADS // 2)], axis=1)
    ya = _dot(oa, wpa_ref[...])
    nn = nb_ref[...].astype(F32) * gng_ref[...] * (gb * jax.nn.sigmoid(gb))
    yb = _dot(nn.astype(BF16), wpb_ref[...])
    y = jax.nn.sigmoid(gma) * ya + jax.nn.sigmoid(gmb) * yb
    x1 = x + _dot(y.astype(BF16), wo_ref[...])
    h2_ref[...] = _rmsnorm(x1, gffn_ref[...]).astype(BF16)
    acc_ref[...] = x1

    h2 = h2_ref[...]

    def cols(cidx):
        return slice(cidx * FFN_CHUNK, (cidx + 1) * FFN_CHUNK)

    def up_half(u_ref, cidx, slot):
        u = _dot(h2, wup_ref[:, cols(cidx)])
        for s in range(nseg):
            u_ref[slot, s, PADR:PADR + seglen, :] = u[s * seglen:(s + 1) * seglen, :]
            u_ref[slot, s, H0:PADR, :] = carry_ref[s, H0:PADR, cols(cidx)]
            carry_ref[s, H0:PADR, cols(cidx)] = (
                u[(s + 1) * seglen - (CONV_WIDTH - 1):(s + 1) * seglen, :])

    def conv_half(u_ref, cidx, slot):
        w = cw_ref[:, cols(cidx)]
        b = cb_ref[:, cols(cidx)]
        outs = []
        for s in range(nseg):
            acc = w[0:1] * u_ref[slot, s, H0:H0 + seglen, :]
            for jj in range(1, CONV_WIDTH):
                acc = acc + w[jj:jj + 1] * u_ref[slot, s, H0 + jj:H0 + jj + seglen, :]
            outs.append(b + acc)
        return outs[0] if nseg == 1 else jnp.concatenate(outs, axis=0)

    def stage_up(c):
        up_half(ua_ref, c, c % UP_AHEAD_SLOTS)
        up_half(ub_ref, NC + c, c % UP_AHEAD_SLOTS)

    for c in range(UP_AHEAD_SLOTS - 1):
        stage_up(c)
    group = []
    for c in range(NC):
        if c + UP_AHEAD_SLOTS - 1 < NC:
            stage_up(c + UP_AHEAD_SLOTS - 1)
        a = conv_half(ua_ref, c, c % UP_AHEAD_SLOTS)
        b = conv_half(ub_ref, NC + c, c % UP_AHEAD_SLOTS)
        group.append((_gelu_tanh(a) * b).astype(BF16))
        if len(group) == DOWN_GROUP or c + 1 == NC:
            r0 = (c + 1 - len(group)) * FFN_CHUNK
            wd = wdn_ref[r0:(c + 1) * FFN_CHUNK, :]
            acc_ref[...] += _dot(jnp.concatenate(group, axis=1), wd)
            group = []
    y_ref[...] = _rmsnorm(acc_ref[...], gfin_ref[...])

    @pl.when(i == pl.num_programs(0) - 1)
    def _():
        conv_ref[...] = carry_ref[:, H0:PADR, :]


def _mixer_ffn(x, oa, nb, prev, weights, tm, nseg, seglen):
    M = x.shape[0]
    FC = FFN_CHUNK
    row = lambda w: pl.BlockSpec((tm, w), lambda i: (i, 0))
    wspec = lambda a: pl.BlockSpec(a.shape, lambda i, n=a.ndim: (0,) * n,
                                   pipeline_mode=pl.Buffered(1))
    in_specs = [row(D_MODEL),
                pl.BlockSpec((FOX_HEADS // 2, tm, LANES), lambda i: (0, i, 0)),
                row(RET_VW), wspec(prev)] + [wspec(w) for w in weights]
    out_shape = (jax.ShapeDtypeStruct((M, D_MODEL), F32),
                 jax.ShapeDtypeStruct((nseg, CONV_WIDTH - 1, 2 * FFN_DIM), F32))
    out_specs = (row(D_MODEL), _const_spec(out_shape[1].shape))
    return pl.pallas_call(
        functools.partial(_mixer_ffn_kernel, nseg=nseg, seglen=seglen),
        grid=(M // tm,),
        in_specs=in_specs,
        out_specs=out_specs,
        out_shape=out_shape,
        scratch_shapes=[pltpu.VMEM((nseg, 8, 2 * FFN_DIM), F32),
                        pltpu.VMEM((UP_AHEAD_SLOTS, nseg, 8 + seglen, FC), F32),
                        pltpu.VMEM((UP_AHEAD_SLOTS, nseg, 8 + seglen, FC), F32),
                        pltpu.VMEM((tm, D_MODEL), F32),
                        pltpu.VMEM((tm, D_MODEL), BF16)],
        compiler_params=pltpu.CompilerParams(dimension_semantics=("arbitrary",),
                                             vmem_limit_bytes=VMEM_LIMIT),
        name="mixer_ffn",
    )(x, oa, nb, prev, *weights)


def _rotary_tables(start, n):
    half = RET_KEY_DIM // 2
    inv = 1.0 / (ROPE_BASE ** jnp.linspace(0.0, 1.0, half, dtype=F32))
    fine = min(n, ROT_FINE)
    assert n % fine == 0
    a_hi = (start + fine * jnp.arange(n // fine)).astype(F32)[:, None] * inv[None, :]
    a_lo = jnp.arange(fine).astype(F32)[:, None] * inv[None, :]
    ch, sh = jnp.cos(a_hi)[:, None, :], jnp.sin(a_hi)[:, None, :]
    cl, sl = jnp.cos(a_lo)[None, :, :], jnp.sin(a_lo)[None, :, :]
    cos = (ch * cl - sh * sl).reshape(n, half)
    sin = (sh * cl + ch * sl).reshape(n, half)
    return jnp.concatenate([cos, cos], axis=1), jnp.concatenate([-sin, sin], axis=1)


def _pad_heads(wt, pad):
    d = wt.shape[1]
    wt = wt.reshape(FOX_HEADS, FOX_HEAD_DIM, d)
    wt = jnp.pad(wt, ((0, 0), (0, pad - FOX_HEAD_DIM), (0, 0)))
    return wt.reshape(FOX_HEADS * pad, d)


def _prompt_consts(tm):
    tri = np.tril(np.ones((tm, tm), np.float32))
    eq = np.zeros((LANES, FOX_HEADS * HEAD_PAD), np.float32)
    ek = np.zeros((LANES, FOX_HEADS * HEAD_PAD), np.float32)
    oneq = np.zeros((1, FOX_HEADS * HEAD_PAD), np.float32)
    onek = np.zeros((1, FOX_HEADS * HEAD_PAD), np.float32)
    onev = np.zeros((1, FOX_HEADS * V_PAD), np.float32)
    for hh in range(FOX_HEADS):
        base = hh * HEAD_PAD + BIAS_COL
        for part in range(3):
            eq[part * FOX_HEADS + hh, base + part] = 1.0
            ek[part * FOX_HEADS + hh, base + 3 + part] = -1.0
            onek[0, base + part] = 1.0
            oneq[0, base + 3 + part] = 1.0
        onev[0, hh * V_PAD + FOX_HEAD_DIM] = 1.0
    return (jnp.asarray(tri, BF16), jnp.asarray(eq, BF16), jnp.asarray(ek, BF16),
            jnp.asarray(oneq), jnp.asarray(onek), jnp.asarray(onev))


def _tile(n, pref):
    t = min(n, pref)
    while n % t:
        t //= 2
    return t


def kernel(x_prompt, x_sample, cache_fox_k, cache_fox_v, cache_fox_logf, state_ret, state_ffn_conv,
           norm_mix_g, w_in, b_fox_f, gn_ret_g, w_pa, w_pb, w_o, norm_ffn_g, w_up, conv_w, conv_b,
           w_down, norm_final_g):
    depth = w_in.shape[0]
    Bp, S, _ = x_prompt.shape
    Bs, Ts, _ = x_sample.shape
    P = cache_fox_k.shape[2]
    assert depth == 1 and Bp == 1, "kernel handles the single-layer, single-prompt configuration"
    l = 0

    wt = jnp.swapaxes(w_in[l], 0, 1).astype(BF16)
    o0 = 3 * FOX_W
    o1 = o0 + FOX_HEADS
    o2 = o1 + 2 * RET_KW + RET_VW
    wq_aug = _pad_heads(wt[:FOX_W], HEAD_PAD)
    wk_aug = _pad_heads(wt[FOX_W:2 * FOX_W], HEAD_PAD)
    wf = jnp.pad(wt[o0:o1], ((0, LANES - FOX_HEADS), (0, 0)))
    wvf = jnp.concatenate([_pad_heads(wt[2 * FOX_W:o0], V_PAD), wf], axis=0)
    wqkv = wt[:o0]
    bf = jnp.pad(b_fox_f[l].astype(F32), (0, LANES - FOX_HEADS))[None, :]
    wb = wt[o1:o2]
    wg = wt[o2:]
    gmix = norm_mix_g[l].astype(F32)[None, :]
    mix_weights = (
        gmix, wg, gn_ret_g[l].astype(F32)[None, :], w_pa[l].astype(BF16), w_pb[l].astype(BF16),
        w_o[l].astype(BF16), norm_ffn_g[l].astype(F32)[None, :],
        w_up[l].astype(BF16),
        jnp.pad(conv_w[l].astype(F32), ((0, 8 - CONV_WIDTH), (0, 0))),
        conv_b[l].astype(F32)[None, :],
        w_down[l].astype(BF16),
        norm_final_g.astype(F32)[None, :],
    )
    hist_pad = ((0, 0), (8 - (CONV_WIDTH - 1), 0), (0, 0))

    tm_a = _tile(S, 512)
    cos_p, sin_p = _rotary_tables(0, S)
    (qT, ka, vT, kT_p, vT_p, logfT_p, qb, kb, vb, stats, sqq) = _inproj_prompt(
        x_prompt[0], gmix, wq_aug, wk_aug, wvf, bf, wb, cos_p, sin_p,
        _prompt_consts(tm_a), tm_a)
    oa_p = _fox_prompt(qT, ka, vT, stats, sqq, _tile(S, 1024))
    zero_state = jnp.zeros((1, RET_HEADS, RET_KEY_DIM, RET_VAL_DIM), F32)
    c_ret = _tile(S, 256)
    nb_p, ret_p = _retention(qb, kb, vb, zero_state, 1, S, c_ret,
                             per_step=math.gcd(S // c_ret, 4))
    tm_d = _tile(S, 256)
    zero_prev = jnp.zeros((1, 8, 2 * FFN_DIM), F32)
    y_p, conv_p = _mixer_ffn(x_prompt[0], oa_p, nb_p, zero_prev, mix_weights, tm_d, 1, tm_d)

    Ms = Bs * Ts
    cos_s, sin_s = _rotary_tables(P, Ts)
    cos_s = jnp.tile(cos_s, (Bs, 1))
    sin_s = jnp.tile(sin_s, (Bs, 1))
    (q_s, k_s, v_s, logf_s, qb_s, kb_s, vb_s) = _inproj_sample(
        x_sample.reshape(Ms, D_MODEL), gmix, wqkv, wf, bf, wb, cos_s, sin_s)
    KP = ((P + Ts + LANES - 1) // LANES) * LANES
    lf_all = jnp.concatenate([cache_fox_logf[l].astype(F32), logf_s.reshape(Bs, Ts, FOX_HEADS)], axis=1)
    lfT = jnp.pad(jnp.swapaxes(lf_all, 1, 2), ((0, 0), (0, 0), (0, KP - P - Ts)))
    oa_s = _fox_sample(q_s, k_s, v_s, jnp.transpose(cache_fox_k[l], (0, 2, 3, 1)),
                       jnp.transpose(cache_fox_v[l], (0, 2, 3, 1)), lfT, Bs, Ts)
    oa_s = jnp.moveaxis(oa_s.reshape(Ms, FOX_HEADS // 2, LANES), 1, 0)
    nb_s, ret_s = _retention(qb_s, kb_s, vb_s, state_ret[l].astype(F32), Bs, Ts, Ts,
                             G=math.gcd(Bs, 4))
    prev_s = jnp.pad(state_ffn_conv[l].astype(F32), hist_pad)
    y_s, conv_s = _mixer_ffn(x_sample.reshape(Ms, D_MODEL), oa_s, nb_s, prev_s, mix_weights,
                             Ms, Bs, Ts)

    hshape = (FOX_HEADS, FOX_HEAD_DIM)
    return (
        y_p[None],
        y_s.reshape(Bs, Ts, D_MODEL),
        jnp.transpose(kT_p, (2, 0, 1))[None, None],
        jnp.transpose(vT_p, (2, 0, 1))[None, None],
        jnp.transpose(logfT_p, (1, 0))[None, None],
        ret_p[None],
        conv_p[None],
        k_s.reshape((1, Bs, Ts) + hshape),
        v_s.reshape((1, Bs, Ts) + hshape),
        logf_s.reshape(1, Bs, Ts, FOX_HEADS),
        ret_s[None],
        conv_s[None],
    )
```

```python
import functools
import math

import numpy as np
import jax
import jax.numpy as jnp
from jax import lax
from jax.experimental import pallas as pl
from jax.experimental.pallas import tpu as pltpu

F32 = jnp.float32
BF16 = jnp.bfloat16

D_MODEL = 1024
FOX_HEADS = 8
FOX_HEAD_DIM = 64
RET_HEADS = 4
RET_KEY_DIM = 128
RET_VAL_DIM = 256
FFN_DIM = 2816
CONV_WIDTH = 3
EPS = 1e-6
ROPE_BASE = 10000.0

FOX_W = FOX_HEADS * FOX_HEAD_DIM
RET_KW = RET_HEADS * RET_KEY_DIM
RET_VW = RET_HEADS * RET_VAL_DIM

LOG2E = 1.4426950408889634
LANES = 128
HEAD_PAD = LANES
V_PAD = 80
BIAS_COL = FOX_HEAD_DIM
NEG = -1e30
STALE_SAFE_LOG2 = 64.0
PRUNE_LOG2 = -160.0
NORM_SLACK = 1.02
HEAD_GROUP = 2
ROT_FINE = 128
FFN_CHUNK = 256
N_FFN_CHUNKS = FFN_DIM // FFN_CHUNK
DOWN_GROUP = 4
UP_AHEAD_SLOTS = 3
VMEM_LIMIT = 56 * 1024 * 1024


def _rmsnorm(x, g):
    ms = jnp.mean(x * x, axis=-1, keepdims=True)
    return x * lax.rsqrt(ms + EPS) * g


def _split3(x):
    hi = x.astype(BF16)
    r1 = x - hi.astype(F32)
    mid = r1.astype(BF16)
    lo = (r1 - mid.astype(F32)).astype(BF16)
    return hi, mid, lo


def _log_sigmoid(x):
    return jnp.minimum(x, 0.0) - jnp.log1p(jnp.exp(-jnp.abs(x)))


def _dot(a, b):
    return jnp.dot(a, b, preferred_element_type=F32)


def _dot_t(a, bt):
    return lax.dot_general(a, bt, (((1,), (1,)), ((), ())), preferred_element_type=F32)


def _gelu_tanh(x):
    c0 = math.sqrt(2.0 / math.pi)
    hx = 0.5 * x
    return hx + hx * jnp.tanh(x * (c0 + (c0 * 0.044715) * (x * x)))


def _rotary(x, cos2, sin2):
    return x * cos2 + pltpu.roll(x, RET_KEY_DIM // 2, 1) * sin2


def _const_spec(shape):
    n = len(shape)
    return pl.BlockSpec(shape, lambda *_: (0,) * n)


def _inproj_prompt_kernel(x_ref, g_ref, wq_ref, wk_ref, wvf_ref, bf_ref, wb_ref,
                          cos_ref, sin_ref, tri_ref, eq_ref, ek_ref, oneq_ref, onek_ref, onev_ref,
                          qT_ref, ka_ref, vT_ref, kT32_ref, vT32_ref, logf_ref, qb_ref, kb_ref, vb_ref,
                          stats_ref, sqq_ref, carry_ref):
    tm = x_ref.shape[0]
    VW = FOX_HEADS * V_PAD

    @pl.when(pl.program_id(0) == 0)
    def _():
        carry_ref[...] = jnp.zeros_like(carry_ref)

    h = _rmsnorm(x_ref[...], g_ref[...]).astype(BF16)
    zvf = _dot_t(h, wvf_ref[...])

    logf = _log_sigmoid(zvf[:, VW:] + bf_ref[...])
    logf_ref[...] = logf.T[:FOX_HEADS, :]
    lane = lax.broadcasted_iota(jnp.int32, logf.shape, 1)
    logf = jnp.where(lane < FOX_HEADS, logf, 0.0)
    r = _dot(tri_ref[...], jnp.concatenate(_split3(logf), axis=1))
    c = r[:, :LANES] + r[:, LANES:2 * LANES] + r[:, 2 * LANES:] + carry_ref[...]
    carry_ref[...] = c[tm - 1:tm, :]
    c2 = c * LOG2E
    hi, mid, lo = (t.astype(F32) for t in _split3(c2))
    c3 = (hi + pltpu.roll(mid, FOX_HEADS, 1) + pltpu.roll(lo, 2 * FOX_HEADS, 1)).astype(BF16)

    q_aug = (_dot_t(h, wq_ref[...]) * (FOX_HEAD_DIM ** -0.5 * LOG2E)
             + _dot(c3, eq_ref[...]) + oneq_ref[...])
    qT32 = q_aug.T
    k_aug = _dot_t(h, wk_ref[...]) + _dot(c3, ek_ref[...]) + onek_ref[...]
    kT32 = k_aug.T
    vT32 = (zvf[:, :VW] + onev_ref[...]).T
    qT = qT32.astype(BF16)
    k_aug = k_aug.astype(BF16)
    vT = vT32.astype(BF16)

    lane1 = lax.broadcasted_iota(jnp.int32, (1, LANES), 1)
    nq2 = jnp.zeros((1, LANES), F32)
    nk2 = jnp.zeros((1, LANES), F32)
    for hh in range(FOX_HEADS):
        qT_ref[hh] = qT[hh * HEAD_PAD:(hh + 1) * HEAD_PAD, :]
        ka_ref[hh] = k_aug[:, hh * HEAD_PAD:(hh + 1) * HEAD_PAD]
        vT_ref[hh] = vT[hh * V_PAD:(hh + 1) * V_PAD, :]
        qh = qT32[hh * HEAD_PAD:hh * HEAD_PAD + FOX_HEAD_DIM, :]
        kh = kT32[hh * HEAD_PAD:hh * HEAD_PAD + FOX_HEAD_DIM, :]
        kT32_ref[hh] = kh
        vT32_ref[hh] = vT32[hh * V_PAD:hh * V_PAD + FOX_HEAD_DIM, :]
        sqq_ref[hh:hh + 1, :] = jnp.sum(qh * kh, axis=0, keepdims=True)
        q2 = jnp.max(jnp.sum(qh * qh, axis=0, keepdims=True), axis=1, keepdims=True)
        k2 = jnp.max(jnp.sum(kh * kh, axis=0, keepdims=True), axis=1, keepdims=True)
        nq2 = jnp.where(lane1 == hh, q2, nq2)
        nk2 = jnp.where(lane1 == hh, k2, nk2)
    stats_ref[0] = jnp.concatenate(
        [nq2, nk2, c2[0:1, :], c2[tm - 1:tm, :], jnp.zeros((4, LANES), F32)], axis=0)

    zb = _dot_t(h, wb_ref[...])
    cos2 = cos_ref[...]
    sin2 = sin_ref[...]
    for hh in range(RET_HEADS):
        sl = slice(hh * RET_KEY_DIM, (hh + 1) * RET_KEY_DIM)
        qb_ref[:, sl] = _rotary(zb[:, sl], cos2, sin2).astype(BF16)
        xk = zb[:, RET_KW + hh * RET_KEY_DIM:RET_KW + (hh + 1) * RET_KEY_DIM]
        kb_ref[:, sl] = (_rotary(xk, cos2, sin2) * (RET_KEY_DIM ** -0.5)).astype(BF16)
    vb_ref[...] = zb[:, 2 * RET_KW:].astype(BF16)


def _inproj_prompt(x, g, wq, wk, wvf, bf, wb, cos2, sin2, consts, tm):
    S = x.shape[0]
    tri, eq, ek, oneq, onek, onev = consts
    row = lambda w: pl.BlockSpec((tm, w), lambda i: (i, 0))
    headT = pl.BlockSpec((FOX_HEADS, FOX_HEAD_DIM, tm), lambda i: (0, 0, i))
    in_specs = [row(D_MODEL), _const_spec(g.shape), _const_spec(wq.shape), _const_spec(wk.shape),
                _const_spec(wvf.shape),
                _const_spec(bf.shape), _const_spec(wb.shape), row(LANES), row(LANES),
                _const_spec(tri.shape), _const_spec(eq.shape), _const_spec(ek.shape),
                _const_spec(oneq.shape), _const_spec(onek.shape), _const_spec(onev.shape)]
    out_shape = (
        jax.ShapeDtypeStruct((FOX_HEADS, HEAD_PAD, S), BF16),
        jax.ShapeDtypeStruct((FOX_HEADS, S, HEAD_PAD), BF16),
        jax.ShapeDtypeStruct((FOX_HEADS, V_PAD, S), BF16),
        jax.ShapeDtypeStruct((FOX_HEADS, FOX_HEAD_DIM, S), F32),
        jax.ShapeDtypeStruct((FOX_HEADS, FOX_HEAD_DIM, S), F32),
        jax.ShapeDtypeStruct((FOX_HEADS, S), F32),
        jax.ShapeDtypeStruct((S, RET_KW), BF16),
        jax.ShapeDtypeStruct((S, RET_KW), BF16),
        jax.ShapeDtypeStruct((S, RET_VW), BF16),
        jax.ShapeDtypeStruct((S // tm, 8, LANES), F32),
        jax.ShapeDtypeStruct((FOX_HEADS, S), F32),
    )
    out_specs = (
        pl.BlockSpec((FOX_HEADS, HEAD_PAD, tm), lambda i: (0, 0, i)),
        pl.BlockSpec((FOX_HEADS, tm, HEAD_PAD), lambda i: (0, i, 0)),
        pl.BlockSpec((FOX_HEADS, V_PAD, tm), lambda i: (0, 0, i)),
        headT, headT, pl.BlockSpec((FOX_HEADS, tm), lambda i: (0, i)),
        row(RET_KW), row(RET_KW), row(RET_VW),
        pl.BlockSpec((1, 8, LANES), lambda i: (i, 0, 0)),
        pl.BlockSpec((FOX_HEADS, tm), lambda i: (0, i)),
    )
    return pl.pallas_call(
        _inproj_prompt_kernel,
        grid=(S // tm,),
        in_specs=in_specs,
        out_specs=out_specs,
        out_shape=out_shape,
        scratch_shapes=[pltpu.VMEM((1, LANES), F32)],
        compiler_params=pltpu.CompilerParams(dimension_semantics=("arbitrary",),
                                             vmem_limit_bytes=VMEM_LIMIT),
        name="inproj_prompt",
    )(x, g, wq, wk, wvf, bf, wb, cos2, sin2, tri, eq, ek, oneq, onek, onev)


def _inproj_sample_kernel(x_ref, g_ref, wqkv_ref, wf_ref, bf_ref, wb_ref, cos_ref, sin_ref,
                          q_ref, k32_ref, v32_ref, logf_ref, qb_ref, kb_ref, vb_ref):
    h = _rmsnorm(x_ref[...], g_ref[...]).astype(BF16)
    logf = _log_sigmoid(_dot_t(h, wf_ref[...]) + bf_ref[...])
    logf_ref[...] = logf[:, :FOX_HEADS]
    z = _dot_t(h, wqkv_ref[...])
    q_ref[...] = (z[:, :FOX_W] * (FOX_HEAD_DIM ** -0.5 * LOG2E)).astype(BF16)
    k32_ref[...] = z[:, FOX_W:2 * FOX_W]
    v32_ref[...] = z[:, 2 * FOX_W:]
    zb = _dot_t(h, wb_ref[...])
    cos2 = cos_ref[...]
    sin2 = sin_ref[...]
    for hh in range(RET_HEADS):
        sl = slice(hh * RET_KEY_DIM, (hh + 1) * RET_KEY_DIM)
        qb_ref[:, sl] = _rotary(zb[:, sl], cos2, sin2).astype(BF16)
        xk = zb[:, RET_KW + hh * RET_KEY_DIM:RET_KW + (hh + 1) * RET_KEY_DIM]
        kb_ref[:, sl] = (_rotary(xk, cos2, sin2) * (RET_KEY_DIM ** -0.5)).astype(BF16)
    vb_ref[...] = zb[:, 2 * RET_KW:].astype(BF16)


def _inproj_sample(x, g, wqkv, wf, bf, wb, cos2, sin2):
    M = x.shape[0]
    args = (x, g, wqkv, wf, bf, wb, cos2, sin2)
    out_shape = (
        jax.ShapeDtypeStruct((M, FOX_W), BF16),
        jax.ShapeDtypeStruct((M, FOX_W), F32),
        jax.ShapeDtypeStruct((M, FOX_W), F32),
        jax.ShapeDtypeStruct((M, FOX_HEADS), F32),
        jax.ShapeDtypeStruct((M, RET_KW), BF16),
        jax.ShapeDtypeStruct((M, RET_KW), BF16),
        jax.ShapeDtypeStruct((M, RET_VW), BF16),
    )
    return pl.pallas_call(
        _inproj_sample_kernel,
        grid=(1,),
        in_specs=[_const_spec(a.shape) for a in args],
        out_specs=tuple(_const_spec(o.shape) for o in out_shape),
        out_shape=out_shape,
        compiler_params=pltpu.CompilerParams(dimension_semantics=("arbitrary",),
                                             vmem_limit_bytes=VMEM_LIMIT),
        name="inproj_sample",
    )(*args)


def _fox_prompt_kernel(it_ref, jt_ref, jfetch_ref, mode_ref, qT_ref, ka_ref, vT_ref,
                       sqq_ref, o_ref, m_ref, acc_ref, *, n_strips):
    del jfetch_ref
    t = pl.program_id(0)
    i = it_ref[t]
    j = jt_ref[t]
    T = qT_ref.shape[2]
    SUB = T // n_strips
    EXACT = n_strips + 1

    def scores(hh):
        return _dot(ka_ref[hh], qT_ref[hh])

    def exact_head(hh):
        s = scores(hh)
        kk = lax.broadcasted_iota(jnp.int32, s.shape, 0) + j * T
        qq = lax.broadcasted_iota(jnp.int32, s.shape, 1) + i * T
        s = jnp.where(kk > qq, NEG, s)
        m_old = m_ref[hh]
        m_new = jnp.maximum(m_old, jnp.max(s, axis=0, keepdims=True))
        p = jnp.exp2(s - m_new).astype(BF16)
        alpha = jnp.exp2(m_old - m_new)
        acc_ref[hh] = alpha * acc_ref[hh] + _dot(vT_ref[hh], p)
        m_ref[hh] = m_new


    def diag_pass(hh):
        half = T // 2
        m = sqq_ref[pl.ds(hh, 1), :]
        m_ref[hh] = m
        kk = lax.broadcasted_iota(jnp.int32, (half, half), 0)
        qq = lax.broadcasted_iota(jnp.int32, (half, half), 1)
        tri = kk > qq
        s_lo = _dot(ka_ref[hh, :half, :], qT_ref[hh])
        s_lo = jnp.concatenate([jnp.where(tri, NEG, s_lo[:, :half]), s_lo[:, half:]], axis=1)
        acc_ref[hh] = _dot(vT_ref[hh, :, :half], jnp.exp2(s_lo - m).astype(BF16))
        s_hi = jnp.where(tri, NEG, _dot(ka_ref[hh, half:, :], qT_ref[hh, :, half:]))
        p_hi = jnp.exp2(s_hi - m[:, half:]).astype(BF16)
        acc_ref[hh, :, half:] += _dot(vT_ref[hh, :, half:], p_hi)

    def diag_head(hh, mode):
        @pl.when(mode != EXACT)
        def _():
            diag_pass(hh)

        @pl.when(mode == EXACT)
        def _():
            m_ref[hh] = sqq_ref[pl.ds(hh, 1), :]
            acc_ref[hh] = jnp.zeros((V_PAD, T), F32)
            exact_head(hh)

    def full_pass(hh):
        p = jnp.exp2(scores(hh) - m_ref[hh]).astype(BF16)
        acc_ref[hh] += _dot(vT_ref[hh], p)

    def off_head(hh, mode):
        for nn in range(1, n_strips + 1):
            k0 = (n_strips - nn) * SUB

            @pl.when(mode == nn)
            def _():
                p = jnp.exp2(_dot(ka_ref[hh, k0:, :], qT_ref[hh]) - m_ref[hh]).astype(BF16)
                acc_ref[hh] += _dot(vT_ref[hh, :, k0:], p)

        @pl.when(mode == EXACT)
        def _():
            exact_head(hh)

    def head_groups(is_common, common_pass, per_head):
        def group(gi, carry):
            h0 = gi * HEAD_GROUP
            all_common = is_common(mode_ref[t * FOX_HEADS + h0])
            for d in range(1, HEAD_GROUP):
                all_common = jnp.logical_and(all_common,
                                             is_common(mode_ref[t * FOX_HEADS + h0 + d]))

            @pl.when(all_common)
            def _():
                for d in range(HEAD_GROUP):
                    common_pass(h0 + d)

            @pl.when(jnp.logical_not(all_common))
            def _():
                def body(hh, c):
                    per_head(hh, mode_ref[t * FOX_HEADS + hh])
                    return c
                lax.fori_loop(h0, h0 + HEAD_GROUP, body, 0)
            return carry
        lax.fori_loop(0, FOX_HEADS // HEAD_GROUP, group, 0)

    @pl.when(j == i)
    def _():
        head_groups(lambda md: md != EXACT, diag_pass, diag_head)

    @pl.when(j < i)
    def _():
        head_groups(lambda md: md == n_strips, full_pass, off_head)

    @pl.when(j == 0)
    def _():
        for pr in range(FOX_HEADS // 2):
            halves = []
            for hh in (2 * pr, 2 * pr + 1):
                a = acc_ref[hh]
                halves.append(a[:FOX_HEAD_DIM] / a[FOX_HEAD_DIM:FOX_HEAD_DIM + 1])
            o_ref[pr] = jnp.concatenate(halves, axis=0).T.astype(BF16)


def _prune_tables(stats, it, jt, nb):
    per = stats.shape[0] // nb
    st = stats.reshape(nb, per, 8, LANES)[:, :, :, :FOX_HEADS]
    nq = jnp.sqrt(jnp.max(st[:, :, 0, :], axis=1)) * NORM_SLACK
    nk_strip = jnp.sqrt(st[:, :, 1, :]) * NORM_SLACK
    nk = jnp.max(nk_strip, axis=1)
    c_first = st[:, 0, 2, :]
    c_last = st[:, :, 3, :]
    bound = (nq[it][:, None, :] * (nk_strip[jt] + nk[it][:, None, :])
             - (c_last[jt] - c_first[it][:, None, :]))
    live = jnp.logical_or(jnp.asarray(jt == it)[:, None, None],
                          jnp.logical_not(bound < PRUNE_LOG2))
    strip_no = jnp.arange(per, dtype=jnp.int32)[None, :, None]
    n_keep = per - jnp.min(jnp.where(live, strip_no, per), axis=1)
    safe = nq[it] * (nk[jt] + nk[it]) < STALE_SAFE_LOG2
    mode = jnp.where(n_keep == 0, 0, jnp.where(safe, n_keep, per + 1))
    steps = jnp.arange(len(it), dtype=jnp.int32)
    last_live = lax.cummax(jnp.where(jnp.any(n_keep > 0, axis=1), steps, 0))
    return jnp.asarray(jt)[last_live], mode.astype(jnp.int32).reshape(-1), per


def _fox_prompt(qT, ka, vT, stats, sqq, T):
    S = ka.shape[1]
    nb = S // T
    it = np.array([i for i in range(nb) for _ in range(i + 1)], np.int32)
    jt = np.array([j for i in range(nb) for j in range(i, -1, -1)], np.int32)
    jfetch, mode, n_strips = _prune_tables(stats, it, jt, nb)
    grid_spec = pltpu.PrefetchScalarGridSpec(
        num_scalar_prefetch=4,
        grid=(len(it),),
        in_specs=[
            pl.BlockSpec((FOX_HEADS, HEAD_PAD, T), lambda t, it, jt, jf, md: (0, 0, it[t])),
            pl.BlockSpec((FOX_HEADS, T, HEAD_PAD), lambda t, it, jt, jf, md: (0, jf[t], 0)),
            pl.BlockSpec((FOX_HEADS, V_PAD, T), lambda t, it, jt, jf, md: (0, 0, jf[t])),
            pl.BlockSpec((FOX_HEADS, T), lambda t, it, jt, jf, md: (0, it[t])),
        ],
        out_specs=pl.BlockSpec((FOX_HEADS // 2, T, LANES),
                               lambda t, it, jt, jf, md: (0, it[t], 0)),
        scratch_shapes=[pltpu.VMEM((FOX_HEADS, 1, T), F32),
                        pltpu.VMEM((FOX_HEADS, V_PAD, T), F32)],
    )
    return pl.pallas_call(
        functools.partial(_fox_prompt_kernel, n_strips=n_strips),
        grid_spec=grid_spec,
        out_shape=jax.ShapeDtypeStruct((FOX_HEADS // 2, S, LANES), BF16),
        compiler_params=pltpu.CompilerParams(dimension_semantics=("arbitrary",),
                                             vmem_limit_bytes=VMEM_LIMIT),
        name="fox_prompt",
    )(jnp.asarray(it), jnp.asarray(jt), jfetch, mode, qT, ka, vT, sqq)


def _fox_sample_kernel(q_ref, kn_ref, vn_ref, ckT_ref, cvT_ref, lfT_ref, up_ref, ex_ref, o_ref):
    P = ckT_ref.shape[3]
    Tn = q_ref.shape[0]
    KP = lfT_ref.shape[2]
    HQ = FOX_HEADS * Tn
    nchunk = KP // LANES
    nt = (((1,), (1,)), ((), ()))

    def stack3(x):
        parts3 = [t.astype(F32) for t in _split3(x)] + [jnp.zeros_like(x)]
        return jnp.concatenate(parts3, axis=0).astype(BF16)

    x3 = stack3(lfT_ref[0])
    up = up_ref[...]
    parts = [_dot(x3[:, cidx * LANES:(cidx + 1) * LANES], up) for cidx in range(nchunk)]
    run = jnp.zeros((4 * FOX_HEADS, 1), F32)
    for cidx in range(nchunk):
        total = parts[cidx][:, LANES - 1:LANES]
        parts[cidx] = parts[cidx] + run
        run = run + total
    y = jnp.concatenate(parts, axis=1)
    cT = (y[:FOX_HEADS] + y[FOX_HEADS:2 * FOX_HEADS] + y[2 * FOX_HEADS:3 * FOX_HEADS]) * LOG2E
    ckx = _dot(ex_ref[...], stack3(cT))

    tail = ckx[:, P:P + LANES]
    rowq = lax.broadcasted_iota(jnp.int32, tail.shape, 0) % Tn
    lanek = lax.broadcasted_iota(jnp.int32, tail.shape, 1)
    cq = jnp.sum(jnp.where(lanek == rowq, tail, 0.0), axis=1, keepdims=True)

    q = q_ref[...]
    qt = jnp.concatenate([q] * FOX_HEADS, axis=0)
    rh = lax.broadcasted_iota(jnp.int32, qt.shape, 0) // Tn
    lh = lax.broadcasted_iota(jnp.int32, qt.shape, 1) // FOX_HEAD_DIM
    qbd = jnp.where(rh == lh, qt, jnp.zeros_like(qt))

    kT = ckT_ref[0].reshape(FOX_W, P).astype(BF16)
    vT = cvT_ref[0].reshape(FOX_W, P).astype(BF16)
    s_c = _dot(qbd, kT) + cq - ckx[:, :P]
    s_n = lax.dot_general(qbd, kn_ref[...].astype(BF16), nt, preferred_element_type=F32)
    s_n = s_n + cq - ckx[:, P:P + Tn]
    key = lax.broadcasted_iota(jnp.int32, s_n.shape, 1)
    qrow = lax.broadcasted_iota(jnp.int32, s_n.shape, 0) % Tn
    s_n = jnp.where(key > qrow, NEG, s_n)
    m = jnp.maximum(jnp.max(s_c, axis=1, keepdims=True), jnp.max(s_n, axis=1, keepdims=True))
    p_c = jnp.exp2(s_c - m)
    p_n = jnp.exp2(s_n - m)
    l = jnp.sum(p_c, axis=1, keepdims=True) + jnp.sum(p_n, axis=1, keepdims=True)
    z = lax.dot_general(p_c.astype(BF16), vT, nt, preferred_element_type=F32)
    z = (z + _dot(p_n.astype(BF16), vn_ref[...].astype(BF16))) / l
    zh = lax.broadcasted_iota(jnp.int32, (Tn, FOX_W), 1) // FOX_HEAD_DIM
    o = jnp.zeros((Tn, FOX_W), F32)
    for hh in range(FOX_HEADS):
        o = o + jnp.where(zh == hh, z[hh * Tn:(hh + 1) * Tn, :], 0.0)
    o_ref[...] = o.astype(BF16)


def _fox_sample(q, kn, vn, cache_kT, cache_vT, lfT, B, Tn):
    P = cache_kT.shape[3]
    KP = lfT.shape[2]
    HQ = FOX_HEADS * Tn
    up = jnp.asarray(np.triu(np.ones((LANES, LANES), np.float32)), BF16)
    ex = np.zeros((HQ, 4 * FOX_HEADS), np.float32)
    for part in range(3):
        for hh in range(FOX_HEADS):
            ex[hh * Tn:(hh + 1) * Tn, part * FOX_HEADS + hh] = 1.0
    ex = jnp.asarray(ex, BF16)
    rowb = lambda w: pl.BlockSpec((Tn, w), lambda b: (b, 0))
    return pl.pallas_call(
        _fox_sample_kernel,
        grid=(B,),
        in_specs=[rowb(FOX_W), rowb(FOX_W), rowb(FOX_W),
                  pl.BlockSpec((1, FOX_HEADS, FOX_HEAD_DIM, P), lambda b: (b, 0, 0, 0)),
                  pl.BlockSpec((1, FOX_HEADS, FOX_HEAD_DIM, P), lambda b: (b, 0, 0, 0)),
                  pl.BlockSpec((1, FOX_HEADS, KP), lambda b: (b, 0, 0)),
                  _const_spec(up.shape), _const_spec(ex.shape)],
        out_specs=rowb(FOX_W),
        out_shape=jax.ShapeDtypeStruct((B * Tn, FOX_W), BF16),
        compiler_params=pltpu.CompilerParams(dimension_semantics=("arbitrary",),
                                             vmem_limit_bytes=VMEM_LIMIT),
        name="fox_sample",
    )(q, kn, vn, cache_kT, cache_vT, lfT, up, ex)


def _retention_kernel(q_ref, k_ref, v_ref, s0_ref, dmat_ref, xi_ref, zeta_ref, gam_ref,
                      n_ref, sout_ref, st_ref):
    c = pl.program_id(1)
    G = st_ref.shape[0]
    C = dmat_ref.shape[1]
    n_units = q_ref.shape[0] // C
    per_stream = n_units // G

    @pl.when(c == 0)
    def _():
        st_ref[...] = s0_ref[...]

    for unit in range(n_units):
        g = unit // per_stream
        rows = slice(unit * C, (unit + 1) * C)
        for hh in range(RET_HEADS):
            q = q_ref[rows, hh * RET_KEY_DIM:(hh + 1) * RET_KEY_DIM]
            k = k_ref[rows, hh * RET_KEY_DIM:(hh + 1) * RET_KEY_DIM]
            v = v_ref[rows, hh * RET_VAL_DIM:(hh + 1) * RET_VAL_DIM]
            st = st_ref[g, hh]
            sc = lax.dot_general(q, k, (((1,), (1,)), ((), ())), preferred_element_type=F32)
            sc = sc * dmat_ref[hh]
            o = _dot(sc.astype(BF16), v) + _dot(q, st.astype(BF16)) * xi_ref[hh]
            kz = (k.astype(F32) * zeta_ref[hh]).astype(BF16)
            upd = lax.dot_general(kz, v, (((0,), (0,)), ((), ())), preferred_element_type=F32)
            st_ref[g, hh] = gam_ref[hh] * st + upd
            mu = jnp.mean(o, axis=-1, keepdims=True)
            d = o - mu
            var = jnp.mean(d * d, axis=-1, keepdims=True)
            n_ref[rows, hh * RET_VAL_DIM:(hh + 1) * RET_VAL_DIM] = (
                d * lax.rsqrt(var + EPS)).astype(BF16)

    @pl.when(c == pl.num_programs(1) - 1)
    def _():
        sout_ref[...] = st_ref[...]


def _retention(q, k, v, state0, B, L, C, G=1, per_step=1):
    assert L % (C * per_step) == 0
    nc = L // (C * per_step)
    assert B % G == 0 and (G == 1 or (nc == 1 and per_step == 1))
    f32 = np.float32
    lg = np.log(f32(1.0) - np.exp2(f32(-5.0) - np.arange(RET_HEADS, dtype=f32))).astype(f32)
    idx = np.arange(C, dtype=f32)
    diff = idx[:, None] - idx[None, :]
    dmat = np.where(diff[None] >= 0, np.exp(np.maximum(diff, 0)[None] * lg[:, None, None]), 0)
    xi = np.exp((idx[None, :] + f32(1.0)) * lg[:, None])
    zeta = np.exp((f32(C) - f32(1.0) - idx[None, :]) * lg[:, None])
    xi = np.broadcast_to(xi[:, :, None], (RET_HEADS, C, RET_VAL_DIM))
    zeta = np.broadcast_to(zeta[:, :, None], (RET_HEADS, C, RET_KEY_DIM))
    gam = np.broadcast_to(np.exp(f32(C) * lg)[:, None, None], (RET_HEADS, 1, RET_VAL_DIM))
    dmat, xi, zeta, gam = (jnp.asarray(a, F32) for a in (dmat, xi, zeta, gam))
    rowc = lambda w: pl.BlockSpec((G * per_step * C, w), lambda b, c: (b * nc + c, 0))
    st_spec = pl.BlockSpec((G, RET_HEADS, RET_KEY_DIM, RET_VAL_DIM), lambda b, c: (b, 0, 0, 0))
    return pl.pallas_call(
        _retention_kernel,
        grid=(B // G, nc),
        in_specs=[rowc(RET_KW), rowc(RET_KW), rowc(RET_VW), st_spec,
                  _const_spec(dmat.shape), _const_spec(xi.shape), _const_spec(zeta.shape),
                  _const_spec(gam.shape)],
        out_specs=(rowc(RET_VW), st_spec),
        out_shape=(jax.ShapeDtypeStruct((B * L, RET_VW), BF16),
                   jax.ShapeDtypeStruct((B, RET_HEADS, RET_KEY_DIM, RET_VAL_DIM), F32)),
        scratch_shapes=[pltpu.VMEM((G, RET_HEADS, RET_KEY_DIM, RET_VAL_DIM), F32)],
        compiler_params=pltpu.CompilerParams(dimension_semantics=("arbitrary", "arbitrary"),
                                             vmem_limit_bytes=VMEM_LIMIT),
        name="retention",
    )(q, k, v, state0, dmat, xi, zeta, gam)


def _mixer_ffn_kernel(x_ref, oa_ref, nb_ref, prev_ref, gmix_ref, wg_ref, gng_ref, wpa_ref, wpb_ref,
                      wo_ref, gffn_ref, wup_ref, cw_ref, cb_ref, wdn_ref, gfin_ref,
                      y_ref, conv_ref, carry_ref, ua_ref, ub_ref, acc_ref, h2_ref,
                      *, nseg, seglen):
    i = pl.program_id(0)
    NC = N_FFN_CHUNKS
    PADR = 8
    H0 = PADR - (CONV_WIDTH - 1)

    @pl.when(i == 0)
    def _():
        carry_ref[...] = prev_ref[...]

    x = x_ref[...]
    h = _rmsnorm(x, gmix_ref[...]).astype(BF16)
    gb = _dot_t(h, wg_ref[:RET_VW, :])
    nn = nb_ref[...].astype(F32) * gng_ref[...] * (gb * jax.nn.sigmoid(gb))
    yb = _dot(nn.astype(BF16), wpb_ref[...])
    oa = jnp.concatenate([oa_ref[p] for p in range(FOX_HEADS // 2)], axis=1)
    gma = _dot_t(h, wg_ref[RET_VW:RET_VW + D_MODEL, :])
    y = jax.nn.sigmoid(gma) * _dot(oa, wpa_ref[...])
    gmb = _dot_t(h, wg_ref[RET_VW + D_MODEL:, :])
    y = y + jax.nn.sigmoid(gmb) * yb
    x1 = x + _dot(y.astype(BF16), wo_ref[...])
    h2_ref[...] = _rmsnorm(x1, gffn_ref[...]).astype(BF16)
    acc_ref[...] = x1

    h2 = h2_ref[...]

    def cols(cidx):
        return slice(cidx * FFN_CHUNK, (cidx + 1) * FFN_CHUNK)

    def up_half(u_ref, cidx, slot):
        u = _dot(h2, wup_ref[:, cols(cidx)])
        for s in range(nseg):
            u_ref[slot, s, PADR:PADR + seglen, :] = u[s * seglen:(s + 1) * seglen, :]
            u_ref[slot, s, H0:PADR, :] = carry_ref[s, H0:PADR, cols(cidx)]
            carry_ref[s, H0:PADR, cols(cidx)] = (
                u[(s + 1) * seglen - (CONV_WIDTH - 1):(s + 1) * seglen, :])

    def conv_half(u_ref, cidx, slot):
        w = cw_ref[:, cols(cidx)]
        b = cb_ref[:, cols(cidx)]
        outs = []
        for s in range(nseg):
            acc = w[0:1] * u_ref[slot, s, H0:H0 + seglen, :]
            for jj in range(1, CONV_WIDTH):
                acc = acc + w[jj:jj + 1] * u_ref[slot, s, H0 + jj:H0 + jj + seglen, :]
            outs.append(b + acc)
        return outs[0] if nseg == 1 else jnp.concatenate(outs, axis=0)

    def stage_up(c):
        up_half(ua_ref, c, c % UP_AHEAD_SLOTS)
        up_half(ub_ref, NC + c, c % UP_AHEAD_SLOTS)

    for c in range(UP_AHEAD_SLOTS - 1):
        stage_up(c)
    group = []
    for c in range(NC):
        if c + UP_AHEAD_SLOTS - 1 < NC:
            stage_up(c + UP_AHEAD_SLOTS - 1)
        a = conv_half(ua_ref, c, c % UP_AHEAD_SLOTS)
        b = conv_half(ub_ref, NC + c, c % UP_AHEAD_SLOTS)
        group.append((_gelu_tanh(a) * b).astype(BF16))
        if len(group) == DOWN_GROUP or c + 1 == NC:
            r0 = (c + 1 - len(group)) * FFN_CHUNK
            wd = wdn_ref[r0:(c + 1) * FFN_CHUNK, :]
            acc_ref[...] += _dot(jnp.concatenate(group, axis=1), wd)
            group = []
    y_ref[...] = _rmsnorm(acc_ref[...], gfin_ref[...])

    @pl.when(i == pl.num_programs(0) - 1)
    def _():
        conv_ref[...] = carry_ref[:, H0:PADR, :]


def _mixer_ffn(x, oa, nb, prev, weights, tm, nseg, seglen):
    M = x.shape[0]
    FC = FFN_CHUNK
    row = lambda w: pl.BlockSpec((tm, w), lambda i: (i, 0))
    wspec = lambda a: pl.BlockSpec(a.shape, lambda i, n=a.ndim: (0,) * n,
                                   pipeline_mode=pl.Buffered(1))
    in_specs = [row(D_MODEL),
                pl.BlockSpec((FOX_HEADS // 2, tm, LANES), lambda i: (0, i, 0)),
                row(RET_VW), wspec(prev)] + [wspec(w) for w in weights]
    out_shape = (jax.ShapeDtypeStruct((M, D_MODEL), F32),
                 jax.ShapeDtypeStruct((nseg, CONV_WIDTH - 1, 2 * FFN_DIM), F32))
    out_specs = (row(D_MODEL), _const_spec(out_shape[1].shape))
    return pl.pallas_call(
        functools.partial(_mixer_ffn_kernel, nseg=nseg, seglen=seglen),
        grid=(M // tm,),
        in_specs=in_specs,
        out_specs=out_specs,
        out_shape=out_shape,
        scratch_shapes=[pltpu.VMEM((nseg, 8, 2 * FFN_DIM), F32),
                        pltpu.VMEM((UP_AHEAD_SLOTS, nseg, 8 + seglen, FC), F32),
                        pltpu.VMEM((UP_AHEAD_SLOTS, nseg, 8 + seglen, FC), F32),
                        pltpu.VMEM((tm, D_MODEL), F32),
                        pltpu.VMEM((tm, D_MODEL), BF16)],
        compiler_params=pltpu.CompilerParams(dimension_semantics=("arbitrary",),
                                             vmem_limit_bytes=VMEM_LIMIT),
        name="mixer_ffn",
    )(x, oa, nb, prev, *weights)


def _rotary_tables(start, n):
    half = RET_KEY_DIM // 2
    inv = 1.0 / (ROPE_BASE ** jnp.linspace(0.0, 1.0, half, dtype=F32))
    fine = min(n, ROT_FINE)
    assert n % fine == 0
    a_hi = (start + fine * jnp.arange(n // fine)).astype(F32)[:, None] * inv[None, :]
    a_lo = jnp.arange(fine).astype(F32)[:, None] * inv[None, :]
    ch, sh = jnp.cos(a_hi)[:, None, :], jnp.sin(a_hi)[:, None, :]
    cl, sl = jnp.cos(a_lo)[None, :, :], jnp.sin(a_lo)[None, :, :]
    cos = (ch * cl - sh * sl).reshape(n, half)
    sin = (sh * cl + ch * sl).reshape(n, half)
    return jnp.concatenate([cos, cos], axis=1), jnp.concatenate([-sin, sin], axis=1)


def _pad_heads(wt, pad):
    d = wt.shape[1]
    wt = wt.reshape(FOX_HEADS, FOX_HEAD_DIM, d)
    wt = jnp.pad(wt, ((0, 0), (0, pad - FOX_HEAD_DIM), (0, 0)))
    return wt.reshape(FOX_HEADS * pad, d)


def _prompt_consts(tm):
    tri = np.tril(np.ones((tm, tm), np.float32))
    eq = np.zeros((LANES, FOX_HEADS * HEAD_PAD), np.float32)
    ek = np.zeros((LANES, FOX_HEADS * HEAD_PAD), np.float32)
    oneq = np.zeros((1, FOX_HEADS * HEAD_PAD), np.float32)
    onek = np.zeros((1, FOX_HEADS * HEAD_PAD), np.float32)
    onev = np.zeros((1, FOX_HEADS * V_PAD), np.float32)
    for hh in range(FOX_HEADS):
        base = hh * HEAD_PAD + BIAS_COL
        for part in range(3):
            eq[part * FOX_HEADS + hh, base + part] = 1.0
            ek[part * FOX_HEADS + hh, base + 3 + part] = -1.0
            onek[0, base + part] = 1.0
            oneq[0, base + 3 + part] = 1.0
        onev[0, hh * V_PAD + FOX_HEAD_DIM] = 1.0
    return (jnp.asarray(tri, BF16), jnp.asarray(eq, BF16), jnp.asarray(ek, BF16),
            jnp.asarray(oneq), jnp.asarray(onek), jnp.asarray(onev))


def _tile(n, pref):
    t = min(n, pref)
    while n % t:
        t //= 2
    return t


def kernel(x_prompt, x_sample, cache_fox_k, cache_fox_v, cache_fox_logf, state_ret, state_ffn_conv,
           norm_mix_g, w_in, b_fox_f, gn_ret_g, w_pa, w_pb, w_o, norm_ffn_g, w_up, conv_w, conv_b,
           w_down, norm_final_g):
    depth = w_in.shape[0]
    Bp, S, _ = x_prompt.shape
    Bs, Ts, _ = x_sample.shape
    P = cache_fox_k.shape[2]
    assert depth == 1 and Bp == 1, "kernel handles the single-layer, single-prompt configuration"
    l = 0

    wt = jnp.swapaxes(w_in[l], 0, 1).astype(BF16)
    o0 = 3 * FOX_W
    o1 = o0 + FOX_HEADS
    o2 = o1 + 2 * RET_KW + RET_VW
    wq_aug = _pad_heads(wt[:FOX_W], HEAD_PAD)
    wk_aug = _pad_heads(wt[FOX_W:2 * FOX_W], HEAD_PAD)
    wf = jnp.pad(wt[o0:o1], ((0, LANES - FOX_HEADS), (0, 0)))
    wvf = jnp.concatenate([_pad_heads(wt[2 * FOX_W:o0], V_PAD), wf], axis=0)
    wqkv = wt[:o0]
    bf = jnp.pad(b_fox_f[l].astype(F32), (0, LANES - FOX_HEADS))[None, :]
    wb = wt[o1:o2]
    wg = wt[o2:]
    gmix = norm_mix_g[l].astype(F32)[None, :]
    mix_weights = (
        gmix, wg, gn_ret_g[l].astype(F32)[None, :], w_pa[l].astype(BF16), w_pb[l].astype(BF16),
        w_o[l].astype(BF16), norm_ffn_g[l].astype(F32)[None, :],
        w_up[l].astype(BF16),
        jnp.pad(conv_w[l].astype(F32), ((0, 8 - CONV_WIDTH), (0, 0))),
        conv_b[l].astype(F32)[None, :],
        w_down[l].astype(BF16),
        norm_final_g.astype(F32)[None, :],
    )
    hist_pad = ((0, 0), (8 - (CONV_WIDTH - 1), 0), (0, 0))

    tm_a = _tile(S, 512)
    cos_p, sin_p = _rotary_tables(0, S)
    (qT, ka, vT, kT_p, vT_p, logfT_p, qb, kb, vb, stats, sqq) = _inproj_prompt(
        x_prompt[0], gmix, wq_aug, wk_aug, wvf, bf, wb, cos_p, sin_p,
        _prompt_consts(tm_a), tm_a)
    oa_p = _fox_prompt(qT, ka, vT, stats, sqq, _tile(S, 1024))
    zero_state = jnp.zeros((1, RET_HEADS, RET_KEY_DIM, RET_VAL_DIM), F32)
    c_ret = _tile(S, 256)
    nb_p, ret_p = _retention(qb, kb, vb, zero_state, 1, S, c_ret,
                             per_step=math.gcd(S // c_ret, 8))
    tm_d = _tile(S, 256)
    zero_prev = jnp.zeros((1, 8, 2 * FFN_DIM), F32)
    y_p, conv_p = _mixer_ffn(x_prompt[0], oa_p, nb_p, zero_prev, mix_weights, tm_d, 1, tm_d)

    Ms = Bs * Ts
    cos_s, sin_s = _rotary_tables(P, Ts)
    cos_s = jnp.tile(cos_s, (Bs, 1))
    sin_s = jnp.tile(sin_s, (Bs, 1))
    (q_s, k_s, v_s, logf_s, qb_s, kb_s, vb_s) = _inproj_sample(
        x_sample.reshape(Ms, D_MODEL), gmix, wqkv, wf, bf, wb, cos_s, sin_s)
    KP = ((P + Ts + LANES - 1) // LANES) * LANES
    lf_all = jnp.concatenate([cache_fox_logf[l].astype(F32), logf_s.reshape(Bs, Ts, FOX_HEADS)], axis=1)
    lfT = jnp.pad(jnp.swapaxes(lf_all, 1, 2), ((0, 0), (0, 0), (0, KP - P - Ts)))
    oa_s = _fox_sample(q_s, k_s, v_s, jnp.transpose(cache_fox_k[l], (0, 2, 3, 1)),
                       jnp.transpose(cache_fox_v[l], (0, 2, 3, 1)), lfT, Bs, Ts)
    oa_s = jnp.moveaxis(oa_s.reshape(Ms, FOX_HEADS // 2, LANES), 1, 0)
    nb_s, ret_s = _retention(qb_s, kb_s, vb_s, state_ret[l].astype(F32), Bs, Ts, Ts,
                             G=math.gcd(Bs, 4))
    prev_s = jnp.pad(state_ffn_conv[l].astype(F32), hist_pad)
    y_s, conv_s = _mixer_ffn(x_sample.reshape(Ms, D_MODEL), oa_s, nb_s, prev_s, mix_weights,
                             Ms, Bs, Ts)

    hshape = (FOX_HEADS, FOX_HEAD_DIM)
    return (
        y_p[None],
        y_s.reshape(Bs, Ts, D_MODEL),
        jnp.transpose(kT_p, (2, 0, 1))[None, None],
        jnp.transpose(vT_p, (2, 0, 1))[None, None],
        jnp.transpose(logfT_p, (1, 0))[None, None],
        ret_p[None],
        conv_p[None],
        k_s.reshape((1, Bs, Ts) + hshape),
        v_s.reshape((1, Bs, Ts) + hshape),
        logf_s.reshape(1, Bs, Ts, FOX_HEADS),
        ret_s[None],
        conv_s[None],
    )
```
